```python
import jax, jax.numpy as jnp
from jax import lax
import numpy as np

D_MODEL = 1024
BATCH = 16
SEQ = 2048
DEPTH = 4

N_MIXERS = 2
N_MLA = (DEPTH + 1) // 2
N_HGRN = DEPTH // 2

MLA_HEADS = 16
QK_NOPE = 64
QK_ROPE = 32
V_HEAD = 64
Q_LORA = 768
KV_LORA = 256
ROPE_THETA = 10000.0
Q_BLOCK = 128

HGRN_EXPAND = 128
HGRN_HEADS = D_MODEL // HGRN_EXPAND
HGRN_V = D_MODEL // HGRN_HEADS
HGRN_CHUNK = 64

D_FF = -(-8 * D_MODEL // (3 * 256)) * 256

ALPHA = (2.0 * DEPTH) ** 0.25
BETA = (8.0 * DEPTH) ** -0.25
LN_EPS = 1e-5
RMS_EPS = 1e-6

kernel_name = 'hybrid_mla_hgrn2_deepnorm_adaln'


def layer_norm(x, g, b):
    xf = x.astype(jnp.float32)
    mu = jnp.mean(xf, -1, keepdims=True)
    var = jnp.mean(jnp.square(xf - mu), -1, keepdims=True)
    return ((xf - mu) * lax.rsqrt(var + LN_EPS) * g + b).astype(x.dtype)


def rms_norm(x, g):
    xf = x.astype(jnp.float32)
    ms = jnp.mean(jnp.square(xf), -1, keepdims=True)
    return (xf * lax.rsqrt(ms + RMS_EPS) * g).astype(x.dtype)


def rope_cos_sin(positions):
    inv_freq = ROPE_THETA ** (-jnp.arange(0, QK_ROPE, 2, dtype=jnp.float32) / QK_ROPE)
    ang = positions.astype(jnp.float32)[..., None] * inv_freq
    return jnp.cos(ang), jnp.sin(ang)


def apply_rope(x, cos, sin):
    x1, x2 = jnp.split(x.astype(jnp.float32), 2, axis=-1)
    return jnp.concatenate([x1 * cos - x2 * sin, x1 * sin + x2 * cos], -1).astype(x.dtype)


def causal_mla_attention(q_nope, q_rope, k_nope, k_rope, v):
    S = q_nope.shape[1]
    scale = (QK_NOPE + QK_ROPE) ** -0.5
    neg = jnp.finfo(jnp.float32).min
    outs = []
    for blk in range(S // Q_BLOCK):
        q0 = blk * Q_BLOCK
        kend = q0 + Q_BLOCK
        s = (jnp.einsum('bqhd,bkhd->bhqk', q_nope[:, q0:kend], k_nope[:, :kend])
             + jnp.einsum('bqhr,bkr->bhqk', q_rope[:, q0:kend], k_rope[:, :kend]))
        s = s.astype(jnp.float32) * scale
        mask = (q0 + jnp.arange(Q_BLOCK))[:, None] >= jnp.arange(kend)[None, :]
        p = jax.nn.softmax(jnp.where(mask, s, neg), axis=-1).astype(v.dtype)
        outs.append(jnp.einsum('bhqk,bkhd->bqhd', p, v[:, :kend]))
    return jnp.concatenate(outs, axis=1)


def mla(h, cos, sin, w_in, q_norm_g, w_qb, kv_norm_g, w_kvb, w_o):
    B, S, _ = h.shape
    proj = h @ w_in
    q_lat, kv_lat, k_rope = jnp.split(proj, [Q_LORA, Q_LORA + KV_LORA], axis=-1)
    q = (rms_norm(q_lat, q_norm_g) @ w_qb).reshape(B, S, MLA_HEADS, QK_NOPE + QK_ROPE)
    kv = (rms_norm(kv_lat, kv_norm_g) @ w_kvb).reshape(B, S, MLA_HEADS, QK_NOPE + V_HEAD)
    q_nope, q_rope = jnp.split(q, [QK_NOPE], axis=-1)
    k_nope, v = jnp.split(kv, [QK_NOPE], axis=-1)
    q_rope = apply_rope(q_rope, cos[:, :, None, :], sin[:, :, None, :])
    k_rope = apply_rope(k_rope, cos, sin)
    o = causal_mla_attention(q_nope, q_rope, k_nope, k_rope, v)
    return o.reshape(B, S, MLA_HEADS * V_HEAD) @ w_o


def chunk_gated_recurrence(q, k, v, log_f):
    B, S, H, K = q.shape
    V = v.shape[-1]
    C = HGRN_CHUNK
    N = S // C

    def to_chunks(t):
        return t.reshape(B, N, C, H, t.shape[-1]).transpose(1, 0, 3, 2, 4)

    causal = jnp.tril(jnp.ones((C, C), dtype=bool))[:, :, None]

    def step(state, inp):
        q_c, k_c, v_c, g_c = inp
        b = jnp.cumsum(g_c, axis=-2)
        diff = b[..., :, None, :] - b[..., None, :, :]
        decay = jnp.where(causal, jnp.exp(jnp.where(causal, diff, 0.0)), 0.0)
        attn = jnp.einsum('bhtk,bhsk,bhtsk->bhts', q_c, k_c, decay)
        o = (jnp.einsum('bhts,bhsv->bhtv', attn, v_c)
             + jnp.einsum('bhtk,bhkv->bhtv', q_c * jnp.exp(b), state))
        b_last = b[..., -1:, :]
        state = (jnp.exp(b_last[..., 0, :])[..., None] * state
                 + jnp.einsum('bhsk,bhsv->bhkv', k_c * jnp.exp(b_last - b), v_c))
        return state, o

    state0 = jnp.zeros((B, H, K, V), jnp.float32)
    _, o = lax.scan(step, state0, (to_chunks(q), to_chunks(k), to_chunks(v), to_chunks(log_f)))
    return o.transpose(1, 0, 3, 2, 4).reshape(B, S, H, V)


def hgrn2(h, lb, w_in, g_norm_g, w_o):
    B, S, _ = h.shape
    HK = HGRN_HEADS * HGRN_EXPAND
    HV = HGRN_HEADS * HGRN_V
    q, fx, i, g = jnp.split(h @ w_in, [HK, 2 * HK, 2 * HK + HV], axis=-1)
    q = jax.nn.silu(q.astype(jnp.float32)).reshape(B, S, HGRN_HEADS, HGRN_EXPAND)
    fx = fx.astype(jnp.float32).reshape(B, S, HGRN_HEADS, HGRN_EXPAND)
    lb = lb.astype(jnp.float32).reshape(HGRN_HEADS, HGRN_EXPAND)
    sig = jax.nn.sigmoid(fx)
    f = lb + (1.0 - lb) * sig
    log_f = jnp.log(f)
    k = 1.0 - f
    v = i.astype(jnp.float32).reshape(B, S, HGRN_HEADS, HGRN_V)
    o = chunk_gated_recurrence(q, k, v, log_f)
    o = rms_norm(o, g_norm_g).reshape(B, S, HV).astype(h.dtype)
    return (o * jax.nn.silu(g)) @ w_o


def swiglu(h, w_in, w_out):
    gate, up = jnp.split(h @ w_in, 2, axis=-1)
    return (jax.nn.silu(gate) * up) @ w_out


def ada_mod(c, w, b):
    mod = (jax.nn.silu(c) @ w + b)[:, None, :]
    shift, scale, gate = jnp.split(mod, 3, axis=-1)
    return shift, scale, gate


def _w(k, shape, fan_in, scale=1.0):
    return jax.random.normal(k, shape, jnp.float32) * (scale * fan_in ** -0.5)


def _fwd_setup_inputs(seed: int = 0) -> dict:
    key = jax.random.key(seed)
    ks = jax.random.split(key, 24)
    D = D_MODEL
    x = jax.random.normal(ks[0], (BATCH, SEQ, D), jnp.float32)
    c = jax.random.normal(ks[1], (BATCH, D), jnp.float32)
    offsets = jax.random.randint(ks[2], (BATCH, 1), 0, 4096, dtype=jnp.int32)
    positions = offsets + jnp.arange(SEQ, dtype=jnp.int32)[None, :]
    mla_w_in = _w(ks[3], (N_MLA, D, Q_LORA + KV_LORA + QK_ROPE), D)
    mla_q_norm = 1.0 + 0.02 * jax.random.normal(ks[4], (N_MLA, Q_LORA), jnp.float32)
    mla_w_qb = _w(ks[5], (N_MLA, Q_LORA, MLA_HEADS * (QK_NOPE + QK_ROPE)), Q_LORA)
    mla_kv_norm = 1.0 + 0.02 * jax.random.normal(ks[6], (N_MLA, KV_LORA), jnp.float32)
    mla_w_kvb = _w(ks[7], (N_MLA, KV_LORA, MLA_HEADS * (QK_NOPE + V_HEAD)), KV_LORA)
    mla_w_o = _w(ks[8], (N_MLA, MLA_HEADS * V_HEAD, D), MLA_HEADS * V_HEAD, BETA)
    hgrn_lb = 0.5 * jax.random.normal(ks[9], (N_HGRN, HGRN_HEADS * HGRN_EXPAND), jnp.float32)
    hgrn_w_in = _w(ks[10], (N_HGRN, D, 2 * HGRN_HEADS * HGRN_EXPAND + HGRN_HEADS * HGRN_V + D), D)
    hgrn_g_norm = 1.0 + 0.02 * jax.random.normal(ks[11], (N_HGRN, HGRN_V), jnp.float32)
    hgrn_w_o = _w(ks[12], (N_HGRN, HGRN_HEADS * HGRN_V, D), HGRN_HEADS * HGRN_V, BETA)
    ffn_w_in = _w(ks[13], (DEPTH, D, 2 * D_FF), D)
    ffn_w_out = _w(ks[14], (DEPTH, D_FF, D), D_FF, BETA)
    ada_w = _w(ks[15], (DEPTH, 2, D, 3 * D), D, 0.1)
    ada_b = 0.01 * jax.random.normal(ks[16], (DEPTH, 2, 3 * D), jnp.float32)
    ln_g = 1.0 + 0.02 * jax.random.normal(ks[17], (DEPTH, 2, D), jnp.float32)
    ln_b = 0.01 * jax.random.normal(ks[18], (DEPTH, 2, D), jnp.float32)
    return {'x': x, 'c': c, 'positions': positions,
            'mla_w_in': mla_w_in, 'mla_q_norm': mla_q_norm, 'mla_w_qb': mla_w_qb,
            'mla_kv_norm': mla_kv_norm, 'mla_w_kvb': mla_w_kvb, 'mla_w_o': mla_w_o,
            'hgrn_lb': hgrn_lb, 'hgrn_w_in': hgrn_w_in, 'hgrn_g_norm': hgrn_g_norm, 'hgrn_w_o': hgrn_w_o,
            'ffn_w_in': ffn_w_in, 'ffn_w_out': ffn_w_out,
            'ada_w': ada_w, 'ada_b': ada_b, 'ln_g': ln_g, 'ln_b': ln_b}


def _fwd_reference(x, c, positions, mla_w_in, mla_q_norm, mla_w_qb, mla_kv_norm, mla_w_kvb, mla_w_o,
              hgrn_lb, hgrn_w_in, hgrn_g_norm, hgrn_w_o, ffn_w_in, ffn_w_out,
              ada_w, ada_b, ln_g, ln_b):
    cos, sin = rope_cos_sin(positions)
    lb_soft = jax.nn.softmax(hgrn_lb.astype(jnp.float32), axis=0)
    lower_bounds = jnp.cumsum(lb_soft, axis=0) - lb_soft[0]
    for layer in range(DEPTH):
        j = layer // N_MIXERS
        shift, scale, gate = ada_mod(c, ada_w[layer, 0], ada_b[layer, 0])
        h = x * (1.0 + scale) + shift
        if layer % N_MIXERS == 0:
            y = mla(h, cos, sin, mla_w_in[j], mla_q_norm[j], mla_w_qb[j],
                    mla_kv_norm[j], mla_w_kvb[j], mla_w_o[j])
        else:
            y = hgrn2(h, lower_bounds[j], hgrn_w_in[j], hgrn_g_norm[j], hgrn_w_o[j])
        x = layer_norm(ALPHA * x + (1.0 + gate) * y, ln_g[layer, 0], ln_b[layer, 0])
        shift, scale, gate = ada_mod(c, ada_w[layer, 1], ada_b[layer, 1])
        h = x * (1.0 + scale) + shift
        y = swiglu(h, ffn_w_in[layer], ffn_w_out[layer])
        x = layer_norm(ALPHA * x + (1.0 + gate) * y, ln_g[layer, 1], ln_b[layer, 1])
    return x


import jax as _jax
import jax.numpy as _jnp

TWIN_FORMAT = 'train_step'
FWD_PARAMS = ['x', 'c', 'positions', 'mla_w_in', 'mla_q_norm', 'mla_w_qb', 'mla_kv_norm', 'mla_w_kvb', 'mla_w_o', 'hgrn_lb', 'hgrn_w_in', 'hgrn_g_norm', 'hgrn_w_o', 'ffn_w_in', 'ffn_w_out', 'ada_w', 'ada_b', 'ln_g', 'ln_b']
TWIN_WEIGHTS = ['mla_w_in', 'mla_q_norm', 'mla_w_qb', 'mla_kv_norm', 'mla_w_kvb', 'mla_w_o', 'hgrn_lb', 'hgrn_w_in', 'hgrn_g_norm', 'hgrn_w_o', 'ffn_w_in', 'ffn_w_out', 'ada_w', 'ada_b', 'ln_g', 'ln_b']
TWIN_DIFF_INPUT = 'x'
TWIN_INPUTS = ['x', 'c', 'positions', 'mla_w_in', 'mla_q_norm', 'mla_w_qb', 'mla_kv_norm', 'mla_w_kvb', 'mla_w_o', 'hgrn_lb', 'hgrn_w_in', 'hgrn_g_norm', 'hgrn_w_o', 'ffn_w_in', 'ffn_w_out', 'ada_w', 'ada_b', 'ln_g', 'ln_b', 'loss_target', 'm_mla_w_in', 'm_mla_q_norm', 'm_mla_w_qb', 'm_mla_kv_norm', 'm_mla_w_kvb', 'm_mla_w_o', 'm_hgrn_lb', 'm_hgrn_w_in', 'm_hgrn_g_norm', 'm_hgrn_w_o', 'm_ffn_w_in', 'm_ffn_w_out', 'm_ada_w', 'm_ada_b', 'm_ln_g', 'm_ln_b', 'v_mla_w_in', 'v_mla_q_norm', 'v_mla_w_qb', 'v_mla_kv_norm', 'v_mla_w_kvb', 'v_mla_w_o', 'v_hgrn_lb', 'v_hgrn_w_in', 'v_hgrn_g_norm', 'v_hgrn_w_o', 'v_ffn_w_in', 'v_ffn_w_out', 'v_ada_w', 'v_ada_b', 'v_ln_g', 'v_ln_b']
TWIN_OUTPUTS = ['loss', 'grad_x', 'grad_mla_w_in', 'grad_mla_q_norm', 'grad_mla_w_qb', 'grad_mla_kv_norm', 'grad_mla_w_kvb', 'grad_mla_w_o', 'grad_hgrn_lb', 'grad_hgrn_w_in', 'grad_hgrn_g_norm', 'grad_hgrn_w_o', 'grad_ffn_w_in', 'grad_ffn_w_out', 'grad_ada_w', 'grad_ada_b', 'grad_ln_g', 'grad_ln_b', 'delta_mla_w_in', 'delta_mla_q_norm', 'delta_mla_w_qb', 'delta_mla_kv_norm', 'delta_mla_w_kvb', 'delta_mla_w_o', 'delta_hgrn_lb', 'delta_hgrn_w_in', 'delta_hgrn_g_norm', 'delta_hgrn_w_o', 'delta_ffn_w_in', 'delta_ffn_w_out', 'delta_ada_w', 'delta_ada_b', 'delta_ln_g', 'delta_ln_b', 'new_m_mla_w_in', 'new_m_mla_q_norm', 'new_m_mla_w_qb', 'new_m_mla_kv_norm', 'new_m_mla_w_kvb', 'new_m_mla_w_o', 'new_m_hgrn_lb', 'new_m_hgrn_w_in', 'new_m_hgrn_g_norm', 'new_m_hgrn_w_o', 'new_m_ffn_w_in', 'new_m_ffn_w_out', 'new_m_ada_w', 'new_m_ada_b', 'new_m_ln_g', 'new_m_ln_b', 'new_v_mla_w_in', 'new_v_mla_q_norm', 'new_v_mla_w_qb', 'new_v_mla_kv_norm', 'new_v_mla_w_kvb', 'new_v_mla_w_o', 'new_v_hgrn_lb', 'new_v_hgrn_w_in', 'new_v_hgrn_g_norm', 'new_v_hgrn_w_o', 'new_v_ffn_w_in', 'new_v_ffn_w_out', 'new_v_ada_w', 'new_v_ada_b', 'new_v_ln_g', 'new_v_ln_b']
TWIN_LEAF_KINDS = {'loss': 'loss', 'grad_x': 'grad_x', 'grad_mla_w_in': 'grad_w', 'grad_mla_q_norm': 'grad_w', 'grad_mla_w_qb': 'grad_w', 'grad_mla_kv_norm': 'grad_w', 'grad_mla_w_kvb': 'grad_w', 'grad_mla_w_o': 'grad_w', 'grad_hgrn_lb': 'grad_w', 'grad_hgrn_w_in': 'grad_w', 'grad_hgrn_g_norm': 'grad_w', 'grad_hgrn_w_o': 'grad_w', 'grad_ffn_w_in': 'grad_w', 'grad_ffn_w_out': 'grad_w', 'grad_ada_w': 'grad_w', 'grad_ada_b': 'grad_w', 'grad_ln_g': 'grad_w', 'grad_ln_b': 'grad_w', 'delta_mla_w_in': 'delta_w', 'delta_mla_q_norm': 'delta_w', 'delta_mla_w_qb': 'delta_w', 'delta_mla_kv_norm': 'delta_w', 'delta_mla_w_kvb': 'delta_w', 'delta_mla_w_o': 'delta_w', 'delta_hgrn_lb': 'delta_w', 'delta_hgrn_w_in': 'delta_w', 'delta_hgrn_g_norm': 'delta_w', 'delta_hgrn_w_o': 'delta_w', 'delta_ffn_w_in': 'delta_w', 'delta_ffn_w_out': 'delta_w', 'delta_ada_w': 'delta_w', 'delta_ada_b': 'delta_w', 'delta_ln_g': 'delta_w', 'delta_ln_b': 'delta_w', 'new_m_mla_w_in': 'new_m', 'new_m_mla_q_norm': 'new_m', 'new_m_mla_w_qb': 'new_m', 'new_m_mla_kv_norm': 'new_m', 'new_m_mla_w_kvb': 'new_m', 'new_m_mla_w_o': 'new_m', 'new_m_hgrn_lb': 'new_m', 'new_m_hgrn_w_in': 'new_m', 'new_m_hgrn_g_norm': 'new_m', 'new_m_hgrn_w_o': 'new_m', 'new_m_ffn_w_in': 'new_m', 'new_m_ffn_w_out': 'new_m', 'new_m_ada_w': 'new_m', 'new_m_ada_b': 'new_m', 'new_m_ln_g': 'new_m', 'new_m_ln_b': 'new_m', 'new_v_mla_w_in': 'new_v', 'new_v_mla_q_norm': 'new_v', 'new_v_mla_w_qb': 'new_v', 'new_v_mla_kv_norm': 'new_v', 'new_v_mla_w_kvb': 'new_v', 'new_v_mla_w_o': 'new_v', 'new_v_hgrn_lb': 'new_v', 'new_v_hgrn_w_in': 'new_v', 'new_v_hgrn_g_norm': 'new_v', 'new_v_hgrn_w_o': 'new_v', 'new_v_ffn_w_in': 'new_v', 'new_v_ffn_w_out': 'new_v', 'new_v_ada_w': 'new_v', 'new_v_ada_b': 'new_v', 'new_v_ln_g': 'new_v', 'new_v_ln_b': 'new_v'}


def _forward(args):
    return _fwd_reference(*[args[k] for k in FWD_PARAMS])


def _output_shape():
    out = _jax.eval_shape(lambda: _forward(_fwd_setup_inputs(0)))
    return out.shape, out.dtype

N_MICROBATCH = 1
ADAM_LR = 0.001
ADAM_B1 = 0.9
ADAM_B2 = 0.999
ADAM_EPS = 1e-08
ADAM_WD = 0.01
ADAM_STEP = 10
PER_EXAMPLE_BATCH_AXIS = {'x': 0, 'c': 0, 'positions': 0, 'loss_target': 0}
SHARED_INPUTS = []
_WEIGHT_DTYPES = {'mla_w_in': _jnp.float32, 'mla_q_norm': _jnp.float32, 'mla_w_qb': _jnp.float32, 'mla_kv_norm': _jnp.float32, 'mla_w_kvb': _jnp.float32, 'mla_w_o': _jnp.float32, 'hgrn_lb': _jnp.float32, 'hgrn_w_in': _jnp.float32, 'hgrn_g_norm': _jnp.float32, 'hgrn_w_o': _jnp.float32, 'ffn_w_in': _jnp.float32, 'ffn_w_out': _jnp.float32, 'ada_w': _jnp.float32, 'ada_b': _jnp.float32, 'ln_g': _jnp.float32, 'ln_b': _jnp.float32}
MOMENT_SCALE = {'mla_w_in': 1.583908e-02, 'mla_q_norm': 9.769472e-03, 'mla_w_qb': 6.937867e-03, 'mla_kv_norm': 2.884102e-02, 'mla_w_kvb': 9.337610e-03, 'mla_w_o': 2.702049e-02, 'hgrn_lb': 2.423698e-03, 'hgrn_w_in': 1.963607e-02, 'hgrn_g_norm': 7.797852e-02, 'hgrn_w_o': 6.456812e-02, 'ffn_w_in': 1.664663e-02, 'ffn_w_out': 6.463362e-02, 'ada_w': 2.363066e-02, 'ada_b': 4.158084e-02, 'ln_g': 1.137233e+01, 'ln_b': 3.755206e-01}


def _to_microbatches(a, axis):
    t = _jnp.moveaxis(a, axis, 0)
    t = t.reshape((N_MICROBATCH, t.shape[0] // N_MICROBATCH) + t.shape[1:])
    return _jnp.moveaxis(t, 1, axis + 1)


def setup_inputs(seed: int = 0) -> dict:
    inp = _fwd_setup_inputs(seed)
    key = _jax.random.fold_in(_jax.random.key(seed), 7919)
    shape, _ = _output_shape()
    out = dict(inp)
    out["loss_target"] = _jax.random.normal(_jax.random.fold_in(key, 0), shape, _jnp.float32)
    for i, name in enumerate(TWIN_WEIGHTS):
        w = inp[name].astype(_jnp.float32)
        if MOMENT_SCALE is None:
            s = _jnp.sqrt(_jnp.mean(_jnp.square(w)) + 1e-30)
        else:
            s = MOMENT_SCALE[name]
        km, kv = _jax.random.split(_jax.random.fold_in(key, i + 1))
        out[name] = w
        out["m_" + name] = s * _jax.random.normal(km, w.shape, _jnp.float32)
        out["v_" + name] = (s * s) * _jax.random.uniform(kv, w.shape, _jnp.float32, 0.5, 1.5)
    if N_MICROBATCH > 1:
        for name, axis in PER_EXAMPLE_BATCH_AXIS.items():
            out[name] = _to_microbatches(out[name], axis)
    return {'x': out['x'], 'c': out['c'], 'positions': out['positions'], 'mla_w_in': out['mla_w_in'], 'mla_q_norm': out['mla_q_norm'], 'mla_w_qb': out['mla_w_qb'], 'mla_kv_norm': out['mla_kv_norm'], 'mla_w_kvb': out['mla_w_kvb'], 'mla_w_o': out['mla_w_o'], 'hgrn_lb': out['hgrn_lb'], 'hgrn_w_in': out['hgrn_w_in'], 'hgrn_g_norm': out['hgrn_g_norm'], 'hgrn_w_o': out['hgrn_w_o'], 'ffn_w_in': out['ffn_w_in'], 'ffn_w_out': out['ffn_w_out'], 'ada_w': out['ada_w'], 'ada_b': out['ada_b'], 'ln_g': out['ln_g'], 'ln_b': out['ln_b'], 'loss_target': out['loss_target'], 'm_mla_w_in': out['m_mla_w_in'], 'm_mla_q_norm': out['m_mla_q_norm'], 'm_mla_w_qb': out['m_mla_w_qb'], 'm_mla_kv_norm': out['m_mla_kv_norm'], 'm_mla_w_kvb': out['m_mla_w_kvb'], 'm_mla_w_o': out['m_mla_w_o'], 'm_hgrn_lb': out['m_hgrn_lb'], 'm_hgrn_w_in': out['m_hgrn_w_in'], 'm_hgrn_g_norm': out['m_hgrn_g_norm'], 'm_hgrn_w_o': out['m_hgrn_w_o'], 'm_ffn_w_in': out['m_ffn_w_in'], 'm_ffn_w_out': out['m_ffn_w_out'], 'm_ada_w': out['m_ada_w'], 'm_ada_b': out['m_ada_b'], 'm_ln_g': out['m_ln_g'], 'm_ln_b': out['m_ln_b'], 'v_mla_w_in': out['v_mla_w_in'], 'v_mla_q_norm': out['v_mla_q_norm'], 'v_mla_w_qb': out['v_mla_w_qb'], 'v_mla_kv_norm': out['v_mla_kv_norm'], 'v_mla_w_kvb': out['v_mla_w_kvb'], 'v_mla_w_o': out['v_mla_w_o'], 'v_hgrn_lb': out['v_hgrn_lb'], 'v_hgrn_w_in': out['v_hgrn_w_in'], 'v_hgrn_g_norm': out['v_hgrn_g_norm'], 'v_hgrn_w_o': out['v_hgrn_w_o'], 'v_ffn_w_in': out['v_ffn_w_in'], 'v_ffn_w_out': out['v_ffn_w_out'], 'v_ada_w': out['v_ada_w'], 'v_ada_b': out['v_ada_b'], 'v_ln_g': out['v_ln_g'], 'v_ln_b': out['v_ln_b']}


def _loss(weights, diff, rest, loss_target):
    with _jax.named_scope("forward"):
        args = {**rest, TWIN_DIFF_INPUT: diff, **{k: w.astype(_WEIGHT_DTYPES[k]) for k, w in weights.items()}}
        y = _forward(args)
    with _jax.named_scope("loss_head"):
        err = _jnp.square(y.astype(_jnp.float32) - loss_target)
        return 0.5 * _jnp.sum(_jnp.mean(err, axis=-1)) if err.ndim else 0.5 * err


def _adamw(w, g, m, v):
    m = ADAM_B1 * m + (1.0 - ADAM_B1) * g
    v = ADAM_B2 * v + (1.0 - ADAM_B2) * _jnp.square(g)
    m_hat = m / (1.0 - ADAM_B1 ** ADAM_STEP)
    v_hat = v / (1.0 - ADAM_B2 ** ADAM_STEP)
    delta = -ADAM_LR * (m_hat / (_jnp.sqrt(v_hat) + ADAM_EPS) + ADAM_WD * w)
    return delta, m, v


def reference(x, c, positions, mla_w_in, mla_q_norm, mla_w_qb, mla_kv_norm, mla_w_kvb, mla_w_o, hgrn_lb, hgrn_w_in, hgrn_g_norm, hgrn_w_o, ffn_w_in, ffn_w_out, ada_w, ada_b, ln_g, ln_b, loss_target, m_mla_w_in, m_mla_q_norm, m_mla_w_qb, m_mla_kv_norm, m_mla_w_kvb, m_mla_w_o, m_hgrn_lb, m_hgrn_w_in, m_hgrn_g_norm, m_hgrn_w_o, m_ffn_w_in, m_ffn_w_out, m_ada_w, m_ada_b, m_ln_g, m_ln_b, v_mla_w_in, v_mla_q_norm, v_mla_w_qb, v_mla_kv_norm, v_mla_w_kvb, v_mla_w_o, v_hgrn_lb, v_hgrn_w_in, v_hgrn_g_norm, v_hgrn_w_o, v_ffn_w_in, v_ffn_w_out, v_ada_w, v_ada_b, v_ln_g, v_ln_b):
    given = dict(x=x, c=c, positions=positions, mla_w_in=mla_w_in, mla_q_norm=mla_q_norm, mla_w_qb=mla_w_qb, mla_kv_norm=mla_kv_norm, mla_w_kvb=mla_w_kvb, mla_w_o=mla_w_o, hgrn_lb=hgrn_lb, hgrn_w_in=hgrn_w_in, hgrn_g_norm=hgrn_g_norm, hgrn_w_o=hgrn_w_o, ffn_w_in=ffn_w_in, ffn_w_out=ffn_w_out, ada_w=ada_w, ada_b=ada_b, ln_g=ln_g, ln_b=ln_b, loss_target=loss_target, m_mla_w_in=m_mla_w_in, m_mla_q_norm=m_mla_q_norm, m_mla_w_qb=m_mla_w_qb, m_mla_kv_norm=m_mla_kv_norm, m_mla_w_kvb=m_mla_w_kvb, m_mla_w_o=m_mla_w_o, m_hgrn_lb=m_hgrn_lb, m_hgrn_w_in=m_hgrn_w_in, m_hgrn_g_norm=m_hgrn_g_norm, m_hgrn_w_o=m_hgrn_w_o, m_ffn_w_in=m_ffn_w_in, m_ffn_w_out=m_ffn_w_out, m_ada_w=m_ada_w, m_ada_b=m_ada_b, m_ln_g=m_ln_g, m_ln_b=m_ln_b, v_mla_w_in=v_mla_w_in, v_mla_q_norm=v_mla_q_norm, v_mla_w_qb=v_mla_w_qb, v_mla_kv_norm=v_mla_kv_norm, v_mla_w_kvb=v_mla_w_kvb, v_mla_w_o=v_mla_w_o, v_hgrn_lb=v_hgrn_lb, v_hgrn_w_in=v_hgrn_w_in, v_hgrn_g_norm=v_hgrn_g_norm, v_hgrn_w_o=v_hgrn_w_o, v_ffn_w_in=v_ffn_w_in, v_ffn_w_out=v_ffn_w_out, v_ada_w=v_ada_w, v_ada_b=v_ada_b, v_ln_g=v_ln_g, v_ln_b=v_ln_b)
    weights = {n: given[n] for n in TWIN_WEIGHTS}
    shared = {n: given[n] for n in SHARED_INPUTS}
    per_example = {n: given[n] for n in ['x', 'c', 'positions']}
    grad_fn = _jax.value_and_grad(_loss, argnums=(0, 1))

    def one_microbatch(ex, loss_target):
        ex = dict(ex)
        diff = ex.pop(TWIN_DIFF_INPUT)
        return grad_fn(weights, diff, {**shared, **ex}, loss_target)

    if N_MICROBATCH == 1:
        loss, (grad_w, grad_x) = one_microbatch(per_example, given["loss_target"])
    else:
        def body(carry, xs):
            loss_sum, grad_sum = carry
            l_k, (gw_k, gx_k) = one_microbatch(xs[0], xs[1])
            with _jax.named_scope("update"):
                return (loss_sum + l_k, _jax.tree.map(_jnp.add, grad_sum, gw_k)), gx_k

        init = (_jnp.zeros((), _jnp.float32), _jax.tree.map(_jnp.zeros_like, weights))
        (loss, grad_w), grad_x = _jax.lax.scan(body, init, (per_example, given["loss_target"]))
    with _jax.named_scope("update"):
        delta_w, new_m, new_v = {}, {}, {}
        for n in TWIN_WEIGHTS:
            delta_w[n], new_m[n], new_v[n] = _adamw(weights[n], grad_w[n], given["m_" + n], given["v_" + n])
    return (loss, grad_x, *[grad_w[n] for n in TWIN_WEIGHTS], *[delta_w[n] for n in TWIN_WEIGHTS],
            *[new_m[n] for n in TWIN_WEIGHTS], *[new_v[n] for n in TWIN_WEIGHTS])
```

```python
import functools
import math

import numpy as np
import jax
import jax.numpy as jnp
from jax import lax
from jax.experimental import pallas as pl
from jax.experimental.pallas import tpu as pltpu

F32 = jnp.float32
BF16 = jnp.bfloat16

N_DEV = 8
LANES = 128
VMEM_LIMIT = 52 * 1024 * 1024

D_MODEL = 1024
DEPTH = 4
MLA_HEADS = 16
QK_NOPE = 64
QK_ROPE = 32
V_HEAD = 64
Q_LORA = 768
KV_LORA = 256
ROPE_THETA = 10000.0
HGRN_EXPAND = 128
HGRN_CHUNK = 64
HGRN_SUB = 16
D_FF = 2816
ALPHA = (2.0 * DEPTH) ** 0.25
LN_EPS = 1e-5
RMS_EPS = 1e-6
ADAM_LR = 0.001
ADAM_B1 = 0.9
ADAM_B2 = 0.999
ADAM_EPS = 1e-08
ADAM_WD = 0.01
ADAM_STEP = 10

ATTN_TQ = 256
ROW_TILE = 256

WEIGHTS = ['mla_w_in', 'mla_q_norm', 'mla_w_qb', 'mla_kv_norm', 'mla_w_kvb', 'mla_w_o', 'hgrn_lb', 'hgrn_w_in',
           'hgrn_g_norm', 'hgrn_w_o', 'ffn_w_in', 'ffn_w_out', 'ada_w', 'ada_b', 'ln_g', 'ln_b']
COL_SHARDED = ['mla_w_in', 'mla_w_qb', 'mla_w_kvb', 'hgrn_w_in', 'ffn_w_in']
ROW_SHARDED = ['mla_w_o', 'hgrn_w_o', 'ffn_w_out']


def _params(*sem):
    if sem:
        return pltpu.CompilerParams(dimension_semantics=sem, vmem_limit_bytes=VMEM_LIMIT)
    return pltpu.CompilerParams(vmem_limit_bytes=VMEM_LIMIT)


def _pick(n, cap):
    best = None
    for t in range(LANES, min(n, cap) + 1, LANES):
        if n % t == 0:
            best = t
    return best or n


def _pick_rows(n, cap):
    best = None
    for t in range(8, min(n, cap) + 1, 8):
        if n % t == 0:
            best = t
    return best or n


def matmul(name, a, b, *, ta=False, tb=False, out_dtype=F32, tm_cap=1024, tn_cap=1024, tk_cap=2048):
    (k1, m) = a.shape if ta else a.shape[::-1]
    (n, k2) = b.shape if tb else b.shape[::-1]
    assert k1 == k2, (name, a.shape, b.shape)
    tm, tn, tk = _pick(m, tm_cap), _pick(n, tn_cap), _pick(k1, tk_cap)
    nk = k1 // tk
    dims = (((0 if ta else 1,), (1 if tb else 0,)), ((), ()))

    def prod(a_ref, b_ref):
        return lax.dot_general(a_ref[...].astype(BF16), b_ref[...].astype(BF16), dims, preferred_element_type=F32)

    if nk == 1:
        def body(a_ref, b_ref, o_ref):
            o_ref[...] = prod(a_ref, b_ref).astype(o_ref.dtype)
        scratch = []
    else:
        def body(a_ref, b_ref, o_ref, acc_ref):
            k = pl.program_id(2)

            @pl.when(k == 0)
            def _():
                acc_ref[...] = jnp.zeros_like(acc_ref)

            acc_ref[...] += prod(a_ref, b_ref)

            @pl.when(k == nk - 1)
            def _():
                o_ref[...] = acc_ref[...].astype(o_ref.dtype)
        scratch = [pltpu.VMEM((tm, tn), F32)]

    a_spec = pl.BlockSpec((tk, tm), lambda i, j, k: (k, i)) if ta else pl.BlockSpec((tm, tk), lambda i, j, k: (i, k))
    b_spec = pl.BlockSpec((tn, tk), lambda i, j, k: (j, k)) if tb else pl.BlockSpec((tk, tn), lambda i, j, k: (k, j))
    return pl.pallas_call(
        body, name=name, grid=(m // tm, n // tn, nk),
        in_specs=[a_spec, b_spec], out_specs=pl.BlockSpec((tm, tn), lambda i, j, k: (i, j)),
        out_shape=jax.ShapeDtypeStruct((m, n), out_dtype), scratch_shapes=scratch,
        compiler_params=_params("parallel", "parallel", "arbitrary"),
    )(a, b)


def mm3(name, a3, w, **kw):
    bsz, s, k = a3.shape
    out = matmul(name, a3.reshape(bsz * s, k), w, **kw)
    return out.reshape(bsz, s, out.shape[-1])


def wgrad(name, a3, g3):
    bsz, s, k = a3.shape
    return matmul(name, a3.reshape(bsz * s, k), g3.reshape(bsz * s, g3.shape[-1]), ta=True)


def _dg(a, b, ca, cb, **kw):
    return lax.dot_general(a, b, (((ca,), (cb,)), ((), ())), preferred_element_type=F32, **kw)


@functools.partial(jax.custom_vjp, nondiff_argnums=(2, 3))
def bdot(a, b, ca, cb):
    return _dg(a.astype(BF16), b.astype(BF16), ca, cb)


def _bdot_fwd(a, b, ca, cb):
    return bdot(a, b, ca, cb), (a, b)


def _bdot_bwd(ca, cb, res, g):
    a, b = res
    a16, b16, g16 = a.astype(BF16), b.astype(BF16), g.astype(BF16)
    if ca == 1:
        da = _dg(g16, b16, 1, 1 if cb == 0 else 0)
    else:
        da = _dg(b16, g16, 1 if cb == 0 else 0, 1)
    if cb == 0:
        db = _dg(a16, g16, 0 if ca == 1 else 1, 0)
    else:
        db = _dg(g16, a16, 0, 0 if ca == 1 else 1)
    return da, db


bdot.defvjp(_bdot_fwd, _bdot_bwd)


def hdot(a, b, ca=1, cb=0):
    return _dg(a, b, ca, cb, precision=lax.Precision.HIGHEST)


def _row_specs(rows, exs, globs, ts):
    specs = [pl.BlockSpec((1, ts, w), lambda b, s, j=j: (b, s, j)) for (_, j, w) in rows]
    specs += [pl.BlockSpec((1, 1, e.shape[-1]), lambda b, s: (b, 0, 0)) for e in exs]
    specs += [pl.BlockSpec((1, g.shape[-1]), lambda b, s: (0, 0)) for g in globs]
    return specs


def _store_pieces(o_ref, pieces, widths):
    off = 0
    for p, w in zip(pieces, widths):
        o_ref[0, :, off:off + w] = p.astype(o_ref.dtype)
        off += w


def _load_pieces(c_ref, widths):
    out, off = [], 0
    for w in widths:
        out.append(c_ref[0, :, off:off + w].astype(F32))
        off += w
    return out


def rowwise(name, f, rows, exs, globs, outs, *, ts, accs=()):
    bsz, s = rows[0][0].shape[:2]
    ts = min(ts, s)
    n_r, n_e, n_g, n_o = len(rows), len(exs), len(globs), len(outs)

    def body(*refs):
        rv = [r[0].astype(F32) for r in refs[:n_r]]
        ev = [e[0] for e in refs[n_r:n_r + n_e]]
        gv = [g[...] for g in refs[n_r + n_e:n_r + n_e + n_g]]
        o_refs = refs[n_r + n_e + n_g:n_r + n_e + n_g + n_o]
        a_refs = refs[n_r + n_e + n_g + n_o:]
        pieces, sums = f(rv, ev, gv)
        idx = 0
        for o_ref, (_, ws) in zip(o_refs, outs):
            _store_pieces(o_ref, pieces[idx:idx + len(ws)], ws)
            idx += len(ws)
        if accs:
            @pl.when((pl.program_id(0) == 0) & (pl.program_id(1) == 0))
            def _():
                for a_ref in a_refs:
                    a_ref[...] = jnp.zeros_like(a_ref)
            for a_ref, val in zip(a_refs, sums):
                a_ref[...] += val

    out_specs = [pl.BlockSpec((1, ts, sum(ws)), lambda b, s: (b, s, 0)) for (_, ws) in outs]
    out_specs += [pl.BlockSpec((1, w), lambda b, s: (0, 0)) for w in accs]
    out_shape = [jax.ShapeDtypeStruct((bsz, s, sum(ws)), dt) for (dt, ws) in outs]
    out_shape += [jax.ShapeDtypeStruct((1, w), F32) for w in accs]
    return pl.pallas_call(
        body, name=name, grid=(bsz, s // ts),
        in_specs=_row_specs(rows, exs, globs, ts), out_specs=out_specs, out_shape=out_shape,
        compiler_params=_params("arbitrary", "arbitrary"),
    )(*[r[0] for r in rows], *exs, *globs)


def rowwise_bwd(name, f, rows, exs, globs, cts, d_groups, *, ts, n_diff, unit_ct=0):
    bsz, s = rows[0][0].shape[:2]
    ts = min(ts, s)
    n_r, n_e, n_g, n_c = len(rows), len(exs), len(globs), len(cts)
    n_d = len(d_groups)

    def body(*refs):
        rv = [r[0].astype(F32) for r in refs[:n_r]]
        ev = [e[0] for e in refs[n_r:n_r + n_e]]
        gv = [g[...] for g in refs[n_r + n_e:n_r + n_e + n_g]]
        base = n_r + n_e + n_g
        c_refs = refs[base:base + n_c]
        d_refs = refs[base + n_c:base + n_c + n_d]
        de_refs = refs[base + n_c + n_d:base + n_c + n_d + n_e]
        dg_refs = refs[base + n_c + n_d + n_e:]
        fixed = rv[n_diff:]
        out, vjp = jax.vjp(lambda r, e, g: f(r + fixed, e, g), rv[:n_diff], ev, gv)
        ct = []
        for c_ref, (_, ws) in zip(c_refs, cts):
            ct += _load_pieces(c_ref, ws)
        ct += [jnp.ones_like(o) for o in out[len(ct):]]
        assert len(ct) == len(out) and len(out) - unit_ct == sum(len(ws) for _, ws in cts), name
        d_r, d_e, d_g = vjp(ct)
        for d_ref, (_, idxs) in zip(d_refs, d_groups):
            _store_pieces(d_ref, [d_r[i] for i in idxs], [rows[i][2] for i in idxs])
        first_s = pl.program_id(1) == 0
        if n_e:
            @pl.when(first_s)
            def _():
                for r in de_refs:
                    r[...] = jnp.zeros_like(r)
            for r, val in zip(de_refs, d_e):
                r[0] += val
        if n_g:
            @pl.when(first_s & (pl.program_id(0) == 0))
            def _():
                for r in dg_refs:
                    r[...] = jnp.zeros_like(r)
            for r, val in zip(dg_refs, d_g):
                r[...] += val

    in_specs = _row_specs(rows, exs, globs, ts)
    in_specs += [pl.BlockSpec((1, ts, sum(ws)), lambda b, s: (b, s, 0)) for (_, ws) in cts]
    out_specs = [pl.BlockSpec((1, ts, sum(rows[i][2] for i in idxs)), lambda b, s: (b, s, 0)) for (_, idxs) in d_groups]
    out_specs += [pl.BlockSpec((1, 1, e.shape[-1]), lambda b, s: (b, 0, 0)) for e in exs]
    out_specs += [pl.BlockSpec((1, g.shape[-1]), lambda b, s: (0, 0)) for g in globs]
    out_shape = [jax.ShapeDtypeStruct((bsz, s, sum(rows[i][2] for i in idxs)), dt) for (dt, idxs) in d_groups]
    out_shape += [jax.ShapeDtypeStruct(e.shape, F32) for e in exs]
    out_shape += [jax.ShapeDtypeStruct(g.shape, F32) for g in globs]
    res = pl.pallas_call(
        body, name=name, grid=(bsz, s // ts),
        in_specs=in_specs, out_specs=out_specs, out_shape=out_shape,
        compiler_params=_params("arbitrary", "arbitrary"),
    )(*[r[0] for r in rows], *exs, *globs, *[c[0] for c in cts])
    return res[:n_d], res[n_d:n_d + n_e], res[n_d + n_e:]


def _full(a):
    return (a, 0, a.shape[-1])


def _view(a, col, w):
    assert col % w == 0
    return (a, col // w, w)


def _layer_norm(z, g, b):
    mu = jnp.mean(z, -1, keepdims=True)
    var = jnp.mean(jnp.square(z - mu), -1, keepdims=True)
    return (z - mu) * lax.rsqrt(var + LN_EPS) * g + b


def _rms_norm(z, g):
    ms = jnp.mean(jnp.square(z), -1, keepdims=True)
    return z * lax.rsqrt(ms + RMS_EPS) * g


def f_mod(rv, ev, gv):
    (x,), (scale, shift) = rv, ev
    return [x * (1.0 + scale) + shift]


def f_mod_with_x(rv, ev, gv):
    return f_mod(rv, ev, gv) + [rv[0]]


def f_ln_mod(rv, ev, gv):
    (x, y), (gate, scale, shift), (g, b) = rv, ev, gv
    xn = _layer_norm(ALPHA * x + (1.0 + gate) * y, g, b)
    return [xn, xn * (1.0 + scale) + shift]


def f_ln_loss(rv, ev, gv):
    (x, y, target), (gate,), (g, b) = rv, ev, gv
    xn = _layer_norm(ALPHA * x + (1.0 + gate) * y, g, b)
    return [0.5 * jnp.mean(jnp.square(xn - target), -1, keepdims=True)]


def f_swiglu(rv, ev, gv):
    gate, up = rv
    return [jax.nn.silu(gate) * up]


def _head_spread(width):
    r2 = QK_ROPE // 2
    j = lax.broadcasted_iota(jnp.int32, (LANES, width), 0)
    col = lax.broadcasted_iota(jnp.int32, (LANES, width), 1) % r2
    return (j == col).astype(F32), (j == col + r2).astype(F32)


def f_mla_mid(rv, ev, gv):
    (q_lat, kv_lat, kr, cos, sin), (q_g, kv_g) = rv, gv
    e1, e2 = _head_spread(cos.shape[-1])
    k1, k2 = hdot(kr, e1), hdot(kr, e2)
    return [_rms_norm(q_lat, q_g), _rms_norm(kv_lat, kv_g), k1 * cos - k2 * sin, k1 * sin + k2 * cos]


def f_mla_post(rv, ev, gv):
    q_nope, q1, q2, kv, cos, sin = rv
    return [q_nope, q1 * cos - q2 * sin, q1 * sin + q2 * cos, kv]


def f_hg_pre(rv, ev, gv):
    (q, fx), (lb,) = rv[:2], gv
    f = lb + (1.0 - lb) * jax.nn.sigmoid(fx)
    return [jax.nn.silu(q), jnp.log(f), 1.0 - f]


def f_hg_pre_with_iv(rv, ev, gv):
    return f_hg_pre(rv, ev, gv) + [rv[2], rv[3]]


def f_hg_post(rv, ev, gv):
    nh = len(rv) // 2
    (gn,) = gv
    return [_rms_norm(rv[h], gn) * jax.nn.silu(rv[nh + h]) for h in range(nh)]


def rope_tables(positions):
    bsz, s = positions.shape
    r2 = QK_ROPE // 2
    width = MLA_HEADS * r2
    inv = (ROPE_THETA ** (-np.arange(0, QK_ROPE, 2, dtype=np.float32) / QK_ROPE)).astype(np.float32)
    inv = jnp.asarray(np.tile(inv, MLA_HEADS)[None, :])
    ts = min(ROW_TILE, s)

    def body(p_ref, inv_ref, cos_ref, sin_ref):
        ang = p_ref[0].astype(F32) * inv_ref[...]
        cos_ref[0] = jnp.cos(ang)
        sin_ref[0] = jnp.sin(ang)

    spec = pl.BlockSpec((1, ts, width), lambda b, s: (b, s, 0))
    return pl.pallas_call(
        body, name="rope_tables", grid=(bsz, s // ts),
        in_specs=[pl.BlockSpec((1, ts, 1), lambda b, s: (b, s, 0)), pl.BlockSpec((1, width), lambda b, s: (0, 0))],
        out_specs=[spec, spec], out_shape=[jax.ShapeDtypeStruct((bsz, s, width), F32)] * 2,
        compiler_params=_params("arbitrary", "arbitrary"),
    )(positions[:, :, None], inv)


def _attn_probs(q, k, row0):
    scale = (QK_NOPE + QK_ROPE) ** -0.5
    s = _dg(q, k, 1, 1) * scale
    rows = row0 + lax.broadcasted_iota(jnp.int32, s.shape, 0)
    cols = lax.broadcasted_iota(jnp.int32, s.shape, 1)
    s = jnp.where(cols <= rows, s, jnp.finfo(F32).min)
    e = jnp.exp(s - jnp.max(s, -1, keepdims=True))
    return e / jnp.sum(e, -1, keepdims=True), scale


def attn_fwd(q, k, v):
    bsz, h, s, dq = q.shape
    dv = v.shape[-1]
    tq = min(ATTN_TQ, s)

    def body(q_ref, k_ref, v_ref, o_ref):
        p, _ = _attn_probs(q_ref[0, 0], k_ref[0, 0], pl.program_id(2) * tq)
        o_ref[0, 0] = _dg(p.astype(BF16), v_ref[0, 0], 1, 0).astype(o_ref.dtype)

    return pl.pallas_call(
        body, name="attn_fwd", grid=(bsz, h, s // tq),
        in_specs=[pl.BlockSpec((1, 1, tq, dq), lambda b, h, i: (b, h, i, 0)),
                  pl.BlockSpec((1, 1, s, dq), lambda b, h, i: (b, h, 0, 0)),
                  pl.BlockSpec((1, 1, s, dv), lambda b, h, i: (b, h, 0, 0))],
        out_specs=pl.BlockSpec((1, 1, tq, dv), lambda b, h, i: (b, h, i, 0)),
        out_shape=jax.ShapeDtypeStruct((bsz, h, s, dv), BF16),
        compiler_params=_params("parallel", "parallel", "arbitrary"),
    )(q, k, v)


def attn_bwd(q, k, v, do):
    bsz, h, s, dq = q.shape
    dv = v.shape[-1]
    tq = min(ATTN_TQ, s)

    def body(q_ref, k_ref, v_ref, do_ref, dq_ref, dk_ref, dv_ref):
        @pl.when(pl.program_id(2) == 0)
        def _():
            dk_ref[...] = jnp.zeros_like(dk_ref)
            dv_ref[...] = jnp.zeros_like(dv_ref)

        qv, kv, vv = q_ref[0, 0], k_ref[0, 0], v_ref[0, 0]
        p, scale = _attn_probs(qv, kv, pl.program_id(2) * tq)
        do16 = do_ref[0, 0].astype(BF16)
        p16 = p.astype(BF16)
        dv_ref[0, 0] += _dg(p16, do16, 0, 0)
        dp = _dg(do16, vv, 1, 1)
        ds = (p * (dp - jnp.sum(dp * p, -1, keepdims=True)) * scale).astype(BF16)
        dq_ref[0, 0] = _dg(ds, kv, 1, 0)
        dk_ref[0, 0] += _dg(ds, qv, 0, 0)

    return pl.pallas_call(
        body, name="attn_bwd", grid=(bsz, h, s // tq),
        in_specs=[pl.BlockSpec((1, 1, tq, dq), lambda b, h, i: (b, h, i, 0)),
                  pl.BlockSpec((1, 1, s, dq), lambda b, h, i: (b, h, 0, 0)),
                  pl.BlockSpec((1, 1, s, dv), lambda b, h, i: (b, h, 0, 0)),
                  pl.BlockSpec((1, 1, tq, dv), lambda b, h, i: (b, h, i, 0))],
        out_specs=[pl.BlockSpec((1, 1, tq, dq), lambda b, h, i: (b, h, i, 0)),
                   pl.BlockSpec((1, 1, s, dq), lambda b, h, i: (b, h, 0, 0)),
                   pl.BlockSpec((1, 1, s, dv), lambda b, h, i: (b, h, 0, 0))],
        out_shape=[jax.ShapeDtypeStruct((bsz, h, s, dq), F32), jax.ShapeDtypeStruct((bsz, h, s, dq), F32),
                   jax.ShapeDtypeStruct((bsz, h, s, dv), F32)],
        compiler_params=_params("parallel", "parallel", "arbitrary"),
    )(q, k, v, do)


def _hgrn_consts():
    sb = HGRN_SUB
    r = lax.broadcasted_iota(jnp.int32, (sb * sb, sb), 0)
    col = lax.broadcasted_iota(jnp.int32, (sb * sb, sb), 1)
    rep_t = (r // sb == col).astype(F32)
    rep_s = (r % sb == col).astype(F32)
    rr = lax.broadcasted_iota(jnp.int32, (sb * sb, 1), 0)
    causal = (rr // sb >= rr % sb).astype(F32)
    a = lax.broadcasted_iota(jnp.int32, (sb, sb), 0)
    b = lax.broadcasted_iota(jnp.int32, (sb, sb), 1)
    tril = (a >= b).astype(F32)
    return rep_t, rep_s, causal, tril


def _hgrn_chunk(qs, gs, ks, vs, st0):
    rep_t, rep_s, causal, tril = _hgrn_consts()
    cum = jnp.zeros((1, gs[0].shape[-1]), F32)
    bs, outs = [], []
    for i in range(len(qs)):
        b = cum + hdot(tril, gs[i])
        o = bdot(qs[i] * jnp.exp(b), st0, 1, 1)
        if i:
            kt = jnp.concatenate([ks[j] * jnp.exp(cum - bs[j]) for j in range(i)], axis=0)
            att = hdot(qs[i] * jnp.exp(b - cum), kt, 1, 1)
            o = o + bdot(att, jnp.concatenate(vs[:i], axis=0), 1, 0)
        decay = jnp.exp(jnp.minimum(hdot(rep_t, b) - hdot(rep_s, b), 0.0)) * causal
        pair = jnp.sum(hdot(rep_t, qs[i]) * hdot(rep_s, ks[i]) * decay, -1, keepdims=True)
        o = o + bdot(hdot(rep_t, pair * rep_s, 0, 0), vs[i], 1, 0)
        bs.append(b)
        outs.append(o)
        cum = cum + jnp.sum(gs[i], 0, keepdims=True)
    kt = jnp.concatenate([ks[j] * jnp.exp(cum - bs[j]) for j in range(len(qs))], axis=0)
    st1 = st0 * jnp.exp(cum) + bdot(jnp.concatenate(vs, axis=0), kt, 0, 0)
    return outs, st1


def _blocks(ref):
    nb = HGRN_CHUNK // HGRN_SUB
    return [ref[0, i * HGRN_SUB:(i + 1) * HGRN_SUB, :].astype(F32) for i in range(nb)]


def hgrn_fwd(qs, logf, kk, proj):
    bsz, s, hk = qs.shape
    kd = HGRN_EXPAND
    nh, nc, c = hk // kd, s // HGRN_CHUNK, HGRN_CHUNK

    def body(q_ref, g_ref, k_ref, v_ref, o_ref, st_ref, state):
        @pl.when(pl.program_id(2) == 0)
        def _():
            state[...] = jnp.zeros_like(state)

        st0 = state[...]
        st_ref[0, 0, 0] = st0
        outs, st1 = _hgrn_chunk(_blocks(q_ref), _blocks(g_ref), _blocks(k_ref), _blocks(v_ref), st0)
        for i, o in enumerate(outs):
            o_ref[0, i * HGRN_SUB:(i + 1) * HGRN_SUB, :] = o
        state[...] = st1

    blk = pl.BlockSpec((1, c, kd), lambda b, h, i: (b, i, h))
    return pl.pallas_call(
        body, name="hgrn_fwd", grid=(bsz, nh, nc),
        in_specs=[blk, blk, blk, pl.BlockSpec((1, c, kd), lambda b, h, i: (b, i, 2 * nh + h))],
        out_specs=[blk, pl.BlockSpec((1, 1, 1, kd, kd), lambda b, h, i: (b, h, i, 0, 0))],
        out_shape=[jax.ShapeDtypeStruct((bsz, s, hk), F32), jax.ShapeDtypeStruct((bsz, nh, nc, kd, kd), F32)],
        scratch_shapes=[pltpu.VMEM((kd, kd), F32)],
        compiler_params=_params("parallel", "parallel", "arbitrary"),
    )(qs, logf, kk, proj)


def hgrn_bwd(qs, logf, kk, proj, states, do):
    bsz, s, hk = qs.shape
    kd = HGRN_EXPAND
    nh, nc, c = hk // kd, s // HGRN_CHUNK, HGRN_CHUNK

    def body(q_ref, g_ref, k_ref, v_ref, st_ref, do_ref, dq_ref, dg_ref, dk_ref, dv_ref, dstate):
        @pl.when(pl.program_id(2) == 0)
        def _():
            dstate[...] = jnp.zeros_like(dstate)

        _, vjp = jax.vjp(_hgrn_chunk, _blocks(q_ref), _blocks(g_ref), _blocks(k_ref), _blocks(v_ref), st_ref[0, 0, 0])
        dq, dg, dk, dv, dst = vjp((_blocks(do_ref), dstate[...]))
        for ref, parts in ((dq_ref, dq), (dg_ref, dg), (dk_ref, dk), (dv_ref, dv)):
            for i, p in enumerate(parts):
                ref[0, i * HGRN_SUB:(i + 1) * HGRN_SUB, :] = p
        dstate[...] = dst

    blk = pl.BlockSpec((1, c, kd), lambda b, h, i: (b, nc - 1 - i, h))
    shape = jax.ShapeDtypeStruct((bsz, s, hk), F32)
    return pl.pallas_call(
        body, name="hgrn_bwd", grid=(bsz, nh, nc),
        in_specs=[blk, blk, blk, pl.BlockSpec((1, c, kd), lambda b, h, i: (b, nc - 1 - i, 2 * nh + h)),
                  pl.BlockSpec((1, 1, 1, kd, kd), lambda b, h, i: (b, h, nc - 1 - i, 0, 0)), blk],
        out_specs=[blk] * 4, out_shape=[shape] * 4,
        scratch_shapes=[pltpu.VMEM((kd, kd), F32)],
        compiler_params=_params("parallel", "parallel", "arbitrary"),
    )(qs, logf, kk, proj, states, do)


def cast_bf16(name, w):
    blk = pl.BlockSpec((1,) + w.shape[1:], lambda l: (l, 0, 0))

    def body(w_ref, o_ref):
        o_ref[...] = w_ref[...].astype(BF16)

    return pl.pallas_call(body, name=name, grid=(w.shape[0],), in_specs=[blk], out_specs=blk,
                          out_shape=jax.ShapeDtypeStruct(w.shape, BF16), compiler_params=_params("arbitrary"))(w)


def _lower_bounds(rows):
    m = functools.reduce(jnp.maximum, rows)
    e = [jnp.exp(r - m) for r in rows]
    z = functools.reduce(lambda a, b: a + b, e)
    soft = [x / z for x in e]
    out, run = [], jnp.zeros_like(rows[0])
    for sft in soft:
        run = run + sft
        out.append(run - soft[0])
    return out


def lower_bounds(lb):
    n = lb.shape[0]

    def body(lb_ref, o_ref):
        for i, r in enumerate(_lower_bounds([lb_ref[i:i + 1, :] for i in range(n)])):
            o_ref[i:i + 1, :] = r

    return pl.pallas_call(body, name="lower_bounds", out_shape=jax.ShapeDtypeStruct(lb.shape, F32),
                          compiler_params=_params())(lb)


def ada_fwd(c_all, ada_w, ada_b):
    nl, ns, d, cols = ada_w.shape
    n_ex = c_all.shape[0]

    def body(c_ref, w_ref, b_ref, o_ref):
        a = jax.nn.silu(c_ref[...]).astype(BF16)
        o_ref[0] = _dg(a, w_ref[0].astype(BF16), 1, 0) + b_ref[0]

    return pl.pallas_call(
        body, name="ada_fwd", grid=(nl * ns,),
        in_specs=[pl.BlockSpec((n_ex, d), lambda i: (0, 0)), pl.BlockSpec((1, d, cols), lambda i: (i, 0, 0)),
                  pl.BlockSpec((1, 1, cols), lambda i: (i, 0, 0))],
        out_specs=pl.BlockSpec((1, n_ex, cols), lambda i: (i, 0, 0)),
        out_shape=jax.ShapeDtypeStruct((nl * ns, n_ex, cols), F32), compiler_params=_params("arbitrary"),
    )(c_all, ada_w.reshape(nl * ns, d, cols), ada_b.reshape(nl * ns, 1, cols))


def ada_bwd(c_all, dmod):
    n, n_ex, cols = dmod.shape
    d = c_all.shape[1]

    def body(c_ref, g_ref, dw_ref, db_ref):
        a = jax.nn.silu(c_ref[...]).astype(BF16)
        g = g_ref[0]
        dw_ref[0] = _dg(a, g.astype(BF16), 0, 0)
        db_ref[0] = jnp.sum(g, 0, keepdims=True)

    return pl.pallas_call(
        body, name="ada_bwd", grid=(n,),
        in_specs=[pl.BlockSpec((n_ex, d), lambda i: (0, 0)), pl.BlockSpec((1, n_ex, cols), lambda i: (i, 0, 0))],
        out_specs=[pl.BlockSpec((1, d, cols), lambda i: (i, 0, 0)), pl.BlockSpec((1, 1, cols), lambda i: (i, 0, 0))],
        out_shape=[jax.ShapeDtypeStruct((n, d, cols), F32), jax.ShapeDtypeStruct((n, 1, cols), F32)],
        compiler_params=_params("arbitrary"),
    )(c_all, dmod)


def _adam_math(g, w, m, v):
    m = ADAM_B1 * m + (1.0 - ADAM_B1) * g
    v = ADAM_B2 * v + (1.0 - ADAM_B2) * jnp.square(g)
    m_hat = m / (1.0 - ADAM_B1 ** ADAM_STEP)
    v_hat = v / (1.0 - ADAM_B2 ** ADAM_STEP)
    delta = -ADAM_LR * (m_hat / (jnp.sqrt(v_hat) + ADAM_EPS) + ADAM_WD * w)
    return delta, m, v


def adam(name, gstack, w, m, v):
    shape = w.shape
    n, cols = gstack.shape[0], shape[-1]
    rows = math.prod(shape[:-1])
    tr = _pick_rows(rows, max(8, (2 * 1024 * 1024) // (4 * cols * n)))

    def body(g_ref, w_ref, m_ref, v_ref, go_ref, d_ref, mo_ref, vo_ref):
        g = g_ref[0]
        for i in range(1, n):
            g = g + g_ref[i]
        delta, m1, v1 = _adam_math(g, w_ref[...], m_ref[...], v_ref[...])
        go_ref[...] = g
        d_ref[...] = delta
        mo_ref[...] = m1
        vo_ref[...] = v1

    blk = pl.BlockSpec((tr, cols), lambda i: (i, 0))
    out = pl.pallas_call(
        body, name=name, grid=(rows // tr,),
        in_specs=[pl.BlockSpec((n, tr, cols), lambda i: (0, i, 0)), blk, blk, blk],
        out_specs=[blk] * 4, out_shape=[jax.ShapeDtypeStruct((rows, cols), F32)] * 4,
        compiler_params=_params("arbitrary"),
    )(gstack.reshape(n, rows, cols), w.reshape(rows, cols), m.reshape(rows, cols), v.reshape(rows, cols))
    return [o.reshape(shape) for o in out]


def adam_lb(gstack, lb, m, v):
    n, nl = gstack.shape[0], lb.shape[0]

    def body(g_ref, w_ref, m_ref, v_ref, go_ref, d_ref, mo_ref, vo_ref):
        rows = [w_ref[i:i + 1, :] for i in range(nl)]
        ct = []
        for i in range(nl):
            g = g_ref[0, i:i + 1, :]
            for j in range(1, n):
                g = g + g_ref[j, i:i + 1, :]
            ct.append(g)
        _, vjp = jax.vjp(_lower_bounds, rows)
        (grads,) = vjp(ct)
        for i in range(nl):
            delta, m1, v1 = _adam_math(grads[i], rows[i], m_ref[i:i + 1, :], v_ref[i:i + 1, :])
            go_ref[i:i + 1, :] = grads[i]
            d_ref[i:i + 1, :] = delta
            mo_ref[i:i + 1, :] = m1
            vo_ref[i:i + 1, :] = v1

    return pl.pallas_call(body, name="adam_hgrn_lb", out_shape=[jax.ShapeDtypeStruct(lb.shape, F32)] * 4,
                          compiler_params=_params())(gstack, lb, m, v)


def exchange(name, items):
    n = len(items)
    out_shape = []
    for src, mode in items:
        out_shape.append(jax.ShapeDtypeStruct(((N_DEV,) + src.shape) if mode == "gather" else src.shape, src.dtype))

    def body(*refs):
        srcs, outs = refs[:n], refs[n:2 * n]
        send_sems, recv_sems, local_sems = refs[2 * n:]
        x, y, c = lax.axis_index("x"), lax.axis_index("y"), lax.axis_index("c")
        me = 4 * x + 2 * y + c
        copies = []
        for i, (_, mode) in enumerate(items):
            mine = srcs[i] if mode == "gather" else srcs[i].at[me]
            cp = pltpu.make_async_copy(mine, outs[i].at[me], local_sems.at[i])
            cp.start()
            copies.append(cp)
            for p in range(1, N_DEV):
                px = 1 - x if p & 4 else x
                py = 1 - y if p & 2 else y
                pc = 1 - c if p & 1 else c
                part = srcs[i] if mode == "gather" else srcs[i].at[4 * px + 2 * py + pc]
                cp = pltpu.make_async_remote_copy(
                    src_ref=part, dst_ref=outs[i].at[me], send_sem=send_sems.at[i, p - 1], recv_sem=recv_sems.at[i, p - 1],
                    device_id=(px, py, pc), device_id_type=pl.DeviceIdType.MESH)
                cp.start()
                copies.append(cp)
        for cp in copies:
            cp.wait()

    any_spec = pl.BlockSpec(memory_space=pl.ANY)
    return pl.pallas_call(
        body, name=name, in_specs=[any_spec] * n, out_specs=[any_spec] * n, out_shape=out_shape,
        scratch_shapes=[pltpu.SemaphoreType.DMA((n, N_DEV - 1)), pltpu.SemaphoreType.DMA((n, N_DEV - 1)),
                        pltpu.SemaphoreType.DMA((n,))],
    )(*[src for src, _ in items])


def _cols_from_gather(g):
    _, l, k, n = g.shape
    return g.transpose(1, 2, 0, 3).reshape(l, k, N_DEV * n)


def _rows_from_gather(g):
    _, l, r, n = g.shape
    return g.transpose(1, 0, 2, 3).reshape(l, N_DEV * r, n)


def _cols_to_slabs(w):
    l, k, n = w.shape
    return w.reshape(l, k, N_DEV, n // N_DEV).transpose(2, 0, 1, 3)


def _rows_to_slabs(w):
    l, k, n = w.shape
    return w.reshape(l, N_DEV, k // N_DEV, n).transpose(1, 0, 2, 3)


def _kr_pad():
    return LANES - QK_ROPE


def _w_in_internal(w):
    return jnp.pad(w, ((0, 0), (0, 0), (0, _kr_pad())))


def _qb_internal(w, inverse=False):
    h, n, r2 = MLA_HEADS, QK_NOPE, QK_ROPE // 2
    lead = w.shape[:-1]
    if not inverse:
        w = w.reshape(lead + (h, n + 2 * r2))
        parts = [w[..., :n], w[..., n:n + r2], w[..., n + r2:]]
        return jnp.concatenate([p.reshape(lead + (-1,)) for p in parts], axis=-1)
    parts = [w[..., :h * n].reshape(lead + (h, n)), w[..., h * n:h * (n + r2)].reshape(lead + (h, r2)),
             w[..., h * (n + r2):].reshape(lead + (h, r2))]
    return jnp.concatenate(parts, axis=-1).reshape(lead + (-1,))


def _kvb_internal(w, inverse=False):
    h, n, vd = MLA_HEADS, QK_NOPE, V_HEAD
    lead = w.shape[:-1]
    if not inverse:
        w = w.reshape(lead + (h, n + vd))
        return jnp.concatenate([w[..., :n].reshape(lead + (-1,)), w[..., n:].reshape(lead + (-1,))], axis=-1)
    parts = [w[..., :h * n].reshape(lead + (h, n)), w[..., h * n:].reshape(lead + (h, vd))]
    return jnp.concatenate(parts, axis=-1).reshape(lead + (-1,))


def _to_heads(parts):
    bsz, s = parts[0].shape[:2]
    t = jnp.concatenate([p.reshape(bsz, s, MLA_HEADS, -1) for p in parts], axis=-1)
    return t.transpose(0, 2, 1, 3)


def _from_heads(t, widths):
    bsz, h, s, _ = t.shape
    t = t.transpose(0, 2, 1, 3)
    out, off = [], 0
    for w in widths:
        out.append(t[..., off:off + w].reshape(bsz, s, h * w))
        off += w
    return out


def _mla_forward(h, w, tabs):
    cos, sin = tabs
    hn, r2 = MLA_HEADS * QK_NOPE, MLA_HEADS * (QK_ROPE // 2)
    proj = mm3("mla_proj", h, w['w_in'])
    qn, kvn, krt = rowwise(
        "mla_mid", lambda rv, ev, gv: (f_mla_mid(rv, ev, gv), []),
        [_view(proj, 0, Q_LORA), _view(proj, Q_LORA, KV_LORA), _view(proj, Q_LORA + KV_LORA, LANES), _full(cos), _full(sin)],
        [], [w['q_norm'], w['kv_norm']], [(BF16, [Q_LORA]), (BF16, [KV_LORA]), (BF16, [r2, r2])], ts=ROW_TILE)
    q = mm3("mla_q", qn, w['w_qb'])
    kv = mm3("mla_kv", kvn, w['w_kvb'])
    qb, kvb = rowwise(
        "mla_post", lambda rv, ev, gv: (f_mla_post(rv, ev, gv), []),
        [_view(q, 0, hn), _view(q, hn, r2), _view(q, hn + r2, r2), _full(kv), _full(cos), _full(sin)],
        [], [], [(BF16, [hn, r2, r2]), (BF16, [kv.shape[-1]])], ts=ROW_TILE)
    qh = _to_heads([qb[..., :hn], qb[..., hn:hn + r2], qb[..., hn + r2:]])
    kh = _to_heads([kvb[..., :hn], krt[..., :r2], krt[..., r2:]])
    vh = _to_heads([kvb[..., hn:]])
    oh = attn_fwd(qh, kh, vh)
    (o,) = _from_heads(oh, [V_HEAD])
    y = mm3("mla_out", o, w['w_o'])
    return y, dict(h=h, proj=proj, qn=qn, kvn=kvn, q=q, kv=kv, qh=qh, kh=kh, vh=vh, o=o)


def _mla_backward(dy, sv, w, tabs):
    cos, sin = tabs
    hn, r2 = MLA_HEADS * QK_NOPE, MLA_HEADS * (QK_ROPE // 2)
    g = {}
    g['w_o'] = wgrad("mla_out_dw", sv['o'], dy)
    do = mm3("mla_out_dx", dy, w['w_o'], tb=True)
    doh = _to_heads([do])
    dqh, dkh, dvh = attn_bwd(sv['qh'], sv['kh'], sv['vh'], doh)
    dqb = jnp.concatenate(_from_heads(dqh, [QK_NOPE, QK_ROPE // 2, QK_ROPE // 2]), axis=-1)
    dkn, dk1, dk2 = _from_heads(dkh, [QK_NOPE, QK_ROPE // 2, QK_ROPE // 2])
    (dvv,) = _from_heads(dvh, [V_HEAD])
    dkvb = jnp.concatenate([dkn, dvv], axis=-1)
    dkrt = jnp.concatenate([dk1, dk2], axis=-1)
    q, kv = sv['q'], sv['kv']
    (dq, dkv), _, _ = rowwise_bwd(
        "mla_post_bwd", f_mla_post,
        [_view(q, 0, hn), _view(q, hn, r2), _view(q, hn + r2, r2), _full(kv), _full(cos), _full(sin)], [], [],
        [(dqb, [hn, r2, r2]), (dkvb, [kv.shape[-1]])], [(BF16, [0, 1, 2]), (BF16, [3])], ts=ROW_TILE, n_diff=4)
    g['w_qb'] = wgrad("mla_q_dw", sv['qn'], dq)
    g['w_kvb'] = wgrad("mla_kv_dw", sv['kvn'], dkv)
    dqn = mm3("mla_q_dx", dq, w['w_qb'], tb=True)
    dkvn = mm3("mla_kv_dx", dkv, w['w_kvb'], tb=True)
    proj = sv['proj']
    (dproj,), _, (g['q_norm'], g['kv_norm']) = rowwise_bwd(
        "mla_mid_bwd", f_mla_mid,
        [_view(proj, 0, Q_LORA), _view(proj, Q_LORA, KV_LORA), _view(proj, Q_LORA + KV_LORA, LANES), _full(cos), _full(sin)],
        [], [w['q_norm'], w['kv_norm']], [(dqn, [Q_LORA]), (dkvn, [KV_LORA]), (dkrt, [r2, r2])],
        [(BF16, [0, 1, 2])], ts=ROW_TILE, n_diff=3)
    g['w_in'] = wgrad("mla_proj_dw", sv['h'], dproj)
    dh = mm3("mla_proj_dx", dproj, w['w_in'], tb=True)
    return dh, g


def _hgrn_views(proj):
    d = proj.shape[-1] // 4
    return [_view(proj, i * d, d) for i in range(4)]


def _head_views(a, col0, nh):
    return [_view(a, col0 + i * HGRN_EXPAND, HGRN_EXPAND) for i in range(nh)]


def _hgrn_forward(h, w):
    proj = mm3("hgrn_proj", h, w['w_in'])
    d = proj.shape[-1] // 4
    nh = d // HGRN_EXPAND
    vq, vf, _, _ = _hgrn_views(proj)
    qs, logf, kk = rowwise("hgrn_pre", lambda rv, ev, gv: (f_hg_pre(rv, ev, gv), []), [vq, vf], [], [w['lb']],
                           [(F32, [d]), (F32, [d]), (F32, [d])], ts=ROW_TILE)
    o, states = hgrn_fwd(qs, logf, kk, proj)
    (z,) = rowwise("hgrn_post", lambda rv, ev, gv: (f_hg_post(rv, ev, gv), []),
                   _head_views(o, 0, nh) + _head_views(proj, 3 * d, nh), [], [w['g_norm']],
                   [(BF16, [HGRN_EXPAND] * nh)], ts=ROW_TILE)
    y = mm3("hgrn_out", z, w['w_o'])
    return y, dict(h=h, proj=proj, qs=qs, logf=logf, kk=kk, o=o, states=states, z=z)


def _hgrn_backward(dy, sv, w):
    g = {}
    proj, o = sv['proj'], sv['o']
    d = proj.shape[-1] // 4
    nh = d // HGRN_EXPAND
    g['w_o'] = wgrad("hgrn_out_dw", sv['z'], dy)
    dz = mm3("hgrn_out_dx", dy, w['w_o'], tb=True)
    (do, dgate), _, (g['g_norm'],) = rowwise_bwd(
        "hgrn_post_bwd", f_hg_post, _head_views(o, 0, nh) + _head_views(proj, 3 * d, nh), [], [w['g_norm']],
        [(dz, [HGRN_EXPAND] * nh)], [(F32, list(range(nh))), (F32, list(range(nh, 2 * nh)))], ts=ROW_TILE, n_diff=2 * nh)
    dqs, dlogf, dkk, dv = hgrn_bwd(sv['qs'], sv['logf'], sv['kk'], proj, sv['states'], do)
    (dproj,), _, (g['lb'],) = rowwise_bwd(
        "hgrn_pre_bwd", f_hg_pre_with_iv, _hgrn_views(proj), [], [w['lb']],
        [(dqs, [d]), (dlogf, [d]), (dkk, [d]), (dv, [d]), (dgate, [d])], [(BF16, [0, 1, 2, 3])], ts=ROW_TILE // 2, n_diff=4)
    g['w_in'] = wgrad("hgrn_proj_dw", sv['h'], dproj)
    dh = mm3("hgrn_proj_dx", dproj, w['w_in'], tb=True)
    return dh, g


def _ffn_forward(h, w):
    u = mm3("ffn_in", h, w['w_in'])
    dff = u.shape[-1] // 2
    (a,) = rowwise("ffn_act", lambda rv, ev, gv: (f_swiglu(rv, ev, gv), []), [_view(u, 0, dff), _view(u, dff, dff)],
                   [], [], [(BF16, [dff])], ts=ROW_TILE)
    y = mm3("ffn_out", a, w['w_out'])
    return y, dict(h=h, u=u, a=a)


def _ffn_backward(dy, sv, w):
    g = {}
    u = sv['u']
    dff = u.shape[-1] // 2
    g['w_out'] = wgrad("ffn_out_dw", sv['a'], dy)
    da = mm3("ffn_out_dx", dy, w['w_out'], tb=True)
    (du,), _, _ = rowwise_bwd("ffn_act_bwd", f_swiglu, [_view(u, 0, dff), _view(u, dff, dff)], [], [],
                              [(da, [dff])], [(BF16, [0, 1])], ts=ROW_TILE // 2, n_diff=2)
    g['w_in'] = wgrad("ffn_in_dw", sv['h'], du)
    dh = mm3("ffn_in_dx", du, w['w_in'], tb=True)
    return dh, g


def kernel(x, c, positions, mla_w_in, mla_q_norm, mla_w_qb, mla_kv_norm, mla_w_kvb, mla_w_o, hgrn_lb, hgrn_w_in, hgrn_g_norm, hgrn_w_o, ffn_w_in, ffn_w_out, ada_w, ada_b, ln_g, ln_b, loss_target, m_mla_w_in, m_mla_q_norm, m_mla_w_qb, m_mla_kv_norm, m_mla_w_kvb, m_mla_w_o, m_hgrn_lb, m_hgrn_w_in, m_hgrn_g_norm, m_hgrn_w_o, m_ffn_w_in, m_ffn_w_out, m_ada_w, m_ada_b, m_ln_g, m_ln_b, v_mla_w_in, v_mla_q_norm, v_mla_w_qb, v_mla_kv_norm, v_mla_w_kvb, v_mla_w_o, v_hgrn_lb, v_hgrn_w_in, v_hgrn_g_norm, v_hgrn_w_o, v_ffn_w_in, v_ffn_w_out, v_ada_w, v_ada_b, v_ln_g, v_ln_b):
    W = dict(zip(WEIGHTS, (mla_w_in, mla_q_norm, mla_w_qb, mla_kv_norm, mla_w_kvb, mla_w_o, hgrn_lb, hgrn_w_in, hgrn_g_norm,
                           hgrn_w_o, ffn_w_in, ffn_w_out, ada_w, ada_b, ln_g, ln_b)))
    M1 = dict(zip(WEIGHTS, (m_mla_w_in, m_mla_q_norm, m_mla_w_qb, m_mla_kv_norm, m_mla_w_kvb, m_mla_w_o, m_hgrn_lb, m_hgrn_w_in,
                            m_hgrn_g_norm, m_hgrn_w_o, m_ffn_w_in, m_ffn_w_out, m_ada_w, m_ada_b, m_ln_g, m_ln_b)))
    M2 = dict(zip(WEIGHTS, (v_mla_w_in, v_mla_q_norm, v_mla_w_qb, v_mla_kv_norm, v_mla_w_kvb, v_mla_w_o, v_hgrn_lb, v_hgrn_w_in,
                            v_hgrn_g_norm, v_hgrn_w_o, v_ffn_w_in, v_ffn_w_out, v_ada_w, v_ada_b, v_ln_g, v_ln_b)))
    bsz, seq, d = x.shape
    depth, n_mla, n_hgrn = ffn_w_in.shape[0], mla_w_in.shape[0], hgrn_w_in.shape[0]
    n_sub = 2 * depth

    big = COL_SHARDED + ROW_SHARDED
    lower_shard = lower_bounds(hgrn_lb)
    items = [(cast_bf16("cast_" + n, W[n]), "gather") for n in big]
    items += [(lower_shard, "gather"), (ln_g, "gather"), (ln_b, "gather"), (c, "gather")]
    got = exchange("gather_params", items)
    G = {n: (_cols_from_gather(a) if n in COL_SHARDED else _rows_from_gather(a)) for n, a in zip(big, got)}
    lower_all = got[len(big)].transpose(1, 0, 2).reshape(n_hgrn, -1)
    ln_g_all = got[len(big) + 1].transpose(1, 2, 0, 3).reshape(depth, 2, d)
    ln_b_all = got[len(big) + 2].transpose(1, 2, 0, 3).reshape(depth, 2, d)
    c_all = got[len(big) + 3].reshape(N_DEV * bsz, d)
    w_in_mla = _w_in_internal(G['mla_w_in'])
    w_qb = _qb_internal(G['mla_w_qb'])
    w_kvb = _kvb_internal(G['mla_w_kvb'])

    cols = ada_w.shape[-1]
    mod_loc = ada_fwd(c_all, ada_w, ada_b)
    (mod_got,) = exchange("scatter_mod", [(mod_loc.reshape(n_sub, N_DEV, bsz, cols).transpose(1, 0, 2, 3), "a2a")])
    mod = mod_got.transpose(1, 2, 0, 3).reshape(n_sub, bsz, 1, 3 * d)
    shift = [mod[k, :, :, 0:d] for k in range(n_sub)]
    scale = [mod[k, :, :, d:2 * d] for k in range(n_sub)]
    gate = [mod[k, :, :, 2 * d:] for k in range(n_sub)]
    lng = [ln_g_all[k // 2, k % 2][None, :] for k in range(n_sub)]
    lnb = [ln_b_all[k // 2, k % 2][None, :] for k in range(n_sub)]

    tabs = rope_tables(positions)

    def sub_weights(k):
        layer, j = k // 2, k // 4
        if k % 2:
            return 'ffn', layer, dict(w_in=G['ffn_w_in'][layer], w_out=G['ffn_w_out'][layer])
        if layer % 2 == 0:
            return 'mla', j, dict(w_in=w_in_mla[j], q_norm=mla_q_norm[j][None, :], w_qb=w_qb[j], kv_norm=mla_kv_norm[j][None, :],
                                  w_kvb=w_kvb[j], w_o=G['mla_w_o'][j])
        return 'hgrn', j, dict(w_in=G['hgrn_w_in'][j], lb=lower_all[j][None, :], g_norm=hgrn_g_norm[j][None, :], w_o=G['hgrn_w_o'][j])

    (h,) = rowwise("mod_first", lambda rv, ev, gv: (f_mod(rv, ev, gv), []), [_full(x)], [scale[0], shift[0]], [],
                   [(BF16, [d])], ts=ROW_TILE)
    xs, ys, saved = [x], [], []
    loss_acc = None
    for k in range(n_sub):
        kind, _, w = sub_weights(k)
        if kind == 'ffn':
            y, sv = _ffn_forward(h, w)
        elif kind == 'mla':
            y, sv = _mla_forward(h, w, tabs)
        else:
            y, sv = _hgrn_forward(h, w)
        ys.append(y)
        saved.append(sv)
        if k + 1 < n_sub:
            xn, h = rowwise("ln_mod", lambda rv, ev, gv: (f_ln_mod(rv, ev, gv), []), [_full(xs[k]), _full(y)],
                            [gate[k], scale[k + 1], shift[k + 1]], [lng[k], lnb[k]], [(F32, [d]), (BF16, [d])], ts=ROW_TILE)
            xs.append(xn)
        else:
            def loss_rows(rv, ev, gv):
                (row,) = f_ln_loss(rv, ev, gv)
                return [], [jnp.broadcast_to(jnp.sum(row, keepdims=True), (1, LANES))]
            (loss_acc,) = rowwise("ln_loss", loss_rows, [_full(xs[k]), _full(y), _full(loss_target)], [gate[k]], [lng[k], lnb[k]],
                                  [], ts=ROW_TILE, accs=[LANES])
    loss = lax.psum(loss_acc[0, 0], ("x", "y", "c"))

    d_shift, d_scale, d_gate = [None] * n_sub, [None] * n_sub, [None] * n_sub
    d_lng, d_lnb = [None] * n_sub, [None] * n_sub
    part = {n: [None] * W[n].shape[0] for n in big + ['mla_q_norm', 'mla_kv_norm', 'hgrn_g_norm']}
    d_lower = [None] * n_hgrn
    k = n_sub - 1
    (dx, dy), (d_gate[k],), (d_lng[k], d_lnb[k]) = rowwise_bwd(
        "ln_loss_bwd", f_ln_loss, [_full(xs[k]), _full(ys[k]), _full(loss_target)], [gate[k]], [lng[k], lnb[k]], [],
        [(F32, [0]), (BF16, [1])], ts=ROW_TILE, n_diff=2, unit_ct=1)
    grad_x = None
    for k in range(n_sub - 1, -1, -1):
        kind, j, w = sub_weights(k)
        if kind == 'ffn':
            dh, g = _ffn_backward(dy, saved[k], w)
            part['ffn_w_in'][j], part['ffn_w_out'][j] = g['w_in'], g['w_out']
        elif kind == 'mla':
            dh, g = _mla_backward(dy, saved[k], w, tabs)
            part['mla_w_in'][j] = g['w_in'][:, :mla_w_in.shape[-1] * N_DEV]
            part['mla_w_qb'][j] = _qb_internal(g['w_qb'], inverse=True)
            part['mla_w_kvb'][j] = _kvb_internal(g['w_kvb'], inverse=True)
            part['mla_w_o'][j] = g['w_o']
            part['mla_q_norm'][j], part['mla_kv_norm'][j] = g['q_norm'][0], g['kv_norm'][0]
        else:
            dh, g = _hgrn_backward(dy, saved[k], w)
            part['hgrn_w_in'][j], part['hgrn_w_o'][j] = g['w_in'], g['w_o']
            part['hgrn_g_norm'][j] = g['g_norm'][0]
            d_lower[j] = g['lb'][0]
        if k:
            (dx, dy), (d_gate[k - 1], d_scale[k], d_shift[k]), (d_lng[k - 1], d_lnb[k - 1]) = rowwise_bwd(
                "ln_mod_bwd", f_ln_mod, [_full(xs[k - 1]), _full(ys[k - 1])], [gate[k - 1], scale[k], shift[k]],
                [lng[k - 1], lnb[k - 1]], [(dx, [d]), (dh, [d])], [(F32, [0]), (BF16, [1])], ts=ROW_TILE, n_diff=2)
        else:
            (grad_x,), (d_scale[0], d_shift[0]), _ = rowwise_bwd(
                "mod_first_bwd", f_mod_with_x, [_full(x)], [scale[0], shift[0]], [], [(dh, [d]), (dx, [d])],
                [(F32, [0])], ts=ROW_TILE, n_diff=1)

    slabs = []
    for n in big:
        full = jnp.stack(part[n])
        slabs.append(((_cols_to_slabs if n in COL_SHARDED else _rows_to_slabs)(full), "a2a"))
    slabs.append((jnp.stack(d_lower).reshape(n_hgrn, N_DEV, -1).transpose(1, 0, 2), "a2a"))
    for parts in (d_lng, d_lnb):
        full = jnp.stack([p[0] for p in parts]).reshape(depth, 2, N_DEV, d // N_DEV)
        slabs.append((full.transpose(2, 0, 1, 3), "a2a"))
    dmod = jnp.concatenate([jnp.stack(d_shift), jnp.stack(d_scale), jnp.stack(d_gate)], axis=-1)
    slabs.append((dmod.reshape(n_sub, bsz, N_DEV, cols).transpose(2, 0, 1, 3), "a2a"))
    small = ['mla_q_norm', 'mla_kv_norm', 'hgrn_g_norm']
    slabs += [(jnp.stack(part[n]), "gather") for n in small]
    got = exchange("scatter_grads", slabs)
    stacks = dict(zip(big + ['hgrn_lb', 'ln_g', 'ln_b', 'dmod'] + small, got))

    dmod_all = stacks['dmod'].transpose(1, 0, 2, 3).reshape(n_sub, N_DEV * bsz, cols)
    g_ada_w, g_ada_b = ada_bwd(c_all, dmod_all)
    stacks['ada_w'] = g_ada_w.reshape((1,) + ada_w.shape)
    stacks['ada_b'] = g_ada_b.reshape((1,) + ada_b.shape)

    res = {}
    for n in WEIGHTS:
        if n == 'hgrn_lb':
            res[n] = adam_lb(stacks[n], W[n], M1[n], M2[n])
        else:
            res[n] = adam("adam_" + n, stacks[n], W[n], M1[n], M2[n])
    return (loss, grad_x, *[res[n][0] for n in WEIGHTS], *[res[n][1] for n in WEIGHTS], *[res[n][2] for n in WEIGHTS],
            *[res[n][3] for n in WEIGHTS])
```

```python
import functools
import math

import numpy as np
import jax
import jax.numpy as jnp
from jax import lax
from jax.experimental import pallas as pl
from jax.experimental.pallas import tpu as pltpu

F32 = jnp.float32
BF16 = jnp.bfloat16

N_DEV = 8
LANES = 128
VMEM_LIMIT = 52 * 1024 * 1024

D_MODEL = 1024
DEPTH = 4
MLA_HEADS = 16
QK_NOPE = 64
QK_ROPE = 32
V_HEAD = 64
Q_LORA = 768
KV_LORA = 256
ROPE_THETA = 10000.0
HGRN_EXPAND = 128
HGRN_CHUNK = 64
HGRN_HEADS_PER_STEP = 2
D_FF = 2816
ALPHA = (2.0 * DEPTH) ** 0.25
LN_EPS = 1e-5
RMS_EPS = 1e-6
ADAM_LR = 0.001
ADAM_B1 = 0.9
ADAM_B2 = 0.999
ADAM_EPS = 1e-08
ADAM_WD = 0.01
ADAM_STEP = 10

ATTN_TQ = 256
ROW_TILE = 256

WEIGHTS = ['mla_w_in', 'mla_q_norm', 'mla_w_qb', 'mla_kv_norm', 'mla_w_kvb', 'mla_w_o', 'hgrn_lb', 'hgrn_w_in',
           'hgrn_g_norm', 'hgrn_w_o', 'ffn_w_in', 'ffn_w_out', 'ada_w', 'ada_b', 'ln_g', 'ln_b']
COL_SHARDED = ['mla_w_in', 'mla_w_qb', 'mla_w_kvb', 'hgrn_w_in', 'ffn_w_in']
ROW_SHARDED = ['mla_w_o', 'hgrn_w_o', 'ffn_w_out']


def _params(*sem):
    if sem:
        return pltpu.CompilerParams(dimension_semantics=sem, vmem_limit_bytes=VMEM_LIMIT)
    return pltpu.CompilerParams(vmem_limit_bytes=VMEM_LIMIT)


def _pick(n, cap):
    best = None
    for t in range(LANES, min(n, cap) + 1, LANES):
        if n % t == 0:
            best = t
    return best or n


def _pick_rows(n, cap):
    best = None
    for t in range(8, min(n, cap) + 1, 8):
        if n % t == 0:
            best = t
    return best or n


def matmul(name, a, b, *, ta=False, tb=False, out_dtype=F32, tm_cap=1024, tn_cap=1024, tk_cap=2048):
    (k1, m) = a.shape if ta else a.shape[::-1]
    (n, k2) = b.shape if tb else b.shape[::-1]
    assert k1 == k2, (name, a.shape, b.shape)
    tm, tn, tk = _pick(m, tm_cap), _pick(n, tn_cap), _pick(k1, tk_cap)
    nk = k1 // tk
    dims = (((0 if ta else 1,), (1 if tb else 0,)), ((), ()))

    def prod(a_ref, b_ref):
        return lax.dot_general(a_ref[...].astype(BF16), b_ref[...].astype(BF16), dims, preferred_element_type=F32)

    if nk == 1:
        def body(a_ref, b_ref, o_ref):
            o_ref[...] = prod(a_ref, b_ref).astype(o_ref.dtype)
        scratch = []
    else:
        def body(a_ref, b_ref, o_ref, acc_ref):
            k = pl.program_id(2)

            @pl.when(k == 0)
            def _():
                acc_ref[...] = jnp.zeros_like(acc_ref)

            acc_ref[...] += prod(a_ref, b_ref)

            @pl.when(k == nk - 1)
            def _():
                o_ref[...] = acc_ref[...].astype(o_ref.dtype)
        scratch = [pltpu.VMEM((tm, tn), F32)]

    a_spec = pl.BlockSpec((tk, tm), lambda i, j, k: (k, i)) if ta else pl.BlockSpec((tm, tk), lambda i, j, k: (i, k))
    b_spec = pl.BlockSpec((tn, tk), lambda i, j, k: (j, k)) if tb else pl.BlockSpec((tk, tn), lambda i, j, k: (k, j))
    return pl.pallas_call(
        body, name=name, grid=(m // tm, n // tn, nk),
        in_specs=[a_spec, b_spec], out_specs=pl.BlockSpec((tm, tn), lambda i, j, k: (i, j)),
        out_shape=jax.ShapeDtypeStruct((m, n), out_dtype), scratch_shapes=scratch,
        compiler_params=_params("parallel", "parallel", "arbitrary"),
    )(a, b)


def mm3(name, a3, w, **kw):
    bsz, s, k = a3.shape
    out = matmul(name, a3.reshape(bsz * s, k), w, **kw)
    return out.reshape(bsz, s, out.shape[-1])


def wgrad(name, a3, g3):
    bsz, s, k = a3.shape
    return matmul(name, a3.reshape(bsz * s, k), g3.reshape(bsz * s, g3.shape[-1]), ta=True)


def _dg(a, b, ca, cb, **kw):
    return lax.dot_general(a, b, (((ca,), (cb,)), ((), ())), preferred_element_type=F32, **kw)


@functools.partial(jax.custom_vjp, nondiff_argnums=(2, 3))
def bdot(a, b, ca, cb):
    return _dg(a.astype(BF16), b.astype(BF16), ca, cb)


def _bdot_fwd(a, b, ca, cb):
    return bdot(a, b, ca, cb), (a, b)


def _bdot_bwd(ca, cb, res, g):
    a, b = res
    a16, b16, g16 = a.astype(BF16), b.astype(BF16), g.astype(BF16)
    if ca == 1:
        da = _dg(g16, b16, 1, 1 if cb == 0 else 0)
    else:
        da = _dg(b16, g16, 1 if cb == 0 else 0, 1)
    if cb == 0:
        db = _dg(a16, g16, 0 if ca == 1 else 1, 0)
    else:
        db = _dg(g16, a16, 0, 0 if ca == 1 else 1)
    return da, db


bdot.defvjp(_bdot_fwd, _bdot_bwd)


def hdot(a, b, ca=1, cb=0):
    return _dg(a, b, ca, cb, precision=lax.Precision.HIGHEST)


def _row_specs(rows, exs, globs, ts):
    specs = [pl.BlockSpec((1, ts, w), lambda b, s, j=j: (b, s, j)) for (_, j, w) in rows]
    specs += [pl.BlockSpec((1, 1, e.shape[-1]), lambda b, s: (b, 0, 0)) for e in exs]
    specs += [pl.BlockSpec((1, g.shape[-1]), lambda b, s: (0, 0)) for g in globs]
    return specs


def _store_pieces(o_ref, pieces, widths):
    off = 0
    for p, w in zip(pieces, widths):
        o_ref[0, :, off:off + w] = p.astype(o_ref.dtype)
        off += w


def _load_pieces(c_ref, widths):
    out, off = [], 0
    for w in widths:
        out.append(c_ref[0, :, off:off + w].astype(F32))
        off += w
    return out


def rowwise(name, f, rows, exs, globs, outs, *, ts, accs=()):
    bsz, s = rows[0][0].shape[:2]
    ts = min(ts, s)
    n_r, n_e, n_g, n_o = len(rows), len(exs), len(globs), len(outs)

    def body(*refs):
        rv = [r[0].astype(F32) for r in refs[:n_r]]
        ev = [e[0] for e in refs[n_r:n_r + n_e]]
        gv = [g[...] for g in refs[n_r + n_e:n_r + n_e + n_g]]
        o_refs = refs[n_r + n_e + n_g:n_r + n_e + n_g + n_o]
        a_refs = refs[n_r + n_e + n_g + n_o:]
        pieces, sums = f(rv, ev, gv)
        idx = 0
        for o_ref, (_, ws) in zip(o_refs, outs):
            _store_pieces(o_ref, pieces[idx:idx + len(ws)], ws)
            idx += len(ws)
        if accs:
            @pl.when((pl.program_id(0) == 0) & (pl.program_id(1) == 0))
            def _():
                for a_ref in a_refs:
                    a_ref[...] = jnp.zeros_like(a_ref)
            for a_ref, val in zip(a_refs, sums):
                a_ref[...] += val

    out_specs = [pl.BlockSpec((1, ts, sum(ws)), lambda b, s: (b, s, 0)) for (_, ws) in outs]
    out_specs += [pl.BlockSpec((1, w), lambda b, s: (0, 0)) for w in accs]
    out_shape = [jax.ShapeDtypeStruct((bsz, s, sum(ws)), dt) for (dt, ws) in outs]
    out_shape += [jax.ShapeDtypeStruct((1, w), F32) for w in accs]
    return pl.pallas_call(
        body, name=name, grid=(bsz, s // ts),
        in_specs=_row_specs(rows, exs, globs, ts), out_specs=out_specs, out_shape=out_shape,
        compiler_params=_params("arbitrary", "arbitrary"),
    )(*[r[0] for r in rows], *exs, *globs)


def rowwise_bwd(name, f, rows, exs, globs, cts, d_groups, *, ts, n_diff, unit_ct=0):
    bsz, s = rows[0][0].shape[:2]
    ts = min(ts, s)
    n_r, n_e, n_g, n_c = len(rows), len(exs), len(globs), len(cts)
    n_d = len(d_groups)

    def body(*refs):
        rv = [r[0].astype(F32) for r in refs[:n_r]]
        ev = [e[0] for e in refs[n_r:n_r + n_e]]
        gv = [g[...] for g in refs[n_r + n_e:n_r + n_e + n_g]]
        base = n_r + n_e + n_g
        c_refs = refs[base:base + n_c]
        d_refs = refs[base + n_c:base + n_c + n_d]
        de_refs = refs[base + n_c + n_d:base + n_c + n_d + n_e]
        dg_refs = refs[base + n_c + n_d + n_e:]
        fixed = rv[n_diff:]
        out, vjp = jax.vjp(lambda r, e, g: f(r + fixed, e, g), rv[:n_diff], ev, gv)
        ct = []
        for c_ref, (_, ws) in zip(c_refs, cts):
            ct += _load_pieces(c_ref, ws)
        ct += [jnp.ones_like(o) for o in out[len(ct):]]
        assert len(ct) == len(out) and len(out) - unit_ct == sum(len(ws) for _, ws in cts), name
        d_r, d_e, d_g = vjp(ct)
        for d_ref, (_, idxs) in zip(d_refs, d_groups):
            _store_pieces(d_ref, [d_r[i] for i in idxs], [rows[i][2] for i in idxs])
        first_s = pl.program_id(1) == 0
        if n_e:
            @pl.when(first_s)
            def _():
                for r in de_refs:
                    r[...] = jnp.zeros_like(r)
            for r, val in zip(de_refs, d_e):
                r[0] += val
        if n_g:
            @pl.when(first_s & (pl.program_id(0) == 0))
            def _():
                for r in dg_refs:
                    r[...] = jnp.zeros_like(r)
            for r, val in zip(dg_refs, d_g):
                r[...] += val

    in_specs = _row_specs(rows, exs, globs, ts)
    in_specs += [pl.BlockSpec((1, ts, sum(ws)), lambda b, s: (b, s, 0)) for (_, ws) in cts]
    out_specs = [pl.BlockSpec((1, ts, sum(rows[i][2] for i in idxs)), lambda b, s: (b, s, 0)) for (_, idxs) in d_groups]
    out_specs += [pl.BlockSpec((1, 1, e.shape[-1]), lambda b, s: (b, 0, 0)) for e in exs]
    out_specs += [pl.BlockSpec((1, g.shape[-1]), lambda b, s: (0, 0)) for g in globs]
    out_shape = [jax.ShapeDtypeStruct((bsz, s, sum(rows[i][2] for i in idxs)), dt) for (dt, idxs) in d_groups]
    out_shape += [jax.ShapeDtypeStruct(e.shape, F32) for e in exs]
    out_shape += [jax.ShapeDtypeStruct(g.shape, F32) for g in globs]
    res = pl.pallas_call(
        body, name=name, grid=(bsz, s // ts),
        in_specs=in_specs, out_specs=out_specs, out_shape=out_shape,
        compiler_params=_params("arbitrary", "arbitrary"),
    )(*[r[0] for r in rows], *exs, *globs, *[c[0] for c in cts])
    return res[:n_d], res[n_d:n_d + n_e], res[n_d + n_e:]


def _full(a):
    return (a, 0, a.shape[-1])


def _view(a, col, w):
    assert col % w == 0
    return (a, col // w, w)


def _layer_norm(z, g, b):
    mu = jnp.mean(z, -1, keepdims=True)
    var = jnp.mean(jnp.square(z - mu), -1, keepdims=True)
    return (z - mu) * lax.rsqrt(var + LN_EPS) * g + b


def _rms_norm(z, g):
    ms = jnp.mean(jnp.square(z), -1, keepdims=True)
    return z * lax.rsqrt(ms + RMS_EPS) * g


def f_mod(rv, ev, gv):
    (x,), (scale, shift) = rv, ev
    return [x * (1.0 + scale) + shift]


def f_mod_with_x(rv, ev, gv):
    return f_mod(rv, ev, gv) + [rv[0]]


def f_ln_mod(rv, ev, gv):
    (x, y), (gate, scale, shift), (g, b) = rv, ev, gv
    xn = _layer_norm(ALPHA * x + (1.0 + gate) * y, g, b)
    return [xn, xn * (1.0 + scale) + shift]


def f_ln_loss(rv, ev, gv):
    (x, y, target), (gate,), (g, b) = rv, ev, gv
    xn = _layer_norm(ALPHA * x + (1.0 + gate) * y, g, b)
    return [0.5 * jnp.mean(jnp.square(xn - target), -1, keepdims=True)]


def f_swiglu(rv, ev, gv):
    gate, up = rv
    return [jax.nn.silu(gate) * up]


def _head_spread(width):
    r2 = QK_ROPE // 2
    j = lax.broadcasted_iota(jnp.int32, (LANES, width), 0)
    col = lax.broadcasted_iota(jnp.int32, (LANES, width), 1) % r2
    return (j == col).astype(F32), (j == col + r2).astype(F32)


def f_mla_mid(rv, ev, gv):
    (q_lat, kv_lat, kr, cos, sin), (q_g, kv_g) = rv, gv
    e1, e2 = _head_spread(cos.shape[-1])
    k1, k2 = hdot(kr, e1), hdot(kr, e2)
    return [_rms_norm(q_lat, q_g), _rms_norm(kv_lat, kv_g), k1 * cos - k2 * sin, k1 * sin + k2 * cos]


def f_mla_post(rv, ev, gv):
    q_nope, q1, q2, kv, cos, sin = rv
    return [q_nope, q1 * cos - q2 * sin, q1 * sin + q2 * cos, kv]


def f_hg_pre(rv, ev, gv):
    (q, fx), (lb,) = rv[:2], gv
    f = lb + (1.0 - lb) * jax.nn.sigmoid(fx)
    return [jax.nn.silu(q), jnp.log(f), 1.0 - f]


def f_hg_pre_with_iv(rv, ev, gv):
    return f_hg_pre(rv, ev, gv) + [rv[2], rv[3]]


def f_hg_post(rv, ev, gv):
    nh = len(rv) // 2
    (gn,) = gv
    return [_rms_norm(rv[h], gn) * jax.nn.silu(rv[nh + h]) for h in range(nh)]


def rope_tables(positions):
    bsz, s = positions.shape
    r2 = QK_ROPE // 2
    width = MLA_HEADS * r2
    inv = (ROPE_THETA ** (-np.arange(0, QK_ROPE, 2, dtype=np.float32) / QK_ROPE)).astype(np.float32)
    inv = jnp.asarray(np.tile(inv, MLA_HEADS)[None, :])
    ts = min(ROW_TILE, s)

    def body(p_ref, inv_ref, cos_ref, sin_ref):
        ang = p_ref[0].astype(F32) * inv_ref[...]
        cos_ref[0] = jnp.cos(ang)
        sin_ref[0] = jnp.sin(ang)

    spec = pl.BlockSpec((1, ts, width), lambda b, s: (b, s, 0))
    return pl.pallas_call(
        body, name="rope_tables", grid=(bsz, s // ts),
        in_specs=[pl.BlockSpec((1, ts, 1), lambda b, s: (b, s, 0)), pl.BlockSpec((1, width), lambda b, s: (0, 0))],
        out_specs=[spec, spec], out_shape=[jax.ShapeDtypeStruct((bsz, s, width), F32)] * 2,
        compiler_params=_params("arbitrary", "arbitrary"),
    )(positions[:, :, None], inv)


def _attn_probs(q, k, row0):
    scale = (QK_NOPE + QK_ROPE) ** -0.5
    s = _dg(q, k, 1, 1) * scale
    rows = row0 + lax.broadcasted_iota(jnp.int32, s.shape, 0)
    cols = lax.broadcasted_iota(jnp.int32, s.shape, 1)
    s = jnp.where(cols <= rows, s, jnp.finfo(F32).min)
    e = jnp.exp(s - jnp.max(s, -1, keepdims=True))
    return e / jnp.sum(e, -1, keepdims=True), scale


def attn_fwd(q, k, v):
    bsz, h, s, dq = q.shape
    dv = v.shape[-1]
    tq = min(ATTN_TQ, s)

    def body(q_ref, k_ref, v_ref, o_ref):
        p, _ = _attn_probs(q_ref[0, 0], k_ref[0, 0], pl.program_id(2) * tq)
        o_ref[0, 0] = _dg(p.astype(BF16), v_ref[0, 0], 1, 0).astype(o_ref.dtype)

    return pl.pallas_call(
        body, name="attn_fwd", grid=(bsz, h, s // tq),
        in_specs=[pl.BlockSpec((1, 1, tq, dq), lambda b, h, i: (b, h, i, 0)),
                  pl.BlockSpec((1, 1, s, dq), lambda b, h, i: (b, h, 0, 0)),
                  pl.BlockSpec((1, 1, s, dv), lambda b, h, i: (b, h, 0, 0))],
        out_specs=pl.BlockSpec((1, 1, tq, dv), lambda b, h, i: (b, h, i, 0)),
        out_shape=jax.ShapeDtypeStruct((bsz, h, s, dv), BF16),
        compiler_params=_params("parallel", "parallel", "arbitrary"),
    )(q, k, v)


def attn_bwd(q, k, v, do):
    bsz, h, s, dq = q.shape
    dv = v.shape[-1]
    tq = min(ATTN_TQ, s)

    def body(q_ref, k_ref, v_ref, do_ref, dq_ref, dk_ref, dv_ref):
        @pl.when(pl.program_id(2) == 0)
        def _():
            dk_ref[...] = jnp.zeros_like(dk_ref)
            dv_ref[...] = jnp.zeros_like(dv_ref)

        qv, kv, vv = q_ref[0, 0], k_ref[0, 0], v_ref[0, 0]
        p, scale = _attn_probs(qv, kv, pl.program_id(2) * tq)
        do16 = do_ref[0, 0].astype(BF16)
        p16 = p.astype(BF16)
        dv_ref[0, 0] += _dg(p16, do16, 0, 0)
        dp = _dg(do16, vv, 1, 1)
        ds = (p * (dp - jnp.sum(dp * p, -1, keepdims=True)) * scale).astype(BF16)
        dq_ref[0, 0] = _dg(ds, kv, 1, 0)
        dk_ref[0, 0] += _dg(ds, qv, 0, 0)

    return pl.pallas_call(
        body, name="attn_bwd", grid=(bsz, h, s // tq),
        in_specs=[pl.BlockSpec((1, 1, tq, dq), lambda b, h, i: (b, h, i, 0)),
                  pl.BlockSpec((1, 1, s, dq), lambda b, h, i: (b, h, 0, 0)),
                  pl.BlockSpec((1, 1, s, dv), lambda b, h, i: (b, h, 0, 0)),
                  pl.BlockSpec((1, 1, tq, dv), lambda b, h, i: (b, h, i, 0))],
        out_specs=[pl.BlockSpec((1, 1, tq, dq), lambda b, h, i: (b, h, i, 0)),
                   pl.BlockSpec((1, 1, s, dq), lambda b, h, i: (b, h, 0, 0)),
                   pl.BlockSpec((1, 1, s, dv), lambda b, h, i: (b, h, 0, 0))],
        out_shape=[jax.ShapeDtypeStruct((bsz, h, s, dq), F32), jax.ShapeDtypeStruct((bsz, h, s, dq), F32),
                   jax.ShapeDtypeStruct((bsz, h, s, dv), F32)],
        compiler_params=_params("parallel", "parallel", "arbitrary"),
    )(q, k, v, do)


def _hgrn_tables(c):
    levels = c.bit_length() - 1
    assert 1 << levels == c
    r = np.arange(c)
    prefix, sign, mask = [r[:, None] >= r[None, :]], [], []
    for l in range(levels):
        ref = ((r >> (l + 1)) << (l + 1)) + (1 << l) - 1
        lower = ((r >> l) & 1) == 1
        prefix.append(r[None, :] <= ref[:, None])
        sign.append(np.broadcast_to(np.where(lower, 1.0, -1.0)[:, None], (c, LANES)))
        mask.append((((r[:, None] ^ r[None, :]) >> l) == 1) & lower[:, None])
    return (jnp.asarray(np.concatenate(prefix, 0), BF16), jnp.asarray(np.stack(sign), F32),
            jnp.asarray(np.stack(mask), F32))


def _const_specs(tables):
    return [pl.BlockSpec(t.shape, lambda b, h, i, nd=t.ndim: (0,) * nd) for t in tables]


def _split3(x):
    hi = x.astype(BF16)
    rest = x - hi.astype(F32)
    mid = rest.astype(BF16)
    return hi, mid, (rest - mid.astype(F32)).astype(BF16)


@functools.partial(jax.custom_vjp, nondiff_argnums=(2,))
def prefix_sums(p, g, n):
    k = g.shape[1]
    r = _dg(p, jnp.concatenate(_split3(g), axis=1), 1, 0)
    r = r[:, :k] + r[:, k:2 * k] + r[:, 2 * k:]
    c = r.shape[0] // n
    return tuple(r[i * c:(i + 1) * c] for i in range(n))


def _prefix_fwd(p, g, n):
    return prefix_sums(p, g, n), p


def _prefix_bwd(n, p, ct):
    ct = jnp.concatenate(ct, axis=0)
    k = ct.shape[1]
    r = _dg(p, jnp.concatenate(_split3(ct), axis=1), 0, 0)
    return jnp.zeros_like(p), r[:, :k] + r[:, k:2 * k] + r[:, 2 * k:]


prefix_sums.defvjp(_prefix_fwd, _prefix_bwd)


def _dot3_raw(a, b, ca, cb):
    ah = a.astype(BF16)
    al = (a - ah.astype(F32)).astype(BF16)
    bh = b.astype(BF16)
    bl = (b - bh.astype(F32)).astype(BF16)
    if (ca == 1 and a.shape[1] % LANES) or (cb == 1 and b.shape[1] % LANES):
        return _dg(ah, bh, ca, cb) + _dg(ah, bl, ca, cb) + _dg(al, bh, ca, cb)
    return _dg(jnp.concatenate([ah, ah, al], axis=ca), jnp.concatenate([bh, bl, bh], axis=cb), ca, cb)


@functools.partial(jax.custom_vjp, nondiff_argnums=(2, 3))
def dot3(a, b, ca, cb):
    return _dot3_raw(a, b, ca, cb)


def _dot3_fwd(a, b, ca, cb):
    return _dot3_raw(a, b, ca, cb), (a, b)


def _dot3_bwd(ca, cb, res, g):
    a, b = res
    if ca == 1:
        da = _dot3_raw(g, b, 1, 1 if cb == 0 else 0)
    else:
        da = _dot3_raw(b, g, 1 if cb == 0 else 0, 1)
    if cb == 0:
        db = _dot3_raw(a, g, 0 if ca == 1 else 1, 0)
    else:
        db = _dot3_raw(g, a, 0, 0 if ca == 1 else 1)
    return da, db


dot3.defvjp(_dot3_fwd, _dot3_bwd)


def _hgrn_chunk(q, g, k, v, st0, prefix, sign, mask):
    levels = len(mask)
    pre = prefix_sums(prefix, g, levels + 1)
    b = pre[0]
    o = bdot(q * jnp.exp(b), st0, 1, 1)
    att = None
    for l in range(levels):
        e = jnp.exp((b - pre[l + 1]) * sign[l])
        a = dot3(q * e, k * e, 1, 1) * mask[l]
        att = a if att is None else att + a
    o = o + bdot(att, v, 1, 0) + jnp.sum(q * k, -1, keepdims=True) * v
    total = jnp.sum(g, 0, keepdims=True)
    st1 = st0 * jnp.exp(total) + bdot(v, k * jnp.exp(total - b), 0, 0)
    return o, st1


def hgrn_fwd(qs, logf, kk, proj):
    bsz, s, hk = qs.shape
    kd = HGRN_EXPAND
    c = min(HGRN_CHUNK, s)
    nh, nc = hk // kd, s // c
    hp = math.gcd(nh, HGRN_HEADS_PER_STEP)
    tables = _hgrn_tables(c)
    levels = tables[2].shape[0]

    def body(q_ref, g_ref, k_ref, v_ref, p_ref, sg_ref, mk_ref, o_ref, st_ref, state):
        @pl.when(pl.program_id(2) == 0)
        def _():
            state[...] = jnp.zeros_like(state)

        prefix, sign, mask = p_ref[...], [sg_ref[l] for l in range(levels)], [mk_ref[l] for l in range(levels)]
        for j in range(hp):
            cols = slice(j * kd, (j + 1) * kd)
            st0 = state[j]
            st_ref[0, j, 0] = st0
            o, st1 = _hgrn_chunk(q_ref[0, :, cols], g_ref[0, :, cols], k_ref[0, :, cols], v_ref[0, :, cols], st0,
                                 prefix, sign, mask)
            o_ref[0, :, cols] = o
            state[j] = st1

    blk = pl.BlockSpec((1, c, hp * kd), lambda b, h, i: (b, i, h))
    return pl.pallas_call(
        body, name="hgrn_fwd", grid=(bsz, nh // hp, nc),
        in_specs=[blk, blk, blk, pl.BlockSpec((1, c, hp * kd), lambda b, h, i: (b, i, 2 * (nh // hp) + h))] + _const_specs(tables),
        out_specs=[blk, pl.BlockSpec((1, hp, 1, kd, kd), lambda b, h, i: (b, h, i, 0, 0))],
        out_shape=[jax.ShapeDtypeStruct((bsz, s, hk), F32), jax.ShapeDtypeStruct((bsz, nh, nc, kd, kd), F32)],
        scratch_shapes=[pltpu.VMEM((hp, kd, kd), F32)],
        compiler_params=_params("parallel", "parallel", "arbitrary"),
    )(qs, logf, kk, proj, *tables)


def hgrn_bwd(qs, logf, kk, proj, states, do):
    bsz, s, hk = qs.shape
    kd = HGRN_EXPAND
    c = min(HGRN_CHUNK, s)
    nh, nc = hk // kd, s // c
    hp = math.gcd(nh, HGRN_HEADS_PER_STEP)
    tables = _hgrn_tables(c)
    levels = tables[2].shape[0]

    def body(q_ref, g_ref, k_ref, v_ref, st_ref, do_ref, p_ref, sg_ref, mk_ref, dq_ref, dg_ref, dk_ref, dv_ref, dstate):
        @pl.when(pl.program_id(2) == 0)
        def _():
            dstate[...] = jnp.zeros_like(dstate)

        prefix, sign, mask = p_ref[...], [sg_ref[l] for l in range(levels)], [mk_ref[l] for l in range(levels)]
        for j in range(hp):
            cols = slice(j * kd, (j + 1) * kd)
            _, vjp = jax.vjp(lambda q, g, k, v, st: _hgrn_chunk(q, g, k, v, st, prefix, sign, mask),
                             q_ref[0, :, cols], g_ref[0, :, cols], k_ref[0, :, cols], v_ref[0, :, cols], st_ref[0, j, 0])
            dq, dg, dk, dv, dst = vjp((do_ref[0, :, cols], dstate[j]))
            dq_ref[0, :, cols] = dq
            dg_ref[0, :, cols] = dg
            dk_ref[0, :, cols] = dk
            dv_ref[0, :, cols] = dv
            dstate[j] = dst

    blk = pl.BlockSpec((1, c, hp * kd), lambda b, h, i: (b, nc - 1 - i, h))
    shape = jax.ShapeDtypeStruct((bsz, s, hk), F32)
    return pl.pallas_call(
        body, name="hgrn_bwd", grid=(bsz, nh // hp, nc),
        in_specs=[blk, blk, blk, pl.BlockSpec((1, c, hp * kd), lambda b, h, i: (b, nc - 1 - i, 2 * (nh // hp) + h)),
                  pl.BlockSpec((1, hp, 1, kd, kd), lambda b, h, i: (b, h, nc - 1 - i, 0, 0)), blk] + _const_specs(tables),
        out_specs=[blk] * 4, out_shape=[shape] * 4,
        scratch_shapes=[pltpu.VMEM((hp, kd, kd), F32)],
        compiler_params=_params("parallel", "parallel", "arbitrary"),
    )(qs, logf, kk, proj, states, do, *tables)


def cast_bf16(name, w):
    blk = pl.BlockSpec((1,) + w.shape[1:], lambda l: (l, 0, 0))

    def body(w_ref, o_ref):
        o_ref[...] = w_ref[...].astype(BF16)

    return pl.pallas_call(body, name=name, grid=(w.shape[0],), in_specs=[blk], out_specs=blk,
                          out_shape=jax.ShapeDtypeStruct(w.shape, BF16), compiler_params=_params("arbitrary"))(w)


def _lower_bounds(rows):
    m = functools.reduce(jnp.maximum, rows)
    e = [jnp.exp(r - m) for r in rows]
    z = functools.reduce(lambda a, b: a + b, e)
    soft = [x / z for x in e]
    out, run = [], jnp.zeros_like(rows[0])
    for sft in soft:
        run = run + sft
        out.append(run - soft[0])
    return out


def lower_bounds(lb):
    n = lb.shape[0]

    def body(lb_ref, o_ref):
        for i, r in enumerate(_lower_bounds([lb_ref[i:i + 1, :] for i in range(n)])):
            o_ref[i:i + 1, :] = r

    return pl.pallas_call(body, name="lower_bounds", out_shape=jax.ShapeDtypeStruct(lb.shape, F32),
                          compiler_params=_params())(lb)


def ada_fwd(c_all, ada_w, ada_b):
    nl, ns, d, cols = ada_w.shape
    n_ex = c_all.shape[0]

    def body(c_ref, w_ref, b_ref, o_ref):
        a = jax.nn.silu(c_ref[...]).astype(BF16)
        o_ref[0] = _dg(a, w_ref[0].astype(BF16), 1, 0) + b_ref[0]

    return pl.pallas_call(
        body, name="ada_fwd", grid=(nl * ns,),
        in_specs=[pl.BlockSpec((n_ex, d), lambda i: (0, 0)), pl.BlockSpec((1, d, cols), lambda i: (i, 0, 0)),
                  pl.BlockSpec((1, 1, cols), lambda i: (i, 0, 0))],
        out_specs=pl.BlockSpec((1, n_ex, cols), lambda i: (i, 0, 0)),
        out_shape=jax.ShapeDtypeStruct((nl * ns, n_ex, cols), F32), compiler_params=_params("arbitrary"),
    )(c_all, ada_w.reshape(nl * ns, d, cols), ada_b.reshape(nl * ns, 1, cols))


def ada_bwd(c_all, dmod):
    n, n_ex, cols = dmod.shape
    d = c_all.shape[1]

    def body(c_ref, g_ref, dw_ref, db_ref):
        a = jax.nn.silu(c_ref[...]).astype(BF16)
        g = g_ref[0]
        dw_ref[0] = _dg(a, g.astype(BF16), 0, 0)
        db_ref[0] = jnp.sum(g, 0, keepdims=True)

    return pl.pallas_call(
        body, name="ada_bwd", grid=(n,),
        in_specs=[pl.BlockSpec((n_ex, d), lambda i: (0, 0)), pl.BlockSpec((1, n_ex, cols), lambda i: (i, 0, 0))],
        out_specs=[pl.BlockSpec((1, d, cols), lambda i: (i, 0, 0)), pl.BlockSpec((1, 1, cols), lambda i: (i, 0, 0))],
        out_shape=[jax.ShapeDtypeStruct((n, d, cols), F32), jax.ShapeDtypeStruct((n, 1, cols), F32)],
        compiler_params=_params("arbitrary"),
    )(c_all, dmod)


def _adam_math(g, w, m, v):
    m = ADAM_B1 * m + (1.0 - ADAM_B1) * g
    v = ADAM_B2 * v + (1.0 - ADAM_B2) * jnp.square(g)
    m_hat = m / (1.0 - ADAM_B1 ** ADAM_STEP)
    v_hat = v / (1.0 - ADAM_B2 ** ADAM_STEP)
    delta = -ADAM_LR * (m_hat / (jnp.sqrt(v_hat) + ADAM_EPS) + ADAM_WD * w)
    return delta, m, v


def adam(name, gstack, w, m, v):
    shape = w.shape
    n, cols = gstack.shape[0], shape[-1]
    rows = math.prod(shape[:-1])
    tr = _pick_rows(rows, max(8, (2 * 1024 * 1024) // (4 * cols * n)))

    def body(g_ref, w_ref, m_ref, v_ref, go_ref, d_ref, mo_ref, vo_ref):
        g = g_ref[0]
        for i in range(1, n):
            g = g + g_ref[i]
        delta, m1, v1 = _adam_math(g, w_ref[...], m_ref[...], v_ref[...])
        go_ref[...] = g
        d_ref[...] = delta
        mo_ref[...] = m1
        vo_ref[...] = v1

    blk = pl.BlockSpec((tr, cols), lambda i: (i, 0))
    out = pl.pallas_call(
        body, name=name, grid=(rows // tr,),
        in_specs=[pl.BlockSpec((n, tr, cols), lambda i: (0, i, 0)), blk, blk, blk],
        out_specs=[blk] * 4, out_shape=[jax.ShapeDtypeStruct((rows, cols), F32)] * 4,
        compiler_params=_params("arbitrary"),
    )(gstack.reshape(n, rows, cols), w.reshape(rows, cols), m.reshape(rows, cols), v.reshape(rows, cols))
    return [o.reshape(shape) for o in out]


def adam_lb(gstack, lb, m, v):
    n, nl = gstack.shape[0], lb.shape[0]

    def body(g_ref, w_ref, m_ref, v_ref, go_ref, d_ref, mo_ref, vo_ref):
        rows = [w_ref[i:i + 1, :] for i in range(nl)]
        ct = []
        for i in range(nl):
            g = g_ref[0, i:i + 1, :]
            for j in range(1, n):
                g = g + g_ref[j, i:i + 1, :]
            ct.append(g)
        _, vjp = jax.vjp(_lower_bounds, rows)
        (grads,) = vjp(ct)
        for i in range(nl):
            delta, m1, v1 = _adam_math(grads[i], rows[i], m_ref[i:i + 1, :], v_ref[i:i + 1, :])
            go_ref[i:i + 1, :] = grads[i]
            d_ref[i:i + 1, :] = delta
            mo_ref[i:i + 1, :] = m1
            vo_ref[i:i + 1, :] = v1

    return pl.pallas_call(body, name="adam_hgrn_lb", out_shape=[jax.ShapeDtypeStruct(lb.shape, F32)] * 4,
                          compiler_params=_params())(gstack, lb, m, v)


def exchange(name, items):
    n = len(items)
    out_shape = []
    for src, mode in items:
        out_shape.append(jax.ShapeDtypeStruct(((N_DEV,) + src.shape) if mode == "gather" else src.shape, src.dtype))

    def body(*refs):
        srcs, outs = refs[:n], refs[n:2 * n]
        send_sems, recv_sems, local_sems = refs[2 * n:]
        x, y, c = lax.axis_index("x"), lax.axis_index("y"), lax.axis_index("c")
        me = 4 * x + 2 * y + c
        copies = []
        for i, (_, mode) in enumerate(items):
            mine = srcs[i] if mode == "gather" else srcs[i].at[me]
            cp = pltpu.make_async_copy(mine, outs[i].at[me], local_sems.at[i])
            cp.start()
            copies.append(cp)
            for p in range(1, N_DEV):
                px = 1 - x if p & 4 else x
                py = 1 - y if p & 2 else y
                pc = 1 - c if p & 1 else c
                part = srcs[i] if mode == "gather" else srcs[i].at[4 * px + 2 * py + pc]
                cp = pltpu.make_async_remote_copy(
                    src_ref=part, dst_ref=outs[i].at[me], send_sem=send_sems.at[i, p - 1], recv_sem=recv_sems.at[i, p - 1],
                    device_id=(px, py, pc), device_id_type=pl.DeviceIdType.MESH)
                cp.start()
                copies.append(cp)
        for cp in copies:
            cp.wait()

    any_spec = pl.BlockSpec(memory_space=pl.ANY)
    return pl.pallas_call(
        body, name=name, in_specs=[any_spec] * n, out_specs=[any_spec] * n, out_shape=out_shape,
        scratch_shapes=[pltpu.SemaphoreType.DMA((n, N_DEV - 1)), pltpu.SemaphoreType.DMA((n, N_DEV - 1)),
                        pltpu.SemaphoreType.DMA((n,))],
    )(*[src for src, _ in items])


def _cols_from_gather(g):
    _, l, k, n = g.shape
    return g.transpose(1, 2, 0, 3).reshape(l, k, N_DEV * n)


def _rows_from_gather(g):
    _, l, r, n = g.shape
    return g.transpose(1, 0, 2, 3).reshape(l, N_DEV * r, n)


def _cols_to_slabs(w):
    l, k, n = w.shape
    return w.reshape(l, k, N_DEV, n // N_DEV).transpose(2, 0, 1, 3)


def _rows_to_slabs(w):
    l, k, n = w.shape
    return w.reshape(l, N_DEV, k // N_DEV, n).transpose(1, 0, 2, 3)


def _kr_pad():
    return LANES - QK_ROPE


def _w_in_internal(w):
    return jnp.pad(w, ((0, 0), (0, 0), (0, _kr_pad())))


def _qb_internal(w, inverse=False):
    h, n, r2 = MLA_HEADS, QK_NOPE, QK_ROPE // 2
    lead = w.shape[:-1]
    if not inverse:
        w = w.reshape(lead + (h, n + 2 * r2))
        parts = [w[..., :n], w[..., n:n + r2], w[..., n + r2:]]
        return jnp.concatenate([p.reshape(lead + (-1,)) for p in parts], axis=-1)
    parts = [w[..., :h * n].reshape(lead + (h, n)), w[..., h * n:h * (n + r2)].reshape(lead + (h, r2)),
             w[..., h * (n + r2):].reshape(lead + (h, r2))]
    return jnp.concatenate(parts, axis=-1).reshape(lead + (-1,))


def _kvb_internal(w, inverse=False):
    h, n, vd = MLA_HEADS, QK_NOPE, V_HEAD
    lead = w.shape[:-1]
    if not inverse:
        w = w.reshape(lead + (h, n + vd))
        return jnp.concatenate([w[..., :n].reshape(lead + (-1,)), w[..., n:].reshape(lead + (-1,))], axis=-1)
    parts = [w[..., :h * n].reshape(lead + (h, n)), w[..., h * n:].reshape(lead + (h, vd))]
    return jnp.concatenate(parts, axis=-1).reshape(lead + (-1,))


def _to_heads(parts):
    bsz, s = parts[0].shape[:2]
    t = jnp.concatenate([p.reshape(bsz, s, MLA_HEADS, -1) for p in parts], axis=-1)
    return t.transpose(0, 2, 1, 3)


def _from_heads(t, widths):
    bsz, h, s, _ = t.shape
    t = t.transpose(0, 2, 1, 3)
    out, off = [], 0
    for w in widths:
        out.append(t[..., off:off + w].reshape(bsz, s, h * w))
        off += w
    return out


def _mla_forward(h, w, tabs):
    cos, sin = tabs
    hn, r2 = MLA_HEADS * QK_NOPE, MLA_HEADS * (QK_ROPE // 2)
    proj = mm3("mla_proj", h, w['w_in'])
    qn, kvn, krt = rowwise(
        "mla_mid", lambda rv, ev, gv: (f_mla_mid(rv, ev, gv), []),
        [_view(proj, 0, Q_LORA), _view(proj, Q_LORA, KV_LORA), _view(proj, Q_LORA + KV_LORA, LANES), _full(cos), _full(sin)],
        [], [w['q_norm'], w['kv_norm']], [(BF16, [Q_LORA]), (BF16, [KV_LORA]), (BF16, [r2, r2])], ts=ROW_TILE)
    q = mm3("mla_q", qn, w['w_qb'])
    kv = mm3("mla_kv", kvn, w['w_kvb'])
    qb, kvb = rowwise(
        "mla_post", lambda rv, ev, gv: (f_mla_post(rv, ev, gv), []),
        [_view(q, 0, hn), _view(q, hn, r2), _view(q, hn + r2, r2), _full(kv), _full(cos), _full(sin)],
        [], [], [(BF16, [hn, r2, r2]), (BF16, [kv.shape[-1]])], ts=ROW_TILE)
    qh = _to_heads([qb[..., :hn], qb[..., hn:hn + r2], qb[..., hn + r2:]])
    kh = _to_heads([kvb[..., :hn], krt[..., :r2], krt[..., r2:]])
    vh = _to_heads([kvb[..., hn:]])
    oh = attn_fwd(qh, kh, vh)
    (o,) = _from_heads(oh, [V_HEAD])
    y = mm3("mla_out", o, w['w_o'])
    return y, dict(h=h, proj=proj, qn=qn, kvn=kvn, q=q, kv=kv, qh=qh, kh=kh, vh=vh, o=o)


def _mla_backward(dy, sv, w, tabs):
    cos, sin = tabs
    hn, r2 = MLA_HEADS * QK_NOPE, MLA_HEADS * (QK_ROPE // 2)
    g = {}
    g['w_o'] = wgrad("mla_out_dw", sv['o'], dy)
    do = mm3("mla_out_dx", dy, w['w_o'], tb=True)
    doh = _to_heads([do])
    dqh, dkh, dvh = attn_bwd(sv['qh'], sv['kh'], sv['vh'], doh)
    dqb = jnp.concatenate(_from_heads(dqh, [QK_NOPE, QK_ROPE // 2, QK_ROPE // 2]), axis=-1)
    dkn, dk1, dk2 = _from_heads(dkh, [QK_NOPE, QK_ROPE // 2, QK_ROPE // 2])
    (dvv,) = _from_heads(dvh, [V_HEAD])
    dkvb = jnp.concatenate([dkn, dvv], axis=-1)
    dkrt = jnp.concatenate([dk1, dk2], axis=-1)
    q, kv = sv['q'], sv['kv']
    (dq, dkv), _, _ = rowwise_bwd(
        "mla_post_bwd", f_mla_post,
        [_view(q, 0, hn), _view(q, hn, r2), _view(q, hn + r2, r2), _full(kv), _full(cos), _full(sin)], [], [],
        [(dqb, [hn, r2, r2]), (dkvb, [kv.shape[-1]])], [(BF16, [0, 1, 2]), (BF16, [3])], ts=ROW_TILE, n_diff=4)
    g['w_qb'] = wgrad("mla_q_dw", sv['qn'], dq)
    g['w_kvb'] = wgrad("mla_kv_dw", sv['kvn'], dkv)
    dqn = mm3("mla_q_dx", dq, w['w_qb'], tb=True)
    dkvn = mm3("mla_kv_dx", dkv, w['w_kvb'], tb=True)
    proj = sv['proj']
    (dproj,), _, (g['q_norm'], g['kv_norm']) = rowwise_bwd(
        "mla_mid_bwd", f_mla_mid,
        [_view(proj, 0, Q_LORA), _view(proj, Q_LORA, KV_LORA), _view(proj, Q_LORA + KV_LORA, LANES), _full(cos), _full(sin)],
        [], [w['q_norm'], w['kv_norm']], [(dqn, [Q_LORA]), (dkvn, [KV_LORA]), (dkrt, [r2, r2])],
        [(BF16, [0, 1, 2])], ts=ROW_TILE, n_diff=3)
    g['w_in'] = wgrad("mla_proj_dw", sv['h'], dproj)
    dh = mm3("mla_proj_dx", dproj, w['w_in'], tb=True)
    return dh, g


def _hgrn_views(proj):
    d = proj.shape[-1] // 4
    return [_view(proj, i * d, d) for i in range(4)]


def _head_views(a, col0, nh):
    return [_view(a, col0 + i * HGRN_EXPAND, HGRN_EXPAND) for i in range(nh)]


def _hgrn_forward(h, w):
    proj = mm3("hgrn_proj", h, w['w_in'])
    d = proj.shape[-1] // 4
    nh = d // HGRN_EXPAND
    vq, vf, _, _ = _hgrn_views(proj)
    qs, logf, kk = rowwise("hgrn_pre", lambda rv, ev, gv: (f_hg_pre(rv, ev, gv), []), [vq, vf], [], [w['lb']],
                           [(F32, [d]), (F32, [d]), (F32, [d])], ts=ROW_TILE)
    o, states = hgrn_fwd(qs, logf, kk, proj)
    (z,) = rowwise("hgrn_post", lambda rv, ev, gv: (f_hg_post(rv, ev, gv), []),
                   _head_views(o, 0, nh) + _head_views(proj, 3 * d, nh), [], [w['g_norm']],
                   [(BF16, [HGRN_EXPAND] * nh)], ts=ROW_TILE)
    y = mm3("hgrn_out", z, w['w_o'])
    return y, dict(h=h, proj=proj, qs=qs, logf=logf, kk=kk, o=o, states=states, z=z)


def _hgrn_backward(dy, sv, w):
    g = {}
    proj, o = sv['proj'], sv['o']
    d = proj.shape[-1] // 4
    nh = d // HGRN_EXPAND
    g['w_o'] = wgrad("hgrn_out_dw", sv['z'], dy)
    dz = mm3("hgrn_out_dx", dy, w['w_o'], tb=True)
    (do, dgate), _, (g['g_norm'],) = rowwise_bwd(
        "hgrn_post_bwd", f_hg_post, _head_views(o, 0, nh) + _head_views(proj, 3 * d, nh), [], [w['g_norm']],
        [(dz, [HGRN_EXPAND] * nh)], [(F32, list(range(nh))), (F32, list(range(nh, 2 * nh)))], ts=ROW_TILE, n_diff=2 * nh)
    dqs, dlogf, dkk, dv = hgrn_bwd(sv['qs'], sv['logf'], sv['kk'], proj, sv['states'], do)
    (dproj,), _, (g['lb'],) = rowwise_bwd(
        "hgrn_pre_bwd", f_hg_pre_with_iv, _hgrn_views(proj), [], [w['lb']],
        [(dqs, [d]), (dlogf, [d]), (dkk, [d]), (dv, [d]), (dgate, [d])], [(BF16, [0, 1, 2, 3])], ts=ROW_TILE // 2, n_diff=4)
    g['w_in'] = wgrad("hgrn_proj_dw", sv['h'], dproj)
    dh = mm3("hgrn_proj_dx", dproj, w['w_in'], tb=True)
    return dh, g


def _ffn_forward(h, w):
    u = mm3("ffn_in", h, w['w_in'])
    dff = u.shape[-1] // 2
    (a,) = rowwise("ffn_act", lambda rv, ev, gv: (f_swiglu(rv, ev, gv), []), [_view(u, 0, dff), _view(u, dff, dff)],
                   [], [], [(BF16, [dff])], ts=ROW_TILE)
    y = mm3("ffn_out", a, w['w_out'])
    return y, dict(h=h, u=u, a=a)


def _ffn_backward(dy, sv, w):
    g = {}
    u = sv['u']
    dff = u.shape[-1] // 2
    g['w_out'] = wgrad("ffn_out_dw", sv['a'], dy)
    da = mm3("ffn_out_dx", dy, w['w_out'], tb=True)
    (du,), _, _ = rowwise_bwd("ffn_act_bwd", f_swiglu, [_view(u, 0, dff), _view(u, dff, dff)], [], [],
                              [(da, [dff])], [(BF16, [0, 1])], ts=ROW_TILE // 2, n_diff=2)
    g['w_in'] = wgrad("ffn_in_dw", sv['h'], du)
    dh = mm3("ffn_in_dx", du, w['w_in'], tb=True)
    return dh, g


def kernel(x, c, positions, mla_w_in, mla_q_norm, mla_w_qb, mla_kv_norm, mla_w_kvb, mla_w_o, hgrn_lb, hgrn_w_in, hgrn_g_norm, hgrn_w_o, ffn_w_in, ffn_w_out, ada_w, ada_b, ln_g, ln_b, loss_target, m_mla_w_in, m_mla_q_norm, m_mla_w_qb, m_mla_kv_norm, m_mla_w_kvb, m_mla_w_o, m_hgrn_lb, m_hgrn_w_in, m_hgrn_g_norm, m_hgrn_w_o, m_ffn_w_in, m_ffn_w_out, m_ada_w, m_ada_b, m_ln_g, m_ln_b, v_mla_w_in, v_mla_q_norm, v_mla_w_qb, v_mla_kv_norm, v_mla_w_kvb, v_mla_w_o, v_hgrn_lb, v_hgrn_w_in, v_hgrn_g_norm, v_hgrn_w_o, v_ffn_w_in, v_ffn_w_out, v_ada_w, v_ada_b, v_ln_g, v_ln_b):
    W = dict(zip(WEIGHTS, (mla_w_in, mla_q_norm, mla_w_qb, mla_kv_norm, mla_w_kvb, mla_w_o, hgrn_lb, hgrn_w_in, hgrn_g_norm,
                           hgrn_w_o, ffn_w_in, ffn_w_out, ada_w, ada_b, ln_g, ln_b)))
    M1 = dict(zip(WEIGHTS, (m_mla_w_in, m_mla_q_norm, m_mla_w_qb, m_mla_kv_norm, m_mla_w_kvb, m_mla_w_o, m_hgrn_lb, m_hgrn_w_in,
                            m_hgrn_g_norm, m_hgrn_w_o, m_ffn_w_in, m_ffn_w_out, m_ada_w, m_ada_b, m_ln_g, m_ln_b)))
    M2 = dict(zip(WEIGHTS, (v_mla_w_in, v_mla_q_norm, v_mla_w_qb, v_mla_kv_norm, v_mla_w_kvb, v_mla_w_o, v_hgrn_lb, v_hgrn_w_in,
                            v_hgrn_g_norm, v_hgrn_w_o, v_ffn_w_in, v_ffn_w_out, v_ada_w, v_ada_b, v_ln_g, v_ln_b)))
    bsz, seq, d = x.shape
    depth, n_mla, n_hgrn = ffn_w_in.shape[0], mla_w_in.shape[0], hgrn_w_in.shape[0]
    n_sub = 2 * depth

    big = COL_SHARDED + ROW_SHARDED
    lower_shard = lower_bounds(hgrn_lb)
    items = [(cast_bf16("cast_" + n, W[n]), "gather") for n in big]
    items += [(lower_shard, "gather"), (ln_g, "gather"), (ln_b, "gather"), (c, "gather")]
    got = exchange("gather_params", items)
    G = {n: (_cols_from_gather(a) if n in COL_SHARDED else _rows_from_gather(a)) for n, a in zip(big, got)}
    lower_all = got[len(big)].transpose(1, 0, 2).reshape(n_hgrn, -1)
    ln_g_all = got[len(big) + 1].transpose(1, 2, 0, 3).reshape(depth, 2, d)
    ln_b_all = got[len(big) + 2].transpose(1, 2, 0, 3).reshape(depth, 2, d)
    c_all = got[len(big) + 3].reshape(N_DEV * bsz, d)
    w_in_mla = _w_in_internal(G['mla_w_in'])
    w_qb = _qb_internal(G['mla_w_qb'])
    w_kvb = _kvb_internal(G['mla_w_kvb'])

    cols = ada_w.shape[-1]
    mod_loc = ada_fwd(c_all, ada_w, ada_b)
    (mod_got,) = exchange("scatter_mod", [(mod_loc.reshape(n_sub, N_DEV, bsz, cols).transpose(1, 0, 2, 3), "a2a")])
    mod = mod_got.transpose(1, 2, 0, 3).reshape(n_sub, bsz, 1, 3 * d)
    shift = [mod[k, :, :, 0:d] for k in range(n_sub)]
    scale = [mod[k, :, :, d:2 * d] for k in range(n_sub)]
    gate = [mod[k, :, :, 2 * d:] for k in range(n_sub)]
    lng = [ln_g_all[k // 2, k % 2][None, :] for k in range(n_sub)]
    lnb = [ln_b_all[k // 2, k % 2][None, :] for k in range(n_sub)]

    tabs = rope_tables(positions)

    def sub_weights(k):
        layer, j = k // 2, k // 4
        if k % 2:
            return 'ffn', layer, dict(w_in=G['ffn_w_in'][layer], w_out=G['ffn_w_out'][layer])
        if layer % 2 == 0:
            return 'mla', j, dict(w_in=w_in_mla[j], q_norm=mla_q_norm[j][None, :], w_qb=w_qb[j], kv_norm=mla_kv_norm[j][None, :],
                                  w_kvb=w_kvb[j], w_o=G['mla_w_o'][j])
        return 'hgrn', j, dict(w_in=G['hgrn_w_in'][j], lb=lower_all[j][None, :], g_norm=hgrn_g_norm[j][None, :], w_o=G['hgrn_w_o'][j])

    (h,) = rowwise("mod_first", lambda rv, ev, gv: (f_mod(rv, ev, gv), []), [_full(x)], [scale[0], shift[0]], [],
                   [(BF16, [d])], ts=ROW_TILE)
    xs, ys, saved = [x], [], []
    loss_acc = None
    for k in range(n_sub):
        kind, _, w = sub_weights(k)
        if kind == 'ffn':
            y, sv = _ffn_forward(h, w)
        elif kind == 'mla':
            y, sv = _mla_forward(h, w, tabs)
        else:
            y, sv = _hgrn_forward(h, w)
        ys.append(y)
        saved.append(sv)
        if k + 1 < n_sub:
            xn, h = rowwise("ln_mod", lambda rv, ev, gv: (f_ln_mod(rv, ev, gv), []), [_full(xs[k]), _full(y)],
                            [gate[k], scale[k + 1], shift[k + 1]], [lng[k], lnb[k]], [(F32, [d]), (BF16, [d])], ts=ROW_TILE)
            xs.append(xn)
        else:
            def loss_rows(rv, ev, gv):
                (row,) = f_ln_loss(rv, ev, gv)
                return [], [jnp.broadcast_to(jnp.sum(row, keepdims=True), (1, LANES))]
            (loss_acc,) = rowwise("ln_loss", loss_rows, [_full(xs[k]), _full(y), _full(loss_target)], [gate[k]], [lng[k], lnb[k]],
                                  [], ts=ROW_TILE, accs=[LANES])
    loss = lax.psum(loss_acc[0, 0], ("x", "y", "c"))

    d_shift, d_scale, d_gate = [None] * n_sub, [None] * n_sub, [None] * n_sub
    d_lng, d_lnb = [None] * n_sub, [None] * n_sub
    part = {n: [None] * W[n].shape[0] for n in big + ['mla_q_norm', 'mla_kv_norm', 'hgrn_g_norm']}
    d_lower = [None] * n_hgrn
    k = n_sub - 1
    (dx, dy), (d_gate[k],), (d_lng[k], d_lnb[k]) = rowwise_bwd(
        "ln_loss_bwd", f_ln_loss, [_full(xs[k]), _full(ys[k]), _full(loss_target)], [gate[k]], [lng[k], lnb[k]], [],
        [(F32, [0]), (BF16, [1])], ts=ROW_TILE, n_diff=2, unit_ct=1)
    grad_x = None
    for k in range(n_sub - 1, -1, -1):
        kind, j, w = sub_weights(k)
        if kind == 'ffn':
            dh, g = _ffn_backward(dy, saved[k], w)
            part['ffn_w_in'][j], part['ffn_w_out'][j] = g['w_in'], g['w_out']
        elif kind == 'mla':
            dh, g = _mla_backward(dy, saved[k], w, tabs)
            part['mla_w_in'][j] = g['w_in'][:, :mla_w_in.shape[-1] * N_DEV]
            part['mla_w_qb'][j] = _qb_internal(g['w_qb'], inverse=True)
            part['mla_w_kvb'][j] = _kvb_internal(g['w_kvb'], inverse=True)
            part['mla_w_o'][j] = g['w_o']
            part['mla_q_norm'][j], part['mla_kv_norm'][j] = g['q_norm'][0], g['kv_norm'][0]
        else:
            dh, g = _hgrn_backward(dy, saved[k], w)
            part['hgrn_w_in'][j], part['hgrn_w_o'][j] = g['w_in'], g['w_o']
            part['hgrn_g_norm'][j] = g['g_norm'][0]
            d_lower[j] = g['lb'][0]
        if k:
            (dx, dy), (d_gate[k - 1], d_scale[k], d_shift[k]), (d_lng[k - 1], d_lnb[k - 1]) = rowwise_bwd(
                "ln_mod_bwd", f_ln_mod, [_full(xs[k - 1]), _full(ys[k - 1])], [gate[k - 1], scale[k], shift[k]],
                [lng[k - 1], lnb[k - 1]], [(dx, [d]), (dh, [d])], [(F32, [0]), (BF16, [1])], ts=ROW_TILE, n_diff=2)
        else:
            (grad_x,), (d_scale[0], d_shift[0]), _ = rowwise_bwd(
                "mod_first_bwd", f_mod_with_x, [_full(x)], [scale[0], shift[0]], [], [(dh, [d]), (dx, [d])],
                [(F32, [0])], ts=ROW_TILE, n_diff=1)

    slabs = []
    for n in big:
        full = jnp.stack(part[n])
        slabs.append(((_cols_to_slabs if n in COL_SHARDED else _rows_to_slabs)(full), "a2a"))
    slabs.append((jnp.stack(d_lower).reshape(n_hgrn, N_DEV, -1).transpose(1, 0, 2), "a2a"))
    for parts in (d_lng, d_lnb):
        full = jnp.stack([p[0] for p in parts]).reshape(depth, 2, N_DEV, d // N_DEV)
        slabs.append((full.transpose(2, 0, 1, 3), "a2a"))
    dmod = jnp.concatenate([jnp.stack(d_shift), jnp.stack(d_scale), jnp.stack(d_gate)], axis=-1)
    slabs.append((dmod.reshape(n_sub, bsz, N_DEV, cols).transpose(2, 0, 1, 3), "a2a"))
    small = ['mla_q_norm', 'mla_kv_norm', 'hgrn_g_norm']
    slabs += [(jnp.stack(part[n]), "gather") for n in small]
    got = exchange("scatter_grads", slabs)
    stacks = dict(zip(big + ['hgrn_lb', 'ln_g', 'ln_b', 'dmod'] + small, got))

    dmod_all = stacks['dmod'].transpose(1, 0, 2, 3).reshape(n_sub, N_DEV * bsz, cols)
    g_ada_w, g_ada_b = ada_bwd(c_all, dmod_all)
    stacks['ada_w'] = g_ada_w.reshape((1,) + ada_w.shape)
    stacks['ada_b'] = g_ada_b.reshape((1,) + ada_b.shape)

    res = {}
    for n in WEIGHTS:
        if n == 'hgrn_lb':
            res[n] = adam_lb(stacks[n], W[n], M1[n], M2[n])
        else:
            res[n] = adam("adam_" + n, stacks[n], W[n], M1[n], M2[n])
    return (loss, grad_x, *[res[n][0] for n in WEIGHTS], *[res[n][1] for n in WEIGHTS], *[res[n][2] for n in WEIGHTS],
            *[res[n][3] for n in WEIGHTS])
```

```python
import functools
import math

import numpy as np
import jax
import jax.numpy as jnp
from jax import lax
from jax.experimental import pallas as pl
from jax.experimental.pallas import tpu as pltpu

F32 = jnp.float32
BF16 = jnp.bfloat16

N_DEV = 8
LANES = 128
VMEM_LIMIT = 52 * 1024 * 1024

D_MODEL = 1024
DEPTH = 4
MLA_HEADS = 16
QK_NOPE = 64
QK_ROPE = 32
V_HEAD = 64
Q_LORA = 768
KV_LORA = 256
ROPE_THETA = 10000.0
HGRN_EXPAND = 128
HGRN_CHUNK = 64
HGRN_HEADS_PER_STEP = 2
D_FF = 2816
ALPHA = (2.0 * DEPTH) ** 0.25
LN_EPS = 1e-5
RMS_EPS = 1e-6
ADAM_LR = 0.001
ADAM_B1 = 0.9
ADAM_B2 = 0.999
ADAM_EPS = 1e-08
ADAM_WD = 0.01
ADAM_STEP = 10

ATTN_TQ = 256
ROW_TILE = 256

WEIGHTS = ['mla_w_in', 'mla_q_norm', 'mla_w_qb', 'mla_kv_norm', 'mla_w_kvb', 'mla_w_o', 'hgrn_lb', 'hgrn_w_in',
           'hgrn_g_norm', 'hgrn_w_o', 'ffn_w_in', 'ffn_w_out', 'ada_w', 'ada_b', 'ln_g', 'ln_b']
COL_SHARDED = ['mla_w_in', 'mla_w_qb', 'mla_w_kvb', 'hgrn_w_in', 'ffn_w_in']
ROW_SHARDED = ['mla_w_o', 'hgrn_w_o', 'ffn_w_out']


def _params(*sem):
    if sem:
        return pltpu.CompilerParams(dimension_semantics=sem, vmem_limit_bytes=VMEM_LIMIT)
    return pltpu.CompilerParams(vmem_limit_bytes=VMEM_LIMIT)


def _pick(n, cap):
    best = None
    for t in range(LANES, min(n, cap) + 1, LANES):
        if n % t == 0:
            best = t
    return best or n


def _pick_rows(n, cap):
    best = None
    for t in range(8, min(n, cap) + 1, 8):
        if n % t == 0:
            best = t
    return best or n


def matmul(name, a, b, *, ta=False, tb=False, out_dtype=F32, tm_cap=1024, tn_cap=1024, tk_cap=2048):
    (k1, m) = a.shape if ta else a.shape[::-1]
    (n, k2) = b.shape if tb else b.shape[::-1]
    assert k1 == k2, (name, a.shape, b.shape)
    tm, tn, tk = _pick(m, tm_cap), _pick(n, tn_cap), _pick(k1, tk_cap)
    nk = k1 // tk
    dims = (((0 if ta else 1,), (1 if tb else 0,)), ((), ()))

    def prod(a_ref, b_ref):
        return lax.dot_general(a_ref[...].astype(BF16), b_ref[...].astype(BF16), dims, preferred_element_type=F32)

    if nk == 1:
        def body(a_ref, b_ref, o_ref):
            o_ref[...] = prod(a_ref, b_ref).astype(o_ref.dtype)
        scratch = []
    else:
        def body(a_ref, b_ref, o_ref, acc_ref):
            k = pl.program_id(2)

            @pl.when(k == 0)
            def _():
                acc_ref[...] = jnp.zeros_like(acc_ref)

            acc_ref[...] += prod(a_ref, b_ref)

            @pl.when(k == nk - 1)
            def _():
                o_ref[...] = acc_ref[...].astype(o_ref.dtype)
        scratch = [pltpu.VMEM((tm, tn), F32)]

    a_spec = pl.BlockSpec((tk, tm), lambda i, j, k: (k, i)) if ta else pl.BlockSpec((tm, tk), lambda i, j, k: (i, k))
    b_spec = pl.BlockSpec((tn, tk), lambda i, j, k: (j, k)) if tb else pl.BlockSpec((tk, tn), lambda i, j, k: (k, j))
    return pl.pallas_call(
        body, name=name, grid=(m // tm, n // tn, nk),
        in_specs=[a_spec, b_spec], out_specs=pl.BlockSpec((tm, tn), lambda i, j, k: (i, j)),
        out_shape=jax.ShapeDtypeStruct((m, n), out_dtype), scratch_shapes=scratch,
        compiler_params=_params("parallel", "parallel", "arbitrary"),
    )(a, b)


def mm3(name, a3, w, **kw):
    bsz, s, k = a3.shape
    out = matmul(name, a3.reshape(bsz * s, k), w, **kw)
    return out.reshape(bsz, s, out.shape[-1])


def wgrad(name, a3, g3):
    bsz, s, k = a3.shape
    return matmul(name, a3.reshape(bsz * s, k), g3.reshape(bsz * s, g3.shape[-1]), ta=True, out_dtype=BF16)


def _dg(a, b, ca, cb, **kw):
    return lax.dot_general(a, b, (((ca,), (cb,)), ((), ())), preferred_element_type=F32, **kw)


@functools.partial(jax.custom_vjp, nondiff_argnums=(2, 3))
def bdot(a, b, ca, cb):
    return _dg(a.astype(BF16), b.astype(BF16), ca, cb)


def _bdot_fwd(a, b, ca, cb):
    return bdot(a, b, ca, cb), (a, b)


def _bdot_bwd(ca, cb, res, g):
    a, b = res
    a16, b16, g16 = a.astype(BF16), b.astype(BF16), g.astype(BF16)
    if ca == 1:
        da = _dg(g16, b16, 1, 1 if cb == 0 else 0)
    else:
        da = _dg(b16, g16, 1 if cb == 0 else 0, 1)
    if cb == 0:
        db = _dg(a16, g16, 0 if ca == 1 else 1, 0)
    else:
        db = _dg(g16, a16, 0, 0 if ca == 1 else 1)
    return da, db


bdot.defvjp(_bdot_fwd, _bdot_bwd)


def hdot(a, b, ca=1, cb=0):
    return _dg(a, b, ca, cb, precision=lax.Precision.HIGHEST)


def _row_specs(rows, exs, globs, ts):
    specs = [pl.BlockSpec((1, ts, w), lambda b, s, j=j: (b, s, j)) for (_, j, w) in rows]
    specs += [pl.BlockSpec((1, 1, e.shape[-1]), lambda b, s: (b, 0, 0)) for e in exs]
    specs += [pl.BlockSpec((1, g.shape[-1]), lambda b, s: (0, 0)) for g in globs]
    return specs


def _store_pieces(o_ref, pieces, widths):
    off = 0
    for p, w in zip(pieces, widths):
        o_ref[0, :, off:off + w] = p.astype(o_ref.dtype)
        off += w


def _load_pieces(c_ref, widths):
    out, off = [], 0
    for w in widths:
        out.append(c_ref[0, :, off:off + w].astype(F32))
        off += w
    return out


def rowwise(name, f, rows, exs, globs, outs, *, ts, accs=()):
    bsz, s = rows[0][0].shape[:2]
    ts = min(ts, s)
    n_r, n_e, n_g, n_o = len(rows), len(exs), len(globs), len(outs)

    def body(*refs):
        rv = [r[0].astype(F32) for r in refs[:n_r]]
        ev = [e[0] for e in refs[n_r:n_r + n_e]]
        gv = [g[...] for g in refs[n_r + n_e:n_r + n_e + n_g]]
        o_refs = refs[n_r + n_e + n_g:n_r + n_e + n_g + n_o]
        a_refs = refs[n_r + n_e + n_g + n_o:]
        pieces, sums = f(rv, ev, gv)
        idx = 0
        for o_ref, (_, ws) in zip(o_refs, outs):
            _store_pieces(o_ref, pieces[idx:idx + len(ws)], ws)
            idx += len(ws)
        if accs:
            @pl.when((pl.program_id(0) == 0) & (pl.program_id(1) == 0))
            def _():
                for a_ref in a_refs:
                    a_ref[...] = jnp.zeros_like(a_ref)
            for a_ref, val in zip(a_refs, sums):
                a_ref[...] += val

    out_specs = [pl.BlockSpec((1, ts, sum(ws)), lambda b, s: (b, s, 0)) for (_, ws) in outs]
    out_specs += [pl.BlockSpec((1, w), lambda b, s: (0, 0)) for w in accs]
    out_shape = [jax.ShapeDtypeStruct((bsz, s, sum(ws)), dt) for (dt, ws) in outs]
    out_shape += [jax.ShapeDtypeStruct((1, w), F32) for w in accs]
    return pl.pallas_call(
        body, name=name, grid=(bsz, s // ts),
        in_specs=_row_specs(rows, exs, globs, ts), out_specs=out_specs, out_shape=out_shape,
        compiler_params=_params("arbitrary", "arbitrary"),
    )(*[r[0] for r in rows], *exs, *globs)


def rowwise_bwd(name, f, rows, exs, globs, cts, d_groups, *, ts, n_diff, unit_ct=0):
    bsz, s = rows[0][0].shape[:2]
    ts = min(ts, s)
    n_r, n_e, n_g, n_c = len(rows), len(exs), len(globs), len(cts)
    n_d = len(d_groups)

    def body(*refs):
        rv = [r[0].astype(F32) for r in refs[:n_r]]
        ev = [e[0] for e in refs[n_r:n_r + n_e]]
        gv = [g[...] for g in refs[n_r + n_e:n_r + n_e + n_g]]
        base = n_r + n_e + n_g
        c_refs = refs[base:base + n_c]
        d_refs = refs[base + n_c:base + n_c + n_d]
        de_refs = refs[base + n_c + n_d:base + n_c + n_d + n_e]
        dg_refs = refs[base + n_c + n_d + n_e:]
        fixed = rv[n_diff:]
        out, vjp = jax.vjp(lambda r, e, g: f(r + fixed, e, g), rv[:n_diff], ev, gv)
        ct = []
        for c_ref, (_, ws) in zip(c_refs, cts):
            ct += _load_pieces(c_ref, ws)
        ct += [jnp.ones_like(o) for o in out[len(ct):]]
        assert len(ct) == len(out) and len(out) - unit_ct == sum(len(ws) for _, ws in cts), name
        d_r, d_e, d_g = vjp(ct)
        for d_ref, (_, idxs) in zip(d_refs, d_groups):
            _store_pieces(d_ref, [d_r[i] for i in idxs], [rows[i][2] for i in idxs])
        first_s = pl.program_id(1) == 0
        if n_e:
            @pl.when(first_s)
            def _():
                for r in de_refs:
                    r[...] = jnp.zeros_like(r)
            for r, val in zip(de_refs, d_e):
                r[0] += val
        if n_g:
            @pl.when(first_s & (pl.program_id(0) == 0))
            def _():
                for r in dg_refs:
                    r[...] = jnp.zeros_like(r)
            for r, val in zip(dg_refs, d_g):
                r[...] += val

    in_specs = _row_specs(rows, exs, globs, ts)
    in_specs += [pl.BlockSpec((1, ts, sum(ws)), lambda b, s: (b, s, 0)) for (_, ws) in cts]
    out_specs = [pl.BlockSpec((1, ts, sum(rows[i][2] for i in idxs)), lambda b, s: (b, s, 0)) for (_, idxs) in d_groups]
    out_specs += [pl.BlockSpec((1, 1, e.shape[-1]), lambda b, s: (b, 0, 0)) for e in exs]
    out_specs += [pl.BlockSpec((1, g.shape[-1]), lambda b, s: (0, 0)) for g in globs]
    out_shape = [jax.ShapeDtypeStruct((bsz, s, sum(rows[i][2] for i in idxs)), dt) for (dt, idxs) in d_groups]
    out_shape += [jax.ShapeDtypeStruct(e.shape, F32) for e in exs]
    out_shape += [jax.ShapeDtypeStruct(g.shape, F32) for g in globs]
    res = pl.pallas_call(
        body, name=name, grid=(bsz, s // ts),
        in_specs=in_specs, out_specs=out_specs, out_shape=out_shape,
        compiler_params=_params("arbitrary", "arbitrary"),
    )(*[r[0] for r in rows], *exs, *globs, *[c[0] for c in cts])
    return res[:n_d], res[n_d:n_d + n_e], res[n_d + n_e:]


def _full(a):
    return (a, 0, a.shape[-1])


def _view(a, col, w):
    assert col % w == 0
    return (a, col // w, w)


def _layer_norm(z, g, b):
    mu = jnp.mean(z, -1, keepdims=True)
    var = jnp.mean(jnp.square(z - mu), -1, keepdims=True)
    return (z - mu) * lax.rsqrt(var + LN_EPS) * g + b


def _rms_norm(z, g):
    ms = jnp.mean(jnp.square(z), -1, keepdims=True)
    return z * lax.rsqrt(ms + RMS_EPS) * g


def f_mod(rv, ev, gv):
    (x,), (scale, shift) = rv, ev
    return [x * (1.0 + scale) + shift]


def f_mod_with_x(rv, ev, gv):
    return f_mod(rv, ev, gv) + [rv[0]]


def f_ln_mod(rv, ev, gv):
    (x, y), (gate, scale, shift), (g, b) = rv, ev, gv
    xn = _layer_norm(ALPHA * x + (1.0 + gate) * y, g, b)
    return [xn, xn * (1.0 + scale) + shift]


def f_ln_loss(rv, ev, gv):
    (x, y, target), (gate,), (g, b) = rv, ev, gv
    xn = _layer_norm(ALPHA * x + (1.0 + gate) * y, g, b)
    return [0.5 * jnp.mean(jnp.square(xn - target), -1, keepdims=True)]


def f_swiglu(rv, ev, gv):
    gate, up = rv
    return [jax.nn.silu(gate) * up]


def _head_spread(width):
    r2 = QK_ROPE // 2
    j = lax.broadcasted_iota(jnp.int32, (LANES, width), 0)
    col = lax.broadcasted_iota(jnp.int32, (LANES, width), 1) % r2
    return (j == col).astype(F32), (j == col + r2).astype(F32)


def f_mla_mid(rv, ev, gv):
    (q_lat, kv_lat, kr, cos, sin), (q_g, kv_g) = rv, gv
    e1, e2 = _head_spread(cos.shape[-1])
    k1, k2 = hdot(kr, e1), hdot(kr, e2)
    return [_rms_norm(q_lat, q_g), _rms_norm(kv_lat, kv_g), k1 * cos - k2 * sin, k1 * sin + k2 * cos]


def f_mla_post(rv, ev, gv):
    q_nope, q1, q2, kv, cos, sin = rv
    return [q_nope, q1 * cos - q2 * sin, q1 * sin + q2 * cos, kv]


def f_hg_pre(rv, ev, gv):
    (q, fx), (lb,) = rv[:2], gv
    f = lb + (1.0 - lb) * jax.nn.sigmoid(fx)
    return [jax.nn.silu(q), jnp.log(f), 1.0 - f]


def f_hg_pre_with_iv(rv, ev, gv):
    return f_hg_pre(rv, ev, gv) + [rv[2], rv[3]]


def f_hg_post(rv, ev, gv):
    nh = len(rv) // 2
    (gn,) = gv
    return [_rms_norm(rv[h], gn) * jax.nn.silu(rv[nh + h]) for h in range(nh)]


def rope_tables(positions):
    bsz, s = positions.shape
    r2 = QK_ROPE // 2
    width = MLA_HEADS * r2
    inv = (ROPE_THETA ** (-np.arange(0, QK_ROPE, 2, dtype=np.float32) / QK_ROPE)).astype(np.float32)
    inv = jnp.asarray(np.tile(inv, MLA_HEADS)[None, :])
    ts = min(ROW_TILE, s)

    def body(p_ref, inv_ref, cos_ref, sin_ref):
        ang = p_ref[0].astype(F32) * inv_ref[...]
        cos_ref[0] = jnp.cos(ang)
        sin_ref[0] = jnp.sin(ang)

    spec = pl.BlockSpec((1, ts, width), lambda b, s: (b, s, 0))
    return pl.pallas_call(
        body, name="rope_tables", grid=(bsz, s // ts),
        in_specs=[pl.BlockSpec((1, ts, 1), lambda b, s: (b, s, 0)), pl.BlockSpec((1, width), lambda b, s: (0, 0))],
        out_specs=[spec, spec], out_shape=[jax.ShapeDtypeStruct((bsz, s, width), F32)] * 2,
        compiler_params=_params("arbitrary", "arbitrary"),
    )(positions[:, :, None], inv)


def _attn_probs(q, k, row0):
    scale = (QK_NOPE + QK_ROPE) ** -0.5
    s = _dg(q, k, 1, 1) * scale
    rows = row0 + lax.broadcasted_iota(jnp.int32, s.shape, 0)
    cols = lax.broadcasted_iota(jnp.int32, s.shape, 1)
    s = jnp.where(cols <= rows, s, jnp.finfo(F32).min)
    e = jnp.exp(s - jnp.max(s, -1, keepdims=True))
    return e / jnp.sum(e, -1, keepdims=True), scale


def attn_fwd(q, k, v, ride=()):
    bsz, h, s, dq = q.shape
    dv = v.shape[-1]
    tq = min(ATTN_TQ, s)
    grid = (bsz, h, s // tq)
    rd = _Ride(ride)

    def body(*refs):
        (q_ref, k_ref, v_ref), srcs, (o_ref,), outs, sems = rd.split(refs, 3, 1)
        rd.run(srcs, outs, sems, grid)
        for i in range(grid[2]):
            @pl.when(pl.program_id(2) == i)
            def _(i=i):
                kend = (i + 1) * tq
                p, _ = _attn_probs(q_ref[0, 0], k_ref[0, 0, :kend, :], i * tq)
                o_ref[0, 0] = _dg(p.astype(BF16), v_ref[0, 0, :kend, :], 1, 0).astype(o_ref.dtype)

    res = pl.pallas_call(
        body, name="attn_fwd", grid=grid,
        in_specs=[pl.BlockSpec((1, 1, tq, dq), lambda b, h, i: (b, h, i, 0)),
                  pl.BlockSpec((1, 1, s, dq), lambda b, h, i: (b, h, 0, 0)),
                  pl.BlockSpec((1, 1, s, dv), lambda b, h, i: (b, h, 0, 0))] + rd.in_specs,
        out_specs=[pl.BlockSpec((1, 1, tq, dv), lambda b, h, i: (b, h, i, 0))] + rd.out_specs,
        out_shape=[jax.ShapeDtypeStruct((bsz, h, s, dv), BF16)] + rd.out_shape, scratch_shapes=rd.scratch,
        compiler_params=_params("arbitrary", "arbitrary", "arbitrary"),
    )(q, k, v, *rd.srcs)
    return res[0], res[1:]


def attn_bwd(q, k, v, do, ride=()):
    bsz, h, s, dq = q.shape
    dv = v.shape[-1]
    tq = min(ATTN_TQ, s)
    grid = (bsz, h, s // tq)
    rd = _Ride(ride)

    def body(*refs):
        (q_ref, k_ref, v_ref, do_ref), srcs, (dq_ref, dk_ref, dv_ref), outs, sems = rd.split(refs, 4, 3)
        rd.run(srcs, outs, sems, grid)

        @pl.when(pl.program_id(2) == 0)
        def _():
            dk_ref[...] = jnp.zeros_like(dk_ref)
            dv_ref[...] = jnp.zeros_like(dv_ref)

        for i in range(grid[2]):
            @pl.when(pl.program_id(2) == i)
            def _(i=i):
                kend = (i + 1) * tq
                qv, kv, vv = q_ref[0, 0], k_ref[0, 0, :kend, :], v_ref[0, 0, :kend, :]
                p, scale = _attn_probs(qv, kv, i * tq)
                do16 = do_ref[0, 0].astype(BF16)
                dv_ref[0, 0, :kend, :] += _dg(p.astype(BF16), do16, 0, 0)
                dp = _dg(do16, vv, 1, 1)
                ds = (p * (dp - jnp.sum(dp * p, -1, keepdims=True)) * scale).astype(BF16)
                dq_ref[0, 0] = _dg(ds, kv, 1, 0)
                dk_ref[0, 0, :kend, :] += _dg(ds, qv, 0, 0)

    res = pl.pallas_call(
        body, name="attn_bwd", grid=grid,
        in_specs=[pl.BlockSpec((1, 1, tq, dq), lambda b, h, i: (b, h, i, 0)),
                  pl.BlockSpec((1, 1, s, dq), lambda b, h, i: (b, h, 0, 0)),
                  pl.BlockSpec((1, 1, s, dv), lambda b, h, i: (b, h, 0, 0)),
                  pl.BlockSpec((1, 1, tq, dv), lambda b, h, i: (b, h, i, 0))] + rd.in_specs,
        out_specs=[pl.BlockSpec((1, 1, tq, dq), lambda b, h, i: (b, h, i, 0)),
                   pl.BlockSpec((1, 1, s, dq), lambda b, h, i: (b, h, 0, 0)),
                   pl.BlockSpec((1, 1, s, dv), lambda b, h, i: (b, h, 0, 0))] + rd.out_specs,
        out_shape=[jax.ShapeDtypeStruct((bsz, h, s, dq), F32), jax.ShapeDtypeStruct((bsz, h, s, dq), F32),
                   jax.ShapeDtypeStruct((bsz, h, s, dv), F32)] + rd.out_shape, scratch_shapes=rd.scratch,
        compiler_params=_params("arbitrary", "arbitrary", "arbitrary"),
    )(q, k, v, do, *rd.srcs)
    return res[0], res[1], res[2], res[3:]


def _hgrn_tables(c):
    levels = c.bit_length() - 1
    assert 1 << levels == c
    r = np.arange(c)
    prefix, sign, mask = [r[:, None] >= r[None, :]], [], []
    for l in range(levels):
        ref = ((r >> (l + 1)) << (l + 1)) + (1 << l) - 1
        lower = ((r >> l) & 1) == 1
        prefix.append(r[None, :] <= ref[:, None])
        sign.append(np.broadcast_to(np.where(lower, 1.0, -1.0)[:, None], (c, LANES)))
        mask.append((((r[:, None] ^ r[None, :]) >> l) == 1) & lower[:, None])
    return (jnp.asarray(np.concatenate(prefix, 0), BF16), jnp.asarray(np.stack(sign), F32),
            jnp.asarray(np.stack(mask), F32))


def _const_specs(tables):
    return [pl.BlockSpec(t.shape, lambda b, h, i, nd=t.ndim: (0,) * nd) for t in tables]


def _split3(x):
    hi = x.astype(BF16)
    rest = x - hi.astype(F32)
    mid = rest.astype(BF16)
    return hi, mid, (rest - mid.astype(F32)).astype(BF16)


@functools.partial(jax.custom_vjp, nondiff_argnums=(2,))
def prefix_sums(p, g, n):
    k = g.shape[1]
    r = _dg(p, jnp.concatenate(_split3(g), axis=1), 1, 0)
    r = r[:, :k] + r[:, k:2 * k] + r[:, 2 * k:]
    c = r.shape[0] // n
    return tuple(r[i * c:(i + 1) * c] for i in range(n))


def _prefix_fwd(p, g, n):
    return prefix_sums(p, g, n), p


def _prefix_bwd(n, p, ct):
    ct = jnp.concatenate(ct, axis=0)
    k = ct.shape[1]
    r = _dg(p, jnp.concatenate(_split3(ct), axis=1), 0, 0)
    return jnp.zeros_like(p), r[:, :k] + r[:, k:2 * k] + r[:, 2 * k:]


prefix_sums.defvjp(_prefix_fwd, _prefix_bwd)


def _dot3_raw(a, b, ca, cb):
    ah = a.astype(BF16)
    al = (a - ah.astype(F32)).astype(BF16)
    bh = b.astype(BF16)
    bl = (b - bh.astype(F32)).astype(BF16)
    if (ca == 1 and a.shape[1] % LANES) or (cb == 1 and b.shape[1] % LANES):
        return _dg(ah, bh, ca, cb) + _dg(ah, bl, ca, cb) + _dg(al, bh, ca, cb)
    return _dg(jnp.concatenate([ah, ah, al], axis=ca), jnp.concatenate([bh, bl, bh], axis=cb), ca, cb)


@functools.partial(jax.custom_vjp, nondiff_argnums=(2, 3))
def dot3(a, b, ca, cb):
    return _dot3_raw(a, b, ca, cb)


def _dot3_fwd(a, b, ca, cb):
    return _dot3_raw(a, b, ca, cb), (a, b)


def _dot3_bwd(ca, cb, res, g):
    a, b = res
    if ca == 1:
        da = _dot3_raw(g, b, 1, 1 if cb == 0 else 0)
    else:
        da = _dot3_raw(b, g, 1 if cb == 0 else 0, 1)
    if cb == 0:
        db = _dot3_raw(a, g, 0 if ca == 1 else 1, 0)
    else:
        db = _dot3_raw(g, a, 0, 0 if ca == 1 else 1)
    return da, db


dot3.defvjp(_dot3_fwd, _dot3_bwd)


def _hgrn_chunk(q, g, k, v, st0, prefix, sign, mask):
    levels = len(mask)
    pre = prefix_sums(prefix, g, levels + 1)
    b = pre[0]
    o = bdot(q * jnp.exp(b), st0, 1, 1)
    att = None
    for l in range(levels):
        e = jnp.exp((b - pre[l + 1]) * sign[l])
        a = dot3(q * e, k * e, 1, 1) * mask[l]
        att = a if att is None else att + a
    o = o + bdot(att, v, 1, 0) + jnp.sum(q * k, -1, keepdims=True) * v
    total = jnp.sum(g, 0, keepdims=True)
    st1 = st0 * jnp.exp(total) + bdot(v, k * jnp.exp(total - b), 0, 0)
    return o, st1


def hgrn_fwd(qs, logf, kk, proj, ride=()):
    bsz, s, hk = qs.shape
    kd = HGRN_EXPAND
    c = min(HGRN_CHUNK, s)
    nh, nc = hk // kd, s // c
    hp = math.gcd(nh, HGRN_HEADS_PER_STEP)
    tables = _hgrn_tables(c)
    levels = tables[2].shape[0]
    grid = (bsz, nh // hp, nc)
    rd = _Ride(ride)

    def body(*refs):
        (q_ref, g_ref, k_ref, v_ref, p_ref, sg_ref, mk_ref), srcs, (o_ref, st_ref), outs, scratch = rd.split(refs, 7, 2)
        state = scratch[0]
        rd.run(srcs, outs, scratch, grid)

        @pl.when(pl.program_id(2) == 0)
        def _():
            state[...] = jnp.zeros_like(state)

        prefix, sign, mask = p_ref[...], [sg_ref[l] for l in range(levels)], [mk_ref[l] for l in range(levels)]
        for j in range(hp):
            cols = slice(j * kd, (j + 1) * kd)
            st0 = state[j]
            st_ref[0, j, 0] = st0
            o, st1 = _hgrn_chunk(q_ref[0, :, cols], g_ref[0, :, cols], k_ref[0, :, cols], v_ref[0, :, cols], st0,
                                 prefix, sign, mask)
            o_ref[0, :, cols] = o
            state[j] = st1

    blk = pl.BlockSpec((1, c, hp * kd), lambda b, h, i: (b, i, h))
    res = pl.pallas_call(
        body, name="hgrn_fwd", grid=grid,
        in_specs=[blk, blk, blk, pl.BlockSpec((1, c, hp * kd), lambda b, h, i: (b, i, 2 * (nh // hp) + h))]
        + _const_specs(tables) + rd.in_specs,
        out_specs=[blk, pl.BlockSpec((1, hp, 1, kd, kd), lambda b, h, i: (b, h, i, 0, 0))] + rd.out_specs,
        out_shape=[jax.ShapeDtypeStruct((bsz, s, hk), F32), jax.ShapeDtypeStruct((bsz, nh, nc, kd, kd), F32)] + rd.out_shape,
        scratch_shapes=[pltpu.VMEM((hp, kd, kd), F32)] + rd.scratch,
        compiler_params=_params("arbitrary", "arbitrary", "arbitrary"),
    )(qs, logf, kk, proj, *tables, *rd.srcs)
    return res[0], res[1], res[2:]


def hgrn_bwd(qs, logf, kk, proj, states, do, ride=()):
    bsz, s, hk = qs.shape
    kd = HGRN_EXPAND
    c = min(HGRN_CHUNK, s)
    nh, nc = hk // kd, s // c
    hp = math.gcd(nh, HGRN_HEADS_PER_STEP)
    tables = _hgrn_tables(c)
    levels = tables[2].shape[0]
    grid = (bsz, nh // hp, nc)
    rd = _Ride(ride)

    def body(*refs):
        ((q_ref, g_ref, k_ref, v_ref, st_ref, do_ref, p_ref, sg_ref, mk_ref), srcs, (dq_ref, dg_ref, dk_ref, dv_ref), outs,
         scratch) = rd.split(refs, 9, 4)
        dstate = scratch[0]
        rd.run(srcs, outs, scratch, grid)

        @pl.when(pl.program_id(2) == 0)
        def _():
            dstate[...] = jnp.zeros_like(dstate)

        prefix, sign, mask = p_ref[...], [sg_ref[l] for l in range(levels)], [mk_ref[l] for l in range(levels)]
        for j in range(hp):
            cols = slice(j * kd, (j + 1) * kd)
            _, vjp = jax.vjp(lambda q, g, k, v, st: _hgrn_chunk(q, g, k, v, st, prefix, sign, mask),
                             q_ref[0, :, cols], g_ref[0, :, cols], k_ref[0, :, cols], v_ref[0, :, cols], st_ref[0, j, 0])
            dq, dg, dk, dv, dst = vjp((do_ref[0, :, cols], dstate[j]))
            dq_ref[0, :, cols] = dq
            dg_ref[0, :, cols] = dg
            dk_ref[0, :, cols] = dk
            dv_ref[0, :, cols] = dv
            dstate[j] = dst

    blk = pl.BlockSpec((1, c, hp * kd), lambda b, h, i: (b, nc - 1 - i, h))
    shape = jax.ShapeDtypeStruct((bsz, s, hk), F32)
    res = pl.pallas_call(
        body, name="hgrn_bwd", grid=grid,
        in_specs=[blk, blk, blk, pl.BlockSpec((1, c, hp * kd), lambda b, h, i: (b, nc - 1 - i, 2 * (nh // hp) + h)),
                  pl.BlockSpec((1, hp, 1, kd, kd), lambda b, h, i: (b, h, nc - 1 - i, 0, 0)), blk]
        + _const_specs(tables) + rd.in_specs,
        out_specs=[blk] * 4 + rd.out_specs, out_shape=[shape] * 4 + rd.out_shape,
        scratch_shapes=[pltpu.VMEM((hp, kd, kd), F32)] + rd.scratch,
        compiler_params=_params("arbitrary", "arbitrary", "arbitrary"),
    )(qs, logf, kk, proj, states, do, *tables, *rd.srcs)
    return res[0], res[1], res[2], res[3], res[4:]


def cast_bf16(name, w):
    blk = pl.BlockSpec((1,) + w.shape[1:], lambda l: (l, 0, 0))

    def body(w_ref, o_ref):
        o_ref[...] = w_ref[...].astype(BF16)

    return pl.pallas_call(body, name=name, grid=(w.shape[0],), in_specs=[blk], out_specs=blk,
                          out_shape=jax.ShapeDtypeStruct(w.shape, BF16), compiler_params=_params("arbitrary"))(w)


def _lower_bounds(rows):
    m = functools.reduce(jnp.maximum, rows)
    e = [jnp.exp(r - m) for r in rows]
    z = functools.reduce(lambda a, b: a + b, e)
    soft = [x / z for x in e]
    out, run = [], jnp.zeros_like(rows[0])
    for sft in soft:
        run = run + sft
        out.append(run - soft[0])
    return out


def lower_bounds(lb):
    n = lb.shape[0]

    def body(lb_ref, o_ref):
        for i, r in enumerate(_lower_bounds([lb_ref[i:i + 1, :] for i in range(n)])):
            o_ref[i:i + 1, :] = r

    return pl.pallas_call(body, name="lower_bounds", out_shape=jax.ShapeDtypeStruct(lb.shape, F32),
                          compiler_params=_params())(lb)


def ada_fwd(c_all, ada_w, ada_b):
    nl, ns, d, cols = ada_w.shape
    n_ex = c_all.shape[0]

    def body(c_ref, w_ref, b_ref, o_ref):
        a = jax.nn.silu(c_ref[...]).astype(BF16)
        o_ref[0] = _dg(a, w_ref[0].astype(BF16), 1, 0) + b_ref[0]

    return pl.pallas_call(
        body, name="ada_fwd", grid=(nl * ns,),
        in_specs=[pl.BlockSpec((n_ex, d), lambda i: (0, 0)), pl.BlockSpec((1, d, cols), lambda i: (i, 0, 0)),
                  pl.BlockSpec((1, 1, cols), lambda i: (i, 0, 0))],
        out_specs=pl.BlockSpec((1, n_ex, cols), lambda i: (i, 0, 0)),
        out_shape=jax.ShapeDtypeStruct((nl * ns, n_ex, cols), F32), compiler_params=_params("arbitrary"),
    )(c_all, ada_w.reshape(nl * ns, d, cols), ada_b.reshape(nl * ns, 1, cols))


def ada_bwd(c_all, dmod):
    n, n_ex, cols = dmod.shape
    d = c_all.shape[1]

    def body(c_ref, g_ref, dw_ref, db_ref):
        a = jax.nn.silu(c_ref[...]).astype(BF16)
        g = g_ref[0]
        dw_ref[0] = _dg(a, g.astype(BF16), 0, 0)
        db_ref[0] = jnp.sum(g, 0, keepdims=True)

    return pl.pallas_call(
        body, name="ada_bwd", grid=(n,),
        in_specs=[pl.BlockSpec((n_ex, d), lambda i: (0, 0)), pl.BlockSpec((1, n_ex, cols), lambda i: (i, 0, 0))],
        out_specs=[pl.BlockSpec((1, d, cols), lambda i: (i, 0, 0)), pl.BlockSpec((1, 1, cols), lambda i: (i, 0, 0))],
        out_shape=[jax.ShapeDtypeStruct((n, d, cols), F32), jax.ShapeDtypeStruct((n, 1, cols), F32)],
        compiler_params=_params("arbitrary"),
    )(c_all, dmod)


def _adam_math(g, w, m, v):
    m = ADAM_B1 * m + (1.0 - ADAM_B1) * g
    v = ADAM_B2 * v + (1.0 - ADAM_B2) * jnp.square(g)
    m_hat = m / (1.0 - ADAM_B1 ** ADAM_STEP)
    v_hat = v / (1.0 - ADAM_B2 ** ADAM_STEP)
    delta = -ADAM_LR * (m_hat / (jnp.sqrt(v_hat) + ADAM_EPS) + ADAM_WD * w)
    return delta, m, v


def adam(name, gstack, w, m, v):
    shape = w.shape
    n, cols = gstack.shape[0], shape[-1]
    rows = math.prod(shape[:-1])
    tr = _pick_rows(rows, max(8, (2 * 1024 * 1024) // (4 * cols * n)))

    def body(g_ref, w_ref, m_ref, v_ref, go_ref, d_ref, mo_ref, vo_ref):
        g = g_ref[0].astype(F32)
        for i in range(1, n):
            g = g + g_ref[i].astype(F32)
        delta, m1, v1 = _adam_math(g, w_ref[...], m_ref[...], v_ref[...])
        go_ref[...] = g
        d_ref[...] = delta
        mo_ref[...] = m1
        vo_ref[...] = v1

    blk = pl.BlockSpec((tr, cols), lambda i: (i, 0))
    out = pl.pallas_call(
        body, name=name, grid=(rows // tr,),
        in_specs=[pl.BlockSpec((n, tr, cols), lambda i: (0, i, 0)), blk, blk, blk],
        out_specs=[blk] * 4, out_shape=[jax.ShapeDtypeStruct((rows, cols), F32)] * 4,
        compiler_params=_params("arbitrary"),
    )(gstack.reshape(n, rows, cols), w.reshape(rows, cols), m.reshape(rows, cols), v.reshape(rows, cols))
    return [o.reshape(shape) for o in out]


def adam_lb(gstack, lb, m, v):
    n, nl = gstack.shape[0], lb.shape[0]

    def body(g_ref, w_ref, m_ref, v_ref, go_ref, d_ref, mo_ref, vo_ref):
        rows = [w_ref[i:i + 1, :] for i in range(nl)]
        ct = []
        for i in range(nl):
            g = g_ref[0, i:i + 1, :]
            for j in range(1, n):
                g = g + g_ref[j, i:i + 1, :]
            ct.append(g)
        _, vjp = jax.vjp(_lower_bounds, rows)
        (grads,) = vjp(ct)
        for i in range(nl):
            delta, m1, v1 = _adam_math(grads[i], rows[i], m_ref[i:i + 1, :], v_ref[i:i + 1, :])
            go_ref[i:i + 1, :] = grads[i]
            d_ref[i:i + 1, :] = delta
            mo_ref[i:i + 1, :] = m1
            vo_ref[i:i + 1, :] = v1

    return pl.pallas_call(body, name="adam_hgrn_lb", out_shape=[jax.ShapeDtypeStruct(lb.shape, F32)] * 4,
                          compiler_params=_params())(gstack, lb, m, v)


class _Ride:
    def __init__(self, items):
        self.items = list(items)
        n = len(self.items)
        self.srcs = [src for src, _ in self.items]
        self.in_specs = [pl.BlockSpec(memory_space=pl.ANY)] * n
        self.out_specs = [pl.BlockSpec(memory_space=pl.ANY)] * n
        self.out_shape = [jax.ShapeDtypeStruct(((N_DEV,) + s.shape) if mode == "gather" else s.shape, s.dtype)
                          for s, mode in self.items]
        self.scratch = [pltpu.SemaphoreType.DMA((n, N_DEV - 1)), pltpu.SemaphoreType.DMA((n, N_DEV - 1)),
                        pltpu.SemaphoreType.DMA((n,))] if n else []

    def split(self, refs, n_in, n_out):
        n = len(self.items)
        a, b = n_in + n, n_in + 2 * n + n_out
        return refs[:n_in], refs[n_in:a], refs[a:a + n_out], refs[a + n_out:b], refs[b:]

    def _copies(self, srcs, outs, sems):
        send_sems, recv_sems, local_sems = sems
        x, y, c = lax.axis_index("x"), lax.axis_index("y"), lax.axis_index("c")
        me = 4 * x + 2 * y + c
        copies = []
        for i, (_, mode) in enumerate(self.items):
            mine = srcs[i] if mode == "gather" else srcs[i].at[me]
            copies.append(pltpu.make_async_copy(mine, outs[i].at[me], local_sems.at[i]))
            for p in range(1, N_DEV):
                px = 1 - x if p & 4 else x
                py = 1 - y if p & 2 else y
                pc = 1 - c if p & 1 else c
                part = srcs[i] if mode == "gather" else srcs[i].at[4 * px + 2 * py + pc]
                copies.append(pltpu.make_async_remote_copy(
                    src_ref=part, dst_ref=outs[i].at[me], send_sem=send_sems.at[i, p - 1], recv_sem=recv_sems.at[i, p - 1],
                    device_id=(px, py, pc), device_id_type=pl.DeviceIdType.MESH))
        return copies

    def run(self, srcs, outs, scratch, grid=()):
        if not self.items:
            return
        sems = scratch[len(scratch) - 3:]
        if not grid:
            copies = self._copies(srcs, outs, sems)
            for cp in copies:
                cp.start()
            for cp in copies:
                cp.wait()
            return
        ids = [pl.program_id(a) for a in range(len(grid))]
        first = functools.reduce(lambda a, b: a & b, [i == 0 for i in ids])
        last = functools.reduce(lambda a, b: a & b, [i == g - 1 for i, g in zip(ids, grid)])

        @pl.when(first)
        def _():
            for cp in self._copies(srcs, outs, sems):
                cp.start()

        @pl.when(last)
        def _():
            for cp in self._copies(srcs, outs, sems):
                cp.wait()


def exchange(name, items):
    rd = _Ride(items)

    def body(*refs):
        _, srcs, _, outs, scratch = rd.split(refs, 0, 0)
        rd.run(srcs, outs, scratch)

    return pl.pallas_call(body, name=name, in_specs=rd.in_specs, out_specs=rd.out_specs, out_shape=rd.out_shape,
                          scratch_shapes=rd.scratch)(*rd.srcs)


def _from_gather(name, g):
    if name in COL_SHARDED:
        _, k, n = g.shape
        return g.transpose(1, 0, 2).reshape(k, N_DEV * n)
    return g.reshape(-1, g.shape[-1])


def _to_slabs(name, w):
    k, n = w.shape
    if name in COL_SHARDED:
        return w.reshape(k, N_DEV, n // N_DEV).transpose(1, 0, 2)
    return w.reshape(N_DEV, k // N_DEV, n)


def _w_in_internal(w):
    return jnp.pad(w, ((0, 0), (0, LANES - QK_ROPE)))


def _qb_internal(w, inverse=False):
    h, n, r2 = MLA_HEADS, QK_NOPE, QK_ROPE // 2
    lead = w.shape[:-1]
    if not inverse:
        w = w.reshape(lead + (h, n + 2 * r2))
        parts = [w[..., :n], w[..., n:n + r2], w[..., n + r2:]]
        return jnp.concatenate([p.reshape(lead + (-1,)) for p in parts], axis=-1)
    parts = [w[..., :h * n].reshape(lead + (h, n)), w[..., h * n:h * (n + r2)].reshape(lead + (h, r2)),
             w[..., h * (n + r2):].reshape(lead + (h, r2))]
    return jnp.concatenate(parts, axis=-1).reshape(lead + (-1,))


def _kvb_internal(w, inverse=False):
    h, n, vd = MLA_HEADS, QK_NOPE, V_HEAD
    lead = w.shape[:-1]
    if not inverse:
        w = w.reshape(lead + (h, n + vd))
        return jnp.concatenate([w[..., :n].reshape(lead + (-1,)), w[..., n:].reshape(lead + (-1,))], axis=-1)
    parts = [w[..., :h * n].reshape(lead + (h, n)), w[..., h * n:].reshape(lead + (h, vd))]
    return jnp.concatenate(parts, axis=-1).reshape(lead + (-1,))


def _to_heads(parts):
    bsz, s = parts[0].shape[:2]
    t = jnp.concatenate([p.reshape(bsz, s, MLA_HEADS, -1) for p in parts], axis=-1)
    return t.transpose(0, 2, 1, 3)


def _from_heads(t, widths):
    bsz, h, s, _ = t.shape
    t = t.transpose(0, 2, 1, 3)
    out, off = [], 0
    for w in widths:
        out.append(t[..., off:off + w].reshape(bsz, s, h * w))
        off += w
    return out


def _mla_forward(h, w, tabs, ride=()):
    cos, sin = tabs
    hn, r2 = MLA_HEADS * QK_NOPE, MLA_HEADS * (QK_ROPE // 2)
    proj = mm3("mla_proj", h, w['w_in'])
    qn, kvn, krt = rowwise(
        "mla_mid", lambda rv, ev, gv: (f_mla_mid(rv, ev, gv), []),
        [_view(proj, 0, Q_LORA), _view(proj, Q_LORA, KV_LORA), _view(proj, Q_LORA + KV_LORA, LANES), _full(cos), _full(sin)],
        [], [w['q_norm'], w['kv_norm']], [(BF16, [Q_LORA]), (BF16, [KV_LORA]), (BF16, [r2, r2])], ts=ROW_TILE)
    q = mm3("mla_q", qn, w['w_qb'])
    kv = mm3("mla_kv", kvn, w['w_kvb'])
    qb, kvb = rowwise(
        "mla_post", lambda rv, ev, gv: (f_mla_post(rv, ev, gv), []),
        [_view(q, 0, hn), _view(q, hn, r2), _view(q, hn + r2, r2), _full(kv), _full(cos), _full(sin)],
        [], [], [(BF16, [hn, r2, r2]), (BF16, [kv.shape[-1]])], ts=ROW_TILE)
    qh = _to_heads([qb[..., :hn], qb[..., hn:hn + r2], qb[..., hn + r2:]])
    kh = _to_heads([kvb[..., :hn], krt[..., :r2], krt[..., r2:]])
    vh = _to_heads([kvb[..., hn:]])
    oh, got = attn_fwd(qh, kh, vh, ride)
    (o,) = _from_heads(oh, [V_HEAD])
    y = mm3("mla_out", o, w['w_o'])
    return y, dict(h=h, proj=proj, qn=qn, kvn=kvn, q=q, kv=kv, qh=qh, kh=kh, vh=vh, o=o), got


def _mla_backward(dy, sv, w, tabs, ride=()):
    cos, sin = tabs
    hn, r2 = MLA_HEADS * QK_NOPE, MLA_HEADS * (QK_ROPE // 2)
    g = {}
    g['w_o'] = wgrad("mla_out_dw", sv['o'], dy)
    do = mm3("mla_out_dx", dy, w['w_o'], tb=True)
    doh = _to_heads([do])
    dqh, dkh, dvh, got = attn_bwd(sv['qh'], sv['kh'], sv['vh'], doh, ride)
    dqb = jnp.concatenate(_from_heads(dqh, [QK_NOPE, QK_ROPE // 2, QK_ROPE // 2]), axis=-1)
    dkn, dk1, dk2 = _from_heads(dkh, [QK_NOPE, QK_ROPE // 2, QK_ROPE // 2])
    (dvv,) = _from_heads(dvh, [V_HEAD])
    dkvb = jnp.concatenate([dkn, dvv], axis=-1)
    dkrt = jnp.concatenate([dk1, dk2], axis=-1)
    q, kv = sv['q'], sv['kv']
    (dq, dkv), _, _ = rowwise_bwd(
        "mla_post_bwd", f_mla_post,
        [_view(q, 0, hn), _view(q, hn, r2), _view(q, hn + r2, r2), _full(kv), _full(cos), _full(sin)], [], [],
        [(dqb, [hn, r2, r2]), (dkvb, [kv.shape[-1]])], [(BF16, [0, 1, 2]), (BF16, [3])], ts=ROW_TILE, n_diff=4)
    g['w_qb'] = wgrad("mla_q_dw", sv['qn'], dq)
    g['w_kvb'] = wgrad("mla_kv_dw", sv['kvn'], dkv)
    dqn = mm3("mla_q_dx", dq, w['w_qb'], tb=True)
    dkvn = mm3("mla_kv_dx", dkv, w['w_kvb'], tb=True)
    proj = sv['proj']
    (dproj,), _, (g['q_norm'], g['kv_norm']) = rowwise_bwd(
        "mla_mid_bwd", f_mla_mid,
        [_view(proj, 0, Q_LORA), _view(proj, Q_LORA, KV_LORA), _view(proj, Q_LORA + KV_LORA, LANES), _full(cos), _full(sin)],
        [], [w['q_norm'], w['kv_norm']], [(dqn, [Q_LORA]), (dkvn, [KV_LORA]), (dkrt, [r2, r2])],
        [(BF16, [0, 1, 2])], ts=ROW_TILE, n_diff=3)
    g['w_in'] = wgrad("mla_proj_dw", sv['h'], dproj)
    dh = mm3("mla_proj_dx", dproj, w['w_in'], tb=True)
    return dh, g, got


def _hgrn_views(proj):
    d = proj.shape[-1] // 4
    return [_view(proj, i * d, d) for i in range(4)]


def _head_views(a, col0, nh):
    return [_view(a, col0 + i * HGRN_EXPAND, HGRN_EXPAND) for i in range(nh)]


def _hgrn_forward(h, w, ride=()):
    proj = mm3("hgrn_proj", h, w['w_in'])
    d = proj.shape[-1] // 4
    nh = d // HGRN_EXPAND
    vq, vf, _, _ = _hgrn_views(proj)
    qs, logf, kk = rowwise("hgrn_pre", lambda rv, ev, gv: (f_hg_pre(rv, ev, gv), []), [vq, vf], [], [w['lb']],
                           [(F32, [d]), (F32, [d]), (F32, [d])], ts=ROW_TILE)
    o, states, got = hgrn_fwd(qs, logf, kk, proj, ride)
    (z,) = rowwise("hgrn_post", lambda rv, ev, gv: (f_hg_post(rv, ev, gv), []),
                   _head_views(o, 0, nh) + _head_views(proj, 3 * d, nh), [], [w['g_norm']],
                   [(BF16, [HGRN_EXPAND] * nh)], ts=ROW_TILE)
    y = mm3("hgrn_out", z, w['w_o'])
    return y, dict(h=h, proj=proj, qs=qs, logf=logf, kk=kk, o=o, states=states, z=z), got


def _hgrn_backward(dy, sv, w, ride=()):
    g = {}
    proj, o = sv['proj'], sv['o']
    d = proj.shape[-1] // 4
    nh = d // HGRN_EXPAND
    g['w_o'] = wgrad("hgrn_out_dw", sv['z'], dy)
    dz = mm3("hgrn_out_dx", dy, w['w_o'], tb=True)
    (do, dgate), _, (g['g_norm'],) = rowwise_bwd(
        "hgrn_post_bwd", f_hg_post, _head_views(o, 0, nh) + _head_views(proj, 3 * d, nh), [], [w['g_norm']],
        [(dz, [HGRN_EXPAND] * nh)], [(F32, list(range(nh))), (F32, list(range(nh, 2 * nh)))], ts=ROW_TILE, n_diff=2 * nh)
    dqs, dlogf, dkk, dv, got = hgrn_bwd(sv['qs'], sv['logf'], sv['kk'], proj, sv['states'], do, ride)
    (dproj,), _, (g['lb'],) = rowwise_bwd(
        "hgrn_pre_bwd", f_hg_pre_with_iv, _hgrn_views(proj), [], [w['lb']],
        [(dqs, [d]), (dlogf, [d]), (dkk, [d]), (dv, [d]), (dgate, [d])], [(BF16, [0, 1, 2, 3])], ts=ROW_TILE // 2, n_diff=4)
    g['w_in'] = wgrad("hgrn_proj_dw", sv['h'], dproj)
    dh = mm3("hgrn_proj_dx", dproj, w['w_in'], tb=True)
    return dh, g, got


def _ffn_forward(h, w):
    u = mm3("ffn_in", h, w['w_in'])
    dff = u.shape[-1] // 2
    (a,) = rowwise("ffn_act", lambda rv, ev, gv: (f_swiglu(rv, ev, gv), []), [_view(u, 0, dff), _view(u, dff, dff)],
                   [], [], [(BF16, [dff])], ts=ROW_TILE)
    y = mm3("ffn_out", a, w['w_out'])
    return y, dict(h=h, u=u, a=a)


def _ffn_backward(dy, sv, w):
    g = {}
    u = sv['u']
    dff = u.shape[-1] // 2
    g['w_out'] = wgrad("ffn_out_dw", sv['a'], dy)
    da = mm3("ffn_out_dx", dy, w['w_out'], tb=True)
    (du,), _, _ = rowwise_bwd("ffn_act_bwd", f_swiglu, [_view(u, 0, dff), _view(u, dff, dff)], [], [],
                              [(da, [dff])], [(BF16, [0, 1])], ts=ROW_TILE // 2, n_diff=2)
    g['w_in'] = wgrad("ffn_in_dw", sv['h'], du)
    dh = mm3("ffn_in_dx", du, w['w_in'], tb=True)
    return dh, g


def kernel(x, c, positions, mla_w_in, mla_q_norm, mla_w_qb, mla_kv_norm, mla_w_kvb, mla_w_o, hgrn_lb, hgrn_w_in, hgrn_g_norm, hgrn_w_o, ffn_w_in, ffn_w_out, ada_w, ada_b, ln_g, ln_b, loss_target, m_mla_w_in, m_mla_q_norm, m_mla_w_qb, m_mla_kv_norm, m_mla_w_kvb, m_mla_w_o, m_hgrn_lb, m_hgrn_w_in, m_hgrn_g_norm, m_hgrn_w_o, m_ffn_w_in, m_ffn_w_out, m_ada_w, m_ada_b, m_ln_g, m_ln_b, v_mla_w_in, v_mla_q_norm, v_mla_w_qb, v_mla_kv_norm, v_mla_w_kvb, v_mla_w_o, v_hgrn_lb, v_hgrn_w_in, v_hgrn_g_norm, v_hgrn_w_o, v_ffn_w_in, v_ffn_w_out, v_ada_w, v_ada_b, v_ln_g, v_ln_b):
    W = dict(zip(WEIGHTS, (mla_w_in, mla_q_norm, mla_w_qb, mla_kv_norm, mla_w_kvb, mla_w_o, hgrn_lb, hgrn_w_in, hgrn_g_norm,
                           hgrn_w_o, ffn_w_in, ffn_w_out, ada_w, ada_b, ln_g, ln_b)))
    M1 = dict(zip(WEIGHTS, (m_mla_w_in, m_mla_q_norm, m_mla_w_qb, m_mla_kv_norm, m_mla_w_kvb, m_mla_w_o, m_hgrn_lb, m_hgrn_w_in,
                            m_hgrn_g_norm, m_hgrn_w_o, m_ffn_w_in, m_ffn_w_out, m_ada_w, m_ada_b, m_ln_g, m_ln_b)))
    M2 = dict(zip(WEIGHTS, (v_mla_w_in, v_mla_q_norm, v_mla_w_qb, v_mla_kv_norm, v_mla_w_kvb, v_mla_w_o, v_hgrn_lb, v_hgrn_w_in,
                            v_hgrn_g_norm, v_hgrn_w_o, v_ffn_w_in, v_ffn_w_out, v_ada_w, v_ada_b, v_ln_g, v_ln_b)))
    bsz, seq, d = x.shape
    depth, n_mla, n_hgrn = ffn_w_in.shape[0], mla_w_in.shape[0], hgrn_w_in.shape[0]
    n_sub = 2 * depth

    big = COL_SHARDED + ROW_SHARDED
    wb = {n: cast_bf16("cast_" + n, W[n]) for n in big}

    def layer_names(layer):
        mixer = ['mla_w_in', 'mla_w_qb', 'mla_w_kvb', 'mla_w_o'] if layer % 2 == 0 else ['hgrn_w_in', 'hgrn_w_o']
        return [(n, layer // 2) for n in mixer] + [('ffn_w_in', layer), ('ffn_w_out', layer)]

    def weight_items(layer):
        return [(wb[n][j], "gather") for n, j in layer_names(layer)]

    G = {}
    internal = {'mla_w_in': _w_in_internal, 'mla_w_qb': _qb_internal, 'mla_w_kvb': _kvb_internal}

    def take_weights(layer, got):
        for (n, j), a in zip(layer_names(layer), got):
            G[n, j] = internal.get(n, lambda w: w)(_from_gather(n, a))

    lower_shard = lower_bounds(hgrn_lb)
    got = exchange("gather_first", [(lower_shard, "gather"), (ln_g, "gather"), (ln_b, "gather"), (c, "gather")] + weight_items(0))
    lower_all = got[0].transpose(1, 0, 2).reshape(n_hgrn, -1)
    ln_g_all = got[1].transpose(1, 2, 0, 3).reshape(depth, 2, d)
    ln_b_all = got[2].transpose(1, 2, 0, 3).reshape(depth, 2, d)
    c_all = got[3].reshape(N_DEV * bsz, d)
    take_weights(0, got[4:])

    cols = ada_w.shape[-1]
    mod_loc = ada_fwd(c_all, ada_w, ada_b)
    (mod_got,) = exchange("scatter_mod", [(mod_loc.reshape(n_sub, N_DEV, bsz, cols).transpose(1, 0, 2, 3), "a2a")])
    mod = mod_got.transpose(1, 2, 0, 3).reshape(n_sub, bsz, 1, 3 * d)
    shift = [mod[k, :, :, 0:d] for k in range(n_sub)]
    scale = [mod[k, :, :, d:2 * d] for k in range(n_sub)]
    gate = [mod[k, :, :, 2 * d:] for k in range(n_sub)]
    lng = [ln_g_all[k // 2, k % 2][None, :] for k in range(n_sub)]
    lnb = [ln_b_all[k // 2, k % 2][None, :] for k in range(n_sub)]

    tabs = rope_tables(positions)

    def sub_weights(k):
        layer, j = k // 2, k // 4
        if k % 2:
            return 'ffn', layer, dict(w_in=G['ffn_w_in', layer], w_out=G['ffn_w_out', layer])
        if layer % 2 == 0:
            return 'mla', j, dict(w_in=G['mla_w_in', j], q_norm=mla_q_norm[j][None, :], w_qb=G['mla_w_qb', j],
                                  kv_norm=mla_kv_norm[j][None, :], w_kvb=G['mla_w_kvb', j], w_o=G['mla_w_o', j])
        return 'hgrn', j, dict(w_in=G['hgrn_w_in', j], lb=lower_all[j][None, :], g_norm=hgrn_g_norm[j][None, :],
                               w_o=G['hgrn_w_o', j])

    (h,) = rowwise("mod_first", lambda rv, ev, gv: (f_mod(rv, ev, gv), []), [_full(x)], [scale[0], shift[0]], [],
                   [(BF16, [d])], ts=ROW_TILE)
    xs, ys, saved = [x], [], []
    loss_acc = None
    for k in range(n_sub):
        kind, _, w = sub_weights(k)
        nxt = k // 2 + 1
        ride = weight_items(nxt) if (k % 2 == 0 and nxt < depth) else []
        if kind == 'ffn':
            y, sv = _ffn_forward(h, w)
        elif kind == 'mla':
            y, sv, got = _mla_forward(h, w, tabs, ride)
        else:
            y, sv, got = _hgrn_forward(h, w, ride)
        if ride:
            take_weights(nxt, got)
        ys.append(y)
        saved.append(sv)
        if k + 1 < n_sub:
            xn, h = rowwise("ln_mod", lambda rv, ev, gv: (f_ln_mod(rv, ev, gv), []), [_full(xs[k]), _full(y)],
                            [gate[k], scale[k + 1], shift[k + 1]], [lng[k], lnb[k]], [(F32, [d]), (BF16, [d])], ts=ROW_TILE)
            xs.append(xn)
        else:
            def loss_rows(rv, ev, gv):
                (row,) = f_ln_loss(rv, ev, gv)
                return [], [jnp.broadcast_to(jnp.sum(row, keepdims=True), (1, LANES))]
            (loss_acc,) = rowwise("ln_loss", loss_rows, [_full(xs[k]), _full(y), _full(loss_target)], [gate[k]], [lng[k], lnb[k]],
                                  [], ts=ROW_TILE, accs=[LANES])
    loss = lax.psum(loss_acc[0, 0], ("x", "y", "c"))

    d_shift, d_scale, d_gate = [None] * n_sub, [None] * n_sub, [None] * n_sub
    d_lng, d_lnb = [None] * n_sub, [None] * n_sub
    part = {n: [None] * W[n].shape[0] for n in ['mla_q_norm', 'mla_kv_norm', 'hgrn_g_norm']}
    recv = {n: [None] * W[n].shape[0] for n in big}
    d_lower = [None] * n_hgrn
    k = n_sub - 1
    (dx, dy), (d_gate[k],), (d_lng[k], d_lnb[k]) = rowwise_bwd(
        "ln_loss_bwd", f_ln_loss, [_full(xs[k]), _full(ys[k]), _full(loss_target)], [gate[k]], [lng[k], lnb[k]], [],
        [(F32, [0]), (BF16, [1])], ts=ROW_TILE, n_diff=2, unit_ct=1)
    grad_x = None
    waiting, mine = [], {}

    def take_grads(names, got):
        for (n, j), a in zip(names, got):
            recv[n][j] = a

    for k in range(n_sub - 1, -1, -1):
        kind, j, w = sub_weights(k)
        ride = [(_to_slabs(n, mine[n, jj]), "a2a") for n, jj in waiting] if kind != 'ffn' else []
        if kind == 'ffn':
            dh, g = _ffn_backward(dy, saved[k], w)
            new = {('ffn_w_in', j): g['w_in'], ('ffn_w_out', j): g['w_out']}
        elif kind == 'mla':
            dh, g, got = _mla_backward(dy, saved[k], w, tabs, ride)
            new = {('mla_w_in', j): g['w_in'][:, :mla_w_in.shape[-1] * N_DEV], ('mla_w_qb', j): _qb_internal(g['w_qb'], inverse=True),
                   ('mla_w_kvb', j): _kvb_internal(g['w_kvb'], inverse=True), ('mla_w_o', j): g['w_o']}
            part['mla_q_norm'][j], part['mla_kv_norm'][j] = g['q_norm'][0], g['kv_norm'][0]
        else:
            dh, g, got = _hgrn_backward(dy, saved[k], w, ride)
            new = {('hgrn_w_in', j): g['w_in'], ('hgrn_w_o', j): g['w_o']}
            part['hgrn_g_norm'][j] = g['g_norm'][0]
            d_lower[j] = g['lb'][0]
        if kind != 'ffn':
            take_grads(waiting, got)
            waiting = []
        mine.update(new)
        if kind != 'ffn':
            waiting = layer_names(k // 2)
        if k:
            (dx, dy), (d_gate[k - 1], d_scale[k], d_shift[k]), (d_lng[k - 1], d_lnb[k - 1]) = rowwise_bwd(
                "ln_mod_bwd", f_ln_mod, [_full(xs[k - 1]), _full(ys[k - 1])], [gate[k - 1], scale[k], shift[k]],
                [lng[k - 1], lnb[k - 1]], [(dx, [d]), (dh, [d])], [(F32, [0]), (BF16, [1])], ts=ROW_TILE, n_diff=2)
        else:
            (grad_x,), (d_scale[0], d_shift[0]), _ = rowwise_bwd(
                "mod_first_bwd", f_mod_with_x, [_full(x)], [scale[0], shift[0]], [], [(dh, [d]), (dx, [d])],
                [(F32, [0])], ts=ROW_TILE, n_diff=1)

    slabs = [(_to_slabs(n, mine[n, jj]), "a2a") for n, jj in waiting]
    slabs.append((jnp.stack(d_lower).reshape(n_hgrn, N_DEV, -1).transpose(1, 0, 2), "a2a"))
    for parts in (d_lng, d_lnb):
        full = jnp.stack([p[0] for p in parts]).reshape(depth, 2, N_DEV, d // N_DEV)
        slabs.append((full.transpose(2, 0, 1, 3), "a2a"))
    dmod = jnp.concatenate([jnp.stack(d_shift), jnp.stack(d_scale), jnp.stack(d_gate)], axis=-1)
    slabs.append((dmod.reshape(n_sub, bsz, N_DEV, cols).transpose(2, 0, 1, 3), "a2a"))
    small = ['mla_q_norm', 'mla_kv_norm', 'hgrn_g_norm']
    slabs += [(jnp.stack(part[n]), "gather") for n in small]
    got = exchange("scatter_last", slabs)
    take_grads(waiting, got)
    stacks = dict(zip(['hgrn_lb', 'ln_g', 'ln_b', 'dmod'] + small, got[len(waiting):]))
    for n in big:
        stacks[n] = jnp.stack(recv[n], axis=1)

    dmod_all = stacks['dmod'].transpose(1, 0, 2, 3).reshape(n_sub, N_DEV * bsz, cols)
    g_ada_w, g_ada_b = ada_bwd(c_all, dmod_all)
    stacks['ada_w'] = g_ada_w.reshape((1,) + ada_w.shape)
    stacks['ada_b'] = g_ada_b.reshape((1,) + ada_b.shape)

    res = {}
    for n in WEIGHTS:
        if n == 'hgrn_lb':
            res[n] = adam_lb(stacks[n], W[n], M1[n], M2[n])
        else:
            res[n] = adam("adam_" + n, stacks[n], W[n], M1[n], M2[n])
    return (loss, grad_x, *[res[n][0] for n in WEIGHTS], *[res[n][1] for n in WEIGHTS], *[res[n][2] for n in WEIGHTS],
            *[res[n][3] for n in WEIGHTS])
```

```python
import functools
import math

import numpy as np
import jax
import jax.numpy as jnp
from jax import lax
from jax.experimental import pallas as pl
from jax.experimental.pallas import tpu as pltpu

F32 = jnp.float32
BF16 = jnp.bfloat16

N_DEV = 8
LANES = 128
VMEM_LIMIT = 52 * 1024 * 1024

D_MODEL = 1024
DEPTH = 4
MLA_HEADS = 16
QK_NOPE = 64
QK_ROPE = 32
V_HEAD = 64
Q_LORA = 768
KV_LORA = 256
ROPE_THETA = 10000.0
HGRN_EXPAND = 128
HGRN_CHUNK = 128
HGRN_HEADS_PER_STEP = 2
D_FF = 2816
ALPHA = (2.0 * DEPTH) ** 0.25
LN_EPS = 1e-5
RMS_EPS = 1e-6
ADAM_LR = 0.001
ADAM_B1 = 0.9
ADAM_B2 = 0.999
ADAM_EPS = 1e-08
ADAM_WD = 0.01
ADAM_STEP = 10

ATTN_TQ = 256
ROW_TILE = 256

WEIGHTS = ['mla_w_in', 'mla_q_norm', 'mla_w_qb', 'mla_kv_norm', 'mla_w_kvb', 'mla_w_o', 'hgrn_lb', 'hgrn_w_in',
           'hgrn_g_norm', 'hgrn_w_o', 'ffn_w_in', 'ffn_w_out', 'ada_w', 'ada_b', 'ln_g', 'ln_b']
COL_SHARDED = ['mla_w_in', 'mla_w_qb', 'mla_w_kvb', 'hgrn_w_in', 'ffn_w_in']
ROW_SHARDED = ['mla_w_o', 'hgrn_w_o', 'ffn_w_out']


def _params(*sem):
    if sem:
        return pltpu.CompilerParams(dimension_semantics=sem, vmem_limit_bytes=VMEM_LIMIT)
    return pltpu.CompilerParams(vmem_limit_bytes=VMEM_LIMIT)


def _pick(n, cap):
    best = None
    for t in range(LANES, min(n, cap) + 1, LANES):
        if n % t == 0:
            best = t
    return best or n


def _pick_rows(n, cap):
    best = None
    for t in range(8, min(n, cap) + 1, 8):
        if n % t == 0:
            best = t
    return best or n


def matmul(name, a, b, *, ta=False, tb=False, out_dtype=F32, tm_cap=1024, tn_cap=1024, tk_cap=2048):
    (k1, m) = a.shape if ta else a.shape[::-1]
    (n, k2) = b.shape if tb else b.shape[::-1]
    assert k1 == k2, (name, a.shape, b.shape)
    tm, tn, tk = _pick(m, tm_cap), _pick(n, tn_cap), _pick(k1, tk_cap)
    nk = k1 // tk
    dims = (((0 if ta else 1,), (1 if tb else 0,)), ((), ()))

    def prod(a_ref, b_ref):
        return lax.dot_general(a_ref[...].astype(BF16), b_ref[...].astype(BF16), dims, preferred_element_type=F32)

    if nk == 1:
        def body(a_ref, b_ref, o_ref):
            o_ref[...] = prod(a_ref, b_ref).astype(o_ref.dtype)
        scratch = []
    else:
        def body(a_ref, b_ref, o_ref, acc_ref):
            k = pl.program_id(2)

            @pl.when(k == 0)
            def _():
                acc_ref[...] = jnp.zeros_like(acc_ref)

            acc_ref[...] += prod(a_ref, b_ref)

            @pl.when(k == nk - 1)
            def _():
                o_ref[...] = acc_ref[...].astype(o_ref.dtype)
        scratch = [pltpu.VMEM((tm, tn), F32)]

    a_spec = pl.BlockSpec((tk, tm), lambda i, j, k: (k, i)) if ta else pl.BlockSpec((tm, tk), lambda i, j, k: (i, k))
    b_spec = pl.BlockSpec((tn, tk), lambda i, j, k: (j, k)) if tb else pl.BlockSpec((tk, tn), lambda i, j, k: (k, j))
    return pl.pallas_call(
        body, name=name, grid=(m // tm, n // tn, nk),
        in_specs=[a_spec, b_spec], out_specs=pl.BlockSpec((tm, tn), lambda i, j, k: (i, j)),
        out_shape=jax.ShapeDtypeStruct((m, n), out_dtype), scratch_shapes=scratch,
        compiler_params=_params("parallel", "parallel", "arbitrary"),
    )(a, b)


def mm3(name, a3, w, **kw):
    bsz, s, k = a3.shape
    out = matmul(name, a3.reshape(bsz * s, k), w, **kw)
    return out.reshape(bsz, s, out.shape[-1])


def wgrad(name, a3, g3):
    bsz, s, k = a3.shape
    return matmul(name, a3.reshape(bsz * s, k), g3.reshape(bsz * s, g3.shape[-1]), ta=True, out_dtype=BF16)


def _dg(a, b, ca, cb, **kw):
    return lax.dot_general(a, b, (((ca,), (cb,)), ((), ())), preferred_element_type=F32, **kw)


@functools.partial(jax.custom_vjp, nondiff_argnums=(2, 3))
def bdot(a, b, ca, cb):
    return _dg(a.astype(BF16), b.astype(BF16), ca, cb)


def _bdot_fwd(a, b, ca, cb):
    return bdot(a, b, ca, cb), (a, b)


def _bdot_bwd(ca, cb, res, g):
    a, b = res
    a16, b16, g16 = a.astype(BF16), b.astype(BF16), g.astype(BF16)
    if ca == 1:
        da = _dg(g16, b16, 1, 1 if cb == 0 else 0)
    else:
        da = _dg(b16, g16, 1 if cb == 0 else 0, 1)
    if cb == 0:
        db = _dg(a16, g16, 0 if ca == 1 else 1, 0)
    else:
        db = _dg(g16, a16, 0, 0 if ca == 1 else 1)
    return da, db


bdot.defvjp(_bdot_fwd, _bdot_bwd)


def hdot(a, b, ca=1, cb=0):
    return _dg(a, b, ca, cb, precision=lax.Precision.HIGHEST)


def _row_specs(rows, exs, globs, ts):
    specs = [pl.BlockSpec((1, ts, w), lambda b, s, j=j: (b, s, j)) for (_, j, w) in rows]
    specs += [pl.BlockSpec((1, 1, e.shape[-1]), lambda b, s: (b, 0, 0)) for e in exs]
    specs += [pl.BlockSpec((1, g.shape[-1]), lambda b, s: (0, 0)) for g in globs]
    return specs


def _store_pieces(o_ref, pieces, widths):
    off = 0
    for p, w in zip(pieces, widths):
        o_ref[0, :, off:off + w] = p.astype(o_ref.dtype)
        off += w


def _load_pieces(c_ref, widths):
    out, off = [], 0
    for w in widths:
        out.append(c_ref[0, :, off:off + w].astype(F32))
        off += w
    return out


def rowwise(name, f, rows, exs, globs, outs, *, ts, accs=()):
    bsz, s = rows[0][0].shape[:2]
    ts = min(ts, s)
    n_r, n_e, n_g, n_o = len(rows), len(exs), len(globs), len(outs)

    def body(*refs):
        rv = [r[0].astype(F32) for r in refs[:n_r]]
        ev = [e[0] for e in refs[n_r:n_r + n_e]]
        gv = [g[...] for g in refs[n_r + n_e:n_r + n_e + n_g]]
        o_refs = refs[n_r + n_e + n_g:n_r + n_e + n_g + n_o]
        a_refs = refs[n_r + n_e + n_g + n_o:]
        pieces, sums = f(rv, ev, gv)
        idx = 0
        for o_ref, (_, ws) in zip(o_refs, outs):
            _store_pieces(o_ref, pieces[idx:idx + len(ws)], ws)
            idx += len(ws)
        if accs:
            @pl.when((pl.program_id(0) == 0) & (pl.program_id(1) == 0))
            def _():
                for a_ref in a_refs:
                    a_ref[...] = jnp.zeros_like(a_ref)
            for a_ref, val in zip(a_refs, sums):
                a_ref[...] += val

    out_specs = [pl.BlockSpec((1, ts, sum(ws)), lambda b, s: (b, s, 0)) for (_, ws) in outs]
    out_specs += [pl.BlockSpec((1, w), lambda b, s: (0, 0)) for w in accs]
    out_shape = [jax.ShapeDtypeStruct((bsz, s, sum(ws)), dt) for (dt, ws) in outs]
    out_shape += [jax.ShapeDtypeStruct((1, w), F32) for w in accs]
    return pl.pallas_call(
        body, name=name, grid=(bsz, s // ts),
        in_specs=_row_specs(rows, exs, globs, ts), out_specs=out_specs, out_shape=out_shape,
        compiler_params=_params("arbitrary", "arbitrary"),
    )(*[r[0] for r in rows], *exs, *globs)


def rowwise_bwd(name, f, rows, exs, globs, cts, d_groups, *, ts, n_diff, unit_ct=0):
    bsz, s = rows[0][0].shape[:2]
    ts = min(ts, s)
    n_r, n_e, n_g, n_c = len(rows), len(exs), len(globs), len(cts)
    n_d = len(d_groups)

    def body(*refs):
        rv = [r[0].astype(F32) for r in refs[:n_r]]
        ev = [e[0] for e in refs[n_r:n_r + n_e]]
        gv = [g[...] for g in refs[n_r + n_e:n_r + n_e + n_g]]
        base = n_r + n_e + n_g
        c_refs = refs[base:base + n_c]
        d_refs = refs[base + n_c:base + n_c + n_d]
        de_refs = refs[base + n_c + n_d:base + n_c + n_d + n_e]
        dg_refs = refs[base + n_c + n_d + n_e:]
        fixed = rv[n_diff:]
        out, vjp = jax.vjp(lambda r, e, g: f(r + fixed, e, g), rv[:n_diff], ev, gv)
        ct = []
        for c_ref, (_, ws) in zip(c_refs, cts):
            ct += _load_pieces(c_ref, ws)
        ct += [jnp.ones_like(o) for o in out[len(ct):]]
        assert len(ct) == len(out) and len(out) - unit_ct == sum(len(ws) for _, ws in cts), name
        d_r, d_e, d_g = vjp(ct)
        for d_ref, (_, idxs) in zip(d_refs, d_groups):
            _store_pieces(d_ref, [d_r[i] for i in idxs], [rows[i][2] for i in idxs])
        first_s = pl.program_id(1) == 0
        if n_e:
            @pl.when(first_s)
            def _():
                for r in de_refs:
                    r[...] = jnp.zeros_like(r)
            for r, val in zip(de_refs, d_e):
                r[0] += val
        if n_g:
            @pl.when(first_s & (pl.program_id(0) == 0))
            def _():
                for r in dg_refs:
                    r[...] = jnp.zeros_like(r)
            for r, val in zip(dg_refs, d_g):
                r[...] += val

    in_specs = _row_specs(rows, exs, globs, ts)
    in_specs += [pl.BlockSpec((1, ts, sum(ws)), lambda b, s: (b, s, 0)) for (_, ws) in cts]
    out_specs = [pl.BlockSpec((1, ts, sum(rows[i][2] for i in idxs)), lambda b, s: (b, s, 0)) for (_, idxs) in d_groups]
    out_specs += [pl.BlockSpec((1, 1, e.shape[-1]), lambda b, s: (b, 0, 0)) for e in exs]
    out_specs += [pl.BlockSpec((1, g.shape[-1]), lambda b, s: (0, 0)) for g in globs]
    out_shape = [jax.ShapeDtypeStruct((bsz, s, sum(rows[i][2] for i in idxs)), dt) for (dt, idxs) in d_groups]
    out_shape += [jax.ShapeDtypeStruct(e.shape, F32) for e in exs]
    out_shape += [jax.ShapeDtypeStruct(g.shape, F32) for g in globs]
    res = pl.pallas_call(
        body, name=name, grid=(bsz, s // ts),
        in_specs=in_specs, out_specs=out_specs, out_shape=out_shape,
        compiler_params=_params("arbitrary", "arbitrary"),
    )(*[r[0] for r in rows], *exs, *globs, *[c[0] for c in cts])
    return res[:n_d], res[n_d:n_d + n_e], res[n_d + n_e:]


def _full(a):
    return (a, 0, a.shape[-1])


def _view(a, col, w):
    assert col % w == 0
    return (a, col // w, w)


def _layer_norm(z, g, b):
    mu = jnp.mean(z, -1, keepdims=True)
    var = jnp.mean(jnp.square(z - mu), -1, keepdims=True)
    return (z - mu) * lax.rsqrt(var + LN_EPS) * g + b


def _rms_norm(z, g):
    ms = jnp.mean(jnp.square(z), -1, keepdims=True)
    return z * lax.rsqrt(ms + RMS_EPS) * g


def f_mod(rv, ev, gv):
    (x,), (scale, shift) = rv, ev
    return [x * (1.0 + scale) + shift]


def f_mod_with_x(rv, ev, gv):
    return f_mod(rv, ev, gv) + [rv[0]]


def f_ln_mod(rv, ev, gv):
    (x, y), (gate, scale, shift), (g, b) = rv, ev, gv
    xn = _layer_norm(ALPHA * x + (1.0 + gate) * y, g, b)
    return [xn, xn * (1.0 + scale) + shift]


def f_ln_loss(rv, ev, gv):
    (x, y, target), (gate,), (g, b) = rv, ev, gv
    xn = _layer_norm(ALPHA * x + (1.0 + gate) * y, g, b)
    return [0.5 * jnp.mean(jnp.square(xn - target), -1, keepdims=True)]


def f_swiglu(rv, ev, gv):
    gate, up = rv
    return [jax.nn.silu(gate) * up]


def _head_spread(width):
    r2 = QK_ROPE // 2
    j = lax.broadcasted_iota(jnp.int32, (LANES, width), 0)
    col = lax.broadcasted_iota(jnp.int32, (LANES, width), 1) % r2
    return (j == col).astype(F32), (j == col + r2).astype(F32)


def f_mla_mid(rv, ev, gv):
    (q_lat, kv_lat, kr, cos, sin), (q_g, kv_g) = rv, gv
    e1, e2 = _head_spread(cos.shape[-1])
    k1, k2 = hdot(kr, e1), hdot(kr, e2)
    return [_rms_norm(q_lat, q_g), _rms_norm(kv_lat, kv_g), k1 * cos - k2 * sin, k1 * sin + k2 * cos]


def f_mla_post(rv, ev, gv):
    q_nope, q1, q2, kv, cos, sin = rv
    return [q_nope, q1 * cos - q2 * sin, q1 * sin + q2 * cos, kv]


def f_hg_pre(rv, ev, gv):
    (q, fx), (lb,) = rv[:2], gv
    f = lb + (1.0 - lb) * jax.nn.sigmoid(fx)
    return [jax.nn.silu(q), jnp.log(f), 1.0 - f]


def f_hg_pre_with_iv(rv, ev, gv):
    return f_hg_pre(rv, ev, gv) + [rv[2], rv[3]]


def f_hg_post(rv, ev, gv):
    nh = len(rv) // 2
    (gn,) = gv
    return [_rms_norm(rv[h], gn) * jax.nn.silu(rv[nh + h]) for h in range(nh)]


def rope_tables(positions):
    bsz, s = positions.shape
    r2 = QK_ROPE // 2
    width = MLA_HEADS * r2
    inv = (ROPE_THETA ** (-np.arange(0, QK_ROPE, 2, dtype=np.float32) / QK_ROPE)).astype(np.float32)
    inv = jnp.asarray(np.tile(inv, MLA_HEADS)[None, :])
    ts = min(ROW_TILE, s)

    def body(p_ref, inv_ref, cos_ref, sin_ref):
        ang = p_ref[0].astype(F32) * inv_ref[...]
        cos_ref[0] = jnp.cos(ang)
        sin_ref[0] = jnp.sin(ang)

    spec = pl.BlockSpec((1, ts, width), lambda b, s: (b, s, 0))
    return pl.pallas_call(
        body, name="rope_tables", grid=(bsz, s // ts),
        in_specs=[pl.BlockSpec((1, ts, 1), lambda b, s: (b, s, 0)), pl.BlockSpec((1, width), lambda b, s: (0, 0))],
        out_specs=[spec, spec], out_shape=[jax.ShapeDtypeStruct((bsz, s, width), F32)] * 2,
        compiler_params=_params("arbitrary", "arbitrary"),
    )(positions[:, :, None], inv)


def _attn_probs(q, k, row0):
    scale = (QK_NOPE + QK_ROPE) ** -0.5
    s = _dg(q, k, 1, 1) * scale
    rows = row0 + lax.broadcasted_iota(jnp.int32, s.shape, 0)
    cols = lax.broadcasted_iota(jnp.int32, s.shape, 1)
    s = jnp.where(cols <= rows, s, jnp.finfo(F32).min)
    e = jnp.exp(s - jnp.max(s, -1, keepdims=True))
    return e / jnp.sum(e, -1, keepdims=True), scale


def attn_fwd(q, k, v, ride=()):
    bsz, h, s, dq = q.shape
    dv = v.shape[-1]
    tq = min(ATTN_TQ, s)
    grid = (bsz, h, s // tq)
    rd = _Ride(ride)

    def body(*refs):
        (q_ref, k_ref, v_ref), srcs, (o_ref,), outs, sems = rd.split(refs, 3, 1)
        rd.run(srcs, outs, sems, grid)
        for i in range(grid[2]):
            @pl.when(pl.program_id(2) == i)
            def _(i=i):
                kend = (i + 1) * tq
                p, _ = _attn_probs(q_ref[0, 0], k_ref[0, 0, :kend, :], i * tq)
                o_ref[0, 0] = _dg(p.astype(BF16), v_ref[0, 0, :kend, :], 1, 0).astype(o_ref.dtype)

    res = pl.pallas_call(
        body, name="attn_fwd", grid=grid,
        in_specs=[pl.BlockSpec((1, 1, tq, dq), lambda b, h, i: (b, h, i, 0)),
                  pl.BlockSpec((1, 1, s, dq), lambda b, h, i: (b, h, 0, 0)),
                  pl.BlockSpec((1, 1, s, dv), lambda b, h, i: (b, h, 0, 0))] + rd.in_specs,
        out_specs=[pl.BlockSpec((1, 1, tq, dv), lambda b, h, i: (b, h, i, 0))] + rd.out_specs,
        out_shape=[jax.ShapeDtypeStruct((bsz, h, s, dv), BF16)] + rd.out_shape, scratch_shapes=rd.scratch,
        compiler_params=_params("arbitrary", "arbitrary", "arbitrary"),
    )(q, k, v, *rd.srcs)
    return res[0], res[1:]


def attn_bwd(q, k, v, do, ride=()):
    bsz, h, s, dq = q.shape
    dv = v.shape[-1]
    tq = min(ATTN_TQ, s)
    grid = (bsz, h, s // tq)
    rd = _Ride(ride)

    def body(*refs):
        (q_ref, k_ref, v_ref, do_ref), srcs, (dq_ref, dk_ref, dv_ref), outs, sems = rd.split(refs, 4, 3)
        rd.run(srcs, outs, sems, grid)

        @pl.when(pl.program_id(2) == 0)
        def _():
            dk_ref[...] = jnp.zeros_like(dk_ref)
            dv_ref[...] = jnp.zeros_like(dv_ref)

        for i in range(grid[2]):
            @pl.when(pl.program_id(2) == i)
            def _(i=i):
                kend = (i + 1) * tq
                qv, kv, vv = q_ref[0, 0], k_ref[0, 0, :kend, :], v_ref[0, 0, :kend, :]
                p, scale = _attn_probs(qv, kv, i * tq)
                do16 = do_ref[0, 0].astype(BF16)
                dv_ref[0, 0, :kend, :] += _dg(p.astype(BF16), do16, 0, 0)
                dp = _dg(do16, vv, 1, 1)
                ds = (p * (dp - jnp.sum(dp * p, -1, keepdims=True)) * scale).astype(BF16)
                dq_ref[0, 0] = _dg(ds, kv, 1, 0)
                dk_ref[0, 0, :kend, :] += _dg(ds, qv, 0, 0)

    res = pl.pallas_call(
        body, name="attn_bwd", grid=grid,
        in_specs=[pl.BlockSpec((1, 1, tq, dq), lambda b, h, i: (b, h, i, 0)),
                  pl.BlockSpec((1, 1, s, dq), lambda b, h, i: (b, h, 0, 0)),
                  pl.BlockSpec((1, 1, s, dv), lambda b, h, i: (b, h, 0, 0)),
                  pl.BlockSpec((1, 1, tq, dv), lambda b, h, i: (b, h, i, 0))] + rd.in_specs,
        out_specs=[pl.BlockSpec((1, 1, tq, dq), lambda b, h, i: (b, h, i, 0)),
                   pl.BlockSpec((1, 1, s, dq), lambda b, h, i: (b, h, 0, 0)),
                   pl.BlockSpec((1, 1, s, dv), lambda b, h, i: (b, h, 0, 0))] + rd.out_specs,
        out_shape=[jax.ShapeDtypeStruct((bsz, h, s, dq), F32), jax.ShapeDtypeStruct((bsz, h, s, dq), F32),
                   jax.ShapeDtypeStruct((bsz, h, s, dv), F32)] + rd.out_shape, scratch_shapes=rd.scratch,
        compiler_params=_params("arbitrary", "arbitrary", "arbitrary"),
    )(q, k, v, do, *rd.srcs)
    return res[0], res[1], res[2], res[3:]


def _hgrn_tables(c):
    levels = c.bit_length() - 1
    assert 1 << levels == c
    r = np.arange(c)
    prefix, sign, mask = [r[:, None] >= r[None, :]], [], []
    for l in range(levels):
        ref = ((r >> (l + 1)) << (l + 1)) + (1 << l) - 1
        lower = ((r >> l) & 1) == 1
        prefix.append(r[None, :] <= ref[:, None])
        sign.append(np.broadcast_to(np.where(lower, 1.0, -1.0)[:, None], (c, LANES)))
        mask.append((((r[:, None] ^ r[None, :]) >> l) == 1) & lower[:, None])
    return (jnp.asarray(np.concatenate(prefix, 0), BF16), jnp.asarray(np.stack(sign), F32),
            jnp.asarray(np.stack(mask), F32))


def _const_specs(tables):
    return [pl.BlockSpec(t.shape, lambda b, h, i, nd=t.ndim: (0,) * nd) for t in tables]


def _split3(x):
    hi = x.astype(BF16)
    rest = x - hi.astype(F32)
    mid = rest.astype(BF16)
    return hi, mid, (rest - mid.astype(F32)).astype(BF16)


@functools.partial(jax.custom_vjp, nondiff_argnums=(2,))
def prefix_sums(p, g, n):
    k = g.shape[1]
    r = _dg(p, jnp.concatenate(_split3(g), axis=1), 1, 0)
    r = r[:, :k] + r[:, k:2 * k] + r[:, 2 * k:]
    c = r.shape[0] // n
    return tuple(r[i * c:(i + 1) * c] for i in range(n))


def _prefix_fwd(p, g, n):
    return prefix_sums(p, g, n), p


def _prefix_bwd(n, p, ct):
    ct = jnp.concatenate(ct, axis=0)
    k = ct.shape[1]
    r = _dg(p, jnp.concatenate(_split3(ct), axis=1), 0, 0)
    return jnp.zeros_like(p), r[:, :k] + r[:, k:2 * k] + r[:, 2 * k:]


prefix_sums.defvjp(_prefix_fwd, _prefix_bwd)


def _dot3_raw(a, b, ca, cb):
    ah = a.astype(BF16)
    al = (a - ah.astype(F32)).astype(BF16)
    bh = b.astype(BF16)
    bl = (b - bh.astype(F32)).astype(BF16)
    if (ca == 1 and a.shape[1] % LANES) or (cb == 1 and b.shape[1] % LANES):
        return _dg(ah, bh, ca, cb) + _dg(ah, bl, ca, cb) + _dg(al, bh, ca, cb)
    return _dg(jnp.concatenate([ah, ah, al], axis=ca), jnp.concatenate([bh, bl, bh], axis=cb), ca, cb)


@functools.partial(jax.custom_vjp, nondiff_argnums=(2, 3))
def dot3(a, b, ca, cb):
    return _dot3_raw(a, b, ca, cb)


def _dot3_fwd(a, b, ca, cb):
    return _dot3_raw(a, b, ca, cb), (a, b)


def _dot3_bwd(ca, cb, res, g):
    a, b = res
    if ca == 1:
        da = _dot3_raw(g, b, 1, 1 if cb == 0 else 0)
    else:
        da = _dot3_raw(b, g, 1 if cb == 0 else 0, 1)
    if cb == 0:
        db = _dot3_raw(a, g, 0 if ca == 1 else 1, 0)
    else:
        db = _dot3_raw(g, a, 0, 0 if ca == 1 else 1)
    return da, db


dot3.defvjp(_dot3_fwd, _dot3_bwd)


def _hgrn_chunk(q, g, k, v, st0, prefix, sign, mask):
    levels = len(mask)
    pre = prefix_sums(prefix, g, levels + 1)
    b = pre[0]
    o = bdot(q * jnp.exp(b), st0, 1, 1)
    att = None
    for l in range(levels):
        e = jnp.exp((b - pre[l + 1]) * sign[l])
        a = dot3(q * e, k * e, 1, 1) * mask[l]
        att = a if att is None else att + a
    o = o + bdot(att, v, 1, 0) + jnp.sum(q * k, -1, keepdims=True) * v
    total = jnp.sum(g, 0, keepdims=True)
    st1 = st0 * jnp.exp(total) + bdot(v, k * jnp.exp(total - b), 0, 0)
    return o, st1


def hgrn_fwd(qs, logf, kk, proj, ride=()):
    bsz, s, hk = qs.shape
    kd = HGRN_EXPAND
    c = min(HGRN_CHUNK, s)
    nh, nc = hk // kd, s // c
    hp = math.gcd(nh, HGRN_HEADS_PER_STEP)
    tables = _hgrn_tables(c)
    levels = tables[2].shape[0]
    grid = (bsz, nh // hp, nc)
    rd = _Ride(ride)

    def body(*refs):
        (q_ref, g_ref, k_ref, v_ref, p_ref, sg_ref, mk_ref), srcs, (o_ref, st_ref), outs, scratch = rd.split(refs, 7, 2)
        state = scratch[0]
        rd.run(srcs, outs, scratch, grid)

        @pl.when(pl.program_id(2) == 0)
        def _():
            state[...] = jnp.zeros_like(state)

        prefix, sign, mask = p_ref[...], [sg_ref[l] for l in range(levels)], [mk_ref[l] for l in range(levels)]
        for j in range(hp):
            cols = slice(j * kd, (j + 1) * kd)
            st0 = state[j]
            st_ref[0, j, 0] = st0
            o, st1 = _hgrn_chunk(q_ref[0, :, cols], g_ref[0, :, cols], k_ref[0, :, cols], v_ref[0, :, cols], st0,
                                 prefix, sign, mask)
            o_ref[0, :, cols] = o
            state[j] = st1

    blk = pl.BlockSpec((1, c, hp * kd), lambda b, h, i: (b, i, h))
    res = pl.pallas_call(
        body, name="hgrn_fwd", grid=grid,
        in_specs=[blk, blk, blk, pl.BlockSpec((1, c, hp * kd), lambda b, h, i: (b, i, 2 * (nh // hp) + h))]
        + _const_specs(tables) + rd.in_specs,
        out_specs=[blk, pl.BlockSpec((1, hp, 1, kd, kd), lambda b, h, i: (b, h, i, 0, 0))] + rd.out_specs,
        out_shape=[jax.ShapeDtypeStruct((bsz, s, hk), F32), jax.ShapeDtypeStruct((bsz, nh, nc, kd, kd), F32)] + rd.out_shape,
        scratch_shapes=[pltpu.VMEM((hp, kd, kd), F32)] + rd.scratch,
        compiler_params=_params("arbitrary", "arbitrary", "arbitrary"),
    )(qs, logf, kk, proj, *tables, *rd.srcs)
    return res[0], res[1], res[2:]


def hgrn_bwd(qs, logf, kk, proj, states, do, ride=()):
    bsz, s, hk = qs.shape
    kd = HGRN_EXPAND
    c = min(HGRN_CHUNK, s)
    nh, nc = hk // kd, s // c
    hp = math.gcd(nh, HGRN_HEADS_PER_STEP)
    tables = _hgrn_tables(c)
    levels = tables[2].shape[0]
    grid = (bsz, nh // hp, nc)
    rd = _Ride(ride)

    def body(*refs):
        ((q_ref, g_ref, k_ref, v_ref, st_ref, do_ref, p_ref, sg_ref, mk_ref), srcs, (dq_ref, dg_ref, dk_ref, dv_ref), outs,
         scratch) = rd.split(refs, 9, 4)
        dstate = scratch[0]
        rd.run(srcs, outs, scratch, grid)

        @pl.when(pl.program_id(2) == 0)
        def _():
            dstate[...] = jnp.zeros_like(dstate)

        prefix, sign, mask = p_ref[...], [sg_ref[l] for l in range(levels)], [mk_ref[l] for l in range(levels)]
        for j in range(hp):
            cols = slice(j * kd, (j + 1) * kd)
            _, vjp = jax.vjp(lambda q, g, k, v, st: _hgrn_chunk(q, g, k, v, st, prefix, sign, mask),
                             q_ref[0, :, cols], g_ref[0, :, cols], k_ref[0, :, cols], v_ref[0, :, cols], st_ref[0, j, 0])
            dq, dg, dk, dv, dst = vjp((do_ref[0, :, cols], dstate[j]))
            dq_ref[0, :, cols] = dq
            dg_ref[0, :, cols] = dg
            dk_ref[0, :, cols] = dk
            dv_ref[0, :, cols] = dv
            dstate[j] = dst

    blk = pl.BlockSpec((1, c, hp * kd), lambda b, h, i: (b, nc - 1 - i, h))
    shape = jax.ShapeDtypeStruct((bsz, s, hk), F32)
    res = pl.pallas_call(
        body, name="hgrn_bwd", grid=grid,
        in_specs=[blk, blk, blk, pl.BlockSpec((1, c, hp * kd), lambda b, h, i: (b, nc - 1 - i, 2 * (nh // hp) + h)),
                  pl.BlockSpec((1, hp, 1, kd, kd), lambda b, h, i: (b, h, nc - 1 - i, 0, 0)), blk]
        + _const_specs(tables) + rd.in_specs,
        out_specs=[blk] * 4 + rd.out_specs, out_shape=[shape] * 4 + rd.out_shape,
        scratch_shapes=[pltpu.VMEM((hp, kd, kd), F32)] + rd.scratch,
        compiler_params=_params("arbitrary", "arbitrary", "arbitrary"),
    )(qs, logf, kk, proj, states, do, *tables, *rd.srcs)
    return res[0], res[1], res[2], res[3], res[4:]


def cast_bf16(name, w):
    blk = pl.BlockSpec((1,) + w.shape[1:], lambda l: (l, 0, 0))

    def body(w_ref, o_ref):
        o_ref[...] = w_ref[...].astype(BF16)

    return pl.pallas_call(body, name=name, grid=(w.shape[0],), in_specs=[blk], out_specs=blk,
                          out_shape=jax.ShapeDtypeStruct(w.shape, BF16), compiler_params=_params("arbitrary"))(w)


def _lower_bounds(rows):
    m = functools.reduce(jnp.maximum, rows)
    e = [jnp.exp(r - m) for r in rows]
    z = functools.reduce(lambda a, b: a + b, e)
    soft = [x / z for x in e]
    out, run = [], jnp.zeros_like(rows[0])
    for sft in soft:
        run = run + sft
        out.append(run - soft[0])
    return out


def lower_bounds(lb):
    n = lb.shape[0]

    def body(lb_ref, o_ref):
        for i, r in enumerate(_lower_bounds([lb_ref[i:i + 1, :] for i in range(n)])):
            o_ref[i:i + 1, :] = r

    return pl.pallas_call(body, name="lower_bounds", out_shape=jax.ShapeDtypeStruct(lb.shape, F32),
                          compiler_params=_params())(lb)


def ada_fwd(c_all, ada_w, ada_b):
    nl, ns, d, cols = ada_w.shape
    n_ex = c_all.shape[0]

    def body(c_ref, w_ref, b_ref, o_ref):
        a = jax.nn.silu(c_ref[...]).astype(BF16)
        o_ref[0] = _dg(a, w_ref[0].astype(BF16), 1, 0) + b_ref[0]

    return pl.pallas_call(
        body, name="ada_fwd", grid=(nl * ns,),
        in_specs=[pl.BlockSpec((n_ex, d), lambda i: (0, 0)), pl.BlockSpec((1, d, cols), lambda i: (i, 0, 0)),
                  pl.BlockSpec((1, 1, cols), lambda i: (i, 0, 0))],
        out_specs=pl.BlockSpec((1, n_ex, cols), lambda i: (i, 0, 0)),
        out_shape=jax.ShapeDtypeStruct((nl * ns, n_ex, cols), F32), compiler_params=_params("arbitrary"),
    )(c_all, ada_w.reshape(nl * ns, d, cols), ada_b.reshape(nl * ns, 1, cols))


def ada_bwd(c_all, dmod):
    n, n_ex, cols = dmod.shape
    d = c_all.shape[1]

    def body(c_ref, g_ref, dw_ref, db_ref):
        a = jax.nn.silu(c_ref[...]).astype(BF16)
        g = g_ref[0]
        dw_ref[0] = _dg(a, g.astype(BF16), 0, 0)
        db_ref[0] = jnp.sum(g, 0, keepdims=True)

    return pl.pallas_call(
        body, name="ada_bwd", grid=(n,),
        in_specs=[pl.BlockSpec((n_ex, d), lambda i: (0, 0)), pl.BlockSpec((1, n_ex, cols), lambda i: (i, 0, 0))],
        out_specs=[pl.BlockSpec((1, d, cols), lambda i: (i, 0, 0)), pl.BlockSpec((1, 1, cols), lambda i: (i, 0, 0))],
        out_shape=[jax.ShapeDtypeStruct((n, d, cols), F32), jax.ShapeDtypeStruct((n, 1, cols), F32)],
        compiler_params=_params("arbitrary"),
    )(c_all, dmod)


def _adam_math(g, w, m, v):
    m = ADAM_B1 * m + (1.0 - ADAM_B1) * g
    v = ADAM_B2 * v + (1.0 - ADAM_B2) * jnp.square(g)
    m_hat = m / (1.0 - ADAM_B1 ** ADAM_STEP)
    v_hat = v / (1.0 - ADAM_B2 ** ADAM_STEP)
    delta = -ADAM_LR * (m_hat / (jnp.sqrt(v_hat) + ADAM_EPS) + ADAM_WD * w)
    return delta, m, v


def adam(name, gstack, w, m, v):
    shape = w.shape
    n, cols = gstack.shape[0], shape[-1]
    rows = math.prod(shape[:-1])
    tr = _pick_rows(rows, max(8, (2 * 1024 * 1024) // (4 * cols * n)))

    def body(g_ref, w_ref, m_ref, v_ref, go_ref, d_ref, mo_ref, vo_ref):
        g = g_ref[0].astype(F32)
        for i in range(1, n):
            g = g + g_ref[i].astype(F32)
        delta, m1, v1 = _adam_math(g, w_ref[...], m_ref[...], v_ref[...])
        go_ref[...] = g
        d_ref[...] = delta
        mo_ref[...] = m1
        vo_ref[...] = v1

    blk = pl.BlockSpec((tr, cols), lambda i: (i, 0))
    out = pl.pallas_call(
        body, name=name, grid=(rows // tr,),
        in_specs=[pl.BlockSpec((n, tr, cols), lambda i: (0, i, 0)), blk, blk, blk],
        out_specs=[blk] * 4, out_shape=[jax.ShapeDtypeStruct((rows, cols), F32)] * 4,
        compiler_params=_params("arbitrary"),
    )(gstack.reshape(n, rows, cols), w.reshape(rows, cols), m.reshape(rows, cols), v.reshape(rows, cols))
    return [o.reshape(shape) for o in out]


def adam_layers(name, gs, w, m, v):
    shape = w.shape
    nl, n, cols = len(gs), gs[0].shape[0], shape[-1]
    rows = math.prod(shape[1:-1])
    tr = _pick_rows(rows, max(16, (2 * 1024 * 1024) // (4 * cols * n)))
    nt = rows // tr

    def body(*refs):
        g_refs, (w_ref, m_ref, v_ref, go_ref, d_ref, mo_ref, vo_ref) = refs[:nl], refs[nl:]
        for j in range(nl):
            @pl.when(pl.program_id(0) == j)
            def _(j=j):
                g = g_refs[j][0].astype(F32)
                for i in range(1, n):
                    g = g + g_refs[j][i].astype(F32)
                delta, m1, v1 = _adam_math(g, w_ref[...], m_ref[...], v_ref[...])
                go_ref[...] = g
                d_ref[...] = delta
                mo_ref[...] = m1
                vo_ref[...] = v1

    def g_spec(j):
        return pl.BlockSpec((n, tr, cols), lambda l, i: (0, jnp.where(l == j, i, jnp.where(l < j, 0, nt - 1)), 0))

    blk = pl.BlockSpec((tr, cols), lambda l, i: (l * nt + i, 0))
    out = pl.pallas_call(
        body, name=name, grid=(nl, nt),
        in_specs=[g_spec(j) for j in range(nl)] + [blk, blk, blk],
        out_specs=[blk] * 4, out_shape=[jax.ShapeDtypeStruct((nl * rows, cols), F32)] * 4,
        compiler_params=_params("arbitrary", "arbitrary"),
    )(*[g.reshape(n, rows, cols) for g in gs], w.reshape(nl * rows, cols), m.reshape(nl * rows, cols), v.reshape(nl * rows, cols))
    return [o.reshape(shape) for o in out]


def adam_lb(gstack, lb, m, v):
    n, nl = gstack.shape[0], lb.shape[0]

    def body(g_ref, w_ref, m_ref, v_ref, go_ref, d_ref, mo_ref, vo_ref):
        rows = [w_ref[i:i + 1, :] for i in range(nl)]
        ct = []
        for i in range(nl):
            g = g_ref[0, i:i + 1, :]
            for j in range(1, n):
                g = g + g_ref[j, i:i + 1, :]
            ct.append(g)
        _, vjp = jax.vjp(_lower_bounds, rows)
        (grads,) = vjp(ct)
        for i in range(nl):
            delta, m1, v1 = _adam_math(grads[i], rows[i], m_ref[i:i + 1, :], v_ref[i:i + 1, :])
            go_ref[i:i + 1, :] = grads[i]
            d_ref[i:i + 1, :] = delta
            mo_ref[i:i + 1, :] = m1
            vo_ref[i:i + 1, :] = v1

    return pl.pallas_call(body, name="adam_hgrn_lb", out_shape=[jax.ShapeDtypeStruct(lb.shape, F32)] * 4,
                          compiler_params=_params())(gstack, lb, m, v)


class _Ride:
    def __init__(self, items):
        self.items = list(items)
        n = len(self.items)
        self.srcs = [src for src, _ in self.items]
        self.in_specs = [pl.BlockSpec(memory_space=pl.ANY)] * n
        self.out_specs = [pl.BlockSpec(memory_space=pl.ANY)] * n
        self.out_shape = [jax.ShapeDtypeStruct(((N_DEV,) + s.shape) if mode == "gather" else s.shape, s.dtype)
                          for s, mode in self.items]
        self.scratch = [pltpu.SemaphoreType.DMA((n, N_DEV - 1)), pltpu.SemaphoreType.DMA((n, N_DEV - 1)),
                        pltpu.SemaphoreType.DMA((n,))] if n else []

    def split(self, refs, n_in, n_out):
        n = len(self.items)
        a, b = n_in + n, n_in + 2 * n + n_out
        return refs[:n_in], refs[n_in:a], refs[a:a + n_out], refs[a + n_out:b], refs[b:]

    def _copies(self, srcs, outs, sems):
        send_sems, recv_sems, local_sems = sems
        x, y, c = lax.axis_index("x"), lax.axis_index("y"), lax.axis_index("c")
        me = 4 * x + 2 * y + c
        copies = []
        for i, (_, mode) in enumerate(self.items):
            mine = srcs[i] if mode == "gather" else srcs[i].at[me]
            copies.append(pltpu.make_async_copy(mine, outs[i].at[me], local_sems.at[i]))
            for p in range(1, N_DEV):
                px = 1 - x if p & 4 else x
                py = 1 - y if p & 2 else y
                pc = 1 - c if p & 1 else c
                part = srcs[i] if mode == "gather" else srcs[i].at[4 * px + 2 * py + pc]
                copies.append(pltpu.make_async_remote_copy(
                    src_ref=part, dst_ref=outs[i].at[me], send_sem=send_sems.at[i, p - 1], recv_sem=recv_sems.at[i, p - 1],
                    device_id=(px, py, pc), device_id_type=pl.DeviceIdType.MESH))
        return copies

    def run(self, srcs, outs, scratch, grid=()):
        if not self.items:
            return
        sems = scratch[len(scratch) - 3:]
        if not grid:
            copies = self._copies(srcs, outs, sems)
            for cp in copies:
                cp.start()
            for cp in copies:
                cp.wait()
            return
        ids = [pl.program_id(a) for a in range(len(grid))]
        first = functools.reduce(lambda a, b: a & b, [i == 0 for i in ids])
        last = functools.reduce(lambda a, b: a & b, [i == g - 1 for i, g in zip(ids, grid)])

        @pl.when(first)
        def _():
            for cp in self._copies(srcs, outs, sems):
                cp.start()

        @pl.when(last)
        def _():
            for cp in self._copies(srcs, outs, sems):
                cp.wait()


def exchange(name, items):
    rd = _Ride(items)

    def body(*refs):
        _, srcs, _, outs, scratch = rd.split(refs, 0, 0)
        rd.run(srcs, outs, scratch)

    return pl.pallas_call(body, name=name, in_specs=rd.in_specs, out_specs=rd.out_specs, out_shape=rd.out_shape,
                          scratch_shapes=rd.scratch)(*rd.srcs)


def _from_gather(name, g):
    if name in COL_SHARDED:
        _, k, n = g.shape
        return g.transpose(1, 0, 2).reshape(k, N_DEV * n)
    return g.reshape(-1, g.shape[-1])


def _to_slabs(name, w):
    k, n = w.shape
    if name in COL_SHARDED:
        return w.reshape(k, N_DEV, n // N_DEV).transpose(1, 0, 2)
    return w.reshape(N_DEV, k // N_DEV, n)


def _w_in_internal(w):
    return jnp.pad(w, ((0, 0), (0, LANES - QK_ROPE)))


def _qb_internal(w, inverse=False):
    h, n, r2 = MLA_HEADS, QK_NOPE, QK_ROPE // 2
    lead = w.shape[:-1]
    if not inverse:
        w = w.reshape(lead + (h, n + 2 * r2))
        parts = [w[..., :n], w[..., n:n + r2], w[..., n + r2:]]
        return jnp.concatenate([p.reshape(lead + (-1,)) for p in parts], axis=-1)
    parts = [w[..., :h * n].reshape(lead + (h, n)), w[..., h * n:h * (n + r2)].reshape(lead + (h, r2)),
             w[..., h * (n + r2):].reshape(lead + (h, r2))]
    return jnp.concatenate(parts, axis=-1).reshape(lead + (-1,))


def _kvb_internal(w, inverse=False):
    h, n, vd = MLA_HEADS, QK_NOPE, V_HEAD
    lead = w.shape[:-1]
    if not inverse:
        w = w.reshape(lead + (h, n + vd))
        return jnp.concatenate([w[..., :n].reshape(lead + (-1,)), w[..., n:].reshape(lead + (-1,))], axis=-1)
    parts = [w[..., :h * n].reshape(lead + (h, n)), w[..., h * n:].reshape(lead + (h, vd))]
    return jnp.concatenate(parts, axis=-1).reshape(lead + (-1,))


def _to_heads(parts):
    bsz, s = parts[0].shape[:2]
    t = jnp.concatenate([p.reshape(bsz, s, MLA_HEADS, -1) for p in parts], axis=-1)
    return t.transpose(0, 2, 1, 3)


def _from_heads(t, widths):
    bsz, h, s, _ = t.shape
    t = t.transpose(0, 2, 1, 3)
    out, off = [], 0
    for w in widths:
        out.append(t[..., off:off + w].reshape(bsz, s, h * w))
        off += w
    return out


def _mla_forward(h, w, tabs, ride=()):
    cos, sin = tabs
    hn, r2 = MLA_HEADS * QK_NOPE, MLA_HEADS * (QK_ROPE // 2)
    proj = mm3("mla_proj", h, w['w_in'])
    qn, kvn, krt = rowwise(
        "mla_mid", lambda rv, ev, gv: (f_mla_mid(rv, ev, gv), []),
        [_view(proj, 0, Q_LORA), _view(proj, Q_LORA, KV_LORA), _view(proj, Q_LORA + KV_LORA, LANES), _full(cos), _full(sin)],
        [], [w['q_norm'], w['kv_norm']], [(BF16, [Q_LORA]), (BF16, [KV_LORA]), (BF16, [r2, r2])], ts=ROW_TILE)
    q = mm3("mla_q", qn, w['w_qb'])
    kv = mm3("mla_kv", kvn, w['w_kvb'])
    qb, kvb = rowwise(
        "mla_post", lambda rv, ev, gv: (f_mla_post(rv, ev, gv), []),
        [_view(q, 0, hn), _view(q, hn, r2), _view(q, hn + r2, r2), _full(kv), _full(cos), _full(sin)],
        [], [], [(BF16, [hn, r2, r2]), (BF16, [kv.shape[-1]])], ts=ROW_TILE)
    qh = _to_heads([qb[..., :hn], qb[..., hn:hn + r2], qb[..., hn + r2:]])
    kh = _to_heads([kvb[..., :hn], krt[..., :r2], krt[..., r2:]])
    vh = _to_heads([kvb[..., hn:]])
    oh, got = attn_fwd(qh, kh, vh, ride)
    (o,) = _from_heads(oh, [V_HEAD])
    y = mm3("mla_out", o, w['w_o'])
    return y, dict(h=h, proj=proj, qn=qn, kvn=kvn, q=q, kv=kv, qh=qh, kh=kh, vh=vh, o=o), got


def _mla_backward(dy, sv, w, tabs, ride=()):
    cos, sin = tabs
    hn, r2 = MLA_HEADS * QK_NOPE, MLA_HEADS * (QK_ROPE // 2)
    g = {}
    g['w_o'] = wgrad("mla_out_dw", sv['o'], dy)
    do = mm3("mla_out_dx", dy, w['w_o'], tb=True)
    doh = _to_heads([do])
    dqh, dkh, dvh, got = attn_bwd(sv['qh'], sv['kh'], sv['vh'], doh, ride)
    dqb = jnp.concatenate(_from_heads(dqh, [QK_NOPE, QK_ROPE // 2, QK_ROPE // 2]), axis=-1)
    dkn, dk1, dk2 = _from_heads(dkh, [QK_NOPE, QK_ROPE // 2, QK_ROPE // 2])
    (dvv,) = _from_heads(dvh, [V_HEAD])
    dkvb = jnp.concatenate([dkn, dvv], axis=-1)
    dkrt = jnp.concatenate([dk1, dk2], axis=-1)
    q, kv = sv['q'], sv['kv']
    (dq, dkv), _, _ = rowwise_bwd(
        "mla_post_bwd", f_mla_post,
        [_view(q, 0, hn), _view(q, hn, r2), _view(q, hn + r2, r2), _full(kv), _full(cos), _full(sin)], [], [],
        [(dqb, [hn, r2, r2]), (dkvb, [kv.shape[-1]])], [(BF16, [0, 1, 2]), (BF16, [3])], ts=ROW_TILE, n_diff=4)
    g['w_qb'] = wgrad("mla_q_dw", sv['qn'], dq)
    g['w_kvb'] = wgrad("mla_kv_dw", sv['kvn'], dkv)
    dqn = mm3("mla_q_dx", dq, w['w_qb'], tb=True)
    dkvn = mm3("mla_kv_dx", dkv, w['w_kvb'], tb=True)
    proj = sv['proj']
    (dproj,), _, (g['q_norm'], g['kv_norm']) = rowwise_bwd(
        "mla_mid_bwd", f_mla_mid,
        [_view(proj, 0, Q_LORA), _view(proj, Q_LORA, KV_LORA), _view(proj, Q_LORA + KV_LORA, LANES), _full(cos), _full(sin)],
        [], [w['q_norm'], w['kv_norm']], [(dqn, [Q_LORA]), (dkvn, [KV_LORA]), (dkrt, [r2, r2])],
        [(BF16, [0, 1, 2])], ts=ROW_TILE, n_diff=3)
    g['w_in'] = wgrad("mla_proj_dw", sv['h'], dproj)
    dh = mm3("mla_proj_dx", dproj, w['w_in'], tb=True)
    return dh, g, got


def _hgrn_views(proj):
    d = proj.shape[-1] // 4
    return [_view(proj, i * d, d) for i in range(4)]


def _head_views(a, col0, nh):
    return [_view(a, col0 + i * HGRN_EXPAND, HGRN_EXPAND) for i in range(nh)]


def _hgrn_forward(h, w, ride=()):
    proj = mm3("hgrn_proj", h, w['w_in'])
    d = proj.shape[-1] // 4
    nh = d // HGRN_EXPAND
    vq, vf, _, _ = _hgrn_views(proj)
    qs, logf, kk = rowwise("hgrn_pre", lambda rv, ev, gv: (f_hg_pre(rv, ev, gv), []), [vq, vf], [], [w['lb']],
                           [(F32, [d]), (F32, [d]), (F32, [d])], ts=ROW_TILE)
    o, states, got = hgrn_fwd(qs, logf, kk, proj, ride)
    (z,) = rowwise("hgrn_post", lambda rv, ev, gv: (f_hg_post(rv, ev, gv), []),
                   _head_views(o, 0, nh) + _head_views(proj, 3 * d, nh), [], [w['g_norm']],
                   [(BF16, [HGRN_EXPAND] * nh)], ts=ROW_TILE)
    y = mm3("hgrn_out", z, w['w_o'])
    return y, dict(h=h, proj=proj, qs=qs, logf=logf, kk=kk, o=o, states=states, z=z), got


def _hgrn_backward(dy, sv, w, ride=()):
    g = {}
    proj, o = sv['proj'], sv['o']
    d = proj.shape[-1] // 4
    nh = d // HGRN_EXPAND
    g['w_o'] = wgrad("hgrn_out_dw", sv['z'], dy)
    dz = mm3("hgrn_out_dx", dy, w['w_o'], tb=True)
    (do, dgate), _, (g['g_norm'],) = rowwise_bwd(
        "hgrn_post_bwd", f_hg_post, _head_views(o, 0, nh) + _head_views(proj, 3 * d, nh), [], [w['g_norm']],
        [(dz, [HGRN_EXPAND] * nh)], [(F32, list(range(nh))), (F32, list(range(nh, 2 * nh)))], ts=ROW_TILE, n_diff=2 * nh)
    dqs, dlogf, dkk, dv, got = hgrn_bwd(sv['qs'], sv['logf'], sv['kk'], proj, sv['states'], do, ride)
    (dproj,), _, (g['lb'],) = rowwise_bwd(
        "hgrn_pre_bwd", f_hg_pre_with_iv, _hgrn_views(proj), [], [w['lb']],
        [(dqs, [d]), (dlogf, [d]), (dkk, [d]), (dv, [d]), (dgate, [d])], [(BF16, [0, 1, 2, 3])], ts=ROW_TILE // 2, n_diff=4)
    g['w_in'] = wgrad("hgrn_proj_dw", sv['h'], dproj)
    dh = mm3("hgrn_proj_dx", dproj, w['w_in'], tb=True)
    return dh, g, got


def _ffn_forward(h, w):
    u = mm3("ffn_in", h, w['w_in'])
    dff = u.shape[-1] // 2
    (a,) = rowwise("ffn_act", lambda rv, ev, gv: (f_swiglu(rv, ev, gv), []), [_view(u, 0, dff), _view(u, dff, dff)],
                   [], [], [(BF16, [dff])], ts=ROW_TILE)
    y = mm3("ffn_out", a, w['w_out'])
    return y, dict(h=h, u=u, a=a)


def _ffn_backward(dy, sv, w):
    g = {}
    u = sv['u']
    dff = u.shape[-1] // 2
    g['w_out'] = wgrad("ffn_out_dw", sv['a'], dy)
    da = mm3("ffn_out_dx", dy, w['w_out'], tb=True)
    (du,), _, _ = rowwise_bwd("ffn_act_bwd", f_swiglu, [_view(u, 0, dff), _view(u, dff, dff)], [], [],
                              [(da, [dff])], [(BF16, [0, 1])], ts=ROW_TILE // 2, n_diff=2)
    g['w_in'] = wgrad("ffn_in_dw", sv['h'], du)
    dh = mm3("ffn_in_dx", du, w['w_in'], tb=True)
    return dh, g


def kernel(x, c, positions, mla_w_in, mla_q_norm, mla_w_qb, mla_kv_norm, mla_w_kvb, mla_w_o, hgrn_lb, hgrn_w_in, hgrn_g_norm, hgrn_w_o, ffn_w_in, ffn_w_out, ada_w, ada_b, ln_g, ln_b, loss_target, m_mla_w_in, m_mla_q_norm, m_mla_w_qb, m_mla_kv_norm, m_mla_w_kvb, m_mla_w_o, m_hgrn_lb, m_hgrn_w_in, m_hgrn_g_norm, m_hgrn_w_o, m_ffn_w_in, m_ffn_w_out, m_ada_w, m_ada_b, m_ln_g, m_ln_b, v_mla_w_in, v_mla_q_norm, v_mla_w_qb, v_mla_kv_norm, v_mla_w_kvb, v_mla_w_o, v_hgrn_lb, v_hgrn_w_in, v_hgrn_g_norm, v_hgrn_w_o, v_ffn_w_in, v_ffn_w_out, v_ada_w, v_ada_b, v_ln_g, v_ln_b):
    W = dict(zip(WEIGHTS, (mla_w_in, mla_q_norm, mla_w_qb, mla_kv_norm, mla_w_kvb, mla_w_o, hgrn_lb, hgrn_w_in, hgrn_g_norm,
                           hgrn_w_o, ffn_w_in, ffn_w_out, ada_w, ada_b, ln_g, ln_b)))
    M1 = dict(zip(WEIGHTS, (m_mla_w_in, m_mla_q_norm, m_mla_w_qb, m_mla_kv_norm, m_mla_w_kvb, m_mla_w_o, m_hgrn_lb, m_hgrn_w_in,
                            m_hgrn_g_norm, m_hgrn_w_o, m_ffn_w_in, m_ffn_w_out, m_ada_w, m_ada_b, m_ln_g, m_ln_b)))
    M2 = dict(zip(WEIGHTS, (v_mla_w_in, v_mla_q_norm, v_mla_w_qb, v_mla_kv_norm, v_mla_w_kvb, v_mla_w_o, v_hgrn_lb, v_hgrn_w_in,
                            v_hgrn_g_norm, v_hgrn_w_o, v_ffn_w_in, v_ffn_w_out, v_ada_w, v_ada_b, v_ln_g, v_ln_b)))
    bsz, seq, d = x.shape
    depth, n_mla, n_hgrn = ffn_w_in.shape[0], mla_w_in.shape[0], hgrn_w_in.shape[0]
    n_sub = 2 * depth

    big = COL_SHARDED + ROW_SHARDED
    wb = {n: cast_bf16("cast_" + n, W[n]) for n in big}

    def mixer_names(layer):
        mixer = ['mla_w_in', 'mla_w_qb', 'mla_w_kvb', 'mla_w_o'] if layer % 2 == 0 else ['hgrn_w_in', 'hgrn_w_o']
        return [(n, layer // 2) for n in mixer]

    def carried(layer):
        return [('ffn_w_in', layer), ('ffn_w_out', layer)] + (mixer_names(layer + 1) if layer + 1 < depth else [])

    def weight_items(names):
        return [(wb[n][j], "gather") for n, j in names]

    G = {}
    internal = {'mla_w_in': _w_in_internal, 'mla_w_qb': _qb_internal, 'mla_w_kvb': _kvb_internal}

    def take_weights(names, got):
        for (n, j), a in zip(names, got):
            G[n, j] = internal.get(n, lambda w: w)(_from_gather(n, a))

    lower_shard = lower_bounds(hgrn_lb)
    got = exchange("gather_first", [(lower_shard, "gather"), (ln_g, "gather"), (ln_b, "gather"), (c, "gather")]
                   + weight_items(mixer_names(0)))
    lower_all = got[0].transpose(1, 0, 2).reshape(n_hgrn, -1)
    ln_g_all = got[1].transpose(1, 2, 0, 3).reshape(depth, 2, d)
    ln_b_all = got[2].transpose(1, 2, 0, 3).reshape(depth, 2, d)
    c_all = got[3].reshape(N_DEV * bsz, d)
    take_weights(mixer_names(0), got[4:])

    cols = ada_w.shape[-1]
    mod_loc = ada_fwd(c_all, ada_w, ada_b)
    (mod_got,) = exchange("scatter_mod", [(mod_loc.reshape(n_sub, N_DEV, bsz, cols).transpose(1, 0, 2, 3), "a2a")])
    mod = mod_got.transpose(1, 2, 0, 3).reshape(n_sub, bsz, 1, 3 * d)
    shift = [mod[k, :, :, 0:d] for k in range(n_sub)]
    scale = [mod[k, :, :, d:2 * d] for k in range(n_sub)]
    gate = [mod[k, :, :, 2 * d:] for k in range(n_sub)]
    lng = [ln_g_all[k // 2, k % 2][None, :] for k in range(n_sub)]
    lnb = [ln_b_all[k // 2, k % 2][None, :] for k in range(n_sub)]

    tabs = rope_tables(positions)

    def sub_weights(k):
        layer, j = k // 2, k // 4
        if k % 2:
            return 'ffn', layer, dict(w_in=G['ffn_w_in', layer], w_out=G['ffn_w_out', layer])
        if layer % 2 == 0:
            return 'mla', j, dict(w_in=G['mla_w_in', j], q_norm=mla_q_norm[j][None, :], w_qb=G['mla_w_qb', j],
                                  kv_norm=mla_kv_norm[j][None, :], w_kvb=G['mla_w_kvb', j], w_o=G['mla_w_o', j])
        return 'hgrn', j, dict(w_in=G['hgrn_w_in', j], lb=lower_all[j][None, :], g_norm=hgrn_g_norm[j][None, :],
                               w_o=G['hgrn_w_o', j])

    (h,) = rowwise("mod_first", lambda rv, ev, gv: (f_mod(rv, ev, gv), []), [_full(x)], [scale[0], shift[0]], [],
                   [(BF16, [d])], ts=ROW_TILE)
    xs, ys, saved = [x], [], []
    loss_acc = None
    for k in range(n_sub):
        kind, _, w = sub_weights(k)
        ride = weight_items(carried(k // 2)) if k % 2 == 0 else []
        if kind == 'ffn':
            y, sv = _ffn_forward(h, w)
        elif kind == 'mla':
            y, sv, got = _mla_forward(h, w, tabs, ride)
        else:
            y, sv, got = _hgrn_forward(h, w, ride)
        if ride:
            take_weights(carried(k // 2), got)
        ys.append(y)
        saved.append(sv)
        if k + 1 < n_sub:
            xn, h = rowwise("ln_mod", lambda rv, ev, gv: (f_ln_mod(rv, ev, gv), []), [_full(xs[k]), _full(y)],
                            [gate[k], scale[k + 1], shift[k + 1]], [lng[k], lnb[k]], [(F32, [d]), (BF16, [d])], ts=ROW_TILE)
            xs.append(xn)
        else:
            def loss_rows(rv, ev, gv):
                (row,) = f_ln_loss(rv, ev, gv)
                return [], [jnp.broadcast_to(jnp.sum(row, keepdims=True), (1, LANES))]
            (loss_acc,) = rowwise("ln_loss", loss_rows, [_full(xs[k]), _full(y), _full(loss_target)], [gate[k]], [lng[k], lnb[k]],
                                  [], ts=ROW_TILE, accs=[LANES])
    loss = lax.psum(loss_acc[0, 0], ("x", "y", "c"))

    d_shift, d_scale, d_gate = [None] * n_sub, [None] * n_sub, [None] * n_sub
    d_lng, d_lnb = [None] * n_sub, [None] * n_sub
    part = {n: [None] * W[n].shape[0] for n in ['mla_q_norm', 'mla_kv_norm', 'hgrn_g_norm']}
    recv = {n: [None] * W[n].shape[0] for n in big}
    d_lower = [None] * n_hgrn
    k = n_sub - 1
    (dx, dy), (d_gate[k],), (d_lng[k], d_lnb[k]) = rowwise_bwd(
        "ln_loss_bwd", f_ln_loss, [_full(xs[k]), _full(ys[k]), _full(loss_target)], [gate[k]], [lng[k], lnb[k]], [],
        [(F32, [0]), (BF16, [1])], ts=ROW_TILE, n_diff=2, unit_ct=1)
    grad_x = None
    mine = {}

    def take_grads(names, got):
        for (n, j), a in zip(names, got):
            recv[n][j] = a

    def grad_items(names):
        return [(_to_slabs(n, mine[n, jj]), "a2a") for n, jj in names]

    for k in range(n_sub - 1, -1, -1):
        kind, j, w = sub_weights(k)
        ride = grad_items(carried(k // 2)) if kind != 'ffn' else []
        if kind == 'ffn':
            dh, g = _ffn_backward(dy, saved[k], w)
            new = {('ffn_w_in', j): g['w_in'], ('ffn_w_out', j): g['w_out']}
        elif kind == 'mla':
            dh, g, got = _mla_backward(dy, saved[k], w, tabs, ride)
            new = {('mla_w_in', j): g['w_in'][:, :mla_w_in.shape[-1] * N_DEV], ('mla_w_qb', j): _qb_internal(g['w_qb'], inverse=True),
                   ('mla_w_kvb', j): _kvb_internal(g['w_kvb'], inverse=True), ('mla_w_o', j): g['w_o']}
            part['mla_q_norm'][j], part['mla_kv_norm'][j] = g['q_norm'][0], g['kv_norm'][0]
        else:
            dh, g, got = _hgrn_backward(dy, saved[k], w, ride)
            new = {('hgrn_w_in', j): g['w_in'], ('hgrn_w_o', j): g['w_o']}
            part['hgrn_g_norm'][j] = g['g_norm'][0]
            d_lower[j] = g['lb'][0]
        if kind != 'ffn':
            take_grads(carried(k // 2), got)
        mine.update(new)
        if k:
            (dx, dy), (d_gate[k - 1], d_scale[k], d_shift[k]), (d_lng[k - 1], d_lnb[k - 1]) = rowwise_bwd(
                "ln_mod_bwd", f_ln_mod, [_full(xs[k - 1]), _full(ys[k - 1])], [gate[k - 1], scale[k], shift[k]],
                [lng[k - 1], lnb[k - 1]], [(dx, [d]), (dh, [d])], [(F32, [0]), (BF16, [1])], ts=ROW_TILE, n_diff=2)
        else:
            (grad_x,), (d_scale[0], d_shift[0]), _ = rowwise_bwd(
                "mod_first_bwd", f_mod_with_x, [_full(x)], [scale[0], shift[0]], [], [(dh, [d]), (dx, [d])],
                [(F32, [0])], ts=ROW_TILE, n_diff=1)

    waiting = mixer_names(0)
    slabs = grad_items(waiting)
    slabs.append((jnp.stack(d_lower).reshape(n_hgrn, N_DEV, -1).transpose(1, 0, 2), "a2a"))
    for parts in (d_lng, d_lnb):
        full = jnp.stack([p[0] for p in parts]).reshape(depth, 2, N_DEV, d // N_DEV)
        slabs.append((full.transpose(2, 0, 1, 3), "a2a"))
    dmod = jnp.concatenate([jnp.stack(d_shift), jnp.stack(d_scale), jnp.stack(d_gate)], axis=-1)
    slabs.append((dmod.reshape(n_sub, bsz, N_DEV, cols).transpose(2, 0, 1, 3), "a2a"))
    small = ['mla_q_norm', 'mla_kv_norm', 'hgrn_g_norm']
    slabs += [(jnp.stack(part[n]), "gather") for n in small]
    got = exchange("scatter_last", slabs)
    take_grads(waiting, got)
    stacks = dict(zip(['hgrn_lb', 'ln_g', 'ln_b', 'dmod'] + small, got[len(waiting):]))

    dmod_all = stacks['dmod'].transpose(1, 0, 2, 3).reshape(n_sub, N_DEV * bsz, cols)
    g_ada_w, g_ada_b = ada_bwd(c_all, dmod_all)
    stacks['ada_w'] = g_ada_w.reshape((1,) + ada_w.shape)
    stacks['ada_b'] = g_ada_b.reshape((1,) + ada_b.shape)

    res = {}
    for n in WEIGHTS:
        if n == 'hgrn_lb':
            res[n] = adam_lb(stacks[n], W[n], M1[n], M2[n])
        elif n in big:
            res[n] = adam_layers("adam_" + n, recv[n], W[n], M1[n], M2[n])
        else:
            res[n] = adam("adam_" + n, stacks[n], W[n], M1[n], M2[n])
    return (loss, grad_x, *[res[n][0] for n in WEIGHTS], *[res[n][1] for n in WEIGHTS], *[res[n][2] for n in WEIGHTS],
            *[res[n][3] for n in WEIGHTS])
```

```python
import functools
import math

import numpy as np
import jax
import jax.numpy as jnp
from jax import lax
from jax.experimental import pallas as pl
from jax.experimental.pallas import tpu as pltpu

F32 = jnp.float32
BF16 = jnp.bfloat16

N_DEV = 8
LANES = 128
VMEM_LIMIT = 52 * 1024 * 1024

D_MODEL = 1024
DEPTH = 4
MLA_HEADS = 16
QK_NOPE = 64
QK_ROPE = 32
V_HEAD = 64
Q_LORA = 768
KV_LORA = 256
ROPE_THETA = 10000.0
HGRN_EXPAND = 128
HGRN_CHUNK = 128
HGRN_HEADS_PER_STEP = 2
D_FF = 2816
ALPHA = (2.0 * DEPTH) ** 0.25
LN_EPS = 1e-5
RMS_EPS = 1e-6
ADAM_LR = 0.001
ADAM_B1 = 0.9
ADAM_B2 = 0.999
ADAM_EPS = 1e-08
ADAM_WD = 0.01
ADAM_STEP = 10

ATTN_TQ = 256
ROW_TILE = 256

WEIGHTS = ['mla_w_in', 'mla_q_norm', 'mla_w_qb', 'mla_kv_norm', 'mla_w_kvb', 'mla_w_o', 'hgrn_lb', 'hgrn_w_in',
           'hgrn_g_norm', 'hgrn_w_o', 'ffn_w_in', 'ffn_w_out', 'ada_w', 'ada_b', 'ln_g', 'ln_b']
COL_SHARDED = ['mla_w_in', 'mla_w_qb', 'mla_w_kvb', 'hgrn_w_in', 'ffn_w_in']
ROW_SHARDED = ['mla_w_o', 'hgrn_w_o', 'ffn_w_out']


def _params(*sem):
    if sem:
        return pltpu.CompilerParams(dimension_semantics=sem, vmem_limit_bytes=VMEM_LIMIT)
    return pltpu.CompilerParams(vmem_limit_bytes=VMEM_LIMIT)


def _pick(n, cap):
    best = None
    for t in range(LANES, min(n, cap) + 1, LANES):
        if n % t == 0:
            best = t
    return best or n


def _pick_rows(n, cap):
    best = None
    for t in range(8, min(n, cap) + 1, 8):
        if n % t == 0:
            best = t
    return best or n


def matmul(name, a, b, *, ta=False, tb=False, out_dtype=F32, tm_cap=1024, tn_cap=1536, tk_cap=2048):
    (k1, m) = a.shape if ta else a.shape[::-1]
    (n, k2) = b.shape if tb else b.shape[::-1]
    assert k1 == k2, (name, a.shape, b.shape)
    tm, tn, tk = _pick(m, tm_cap), _pick(n, tn_cap), _pick(k1, tk_cap)
    nk = k1 // tk
    dims = (((0 if ta else 1,), (1 if tb else 0,)), ((), ()))

    def prod(a_ref, b_ref):
        return lax.dot_general(a_ref[...].astype(BF16), b_ref[...].astype(BF16), dims, preferred_element_type=F32)

    if nk == 1:
        def body(a_ref, b_ref, o_ref):
            o_ref[...] = prod(a_ref, b_ref).astype(o_ref.dtype)
        scratch = []
    else:
        def body(a_ref, b_ref, o_ref, acc_ref):
            k = pl.program_id(2)

            @pl.when(k == 0)
            def _():
                acc_ref[...] = jnp.zeros_like(acc_ref)

            acc_ref[...] += prod(a_ref, b_ref)

            @pl.when(k == nk - 1)
            def _():
                o_ref[...] = acc_ref[...].astype(o_ref.dtype)
        scratch = [pltpu.VMEM((tm, tn), F32)]

    a_spec = pl.BlockSpec((tk, tm), lambda i, j, k: (k, i)) if ta else pl.BlockSpec((tm, tk), lambda i, j, k: (i, k))
    b_spec = pl.BlockSpec((tn, tk), lambda i, j, k: (j, k)) if tb else pl.BlockSpec((tk, tn), lambda i, j, k: (k, j))
    return pl.pallas_call(
        body, name=name, grid=(m // tm, n // tn, nk),
        in_specs=[a_spec, b_spec], out_specs=pl.BlockSpec((tm, tn), lambda i, j, k: (i, j)),
        out_shape=jax.ShapeDtypeStruct((m, n), out_dtype), scratch_shapes=scratch,
        compiler_params=_params("parallel", "parallel", "arbitrary"),
    )(a, b)


def mm3(name, a3, w, **kw):
    bsz, s, k = a3.shape
    out = matmul(name, a3.reshape(bsz * s, k), w, **kw)
    return out.reshape(bsz, s, out.shape[-1])


def wgrad(name, a3, g3):
    bsz, s, k = a3.shape
    return matmul(name, a3.reshape(bsz * s, k), g3.reshape(bsz * s, g3.shape[-1]), ta=True, out_dtype=BF16)


def _dg(a, b, ca, cb, **kw):
    return lax.dot_general(a, b, (((ca,), (cb,)), ((), ())), preferred_element_type=F32, **kw)


@functools.partial(jax.custom_vjp, nondiff_argnums=(2, 3))
def bdot(a, b, ca, cb):
    return _dg(a.astype(BF16), b.astype(BF16), ca, cb)


def _bdot_fwd(a, b, ca, cb):
    return bdot(a, b, ca, cb), (a, b)


def _bdot_bwd(ca, cb, res, g):
    a, b = res
    a16, b16, g16 = a.astype(BF16), b.astype(BF16), g.astype(BF16)
    if ca == 1:
        da = _dg(g16, b16, 1, 1 if cb == 0 else 0)
    else:
        da = _dg(b16, g16, 1 if cb == 0 else 0, 1)
    if cb == 0:
        db = _dg(a16, g16, 0 if ca == 1 else 1, 0)
    else:
        db = _dg(g16, a16, 0, 0 if ca == 1 else 1)
    return da, db


bdot.defvjp(_bdot_fwd, _bdot_bwd)


def hdot(a, b, ca=1, cb=0):
    return _dg(a, b, ca, cb, precision=lax.Precision.HIGHEST)


def _row_specs(rows, exs, globs, ts):
    specs = [pl.BlockSpec((1, ts, w), lambda b, s, j=j: (b, s, j)) for (_, j, w) in rows]
    specs += [pl.BlockSpec((1, 1, e.shape[-1]), lambda b, s: (b, 0, 0)) for e in exs]
    specs += [pl.BlockSpec((1, g.shape[-1]), lambda b, s: (0, 0)) for g in globs]
    return specs


def _store_pieces(o_ref, pieces, widths):
    off = 0
    for p, w in zip(pieces, widths):
        o_ref[0, :, off:off + w] = p.astype(o_ref.dtype)
        off += w


def _load_pieces(c_ref, widths):
    out, off = [], 0
    for w in widths:
        out.append(c_ref[0, :, off:off + w].astype(F32))
        off += w
    return out


def rowwise(name, f, rows, exs, globs, outs, *, ts, accs=()):
    bsz, s = rows[0][0].shape[:2]
    ts = min(ts, s)
    n_r, n_e, n_g, n_o = len(rows), len(exs), len(globs), len(outs)

    def body(*refs):
        rv = [r[0].astype(F32) for r in refs[:n_r]]
        ev = [e[0] for e in refs[n_r:n_r + n_e]]
        gv = [g[...] for g in refs[n_r + n_e:n_r + n_e + n_g]]
        o_refs = refs[n_r + n_e + n_g:n_r + n_e + n_g + n_o]
        a_refs = refs[n_r + n_e + n_g + n_o:]
        pieces, sums = f(rv, ev, gv)
        idx = 0
        for o_ref, (_, ws) in zip(o_refs, outs):
            _store_pieces(o_ref, pieces[idx:idx + len(ws)], ws)
            idx += len(ws)
        if accs:
            @pl.when((pl.program_id(0) == 0) & (pl.program_id(1) == 0))
            def _():
                for a_ref in a_refs:
                    a_ref[...] = jnp.zeros_like(a_ref)
            for a_ref, val in zip(a_refs, sums):
                a_ref[...] += val

    out_specs = [pl.BlockSpec((1, ts, sum(ws)), lambda b, s: (b, s, 0)) for (_, ws) in outs]
    out_specs += [pl.BlockSpec((1, w), lambda b, s: (0, 0)) for w in accs]
    out_shape = [jax.ShapeDtypeStruct((bsz, s, sum(ws)), dt) for (dt, ws) in outs]
    out_shape += [jax.ShapeDtypeStruct((1, w), F32) for w in accs]
    return pl.pallas_call(
        body, name=name, grid=(bsz, s // ts),
        in_specs=_row_specs(rows, exs, globs, ts), out_specs=out_specs, out_shape=out_shape,
        compiler_params=_params("arbitrary", "arbitrary"),
    )(*[r[0] for r in rows], *exs, *globs)


def rowwise_bwd(name, f, rows, exs, globs, cts, d_groups, *, ts, n_diff, unit_ct=0):
    bsz, s = rows[0][0].shape[:2]
    ts = min(ts, s)
    n_r, n_e, n_g, n_c = len(rows), len(exs), len(globs), len(cts)
    n_d = len(d_groups)

    def body(*refs):
        rv = [r[0].astype(F32) for r in refs[:n_r]]
        ev = [e[0] for e in refs[n_r:n_r + n_e]]
        gv = [g[...] for g in refs[n_r + n_e:n_r + n_e + n_g]]
        base = n_r + n_e + n_g
        c_refs = refs[base:base + n_c]
        d_refs = refs[base + n_c:base + n_c + n_d]
        de_refs = refs[base + n_c + n_d:base + n_c + n_d + n_e]
        dg_refs = refs[base + n_c + n_d + n_e:]
        fixed = rv[n_diff:]
        out, vjp = jax.vjp(lambda r, e, g: f(r + fixed, e, g), rv[:n_diff], ev, gv)
        ct = []
        for c_ref, (_, ws) in zip(c_refs, cts):
            ct += _load_pieces(c_ref, ws)
        ct += [jnp.ones_like(o) for o in out[len(ct):]]
        assert len(ct) == len(out) and len(out) - unit_ct == sum(len(ws) for _, ws in cts), name
        d_r, d_e, d_g = vjp(ct)
        for d_ref, (_, idxs) in zip(d_refs, d_groups):
            _store_pieces(d_ref, [d_r[i] for i in idxs], [rows[i][2] for i in idxs])
        first_s = pl.program_id(1) == 0
        if n_e:
            @pl.when(first_s)
            def _():
                for r in de_refs:
                    r[...] = jnp.zeros_like(r)
            for r, val in zip(de_refs, d_e):
                r[0] += val
        if n_g:
            @pl.when(first_s & (pl.program_id(0) == 0))
            def _():
                for r in dg_refs:
                    r[...] = jnp.zeros_like(r)
            for r, val in zip(dg_refs, d_g):
                r[...] += val

    in_specs = _row_specs(rows, exs, globs, ts)
    in_specs += [pl.BlockSpec((1, ts, sum(ws)), lambda b, s: (b, s, 0)) for (_, ws) in cts]
    out_specs = [pl.BlockSpec((1, ts, sum(rows[i][2] for i in idxs)), lambda b, s: (b, s, 0)) for (_, idxs) in d_groups]
    out_specs += [pl.BlockSpec((1, 1, e.shape[-1]), lambda b, s: (b, 0, 0)) for e in exs]
    out_specs += [pl.BlockSpec((1, g.shape[-1]), lambda b, s: (0, 0)) for g in globs]
    out_shape = [jax.ShapeDtypeStruct((bsz, s, sum(rows[i][2] for i in idxs)), dt) for (dt, idxs) in d_groups]
    out_shape += [jax.ShapeDtypeStruct(e.shape, F32) for e in exs]
    out_shape += [jax.ShapeDtypeStruct(g.shape, F32) for g in globs]
    res = pl.pallas_call(
        body, name=name, grid=(bsz, s // ts),
        in_specs=in_specs, out_specs=out_specs, out_shape=out_shape,
        compiler_params=_params("arbitrary", "arbitrary"),
    )(*[r[0] for r in rows], *exs, *globs, *[c[0] for c in cts])
    return res[:n_d], res[n_d:n_d + n_e], res[n_d + n_e:]


def _full(a):
    return (a, 0, a.shape[-1])


def _view(a, col, w):
    assert col % w == 0
    return (a, col // w, w)


def _layer_norm(z, g, b):
    mu = jnp.mean(z, -1, keepdims=True)
    var = jnp.mean(jnp.square(z - mu), -1, keepdims=True)
    return (z - mu) * lax.rsqrt(var + LN_EPS) * g + b


def _rms_norm(z, g):
    ms = jnp.mean(jnp.square(z), -1, keepdims=True)
    return z * lax.rsqrt(ms + RMS_EPS) * g


def f_mod(rv, ev, gv):
    (x,), (scale, shift) = rv, ev
    return [x * (1.0 + scale) + shift]


def f_mod_with_x(rv, ev, gv):
    return f_mod(rv, ev, gv) + [rv[0]]


def f_ln_mod(rv, ev, gv):
    (x, y), (gate, scale, shift), (g, b) = rv, ev, gv
    xn = _layer_norm(ALPHA * x + (1.0 + gate) * y, g, b)
    return [xn, xn * (1.0 + scale) + shift]


def f_ln_loss(rv, ev, gv):
    (x, y, target), (gate,), (g, b) = rv, ev, gv
    xn = _layer_norm(ALPHA * x + (1.0 + gate) * y, g, b)
    return [0.5 * jnp.mean(jnp.square(xn - target), -1, keepdims=True)]


def f_swiglu(rv, ev, gv):
    gate, up = rv
    return [jax.nn.silu(gate) * up]


def _head_spread(width):
    r2 = QK_ROPE // 2
    j = lax.broadcasted_iota(jnp.int32, (LANES, width), 0)
    col = lax.broadcasted_iota(jnp.int32, (LANES, width), 1) % r2
    return (j == col).astype(F32), (j == col + r2).astype(F32)


def f_mla_mid(rv, ev, gv):
    (q_lat, kv_lat, kr, cos, sin), (q_g, kv_g) = rv, gv
    e1, e2 = _head_spread(cos.shape[-1])
    k1, k2 = hdot(kr, e1), hdot(kr, e2)
    return [_rms_norm(q_lat, q_g), _rms_norm(kv_lat, kv_g), k1 * cos - k2 * sin, k1 * sin + k2 * cos]


def f_hg_pre(rv, ev, gv):
    (q, fx), (lb,) = rv[:2], gv
    f = lb + (1.0 - lb) * jax.nn.sigmoid(fx)
    return [jax.nn.silu(q), jnp.log(f), 1.0 - f]


def f_hg_pre_with_iv(rv, ev, gv):
    return f_hg_pre(rv, ev, gv) + [rv[2], rv[3]]


def f_hg_post(rv, ev, gv):
    nh = len(rv) // 2
    (gn,) = gv
    return [_rms_norm(rv[h], gn) * jax.nn.silu(rv[nh + h]) for h in range(nh)]


def rope_tables(positions):
    bsz, s = positions.shape
    r2 = QK_ROPE // 2
    width = MLA_HEADS * r2
    inv = (ROPE_THETA ** (-np.arange(0, QK_ROPE, 2, dtype=np.float32) / QK_ROPE)).astype(np.float32)
    inv = jnp.asarray(np.tile(inv, MLA_HEADS)[None, :])
    ts = min(ROW_TILE, s)

    def body(p_ref, inv_ref, cos_ref, sin_ref):
        ang = p_ref[0].astype(F32) * inv_ref[...]
        cos_ref[0] = jnp.cos(ang)
        sin_ref[0] = jnp.sin(ang)

    spec = pl.BlockSpec((1, ts, width), lambda b, s: (b, s, 0))
    return pl.pallas_call(
        body, name="rope_tables", grid=(bsz, s // ts),
        in_specs=[pl.BlockSpec((1, ts, 1), lambda b, s: (b, s, 0)), pl.BlockSpec((1, width), lambda b, s: (0, 0))],
        out_specs=[spec, spec], out_shape=[jax.ShapeDtypeStruct((bsz, s, width), F32)] * 2,
        compiler_params=_params("arbitrary", "arbitrary"),
    )(positions[:, :, None], inv)


def _attn_probs(q, k, row0):
    scale = (QK_NOPE + QK_ROPE) ** -0.5
    s = _dg(q, k, 1, 1) * scale
    rows = row0 + lax.broadcasted_iota(jnp.int32, s.shape, 0)
    cols = lax.broadcasted_iota(jnp.int32, s.shape, 1)
    s = jnp.where(cols <= rows, s, jnp.finfo(F32).min)
    e = jnp.exp(s - jnp.max(s, -1, keepdims=True))
    return e / jnp.sum(e, -1, keepdims=True), scale


ATTN_PAIR = 2


def attn_fwd(q, k, v, ride=()):
    bsz, h, s, dq = q.shape
    dv = v.shape[-1]
    tq = min(ATTN_TQ, s)
    grid = (bsz, h // ATTN_PAIR, s // tq)
    rd = _Ride(ride)

    def body(*refs):
        (q_ref, k_ref, v_ref), srcs, (o_ref,), outs, sems = rd.split(refs, 3, 1)
        rd.run(srcs, outs, sems, grid)
        for i in range(grid[2]):
            @pl.when(pl.program_id(2) == i)
            def _(i=i):
                kend = (i + 1) * tq
                for e in range(ATTN_PAIR):
                    p, _ = _attn_probs(q_ref[0, e], k_ref[0, e, :kend, :], i * tq)
                    o_ref[0, :, e * dv:(e + 1) * dv] = _dg(p.astype(BF16), v_ref[0, e, :kend, :], 1, 0).astype(o_ref.dtype)

    res = pl.pallas_call(
        body, name="attn_fwd", grid=grid,
        in_specs=[pl.BlockSpec((1, ATTN_PAIR, tq, dq), lambda b, h, i: (b, h, i, 0)),
                  pl.BlockSpec((1, ATTN_PAIR, s, dq), lambda b, h, i: (b, h, 0, 0)),
                  pl.BlockSpec((1, ATTN_PAIR, s, dv), lambda b, h, i: (b, h, 0, 0))] + rd.in_specs,
        out_specs=[pl.BlockSpec((1, tq, ATTN_PAIR * dv), lambda b, h, i: (b, i, h))] + rd.out_specs,
        out_shape=[jax.ShapeDtypeStruct((bsz, s, h * dv), BF16)] + rd.out_shape, scratch_shapes=rd.scratch,
        compiler_params=_params("arbitrary", "arbitrary", "arbitrary"),
    )(q, k, v, *rd.srcs)
    return res[0], res[1:]


def attn_bwd(q, k, v, do, ride=()):
    bsz, h, s, dq = q.shape
    dv = v.shape[-1]
    tq = min(ATTN_TQ, s)
    grid = (bsz, h // ATTN_PAIR, s // tq)
    rd = _Ride(ride)

    def body(*refs):
        (q_ref, k_ref, v_ref, do_ref), srcs, (dq_ref, dk_ref, dv_ref), outs, sems = rd.split(refs, 4, 3)
        rd.run(srcs, outs, sems, grid)

        @pl.when(pl.program_id(2) == 0)
        def _():
            dk_ref[...] = jnp.zeros_like(dk_ref)
            dv_ref[...] = jnp.zeros_like(dv_ref)

        for i in range(grid[2]):
            @pl.when(pl.program_id(2) == i)
            def _(i=i):
                kend = (i + 1) * tq
                for e in range(ATTN_PAIR):
                    qv, kv, vv = q_ref[0, e], k_ref[0, e, :kend, :], v_ref[0, e, :kend, :]
                    p, scale = _attn_probs(qv, kv, i * tq)
                    do16 = do_ref[0, :, e * dv:(e + 1) * dv].astype(BF16)
                    dv_ref[0, e, :kend, :] += _dg(p.astype(BF16), do16, 0, 0)
                    dp = _dg(do16, vv, 1, 1)
                    ds = (p * (dp - jnp.sum(dp * p, -1, keepdims=True)) * scale).astype(BF16)
                    dq_ref[0, e] = _dg(ds, kv, 1, 0)
                    dk_ref[0, e, :kend, :] += _dg(ds, qv, 0, 0)

    res = pl.pallas_call(
        body, name="attn_bwd", grid=grid,
        in_specs=[pl.BlockSpec((1, ATTN_PAIR, tq, dq), lambda b, h, i: (b, h, i, 0)),
                  pl.BlockSpec((1, ATTN_PAIR, s, dq), lambda b, h, i: (b, h, 0, 0)),
                  pl.BlockSpec((1, ATTN_PAIR, s, dv), lambda b, h, i: (b, h, 0, 0)),
                  pl.BlockSpec((1, tq, ATTN_PAIR * dv), lambda b, h, i: (b, i, h))] + rd.in_specs,
        out_specs=[pl.BlockSpec((1, ATTN_PAIR, tq, dq), lambda b, h, i: (b, h, i, 0)),
                   pl.BlockSpec((1, ATTN_PAIR, s, dq), lambda b, h, i: (b, h, 0, 0)),
                   pl.BlockSpec((1, ATTN_PAIR, s, dv), lambda b, h, i: (b, h, 0, 0))] + rd.out_specs,
        out_shape=[jax.ShapeDtypeStruct((bsz, h, s, dq), F32), jax.ShapeDtypeStruct((bsz, h, s, dq), F32),
                   jax.ShapeDtypeStruct((bsz, h, s, dv), F32)] + rd.out_shape, scratch_shapes=rd.scratch,
        compiler_params=_params("arbitrary", "arbitrary", "arbitrary"),
    )(q, k, v, do, *rd.srcs)
    return res[0], res[1], res[2], res[3:]


def mla_heads(q, kv, krt, cos, sin):
    bsz, s, _ = q.shape
    nh, n, r2, vd = MLA_HEADS, QK_NOPE, QK_ROPE // 2, V_HEAD
    ts = min(ROW_TILE, s)

    def body(q_ref, kv_ref, kr_ref, cos_ref, sin_ref, qh_ref, kh_ref, vh_ref):
        cos, sin = cos_ref[0], sin_ref[0]
        qv, kvv, kr = q_ref[0], kv_ref[0], kr_ref[0].astype(F32)
        q1, q2 = qv[:, nh * n:nh * (n + r2)], qv[:, nh * (n + r2):]
        qr = [q1 * cos - q2 * sin, q1 * sin + q2 * cos]
        for h in range(nh):
            qh_ref[0, h, :, :n] = qv[:, h * n:(h + 1) * n].astype(BF16)
            kh_ref[0, h, :, :n] = kvv[:, h * n:(h + 1) * n].astype(BF16)
            vh_ref[0, h] = kvv[:, nh * n + h * vd:nh * n + (h + 1) * vd].astype(BF16)
            for j in range(2):
                qh_ref[0, h, :, n + j * r2:n + (j + 1) * r2] = qr[j][:, h * r2:(h + 1) * r2].astype(BF16)
                kh_ref[0, h, :, n + j * r2:n + (j + 1) * r2] = kr[:, (j * nh + h) * r2:(j * nh + h + 1) * r2].astype(BF16)

    def row(a):
        return pl.BlockSpec((1, ts, a.shape[-1]), lambda b, i: (b, i, 0))

    def heads(w):
        return pl.BlockSpec((1, nh, ts, w), lambda b, i: (b, 0, i, 0))

    return pl.pallas_call(
        body, name="mla_heads", grid=(bsz, s // ts), in_specs=[row(q), row(kv), row(krt), row(cos), row(sin)],
        out_specs=[heads(n + 2 * r2), heads(n + 2 * r2), heads(vd)],
        out_shape=[jax.ShapeDtypeStruct((bsz, nh, s, n + 2 * r2), BF16)] * 2 + [jax.ShapeDtypeStruct((bsz, nh, s, vd), BF16)],
        compiler_params=_params("arbitrary", "arbitrary"),
    )(q, kv, krt, cos, sin)


def mla_heads_bwd(dqh, dkh, dvh, cos, sin):
    bsz, nh, s, _ = dqh.shape
    n, r2, vd = QK_NOPE, QK_ROPE // 2, V_HEAD
    ts = min(ROW_TILE, s)

    def body(dq_ref, dk_ref, dv_ref, cos_ref, sin_ref, oq_ref, okv_ref, okr_ref, tq_acc, tkv_acc, rot):
        for h in range(nh):
            tq_acc[:, h * n:(h + 1) * n] = dq_ref[0, h, :, :n]
            tkv_acc[:, h * n:(h + 1) * n] = dk_ref[0, h, :, :n]
            tkv_acc[:, nh * n + h * vd:nh * n + (h + 1) * vd] = dv_ref[0, h]
            for j in range(2):
                rot[j, :, h * r2:(h + 1) * r2] = dq_ref[0, h, :, n + j * r2:n + (j + 1) * r2]
                okr_ref[0, :, (j * nh + h) * r2:(j * nh + h + 1) * r2] = dk_ref[0, h, :, n + j * r2:n + (j + 1) * r2]
        cos, sin = cos_ref[0], sin_ref[0]
        d1, d2 = rot[0], rot[1]
        oq_ref[0, :, :nh * n] = tq_acc[...].astype(BF16)
        oq_ref[0, :, nh * n:nh * (n + r2)] = (d1 * cos + d2 * sin).astype(BF16)
        oq_ref[0, :, nh * (n + r2):] = (d2 * cos - d1 * sin).astype(BF16)
        okv_ref[0] = tkv_acc[...].astype(BF16)

    def row(w):
        return pl.BlockSpec((1, ts, w), lambda b, i: (b, i, 0))

    def heads(w):
        return pl.BlockSpec((1, nh, ts, w), lambda b, i: (b, 0, i, 0))

    return pl.pallas_call(
        body, name="mla_heads_bwd", grid=(bsz, s // ts),
        in_specs=[heads(n + 2 * r2), heads(n + 2 * r2), heads(vd), row(nh * r2), row(nh * r2)],
        out_specs=[row(nh * (n + 2 * r2)), row(nh * (n + vd)), row(2 * nh * r2)],
        out_shape=[jax.ShapeDtypeStruct((bsz, s, nh * (n + 2 * r2)), BF16), jax.ShapeDtypeStruct((bsz, s, nh * (n + vd)), BF16),
                   jax.ShapeDtypeStruct((bsz, s, 2 * nh * r2), F32)],
        scratch_shapes=[pltpu.VMEM((ts, nh * n), F32), pltpu.VMEM((ts, nh * (n + vd)), F32), pltpu.VMEM((2, ts, nh * r2), F32)],
        compiler_params=_params("arbitrary", "arbitrary"),
    )(dqh, dkh, dvh, cos, sin)


def _hgrn_tables(c):
    levels = c.bit_length() - 1
    assert 1 << levels == c
    r = np.arange(c)
    prefix, sign, mask = [r[:, None] >= r[None, :]], [], []
    for l in range(levels):
        ref = ((r >> (l + 1)) << (l + 1)) + (1 << l) - 1
        lower = ((r >> l) & 1) == 1
        prefix.append(r[None, :] <= ref[:, None])
        sign.append(np.broadcast_to(np.where(lower, 1.0, -1.0)[:, None], (c, LANES)))
        mask.append((((r[:, None] ^ r[None, :]) >> l) == 1) & lower[:, None])
    return (jnp.asarray(np.concatenate(prefix, 0), BF16), jnp.asarray(np.stack(sign), F32),
            jnp.asarray(np.stack(mask), F32))


def _const_specs(tables):
    return [pl.BlockSpec(t.shape, lambda b, h, i, nd=t.ndim: (0,) * nd) for t in tables]


def _split3(x):
    hi = x.astype(BF16)
    rest = x - hi.astype(F32)
    mid = rest.astype(BF16)
    return hi, mid, (rest - mid.astype(F32)).astype(BF16)


@functools.partial(jax.custom_vjp, nondiff_argnums=(2,))
def prefix_sums(p, g, n):
    k = g.shape[1]
    r = _dg(p, jnp.concatenate(_split3(g), axis=1), 1, 0)
    r = r[:, :k] + r[:, k:2 * k] + r[:, 2 * k:]
    c = r.shape[0] // n
    return tuple(r[i * c:(i + 1) * c] for i in range(n))


def _prefix_fwd(p, g, n):
    return prefix_sums(p, g, n), p


def _prefix_bwd(n, p, ct):
    ct = jnp.concatenate(ct, axis=0)
    k = ct.shape[1]
    r = _dg(p, jnp.concatenate(_split3(ct), axis=1), 0, 0)
    return jnp.zeros_like(p), r[:, :k] + r[:, k:2 * k] + r[:, 2 * k:]


prefix_sums.defvjp(_prefix_fwd, _prefix_bwd)


def _dot3_raw(a, b, ca, cb):
    ah = a.astype(BF16)
    al = (a - ah.astype(F32)).astype(BF16)
    bh = b.astype(BF16)
    bl = (b - bh.astype(F32)).astype(BF16)
    if (ca == 1 and a.shape[1] % LANES) or (cb == 1 and b.shape[1] % LANES):
        return _dg(ah, bh, ca, cb) + _dg(ah, bl, ca, cb) + _dg(al, bh, ca, cb)
    return _dg(jnp.concatenate([ah, ah, al], axis=ca), jnp.concatenate([bh, bl, bh], axis=cb), ca, cb)


@functools.partial(jax.custom_vjp, nondiff_argnums=(2, 3))
def dot3(a, b, ca, cb):
    return _dot3_raw(a, b, ca, cb)


def _dot3_fwd(a, b, ca, cb):
    return _dot3_raw(a, b, ca, cb), (a, b)


def _dot3_bwd(ca, cb, res, g):
    a, b = res
    if ca == 1:
        da = _dot3_raw(g, b, 1, 1 if cb == 0 else 0)
    else:
        da = _dot3_raw(b, g, 1 if cb == 0 else 0, 1)
    if cb == 0:
        db = _dot3_raw(a, g, 0 if ca == 1 else 1, 0)
    else:
        db = _dot3_raw(g, a, 0, 0 if ca == 1 else 1)
    return da, db


dot3.defvjp(_dot3_fwd, _dot3_bwd)


def _hgrn_chunk(q, g, k, v, st0, prefix, sign, mask):
    levels = len(mask)
    pre = prefix_sums(prefix, g, levels + 1)
    b = pre[0]
    o = bdot(q * jnp.exp(b), st0, 1, 1)
    att = None
    for l in range(levels):
        e = jnp.exp((b - pre[l + 1]) * sign[l])
        a = dot3(q * e, k * e, 1, 1) * mask[l]
        att = a if att is None else att + a
    o = o + bdot(att, v, 1, 0) + jnp.sum(q * k, -1, keepdims=True) * v
    total = jnp.sum(g, 0, keepdims=True)
    st1 = st0 * jnp.exp(total) + bdot(v, k * jnp.exp(total - b), 0, 0)
    return o, st1


def hgrn_fwd(qs, logf, kk, proj, ride=()):
    bsz, s, hk = qs.shape
    kd = HGRN_EXPAND
    c = min(HGRN_CHUNK, s)
    nh, nc = hk // kd, s // c
    hp = math.gcd(nh, HGRN_HEADS_PER_STEP)
    tables = _hgrn_tables(c)
    levels = tables[2].shape[0]
    grid = (bsz, nh // hp, nc)
    rd = _Ride(ride)

    def body(*refs):
        (q_ref, g_ref, k_ref, v_ref, p_ref, sg_ref, mk_ref), srcs, (o_ref, st_ref), outs, scratch = rd.split(refs, 7, 2)
        state = scratch[0]
        rd.run(srcs, outs, scratch, grid)

        @pl.when(pl.program_id(2) == 0)
        def _():
            state[...] = jnp.zeros_like(state)

        prefix, sign, mask = p_ref[...], [sg_ref[l] for l in range(levels)], [mk_ref[l] for l in range(levels)]
        for j in range(hp):
            cols = slice(j * kd, (j + 1) * kd)
            st0 = state[j]
            st_ref[0, j, 0] = st0
            o, st1 = _hgrn_chunk(q_ref[0, :, cols], g_ref[0, :, cols], k_ref[0, :, cols], v_ref[0, :, cols], st0,
                                 prefix, sign, mask)
            o_ref[0, :, cols] = o
            state[j] = st1

    blk = pl.BlockSpec((1, c, hp * kd), lambda b, h, i: (b, i, h))
    res = pl.pallas_call(
        body, name="hgrn_fwd", grid=grid,
        in_specs=[blk, blk, blk, pl.BlockSpec((1, c, hp * kd), lambda b, h, i: (b, i, 2 * (nh // hp) + h))]
        + _const_specs(tables) + rd.in_specs,
        out_specs=[blk, pl.BlockSpec((1, hp, 1, kd, kd), lambda b, h, i: (b, h, i, 0, 0))] + rd.out_specs,
        out_shape=[jax.ShapeDtypeStruct((bsz, s, hk), F32), jax.ShapeDtypeStruct((bsz, nh, nc, kd, kd), F32)] + rd.out_shape,
        scratch_shapes=[pltpu.VMEM((hp, kd, kd), F32)] + rd.scratch,
        compiler_params=_params("arbitrary", "arbitrary", "arbitrary"),
    )(qs, logf, kk, proj, *tables, *rd.srcs)
    return res[0], res[1], res[2:]


def hgrn_bwd(qs, logf, kk, proj, states, do, ride=()):
    bsz, s, hk = qs.shape
    kd = HGRN_EXPAND
    c = min(HGRN_CHUNK, s)
    nh, nc = hk // kd, s // c
    hp = math.gcd(nh, HGRN_HEADS_PER_STEP)
    tables = _hgrn_tables(c)
    levels = tables[2].shape[0]
    grid = (bsz, nh // hp, nc)
    rd = _Ride(ride)

    def body(*refs):
        ((q_ref, g_ref, k_ref, v_ref, st_ref, do_ref, p_ref, sg_ref, mk_ref), srcs, (dq_ref, dg_ref, dk_ref, dv_ref), outs,
         scratch) = rd.split(refs, 9, 4)
        dstate = scratch[0]
        rd.run(srcs, outs, scratch, grid)

        @pl.when(pl.program_id(2) == 0)
        def _():
            dstate[...] = jnp.zeros_like(dstate)

        prefix, sign, mask = p_ref[...], [sg_ref[l] for l in range(levels)], [mk_ref[l] for l in range(levels)]
        for j in range(hp):
            cols = slice(j * kd, (j + 1) * kd)
            _, vjp = jax.vjp(lambda q, g, k, v, st: _hgrn_chunk(q, g, k, v, st, prefix, sign, mask),
                             q_ref[0, :, cols], g_ref[0, :, cols], k_ref[0, :, cols], v_ref[0, :, cols], st_ref[0, j, 0])
            dq, dg, dk, dv, dst = vjp((do_ref[0, :, cols], dstate[j]))
            dq_ref[0, :, cols] = dq
            dg_ref[0, :, cols] = dg
            dk_ref[0, :, cols] = dk
            dv_ref[0, :, cols] = dv
            dstate[j] = dst

    blk = pl.BlockSpec((1, c, hp * kd), lambda b, h, i: (b, nc - 1 - i, h))
    shape = jax.ShapeDtypeStruct((bsz, s, hk), F32)
    res = pl.pallas_call(
        body, name="hgrn_bwd", grid=grid,
        in_specs=[blk, blk, blk, pl.BlockSpec((1, c, hp * kd), lambda b, h, i: (b, nc - 1 - i, 2 * (nh // hp) + h)),
                  pl.BlockSpec((1, hp, 1, kd, kd), lambda b, h, i: (b, h, nc - 1 - i, 0, 0)), blk]
        + _const_specs(tables) + rd.in_specs,
        out_specs=[blk] * 4 + rd.out_specs, out_shape=[shape] * 4 + rd.out_shape,
        scratch_shapes=[pltpu.VMEM((hp, kd, kd), F32)] + rd.scratch,
        compiler_params=_params("arbitrary", "arbitrary", "arbitrary"),
    )(qs, logf, kk, proj, states, do, *tables, *rd.srcs)
    return res[0], res[1], res[2], res[3], res[4:]


def cast_bf16(name, w):
    blk = pl.BlockSpec((1,) + w.shape[1:], lambda l: (l, 0, 0))

    def body(w_ref, o_ref):
        o_ref[...] = w_ref[...].astype(BF16)

    return pl.pallas_call(body, name=name, grid=(w.shape[0],), in_specs=[blk], out_specs=blk,
                          out_shape=jax.ShapeDtypeStruct(w.shape, BF16), compiler_params=_params("arbitrary"))(w)


def _lower_bounds(rows):
    m = functools.reduce(jnp.maximum, rows)
    e = [jnp.exp(r - m) for r in rows]
    z = functools.reduce(lambda a, b: a + b, e)
    soft = [x / z for x in e]
    out, run = [], jnp.zeros_like(rows[0])
    for sft in soft:
        run = run + sft
        out.append(run - soft[0])
    return out


def lower_bounds(lb):
    n = lb.shape[0]

    def body(lb_ref, o_ref):
        for i, r in enumerate(_lower_bounds([lb_ref[i:i + 1, :] for i in range(n)])):
            o_ref[i:i + 1, :] = r

    return pl.pallas_call(body, name="lower_bounds", out_shape=jax.ShapeDtypeStruct(lb.shape, F32),
                          compiler_params=_params())(lb)


def ada_fwd(c_all, ada_w, ada_b):
    nl, ns, d, cols = ada_w.shape
    n_ex = c_all.shape[0]

    def body(c_ref, w_ref, b_ref, o_ref):
        a = jax.nn.silu(c_ref[...]).astype(BF16)
        o_ref[0] = _dg(a, w_ref[0].astype(BF16), 1, 0) + b_ref[0]

    return pl.pallas_call(
        body, name="ada_fwd", grid=(nl * ns,),
        in_specs=[pl.BlockSpec((n_ex, d), lambda i: (0, 0)), pl.BlockSpec((1, d, cols), lambda i: (i, 0, 0)),
                  pl.BlockSpec((1, 1, cols), lambda i: (i, 0, 0))],
        out_specs=pl.BlockSpec((1, n_ex, cols), lambda i: (i, 0, 0)),
        out_shape=jax.ShapeDtypeStruct((nl * ns, n_ex, cols), F32), compiler_params=_params("arbitrary"),
    )(c_all, ada_w.reshape(nl * ns, d, cols), ada_b.reshape(nl * ns, 1, cols))


def ada_bwd(c_all, dmod):
    n, n_ex, cols = dmod.shape
    d = c_all.shape[1]

    def body(c_ref, g_ref, dw_ref, db_ref):
        a = jax.nn.silu(c_ref[...]).astype(BF16)
        g = g_ref[0]
        dw_ref[0] = _dg(a, g.astype(BF16), 0, 0)
        db_ref[0] = jnp.sum(g, 0, keepdims=True)

    return pl.pallas_call(
        body, name="ada_bwd", grid=(n,),
        in_specs=[pl.BlockSpec((n_ex, d), lambda i: (0, 0)), pl.BlockSpec((1, n_ex, cols), lambda i: (i, 0, 0))],
        out_specs=[pl.BlockSpec((1, d, cols), lambda i: (i, 0, 0)), pl.BlockSpec((1, 1, cols), lambda i: (i, 0, 0))],
        out_shape=[jax.ShapeDtypeStruct((n, d, cols), F32), jax.ShapeDtypeStruct((n, 1, cols), F32)],
        compiler_params=_params("arbitrary"),
    )(c_all, dmod)


def _adam_math(g, w, m, v):
    m = ADAM_B1 * m + (1.0 - ADAM_B1) * g
    v = ADAM_B2 * v + (1.0 - ADAM_B2) * jnp.square(g)
    m_hat = m / (1.0 - ADAM_B1 ** ADAM_STEP)
    v_hat = v / (1.0 - ADAM_B2 ** ADAM_STEP)
    delta = -ADAM_LR * (m_hat / (jnp.sqrt(v_hat) + ADAM_EPS) + ADAM_WD * w)
    return delta, m, v


def adam(name, gstack, w, m, v):
    shape = w.shape
    n, cols = gstack.shape[0], shape[-1]
    rows = math.prod(shape[:-1])
    tr = _pick_rows(rows, max(8, (2 * 1024 * 1024) // (4 * cols * n)))

    def body(g_ref, w_ref, m_ref, v_ref, go_ref, d_ref, mo_ref, vo_ref):
        g = g_ref[0].astype(F32)
        for i in range(1, n):
            g = g + g_ref[i].astype(F32)
        delta, m1, v1 = _adam_math(g, w_ref[...], m_ref[...], v_ref[...])
        go_ref[...] = g
        d_ref[...] = delta
        mo_ref[...] = m1
        vo_ref[...] = v1

    blk = pl.BlockSpec((tr, cols), lambda i: (i, 0))
    out = pl.pallas_call(
        body, name=name, grid=(rows // tr,),
        in_specs=[pl.BlockSpec((n, tr, cols), lambda i: (0, i, 0)), blk, blk, blk],
        out_specs=[blk] * 4, out_shape=[jax.ShapeDtypeStruct((rows, cols), F32)] * 4,
        compiler_params=_params("arbitrary"),
    )(gstack.reshape(n, rows, cols), w.reshape(rows, cols), m.reshape(rows, cols), v.reshape(rows, cols))
    return [o.reshape(shape) for o in out]


def adam_layers(name, gs, w, m, v):
    shape = w.shape
    nl, n, cols = len(gs), gs[0].shape[0], shape[-1]
    rows = math.prod(shape[1:-1])
    tr = _pick_rows(rows, max(16, (2 * 1024 * 1024) // (4 * cols * n)))
    nt = rows // tr

    def body(*refs):
        g_refs, (w_ref, m_ref, v_ref, go_ref, d_ref, mo_ref, vo_ref) = refs[:nl], refs[nl:]
        for j in range(nl):
            @pl.when(pl.program_id(0) == j)
            def _(j=j):
                g = g_refs[j][0].astype(F32)
                for i in range(1, n):
                    g = g + g_refs[j][i].astype(F32)
                delta, m1, v1 = _adam_math(g, w_ref[...], m_ref[...], v_ref[...])
                go_ref[...] = g
                d_ref[...] = delta
                mo_ref[...] = m1
                vo_ref[...] = v1

    def g_spec(j):
        return pl.BlockSpec((n, tr, cols), lambda l, i: (0, jnp.where(l == j, i, jnp.where(l < j, 0, nt - 1)), 0))

    blk = pl.BlockSpec((tr, cols), lambda l, i: (l * nt + i, 0))
    out = pl.pallas_call(
        body, name=name, grid=(nl, nt),
        in_specs=[g_spec(j) for j in range(nl)] + [blk, blk, blk],
        out_specs=[blk] * 4, out_shape=[jax.ShapeDtypeStruct((nl * rows, cols), F32)] * 4,
        compiler_params=_params("arbitrary", "arbitrary"),
    )(*[g.reshape(n, rows, cols) for g in gs], w.reshape(nl * rows, cols), m.reshape(nl * rows, cols), v.reshape(nl * rows, cols))
    return [o.reshape(shape) for o in out]


def adam_lb(gstack, lb, m, v):
    n, nl = gstack.shape[0], lb.shape[0]

    def body(g_ref, w_ref, m_ref, v_ref, go_ref, d_ref, mo_ref, vo_ref):
        rows = [w_ref[i:i + 1, :] for i in range(nl)]
        ct = []
        for i in range(nl):
            g = g_ref[0, i:i + 1, :]
            for j in range(1, n):
                g = g + g_ref[j, i:i + 1, :]
            ct.append(g)
        _, vjp = jax.vjp(_lower_bounds, rows)
        (grads,) = vjp(ct)
        for i in range(nl):
            delta, m1, v1 = _adam_math(grads[i], rows[i], m_ref[i:i + 1, :], v_ref[i:i + 1, :])
            go_ref[i:i + 1, :] = grads[i]
            d_ref[i:i + 1, :] = delta
            mo_ref[i:i + 1, :] = m1
            vo_ref[i:i + 1, :] = v1

    return pl.pallas_call(body, name="adam_hgrn_lb", out_shape=[jax.ShapeDtypeStruct(lb.shape, F32)] * 4,
                          compiler_params=_params())(gstack, lb, m, v)


class _Ride:
    def __init__(self, items):
        self.items = list(items)
        n = len(self.items)
        self.srcs = [src for src, _ in self.items]
        self.in_specs = [pl.BlockSpec(memory_space=pl.ANY)] * n
        self.out_specs = [pl.BlockSpec(memory_space=pl.ANY)] * n
        self.out_shape = [jax.ShapeDtypeStruct(((N_DEV,) + s.shape) if mode == "gather" else s.shape, s.dtype)
                          for s, mode in self.items]
        self.scratch = [pltpu.SemaphoreType.DMA((n, N_DEV - 1)), pltpu.SemaphoreType.DMA((n, N_DEV - 1)),
                        pltpu.SemaphoreType.DMA((n,))] if n else []

    def split(self, refs, n_in, n_out):
        n = len(self.items)
        a, b = n_in + n, n_in + 2 * n + n_out
        return refs[:n_in], refs[n_in:a], refs[a:a + n_out], refs[a + n_out:b], refs[b:]

    def _copies(self, srcs, outs, sems):
        send_sems, recv_sems, local_sems = sems
        x, y, c = lax.axis_index("x"), lax.axis_index("y"), lax.axis_index("c")
        me = 4 * x + 2 * y + c
        copies = []
        for i, (_, mode) in enumerate(self.items):
            mine = srcs[i] if mode == "gather" else srcs[i].at[me]
            copies.append(pltpu.make_async_copy(mine, outs[i].at[me], local_sems.at[i]))
            for p in range(1, N_DEV):
                px = 1 - x if p & 4 else x
                py = 1 - y if p & 2 else y
                pc = 1 - c if p & 1 else c
                part = srcs[i] if mode == "gather" else srcs[i].at[4 * px + 2 * py + pc]
                copies.append(pltpu.make_async_remote_copy(
                    src_ref=part, dst_ref=outs[i].at[me], send_sem=send_sems.at[i, p - 1], recv_sem=recv_sems.at[i, p - 1],
                    device_id=(px, py, pc), device_id_type=pl.DeviceIdType.MESH))
        return copies

    def run(self, srcs, outs, scratch, grid=()):
        if not self.items:
            return
        sems = scratch[len(scratch) - 3:]
        if not grid:
            copies = self._copies(srcs, outs, sems)
            for cp in copies:
                cp.start()
            for cp in copies:
                cp.wait()
            return
        ids = [pl.program_id(a) for a in range(len(grid))]
        first = functools.reduce(lambda a, b: a & b, [i == 0 for i in ids])
        last = functools.reduce(lambda a, b: a & b, [i == g - 1 for i, g in zip(ids, grid)])

        @pl.when(first)
        def _():
            for cp in self._copies(srcs, outs, sems):
                cp.start()

        @pl.when(last)
        def _():
            for cp in self._copies(srcs, outs, sems):
                cp.wait()


def exchange(name, items):
    rd = _Ride(items)

    def body(*refs):
        _, srcs, _, outs, scratch = rd.split(refs, 0, 0)
        rd.run(srcs, outs, scratch)

    return pl.pallas_call(body, name=name, in_specs=rd.in_specs, out_specs=rd.out_specs, out_shape=rd.out_shape,
                          scratch_shapes=rd.scratch)(*rd.srcs)


def _from_gather(name, g):
    if name in COL_SHARDED:
        _, k, n = g.shape
        return g.transpose(1, 0, 2).reshape(k, N_DEV * n)
    return g.reshape(-1, g.shape[-1])


def _to_slabs(name, w):
    k, n = w.shape
    if name in COL_SHARDED:
        return w.reshape(k, N_DEV, n // N_DEV).transpose(1, 0, 2)
    return w.reshape(N_DEV, k // N_DEV, n)


def _w_in_internal(w):
    return jnp.pad(w, ((0, 0), (0, LANES - QK_ROPE)))


def _qb_internal(w, inverse=False):
    h, n, r2 = MLA_HEADS, QK_NOPE, QK_ROPE // 2
    lead = w.shape[:-1]
    if not inverse:
        w = w.reshape(lead + (h, n + 2 * r2))
        parts = [w[..., :n], w[..., n:n + r2], w[..., n + r2:]]
        return jnp.concatenate([p.reshape(lead + (-1,)) for p in parts], axis=-1)
    parts = [w[..., :h * n].reshape(lead + (h, n)), w[..., h * n:h * (n + r2)].reshape(lead + (h, r2)),
             w[..., h * (n + r2):].reshape(lead + (h, r2))]
    return jnp.concatenate(parts, axis=-1).reshape(lead + (-1,))


def _kvb_internal(w, inverse=False):
    h, n, vd = MLA_HEADS, QK_NOPE, V_HEAD
    lead = w.shape[:-1]
    if not inverse:
        w = w.reshape(lead + (h, n + vd))
        return jnp.concatenate([w[..., :n].reshape(lead + (-1,)), w[..., n:].reshape(lead + (-1,))], axis=-1)
    parts = [w[..., :h * n].reshape(lead + (h, n)), w[..., h * n:].reshape(lead + (h, vd))]
    return jnp.concatenate(parts, axis=-1).reshape(lead + (-1,))


def _mla_forward(h, w, tabs, ride=()):
    cos, sin = tabs
    r2 = MLA_HEADS * (QK_ROPE // 2)
    proj = mm3("mla_proj", h, w['w_in'])
    qn, kvn, krt = rowwise(
        "mla_mid", lambda rv, ev, gv: (f_mla_mid(rv, ev, gv), []),
        [_view(proj, 0, Q_LORA), _view(proj, Q_LORA, KV_LORA), _view(proj, Q_LORA + KV_LORA, LANES), _full(cos), _full(sin)],
        [], [w['q_norm'], w['kv_norm']], [(BF16, [Q_LORA]), (BF16, [KV_LORA]), (BF16, [r2, r2])], ts=ROW_TILE)
    q = mm3("mla_q", qn, w['w_qb'])
    kv = mm3("mla_kv", kvn, w['w_kvb'])
    qh, kh, vh = mla_heads(q, kv, krt, cos, sin)
    o, got = attn_fwd(qh, kh, vh, ride)
    y = mm3("mla_out", o, w['w_o'])
    return y, dict(h=h, proj=proj, qn=qn, kvn=kvn, qh=qh, kh=kh, vh=vh, o=o), got


def _mla_backward(dy, sv, w, tabs, ride=()):
    cos, sin = tabs
    r2 = MLA_HEADS * (QK_ROPE // 2)
    g = {}
    g['w_o'] = wgrad("mla_out_dw", sv['o'], dy)
    do = mm3("mla_out_dx", dy, w['w_o'], tb=True)
    dqh, dkh, dvh, got = attn_bwd(sv['qh'], sv['kh'], sv['vh'], do, ride)
    dq, dkv, dkrt = mla_heads_bwd(dqh, dkh, dvh, cos, sin)
    g['w_qb'] = wgrad("mla_q_dw", sv['qn'], dq)
    g['w_kvb'] = wgrad("mla_kv_dw", sv['kvn'], dkv)
    dqn = mm3("mla_q_dx", dq, w['w_qb'], tb=True)
    dkvn = mm3("mla_kv_dx", dkv, w['w_kvb'], tb=True)
    proj = sv['proj']
    (dproj,), _, (g['q_norm'], g['kv_norm']) = rowwise_bwd(
        "mla_mid_bwd", f_mla_mid,
        [_view(proj, 0, Q_LORA), _view(proj, Q_LORA, KV_LORA), _view(proj, Q_LORA + KV_LORA, LANES), _full(cos), _full(sin)],
        [], [w['q_norm'], w['kv_norm']], [(dqn, [Q_LORA]), (dkvn, [KV_LORA]), (dkrt, [r2, r2])],
        [(BF16, [0, 1, 2])], ts=ROW_TILE, n_diff=3)
    g['w_in'] = wgrad("mla_proj_dw", sv['h'], dproj)
    dh = mm3("mla_proj_dx", dproj, w['w_in'], tb=True)
    return dh, g, got


def _hgrn_views(proj):
    d = proj.shape[-1] // 4
    return [_view(proj, i * d, d) for i in range(4)]


def _head_views(a, col0, nh):
    return [_view(a, col0 + i * HGRN_EXPAND, HGRN_EXPAND) for i in range(nh)]


def _hgrn_forward(h, w, ride=()):
    proj = mm3("hgrn_proj", h, w['w_in'])
    d = proj.shape[-1] // 4
    nh = d // HGRN_EXPAND
    vq, vf, _, _ = _hgrn_views(proj)
    qs, logf, kk = rowwise("hgrn_pre", lambda rv, ev, gv: (f_hg_pre(rv, ev, gv), []), [vq, vf], [], [w['lb']],
                           [(F32, [d]), (F32, [d]), (F32, [d])], ts=ROW_TILE)
    o, states, got = hgrn_fwd(qs, logf, kk, proj, ride)
    (z,) = rowwise("hgrn_post", lambda rv, ev, gv: (f_hg_post(rv, ev, gv), []),
                   _head_views(o, 0, nh) + _head_views(proj, 3 * d, nh), [], [w['g_norm']],
                   [(BF16, [HGRN_EXPAND] * nh)], ts=ROW_TILE)
    y = mm3("hgrn_out", z, w['w_o'])
    return y, dict(h=h, proj=proj, qs=qs, logf=logf, kk=kk, o=o, states=states, z=z), got


def _hgrn_backward(dy, sv, w, ride=()):
    g = {}
    proj, o = sv['proj'], sv['o']
    d = proj.shape[-1] // 4
    nh = d // HGRN_EXPAND
    g['w_o'] = wgrad("hgrn_out_dw", sv['z'], dy)
    dz = mm3("hgrn_out_dx", dy, w['w_o'], tb=True)
    (do, dgate), _, (g['g_norm'],) = rowwise_bwd(
        "hgrn_post_bwd", f_hg_post, _head_views(o, 0, nh) + _head_views(proj, 3 * d, nh), [], [w['g_norm']],
        [(dz, [HGRN_EXPAND] * nh)], [(F32, list(range(nh))), (F32, list(range(nh, 2 * nh)))], ts=ROW_TILE, n_diff=2 * nh)
    dqs, dlogf, dkk, dv, got = hgrn_bwd(sv['qs'], sv['logf'], sv['kk'], proj, sv['states'], do, ride)
    (dproj,), _, (g['lb'],) = rowwise_bwd(
        "hgrn_pre_bwd", f_hg_pre_with_iv, _hgrn_views(proj), [], [w['lb']],
        [(dqs, [d]), (dlogf, [d]), (dkk, [d]), (dv, [d]), (dgate, [d])], [(BF16, [0, 1, 2, 3])], ts=ROW_TILE // 2, n_diff=4)
    g['w_in'] = wgrad("hgrn_proj_dw", sv['h'], dproj)
    dh = mm3("hgrn_proj_dx", dproj, w['w_in'], tb=True)
    return dh, g, got


def _ffn_forward(h, w):
    u = mm3("ffn_in", h, w['w_in'])
    dff = u.shape[-1] // 2
    (a,) = rowwise("ffn_act", lambda rv, ev, gv: (f_swiglu(rv, ev, gv), []), [_view(u, 0, dff), _view(u, dff, dff)],
                   [], [], [(BF16, [dff])], ts=ROW_TILE)
    y = mm3("ffn_out", a, w['w_out'])
    return y, dict(h=h, u=u, a=a)


def _ffn_backward(dy, sv, w):
    g = {}
    u = sv['u']
    dff = u.shape[-1] // 2
    g['w_out'] = wgrad("ffn_out_dw", sv['a'], dy)
    da = mm3("ffn_out_dx", dy, w['w_out'], tb=True)
    (du,), _, _ = rowwise_bwd("ffn_act_bwd", f_swiglu, [_view(u, 0, dff), _view(u, dff, dff)], [], [],
                              [(da, [dff])], [(BF16, [0, 1])], ts=ROW_TILE // 2, n_diff=2)
    g['w_in'] = wgrad("ffn_in_dw", sv['h'], du)
    dh = mm3("ffn_in_dx", du, w['w_in'], tb=True)
    return dh, g


def kernel(x, c, positions, mla_w_in, mla_q_norm, mla_w_qb, mla_kv_norm, mla_w_kvb, mla_w_o, hgrn_lb, hgrn_w_in, hgrn_g_norm, hgrn_w_o, ffn_w_in, ffn_w_out, ada_w, ada_b, ln_g, ln_b, loss_target, m_mla_w_in, m_mla_q_norm, m_mla_w_qb, m_mla_kv_norm, m_mla_w_kvb, m_mla_w_o, m_hgrn_lb, m_hgrn_w_in, m_hgrn_g_norm, m_hgrn_w_o, m_ffn_w_in, m_ffn_w_out, m_ada_w, m_ada_b, m_ln_g, m_ln_b, v_mla_w_in, v_mla_q_norm, v_mla_w_qb, v_mla_kv_norm, v_mla_w_kvb, v_mla_w_o, v_hgrn_lb, v_hgrn_w_in, v_hgrn_g_norm, v_hgrn_w_o, v_ffn_w_in, v_ffn_w_out, v_ada_w, v_ada_b, v_ln_g, v_ln_b):
    W = dict(zip(WEIGHTS, (mla_w_in, mla_q_norm, mla_w_qb, mla_kv_norm, mla_w_kvb, mla_w_o, hgrn_lb, hgrn_w_in, hgrn_g_norm,
                           hgrn_w_o, ffn_w_in, ffn_w_out, ada_w, ada_b, ln_g, ln_b)))
    M1 = dict(zip(WEIGHTS, (m_mla_w_in, m_mla_q_norm, m_mla_w_qb, m_mla_kv_norm, m_mla_w_kvb, m_mla_w_o, m_hgrn_lb, m_hgrn_w_in,
                            m_hgrn_g_norm, m_hgrn_w_o, m_ffn_w_in, m_ffn_w_out, m_ada_w, m_ada_b, m_ln_g, m_ln_b)))
    M2 = dict(zip(WEIGHTS, (v_mla_w_in, v_mla_q_norm, v_mla_w_qb, v_mla_kv_norm, v_mla_w_kvb, v_mla_w_o, v_hgrn_lb, v_hgrn_w_in,
                            v_hgrn_g_norm, v_hgrn_w_o, v_ffn_w_in, v_ffn_w_out, v_ada_w, v_ada_b, v_ln_g, v_ln_b)))
    bsz, seq, d = x.shape
    depth, n_mla, n_hgrn = ffn_w_in.shape[0], mla_w_in.shape[0], hgrn_w_in.shape[0]
    n_sub = 2 * depth

    big = COL_SHARDED + ROW_SHARDED
    wb = {n: cast_bf16("cast_" + n, W[n]) for n in big}

    def mixer_names(layer):
        mixer = ['mla_w_in', 'mla_w_qb', 'mla_w_kvb', 'mla_w_o'] if layer % 2 == 0 else ['hgrn_w_in', 'hgrn_w_o']
        return [(n, layer // 2) for n in mixer]

    def carried(layer):
        return [('ffn_w_in', layer), ('ffn_w_out', layer)] + (mixer_names(layer + 1) if layer + 1 < depth else [])

    def weight_items(names):
        return [(wb[n][j], "gather") for n, j in names]

    G = {}
    internal = {'mla_w_in': _w_in_internal, 'mla_w_qb': _qb_internal, 'mla_w_kvb': _kvb_internal}

    def take_weights(names, got):
        for (n, j), a in zip(names, got):
            G[n, j] = internal.get(n, lambda w: w)(_from_gather(n, a))

    lower_shard = lower_bounds(hgrn_lb)
    got = exchange("gather_first", [(lower_shard, "gather"), (ln_g, "gather"), (ln_b, "gather"), (c, "gather")]
                   + weight_items(mixer_names(0)))
    lower_all = got[0].transpose(1, 0, 2).reshape(n_hgrn, -1)
    ln_g_all = got[1].transpose(1, 2, 0, 3).reshape(depth, 2, d)
    ln_b_all = got[2].transpose(1, 2, 0, 3).reshape(depth, 2, d)
    c_all = got[3].reshape(N_DEV * bsz, d)
    take_weights(mixer_names(0), got[4:])

    cols = ada_w.shape[-1]
    mod_loc = ada_fwd(c_all, ada_w, ada_b)
    (mod_got,) = exchange("scatter_mod", [(mod_loc.reshape(n_sub, N_DEV, bsz, cols).transpose(1, 0, 2, 3), "a2a")])
    mod = mod_got.transpose(1, 2, 0, 3).reshape(n_sub, bsz, 1, 3 * d)
    shift = [mod[k, :, :, 0:d] for k in range(n_sub)]
    scale = [mod[k, :, :, d:2 * d] for k in range(n_sub)]
    gate = [mod[k, :, :, 2 * d:] for k in range(n_sub)]
    lng = [ln_g_all[k // 2, k % 2][None, :] for k in range(n_sub)]
    lnb = [ln_b_all[k // 2, k % 2][None, :] for k in range(n_sub)]

    tabs = rope_tables(positions)

    def sub_weights(k):
        layer, j = k // 2, k // 4
        if k % 2:
            return 'ffn', layer, dict(w_in=G['ffn_w_in', layer], w_out=G['ffn_w_out', layer])
        if layer % 2 == 0:
            return 'mla', j, dict(w_in=G['mla_w_in', j], q_norm=mla_q_norm[j][None, :], w_qb=G['mla_w_qb', j],
                                  kv_norm=mla_kv_norm[j][None, :], w_kvb=G['mla_w_kvb', j], w_o=G['mla_w_o', j])
        return 'hgrn', j, dict(w_in=G['hgrn_w_in', j], lb=lower_all[j][None, :], g_norm=hgrn_g_norm[j][None, :],
                               w_o=G['hgrn_w_o', j])

    (h,) = rowwise("mod_first", lambda rv, ev, gv: (f_mod(rv, ev, gv), []), [_full(x)], [scale[0], shift[0]], [],
                   [(BF16, [d])], ts=ROW_TILE)
    xs, ys, saved = [x], [], []
    loss_acc = None
    for k in range(n_sub):
        kind, _, w = sub_weights(k)
        ride = weight_items(carried(k // 2)) if k % 2 == 0 else []
        if kind == 'ffn':
            y, sv = _ffn_forward(h, w)
        elif kind == 'mla':
            y, sv, got = _mla_forward(h, w, tabs, ride)
        else:
            y, sv, got = _hgrn_forward(h, w, ride)
        if ride:
            take_weights(carried(k // 2), got)
        ys.append(y)
        saved.append(sv)
        if k + 1 < n_sub:
            xn, h = rowwise("ln_mod", lambda rv, ev, gv: (f_ln_mod(rv, ev, gv), []), [_full(xs[k]), _full(y)],
                            [gate[k], scale[k + 1], shift[k + 1]], [lng[k], lnb[k]], [(F32, [d]), (BF16, [d])], ts=ROW_TILE)
            xs.append(xn)
        else:
            def loss_rows(rv, ev, gv):
                (row,) = f_ln_loss(rv, ev, gv)
                return [], [jnp.broadcast_to(jnp.sum(row, keepdims=True), (1, LANES))]
            (loss_acc,) = rowwise("ln_loss", loss_rows, [_full(xs[k]), _full(y), _full(loss_target)], [gate[k]], [lng[k], lnb[k]],
                                  [], ts=ROW_TILE, accs=[LANES])
    loss = lax.psum(loss_acc[0, 0], ("x", "y", "c"))

    d_shift, d_scale, d_gate = [None] * n_sub, [None] * n_sub, [None] * n_sub
    d_lng, d_lnb = [None] * n_sub, [None] * n_sub
    part = {n: [None] * W[n].shape[0] for n in ['mla_q_norm', 'mla_kv_norm', 'hgrn_g_norm']}
    recv = {n: [None] * W[n].shape[0] for n in big}
    d_lower = [None] * n_hgrn
    k = n_sub - 1
    (dx, dy), (d_gate[k],), (d_lng[k], d_lnb[k]) = rowwise_bwd(
        "ln_loss_bwd", f_ln_loss, [_full(xs[k]), _full(ys[k]), _full(loss_target)], [gate[k]], [lng[k], lnb[k]], [],
        [(F32, [0]), (BF16, [1])], ts=ROW_TILE, n_diff=2, unit_ct=1)
    grad_x = None
    mine = {}

    def take_grads(names, got):
        for (n, j), a in zip(names, got):
            recv[n][j] = a

    def grad_items(names):
        return [(_to_slabs(n, mine[n, jj]), "a2a") for n, jj in names]

    for k in range(n_sub - 1, -1, -1):
        kind, j, w = sub_weights(k)
        ride = grad_items(carried(k // 2)) if kind != 'ffn' else []
        if kind == 'ffn':
            dh, g = _ffn_backward(dy, saved[k], w)
            new = {('ffn_w_in', j): g['w_in'], ('ffn_w_out', j): g['w_out']}
        elif kind == 'mla':
            dh, g, got = _mla_backward(dy, saved[k], w, tabs, ride)
            new = {('mla_w_in', j): g['w_in'][:, :mla_w_in.shape[-1] * N_DEV], ('mla_w_qb', j): _qb_internal(g['w_qb'], inverse=True),
                   ('mla_w_kvb', j): _kvb_internal(g['w_kvb'], inverse=True), ('mla_w_o', j): g['w_o']}
            part['mla_q_norm'][j], part['mla_kv_norm'][j] = g['q_norm'][0], g['kv_norm'][0]
        else:
            dh, g, got = _hgrn_backward(dy, saved[k], w, ride)
            new = {('hgrn_w_in', j): g['w_in'], ('hgrn_w_o', j): g['w_o']}
            part['hgrn_g_norm'][j] = g['g_norm'][0]
            d_lower[j] = g['lb'][0]
        if kind != 'ffn':
            take_grads(carried(k // 2), got)
        mine.update(new)
        if k:
            (dx, dy), (d_gate[k - 1], d_scale[k], d_shift[k]), (d_lng[k - 1], d_lnb[k - 1]) = rowwise_bwd(
                "ln_mod_bwd", f_ln_mod, [_full(xs[k - 1]), _full(ys[k - 1])], [gate[k - 1], scale[k], shift[k]],
                [lng[k - 1], lnb[k - 1]], [(dx, [d]), (dh, [d])], [(F32, [0]), (BF16, [1])], ts=ROW_TILE, n_diff=2)
        else:
            (grad_x,), (d_scale[0], d_shift[0]), _ = rowwise_bwd(
                "mod_first_bwd", f_mod_with_x, [_full(x)], [scale[0], shift[0]], [], [(dh, [d]), (dx, [d])],
                [(F32, [0])], ts=ROW_TILE, n_diff=1)

    waiting = mixer_names(0)
    slabs = grad_items(waiting)
    slabs.append((jnp.stack(d_lower).reshape(n_hgrn, N_DEV, -1).transpose(1, 0, 2), "a2a"))
    for parts in (d_lng, d_lnb):
        full = jnp.stack([p[0] for p in parts]).reshape(depth, 2, N_DEV, d // N_DEV)
        slabs.append((full.transpose(2, 0, 1, 3), "a2a"))
    dmod = jnp.concatenate([jnp.stack(d_shift), jnp.stack(d_scale), jnp.stack(d_gate)], axis=-1)
    slabs.append((dmod.reshape(n_sub, bsz, N_DEV, cols).transpose(2, 0, 1, 3), "a2a"))
    small = ['mla_q_norm', 'mla_kv_norm', 'hgrn_g_norm']
    slabs += [(jnp.stack(part[n]), "gather") for n in small]
    got = exchange("scatter_last", slabs)
    take_grads(waiting, got)
    stacks = dict(zip(['hgrn_lb', 'ln_g', 'ln_b', 'dmod'] + small, got[len(waiting):]))

    dmod_all = stacks['dmod'].transpose(1, 0, 2, 3).reshape(n_sub, N_DEV * bsz, cols)
    g_ada_w, g_ada_b = ada_bwd(c_all, dmod_all)
    stacks['ada_w'] = g_ada_w.reshape((1,) + ada_w.shape)
    stacks['ada_b'] = g_ada_b.reshape((1,) + ada_b.shape)

    res = {}
    for n in WEIGHTS:
        if n == 'hgrn_lb':
            res[n] = adam_lb(stacks[n], W[n], M1[n], M2[n])
        elif n in big:
            res[n] = adam_layers("adam_" + n, recv[n], W[n], M1[n], M2[n])
        else:
            res[n] = adam("adam_" + n, stacks[n], W[n], M1[n], M2[n])
    return (loss, grad_x, *[res[n][0] for n in WEIGHTS], *[res[n][1] for n in WEIGHTS], *[res[n][2] for n in WEIGHTS],
            *[res[n][3] for n in WEIGHTS])
```

```python
import functools
import math

import numpy as np
import jax
import jax.numpy as jnp
from jax import lax
from jax.experimental import pallas as pl
from jax.experimental.pallas import tpu as pltpu

F32 = jnp.float32
BF16 = jnp.bfloat16

N_DEV = 8
LANES = 128
VMEM_LIMIT = 52 * 1024 * 1024

D_MODEL = 1024
DEPTH = 4
MLA_HEADS = 16
QK_NOPE = 64
QK_ROPE = 32
V_HEAD = 64
Q_LORA = 768
KV_LORA = 256
ROPE_THETA = 10000.0
HGRN_EXPAND = 128
HGRN_CHUNK = 128
HGRN_HEADS_PER_STEP = 4
D_FF = 2816
ALPHA = (2.0 * DEPTH) ** 0.25
LN_EPS = 1e-5
RMS_EPS = 1e-6
ADAM_LR = 0.001
ADAM_B1 = 0.9
ADAM_B2 = 0.999
ADAM_EPS = 1e-08
ADAM_WD = 0.01
ADAM_STEP = 10

ATTN_TQ = 256
ROW_TILE = 256

WEIGHTS = ['mla_w_in', 'mla_q_norm', 'mla_w_qb', 'mla_kv_norm', 'mla_w_kvb', 'mla_w_o', 'hgrn_lb', 'hgrn_w_in',
           'hgrn_g_norm', 'hgrn_w_o', 'ffn_w_in', 'ffn_w_out', 'ada_w', 'ada_b', 'ln_g', 'ln_b']
COL_SHARDED = ['mla_w_in', 'mla_w_qb', 'mla_w_kvb', 'hgrn_w_in', 'ffn_w_in']
ROW_SHARDED = ['mla_w_o', 'hgrn_w_o', 'ffn_w_out']


def _params(*sem):
    if sem:
        return pltpu.CompilerParams(dimension_semantics=sem, vmem_limit_bytes=VMEM_LIMIT)
    return pltpu.CompilerParams(vmem_limit_bytes=VMEM_LIMIT)


def _pick(n, cap):
    best = None
    for t in range(LANES, min(n, cap) + 1, LANES):
        if n % t == 0:
            best = t
    return best or n


def _pick_rows(n, cap):
    best = None
    for t in range(8, min(n, cap) + 1, 8):
        if n % t == 0:
            best = t
    return best or n


def matmul(name, a, b, *, ta=False, tb=False, out_dtype=F32, tm_cap=1024, tn_cap=1536, tk_cap=2048):
    parts = list(a) if isinstance(a, (list, tuple)) else [a]
    n_parts = len(parts)
    assert n_parts == 1 or not ta
    (kp, m) = parts[0].shape if ta else parts[0].shape[::-1]
    (n, k2) = b.shape if tb else b.shape[::-1]
    assert kp * n_parts == k2, (name, parts[0].shape, b.shape)
    tm, tn, tk = _pick(m, tm_cap), _pick(n, tn_cap), _pick(kp, tk_cap)
    nkp = kp // tk
    nk = nkp * n_parts
    dims = (((0 if ta else 1,), (1 if tb else 0,)), ((), ()))

    def prod(a_ref, b_ref):
        return lax.dot_general(a_ref[...].astype(BF16), b_ref[...].astype(BF16), dims, preferred_element_type=F32)

    if nk == 1:
        def body(a_ref, b_ref, o_ref):
            o_ref[...] = prod(a_ref, b_ref).astype(o_ref.dtype)
        scratch = []
    else:
        def body(*refs):
            a_refs, (b_ref, o_ref, acc_ref) = refs[:n_parts], refs[n_parts:]
            k = pl.program_id(2)

            @pl.when(k == 0)
            def _():
                acc_ref[...] = jnp.zeros_like(acc_ref)

            if n_parts == 1:
                acc_ref[...] += prod(a_refs[0], b_ref)
            else:
                for p in range(n_parts):
                    @pl.when((k >= p * nkp) & (k < (p + 1) * nkp))
                    def _(p=p):
                        acc_ref[...] += prod(a_refs[p], b_ref)

            @pl.when(k == nk - 1)
            def _():
                o_ref[...] = acc_ref[...].astype(o_ref.dtype)
        scratch = [pltpu.VMEM((tm, tn), F32)]

    if ta:
        a_specs = [pl.BlockSpec((tk, tm), lambda i, j, k: (k, i))]
    elif n_parts == 1:
        a_specs = [pl.BlockSpec((tm, tk), lambda i, j, k: (i, k))]
    else:
        a_specs = [pl.BlockSpec((tm, tk), lambda i, j, k, p=p: (i, jnp.clip(k - p * nkp, 0, nkp - 1))) for p in range(n_parts)]
    b_spec = pl.BlockSpec((tn, tk), lambda i, j, k: (j, k)) if tb else pl.BlockSpec((tk, tn), lambda i, j, k: (k, j))
    return pl.pallas_call(
        body, name=name, grid=(m // tm, n // tn, nk),
        in_specs=a_specs + [b_spec], out_specs=pl.BlockSpec((tm, tn), lambda i, j, k: (i, j)),
        out_shape=jax.ShapeDtypeStruct((m, n), out_dtype), scratch_shapes=scratch,
        compiler_params=_params("parallel", "parallel", "arbitrary"),
    )(*parts, b)


def mm3(name, a3, w, **kw):
    parts = list(a3) if isinstance(a3, (list, tuple)) else [a3]
    bsz, s = parts[0].shape[:2]
    flat = [p.reshape(bsz * s, p.shape[-1]) for p in parts]
    out = matmul(name, flat if len(flat) > 1 else flat[0], w, **kw)
    return out.reshape(bsz, s, out.shape[-1])


def ffn_in_act(h3, w_in):
    bsz, s, k = h3.shape
    m, dff = bsz * s, w_in.shape[1] // 2
    tm, tn = _pick(m, 512), _pick(dff, 1536)
    nj = dff // tn

    def body(h_ref, wg_ref, wu_ref, ug_ref, uu_ref, a_ref):
        hv = h_ref[...].astype(BF16)
        ug = _dg(hv, wg_ref[...].astype(BF16), 1, 0)
        uu = _dg(hv, wu_ref[...].astype(BF16), 1, 0)
        ug_ref[...] = ug
        uu_ref[...] = uu
        a_ref[...] = (jax.nn.silu(ug) * uu).astype(a_ref.dtype)

    out = pl.BlockSpec((tm, tn), lambda j, i: (i, j))
    res = pl.pallas_call(
        body, name="ffn_in_act", grid=(nj, m // tm),
        in_specs=[pl.BlockSpec((tm, k), lambda j, i: (i, 0)), pl.BlockSpec((k, tn), lambda j, i: (0, j)),
                  pl.BlockSpec((k, tn), lambda j, i: (0, nj + j))],
        out_specs=[out, out, out],
        out_shape=[jax.ShapeDtypeStruct((m, dff), F32), jax.ShapeDtypeStruct((m, dff), F32), jax.ShapeDtypeStruct((m, dff), BF16)],
        compiler_params=_params("arbitrary", "arbitrary"),
    )(h3.reshape(m, k), w_in, w_in)
    return [r.reshape(bsz, s, dff) for r in res]


def ffn_out_dx_act(dy3, w_out, ug, uu):
    bsz, s, d = dy3.shape
    m, dff = bsz * s, w_out.shape[0]
    tm, tn = _pick(m, 512), _pick(dff, 1536)

    def body(dy_ref, w_ref, ug_ref, uu_ref, dg_ref, du_ref):
        da = _dg(dy_ref[...].astype(BF16), w_ref[...].astype(BF16), 1, 1)
        _, vjp = jax.vjp(lambda gate, up: jax.nn.silu(gate) * up, ug_ref[...], uu_ref[...])
        dg, du = vjp(da)
        dg_ref[...] = dg.astype(dg_ref.dtype)
        du_ref[...] = du.astype(du_ref.dtype)

    blk = pl.BlockSpec((tm, tn), lambda j, i: (i, j))
    res = pl.pallas_call(
        body, name="ffn_out_dx_act", grid=(dff // tn, m // tm),
        in_specs=[pl.BlockSpec((tm, d), lambda j, i: (i, 0)), pl.BlockSpec((tn, d), lambda j, i: (j, 0)), blk, blk],
        out_specs=[blk, blk], out_shape=[jax.ShapeDtypeStruct((m, dff), BF16)] * 2,
        compiler_params=_params("arbitrary", "arbitrary"),
    )(dy3.reshape(m, d), w_out, ug.reshape(m, dff), uu.reshape(m, dff))
    return [r.reshape(bsz, s, dff) for r in res]


def wgrad(name, a3, g3):
    bsz, s, k = a3.shape
    return matmul(name, a3.reshape(bsz * s, k), g3.reshape(bsz * s, g3.shape[-1]), ta=True, out_dtype=BF16)


def _dg(a, b, ca, cb, **kw):
    return lax.dot_general(a, b, (((ca,), (cb,)), ((), ())), preferred_element_type=F32, **kw)


@functools.partial(jax.custom_vjp, nondiff_argnums=(2, 3))
def bdot(a, b, ca, cb):
    return _dg(a.astype(BF16), b.astype(BF16), ca, cb)


def _bdot_fwd(a, b, ca, cb):
    return bdot(a, b, ca, cb), (a, b)


def _bdot_bwd(ca, cb, res, g):
    a, b = res
    a16, b16, g16 = a.astype(BF16), b.astype(BF16), g.astype(BF16)
    if ca == 1:
        da = _dg(g16, b16, 1, 1 if cb == 0 else 0)
    else:
        da = _dg(b16, g16, 1 if cb == 0 else 0, 1)
    if cb == 0:
        db = _dg(a16, g16, 0 if ca == 1 else 1, 0)
    else:
        db = _dg(g16, a16, 0, 0 if ca == 1 else 1)
    return da, db


bdot.defvjp(_bdot_fwd, _bdot_bwd)


def hdot(a, b, ca=1, cb=0):
    return _dg(a, b, ca, cb, precision=lax.Precision.HIGHEST)


def _row_specs(rows, exs, globs, ts):
    specs = [pl.BlockSpec((1, ts, w), lambda b, s, j=j: (b, s, j)) for (_, j, w) in rows]
    specs += [pl.BlockSpec((1, 1, e.shape[-1]), lambda b, s: (b, 0, 0)) for e in exs]
    specs += [pl.BlockSpec((1, g.shape[-1]), lambda b, s: (0, 0)) for g in globs]
    return specs


def _store_pieces(o_ref, pieces, widths):
    off = 0
    for p, w in zip(pieces, widths):
        o_ref[0, :, off:off + w] = p.astype(o_ref.dtype)
        off += w


def _load_pieces(c_ref, widths):
    out, off = [], 0
    for w in widths:
        out.append(c_ref[0, :, off:off + w].astype(F32))
        off += w
    return out


def rowwise(name, f, rows, exs, globs, outs, *, ts, accs=()):
    bsz, s = rows[0][0].shape[:2]
    ts = min(ts, s)
    n_r, n_e, n_g, n_o = len(rows), len(exs), len(globs), len(outs)

    def body(*refs):
        rv = [r[0].astype(F32) for r in refs[:n_r]]
        ev = [e[0] for e in refs[n_r:n_r + n_e]]
        gv = [g[...] for g in refs[n_r + n_e:n_r + n_e + n_g]]
        o_refs = refs[n_r + n_e + n_g:n_r + n_e + n_g + n_o]
        a_refs = refs[n_r + n_e + n_g + n_o:]
        pieces, sums = f(rv, ev, gv)
        idx = 0
        for o_ref, (_, ws) in zip(o_refs, outs):
            _store_pieces(o_ref, pieces[idx:idx + len(ws)], ws)
            idx += len(ws)
        if accs:
            @pl.when((pl.program_id(0) == 0) & (pl.program_id(1) == 0))
            def _():
                for a_ref in a_refs:
                    a_ref[...] = jnp.zeros_like(a_ref)
            for a_ref, val in zip(a_refs, sums):
                a_ref[...] += val

    out_specs = [pl.BlockSpec((1, ts, sum(ws)), lambda b, s: (b, s, 0)) for (_, ws) in outs]
    out_specs += [pl.BlockSpec((1, w), lambda b, s: (0, 0)) for w in accs]
    out_shape = [jax.ShapeDtypeStruct((bsz, s, sum(ws)), dt) for (dt, ws) in outs]
    out_shape += [jax.ShapeDtypeStruct((1, w), F32) for w in accs]
    return pl.pallas_call(
        body, name=name, grid=(bsz, s // ts),
        in_specs=_row_specs(rows, exs, globs, ts), out_specs=out_specs, out_shape=out_shape,
        compiler_params=_params("arbitrary", "arbitrary"),
    )(*[r[0] for r in rows], *exs, *globs)


def rowwise_bwd(name, f, rows, exs, globs, cts, d_groups, *, ts, n_diff, unit_ct=0):
    bsz, s = rows[0][0].shape[:2]
    ts = min(ts, s)
    n_r, n_e, n_g, n_c = len(rows), len(exs), len(globs), len(cts)
    n_d = len(d_groups)

    def body(*refs):
        rv = [r[0].astype(F32) for r in refs[:n_r]]
        ev = [e[0] for e in refs[n_r:n_r + n_e]]
        gv = [g[...] for g in refs[n_r + n_e:n_r + n_e + n_g]]
        base = n_r + n_e + n_g
        c_refs = refs[base:base + n_c]
        d_refs = refs[base + n_c:base + n_c + n_d]
        de_refs = refs[base + n_c + n_d:base + n_c + n_d + n_e]
        dg_refs = refs[base + n_c + n_d + n_e:]
        fixed = rv[n_diff:]
        out, vjp = jax.vjp(lambda r, e, g: f(r + fixed, e, g), rv[:n_diff], ev, gv)
        ct = []
        for c_ref, (_, ws) in zip(c_refs, cts):
            ct += _load_pieces(c_ref, ws)
        ct += [jnp.ones_like(o) for o in out[len(ct):]]
        assert len(ct) == len(out) and len(out) - unit_ct == sum(len(ws) for _, ws in cts), name
        d_r, d_e, d_g = vjp(ct)
        for d_ref, (_, idxs) in zip(d_refs, d_groups):
            _store_pieces(d_ref, [d_r[i] for i in idxs], [rows[i][2] for i in idxs])
        first_s = pl.program_id(1) == 0
        if n_e:
            @pl.when(first_s)
            def _():
                for r in de_refs:
                    r[...] = jnp.zeros_like(r)
            for r, val in zip(de_refs, d_e):
                r[0] += val
        if n_g:
            @pl.when(first_s & (pl.program_id(0) == 0))
            def _():
                for r in dg_refs:
                    r[...] = jnp.zeros_like(r)
            for r, val in zip(dg_refs, d_g):
                r[...] += val

    in_specs = _row_specs(rows, exs, globs, ts)
    in_specs += [pl.BlockSpec((1, ts, sum(ws)), lambda b, s: (b, s, 0)) for (_, ws) in cts]
    out_specs = [pl.BlockSpec((1, ts, sum(rows[i][2] for i in idxs)), lambda b, s: (b, s, 0)) for (_, idxs) in d_groups]
    out_specs += [pl.BlockSpec((1, 1, e.shape[-1]), lambda b, s: (b, 0, 0)) for e in exs]
    out_specs += [pl.BlockSpec((1, g.shape[-1]), lambda b, s: (0, 0)) for g in globs]
    out_shape = [jax.ShapeDtypeStruct((bsz, s, sum(rows[i][2] for i in idxs)), dt) for (dt, idxs) in d_groups]
    out_shape += [jax.ShapeDtypeStruct(e.shape, F32) for e in exs]
    out_shape += [jax.ShapeDtypeStruct(g.shape, F32) for g in globs]
    res = pl.pallas_call(
        body, name=name, grid=(bsz, s // ts),
        in_specs=in_specs, out_specs=out_specs, out_shape=out_shape,
        compiler_params=_params("arbitrary", "arbitrary"),
    )(*[r[0] for r in rows], *exs, *globs, *[c[0] for c in cts])
    return res[:n_d], res[n_d:n_d + n_e], res[n_d + n_e:]


def _full(a):
    return (a, 0, a.shape[-1])


def _view(a, col, w):
    assert col % w == 0
    return (a, col // w, w)


def _layer_norm(z, g, b):
    mu = jnp.mean(z, -1, keepdims=True)
    var = jnp.mean(jnp.square(z - mu), -1, keepdims=True)
    return (z - mu) * lax.rsqrt(var + LN_EPS) * g + b


def _rms_norm(z, g):
    ms = jnp.mean(jnp.square(z), -1, keepdims=True)
    return z * lax.rsqrt(ms + RMS_EPS) * g


def f_mod(rv, ev, gv):
    (x,), (scale, shift) = rv, ev
    return [x * (1.0 + scale) + shift]


def f_mod_with_x(rv, ev, gv):
    return f_mod(rv, ev, gv) + [rv[0]]


def f_ln_mod(rv, ev, gv):
    (x, y), (gate, scale, shift), (g, b) = rv, ev, gv
    xn = _layer_norm(ALPHA * x + (1.0 + gate) * y, g, b)
    return [xn, xn * (1.0 + scale) + shift]


def f_ln_loss(rv, ev, gv):
    (x, y, target), (gate,), (g, b) = rv, ev, gv
    xn = _layer_norm(ALPHA * x + (1.0 + gate) * y, g, b)
    return [0.5 * jnp.mean(jnp.square(xn - target), -1, keepdims=True)]


def _head_spread(width):
    r2 = QK_ROPE // 2
    j = lax.broadcasted_iota(jnp.int32, (LANES, width), 0)
    col = lax.broadcasted_iota(jnp.int32, (LANES, width), 1) % r2
    return (j == col).astype(F32), (j == col + r2).astype(F32)


def f_mla_mid(rv, ev, gv):
    (q_lat, kv_lat, kr, cos, sin), (q_g, kv_g) = rv, gv
    e1, e2 = _head_spread(cos.shape[-1])
    k1, k2 = hdot(kr, e1), hdot(kr, e2)
    return [_rms_norm(q_lat, q_g), _rms_norm(kv_lat, kv_g), k1 * cos - k2 * sin, k1 * sin + k2 * cos]


def f_hg_pre(rv, ev, gv):
    (q, fx), (lb,) = rv[:2], gv
    f = lb + (1.0 - lb) * jax.nn.sigmoid(fx)
    return [jax.nn.silu(q), jnp.log(f), 1.0 - f]


def f_hg_pre_with_iv(rv, ev, gv):
    return f_hg_pre(rv, ev, gv) + [rv[2], rv[3]]


def f_hg_post(rv, ev, gv):
    nh = len(rv) // 2
    (gn,) = gv
    return [_rms_norm(rv[h], gn) * jax.nn.silu(rv[nh + h]) for h in range(nh)]


def rope_tables(positions):
    bsz, s = positions.shape
    r2 = QK_ROPE // 2
    width = MLA_HEADS * r2
    inv = (ROPE_THETA ** (-np.arange(0, QK_ROPE, 2, dtype=np.float32) / QK_ROPE)).astype(np.float32)
    inv = jnp.asarray(np.tile(inv, MLA_HEADS)[None, :])
    ts = min(ROW_TILE, s)

    def body(p_ref, inv_ref, cos_ref, sin_ref):
        ang = p_ref[0].astype(F32) * inv_ref[...]
        cos_ref[0] = jnp.cos(ang)
        sin_ref[0] = jnp.sin(ang)

    spec = pl.BlockSpec((1, ts, width), lambda b, s: (b, s, 0))
    return pl.pallas_call(
        body, name="rope_tables", grid=(bsz, s // ts),
        in_specs=[pl.BlockSpec((1, ts, 1), lambda b, s: (b, s, 0)), pl.BlockSpec((1, width), lambda b, s: (0, 0))],
        out_specs=[spec, spec], out_shape=[jax.ShapeDtypeStruct((bsz, s, width), F32)] * 2,
        compiler_params=_params("arbitrary", "arbitrary"),
    )(positions[:, :, None], inv)


def _attn_probs(q, k, row0):
    scale = (QK_NOPE + QK_ROPE) ** -0.5
    s = _dg(q, k, 1, 1) * scale
    rows = row0 + lax.broadcasted_iota(jnp.int32, s.shape, 0)
    cols = lax.broadcasted_iota(jnp.int32, s.shape, 1)
    s = jnp.where(cols <= rows, s, jnp.finfo(F32).min)
    e = jnp.exp(s - jnp.max(s, -1, keepdims=True))
    return e / jnp.sum(e, -1, keepdims=True), scale


ATTN_PAIR = 2


def attn_fwd(q, k, v, ride=()):
    bsz, h, s, dq = q.shape
    dv = v.shape[-1]
    tq = min(ATTN_TQ, s)
    grid = (bsz, h // ATTN_PAIR, s // tq)
    rd = _Ride(ride)

    def body(*refs):
        (q_ref, k_ref, v_ref), srcs, (o_ref,), outs, sems = rd.split(refs, 3, 1)
        rd.run(srcs, outs, sems, grid)
        for i in range(grid[2]):
            @pl.when(pl.program_id(2) == i)
            def _(i=i):
                kend = (i + 1) * tq
                for e in range(ATTN_PAIR):
                    p, _ = _attn_probs(q_ref[0, e], k_ref[0, e, :kend, :], i * tq)
                    o_ref[0, :, e * dv:(e + 1) * dv] = _dg(p.astype(BF16), v_ref[0, e, :kend, :], 1, 0).astype(o_ref.dtype)

    res = pl.pallas_call(
        body, name="attn_fwd", grid=grid,
        in_specs=[pl.BlockSpec((1, ATTN_PAIR, tq, dq), lambda b, h, i: (b, h, i, 0)),
                  pl.BlockSpec((1, ATTN_PAIR, s, dq), lambda b, h, i: (b, h, 0, 0)),
                  pl.BlockSpec((1, ATTN_PAIR, s, dv), lambda b, h, i: (b, h, 0, 0))] + rd.in_specs,
        out_specs=[pl.BlockSpec((1, tq, ATTN_PAIR * dv), lambda b, h, i: (b, i, h))] + rd.out_specs,
        out_shape=[jax.ShapeDtypeStruct((bsz, s, h * dv), BF16)] + rd.out_shape, scratch_shapes=rd.scratch,
        compiler_params=_params("arbitrary", "arbitrary", "arbitrary"),
    )(q, k, v, *rd.srcs)
    return res[0], res[1:]


def attn_bwd(q, k, v, do, ride=()):
    bsz, h, s, dq = q.shape
    dv = v.shape[-1]
    tq = min(ATTN_TQ, s)
    grid = (bsz, h // ATTN_PAIR, s // tq)
    rd = _Ride(ride)

    def body(*refs):
        (q_ref, k_ref, v_ref, do_ref), srcs, (dq_ref, dk_ref, dv_ref), outs, sems = rd.split(refs, 4, 3)
        rd.run(srcs, outs, sems, grid)

        @pl.when(pl.program_id(2) == 0)
        def _():
            dk_ref[...] = jnp.zeros_like(dk_ref)
            dv_ref[...] = jnp.zeros_like(dv_ref)

        for i in range(grid[2]):
            @pl.when(pl.program_id(2) == i)
            def _(i=i):
                kend = (i + 1) * tq
                for e in range(ATTN_PAIR):
                    qv, kv, vv = q_ref[0, e], k_ref[0, e, :kend, :], v_ref[0, e, :kend, :]
                    p, scale = _attn_probs(qv, kv, i * tq)
                    do16 = do_ref[0, :, e * dv:(e + 1) * dv].astype(BF16)
                    dv_ref[0, e, :kend, :] += _dg(p.astype(BF16), do16, 0, 0)
                    dp = _dg(do16, vv, 1, 1)
                    ds = (p * (dp - jnp.sum(dp * p, -1, keepdims=True)) * scale).astype(BF16)
                    dq_ref[0, e] = _dg(ds, kv, 1, 0)
                    dk_ref[0, e, :kend, :] += _dg(ds, qv, 0, 0)

    res = pl.pallas_call(
        body, name="attn_bwd", grid=grid,
        in_specs=[pl.BlockSpec((1, ATTN_PAIR, tq, dq), lambda b, h, i: (b, h, i, 0)),
                  pl.BlockSpec((1, ATTN_PAIR, s, dq), lambda b, h, i: (b, h, 0, 0)),
                  pl.BlockSpec((1, ATTN_PAIR, s, dv), lambda b, h, i: (b, h, 0, 0)),
                  pl.BlockSpec((1, tq, ATTN_PAIR * dv), lambda b, h, i: (b, i, h))] + rd.in_specs,
        out_specs=[pl.BlockSpec((1, ATTN_PAIR, tq, dq), lambda b, h, i: (b, h, i, 0)),
                   pl.BlockSpec((1, ATTN_PAIR, s, dq), lambda b, h, i: (b, h, 0, 0)),
                   pl.BlockSpec((1, ATTN_PAIR, s, dv), lambda b, h, i: (b, h, 0, 0))] + rd.out_specs,
        out_shape=[jax.ShapeDtypeStruct((bsz, h, s, dq), F32), jax.ShapeDtypeStruct((bsz, h, s, dq), F32),
                   jax.ShapeDtypeStruct((bsz, h, s, dv), F32)] + rd.out_shape, scratch_shapes=rd.scratch,
        compiler_params=_params("arbitrary", "arbitrary", "arbitrary"),
    )(q, k, v, do, *rd.srcs)
    return res[0], res[1], res[2], res[3:]


def mla_heads(q, kv, krt, cos, sin):
    bsz, s, _ = q.shape
    nh, n, r2, vd = MLA_HEADS, QK_NOPE, QK_ROPE // 2, V_HEAD
    ts = min(ROW_TILE, s)

    def body(q_ref, kv_ref, kr_ref, cos_ref, sin_ref, qh_ref, kh_ref, vh_ref):
        cos, sin = cos_ref[0], sin_ref[0]
        qv, kvv, kr = q_ref[0], kv_ref[0], kr_ref[0].astype(F32)
        q1, q2 = qv[:, nh * n:nh * (n + r2)], qv[:, nh * (n + r2):]
        qr = [q1 * cos - q2 * sin, q1 * sin + q2 * cos]
        for h in range(nh):
            qh_ref[0, h, :, :n] = qv[:, h * n:(h + 1) * n].astype(BF16)
            kh_ref[0, h, :, :n] = kvv[:, h * n:(h + 1) * n].astype(BF16)
            vh_ref[0, h] = kvv[:, nh * n + h * vd:nh * n + (h + 1) * vd].astype(BF16)
            for j in range(2):
                qh_ref[0, h, :, n + j * r2:n + (j + 1) * r2] = qr[j][:, h * r2:(h + 1) * r2].astype(BF16)
                kh_ref[0, h, :, n + j * r2:n + (j + 1) * r2] = kr[:, (j * nh + h) * r2:(j * nh + h + 1) * r2].astype(BF16)

    def row(a):
        return pl.BlockSpec((1, ts, a.shape[-1]), lambda b, i: (b, i, 0))

    def heads(w):
        return pl.BlockSpec((1, nh, ts, w), lambda b, i: (b, 0, i, 0))

    return pl.pallas_call(
        body, name="mla_heads", grid=(bsz, s // ts), in_specs=[row(q), row(kv), row(krt), row(cos), row(sin)],
        out_specs=[heads(n + 2 * r2), heads(n + 2 * r2), heads(vd)],
        out_shape=[jax.ShapeDtypeStruct((bsz, nh, s, n + 2 * r2), BF16)] * 2 + [jax.ShapeDtypeStruct((bsz, nh, s, vd), BF16)],
        compiler_params=_params("arbitrary", "arbitrary"),
    )(q, kv, krt, cos, sin)


def mla_heads_bwd(dqh, dkh, dvh, cos, sin):
    bsz, nh, s, _ = dqh.shape
    n, r2, vd = QK_NOPE, QK_ROPE // 2, V_HEAD
    ts = min(ROW_TILE, s)

    def body(dq_ref, dk_ref, dv_ref, cos_ref, sin_ref, oq_ref, okv_ref, okr_ref, tq_acc, tkv_acc, rot):
        for h in range(nh):
            tq_acc[:, h * n:(h + 1) * n] = dq_ref[0, h, :, :n]
            tkv_acc[:, h * n:(h + 1) * n] = dk_ref[0, h, :, :n]
            tkv_acc[:, nh * n + h * vd:nh * n + (h + 1) * vd] = dv_ref[0, h]
            for j in range(2):
                rot[j, :, h * r2:(h + 1) * r2] = dq_ref[0, h, :, n + j * r2:n + (j + 1) * r2]
                okr_ref[0, :, (j * nh + h) * r2:(j * nh + h + 1) * r2] = dk_ref[0, h, :, n + j * r2:n + (j + 1) * r2]
        cos, sin = cos_ref[0], sin_ref[0]
        d1, d2 = rot[0], rot[1]
        oq_ref[0, :, :nh * n] = tq_acc[...].astype(BF16)
        oq_ref[0, :, nh * n:nh * (n + r2)] = (d1 * cos + d2 * sin).astype(BF16)
        oq_ref[0, :, nh * (n + r2):] = (d2 * cos - d1 * sin).astype(BF16)
        okv_ref[0] = tkv_acc[...].astype(BF16)

    def row(w):
        return pl.BlockSpec((1, ts, w), lambda b, i: (b, i, 0))

    def heads(w):
        return pl.BlockSpec((1, nh, ts, w), lambda b, i: (b, 0, i, 0))

    return pl.pallas_call(
        body, name="mla_heads_bwd", grid=(bsz, s // ts),
        in_specs=[heads(n + 2 * r2), heads(n + 2 * r2), heads(vd), row(nh * r2), row(nh * r2)],
        out_specs=[row(nh * (n + 2 * r2)), row(nh * (n + vd)), row(2 * nh * r2)],
        out_shape=[jax.ShapeDtypeStruct((bsz, s, nh * (n + 2 * r2)), BF16), jax.ShapeDtypeStruct((bsz, s, nh * (n + vd)), BF16),
                   jax.ShapeDtypeStruct((bsz, s, 2 * nh * r2), F32)],
        scratch_shapes=[pltpu.VMEM((ts, nh * n), F32), pltpu.VMEM((ts, nh * (n + vd)), F32), pltpu.VMEM((2, ts, nh * r2), F32)],
        compiler_params=_params("arbitrary", "arbitrary"),
    )(dqh, dkh, dvh, cos, sin)


def _hgrn_tables(c):
    levels = c.bit_length() - 1
    assert 1 << levels == c
    r = np.arange(c)
    prefix, sign, mask = [r[:, None] >= r[None, :]], [], []
    for l in range(levels):
        ref = ((r >> (l + 1)) << (l + 1)) + (1 << l) - 1
        lower = ((r >> l) & 1) == 1
        prefix.append(r[None, :] <= ref[:, None])
        sign.append(np.broadcast_to(np.where(lower, 1.0, -1.0)[:, None], (c, LANES)))
        mask.append((((r[:, None] ^ r[None, :]) >> l) == 1) & lower[:, None])
    return (jnp.asarray(np.concatenate(prefix, 0), BF16), jnp.asarray(np.stack(sign), F32),
            jnp.asarray(np.stack(mask), F32))


def _const_specs(tables):
    return [pl.BlockSpec(t.shape, lambda b, h, i, nd=t.ndim: (0,) * nd) for t in tables]


def _split3(x):
    hi = x.astype(BF16)
    rest = x - hi.astype(F32)
    mid = rest.astype(BF16)
    return hi, mid, (rest - mid.astype(F32)).astype(BF16)


@functools.partial(jax.custom_vjp, nondiff_argnums=(2,))
def prefix_sums(p, g, n):
    k = g.shape[1]
    r = _dg(p, jnp.concatenate(_split3(g), axis=1), 1, 0)
    r = r[:, :k] + r[:, k:2 * k] + r[:, 2 * k:]
    c = r.shape[0] // n
    return tuple(r[i * c:(i + 1) * c] for i in range(n))


def _prefix_fwd(p, g, n):
    return prefix_sums(p, g, n), p


def _prefix_bwd(n, p, ct):
    ct = jnp.concatenate(ct, axis=0)
    k = ct.shape[1]
    r = _dg(p, jnp.concatenate(_split3(ct), axis=1), 0, 0)
    return jnp.zeros_like(p), r[:, :k] + r[:, k:2 * k] + r[:, 2 * k:]


prefix_sums.defvjp(_prefix_fwd, _prefix_bwd)


def _hgrn_chunk(q, g, k, v, st0, prefix, sign, mask):
    levels = len(mask)
    pre = prefix_sums(prefix, g, levels + 1)
    b = pre[0]
    o = bdot(q * jnp.exp(b), st0, 1, 1)
    att = None
    for l in range(levels):
        e = jnp.exp((b - pre[l + 1]) * sign[l])
        a = bdot(q * e, k * e, 1, 1) * mask[l]
        att = a if att is None else att + a
    o = o + bdot(att, v, 1, 0) + jnp.sum(q * k, -1, keepdims=True) * v
    total = jnp.sum(g, 0, keepdims=True)
    st1 = st0 * jnp.exp(total) + bdot(v, k * jnp.exp(total - b), 0, 0)
    return o, st1


def hgrn_fwd(qs, logf, kk, proj, ride=()):
    bsz, s, hk = qs.shape
    kd = HGRN_EXPAND
    c = min(HGRN_CHUNK, s)
    nh, nc = hk // kd, s // c
    hp = math.gcd(nh, HGRN_HEADS_PER_STEP)
    tables = _hgrn_tables(c)
    levels = tables[2].shape[0]
    grid = (bsz, nh // hp, nc)
    rd = _Ride(ride)

    def body(*refs):
        (q_ref, g_ref, k_ref, v_ref, p_ref, sg_ref, mk_ref), srcs, (o_ref, st_ref), outs, scratch = rd.split(refs, 7, 2)
        state = scratch[0]
        rd.run(srcs, outs, scratch, grid)

        @pl.when(pl.program_id(2) == 0)
        def _():
            state[...] = jnp.zeros_like(state)

        prefix, sign, mask = p_ref[...], [sg_ref[l] for l in range(levels)], [mk_ref[l] for l in range(levels)]
        for j in range(hp):
            cols = slice(j * kd, (j + 1) * kd)
            st0 = state[j]
            st_ref[0, j, 0] = st0
            o, st1 = _hgrn_chunk(q_ref[0, :, cols], g_ref[0, :, cols], k_ref[0, :, cols], v_ref[0, :, cols], st0,
                                 prefix, sign, mask)
            o_ref[0, :, cols] = o
            state[j] = st1

    blk = pl.BlockSpec((1, c, hp * kd), lambda b, h, i: (b, i, h))
    res = pl.pallas_call(
        body, name="hgrn_fwd", grid=grid,
        in_specs=[blk, blk, blk, pl.BlockSpec((1, c, hp * kd), lambda b, h, i: (b, i, 2 * (nh // hp) + h))]
        + _const_specs(tables) + rd.in_specs,
        out_specs=[blk, pl.BlockSpec((1, hp, 1, kd, kd), lambda b, h, i: (b, h, i, 0, 0))] + rd.out_specs,
        out_shape=[jax.ShapeDtypeStruct((bsz, s, hk), F32), jax.ShapeDtypeStruct((bsz, nh, nc, kd, kd), F32)] + rd.out_shape,
        scratch_shapes=[pltpu.VMEM((hp, kd, kd), F32)] + rd.scratch,
        compiler_params=_params("arbitrary", "arbitrary", "arbitrary"),
    )(qs, logf, kk, proj, *tables, *rd.srcs)
    return res[0], res[1], res[2:]


def hgrn_bwd(qs, logf, kk, proj, states, do, ride=()):
    bsz, s, hk = qs.shape
    kd = HGRN_EXPAND
    c = min(HGRN_CHUNK, s)
    nh, nc = hk // kd, s // c
    hp = math.gcd(nh, HGRN_HEADS_PER_STEP)
    tables = _hgrn_tables(c)
    levels = tables[2].shape[0]
    grid = (bsz, nh // hp, nc)
    rd = _Ride(ride)

    def body(*refs):
        ((q_ref, g_ref, k_ref, v_ref, st_ref, do_ref, p_ref, sg_ref, mk_ref), srcs, (dq_ref, dg_ref, dk_ref, dv_ref), outs,
         scratch) = rd.split(refs, 9, 4)
        dstate = scratch[0]
        rd.run(srcs, outs, scratch, grid)

        @pl.when(pl.program_id(2) == 0)
        def _():
            dstate[...] = jnp.zeros_like(dstate)

        prefix, sign, mask = p_ref[...], [sg_ref[l] for l in range(levels)], [mk_ref[l] for l in range(levels)]
        for j in range(hp):
            cols = slice(j * kd, (j + 1) * kd)
            _, vjp = jax.vjp(lambda q, g, k, v, st: _hgrn_chunk(q, g, k, v, st, prefix, sign, mask),
                             q_ref[0, :, cols], g_ref[0, :, cols], k_ref[0, :, cols], v_ref[0, :, cols], st_ref[0, j, 0])
            dq, dg, dk, dv, dst = vjp((do_ref[0, :, cols], dstate[j]))
            dq_ref[0, :, cols] = dq
            dg_ref[0, :, cols] = dg
            dk_ref[0, :, cols] = dk
            dv_ref[0, :, cols] = dv
            dstate[j] = dst

    blk = pl.BlockSpec((1, c, hp * kd), lambda b, h, i: (b, nc - 1 - i, h))
    shape = jax.ShapeDtypeStruct((bsz, s, hk), F32)
    res = pl.pallas_call(
        body, name="hgrn_bwd", grid=grid,
        in_specs=[blk, blk, blk, pl.BlockSpec((1, c, hp * kd), lambda b, h, i: (b, nc - 1 - i, 2 * (nh // hp) + h)),
                  pl.BlockSpec((1, hp, 1, kd, kd), lambda b, h, i: (b, h, nc - 1 - i, 0, 0)), blk]
        + _const_specs(tables) + rd.in_specs,
        out_specs=[blk] * 4 + rd.out_specs, out_shape=[shape] * 4 + rd.out_shape,
        scratch_shapes=[pltpu.VMEM((hp, kd, kd), F32)] + rd.scratch,
        compiler_params=_params("arbitrary", "arbitrary", "arbitrary"),
    )(qs, logf, kk, proj, states, do, *tables, *rd.srcs)
    return res[0], res[1], res[2], res[3], res[4:]


def cast_bf16(name, w):
    blk = pl.BlockSpec((1,) + w.shape[1:], lambda l: (l, 0, 0))

    def body(w_ref, o_ref):
        o_ref[...] = w_ref[...].astype(BF16)

    return pl.pallas_call(body, name=name, grid=(w.shape[0],), in_specs=[blk], out_specs=blk,
                          out_shape=jax.ShapeDtypeStruct(w.shape, BF16), compiler_params=_params("arbitrary"))(w)


def _lower_bounds(rows):
    m = functools.reduce(jnp.maximum, rows)
    e = [jnp.exp(r - m) for r in rows]
    z = functools.reduce(lambda a, b: a + b, e)
    soft = [x / z for x in e]
    out, run = [], jnp.zeros_like(rows[0])
    for sft in soft:
        run = run + sft
        out.append(run - soft[0])
    return out


def lower_bounds(lb):
    n = lb.shape[0]

    def body(lb_ref, o_ref):
        for i, r in enumerate(_lower_bounds([lb_ref[i:i + 1, :] for i in range(n)])):
            o_ref[i:i + 1, :] = r

    return pl.pallas_call(body, name="lower_bounds", out_shape=jax.ShapeDtypeStruct(lb.shape, F32),
                          compiler_params=_params())(lb)


def ada_fwd(c_all, ada_w, ada_b):
    nl, ns, d, cols = ada_w.shape
    n_ex = c_all.shape[0]

    def body(c_ref, w_ref, b_ref, o_ref):
        a = jax.nn.silu(c_ref[...]).astype(BF16)
        o_ref[0] = _dg(a, w_ref[0].astype(BF16), 1, 0) + b_ref[0]

    return pl.pallas_call(
        body, name="ada_fwd", grid=(nl * ns,),
        in_specs=[pl.BlockSpec((n_ex, d), lambda i: (0, 0)), pl.BlockSpec((1, d, cols), lambda i: (i, 0, 0)),
                  pl.BlockSpec((1, 1, cols), lambda i: (i, 0, 0))],
        out_specs=pl.BlockSpec((1, n_ex, cols), lambda i: (i, 0, 0)),
        out_shape=jax.ShapeDtypeStruct((nl * ns, n_ex, cols), F32), compiler_params=_params("arbitrary"),
    )(c_all, ada_w.reshape(nl * ns, d, cols), ada_b.reshape(nl * ns, 1, cols))


def ada_bwd(c_all, dmod):
    n, n_ex, cols = dmod.shape
    d = c_all.shape[1]

    def body(c_ref, g_ref, dw_ref, db_ref):
        a = jax.nn.silu(c_ref[...]).astype(BF16)
        g = g_ref[0]
        dw_ref[0] = _dg(a, g.astype(BF16), 0, 0)
        db_ref[0] = jnp.sum(g, 0, keepdims=True)

    return pl.pallas_call(
        body, name="ada_bwd", grid=(n,),
        in_specs=[pl.BlockSpec((n_ex, d), lambda i: (0, 0)), pl.BlockSpec((1, n_ex, cols), lambda i: (i, 0, 0))],
        out_specs=[pl.BlockSpec((1, d, cols), lambda i: (i, 0, 0)), pl.BlockSpec((1, 1, cols), lambda i: (i, 0, 0))],
        out_shape=[jax.ShapeDtypeStruct((n, d, cols), F32), jax.ShapeDtypeStruct((n, 1, cols), F32)],
        compiler_params=_params("arbitrary"),
    )(c_all, dmod)


def _adam_math(g, w, m, v):
    m = ADAM_B1 * m + (1.0 - ADAM_B1) * g
    v = ADAM_B2 * v + (1.0 - ADAM_B2) * jnp.square(g)
    m_hat = m / (1.0 - ADAM_B1 ** ADAM_STEP)
    v_hat = v / (1.0 - ADAM_B2 ** ADAM_STEP)
    delta = -ADAM_LR * (m_hat / (jnp.sqrt(v_hat) + ADAM_EPS) + ADAM_WD * w)
    return delta, m, v


def adam(name, gstack, w, m, v):
    shape = w.shape
    n, cols = gstack.shape[0], shape[-1]
    rows = math.prod(shape[:-1])
    tr = _pick_rows(rows, max(8, (2 * 1024 * 1024) // (4 * cols * n)))

    def body(g_ref, w_ref, m_ref, v_ref, go_ref, d_ref, mo_ref, vo_ref):
        g = g_ref[0].astype(F32)
        for i in range(1, n):
            g = g + g_ref[i].astype(F32)
        delta, m1, v1 = _adam_math(g, w_ref[...], m_ref[...], v_ref[...])
        go_ref[...] = g
        d_ref[...] = delta
        mo_ref[...] = m1
        vo_ref[...] = v1

    blk = pl.BlockSpec((tr, cols), lambda i: (i, 0))
    out = pl.pallas_call(
        body, name=name, grid=(rows // tr,),
        in_specs=[pl.BlockSpec((n, tr, cols), lambda i: (0, i, 0)), blk, blk, blk],
        out_specs=[blk] * 4, out_shape=[jax.ShapeDtypeStruct((rows, cols), F32)] * 4,
        compiler_params=_params("arbitrary"),
    )(gstack.reshape(n, rows, cols), w.reshape(rows, cols), m.reshape(rows, cols), v.reshape(rows, cols))
    return [o.reshape(shape) for o in out]


def adam_layers(name, gs, w, m, v):
    shape = w.shape
    nl, n, cols = len(gs), gs[0].shape[0], shape[-1]
    rows = math.prod(shape[1:-1])
    tr = _pick_rows(rows, max(16, (2 * 1024 * 1024) // (4 * cols * n)))
    nt = rows // tr

    def body(*refs):
        g_refs, (w_ref, m_ref, v_ref, go_ref, d_ref, mo_ref, vo_ref) = refs[:nl], refs[nl:]
        for j in range(nl):
            @pl.when(pl.program_id(0) == j)
            def _(j=j):
                g = g_refs[j][0].astype(F32)
                for i in range(1, n):
                    g = g + g_refs[j][i].astype(F32)
                delta, m1, v1 = _adam_math(g, w_ref[...], m_ref[...], v_ref[...])
                go_ref[...] = g
                d_ref[...] = delta
                mo_ref[...] = m1
                vo_ref[...] = v1

    def g_spec(j):
        return pl.BlockSpec((n, tr, cols), lambda l, i: (0, jnp.where(l == j, i, jnp.where(l < j, 0, nt - 1)), 0))

    blk = pl.BlockSpec((tr, cols), lambda l, i: (l * nt + i, 0))
    out = pl.pallas_call(
        body, name=name, grid=(nl, nt),
        in_specs=[g_spec(j) for j in range(nl)] + [blk, blk, blk],
        out_specs=[blk] * 4, out_shape=[jax.ShapeDtypeStruct((nl * rows, cols), F32)] * 4,
        compiler_params=_params("arbitrary", "arbitrary"),
    )(*[g.reshape(n, rows, cols) for g in gs], w.reshape(nl * rows, cols), m.reshape(nl * rows, cols), v.reshape(nl * rows, cols))
    return [o.reshape(shape) for o in out]


def adam_lb(gstack, lb, m, v):
    n, nl = gstack.shape[0], lb.shape[0]

    def body(g_ref, w_ref, m_ref, v_ref, go_ref, d_ref, mo_ref, vo_ref):
        rows = [w_ref[i:i + 1, :] for i in range(nl)]
        ct = []
        for i in range(nl):
            g = g_ref[0, i:i + 1, :]
            for j in range(1, n):
                g = g + g_ref[j, i:i + 1, :]
            ct.append(g)
        _, vjp = jax.vjp(_lower_bounds, rows)
        (grads,) = vjp(ct)
        for i in range(nl):
            delta, m1, v1 = _adam_math(grads[i], rows[i], m_ref[i:i + 1, :], v_ref[i:i + 1, :])
            go_ref[i:i + 1, :] = grads[i]
            d_ref[i:i + 1, :] = delta
            mo_ref[i:i + 1, :] = m1
            vo_ref[i:i + 1, :] = v1

    return pl.pallas_call(body, name="adam_hgrn_lb", out_shape=[jax.ShapeDtypeStruct(lb.shape, F32)] * 4,
                          compiler_params=_params())(gstack, lb, m, v)


class _Ride:
    def __init__(self, items):
        self.items = list(items)
        n = len(self.items)
        self.srcs = [src for src, _ in self.items]
        self.in_specs = [pl.BlockSpec(memory_space=pl.ANY)] * n
        self.out_specs = [pl.BlockSpec(memory_space=pl.ANY)] * n
        self.out_shape = [jax.ShapeDtypeStruct(((N_DEV,) + s.shape) if mode == "gather" else s.shape, s.dtype)
                          for s, mode in self.items]
        self.scratch = [pltpu.SemaphoreType.DMA((n, N_DEV - 1)), pltpu.SemaphoreType.DMA((n, N_DEV - 1)),
                        pltpu.SemaphoreType.DMA((n,))] if n else []

    def split(self, refs, n_in, n_out):
        n = len(self.items)
        a, b = n_in + n, n_in + 2 * n + n_out
        return refs[:n_in], refs[n_in:a], refs[a:a + n_out], refs[a + n_out:b], refs[b:]

    def _copies(self, srcs, outs, sems):
        send_sems, recv_sems, local_sems = sems
        x, y, c = lax.axis_index("x"), lax.axis_index("y"), lax.axis_index("c")
        me = 4 * x + 2 * y + c
        copies = []
        for i, (_, mode) in enumerate(self.items):
            mine = srcs[i] if mode == "gather" else srcs[i].at[me]
            copies.append(pltpu.make_async_copy(mine, outs[i].at[me], local_sems.at[i]))
            for p in range(1, N_DEV):
                px = 1 - x if p & 4 else x
                py = 1 - y if p & 2 else y
                pc = 1 - c if p & 1 else c
                part = srcs[i] if mode == "gather" else srcs[i].at[4 * px + 2 * py + pc]
                copies.append(pltpu.make_async_remote_copy(
                    src_ref=part, dst_ref=outs[i].at[me], send_sem=send_sems.at[i, p - 1], recv_sem=recv_sems.at[i, p - 1],
                    device_id=(px, py, pc), device_id_type=pl.DeviceIdType.MESH))
        return copies

    def run(self, srcs, outs, scratch, grid=()):
        if not self.items:
            return
        sems = scratch[len(scratch) - 3:]
        if not grid:
            copies = self._copies(srcs, outs, sems)
            for cp in copies:
                cp.start()
            for cp in copies:
                cp.wait()
            return
        ids = [pl.program_id(a) for a in range(len(grid))]
        first = functools.reduce(lambda a, b: a & b, [i == 0 for i in ids])
        last = functools.reduce(lambda a, b: a & b, [i == g - 1 for i, g in zip(ids, grid)])

        @pl.when(first)
        def _():
            for cp in self._copies(srcs, outs, sems):
                cp.start()

        @pl.when(last)
        def _():
            for cp in self._copies(srcs, outs, sems):
                cp.wait()


def exchange(name, items):
    rd = _Ride(items)

    def body(*refs):
        _, srcs, _, outs, scratch = rd.split(refs, 0, 0)
        rd.run(srcs, outs, scratch)

    return pl.pallas_call(body, name=name, in_specs=rd.in_specs, out_specs=rd.out_specs, out_shape=rd.out_shape,
                          scratch_shapes=rd.scratch)(*rd.srcs)


def _from_gather(name, g):
    if name in COL_SHARDED:
        _, k, n = g.shape
        return g.transpose(1, 0, 2).reshape(k, N_DEV * n)
    return g.reshape(-1, g.shape[-1])


def _to_slabs(name, w):
    if isinstance(w, tuple):
        per = N_DEV // len(w)
        return jnp.concatenate([p.reshape(p.shape[0], per, p.shape[1] // per).transpose(1, 0, 2) for p in w], axis=0)
    k, n = w.shape
    if name in COL_SHARDED:
        return w.reshape(k, N_DEV, n // N_DEV).transpose(1, 0, 2)
    return w.reshape(N_DEV, k // N_DEV, n)


def _w_in_internal(w):
    return jnp.pad(w, ((0, 0), (0, LANES - QK_ROPE)))


def _qb_internal(w, inverse=False):
    h, n, r2 = MLA_HEADS, QK_NOPE, QK_ROPE // 2
    lead = w.shape[:-1]
    if not inverse:
        w = w.reshape(lead + (h, n + 2 * r2))
        parts = [w[..., :n], w[..., n:n + r2], w[..., n + r2:]]
        return jnp.concatenate([p.reshape(lead + (-1,)) for p in parts], axis=-1)
    parts = [w[..., :h * n].reshape(lead + (h, n)), w[..., h * n:h * (n + r2)].reshape(lead + (h, r2)),
             w[..., h * (n + r2):].reshape(lead + (h, r2))]
    return jnp.concatenate(parts, axis=-1).reshape(lead + (-1,))


def _kvb_internal(w, inverse=False):
    h, n, vd = MLA_HEADS, QK_NOPE, V_HEAD
    lead = w.shape[:-1]
    if not inverse:
        w = w.reshape(lead + (h, n + vd))
        return jnp.concatenate([w[..., :n].reshape(lead + (-1,)), w[..., n:].reshape(lead + (-1,))], axis=-1)
    parts = [w[..., :h * n].reshape(lead + (h, n)), w[..., h * n:].reshape(lead + (h, vd))]
    return jnp.concatenate(parts, axis=-1).reshape(lead + (-1,))


def _mla_forward(h, w, tabs, ride=()):
    cos, sin = tabs
    r2 = MLA_HEADS * (QK_ROPE // 2)
    proj = mm3("mla_proj", h, w['w_in'])
    qn, kvn, krt = rowwise(
        "mla_mid", lambda rv, ev, gv: (f_mla_mid(rv, ev, gv), []),
        [_view(proj, 0, Q_LORA), _view(proj, Q_LORA, KV_LORA), _view(proj, Q_LORA + KV_LORA, LANES), _full(cos), _full(sin)],
        [], [w['q_norm'], w['kv_norm']], [(BF16, [Q_LORA]), (BF16, [KV_LORA]), (BF16, [r2, r2])], ts=ROW_TILE)
    q = mm3("mla_q", qn, w['w_qb'])
    kv = mm3("mla_kv", kvn, w['w_kvb'])
    qh, kh, vh = mla_heads(q, kv, krt, cos, sin)
    o, got = attn_fwd(qh, kh, vh, ride)
    y = mm3("mla_out", o, w['w_o'])
    return y, dict(h=h, proj=proj, qn=qn, kvn=kvn, qh=qh, kh=kh, vh=vh, o=o), got


def _mla_backward(dy, sv, w, tabs, ride=()):
    cos, sin = tabs
    r2 = MLA_HEADS * (QK_ROPE // 2)
    g = {}
    g['w_o'] = wgrad("mla_out_dw", sv['o'], dy)
    do = mm3("mla_out_dx", dy, w['w_o'], tb=True)
    dqh, dkh, dvh, got = attn_bwd(sv['qh'], sv['kh'], sv['vh'], do, ride)
    dq, dkv, dkrt = mla_heads_bwd(dqh, dkh, dvh, cos, sin)
    g['w_qb'] = wgrad("mla_q_dw", sv['qn'], dq)
    g['w_kvb'] = wgrad("mla_kv_dw", sv['kvn'], dkv)
    dqn = mm3("mla_q_dx", dq, w['w_qb'], tb=True)
    dkvn = mm3("mla_kv_dx", dkv, w['w_kvb'], tb=True)
    proj = sv['proj']
    (dproj,), _, (g['q_norm'], g['kv_norm']) = rowwise_bwd(
        "mla_mid_bwd", f_mla_mid,
        [_view(proj, 0, Q_LORA), _view(proj, Q_LORA, KV_LORA), _view(proj, Q_LORA + KV_LORA, LANES), _full(cos), _full(sin)],
        [], [w['q_norm'], w['kv_norm']], [(dqn, [Q_LORA]), (dkvn, [KV_LORA]), (dkrt, [r2, r2])],
        [(BF16, [0, 1, 2])], ts=ROW_TILE, n_diff=3)
    g['w_in'] = wgrad("mla_proj_dw", sv['h'], dproj)
    dh = mm3("mla_proj_dx", dproj, w['w_in'], tb=True)
    return dh, g, got


def _hgrn_views(proj):
    d = proj.shape[-1] // 4
    return [_view(proj, i * d, d) for i in range(4)]


def _head_views(a, col0, nh):
    return [_view(a, col0 + i * HGRN_EXPAND, HGRN_EXPAND) for i in range(nh)]


def _hgrn_forward(h, w, ride=()):
    proj = mm3("hgrn_proj", h, w['w_in'])
    d = proj.shape[-1] // 4
    nh = d // HGRN_EXPAND
    vq, vf, _, _ = _hgrn_views(proj)
    qs, logf, kk = rowwise("hgrn_pre", lambda rv, ev, gv: (f_hg_pre(rv, ev, gv), []), [vq, vf], [], [w['lb']],
                           [(F32, [d]), (F32, [d]), (F32, [d])], ts=ROW_TILE)
    o, states, got = hgrn_fwd(qs, logf, kk, proj, ride)
    (z,) = rowwise("hgrn_post", lambda rv, ev, gv: (f_hg_post(rv, ev, gv), []),
                   _head_views(o, 0, nh) + _head_views(proj, 3 * d, nh), [], [w['g_norm']],
                   [(BF16, [HGRN_EXPAND] * nh)], ts=ROW_TILE)
    y = mm3("hgrn_out", z, w['w_o'])
    return y, dict(h=h, proj=proj, qs=qs, logf=logf, kk=kk, o=o, states=states, z=z), got


def _hgrn_backward(dy, sv, w, ride=()):
    g = {}
    proj, o = sv['proj'], sv['o']
    d = proj.shape[-1] // 4
    nh = d // HGRN_EXPAND
    g['w_o'] = wgrad("hgrn_out_dw", sv['z'], dy)
    dz = mm3("hgrn_out_dx", dy, w['w_o'], tb=True)
    (do, dgate), _, (g['g_norm'],) = rowwise_bwd(
        "hgrn_post_bwd", f_hg_post, _head_views(o, 0, nh) + _head_views(proj, 3 * d, nh), [], [w['g_norm']],
        [(dz, [HGRN_EXPAND] * nh)], [(F32, list(range(nh))), (F32, list(range(nh, 2 * nh)))], ts=ROW_TILE, n_diff=2 * nh)
    dqs, dlogf, dkk, dv, got = hgrn_bwd(sv['qs'], sv['logf'], sv['kk'], proj, sv['states'], do, ride)
    (dproj,), _, (g['lb'],) = rowwise_bwd(
        "hgrn_pre_bwd", f_hg_pre_with_iv, _hgrn_views(proj), [], [w['lb']],
        [(dqs, [d]), (dlogf, [d]), (dkk, [d]), (dv, [d]), (dgate, [d])], [(BF16, [0, 1, 2, 3])], ts=ROW_TILE // 2, n_diff=4)
    g['w_in'] = wgrad("hgrn_proj_dw", sv['h'], dproj)
    dh = mm3("hgrn_proj_dx", dproj, w['w_in'], tb=True)
    return dh, g, got


def _ffn_forward(h, w):
    ug, uu, a = ffn_in_act(h, w['w_in'])
    y = mm3("ffn_out", a, w['w_out'])
    return y, dict(h=h, ug=ug, uu=uu, a=a)


def _ffn_backward(dy, sv, w):
    g = {}
    g['w_out'] = wgrad("ffn_out_dw", sv['a'], dy)
    dug, duu = ffn_out_dx_act(dy, w['w_out'], sv['ug'], sv['uu'])
    g['w_in'] = (wgrad("ffn_in_dw", sv['h'], dug), wgrad("ffn_in_dw", sv['h'], duu))
    dh = mm3("ffn_in_dx", [dug, duu], w['w_in'], tb=True)
    return dh, g


def kernel(x, c, positions, mla_w_in, mla_q_norm, mla_w_qb, mla_kv_norm, mla_w_kvb, mla_w_o, hgrn_lb, hgrn_w_in, hgrn_g_norm, hgrn_w_o, ffn_w_in, ffn_w_out, ada_w, ada_b, ln_g, ln_b, loss_target, m_mla_w_in, m_mla_q_norm, m_mla_w_qb, m_mla_kv_norm, m_mla_w_kvb, m_mla_w_o, m_hgrn_lb, m_hgrn_w_in, m_hgrn_g_norm, m_hgrn_w_o, m_ffn_w_in, m_ffn_w_out, m_ada_w, m_ada_b, m_ln_g, m_ln_b, v_mla_w_in, v_mla_q_norm, v_mla_w_qb, v_mla_kv_norm, v_mla_w_kvb, v_mla_w_o, v_hgrn_lb, v_hgrn_w_in, v_hgrn_g_norm, v_hgrn_w_o, v_ffn_w_in, v_ffn_w_out, v_ada_w, v_ada_b, v_ln_g, v_ln_b):
    W = dict(zip(WEIGHTS, (mla_w_in, mla_q_norm, mla_w_qb, mla_kv_norm, mla_w_kvb, mla_w_o, hgrn_lb, hgrn_w_in, hgrn_g_norm,
                           hgrn_w_o, ffn_w_in, ffn_w_out, ada_w, ada_b, ln_g, ln_b)))
    M1 = dict(zip(WEIGHTS, (m_mla_w_in, m_mla_q_norm, m_mla_w_qb, m_mla_kv_norm, m_mla_w_kvb, m_mla_w_o, m_hgrn_lb, m_hgrn_w_in,
                            m_hgrn_g_norm, m_hgrn_w_o, m_ffn_w_in, m_ffn_w_out, m_ada_w, m_ada_b, m_ln_g, m_ln_b)))
    M2 = dict(zip(WEIGHTS, (v_mla_w_in, v_mla_q_norm, v_mla_w_qb, v_mla_kv_norm, v_mla_w_kvb, v_mla_w_o, v_hgrn_lb, v_hgrn_w_in,
                            v_hgrn_g_norm, v_hgrn_w_o, v_ffn_w_in, v_ffn_w_out, v_ada_w, v_ada_b, v_ln_g, v_ln_b)))
    bsz, seq, d = x.shape
    depth, n_mla, n_hgrn = ffn_w_in.shape[0], mla_w_in.shape[0], hgrn_w_in.shape[0]
    n_sub = 2 * depth

    big = COL_SHARDED + ROW_SHARDED
    wb = {n: cast_bf16("cast_" + n, W[n]) for n in big}

    def mixer_names(layer):
        mixer = ['mla_w_in', 'mla_w_qb', 'mla_w_kvb', 'mla_w_o'] if layer % 2 == 0 else ['hgrn_w_in', 'hgrn_w_o']
        return [(n, layer // 2) for n in mixer]

    def carried(layer):
        return [('ffn_w_in', layer), ('ffn_w_out', layer)] + (mixer_names(layer + 1) if layer + 1 < depth else [])

    def weight_items(names):
        return [(wb[n][j], "gather") for n, j in names]

    G = {}
    internal = {'mla_w_in': _w_in_internal, 'mla_w_qb': _qb_internal, 'mla_w_kvb': _kvb_internal}

    def take_weights(names, got):
        for (n, j), a in zip(names, got):
            G[n, j] = internal.get(n, lambda w: w)(_from_gather(n, a))

    lower_shard = lower_bounds(hgrn_lb)
    got = exchange("gather_first", [(lower_shard, "gather"), (ln_g, "gather"), (ln_b, "gather"), (c, "gather")]
                   + weight_items(mixer_names(0)))
    lower_all = got[0].transpose(1, 0, 2).reshape(n_hgrn, -1)
    ln_g_all = got[1].transpose(1, 2, 0, 3).reshape(depth, 2, d)
    ln_b_all = got[2].transpose(1, 2, 0, 3).reshape(depth, 2, d)
    c_all = got[3].reshape(N_DEV * bsz, d)
    take_weights(mixer_names(0), got[4:])

    cols = ada_w.shape[-1]
    mod_loc = ada_fwd(c_all, ada_w, ada_b)
    (mod_got,) = exchange("scatter_mod", [(mod_loc.reshape(n_sub, N_DEV, bsz, cols).transpose(1, 0, 2, 3), "a2a")])
    mod = mod_got.transpose(1, 2, 0, 3).reshape(n_sub, bsz, 1, 3 * d)
    shift = [mod[k, :, :, 0:d] for k in range(n_sub)]
    scale = [mod[k, :, :, d:2 * d] for k in range(n_sub)]
    gate = [mod[k, :, :, 2 * d:] for k in range(n_sub)]
    lng = [ln_g_all[k // 2, k % 2][None, :] for k in range(n_sub)]
    lnb = [ln_b_all[k // 2, k % 2][None, :] for k in range(n_sub)]

    tabs = rope_tables(positions)

    def sub_weights(k):
        layer, j = k // 2, k // 4
        if k % 2:
            return 'ffn', layer, dict(w_in=G['ffn_w_in', layer], w_out=G['ffn_w_out', layer])
        if layer % 2 == 0:
            return 'mla', j, dict(w_in=G['mla_w_in', j], q_norm=mla_q_norm[j][None, :], w_qb=G['mla_w_qb', j],
                                  kv_norm=mla_kv_norm[j][None, :], w_kvb=G['mla_w_kvb', j], w_o=G['mla_w_o', j])
        return 'hgrn', j, dict(w_in=G['hgrn_w_in', j], lb=lower_all[j][None, :], g_norm=hgrn_g_norm[j][None, :],
                               w_o=G['hgrn_w_o', j])

    (h,) = rowwise("mod_first", lambda rv, ev, gv: (f_mod(rv, ev, gv), []), [_full(x)], [scale[0], shift[0]], [],
                   [(BF16, [d])], ts=ROW_TILE)
    xs, ys, saved = [x], [], []
    loss_acc = None
    for k in range(n_sub):
        kind, _, w = sub_weights(k)
        ride = weight_items(carried(k // 2)) if k % 2 == 0 else []
        if kind == 'ffn':
            y, sv = _ffn_forward(h, w)
        elif kind == 'mla':
            y, sv, got = _mla_forward(h, w, tabs, ride)
        else:
            y, sv, got = _hgrn_forward(h, w, ride)
        if ride:
            take_weights(carried(k // 2), got)
        ys.append(y)
        saved.append(sv)
        if k + 1 < n_sub:
            xn, h = rowwise("ln_mod", lambda rv, ev, gv: (f_ln_mod(rv, ev, gv), []), [_full(xs[k]), _full(y)],
                            [gate[k], scale[k + 1], shift[k + 1]], [lng[k], lnb[k]], [(F32, [d]), (BF16, [d])], ts=ROW_TILE)
            xs.append(xn)
        else:
            def loss_rows(rv, ev, gv):
                (row,) = f_ln_loss(rv, ev, gv)
                return [], [jnp.broadcast_to(jnp.sum(row, keepdims=True), (1, LANES))]
            (loss_acc,) = rowwise("ln_loss", loss_rows, [_full(xs[k]), _full(y), _full(loss_target)], [gate[k]], [lng[k], lnb[k]],
                                  [], ts=ROW_TILE, accs=[LANES])
    loss = lax.psum(loss_acc[0, 0], ("x", "y", "c"))

    d_shift, d_scale, d_gate = [None] * n_sub, [None] * n_sub, [None] * n_sub
    d_lng, d_lnb = [None] * n_sub, [None] * n_sub
    part = {n: [None] * W[n].shape[0] for n in ['mla_q_norm', 'mla_kv_norm', 'hgrn_g_norm']}
    recv = {n: [None] * W[n].shape[0] for n in big}
    d_lower = [None] * n_hgrn
    k = n_sub - 1
    (dx, dy), (d_gate[k],), (d_lng[k], d_lnb[k]) = rowwise_bwd(
        "ln_loss_bwd", f_ln_loss, [_full(xs[k]), _full(ys[k]), _full(loss_target)], [gate[k]], [lng[k], lnb[k]], [],
        [(F32, [0]), (BF16, [1])], ts=ROW_TILE, n_diff=2, unit_ct=1)
    grad_x = None
    mine = {}

    def take_grads(names, got):
        for (n, j), a in zip(names, got):
            recv[n][j] = a

    def grad_items(names):
        return [(_to_slabs(n, mine[n, jj]), "a2a") for n, jj in names]

    for k in range(n_sub - 1, -1, -1):
        kind, j, w = sub_weights(k)
        ride = grad_items(carried(k // 2)) if kind != 'ffn' else []
        if kind == 'ffn':
            dh, g = _ffn_backward(dy, saved[k], w)
            new = {('ffn_w_in', j): g['w_in'], ('ffn_w_out', j): g['w_out']}
        elif kind == 'mla':
            dh, g, got = _mla_backward(dy, saved[k], w, tabs, ride)
            new = {('mla_w_in', j): g['w_in'][:, :mla_w_in.shape[-1] * N_DEV], ('mla_w_qb', j): _qb_internal(g['w_qb'], inverse=True),
                   ('mla_w_kvb', j): _kvb_internal(g['w_kvb'], inverse=True), ('mla_w_o', j): g['w_o']}
            part['mla_q_norm'][j], part['mla_kv_norm'][j] = g['q_norm'][0], g['kv_norm'][0]
        else:
            dh, g, got = _hgrn_backward(dy, saved[k], w, ride)
            new = {('hgrn_w_in', j): g['w_in'], ('hgrn_w_o', j): g['w_o']}
            part['hgrn_g_norm'][j] = g['g_norm'][0]
            d_lower[j] = g['lb'][0]
        if kind != 'ffn':
            take_grads(carried(k // 2), got)
        mine.update(new)
        if k:
            (dx, dy), (d_gate[k - 1], d_scale[k], d_shift[k]), (d_lng[k - 1], d_lnb[k - 1]) = rowwise_bwd(
                "ln_mod_bwd", f_ln_mod, [_full(xs[k - 1]), _full(ys[k - 1])], [gate[k - 1], scale[k], shift[k]],
                [lng[k - 1], lnb[k - 1]], [(dx, [d]), (dh, [d])], [(F32, [0]), (BF16, [1])], ts=ROW_TILE, n_diff=2)
        else:
            (grad_x,), (d_scale[0], d_shift[0]), _ = rowwise_bwd(
                "mod_first_bwd", f_mod_with_x, [_full(x)], [scale[0], shift[0]], [], [(dh, [d]), (dx, [d])],
                [(F32, [0])], ts=ROW_TILE, n_diff=1)

    waiting = mixer_names(0)
    slabs = grad_items(waiting)
    slabs.append((jnp.stack(d_lower).reshape(n_hgrn, N_DEV, -1).transpose(1, 0, 2), "a2a"))
    for parts in (d_lng, d_lnb):
        full = jnp.stack([p[0] for p in parts]).reshape(depth, 2, N_DEV, d // N_DEV)
        slabs.append((full.transpose(2, 0, 1, 3), "a2a"))
    dmod = jnp.concatenate([jnp.stack(d_shift), jnp.stack(d_scale), jnp.stack(d_gate)], axis=-1)
    slabs.append((dmod.reshape(n_sub, bsz, N_DEV, cols).transpose(2, 0, 1, 3), "a2a"))
    small = ['mla_q_norm', 'mla_kv_norm', 'hgrn_g_norm']
    slabs += [(jnp.stack(part[n]), "gather") for n in small]
    got = exchange("scatter_last", slabs)
    take_grads(waiting, got)
    stacks = dict(zip(['hgrn_lb', 'ln_g', 'ln_b', 'dmod'] + small, got[len(waiting):]))

    dmod_all = stacks['dmod'].transpose(1, 0, 2, 3).reshape(n_sub, N_DEV * bsz, cols)
    g_ada_w, g_ada_b = ada_bwd(c_all, dmod_all)
    stacks['ada_w'] = g_ada_w.reshape((1,) + ada_w.shape)
    stacks['ada_b'] = g_ada_b.reshape((1,) + ada_b.shape)

    res = {}
    for n in WEIGHTS:
        if n == 'hgrn_lb':
            res[n] = adam_lb(stacks[n], W[n], M1[n], M2[n])
        elif n in big:
            res[n] = adam_layers("adam_" + n, recv[n], W[n], M1[n], M2[n])
        else:
            res[n] = adam("adam_" + n, stacks[n], W[n], M1[n], M2[n])
    return (loss, grad_x, *[res[n][0] for n in WEIGHTS], *[res[n][1] for n in WEIGHTS], *[res[n][2] for n in WEIGHTS],
            *[res[n][3] for n in WEIGHTS])
```

```python
import functools
import math

import numpy as np
import jax
import jax.numpy as jnp
from jax import lax
from jax.experimental import pallas as pl
from jax.experimental.pallas import tpu as pltpu

F32 = jnp.float32
BF16 = jnp.bfloat16

N_DEV = 8
LANES = 128
VMEM_LIMIT = 52 * 1024 * 1024

D_MODEL = 1024
DEPTH = 4
MLA_HEADS = 16
QK_NOPE = 64
QK_ROPE = 32
V_HEAD = 64
Q_LORA = 768
KV_LORA = 256
ROPE_THETA = 10000.0
HGRN_EXPAND = 128
HGRN_CHUNK = 128
HGRN_HEADS_PER_STEP = 4
D_FF = 2816
ALPHA = (2.0 * DEPTH) ** 0.25
LN_EPS = 1e-5
RMS_EPS = 1e-6
ADAM_LR = 0.001
ADAM_B1 = 0.9
ADAM_B2 = 0.999
ADAM_EPS = 1e-08
ADAM_WD = 0.01
ADAM_STEP = 10

ATTN_TQ = 512
ROW_TILE = 256

WEIGHTS = ['mla_w_in', 'mla_q_norm', 'mla_w_qb', 'mla_kv_norm', 'mla_w_kvb', 'mla_w_o', 'hgrn_lb', 'hgrn_w_in',
           'hgrn_g_norm', 'hgrn_w_o', 'ffn_w_in', 'ffn_w_out', 'ada_w', 'ada_b', 'ln_g', 'ln_b']
COL_SHARDED = ['mla_w_in', 'mla_w_qb', 'mla_w_kvb', 'hgrn_w_in', 'ffn_w_in']
ROW_SHARDED = ['mla_w_o', 'hgrn_w_o', 'ffn_w_out']


def _params(*sem):
    if sem:
        return pltpu.CompilerParams(dimension_semantics=sem, vmem_limit_bytes=VMEM_LIMIT)
    return pltpu.CompilerParams(vmem_limit_bytes=VMEM_LIMIT)


def _pick(n, cap):
    best = None
    for t in range(LANES, min(n, cap) + 1, LANES):
        if n % t == 0:
            best = t
    return best or n


def _pick_rows(n, cap):
    best = None
    for t in range(8, min(n, cap) + 1, 8):
        if n % t == 0:
            best = t
    return best or n


def matmul(name, a, b, *, ta=False, tb=False, out_dtype=F32, tm_cap=1024, tn_cap=1536, tk_cap=2048):
    parts = list(a) if isinstance(a, (list, tuple)) else [a]
    n_parts = len(parts)
    assert n_parts == 1 or not ta
    (kp, m) = parts[0].shape if ta else parts[0].shape[::-1]
    (n, k2) = b.shape if tb else b.shape[::-1]
    assert kp * n_parts == k2, (name, parts[0].shape, b.shape)
    tm, tn, tk = _pick(m, tm_cap), _pick(n, tn_cap), _pick(kp, tk_cap)
    nkp = kp // tk
    nk = nkp * n_parts
    dims = (((0 if ta else 1,), (1 if tb else 0,)), ((), ()))

    def prod(a_ref, b_ref):
        return lax.dot_general(a_ref[...].astype(BF16), b_ref[...].astype(BF16), dims, preferred_element_type=F32)

    if nk == 1:
        def body(a_ref, b_ref, o_ref):
            o_ref[...] = prod(a_ref, b_ref).astype(o_ref.dtype)
        scratch = []
    else:
        def body(*refs):
            a_refs, (b_ref, o_ref, acc_ref) = refs[:n_parts], refs[n_parts:]
            k = pl.program_id(2)

            @pl.when(k == 0)
            def _():
                acc_ref[...] = jnp.zeros_like(acc_ref)

            if n_parts == 1:
                acc_ref[...] += prod(a_refs[0], b_ref)
            else:
                for p in range(n_parts):
                    @pl.when((k >= p * nkp) & (k < (p + 1) * nkp))
                    def _(p=p):
                        acc_ref[...] += prod(a_refs[p], b_ref)

            @pl.when(k == nk - 1)
            def _():
                o_ref[...] = acc_ref[...].astype(o_ref.dtype)
        scratch = [pltpu.VMEM((tm, tn), F32)]

    if ta:
        a_specs = [pl.BlockSpec((tk, tm), lambda i, j, k: (k, i))]
    elif n_parts == 1:
        a_specs = [pl.BlockSpec((tm, tk), lambda i, j, k: (i, k))]
    else:
        a_specs = [pl.BlockSpec((tm, tk), lambda i, j, k, p=p: (i, jnp.clip(k - p * nkp, 0, nkp - 1))) for p in range(n_parts)]
    b_spec = pl.BlockSpec((tn, tk), lambda i, j, k: (j, k)) if tb else pl.BlockSpec((tk, tn), lambda i, j, k: (k, j))
    return pl.pallas_call(
        body, name=name, grid=(m // tm, n // tn, nk),
        in_specs=a_specs + [b_spec], out_specs=pl.BlockSpec((tm, tn), lambda i, j, k: (i, j)),
        out_shape=jax.ShapeDtypeStruct((m, n), out_dtype), scratch_shapes=scratch,
        compiler_params=_params("parallel", "parallel", "arbitrary"),
    )(*parts, b)


def mm3(name, a3, w, **kw):
    parts = list(a3) if isinstance(a3, (list, tuple)) else [a3]
    bsz, s = parts[0].shape[:2]
    flat = [p.reshape(bsz * s, p.shape[-1]) for p in parts]
    out = matmul(name, flat if len(flat) > 1 else flat[0], w, **kw)
    return out.reshape(bsz, s, out.shape[-1])


def ffn_in_act(h3, w_in):
    bsz, s, k = h3.shape
    m, dff = bsz * s, w_in.shape[1] // 2
    tm, tn = _pick(m, 512), _pick(dff, 1536)
    nj = dff // tn

    def body(h_ref, wg_ref, wu_ref, ug_ref, uu_ref, a_ref):
        hv = h_ref[...].astype(BF16)
        ug = _dg(hv, wg_ref[...].astype(BF16), 1, 0)
        uu = _dg(hv, wu_ref[...].astype(BF16), 1, 0)
        ug_ref[...] = ug.astype(ug_ref.dtype)
        uu_ref[...] = uu.astype(uu_ref.dtype)
        a_ref[...] = (jax.nn.silu(ug) * uu).astype(a_ref.dtype)

    out = pl.BlockSpec((tm, tn), lambda j, i: (i, j))
    res = pl.pallas_call(
        body, name="ffn_in_act", grid=(nj, m // tm),
        in_specs=[pl.BlockSpec((tm, k), lambda j, i: (i, 0)), pl.BlockSpec((k, tn), lambda j, i: (0, j)),
                  pl.BlockSpec((k, tn), lambda j, i: (0, nj + j))],
        out_specs=[out, out, out],
        out_shape=[jax.ShapeDtypeStruct((m, dff), BF16)] * 3,
        compiler_params=_params("arbitrary", "arbitrary"),
    )(h3.reshape(m, k), w_in, w_in)
    return [r.reshape(bsz, s, dff) for r in res]


def ffn_out_dx_act(dy3, w_out, ug, uu):
    bsz, s, d = dy3.shape
    m, dff = bsz * s, w_out.shape[0]
    tm, tn = _pick(m, 512), _pick(dff, 1536)

    def body(dy_ref, w_ref, ug_ref, uu_ref, dg_ref, du_ref):
        da = _dg(dy_ref[...].astype(BF16), w_ref[...].astype(BF16), 1, 1)
        _, vjp = jax.vjp(lambda gate, up: jax.nn.silu(gate) * up, ug_ref[...].astype(F32), uu_ref[...].astype(F32))
        dg, du = vjp(da)
        dg_ref[...] = dg.astype(dg_ref.dtype)
        du_ref[...] = du.astype(du_ref.dtype)

    blk = pl.BlockSpec((tm, tn), lambda j, i: (i, j))
    res = pl.pallas_call(
        body, name="ffn_out_dx_act", grid=(dff // tn, m // tm),
        in_specs=[pl.BlockSpec((tm, d), lambda j, i: (i, 0)), pl.BlockSpec((tn, d), lambda j, i: (j, 0)), blk, blk],
        out_specs=[blk, blk], out_shape=[jax.ShapeDtypeStruct((m, dff), BF16)] * 2,
        compiler_params=_params("arbitrary", "arbitrary"),
    )(dy3.reshape(m, d), w_out, ug.reshape(m, dff), uu.reshape(m, dff))
    return [r.reshape(bsz, s, dff) for r in res]


def wgrad(name, a3, g3):
    bsz, s, k = a3.shape
    return matmul(name, a3.reshape(bsz * s, k), g3.reshape(bsz * s, g3.shape[-1]), ta=True, out_dtype=BF16)


def _dg(a, b, ca, cb, **kw):
    return lax.dot_general(a, b, (((ca,), (cb,)), ((), ())), preferred_element_type=F32, **kw)


@functools.partial(jax.custom_vjp, nondiff_argnums=(2, 3))
def bdot(a, b, ca, cb):
    return _dg(a.astype(BF16), b.astype(BF16), ca, cb)


def _bdot_fwd(a, b, ca, cb):
    return bdot(a, b, ca, cb), (a, b)


def _bdot_bwd(ca, cb, res, g):
    a, b = res
    a16, b16, g16 = a.astype(BF16), b.astype(BF16), g.astype(BF16)
    if ca == 1:
        da = _dg(g16, b16, 1, 1 if cb == 0 else 0)
    else:
        da = _dg(b16, g16, 1 if cb == 0 else 0, 1)
    if cb == 0:
        db = _dg(a16, g16, 0 if ca == 1 else 1, 0)
    else:
        db = _dg(g16, a16, 0, 0 if ca == 1 else 1)
    return da, db


bdot.defvjp(_bdot_fwd, _bdot_bwd)


def hdot(a, b, ca=1, cb=0):
    return _dg(a, b, ca, cb, precision=lax.Precision.HIGHEST)


def _row_specs(rows, exs, globs, ts):
    specs = [pl.BlockSpec((1, ts, w), lambda b, s, j=j: (b, s, j)) for (_, j, w) in rows]
    specs += [pl.BlockSpec((1, 1, e.shape[-1]), lambda b, s: (b, 0, 0)) for e in exs]
    specs += [pl.BlockSpec((1, g.shape[-1]), lambda b, s: (0, 0)) for g in globs]
    return specs


def _store_pieces(o_ref, pieces, widths):
    off = 0
    for p, w in zip(pieces, widths):
        o_ref[0, :, off:off + w] = p.astype(o_ref.dtype)
        off += w


def _load_pieces(c_ref, widths):
    out, off = [], 0
    for w in widths:
        out.append(c_ref[0, :, off:off + w].astype(F32))
        off += w
    return out


def rowwise(name, f, rows, exs, globs, outs, *, ts, accs=()):
    bsz, s = rows[0][0].shape[:2]
    ts = min(ts, s)
    n_r, n_e, n_g, n_o = len(rows), len(exs), len(globs), len(outs)

    def body(*refs):
        rv = [r[0].astype(F32) for r in refs[:n_r]]
        ev = [e[0] for e in refs[n_r:n_r + n_e]]
        gv = [g[...] for g in refs[n_r + n_e:n_r + n_e + n_g]]
        o_refs = refs[n_r + n_e + n_g:n_r + n_e + n_g + n_o]
        a_refs = refs[n_r + n_e + n_g + n_o:]
        pieces, sums = f(rv, ev, gv)
        idx = 0
        for o_ref, (_, ws) in zip(o_refs, outs):
            _store_pieces(o_ref, pieces[idx:idx + len(ws)], ws)
            idx += len(ws)
        if accs:
            @pl.when((pl.program_id(0) == 0) & (pl.program_id(1) == 0))
            def _():
                for a_ref in a_refs:
                    a_ref[...] = jnp.zeros_like(a_ref)
            for a_ref, val in zip(a_refs, sums):
                a_ref[...] += val

    out_specs = [pl.BlockSpec((1, ts, sum(ws)), lambda b, s: (b, s, 0)) for (_, ws) in outs]
    out_specs += [pl.BlockSpec((1, w), lambda b, s: (0, 0)) for w in accs]
    out_shape = [jax.ShapeDtypeStruct((bsz, s, sum(ws)), dt) for (dt, ws) in outs]
    out_shape += [jax.ShapeDtypeStruct((1, w), F32) for w in accs]
    return pl.pallas_call(
        body, name=name, grid=(bsz, s // ts),
        in_specs=_row_specs(rows, exs, globs, ts), out_specs=out_specs, out_shape=out_shape,
        compiler_params=_params("arbitrary", "arbitrary"),
    )(*[r[0] for r in rows], *exs, *globs)


def rowwise_bwd(name, f, rows, exs, globs, cts, d_groups, *, ts, n_diff, unit_ct=0):
    bsz, s = rows[0][0].shape[:2]
    ts = min(ts, s)
    n_r, n_e, n_g, n_c = len(rows), len(exs), len(globs), len(cts)
    n_d = len(d_groups)

    def body(*refs):
        rv = [r[0].astype(F32) for r in refs[:n_r]]
        ev = [e[0] for e in refs[n_r:n_r + n_e]]
        gv = [g[...] for g in refs[n_r + n_e:n_r + n_e + n_g]]
        base = n_r + n_e + n_g
        c_refs = refs[base:base + n_c]
        d_refs = refs[base + n_c:base + n_c + n_d]
        de_refs = refs[base + n_c + n_d:base + n_c + n_d + n_e]
        dg_refs = refs[base + n_c + n_d + n_e:]
        fixed = rv[n_diff:]
        out, vjp = jax.vjp(lambda r, e, g: f(r + fixed, e, g), rv[:n_diff], ev, gv)
        ct = []
        for c_ref, (_, ws) in zip(c_refs, cts):
            ct += _load_pieces(c_ref, ws)
        ct += [jnp.ones_like(o) for o in out[len(ct):]]
        assert len(ct) == len(out) and len(out) - unit_ct == sum(len(ws) for _, ws in cts), name
        d_r, d_e, d_g = vjp(ct)
        for d_ref, (_, idxs) in zip(d_refs, d_groups):
            _store_pieces(d_ref, [d_r[i] for i in idxs], [rows[i][2] for i in idxs])
        first_s = pl.program_id(1) == 0
        if n_e:
            @pl.when(first_s)
            def _():
                for r in de_refs:
                    r[...] = jnp.zeros_like(r)
            for r, val in zip(de_refs, d_e):
                r[0] += val
        if n_g:
            @pl.when(first_s & (pl.program_id(0) == 0))
            def _():
                for r in dg_refs:
                    r[...] = jnp.zeros_like(r)
            for r, val in zip(dg_refs, d_g):
                r[...] += val

    in_specs = _row_specs(rows, exs, globs, ts)
    in_specs += [pl.BlockSpec((1, ts, sum(ws)), lambda b, s: (b, s, 0)) for (_, ws) in cts]
    out_specs = [pl.BlockSpec((1, ts, sum(rows[i][2] for i in idxs)), lambda b, s: (b, s, 0)) for (_, idxs) in d_groups]
    out_specs += [pl.BlockSpec((1, 1, e.shape[-1]), lambda b, s: (b, 0, 0)) for e in exs]
    out_specs += [pl.BlockSpec((1, g.shape[-1]), lambda b, s: (0, 0)) for g in globs]
    out_shape = [jax.ShapeDtypeStruct((bsz, s, sum(rows[i][2] for i in idxs)), dt) for (dt, idxs) in d_groups]
    out_shape += [jax.ShapeDtypeStruct(e.shape, F32) for e in exs]
    out_shape += [jax.ShapeDtypeStruct(g.shape, F32) for g in globs]
    res = pl.pallas_call(
        body, name=name, grid=(bsz, s // ts),
        in_specs=in_specs, out_specs=out_specs, out_shape=out_shape,
        compiler_params=_params("arbitrary", "arbitrary"),
    )(*[r[0] for r in rows], *exs, *globs, *[c[0] for c in cts])
    return res[:n_d], res[n_d:n_d + n_e], res[n_d + n_e:]


def _full(a):
    return (a, 0, a.shape[-1])


def _view(a, col, w):
    assert col % w == 0
    return (a, col // w, w)


def _layer_norm(z, g, b):
    mu = jnp.mean(z, -1, keepdims=True)
    var = jnp.mean(jnp.square(z - mu), -1, keepdims=True)
    return (z - mu) * lax.rsqrt(var + LN_EPS) * g + b


def _rms_norm(z, g):
    ms = jnp.mean(jnp.square(z), -1, keepdims=True)
    return z * lax.rsqrt(ms + RMS_EPS) * g


def f_mod(rv, ev, gv):
    (x,), (scale, shift) = rv, ev
    return [x * (1.0 + scale) + shift]


def f_mod_with_x(rv, ev, gv):
    return f_mod(rv, ev, gv) + [rv[0]]


def f_ln_mod(rv, ev, gv):
    (x, y), (gate, scale, shift), (g, b) = rv, ev, gv
    xn = _layer_norm(ALPHA * x + (1.0 + gate) * y, g, b)
    return [xn, xn * (1.0 + scale) + shift]


def f_ln_loss(rv, ev, gv):
    (x, y, target), (gate,), (g, b) = rv, ev, gv
    xn = _layer_norm(ALPHA * x + (1.0 + gate) * y, g, b)
    return [0.5 * jnp.mean(jnp.square(xn - target), -1, keepdims=True)]


def _head_spread(width):
    r2 = QK_ROPE // 2
    j = lax.broadcasted_iota(jnp.int32, (LANES, width), 0)
    col = lax.broadcasted_iota(jnp.int32, (LANES, width), 1) % r2
    return (j == col).astype(F32), (j == col + r2).astype(F32)


def f_mla_mid(rv, ev, gv):
    (q_lat, kv_lat, kr, cos, sin), (q_g, kv_g) = rv, gv
    e1, e2 = _head_spread(cos.shape[-1])
    k1, k2 = hdot(kr, e1), hdot(kr, e2)
    return [_rms_norm(q_lat, q_g), _rms_norm(kv_lat, kv_g), k1 * cos - k2 * sin, k1 * sin + k2 * cos]


def f_hg_pre(rv, ev, gv):
    (q, fx), (lb,) = rv[:2], gv
    f = lb + (1.0 - lb) * jax.nn.sigmoid(fx)
    return [jax.nn.silu(q), jnp.log(f), 1.0 - f]


def f_hg_pre_with_iv(rv, ev, gv):
    return f_hg_pre(rv, ev, gv) + [rv[2], rv[3]]


def f_hg_post(rv, ev, gv):
    nh = len(rv) // 2
    (gn,) = gv
    return [_rms_norm(rv[h], gn) * jax.nn.silu(rv[nh + h]) for h in range(nh)]


def rope_tables(positions):
    bsz, s = positions.shape
    r2 = QK_ROPE // 2
    width = MLA_HEADS * r2
    inv = (ROPE_THETA ** (-np.arange(0, QK_ROPE, 2, dtype=np.float32) / QK_ROPE)).astype(np.float32)
    inv = jnp.asarray(np.tile(inv, MLA_HEADS)[None, :])
    ts = min(ROW_TILE, s)

    def body(p_ref, inv_ref, cos_ref, sin_ref):
        ang = p_ref[0].astype(F32) * inv_ref[...]
        cos_ref[0] = jnp.cos(ang)
        sin_ref[0] = jnp.sin(ang)

    spec = pl.BlockSpec((1, ts, width), lambda b, s: (b, s, 0))
    return pl.pallas_call(
        body, name="rope_tables", grid=(bsz, s // ts),
        in_specs=[pl.BlockSpec((1, ts, 1), lambda b, s: (b, s, 0)), pl.BlockSpec((1, width), lambda b, s: (0, 0))],
        out_specs=[spec, spec], out_shape=[jax.ShapeDtypeStruct((bsz, s, width), F32)] * 2,
        compiler_params=_params("arbitrary", "arbitrary"),
    )(positions[:, :, None], inv)


def _attn_probs(q, k, row0):
    scale = (QK_NOPE + QK_ROPE) ** -0.5
    s = _dg(q, k, 1, 1) * scale
    rows = row0 + lax.broadcasted_iota(jnp.int32, s.shape, 0)
    cols = lax.broadcasted_iota(jnp.int32, s.shape, 1)
    s = jnp.where(cols <= rows, s, jnp.finfo(F32).min)
    e = jnp.exp(s - jnp.max(s, -1, keepdims=True))
    return e / jnp.sum(e, -1, keepdims=True), scale


ATTN_PAIR = 2


def attn_fwd(q, k, v, ride=()):
    bsz, h, s, dq = q.shape
    dv = v.shape[-1]
    tq = min(ATTN_TQ, s)
    grid = (bsz, h // ATTN_PAIR, s // tq)
    rd = _Ride(ride)

    def body(*refs):
        (q_ref, k_ref, v_ref), srcs, (o_ref,), outs, sems = rd.split(refs, 3, 1)
        rd.run(srcs, outs, sems, grid)
        for i in range(grid[2]):
            @pl.when(pl.program_id(2) == i)
            def _(i=i):
                kend = (i + 1) * tq
                for e in range(ATTN_PAIR):
                    p, _ = _attn_probs(q_ref[0, e], k_ref[0, e, :kend, :], i * tq)
                    o_ref[0, :, e * dv:(e + 1) * dv] = _dg(p.astype(BF16), v_ref[0, e, :kend, :], 1, 0).astype(o_ref.dtype)

    res = pl.pallas_call(
        body, name="attn_fwd", grid=grid,
        in_specs=[pl.BlockSpec((1, ATTN_PAIR, tq, dq), lambda b, h, i: (b, h, i, 0)),
                  pl.BlockSpec((1, ATTN_PAIR, s, dq), lambda b, h, i: (b, h, 0, 0)),
                  pl.BlockSpec((1, ATTN_PAIR, s, dv), lambda b, h, i: (b, h, 0, 0))] + rd.in_specs,
        out_specs=[pl.BlockSpec((1, tq, ATTN_PAIR * dv), lambda b, h, i: (b, i, h))] + rd.out_specs,
        out_shape=[jax.ShapeDtypeStruct((bsz, s, h * dv), BF16)] + rd.out_shape, scratch_shapes=rd.scratch,
        compiler_params=_params("arbitrary", "arbitrary", "arbitrary"),
    )(q, k, v, *rd.srcs)
    return res[0], res[1:]


def attn_bwd(q, k, v, do, ride=()):
    bsz, h, s, dq = q.shape
    dv = v.shape[-1]
    tq = min(ATTN_TQ, s)
    grid = (bsz, h // ATTN_PAIR, s // tq)
    rd = _Ride(ride)

    def body(*refs):
        (q_ref, k_ref, v_ref, do_ref), srcs, (dq_ref, dk_ref, dv_ref), outs, sems = rd.split(refs, 4, 3)
        rd.run(srcs, outs, sems, grid)

        @pl.when(pl.program_id(2) == 0)
        def _():
            dk_ref[...] = jnp.zeros_like(dk_ref)
            dv_ref[...] = jnp.zeros_like(dv_ref)

        for i in range(grid[2]):
            @pl.when(pl.program_id(2) == i)
            def _(i=i):
                kend = (i + 1) * tq
                for e in range(ATTN_PAIR):
                    qv, kv, vv = q_ref[0, e], k_ref[0, e, :kend, :], v_ref[0, e, :kend, :]
                    p, scale = _attn_probs(qv, kv, i * tq)
                    do16 = do_ref[0, :, e * dv:(e + 1) * dv].astype(BF16)
                    dv_ref[0, e, :kend, :] += _dg(p.astype(BF16), do16, 0, 0)
                    dp = _dg(do16, vv, 1, 1)
                    ds = (p * (dp - jnp.sum(dp * p, -1, keepdims=True)) * scale).astype(BF16)
                    dq_ref[0, e] = _dg(ds, kv, 1, 0)
                    dk_ref[0, e, :kend, :] += _dg(ds, qv, 0, 0)

    res = pl.pallas_call(
        body, name="attn_bwd", grid=grid,
        in_specs=[pl.BlockSpec((1, ATTN_PAIR, tq, dq), lambda b, h, i: (b, h, i, 0)),
                  pl.BlockSpec((1, ATTN_PAIR, s, dq), lambda b, h, i: (b, h, 0, 0)),
                  pl.BlockSpec((1, ATTN_PAIR, s, dv), lambda b, h, i: (b, h, 0, 0)),
                  pl.BlockSpec((1, tq, ATTN_PAIR * dv), lambda b, h, i: (b, i, h))] + rd.in_specs,
        out_specs=[pl.BlockSpec((1, ATTN_PAIR, tq, dq), lambda b, h, i: (b, h, i, 0)),
                   pl.BlockSpec((1, ATTN_PAIR, s, dq), lambda b, h, i: (b, h, 0, 0)),
                   pl.BlockSpec((1, ATTN_PAIR, s, dv), lambda b, h, i: (b, h, 0, 0))] + rd.out_specs,
        out_shape=[jax.ShapeDtypeStruct((bsz, h, s, dq), F32), jax.ShapeDtypeStruct((bsz, h, s, dq), F32),
                   jax.ShapeDtypeStruct((bsz, h, s, dv), F32)] + rd.out_shape, scratch_shapes=rd.scratch,
        compiler_params=_params("arbitrary", "arbitrary", "arbitrary"),
    )(q, k, v, do, *rd.srcs)
    return res[0], res[1], res[2], res[3:]


def mla_heads(q, kv, krt, cos, sin):
    bsz, s, _ = q.shape
    nh, n, r2, vd = MLA_HEADS, QK_NOPE, QK_ROPE // 2, V_HEAD
    ts = min(ROW_TILE, s)

    def body(q_ref, kv_ref, kr_ref, cos_ref, sin_ref, qh_ref, kh_ref, vh_ref):
        cos, sin = cos_ref[0], sin_ref[0]
        qv, kvv, kr = q_ref[0], kv_ref[0], kr_ref[0].astype(F32)
        q1, q2 = qv[:, nh * n:nh * (n + r2)], qv[:, nh * (n + r2):]
        qr = [q1 * cos - q2 * sin, q1 * sin + q2 * cos]
        for h in range(nh):
            qh_ref[0, h, :, :n] = qv[:, h * n:(h + 1) * n].astype(BF16)
            kh_ref[0, h, :, :n] = kvv[:, h * n:(h + 1) * n].astype(BF16)
            vh_ref[0, h] = kvv[:, nh * n + h * vd:nh * n + (h + 1) * vd].astype(BF16)
            for j in range(2):
                qh_ref[0, h, :, n + j * r2:n + (j + 1) * r2] = qr[j][:, h * r2:(h + 1) * r2].astype(BF16)
                kh_ref[0, h, :, n + j * r2:n + (j + 1) * r2] = kr[:, (j * nh + h) * r2:(j * nh + h + 1) * r2].astype(BF16)

    def row(a):
        return pl.BlockSpec((1, ts, a.shape[-1]), lambda b, i: (b, i, 0))

    def heads(w):
        return pl.BlockSpec((1, nh, ts, w), lambda b, i: (b, 0, i, 0))

    return pl.pallas_call(
        body, name="mla_heads", grid=(bsz, s // ts), in_specs=[row(q), row(kv), row(krt), row(cos), row(sin)],
        out_specs=[heads(n + 2 * r2), heads(n + 2 * r2), heads(vd)],
        out_shape=[jax.ShapeDtypeStruct((bsz, nh, s, n + 2 * r2), BF16)] * 2 + [jax.ShapeDtypeStruct((bsz, nh, s, vd), BF16)],
        compiler_params=_params("arbitrary", "arbitrary"),
    )(q, kv, krt, cos, sin)


def mla_heads_bwd(dqh, dkh, dvh, cos, sin):
    bsz, nh, s, _ = dqh.shape
    n, r2, vd = QK_NOPE, QK_ROPE // 2, V_HEAD
    ts = min(ROW_TILE, s)

    def body(dq_ref, dk_ref, dv_ref, cos_ref, sin_ref, oq_ref, okv_ref, okr_ref, tq_acc, tkv_acc, rot):
        for h in range(nh):
            tq_acc[:, h * n:(h + 1) * n] = dq_ref[0, h, :, :n]
            tkv_acc[:, h * n:(h + 1) * n] = dk_ref[0, h, :, :n]
            tkv_acc[:, nh * n + h * vd:nh * n + (h + 1) * vd] = dv_ref[0, h]
            for j in range(2):
                rot[j, :, h * r2:(h + 1) * r2] = dq_ref[0, h, :, n + j * r2:n + (j + 1) * r2]
                okr_ref[0, :, (j * nh + h) * r2:(j * nh + h + 1) * r2] = dk_ref[0, h, :, n + j * r2:n + (j + 1) * r2]
        cos, sin = cos_ref[0], sin_ref[0]
        d1, d2 = rot[0], rot[1]
        oq_ref[0, :, :nh * n] = tq_acc[...].astype(BF16)
        oq_ref[0, :, nh * n:nh * (n + r2)] = (d1 * cos + d2 * sin).astype(BF16)
        oq_ref[0, :, nh * (n + r2):] = (d2 * cos - d1 * sin).astype(BF16)
        okv_ref[0] = tkv_acc[...].astype(BF16)

    def row(w):
        return pl.BlockSpec((1, ts, w), lambda b, i: (b, i, 0))

    def heads(w):
        return pl.BlockSpec((1, nh, ts, w), lambda b, i: (b, 0, i, 0))

    return pl.pallas_call(
        body, name="mla_heads_bwd", grid=(bsz, s // ts),
        in_specs=[heads(n + 2 * r2), heads(n + 2 * r2), heads(vd), row(nh * r2), row(nh * r2)],
        out_specs=[row(nh * (n + 2 * r2)), row(nh * (n + vd)), row(2 * nh * r2)],
        out_shape=[jax.ShapeDtypeStruct((bsz, s, nh * (n + 2 * r2)), BF16), jax.ShapeDtypeStruct((bsz, s, nh * (n + vd)), BF16),
                   jax.ShapeDtypeStruct((bsz, s, 2 * nh * r2), F32)],
        scratch_shapes=[pltpu.VMEM((ts, nh * n), F32), pltpu.VMEM((ts, nh * (n + vd)), F32), pltpu.VMEM((2, ts, nh * r2), F32)],
        compiler_params=_params("arbitrary", "arbitrary"),
    )(dqh, dkh, dvh, cos, sin)


def _hgrn_tables(c):
    levels = c.bit_length() - 1
    assert 1 << levels == c
    r = np.arange(c)
    prefix, sign, mask = [r[:, None] >= r[None, :]], [], []
    for l in range(levels):
        ref = ((r >> (l + 1)) << (l + 1)) + (1 << l) - 1
        lower = ((r >> l) & 1) == 1
        prefix.append(r[None, :] <= ref[:, None])
        sign.append(np.broadcast_to(np.where(lower, 1.0, -1.0)[:, None], (c, LANES)))
        mask.append((((r[:, None] ^ r[None, :]) >> l) == 1) & lower[:, None])
    return (jnp.asarray(np.concatenate(prefix, 0), BF16), jnp.asarray(np.stack(sign), F32),
            jnp.asarray(np.stack(mask), F32))


def _const_specs(tables):
    return [pl.BlockSpec(t.shape, lambda b, h, i, nd=t.ndim: (0,) * nd) for t in tables]


def _split3(x):
    hi = x.astype(BF16)
    rest = x - hi.astype(F32)
    mid = rest.astype(BF16)
    return hi, mid, (rest - mid.astype(F32)).astype(BF16)


@functools.partial(jax.custom_vjp, nondiff_argnums=(2,))
def prefix_sums(p, g, n):
    k = g.shape[1]
    r = _dg(p, jnp.concatenate(_split3(g), axis=1), 1, 0)
    r = r[:, :k] + r[:, k:2 * k] + r[:, 2 * k:]
    c = r.shape[0] // n
    return tuple(r[i * c:(i + 1) * c] for i in range(n))


def _prefix_fwd(p, g, n):
    return prefix_sums(p, g, n), p


def _prefix_bwd(n, p, ct):
    ct = jnp.concatenate(ct, axis=0)
    k = ct.shape[1]
    r = _dg(p, jnp.concatenate(_split3(ct), axis=1), 0, 0)
    return jnp.zeros_like(p), r[:, :k] + r[:, k:2 * k] + r[:, 2 * k:]


prefix_sums.defvjp(_prefix_fwd, _prefix_bwd)


def _hgrn_chunk(q, g, k, v, st0, prefix, sign, mask):
    levels = len(mask)
    pre = prefix_sums(prefix, g, levels + 1)
    b = pre[0]
    o = bdot(q * jnp.exp(b), st0, 1, 1)
    att = None
    for l in range(levels):
        e = jnp.exp((b - pre[l + 1]) * sign[l])
        a = bdot(q * e, k * e, 1, 1) * mask[l]
        att = a if att is None else att + a
    o = o + bdot(att, v, 1, 0) + jnp.sum(q * k, -1, keepdims=True) * v
    total = jnp.sum(g, 0, keepdims=True)
    st1 = st0 * jnp.exp(total) + bdot(v, k * jnp.exp(total - b), 0, 0)
    return o, st1


def hgrn_fwd(qs, logf, kk, proj, ride=()):
    bsz, s, hk = qs.shape
    kd = HGRN_EXPAND
    c = min(HGRN_CHUNK, s)
    nh, nc = hk // kd, s // c
    hp = math.gcd(nh, HGRN_HEADS_PER_STEP)
    tables = _hgrn_tables(c)
    levels = tables[2].shape[0]
    grid = (bsz, nh // hp, nc)
    rd = _Ride(ride)

    def body(*refs):
        (q_ref, g_ref, k_ref, v_ref, p_ref, sg_ref, mk_ref), srcs, (o_ref, st_ref), outs, scratch = rd.split(refs, 7, 2)
        state = scratch[0]
        rd.run(srcs, outs, scratch, grid)

        @pl.when(pl.program_id(2) == 0)
        def _():
            state[...] = jnp.zeros_like(state)

        prefix, sign, mask = p_ref[...], [sg_ref[l] for l in range(levels)], [mk_ref[l] for l in range(levels)]
        for j in range(hp):
            cols = slice(j * kd, (j + 1) * kd)
            st0 = state[j]
            st_ref[0, j, 0] = st0
            o, st1 = _hgrn_chunk(q_ref[0, :, cols], g_ref[0, :, cols], k_ref[0, :, cols], v_ref[0, :, cols], st0,
                                 prefix, sign, mask)
            o_ref[0, :, cols] = o
            state[j] = st1

    blk = pl.BlockSpec((1, c, hp * kd), lambda b, h, i: (b, i, h))
    res = pl.pallas_call(
        body, name="hgrn_fwd", grid=grid,
        in_specs=[blk, blk, blk, pl.BlockSpec((1, c, hp * kd), lambda b, h, i: (b, i, 2 * (nh // hp) + h))]
        + _const_specs(tables) + rd.in_specs,
        out_specs=[blk, pl.BlockSpec((1, hp, 1, kd, kd), lambda b, h, i: (b, h, i, 0, 0))] + rd.out_specs,
        out_shape=[jax.ShapeDtypeStruct((bsz, s, hk), F32), jax.ShapeDtypeStruct((bsz, nh, nc, kd, kd), F32)] + rd.out_shape,
        scratch_shapes=[pltpu.VMEM((hp, kd, kd), F32)] + rd.scratch,
        compiler_params=_params("arbitrary", "arbitrary", "arbitrary"),
    )(qs, logf, kk, proj, *tables, *rd.srcs)
    return res[0], res[1], res[2:]


def hgrn_bwd(qs, logf, kk, proj, states, do, ride=()):
    bsz, s, hk = qs.shape
    kd = HGRN_EXPAND
    c = min(HGRN_CHUNK, s)
    nh, nc = hk // kd, s // c
    hp = math.gcd(nh, HGRN_HEADS_PER_STEP)
    tables = _hgrn_tables(c)
    levels = tables[2].shape[0]
    grid = (bsz, nh // hp, nc)
    rd = _Ride(ride)

    def body(*refs):
        ((q_ref, g_ref, k_ref, v_ref, st_ref, do_ref, p_ref, sg_ref, mk_ref), srcs, (dq_ref, dg_ref, dk_ref, dv_ref), outs,
         scratch) = rd.split(refs, 9, 4)
        dstate = scratch[0]
        rd.run(srcs, outs, scratch, grid)

        @pl.when(pl.program_id(2) == 0)
        def _():
            dstate[...] = jnp.zeros_like(dstate)

        prefix, sign, mask = p_ref[...], [sg_ref[l] for l in range(levels)], [mk_ref[l] for l in range(levels)]
        for j in range(hp):
            cols = slice(j * kd, (j + 1) * kd)
            _, vjp = jax.vjp(lambda q, g, k, v, st: _hgrn_chunk(q, g, k, v, st, prefix, sign, mask),
                             q_ref[0, :, cols], g_ref[0, :, cols], k_ref[0, :, cols], v_ref[0, :, cols], st_ref[0, j, 0])
            dq, dg, dk, dv, dst = vjp((do_ref[0, :, cols], dstate[j]))
            dq_ref[0, :, cols] = dq
            dg_ref[0, :, cols] = dg
            dk_ref[0, :, cols] = dk
            dv_ref[0, :, cols] = dv
            dstate[j] = dst

    blk = pl.BlockSpec((1, c, hp * kd), lambda b, h, i: (b, nc - 1 - i, h))
    shape = jax.ShapeDtypeStruct((bsz, s, hk), F32)
    res = pl.pallas_call(
        body, name="hgrn_bwd", grid=grid,
        in_specs=[blk, blk, blk, pl.BlockSpec((1, c, hp * kd), lambda b, h, i: (b, nc - 1 - i, 2 * (nh // hp) + h)),
                  pl.BlockSpec((1, hp, 1, kd, kd), lambda b, h, i: (b, h, nc - 1 - i, 0, 0)), blk]
        + _const_specs(tables) + rd.in_specs,
        out_specs=[blk] * 4 + rd.out_specs, out_shape=[shape] * 4 + rd.out_shape,
        scratch_shapes=[pltpu.VMEM((hp, kd, kd), F32)] + rd.scratch,
        compiler_params=_params("arbitrary", "arbitrary", "arbitrary"),
    )(qs, logf, kk, proj, states, do, *tables, *rd.srcs)
    return res[0], res[1], res[2], res[3], res[4:]


def cast_bf16(name, w):
    blk = pl.BlockSpec((1,) + w.shape[1:], lambda l: (l, 0, 0))

    def body(w_ref, o_ref):
        o_ref[...] = w_ref[...].astype(BF16)

    return pl.pallas_call(body, name=name, grid=(w.shape[0],), in_specs=[blk], out_specs=blk,
                          out_shape=jax.ShapeDtypeStruct(w.shape, BF16), compiler_params=_params("arbitrary"))(w)


def _lower_bounds(rows):
    m = functools.reduce(jnp.maximum, rows)
    e = [jnp.exp(r - m) for r in rows]
    z = functools.reduce(lambda a, b: a + b, e)
    soft = [x / z for x in e]
    out, run = [], jnp.zeros_like(rows[0])
    for sft in soft:
        run = run + sft
        out.append(run - soft[0])
    return out


def lower_bounds(lb):
    n = lb.shape[0]

    def body(lb_ref, o_ref):
        for i, r in enumerate(_lower_bounds([lb_ref[i:i + 1, :] for i in range(n)])):
            o_ref[i:i + 1, :] = r

    return pl.pallas_call(body, name="lower_bounds", out_shape=jax.ShapeDtypeStruct(lb.shape, F32),
                          compiler_params=_params())(lb)


def ada_fwd(c_all, ada_w, ada_b):
    nl, ns, d, cols = ada_w.shape
    n_ex = c_all.shape[0]

    def body(c_ref, w_ref, b_ref, o_ref):
        a = jax.nn.silu(c_ref[...]).astype(BF16)
        o_ref[0] = _dg(a, w_ref[0].astype(BF16), 1, 0) + b_ref[0]

    return pl.pallas_call(
        body, name="ada_fwd", grid=(nl * ns,),
        in_specs=[pl.BlockSpec((n_ex, d), lambda i: (0, 0)), pl.BlockSpec((1, d, cols), lambda i: (i, 0, 0)),
                  pl.BlockSpec((1, 1, cols), lambda i: (i, 0, 0))],
        out_specs=pl.BlockSpec((1, n_ex, cols), lambda i: (i, 0, 0)),
        out_shape=jax.ShapeDtypeStruct((nl * ns, n_ex, cols), F32), compiler_params=_params("arbitrary"),
    )(c_all, ada_w.reshape(nl * ns, d, cols), ada_b.reshape(nl * ns, 1, cols))


def ada_bwd(c_all, dmod):
    n, n_ex, cols = dmod.shape
    d = c_all.shape[1]

    def body(c_ref, g_ref, dw_ref, db_ref):
        a = jax.nn.silu(c_ref[...]).astype(BF16)
        g = g_ref[0]
        dw_ref[0] = _dg(a, g.astype(BF16), 0, 0)
        db_ref[0] = jnp.sum(g, 0, keepdims=True)

    return pl.pallas_call(
        body, name="ada_bwd", grid=(n,),
        in_specs=[pl.BlockSpec((n_ex, d), lambda i: (0, 0)), pl.BlockSpec((1, n_ex, cols), lambda i: (i, 0, 0))],
        out_specs=[pl.BlockSpec((1, d, cols), lambda i: (i, 0, 0)), pl.BlockSpec((1, 1, cols), lambda i: (i, 0, 0))],
        out_shape=[jax.ShapeDtypeStruct((n, d, cols), F32), jax.ShapeDtypeStruct((n, 1, cols), F32)],
        compiler_params=_params("arbitrary"),
    )(c_all, dmod)


def _adam_math(g, w, m, v):
    m = ADAM_B1 * m + (1.0 - ADAM_B1) * g
    v = ADAM_B2 * v + (1.0 - ADAM_B2) * jnp.square(g)
    m_hat = m / (1.0 - ADAM_B1 ** ADAM_STEP)
    v_hat = v / (1.0 - ADAM_B2 ** ADAM_STEP)
    delta = -ADAM_LR * (m_hat / (jnp.sqrt(v_hat) + ADAM_EPS) + ADAM_WD * w)
    return delta, m, v


def adam(name, gstack, w, m, v):
    shape = w.shape
    n, cols = gstack.shape[0], shape[-1]
    rows = math.prod(shape[:-1])
    tr = _pick_rows(rows, max(8, (2 * 1024 * 1024) // (4 * cols * n)))

    def body(g_ref, w_ref, m_ref, v_ref, go_ref, d_ref, mo_ref, vo_ref):
        g = g_ref[0].astype(F32)
        for i in range(1, n):
            g = g + g_ref[i].astype(F32)
        delta, m1, v1 = _adam_math(g, w_ref[...], m_ref[...], v_ref[...])
        go_ref[...] = g
        d_ref[...] = delta
        mo_ref[...] = m1
        vo_ref[...] = v1

    blk = pl.BlockSpec((tr, cols), lambda i: (i, 0))
    out = pl.pallas_call(
        body, name=name, grid=(rows // tr,),
        in_specs=[pl.BlockSpec((n, tr, cols), lambda i: (0, i, 0)), blk, blk, blk],
        out_specs=[blk] * 4, out_shape=[jax.ShapeDtypeStruct((rows, cols), F32)] * 4,
        compiler_params=_params("arbitrary"),
    )(gstack.reshape(n, rows, cols), w.reshape(rows, cols), m.reshape(rows, cols), v.reshape(rows, cols))
    return [o.reshape(shape) for o in out]


def adam_layers(name, gs, w, m, v):
    shape = w.shape
    nl, n, cols = len(gs), gs[0].shape[0], shape[-1]
    rows = math.prod(shape[1:-1])
    tr = _pick_rows(rows, max(16, (2 * 1024 * 1024) // (4 * cols * n)))
    nt = rows // tr

    def body(*refs):
        g_refs, (w_ref, m_ref, v_ref, go_ref, d_ref, mo_ref, vo_ref) = refs[:nl], refs[nl:]
        for j in range(nl):
            @pl.when(pl.program_id(0) == j)
            def _(j=j):
                g = g_refs[j][0].astype(F32)
                for i in range(1, n):
                    g = g + g_refs[j][i].astype(F32)
                delta, m1, v1 = _adam_math(g, w_ref[...], m_ref[...], v_ref[...])
                go_ref[...] = g
                d_ref[...] = delta
                mo_ref[...] = m1
                vo_ref[...] = v1

    def g_spec(j):
        return pl.BlockSpec((n, tr, cols), lambda l, i: (0, jnp.where(l == j, i, jnp.where(l < j, 0, nt - 1)), 0))

    blk = pl.BlockSpec((tr, cols), lambda l, i: (l * nt + i, 0))
    out = pl.pallas_call(
        body, name=name, grid=(nl, nt),
        in_specs=[g_spec(j) for j in range(nl)] + [blk, blk, blk],
        out_specs=[blk] * 4, out_shape=[jax.ShapeDtypeStruct((nl * rows, cols), F32)] * 4,
        compiler_params=_params("arbitrary", "arbitrary"),
    )(*[g.reshape(n, rows, cols) for g in gs], w.reshape(nl * rows, cols), m.reshape(nl * rows, cols), v.reshape(nl * rows, cols))
    return [o.reshape(shape) for o in out]


def adam_lb(gstack, lb, m, v):
    n, nl = gstack.shape[0], lb.shape[0]

    def body(g_ref, w_ref, m_ref, v_ref, go_ref, d_ref, mo_ref, vo_ref):
        rows = [w_ref[i:i + 1, :] for i in range(nl)]
        ct = []
        for i in range(nl):
            g = g_ref[0, i:i + 1, :]
            for j in range(1, n):
                g = g + g_ref[j, i:i + 1, :]
            ct.append(g)
        _, vjp = jax.vjp(_lower_bounds, rows)
        (grads,) = vjp(ct)
        for i in range(nl):
            delta, m1, v1 = _adam_math(grads[i], rows[i], m_ref[i:i + 1, :], v_ref[i:i + 1, :])
            go_ref[i:i + 1, :] = grads[i]
            d_ref[i:i + 1, :] = delta
            mo_ref[i:i + 1, :] = m1
            vo_ref[i:i + 1, :] = v1

    return pl.pallas_call(body, name="adam_hgrn_lb", out_shape=[jax.ShapeDtypeStruct(lb.shape, F32)] * 4,
                          compiler_params=_params())(gstack, lb, m, v)


class _Ride:
    def __init__(self, items):
        self.items = list(items)
        n = len(self.items)
        self.srcs = [src for src, _ in self.items]
        self.in_specs = [pl.BlockSpec(memory_space=pl.ANY)] * n
        self.out_specs = [pl.BlockSpec(memory_space=pl.ANY)] * n
        self.out_shape = [jax.ShapeDtypeStruct(((N_DEV,) + s.shape) if mode == "gather" else s.shape, s.dtype)
                          for s, mode in self.items]
        self.scratch = [pltpu.SemaphoreType.DMA((n, N_DEV - 1)), pltpu.SemaphoreType.DMA((n, N_DEV - 1)),
                        pltpu.SemaphoreType.DMA((n,))] if n else []

    def split(self, refs, n_in, n_out):
        n = len(self.items)
        a, b = n_in + n, n_in + 2 * n + n_out
        return refs[:n_in], refs[n_in:a], refs[a:a + n_out], refs[a + n_out:b], refs[b:]

    def _copies(self, srcs, outs, sems):
        send_sems, recv_sems, local_sems = sems
        x, y, c = lax.axis_index("x"), lax.axis_index("y"), lax.axis_index("c")
        me = 4 * x + 2 * y + c
        copies = []
        for i, (_, mode) in enumerate(self.items):
            mine = srcs[i] if mode == "gather" else srcs[i].at[me]
            copies.append(pltpu.make_async_copy(mine, outs[i].at[me], local_sems.at[i]))
            for p in range(1, N_DEV):
                px = 1 - x if p & 4 else x
                py = 1 - y if p & 2 else y
                pc = 1 - c if p & 1 else c
                part = srcs[i] if mode == "gather" else srcs[i].at[4 * px + 2 * py + pc]
                copies.append(pltpu.make_async_remote_copy(
                    src_ref=part, dst_ref=outs[i].at[me], send_sem=send_sems.at[i, p - 1], recv_sem=recv_sems.at[i, p - 1],
                    device_id=(px, py, pc), device_id_type=pl.DeviceIdType.MESH))
        return copies

    def run(self, srcs, outs, scratch, grid=()):
        if not self.items:
            return
        sems = scratch[len(scratch) - 3:]
        if not grid:
            copies = self._copies(srcs, outs, sems)
            for cp in copies:
                cp.start()
            for cp in copies:
                cp.wait()
            return
        ids = [pl.program_id(a) for a in range(len(grid))]
        first = functools.reduce(lambda a, b: a & b, [i == 0 for i in ids])
        last = functools.reduce(lambda a, b: a & b, [i == g - 1 for i, g in zip(ids, grid)])

        @pl.when(first)
        def _():
            for cp in self._copies(srcs, outs, sems):
                cp.start()

        @pl.when(last)
        def _():
            for cp in self._copies(srcs, outs, sems):
                cp.wait()


def exchange(name, items):
    rd = _Ride(items)

    def body(*refs):
        _, srcs, _, outs, scratch = rd.split(refs, 0, 0)
        rd.run(srcs, outs, scratch)

    return pl.pallas_call(body, name=name, in_specs=rd.in_specs, out_specs=rd.out_specs, out_shape=rd.out_shape,
                          scratch_shapes=rd.scratch)(*rd.srcs)


def _from_gather(name, g):
    if name in COL_SHARDED:
        _, k, n = g.shape
        return g.transpose(1, 0, 2).reshape(k, N_DEV * n)
    return g.reshape(-1, g.shape[-1])


def _to_slabs(name, w):
    if isinstance(w, tuple):
        per = N_DEV // len(w)
        return jnp.concatenate([p.reshape(p.shape[0], per, p.shape[1] // per).transpose(1, 0, 2) for p in w], axis=0)
    k, n = w.shape
    if name in COL_SHARDED:
        return w.reshape(k, N_DEV, n // N_DEV).transpose(1, 0, 2)
    return w.reshape(N_DEV, k // N_DEV, n)


def _w_in_internal(w):
    return jnp.pad(w, ((0, 0), (0, LANES - QK_ROPE)))


def _qb_internal(w, inverse=False):
    h, n, r2 = MLA_HEADS, QK_NOPE, QK_ROPE // 2
    lead = w.shape[:-1]
    if not inverse:
        w = w.reshape(lead + (h, n + 2 * r2))
        parts = [w[..., :n], w[..., n:n + r2], w[..., n + r2:]]
        return jnp.concatenate([p.reshape(lead + (-1,)) for p in parts], axis=-1)
    parts = [w[..., :h * n].reshape(lead + (h, n)), w[..., h * n:h * (n + r2)].reshape(lead + (h, r2)),
             w[..., h * (n + r2):].reshape(lead + (h, r2))]
    return jnp.concatenate(parts, axis=-1).reshape(lead + (-1,))


def _kvb_internal(w, inverse=False):
    h, n, vd = MLA_HEADS, QK_NOPE, V_HEAD
    lead = w.shape[:-1]
    if not inverse:
        w = w.reshape(lead + (h, n + vd))
        return jnp.concatenate([w[..., :n].reshape(lead + (-1,)), w[..., n:].reshape(lead + (-1,))], axis=-1)
    parts = [w[..., :h * n].reshape(lead + (h, n)), w[..., h * n:].reshape(lead + (h, vd))]
    return jnp.concatenate(parts, axis=-1).reshape(lead + (-1,))


def _mla_forward(h, w, tabs, ride=()):
    cos, sin = tabs
    r2 = MLA_HEADS * (QK_ROPE // 2)
    proj = mm3("mla_proj", h, w['w_in'])
    qn, kvn, krt = rowwise(
        "mla_mid", lambda rv, ev, gv: (f_mla_mid(rv, ev, gv), []),
        [_view(proj, 0, Q_LORA), _view(proj, Q_LORA, KV_LORA), _view(proj, Q_LORA + KV_LORA, LANES), _full(cos), _full(sin)],
        [], [w['q_norm'], w['kv_norm']], [(BF16, [Q_LORA]), (BF16, [KV_LORA]), (BF16, [r2, r2])], ts=ROW_TILE)
    q = mm3("mla_q", qn, w['w_qb'])
    kv = mm3("mla_kv", kvn, w['w_kvb'])
    qh, kh, vh = mla_heads(q, kv, krt, cos, sin)
    o, got = attn_fwd(qh, kh, vh, ride)
    y = mm3("mla_out", o, w['w_o'])
    return y, dict(h=h, proj=proj, qn=qn, kvn=kvn, qh=qh, kh=kh, vh=vh, o=o), got


def _mla_backward(dy, sv, w, tabs, ride=()):
    cos, sin = tabs
    r2 = MLA_HEADS * (QK_ROPE // 2)
    g = {}
    g['w_o'] = wgrad("mla_out_dw", sv['o'], dy)
    do = mm3("mla_out_dx", dy, w['w_o'], tb=True)
    dqh, dkh, dvh, got = attn_bwd(sv['qh'], sv['kh'], sv['vh'], do, ride)
    dq, dkv, dkrt = mla_heads_bwd(dqh, dkh, dvh, cos, sin)
    g['w_qb'] = wgrad("mla_q_dw", sv['qn'], dq)
    g['w_kvb'] = wgrad("mla_kv_dw", sv['kvn'], dkv)
    dqn = mm3("mla_q_dx", dq, w['w_qb'], tb=True)
    dkvn = mm3("mla_kv_dx", dkv, w['w_kvb'], tb=True)
    proj = sv['proj']
    (dproj,), _, (g['q_norm'], g['kv_norm']) = rowwise_bwd(
        "mla_mid_bwd", f_mla_mid,
        [_view(proj, 0, Q_LORA), _view(proj, Q_LORA, KV_LORA), _view(proj, Q_LORA + KV_LORA, LANES), _full(cos), _full(sin)],
        [], [w['q_norm'], w['kv_norm']], [(dqn, [Q_LORA]), (dkvn, [KV_LORA]), (dkrt, [r2, r2])],
        [(BF16, [0, 1, 2])], ts=ROW_TILE, n_diff=3)
    g['w_in'] = wgrad("mla_proj_dw", sv['h'], dproj)
    dh = mm3("mla_proj_dx", dproj, w['w_in'], tb=True)
    return dh, g, got


def _hgrn_views(proj):
    d = proj.shape[-1] // 4
    return [_view(proj, i * d, d) for i in range(4)]


def _head_views(a, col0, nh):
    return [_view(a, col0 + i * HGRN_EXPAND, HGRN_EXPAND) for i in range(nh)]


def _hgrn_forward(h, w, ride=()):
    proj = mm3("hgrn_proj", h, w['w_in'])
    d = proj.shape[-1] // 4
    nh = d // HGRN_EXPAND
    vq, vf, _, _ = _hgrn_views(proj)
    qs, logf, kk = rowwise("hgrn_pre", lambda rv, ev, gv: (f_hg_pre(rv, ev, gv), []), [vq, vf], [], [w['lb']],
                           [(F32, [d]), (F32, [d]), (F32, [d])], ts=ROW_TILE)
    o, states, got = hgrn_fwd(qs, logf, kk, proj, ride)
    (z,) = rowwise("hgrn_post", lambda rv, ev, gv: (f_hg_post(rv, ev, gv), []),
                   _head_views(o, 0, nh) + _head_views(proj, 3 * d, nh), [], [w['g_norm']],
                   [(BF16, [HGRN_EXPAND] * nh)], ts=ROW_TILE)
    y = mm3("hgrn_out", z, w['w_o'])
    return y, dict(h=h, proj=proj, qs=qs, logf=logf, kk=kk, o=o, states=states, z=z), got


def _hgrn_backward(dy, sv, w, ride=()):
    g = {}
    proj, o = sv['proj'], sv['o']
    d = proj.shape[-1] // 4
    nh = d // HGRN_EXPAND
    g['w_o'] = wgrad("hgrn_out_dw", sv['z'], dy)
    dz = mm3("hgrn_out_dx", dy, w['w_o'], tb=True)
    (do, dgate), _, (g['g_norm'],) = rowwise_bwd(
        "hgrn_post_bwd", f_hg_post, _head_views(o, 0, nh) + _head_views(proj, 3 * d, nh), [], [w['g_norm']],
        [(dz, [HGRN_EXPAND] * nh)], [(F32, list(range(nh))), (F32, list(range(nh, 2 * nh)))], ts=ROW_TILE, n_diff=2 * nh)
    dqs, dlogf, dkk, dv, got = hgrn_bwd(sv['qs'], sv['logf'], sv['kk'], proj, sv['states'], do, ride)
    (dproj,), _, (g['lb'],) = rowwise_bwd(
        "hgrn_pre_bwd", f_hg_pre_with_iv, _hgrn_views(proj), [], [w['lb']],
        [(dqs, [d]), (dlogf, [d]), (dkk, [d]), (dv, [d]), (dgate, [d])], [(BF16, [0, 1, 2, 3])], ts=ROW_TILE // 2, n_diff=4)
    g['w_in'] = wgrad("hgrn_proj_dw", sv['h'], dproj)
    dh = mm3("hgrn_proj_dx", dproj, w['w_in'], tb=True)
    return dh, g, got


def _ffn_forward(h, w):
    ug, uu, a = ffn_in_act(h, w['w_in'])
    y = mm3("ffn_out", a, w['w_out'])
    return y, dict(h=h, ug=ug, uu=uu, a=a)


def _ffn_backward(dy, sv, w):
    g = {}
    g['w_out'] = wgrad("ffn_out_dw", sv['a'], dy)
    dug, duu = ffn_out_dx_act(dy, w['w_out'], sv['ug'], sv['uu'])
    g['w_in'] = (wgrad("ffn_in_dw", sv['h'], dug), wgrad("ffn_in_dw", sv['h'], duu))
    dh = mm3("ffn_in_dx", [dug, duu], w['w_in'], tb=True)
    return dh, g


def kernel(x, c, positions, mla_w_in, mla_q_norm, mla_w_qb, mla_kv_norm, mla_w_kvb, mla_w_o, hgrn_lb, hgrn_w_in, hgrn_g_norm, hgrn_w_o, ffn_w_in, ffn_w_out, ada_w, ada_b, ln_g, ln_b, loss_target, m_mla_w_in, m_mla_q_norm, m_mla_w_qb, m_mla_kv_norm, m_mla_w_kvb, m_mla_w_o, m_hgrn_lb, m_hgrn_w_in, m_hgrn_g_norm, m_hgrn_w_o, m_ffn_w_in, m_ffn_w_out, m_ada_w, m_ada_b, m_ln_g, m_ln_b, v_mla_w_in, v_mla_q_norm, v_mla_w_qb, v_mla_kv_norm, v_mla_w_kvb, v_mla_w_o, v_hgrn_lb, v_hgrn_w_in, v_hgrn_g_norm, v_hgrn_w_o, v_ffn_w_in, v_ffn_w_out, v_ada_w, v_ada_b, v_ln_g, v_ln_b):
    W = dict(zip(WEIGHTS, (mla_w_in, mla_q_norm, mla_w_qb, mla_kv_norm, mla_w_kvb, mla_w_o, hgrn_lb, hgrn_w_in, hgrn_g_norm,
                           hgrn_w_o, ffn_w_in, ffn_w_out, ada_w, ada_b, ln_g, ln_b)))
    M1 = dict(zip(WEIGHTS, (m_mla_w_in, m_mla_q_norm, m_mla_w_qb, m_mla_kv_norm, m_mla_w_kvb, m_mla_w_o, m_hgrn_lb, m_hgrn_w_in,
                            m_hgrn_g_norm, m_hgrn_w_o, m_ffn_w_in, m_ffn_w_out, m_ada_w, m_ada_b, m_ln_g, m_ln_b)))
    M2 = dict(zip(WEIGHTS, (v_mla_w_in, v_mla_q_norm, v_mla_w_qb, v_mla_kv_norm, v_mla_w_kvb, v_mla_w_o, v_hgrn_lb, v_hgrn_w_in,
                            v_hgrn_g_norm, v_hgrn_w_o, v_ffn_w_in, v_ffn_w_out, v_ada_w, v_ada_b, v_ln_g, v_ln_b)))
    bsz, seq, d = x.shape
    depth, n_mla, n_hgrn = ffn_w_in.shape[0], mla_w_in.shape[0], hgrn_w_in.shape[0]
    n_sub = 2 * depth

    big = COL_SHARDED + ROW_SHARDED
    wb = {n: cast_bf16("cast_" + n, W[n]) for n in big}

    def mixer_names(layer):
        mixer = ['mla_w_in', 'mla_w_qb', 'mla_w_kvb', 'mla_w_o'] if layer % 2 == 0 else ['hgrn_w_in', 'hgrn_w_o']
        return [(n, layer // 2) for n in mixer]

    def carried(layer):
        return [('ffn_w_in', layer), ('ffn_w_out', layer)] + (mixer_names(layer + 1) if layer + 1 < depth else [])

    def weight_items(names):
        return [(wb[n][j], "gather") for n, j in names]

    G = {}
    internal = {'mla_w_in': _w_in_internal, 'mla_w_qb': _qb_internal, 'mla_w_kvb': _kvb_internal}

    def take_weights(names, got):
        for (n, j), a in zip(names, got):
            G[n, j] = internal.get(n, lambda w: w)(_from_gather(n, a))

    lower_shard = lower_bounds(hgrn_lb)
    got = exchange("gather_first", [(lower_shard, "gather"), (ln_g, "gather"), (ln_b, "gather"), (c, "gather")]
                   + weight_items(mixer_names(0)))
    lower_all = got[0].transpose(1, 0, 2).reshape(n_hgrn, -1)
    ln_g_all = got[1].transpose(1, 2, 0, 3).reshape(depth, 2, d)
    ln_b_all = got[2].transpose(1, 2, 0, 3).reshape(depth, 2, d)
    c_all = got[3].reshape(N_DEV * bsz, d)
    take_weights(mixer_names(0), got[4:])

    cols = ada_w.shape[-1]
    mod_loc = ada_fwd(c_all, ada_w, ada_b)
    (mod_got,) = exchange("scatter_mod", [(mod_loc.reshape(n_sub, N_DEV, bsz, cols).transpose(1, 0, 2, 3), "a2a")])
    mod = mod_got.transpose(1, 2, 0, 3).reshape(n_sub, bsz, 1, 3 * d)
    shift = [mod[k, :, :, 0:d] for k in range(n_sub)]
    scale = [mod[k, :, :, d:2 * d] for k in range(n_sub)]
    gate = [mod[k, :, :, 2 * d:] for k in range(n_sub)]
    lng = [ln_g_all[k // 2, k % 2][None, :] for k in range(n_sub)]
    lnb = [ln_b_all[k // 2, k % 2][None, :] for k in range(n_sub)]

    tabs = rope_tables(positions)

    def sub_weights(k):
        layer, j = k // 2, k // 4
        if k % 2:
            return 'ffn', layer, dict(w_in=G['ffn_w_in', layer], w_out=G['ffn_w_out', layer])
        if layer % 2 == 0:
            return 'mla', j, dict(w_in=G['mla_w_in', j], q_norm=mla_q_norm[j][None, :], w_qb=G['mla_w_qb', j],
                                  kv_norm=mla_kv_norm[j][None, :], w_kvb=G['mla_w_kvb', j], w_o=G['mla_w_o', j])
        return 'hgrn', j, dict(w_in=G['hgrn_w_in', j], lb=lower_all[j][None, :], g_norm=hgrn_g_norm[j][None, :],
                               w_o=G['hgrn_w_o', j])

    (h,) = rowwise("mod_first", lambda rv, ev, gv: (f_mod(rv, ev, gv), []), [_full(x)], [scale[0], shift[0]], [],
                   [(BF16, [d])], ts=ROW_TILE)
    xs, ys, saved = [x], [], []
    loss_acc = None
    for k in range(n_sub):
        kind, _, w = sub_weights(k)
        ride = weight_items(carried(k // 2)) if k % 2 == 0 else []
        if kind == 'ffn':
            y, sv = _ffn_forward(h, w)
        elif kind == 'mla':
            y, sv, got = _mla_forward(h, w, tabs, ride)
        else:
            y, sv, got = _hgrn_forward(h, w, ride)
        if ride:
            take_weights(carried(k // 2), got)
        ys.append(y)
        saved.append(sv)
        if k + 1 < n_sub:
            xn, h = rowwise("ln_mod", lambda rv, ev, gv: (f_ln_mod(rv, ev, gv), []), [_full(xs[k]), _full(y)],
                            [gate[k], scale[k + 1], shift[k + 1]], [lng[k], lnb[k]], [(F32, [d]), (BF16, [d])], ts=ROW_TILE)
            xs.append(xn)
        else:
            def loss_rows(rv, ev, gv):
                (row,) = f_ln_loss(rv, ev, gv)
                return [], [jnp.broadcast_to(jnp.sum(row, keepdims=True), (1, LANES))]
            (loss_acc,) = rowwise("ln_loss", loss_rows, [_full(xs[k]), _full(y), _full(loss_target)], [gate[k]], [lng[k], lnb[k]],
                                  [], ts=ROW_TILE, accs=[LANES])
    loss = lax.psum(loss_acc[0, 0], ("x", "y", "c"))

    d_shift, d_scale, d_gate = [None] * n_sub, [None] * n_sub, [None] * n_sub
    d_lng, d_lnb = [None] * n_sub, [None] * n_sub
    part = {n: [None] * W[n].shape[0] for n in ['mla_q_norm', 'mla_kv_norm', 'hgrn_g_norm']}
    recv = {n: [None] * W[n].shape[0] for n in big}
    d_lower = [None] * n_hgrn
    k = n_sub - 1
    (dx, dy), (d_gate[k],), (d_lng[k], d_lnb[k]) = rowwise_bwd(
        "ln_loss_bwd", f_ln_loss, [_full(xs[k]), _full(ys[k]), _full(loss_target)], [gate[k]], [lng[k], lnb[k]], [],
        [(F32, [0]), (BF16, [1])], ts=ROW_TILE, n_diff=2, unit_ct=1)
    grad_x = None
    mine = {}

    def take_grads(names, got):
        for (n, j), a in zip(names, got):
            recv[n][j] = a

    def grad_items(names):
        return [(_to_slabs(n, mine[n, jj]), "a2a") for n, jj in names]

    for k in range(n_sub - 1, -1, -1):
        kind, j, w = sub_weights(k)
        ride = grad_items(carried(k // 2)) if kind != 'ffn' else []
        if kind == 'ffn':
            dh, g = _ffn_backward(dy, saved[k], w)
            new = {('ffn_w_in', j): g['w_in'], ('ffn_w_out', j): g['w_out']}
        elif kind == 'mla':
            dh, g, got = _mla_backward(dy, saved[k], w, tabs, ride)
            new = {('mla_w_in', j): g['w_in'][:, :mla_w_in.shape[-1] * N_DEV], ('mla_w_qb', j): _qb_internal(g['w_qb'], inverse=True),
                   ('mla_w_kvb', j): _kvb_internal(g['w_kvb'], inverse=True), ('mla_w_o', j): g['w_o']}
            part['mla_q_norm'][j], part['mla_kv_norm'][j] = g['q_norm'][0], g['kv_norm'][0]
        else:
            dh, g, got = _hgrn_backward(dy, saved[k], w, ride)
            new = {('hgrn_w_in', j): g['w_in'], ('hgrn_w_o', j): g['w_o']}
            part['hgrn_g_norm'][j] = g['g_norm'][0]
            d_lower[j] = g['lb'][0]
        if kind != 'ffn':
            take_grads(carried(k // 2), got)
        mine.update(new)
        if k:
            (dx, dy), (d_gate[k - 1], d_scale[k], d_shift[k]), (d_lng[k - 1], d_lnb[k - 1]) = rowwise_bwd(
                "ln_mod_bwd", f_ln_mod, [_full(xs[k - 1]), _full(ys[k - 1])], [gate[k - 1], scale[k], shift[k]],
                [lng[k - 1], lnb[k - 1]], [(dx, [d]), (dh, [d])], [(F32, [0]), (BF16, [1])], ts=ROW_TILE, n_diff=2)
        else:
            (grad_x,), (d_scale[0], d_shift[0]), _ = rowwise_bwd(
                "mod_first_bwd", f_mod_with_x, [_full(x)], [scale[0], shift[0]], [], [(dh, [d]), (dx, [d])],
                [(F32, [0])], ts=ROW_TILE, n_diff=1)

    waiting = mixer_names(0)
    slabs = grad_items(waiting)
    slabs.append((jnp.stack(d_lower).reshape(n_hgrn, N_DEV, -1).transpose(1, 0, 2), "a2a"))
    for parts in (d_lng, d_lnb):
        full = jnp.stack([p[0] for p in parts]).reshape(depth, 2, N_DEV, d // N_DEV)
        slabs.append((full.transpose(2, 0, 1, 3), "a2a"))
    dmod = jnp.concatenate([jnp.stack(d_shift), jnp.stack(d_scale), jnp.stack(d_gate)], axis=-1)
    slabs.append((dmod.reshape(n_sub, bsz, N_DEV, cols).transpose(2, 0, 1, 3), "a2a"))
    small = ['mla_q_norm', 'mla_kv_norm', 'hgrn_g_norm']
    slabs += [(jnp.stack(part[n]), "gather") for n in small]
    got = exchange("scatter_last", slabs)
    take_grads(waiting, got)
    stacks = dict(zip(['hgrn_lb', 'ln_g', 'ln_b', 'dmod'] + small, got[len(waiting):]))

    dmod_all = stacks['dmod'].transpose(1, 0, 2, 3).reshape(n_sub, N_DEV * bsz, cols)
    g_ada_w, g_ada_b = ada_bwd(c_all, dmod_all)
    stacks['ada_w'] = g_ada_w.reshape((1,) + ada_w.shape)
    stacks['ada_b'] = g_ada_b.reshape((1,) + ada_b.shape)

    res = {}
    for n in WEIGHTS:
        if n == 'hgrn_lb':
            res[n] = adam_lb(stacks[n], W[n], M1[n], M2[n])
        elif n in big:
            res[n] = adam_layers("adam_" + n, recv[n], W[n], M1[n], M2[n])
        else:
            res[n] = adam("adam_" + n, stacks[n], W[n], M1[n], M2[n])
    return (loss, grad_x, *[res[n][0] for n in WEIGHTS], *[res[n][1] for n in WEIGHTS], *[res[n][2] for n in WEIGHTS],
            *[res[n][3] for n in WEIGHTS])
```

```python
import functools
import math

import numpy as np
import jax
import jax.numpy as jnp
from jax import lax
from jax.experimental import pallas as pl
from jax.experimental.pallas import tpu as pltpu

F32 = jnp.float32
BF16 = jnp.bfloat16

N_DEV = 8
LANES = 128
VMEM_LIMIT = 52 * 1024 * 1024

D_MODEL = 1024
DEPTH = 4
MLA_HEADS = 16
QK_NOPE = 64
QK_ROPE = 32
V_HEAD = 64
Q_LORA = 768
KV_LORA = 256
ROPE_THETA = 10000.0
HGRN_EXPAND = 128
HGRN_CHUNK = 128
HGRN_HEADS_PER_STEP = 4
D_FF = 2816
ALPHA = (2.0 * DEPTH) ** 0.25
LN_EPS = 1e-5
RMS_EPS = 1e-6
ADAM_LR = 0.001
ADAM_B1 = 0.9
ADAM_B2 = 0.999
ADAM_EPS = 1e-08
ADAM_WD = 0.01
ADAM_STEP = 10

ATTN_TQ = 512
ROW_TILE = 256

WEIGHTS = ['mla_w_in', 'mla_q_norm', 'mla_w_qb', 'mla_kv_norm', 'mla_w_kvb', 'mla_w_o', 'hgrn_lb', 'hgrn_w_in',
           'hgrn_g_norm', 'hgrn_w_o', 'ffn_w_in', 'ffn_w_out', 'ada_w', 'ada_b', 'ln_g', 'ln_b']
COL_SHARDED = ['mla_w_in', 'mla_w_qb', 'mla_w_kvb', 'hgrn_w_in', 'ffn_w_in']
ROW_SHARDED = ['mla_w_o', 'hgrn_w_o', 'ffn_w_out']


def _params(*sem):
    if sem:
        return pltpu.CompilerParams(dimension_semantics=sem, vmem_limit_bytes=VMEM_LIMIT)
    return pltpu.CompilerParams(vmem_limit_bytes=VMEM_LIMIT)


def _pick(n, cap):
    best = None
    for t in range(LANES, min(n, cap) + 1, LANES):
        if n % t == 0:
            best = t
    return best or n


def _pick_rows(n, cap):
    best = None
    for t in range(8, min(n, cap) + 1, 8):
        if n % t == 0:
            best = t
    return best or n


def matmul(name, a, b, *, ta=False, tb=False, out_dtype=F32, tm_cap=1024, tn_cap=1536, tk_cap=2048):
    parts = list(a) if isinstance(a, (list, tuple)) else [a]
    n_parts = len(parts)
    assert n_parts == 1 or not ta
    (kp, m) = parts[0].shape if ta else parts[0].shape[::-1]
    (n, k2) = b.shape if tb else b.shape[::-1]
    assert kp * n_parts == k2, (name, parts[0].shape, b.shape)
    tm, tn, tk = _pick(m, tm_cap), _pick(n, tn_cap), _pick(kp, tk_cap)
    nkp = kp // tk
    nk = nkp * n_parts
    dims = (((0 if ta else 1,), (1 if tb else 0,)), ((), ()))

    def prod(a_ref, b_ref):
        return lax.dot_general(a_ref[...].astype(BF16), b_ref[...].astype(BF16), dims, preferred_element_type=F32)

    if nk == 1:
        def body(a_ref, b_ref, o_ref):
            o_ref[...] = prod(a_ref, b_ref).astype(o_ref.dtype)
        scratch = []
    else:
        def body(*refs):
            a_refs, (b_ref, o_ref, acc_ref) = refs[:n_parts], refs[n_parts:]
            k = pl.program_id(2)

            @pl.when(k == 0)
            def _():
                acc_ref[...] = jnp.zeros_like(acc_ref)

            if n_parts == 1:
                acc_ref[...] += prod(a_refs[0], b_ref)
            else:
                for p in range(n_parts):
                    @pl.when((k >= p * nkp) & (k < (p + 1) * nkp))
                    def _(p=p):
                        acc_ref[...] += prod(a_refs[p], b_ref)

            @pl.when(k == nk - 1)
            def _():
                o_ref[...] = acc_ref[...].astype(o_ref.dtype)
        scratch = [pltpu.VMEM((tm, tn), F32)]

    if ta:
        a_specs = [pl.BlockSpec((tk, tm), lambda i, j, k: (k, i))]
    elif n_parts == 1:
        a_specs = [pl.BlockSpec((tm, tk), lambda i, j, k: (i, k))]
    else:
        a_specs = [pl.BlockSpec((tm, tk), lambda i, j, k, p=p: (i, jnp.clip(k - p * nkp, 0, nkp - 1))) for p in range(n_parts)]
    b_spec = pl.BlockSpec((tn, tk), lambda i, j, k: (j, k)) if tb else pl.BlockSpec((tk, tn), lambda i, j, k: (k, j))
    return pl.pallas_call(
        body, name=name, grid=(m // tm, n // tn, nk),
        in_specs=a_specs + [b_spec], out_specs=pl.BlockSpec((tm, tn), lambda i, j, k: (i, j)),
        out_shape=jax.ShapeDtypeStruct((m, n), out_dtype), scratch_shapes=scratch,
        compiler_params=_params("parallel", "parallel", "arbitrary"),
    )(*parts, b)


def mm3(name, a3, w, **kw):
    parts = list(a3) if isinstance(a3, (list, tuple)) else [a3]
    bsz, s = parts[0].shape[:2]
    flat = [p.reshape(bsz * s, p.shape[-1]) for p in parts]
    out = matmul(name, flat if len(flat) > 1 else flat[0], w, **kw)
    return out.reshape(bsz, s, out.shape[-1])


def ffn_in_act(h3, w_in):
    bsz, s, k = h3.shape
    m, dff = bsz * s, w_in.shape[1] // 2
    tm, tn = _pick(m, 512), _pick(dff, 1536)
    nj = dff // tn

    def body(h_ref, wg_ref, wu_ref, ug_ref, uu_ref, a_ref):
        hv = h_ref[...].astype(BF16)
        ug = _dg(hv, wg_ref[...].astype(BF16), 1, 0)
        uu = _dg(hv, wu_ref[...].astype(BF16), 1, 0)
        ug_ref[...] = ug.astype(ug_ref.dtype)
        uu_ref[...] = uu.astype(uu_ref.dtype)
        a_ref[...] = (jax.nn.silu(ug) * uu).astype(a_ref.dtype)

    out = pl.BlockSpec((tm, tn), lambda j, i: (i, j))
    res = pl.pallas_call(
        body, name="ffn_in_act", grid=(nj, m // tm),
        in_specs=[pl.BlockSpec((tm, k), lambda j, i: (i, 0)), pl.BlockSpec((k, tn), lambda j, i: (0, j)),
                  pl.BlockSpec((k, tn), lambda j, i: (0, nj + j))],
        out_specs=[out, out, out],
        out_shape=[jax.ShapeDtypeStruct((m, dff), BF16)] * 3,
        compiler_params=_params("arbitrary", "arbitrary"),
    )(h3.reshape(m, k), w_in, w_in)
    return [r.reshape(bsz, s, dff) for r in res]


def ffn_out_dx_act(dy3, w_out, ug, uu):
    bsz, s, d = dy3.shape
    m, dff = bsz * s, w_out.shape[0]
    tm, tn = _pick(m, 512), _pick(dff, 1536)

    def body(dy_ref, w_ref, ug_ref, uu_ref, dg_ref, du_ref):
        da = _dg(dy_ref[...].astype(BF16), w_ref[...].astype(BF16), 1, 1)
        _, vjp = jax.vjp(lambda gate, up: jax.nn.silu(gate) * up, ug_ref[...].astype(F32), uu_ref[...].astype(F32))
        dg, du = vjp(da)
        dg_ref[...] = dg.astype(dg_ref.dtype)
        du_ref[...] = du.astype(du_ref.dtype)

    blk = pl.BlockSpec((tm, tn), lambda j, i: (i, j))
    res = pl.pallas_call(
        body, name="ffn_out_dx_act", grid=(dff // tn, m // tm),
        in_specs=[pl.BlockSpec((tm, d), lambda j, i: (i, 0)), pl.BlockSpec((tn, d), lambda j, i: (j, 0)), blk, blk],
        out_specs=[blk, blk], out_shape=[jax.ShapeDtypeStruct((m, dff), BF16)] * 2,
        compiler_params=_params("arbitrary", "arbitrary"),
    )(dy3.reshape(m, d), w_out, ug.reshape(m, dff), uu.reshape(m, dff))
    return [r.reshape(bsz, s, dff) for r in res]


def wgrad(name, a3, g3):
    bsz, s, k = a3.shape
    return matmul(name, a3.reshape(bsz * s, k), g3.reshape(bsz * s, g3.shape[-1]), ta=True, out_dtype=BF16)


def _dg(a, b, ca, cb, **kw):
    return lax.dot_general(a, b, (((ca,), (cb,)), ((), ())), preferred_element_type=F32, **kw)


@functools.partial(jax.custom_vjp, nondiff_argnums=(2, 3))
def bdot(a, b, ca, cb):
    return _dg(a.astype(BF16), b.astype(BF16), ca, cb)


def _bdot_fwd(a, b, ca, cb):
    return bdot(a, b, ca, cb), (a, b)


def _bdot_bwd(ca, cb, res, g):
    a, b = res
    a16, b16, g16 = a.astype(BF16), b.astype(BF16), g.astype(BF16)
    if ca == 1:
        da = _dg(g16, b16, 1, 1 if cb == 0 else 0)
    else:
        da = _dg(b16, g16, 1 if cb == 0 else 0, 1)
    if cb == 0:
        db = _dg(a16, g16, 0 if ca == 1 else 1, 0)
    else:
        db = _dg(g16, a16, 0, 0 if ca == 1 else 1)
    return da, db


bdot.defvjp(_bdot_fwd, _bdot_bwd)


def hdot(a, b, ca=1, cb=0):
    return _dg(a, b, ca, cb, precision=lax.Precision.HIGHEST)


def _row_specs(rows, exs, globs, ts):
    specs = [pl.BlockSpec((1, ts, w), lambda b, s, j=j: (b, s, j)) for (_, j, w) in rows]
    specs += [pl.BlockSpec((1, 1, e.shape[-1]), lambda b, s: (b, 0, 0)) for e in exs]
    specs += [pl.BlockSpec((1, g.shape[-1]), lambda b, s: (0, 0)) for g in globs]
    return specs


def _store_pieces(o_ref, pieces, widths):
    off = 0
    for p, w in zip(pieces, widths):
        o_ref[0, :, off:off + w] = p.astype(o_ref.dtype)
        off += w


def _load_pieces(c_ref, widths):
    out, off = [], 0
    for w in widths:
        out.append(c_ref[0, :, off:off + w].astype(F32))
        off += w
    return out


def rowwise(name, f, rows, exs, globs, outs, *, ts, accs=()):
    bsz, s = rows[0][0].shape[:2]
    ts = min(ts, s)
    n_r, n_e, n_g, n_o = len(rows), len(exs), len(globs), len(outs)

    def body(*refs):
        rv = [r[0].astype(F32) for r in refs[:n_r]]
        ev = [e[0] for e in refs[n_r:n_r + n_e]]
        gv = [g[...] for g in refs[n_r + n_e:n_r + n_e + n_g]]
        o_refs = refs[n_r + n_e + n_g:n_r + n_e + n_g + n_o]
        a_refs = refs[n_r + n_e + n_g + n_o:]
        pieces, sums = f(rv, ev, gv)
        idx = 0
        for o_ref, (_, ws) in zip(o_refs, outs):
            _store_pieces(o_ref, pieces[idx:idx + len(ws)], ws)
            idx += len(ws)
        if accs:
            @pl.when((pl.program_id(0) == 0) & (pl.program_id(1) == 0))
            def _():
                for a_ref in a_refs:
                    a_ref[...] = jnp.zeros_like(a_ref)
            for a_ref, val in zip(a_refs, sums):
                a_ref[...] += val

    out_specs = [pl.BlockSpec((1, ts, sum(ws)), lambda b, s: (b, s, 0)) for (_, ws) in outs]
    out_specs += [pl.BlockSpec((1, w), lambda b, s: (0, 0)) for w in accs]
    out_shape = [jax.ShapeDtypeStruct((bsz, s, sum(ws)), dt) for (dt, ws) in outs]
    out_shape += [jax.ShapeDtypeStruct((1, w), F32) for w in accs]
    return pl.pallas_call(
        body, name=name, grid=(bsz, s // ts),
        in_specs=_row_specs(rows, exs, globs, ts), out_specs=out_specs, out_shape=out_shape,
        compiler_params=_params("arbitrary", "arbitrary"),
    )(*[r[0] for r in rows], *exs, *globs)


def rowwise_bwd(name, f, rows, exs, globs, cts, d_groups, *, ts, n_diff, unit_ct=0):
    bsz, s = rows[0][0].shape[:2]
    ts = min(ts, s)
    n_r, n_e, n_g, n_c = len(rows), len(exs), len(globs), len(cts)
    n_d = len(d_groups)

    def body(*refs):
        rv = [r[0].astype(F32) for r in refs[:n_r]]
        ev = [e[0] for e in refs[n_r:n_r + n_e]]
        gv = [g[...] for g in refs[n_r + n_e:n_r + n_e + n_g]]
        base = n_r + n_e + n_g
        c_refs = refs[base:base + n_c]
        d_refs = refs[base + n_c:base + n_c + n_d]
        de_refs = refs[base + n_c + n_d:base + n_c + n_d + n_e]
        dg_refs = refs[base + n_c + n_d + n_e:]
        fixed = rv[n_diff:]
        out, vjp = jax.vjp(lambda r, e, g: f(r + fixed, e, g), rv[:n_diff], ev, gv)
        ct = []
        for c_ref, (_, ws) in zip(c_refs, cts):
            ct += _load_pieces(c_ref, ws)
        ct += [jnp.ones_like(o) for o in out[len(ct):]]
        assert len(ct) == len(out) and len(out) - unit_ct == sum(len(ws) for _, ws in cts), name
        d_r, d_e, d_g = vjp(ct)
        for d_ref, (_, idxs) in zip(d_refs, d_groups):
            _store_pieces(d_ref, [d_r[i] for i in idxs], [rows[i][2] for i in idxs])
        first_s = pl.program_id(1) == 0
        if n_e:
            @pl.when(first_s)
            def _():
                for r in de_refs:
                    r[...] = jnp.zeros_like(r)
            for r, val in zip(de_refs, d_e):
                r[0] += val
        if n_g:
            @pl.when(first_s & (pl.program_id(0) == 0))
            def _():
                for r in dg_refs:
                    r[...] = jnp.zeros_like(r)
            for r, val in zip(dg_refs, d_g):
                r[...] += val

    in_specs = _row_specs(rows, exs, globs, ts)
    in_specs += [pl.BlockSpec((1, ts, sum(ws)), lambda b, s: (b, s, 0)) for (_, ws) in cts]
    out_specs = [pl.BlockSpec((1, ts, sum(rows[i][2] for i in idxs)), lambda b, s: (b, s, 0)) for (_, idxs) in d_groups]
    out_specs += [pl.BlockSpec((1, 1, e.shape[-1]), lambda b, s: (b, 0, 0)) for e in exs]
    out_specs += [pl.BlockSpec((1, g.shape[-1]), lambda b, s: (0, 0)) for g in globs]
    out_shape = [jax.ShapeDtypeStruct((bsz, s, sum(rows[i][2] for i in idxs)), dt) for (dt, idxs) in d_groups]
    out_shape += [jax.ShapeDtypeStruct(e.shape, F32) for e in exs]
    out_shape += [jax.ShapeDtypeStruct(g.shape, F32) for g in globs]
    res = pl.pallas_call(
        body, name=name, grid=(bsz, s // ts),
        in_specs=in_specs, out_specs=out_specs, out_shape=out_shape,
        compiler_params=_params("arbitrary", "arbitrary"),
    )(*[r[0] for r in rows], *exs, *globs, *[c[0] for c in cts])
    return res[:n_d], res[n_d:n_d + n_e], res[n_d + n_e:]


def _full(a):
    return (a, 0, a.shape[-1])


def _view(a, col, w):
    assert col % w == 0
    return (a, col // w, w)


def _layer_norm(z, g, b):
    mu = jnp.mean(z, -1, keepdims=True)
    var = jnp.mean(jnp.square(z - mu), -1, keepdims=True)
    return (z - mu) * lax.rsqrt(var + LN_EPS) * g + b


def _rms_norm(z, g):
    ms = jnp.mean(jnp.square(z), -1, keepdims=True)
    return z * lax.rsqrt(ms + RMS_EPS) * g


def f_mod(rv, ev, gv):
    (x,), (scale, shift) = rv, ev
    return [x * (1.0 + scale) + shift]


def f_mod_with_x(rv, ev, gv):
    return f_mod(rv, ev, gv) + [rv[0]]


def f_ln_mod(rv, ev, gv):
    (x, y), (gate, scale, shift), (g, b) = rv, ev, gv
    xn = _layer_norm(ALPHA * x + (1.0 + gate) * y, g, b)
    return [xn, xn * (1.0 + scale) + shift]


def f_ln_loss(rv, ev, gv):
    (x, y, target), (gate,), (g, b) = rv, ev, gv
    xn = _layer_norm(ALPHA * x + (1.0 + gate) * y, g, b)
    return [0.5 * jnp.mean(jnp.square(xn - target), -1, keepdims=True)]


def _head_spread(width):
    r2 = QK_ROPE // 2
    j = lax.broadcasted_iota(jnp.int32, (LANES, width), 0)
    col = lax.broadcasted_iota(jnp.int32, (LANES, width), 1) % r2
    return (j == col).astype(F32), (j == col + r2).astype(F32)


def f_mla_mid(rv, ev, gv):
    (q_lat, kv_lat, kr, cos, sin), (q_g, kv_g) = rv, gv
    e1, e2 = _head_spread(cos.shape[-1])
    k1, k2 = hdot(kr, e1), hdot(kr, e2)
    return [_rms_norm(q_lat, q_g), _rms_norm(kv_lat, kv_g), k1 * cos - k2 * sin, k1 * sin + k2 * cos]


def rope_tables(positions):
    bsz, s = positions.shape
    r2 = QK_ROPE // 2
    width = MLA_HEADS * r2
    inv = (ROPE_THETA ** (-np.arange(0, QK_ROPE, 2, dtype=np.float32) / QK_ROPE)).astype(np.float32)
    inv = jnp.asarray(np.tile(inv, MLA_HEADS)[None, :])
    ts = min(ROW_TILE, s)

    def body(p_ref, inv_ref, cos_ref, sin_ref):
        ang = p_ref[0].astype(F32) * inv_ref[...]
        cos_ref[0] = jnp.cos(ang)
        sin_ref[0] = jnp.sin(ang)

    spec = pl.BlockSpec((1, ts, width), lambda b, s: (b, s, 0))
    return pl.pallas_call(
        body, name="rope_tables", grid=(bsz, s // ts),
        in_specs=[pl.BlockSpec((1, ts, 1), lambda b, s: (b, s, 0)), pl.BlockSpec((1, width), lambda b, s: (0, 0))],
        out_specs=[spec, spec], out_shape=[jax.ShapeDtypeStruct((bsz, s, width), F32)] * 2,
        compiler_params=_params("arbitrary", "arbitrary"),
    )(positions[:, :, None], inv)


def _attn_probs(q, k, row0):
    scale = (QK_NOPE + QK_ROPE) ** -0.5
    s = _dg(q, k, 1, 1) * scale
    rows = row0 + lax.broadcasted_iota(jnp.int32, s.shape, 0)
    cols = lax.broadcasted_iota(jnp.int32, s.shape, 1)
    s = jnp.where(cols <= rows, s, jnp.finfo(F32).min)
    e = jnp.exp(s - jnp.max(s, -1, keepdims=True))
    return e / jnp.sum(e, -1, keepdims=True), scale


ATTN_PAIR = 2


def attn_fwd(q, k, v, ride=()):
    bsz, h, s, dq = q.shape
    dv = v.shape[-1]
    tq = min(ATTN_TQ, s)
    grid = (bsz, h // ATTN_PAIR, s // tq)
    rd = _Ride(ride)

    def body(*refs):
        (q_ref, k_ref, v_ref), srcs, (o_ref,), outs, sems = rd.split(refs, 3, 1)
        rd.run(srcs, outs, sems, grid)
        for i in range(grid[2]):
            @pl.when(pl.program_id(2) == i)
            def _(i=i):
                kend = (i + 1) * tq
                for e in range(ATTN_PAIR):
                    p, _ = _attn_probs(q_ref[0, e], k_ref[0, e, :kend, :], i * tq)
                    o_ref[0, :, e * dv:(e + 1) * dv] = _dg(p.astype(BF16), v_ref[0, e, :kend, :], 1, 0).astype(o_ref.dtype)

    res = pl.pallas_call(
        body, name="attn_fwd", grid=grid,
        in_specs=[pl.BlockSpec((1, ATTN_PAIR, tq, dq), lambda b, h, i: (b, h, i, 0)),
                  pl.BlockSpec((1, ATTN_PAIR, s, dq), lambda b, h, i: (b, h, 0, 0)),
                  pl.BlockSpec((1, ATTN_PAIR, s, dv), lambda b, h, i: (b, h, 0, 0))] + rd.in_specs,
        out_specs=[pl.BlockSpec((1, tq, ATTN_PAIR * dv), lambda b, h, i: (b, i, h))] + rd.out_specs,
        out_shape=[jax.ShapeDtypeStruct((bsz, s, h * dv), BF16)] + rd.out_shape, scratch_shapes=rd.scratch,
        compiler_params=_params("arbitrary", "arbitrary", "arbitrary"),
    )(q, k, v, *rd.srcs)
    return res[0], res[1:]


def attn_bwd(q, k, v, do, ride=()):
    bsz, h, s, dq = q.shape
    dv = v.shape[-1]
    tq = min(ATTN_TQ, s)
    grid = (bsz, h // ATTN_PAIR, s // tq)
    rd = _Ride(ride)

    def body(*refs):
        (q_ref, k_ref, v_ref, do_ref), srcs, (dq_ref, dk_ref, dv_ref), outs, sems = rd.split(refs, 4, 3)
        rd.run(srcs, outs, sems, grid)

        @pl.when(pl.program_id(2) == 0)
        def _():
            dk_ref[...] = jnp.zeros_like(dk_ref)
            dv_ref[...] = jnp.zeros_like(dv_ref)

        for i in range(grid[2]):
            @pl.when(pl.program_id(2) == i)
            def _(i=i):
                kend = (i + 1) * tq
                for e in range(ATTN_PAIR):
                    qv, kv, vv = q_ref[0, e], k_ref[0, e, :kend, :], v_ref[0, e, :kend, :]
                    p, scale = _attn_probs(qv, kv, i * tq)
                    do16 = do_ref[0, :, e * dv:(e + 1) * dv].astype(BF16)
                    dv_ref[0, e, :kend, :] += _dg(p.astype(BF16), do16, 0, 0)
                    dp = _dg(do16, vv, 1, 1)
                    ds = (p * (dp - jnp.sum(dp * p, -1, keepdims=True)) * scale).astype(BF16)
                    dq_ref[0, e] = _dg(ds, kv, 1, 0)
                    dk_ref[0, e, :kend, :] += _dg(ds, qv, 0, 0)

    res = pl.pallas_call(
        body, name="attn_bwd", grid=grid,
        in_specs=[pl.BlockSpec((1, ATTN_PAIR, tq, dq), lambda b, h, i: (b, h, i, 0)),
                  pl.BlockSpec((1, ATTN_PAIR, s, dq), lambda b, h, i: (b, h, 0, 0)),
                  pl.BlockSpec((1, ATTN_PAIR, s, dv), lambda b, h, i: (b, h, 0, 0)),
                  pl.BlockSpec((1, tq, ATTN_PAIR * dv), lambda b, h, i: (b, i, h))] + rd.in_specs,
        out_specs=[pl.BlockSpec((1, ATTN_PAIR, tq, dq), lambda b, h, i: (b, h, i, 0)),
                   pl.BlockSpec((1, ATTN_PAIR, s, dq), lambda b, h, i: (b, h, 0, 0)),
                   pl.BlockSpec((1, ATTN_PAIR, s, dv), lambda b, h, i: (b, h, 0, 0))] + rd.out_specs,
        out_shape=[jax.ShapeDtypeStruct((bsz, h, s, dq), F32), jax.ShapeDtypeStruct((bsz, h, s, dq), F32),
                   jax.ShapeDtypeStruct((bsz, h, s, dv), F32)] + rd.out_shape, scratch_shapes=rd.scratch,
        compiler_params=_params("arbitrary", "arbitrary", "arbitrary"),
    )(q, k, v, do, *rd.srcs)
    return res[0], res[1], res[2], res[3:]


def mla_heads(q, kv, krt, cos, sin):
    bsz, s, _ = q.shape
    nh, n, r2, vd = MLA_HEADS, QK_NOPE, QK_ROPE // 2, V_HEAD
    ts = min(ROW_TILE, s)

    def body(q_ref, kv_ref, kr_ref, cos_ref, sin_ref, qh_ref, kh_ref, vh_ref):
        cos, sin = cos_ref[0], sin_ref[0]
        qv, kvv, kr = q_ref[0], kv_ref[0], kr_ref[0].astype(F32)
        q1, q2 = qv[:, nh * n:nh * (n + r2)], qv[:, nh * (n + r2):]
        qr = [q1 * cos - q2 * sin, q1 * sin + q2 * cos]
        for h in range(nh):
            qh_ref[0, h, :, :n] = qv[:, h * n:(h + 1) * n].astype(BF16)
            kh_ref[0, h, :, :n] = kvv[:, h * n:(h + 1) * n].astype(BF16)
            vh_ref[0, h] = kvv[:, nh * n + h * vd:nh * n + (h + 1) * vd].astype(BF16)
            for j in range(2):
                qh_ref[0, h, :, n + j * r2:n + (j + 1) * r2] = qr[j][:, h * r2:(h + 1) * r2].astype(BF16)
                kh_ref[0, h, :, n + j * r2:n + (j + 1) * r2] = kr[:, (j * nh + h) * r2:(j * nh + h + 1) * r2].astype(BF16)

    def row(a):
        return pl.BlockSpec((1, ts, a.shape[-1]), lambda b, i: (b, i, 0))

    def heads(w):
        return pl.BlockSpec((1, nh, ts, w), lambda b, i: (b, 0, i, 0))

    return pl.pallas_call(
        body, name="mla_heads", grid=(bsz, s // ts), in_specs=[row(q), row(kv), row(krt), row(cos), row(sin)],
        out_specs=[heads(n + 2 * r2), heads(n + 2 * r2), heads(vd)],
        out_shape=[jax.ShapeDtypeStruct((bsz, nh, s, n + 2 * r2), BF16)] * 2 + [jax.ShapeDtypeStruct((bsz, nh, s, vd), BF16)],
        compiler_params=_params("arbitrary", "arbitrary"),
    )(q, kv, krt, cos, sin)


def mla_heads_bwd(dqh, dkh, dvh, cos, sin):
    bsz, nh, s, _ = dqh.shape
    n, r2, vd = QK_NOPE, QK_ROPE // 2, V_HEAD
    ts = min(ROW_TILE, s)

    def body(dq_ref, dk_ref, dv_ref, cos_ref, sin_ref, oq_ref, okv_ref, okr_ref, tq_acc, tkv_acc, rot):
        for h in range(nh):
            tq_acc[:, h * n:(h + 1) * n] = dq_ref[0, h, :, :n]
            tkv_acc[:, h * n:(h + 1) * n] = dk_ref[0, h, :, :n]
            tkv_acc[:, nh * n + h * vd:nh * n + (h + 1) * vd] = dv_ref[0, h]
            for j in range(2):
                rot[j, :, h * r2:(h + 1) * r2] = dq_ref[0, h, :, n + j * r2:n + (j + 1) * r2]
                okr_ref[0, :, (j * nh + h) * r2:(j * nh + h + 1) * r2] = dk_ref[0, h, :, n + j * r2:n + (j + 1) * r2]
        cos, sin = cos_ref[0], sin_ref[0]
        d1, d2 = rot[0], rot[1]
        oq_ref[0, :, :nh * n] = tq_acc[...].astype(BF16)
        oq_ref[0, :, nh * n:nh * (n + r2)] = (d1 * cos + d2 * sin).astype(BF16)
        oq_ref[0, :, nh * (n + r2):] = (d2 * cos - d1 * sin).astype(BF16)
        okv_ref[0] = tkv_acc[...].astype(BF16)

    def row(w):
        return pl.BlockSpec((1, ts, w), lambda b, i: (b, i, 0))

    def heads(w):
        return pl.BlockSpec((1, nh, ts, w), lambda b, i: (b, 0, i, 0))

    return pl.pallas_call(
        body, name="mla_heads_bwd", grid=(bsz, s // ts),
        in_specs=[heads(n + 2 * r2), heads(n + 2 * r2), heads(vd), row(nh * r2), row(nh * r2)],
        out_specs=[row(nh * (n + 2 * r2)), row(nh * (n + vd)), row(2 * nh * r2)],
        out_shape=[jax.ShapeDtypeStruct((bsz, s, nh * (n + 2 * r2)), BF16), jax.ShapeDtypeStruct((bsz, s, nh * (n + vd)), BF16),
                   jax.ShapeDtypeStruct((bsz, s, 2 * nh * r2), F32)],
        scratch_shapes=[pltpu.VMEM((ts, nh * n), F32), pltpu.VMEM((ts, nh * (n + vd)), F32), pltpu.VMEM((2, ts, nh * r2), F32)],
        compiler_params=_params("arbitrary", "arbitrary"),
    )(dqh, dkh, dvh, cos, sin)


def _hgrn_tables(c):
    levels = c.bit_length() - 1
    assert 1 << levels == c
    r = np.arange(c)
    prefix, sign, mask = [r[:, None] >= r[None, :]], [], []
    for l in range(levels):
        ref = ((r >> (l + 1)) << (l + 1)) + (1 << l) - 1
        lower = ((r >> l) & 1) == 1
        prefix.append(r[None, :] <= ref[:, None])
        sign.append(np.broadcast_to(np.where(lower, 1.0, -1.0)[:, None], (c, LANES)))
        mask.append((((r[:, None] ^ r[None, :]) >> l) == 1) & lower[:, None])
    return (jnp.asarray(np.concatenate(prefix, 0), BF16), jnp.asarray(np.stack(sign), F32),
            jnp.asarray(np.stack(mask), F32))


def _const_specs(tables):
    return [pl.BlockSpec(t.shape, lambda b, h, i, nd=t.ndim: (0,) * nd) for t in tables]


def _split3(x):
    hi = x.astype(BF16)
    rest = x - hi.astype(F32)
    mid = rest.astype(BF16)
    return hi, mid, (rest - mid.astype(F32)).astype(BF16)


@functools.partial(jax.custom_vjp, nondiff_argnums=(2,))
def prefix_sums(p, g, n):
    k = g.shape[1]
    r = _dg(p, jnp.concatenate(_split3(g), axis=1), 1, 0)
    r = r[:, :k] + r[:, k:2 * k] + r[:, 2 * k:]
    c = r.shape[0] // n
    return tuple(r[i * c:(i + 1) * c] for i in range(n))


def _prefix_fwd(p, g, n):
    return prefix_sums(p, g, n), p


def _prefix_bwd(n, p, ct):
    ct = jnp.concatenate(ct, axis=0)
    k = ct.shape[1]
    r = _dg(p, jnp.concatenate(_split3(ct), axis=1), 0, 0)
    return jnp.zeros_like(p), r[:, :k] + r[:, k:2 * k] + r[:, 2 * k:]


prefix_sums.defvjp(_prefix_fwd, _prefix_bwd)


def _hgrn_chunk(q, g, k, v, st0, prefix, sign, mask):
    levels = len(mask)
    pre = prefix_sums(prefix, g, levels + 1)
    b = pre[0]
    o = bdot(q * jnp.exp(b), st0, 1, 1)
    att = None
    for l in range(levels):
        e = jnp.exp((b - pre[l + 1]) * sign[l])
        a = bdot(q * e, k * e, 1, 1) * mask[l]
        att = a if att is None else att + a
    o = o + bdot(att, v, 1, 0) + jnp.sum(q * k, -1, keepdims=True) * v
    total = jnp.sum(g, 0, keepdims=True)
    st1 = st0 * jnp.exp(total) + bdot(v, k * jnp.exp(total - b), 0, 0)
    return o, st1


def _hgrn_step(q_raw, fx, v, g_raw, st0, lb, gn, prefix, sign, mask):
    f = lb + (1.0 - lb) * jax.nn.sigmoid(fx)
    o, st1 = _hgrn_chunk(jax.nn.silu(q_raw), jnp.log(f), 1.0 - f, v, st0, prefix, sign, mask)
    return _rms_norm(o, gn) * jax.nn.silu(g_raw), st1


def _hgrn_layout(proj):
    bsz, s, width = proj.shape
    kd = HGRN_EXPAND
    c = min(HGRN_CHUNK, s)
    nh = width // (4 * kd)
    hp = math.gcd(nh, HGRN_HEADS_PER_STEP)
    return bsz, s, kd, c, nh, s // c, hp


def hgrn_fwd(proj, lb, gn, ride=()):
    bsz, s, kd, c, nh, nc, hp = _hgrn_layout(proj)
    groups = nh // hp
    tables = _hgrn_tables(c)
    levels = tables[2].shape[0]
    grid = (groups, bsz, nc)
    rd = _Ride(ride)

    def body(*refs):
        ((q_ref, f_ref, v_ref, g_ref, lb_ref, gn_ref, p_ref, sg_ref, mk_ref), srcs, (z_ref, st_ref), outs,
         scratch) = rd.split(refs, 9, 2)
        state = scratch[0]
        rd.run(srcs, outs, scratch, grid)

        @pl.when(pl.program_id(2) == 0)
        def _():
            state[...] = jnp.zeros_like(state)

        prefix, sign, mask = p_ref[...], [sg_ref[l] for l in range(levels)], [mk_ref[l] for l in range(levels)]
        for j in range(hp):
            cols = slice(j * kd, (j + 1) * kd)
            st0 = state[j]
            st_ref[0, j, 0] = st0
            z, st1 = _hgrn_step(q_ref[0, :, cols], f_ref[0, :, cols], v_ref[0, :, cols], g_ref[0, :, cols], st0,
                                lb_ref[:, cols], gn_ref[...], prefix, sign, mask)
            z_ref[0, :, cols] = z.astype(z_ref.dtype)
            state[j] = st1

    def part(k):
        return pl.BlockSpec((1, c, hp * kd), lambda h, b, i: (b, i, k * groups + h))

    res = pl.pallas_call(
        body, name="hgrn_fwd", grid=grid,
        in_specs=[part(0), part(1), part(2), part(3), pl.BlockSpec((1, hp * kd), lambda h, b, i: (0, h)),
                  pl.BlockSpec((1, kd), lambda h, b, i: (0, 0))] + _const_specs(tables) + rd.in_specs,
        out_specs=[part(0), pl.BlockSpec((1, hp, 1, kd, kd), lambda h, b, i: (b, h, i, 0, 0))] + rd.out_specs,
        out_shape=[jax.ShapeDtypeStruct((bsz, s, nh * kd), BF16), jax.ShapeDtypeStruct((bsz, nh, nc, kd, kd), F32)] + rd.out_shape,
        scratch_shapes=[pltpu.VMEM((hp, kd, kd), F32)] + rd.scratch,
        compiler_params=_params("arbitrary", "arbitrary", "arbitrary"),
    )(proj, proj, proj, proj, lb, gn, *tables, *rd.srcs)
    return res[0], res[1], res[2:]


def hgrn_bwd(proj, states, dz, lb, gn, ride=()):
    bsz, s, kd, c, nh, nc, hp = _hgrn_layout(proj)
    groups = nh // hp
    tables = _hgrn_tables(c)
    levels = tables[2].shape[0]
    grid = (groups, bsz, nc)
    rd = _Ride(ride)

    def body(*refs):
        ((q_ref, f_ref, v_ref, g_ref, st_ref, dz_ref, lb_ref, gn_ref, p_ref, sg_ref, mk_ref), srcs,
         (dq_ref, df_ref, dv_ref, dg_ref, dlb_ref, dgn_ref), outs, scratch) = rd.split(refs, 11, 6)
        dstate = scratch[0]
        rd.run(srcs, outs, scratch, grid)
        first_of_group = (pl.program_id(1) == 0) & (pl.program_id(2) == 0)

        @pl.when(pl.program_id(2) == 0)
        def _():
            dstate[...] = jnp.zeros_like(dstate)

        @pl.when(first_of_group)
        def _():
            dlb_ref[...] = jnp.zeros_like(dlb_ref)

        @pl.when(first_of_group & (pl.program_id(0) == 0))
        def _():
            dgn_ref[...] = jnp.zeros_like(dgn_ref)

        prefix, sign, mask = p_ref[...], [sg_ref[l] for l in range(levels)], [mk_ref[l] for l in range(levels)]
        for j in range(hp):
            cols = slice(j * kd, (j + 1) * kd)
            _, vjp = jax.vjp(lambda q, f, v, g, st, lb, gn: _hgrn_step(q, f, v, g, st, lb, gn, prefix, sign, mask),
                             q_ref[0, :, cols], f_ref[0, :, cols], v_ref[0, :, cols], g_ref[0, :, cols], st_ref[0, j, 0],
                             lb_ref[:, cols], gn_ref[...])
            dq, df, dv, dg, dst, dlb, dgn = vjp((dz_ref[0, :, cols], dstate[j]))
            dq_ref[0, :, cols] = dq.astype(dq_ref.dtype)
            df_ref[0, :, cols] = df.astype(df_ref.dtype)
            dv_ref[0, :, cols] = dv.astype(dv_ref.dtype)
            dg_ref[0, :, cols] = dg.astype(dg_ref.dtype)
            dlb_ref[:, cols] += dlb
            dgn_ref[...] += dgn
            dstate[j] = dst

    def part(k):
        return pl.BlockSpec((1, c, hp * kd), lambda h, b, i: (b, nc - 1 - i, k * groups + h))

    shape = jax.ShapeDtypeStruct((bsz, s, nh * kd), BF16)
    lb_spec = pl.BlockSpec((1, hp * kd), lambda h, b, i: (0, h))
    gn_spec = pl.BlockSpec((1, kd), lambda h, b, i: (0, 0))
    res = pl.pallas_call(
        body, name="hgrn_bwd", grid=grid,
        in_specs=[part(0), part(1), part(2), part(3), pl.BlockSpec((1, hp, 1, kd, kd), lambda h, b, i: (b, h, nc - 1 - i, 0, 0)),
                  part(0), lb_spec, gn_spec] + _const_specs(tables) + rd.in_specs,
        out_specs=[part(0)] * 4 + [lb_spec, gn_spec] + rd.out_specs,
        out_shape=[shape] * 4 + [jax.ShapeDtypeStruct(lb.shape, F32), jax.ShapeDtypeStruct(gn.shape, F32)] + rd.out_shape,
        scratch_shapes=[pltpu.VMEM((hp, kd, kd), F32)] + rd.scratch,
        compiler_params=_params("arbitrary", "arbitrary", "arbitrary"),
    )(proj, proj, proj, proj, states, dz, lb, gn, *tables, *rd.srcs)
    return list(res[:4]), res[4], res[5], res[6:]


def cast_bf16(name, w):
    blk = pl.BlockSpec((1,) + w.shape[1:], lambda l: (l, 0, 0))

    def body(w_ref, o_ref):
        o_ref[...] = w_ref[...].astype(BF16)

    return pl.pallas_call(body, name=name, grid=(w.shape[0],), in_specs=[blk], out_specs=blk,
                          out_shape=jax.ShapeDtypeStruct(w.shape, BF16), compiler_params=_params("arbitrary"))(w)


def _lower_bounds(rows):
    m = functools.reduce(jnp.maximum, rows)
    e = [jnp.exp(r - m) for r in rows]
    z = functools.reduce(lambda a, b: a + b, e)
    soft = [x / z for x in e]
    out, run = [], jnp.zeros_like(rows[0])
    for sft in soft:
        run = run + sft
        out.append(run - soft[0])
    return out


def lower_bounds(lb):
    n = lb.shape[0]

    def body(lb_ref, o_ref):
        for i, r in enumerate(_lower_bounds([lb_ref[i:i + 1, :] for i in range(n)])):
            o_ref[i:i + 1, :] = r

    return pl.pallas_call(body, name="lower_bounds", out_shape=jax.ShapeDtypeStruct(lb.shape, F32),
                          compiler_params=_params())(lb)


def ada_fwd(c_all, ada_w, ada_b):
    nl, ns, d, cols = ada_w.shape
    n_ex = c_all.shape[0]

    def body(c_ref, w_ref, b_ref, o_ref):
        a = jax.nn.silu(c_ref[...]).astype(BF16)
        o_ref[0] = _dg(a, w_ref[0].astype(BF16), 1, 0) + b_ref[0]

    return pl.pallas_call(
        body, name="ada_fwd", grid=(nl * ns,),
        in_specs=[pl.BlockSpec((n_ex, d), lambda i: (0, 0)), pl.BlockSpec((1, d, cols), lambda i: (i, 0, 0)),
                  pl.BlockSpec((1, 1, cols), lambda i: (i, 0, 0))],
        out_specs=pl.BlockSpec((1, n_ex, cols), lambda i: (i, 0, 0)),
        out_shape=jax.ShapeDtypeStruct((nl * ns, n_ex, cols), F32), compiler_params=_params("arbitrary"),
    )(c_all, ada_w.reshape(nl * ns, d, cols), ada_b.reshape(nl * ns, 1, cols))


def ada_bwd(c_all, dmod):
    n, n_ex, cols = dmod.shape
    d = c_all.shape[1]

    def body(c_ref, g_ref, dw_ref, db_ref):
        a = jax.nn.silu(c_ref[...]).astype(BF16)
        g = g_ref[0]
        dw_ref[0] = _dg(a, g.astype(BF16), 0, 0)
        db_ref[0] = jnp.sum(g, 0, keepdims=True)

    return pl.pallas_call(
        body, name="ada_bwd", grid=(n,),
        in_specs=[pl.BlockSpec((n_ex, d), lambda i: (0, 0)), pl.BlockSpec((1, n_ex, cols), lambda i: (i, 0, 0))],
        out_specs=[pl.BlockSpec((1, d, cols), lambda i: (i, 0, 0)), pl.BlockSpec((1, 1, cols), lambda i: (i, 0, 0))],
        out_shape=[jax.ShapeDtypeStruct((n, d, cols), F32), jax.ShapeDtypeStruct((n, 1, cols), F32)],
        compiler_params=_params("arbitrary"),
    )(c_all, dmod)


def _adam_math(g, w, m, v):
    m = ADAM_B1 * m + (1.0 - ADAM_B1) * g
    v = ADAM_B2 * v + (1.0 - ADAM_B2) * jnp.square(g)
    m_hat = m / (1.0 - ADAM_B1 ** ADAM_STEP)
    v_hat = v / (1.0 - ADAM_B2 ** ADAM_STEP)
    delta = -ADAM_LR * (m_hat / (jnp.sqrt(v_hat) + ADAM_EPS) + ADAM_WD * w)
    return delta, m, v


def adam(name, gstack, w, m, v):
    shape = w.shape
    n, cols = gstack.shape[0], shape[-1]
    rows = math.prod(shape[:-1])
    tr = _pick_rows(rows, max(8, (2 * 1024 * 1024) // (4 * cols * n)))

    def body(g_ref, w_ref, m_ref, v_ref, go_ref, d_ref, mo_ref, vo_ref):
        g = g_ref[0].astype(F32)
        for i in range(1, n):
            g = g + g_ref[i].astype(F32)
        delta, m1, v1 = _adam_math(g, w_ref[...], m_ref[...], v_ref[...])
        go_ref[...] = g
        d_ref[...] = delta
        mo_ref[...] = m1
        vo_ref[...] = v1

    blk = pl.BlockSpec((tr, cols), lambda i: (i, 0))
    out = pl.pallas_call(
        body, name=name, grid=(rows // tr,),
        in_specs=[pl.BlockSpec((n, tr, cols), lambda i: (0, i, 0)), blk, blk, blk],
        out_specs=[blk] * 4, out_shape=[jax.ShapeDtypeStruct((rows, cols), F32)] * 4,
        compiler_params=_params("arbitrary"),
    )(gstack.reshape(n, rows, cols), w.reshape(rows, cols), m.reshape(rows, cols), v.reshape(rows, cols))
    return [o.reshape(shape) for o in out]


def adam_layers(name, gs, w, m, v):
    shape = w.shape
    nl, n, cols = len(gs), gs[0].shape[0], shape[-1]
    rows = math.prod(shape[1:-1])
    tr = _pick_rows(rows, max(16, (2 * 1024 * 1024) // (4 * cols * n)))
    nt = rows // tr

    def body(*refs):
        g_refs, (w_ref, m_ref, v_ref, go_ref, d_ref, mo_ref, vo_ref) = refs[:nl], refs[nl:]
        for j in range(nl):
            @pl.when(pl.program_id(0) == j)
            def _(j=j):
                g = g_refs[j][0].astype(F32)
                for i in range(1, n):
                    g = g + g_refs[j][i].astype(F32)
                delta, m1, v1 = _adam_math(g, w_ref[...], m_ref[...], v_ref[...])
                go_ref[...] = g
                d_ref[...] = delta
                mo_ref[...] = m1
                vo_ref[...] = v1

    def g_spec(j):
        return pl.BlockSpec((n, tr, cols), lambda l, i: (0, jnp.where(l == j, i, jnp.where(l < j, 0, nt - 1)), 0))

    blk = pl.BlockSpec((tr, cols), lambda l, i: (l * nt + i, 0))
    out = pl.pallas_call(
        body, name=name, grid=(nl, nt),
        in_specs=[g_spec(j) for j in range(nl)] + [blk, blk, blk],
        out_specs=[blk] * 4, out_shape=[jax.ShapeDtypeStruct((nl * rows, cols), F32)] * 4,
        compiler_params=_params("arbitrary", "arbitrary"),
    )(*[g.reshape(n, rows, cols) for g in gs], w.reshape(nl * rows, cols), m.reshape(nl * rows, cols), v.reshape(nl * rows, cols))
    return [o.reshape(shape) for o in out]


def adam_lb(gstack, lb, m, v):
    n, nl = gstack.shape[0], lb.shape[0]

    def body(g_ref, w_ref, m_ref, v_ref, go_ref, d_ref, mo_ref, vo_ref):
        rows = [w_ref[i:i + 1, :] for i in range(nl)]
        ct = []
        for i in range(nl):
            g = g_ref[0, i:i + 1, :]
            for j in range(1, n):
                g = g + g_ref[j, i:i + 1, :]
            ct.append(g)
        _, vjp = jax.vjp(_lower_bounds, rows)
        (grads,) = vjp(ct)
        for i in range(nl):
            delta, m1, v1 = _adam_math(grads[i], rows[i], m_ref[i:i + 1, :], v_ref[i:i + 1, :])
            go_ref[i:i + 1, :] = grads[i]
            d_ref[i:i + 1, :] = delta
            mo_ref[i:i + 1, :] = m1
            vo_ref[i:i + 1, :] = v1

    return pl.pallas_call(body, name="adam_hgrn_lb", out_shape=[jax.ShapeDtypeStruct(lb.shape, F32)] * 4,
                          compiler_params=_params())(gstack, lb, m, v)


class _Ride:
    def __init__(self, items):
        self.items = list(items)
        n = len(self.items)
        self.srcs = [src for src, _ in self.items]
        self.in_specs = [pl.BlockSpec(memory_space=pl.ANY)] * n
        self.out_specs = [pl.BlockSpec(memory_space=pl.ANY)] * n
        self.out_shape = [jax.ShapeDtypeStruct(((N_DEV,) + s.shape) if mode == "gather" else s.shape, s.dtype)
                          for s, mode in self.items]
        self.scratch = [pltpu.SemaphoreType.DMA((n, N_DEV - 1)), pltpu.SemaphoreType.DMA((n, N_DEV - 1)),
                        pltpu.SemaphoreType.DMA((n,))] if n else []

    def split(self, refs, n_in, n_out):
        n = len(self.items)
        a, b = n_in + n, n_in + 2 * n + n_out
        return refs[:n_in], refs[n_in:a], refs[a:a + n_out], refs[a + n_out:b], refs[b:]

    def _copies(self, srcs, outs, sems):
        send_sems, recv_sems, local_sems = sems
        x, y, c = lax.axis_index("x"), lax.axis_index("y"), lax.axis_index("c")
        me = 4 * x + 2 * y + c
        copies = []
        for i, (_, mode) in enumerate(self.items):
            mine = srcs[i] if mode == "gather" else srcs[i].at[me]
            copies.append(pltpu.make_async_copy(mine, outs[i].at[me], local_sems.at[i]))
            for p in range(1, N_DEV):
                px = 1 - x if p & 4 else x
                py = 1 - y if p & 2 else y
                pc = 1 - c if p & 1 else c
                part = srcs[i] if mode == "gather" else srcs[i].at[4 * px + 2 * py + pc]
                copies.append(pltpu.make_async_remote_copy(
                    src_ref=part, dst_ref=outs[i].at[me], send_sem=send_sems.at[i, p - 1], recv_sem=recv_sems.at[i, p - 1],
                    device_id=(px, py, pc), device_id_type=pl.DeviceIdType.MESH))
        return copies

    def run(self, srcs, outs, scratch, grid=()):
        if not self.items:
            return
        sems = scratch[len(scratch) - 3:]
        if not grid:
            copies = self._copies(srcs, outs, sems)
            for cp in copies:
                cp.start()
            for cp in copies:
                cp.wait()
            return
        ids = [pl.program_id(a) for a in range(len(grid))]
        first = functools.reduce(lambda a, b: a & b, [i == 0 for i in ids])
        last = functools.reduce(lambda a, b: a & b, [i == g - 1 for i, g in zip(ids, grid)])

        @pl.when(first)
        def _():
            for cp in self._copies(srcs, outs, sems):
                cp.start()

        @pl.when(last)
        def _():
            for cp in self._copies(srcs, outs, sems):
                cp.wait()


def exchange(name, items):
    rd = _Ride(items)

    def body(*refs):
        _, srcs, _, outs, scratch = rd.split(refs, 0, 0)
        rd.run(srcs, outs, scratch)

    return pl.pallas_call(body, name=name, in_specs=rd.in_specs, out_specs=rd.out_specs, out_shape=rd.out_shape,
                          scratch_shapes=rd.scratch)(*rd.srcs)


def _from_gather(name, g):
    if name in COL_SHARDED:
        _, k, n = g.shape
        return g.transpose(1, 0, 2).reshape(k, N_DEV * n)
    return g.reshape(-1, g.shape[-1])


def _to_slabs(name, w):
    if isinstance(w, tuple):
        per = N_DEV // len(w)
        return jnp.concatenate([p.reshape(p.shape[0], per, p.shape[1] // per).transpose(1, 0, 2) for p in w], axis=0)
    k, n = w.shape
    if name in COL_SHARDED:
        return w.reshape(k, N_DEV, n // N_DEV).transpose(1, 0, 2)
    return w.reshape(N_DEV, k // N_DEV, n)


def _w_in_internal(w):
    return jnp.pad(w, ((0, 0), (0, LANES - QK_ROPE)))


def _qb_internal(w, inverse=False):
    h, n, r2 = MLA_HEADS, QK_NOPE, QK_ROPE // 2
    lead = w.shape[:-1]
    if not inverse:
        w = w.reshape(lead + (h, n + 2 * r2))
        parts = [w[..., :n], w[..., n:n + r2], w[..., n + r2:]]
        return jnp.concatenate([p.reshape(lead + (-1,)) for p in parts], axis=-1)
    parts = [w[..., :h * n].reshape(lead + (h, n)), w[..., h * n:h * (n + r2)].reshape(lead + (h, r2)),
             w[..., h * (n + r2):].reshape(lead + (h, r2))]
    return jnp.concatenate(parts, axis=-1).reshape(lead + (-1,))


def _kvb_internal(w, inverse=False):
    h, n, vd = MLA_HEADS, QK_NOPE, V_HEAD
    lead = w.shape[:-1]
    if not inverse:
        w = w.reshape(lead + (h, n + vd))
        return jnp.concatenate([w[..., :n].reshape(lead + (-1,)), w[..., n:].reshape(lead + (-1,))], axis=-1)
    parts = [w[..., :h * n].reshape(lead + (h, n)), w[..., h * n:].reshape(lead + (h, vd))]
    return jnp.concatenate(parts, axis=-1).reshape(lead + (-1,))


def _mla_forward(h, w, tabs, ride=()):
    cos, sin = tabs
    r2 = MLA_HEADS * (QK_ROPE // 2)
    proj = mm3("mla_proj", h, w['w_in'])
    qn, kvn, krt = rowwise(
        "mla_mid", lambda rv, ev, gv: (f_mla_mid(rv, ev, gv), []),
        [_view(proj, 0, Q_LORA), _view(proj, Q_LORA, KV_LORA), _view(proj, Q_LORA + KV_LORA, LANES), _full(cos), _full(sin)],
        [], [w['q_norm'], w['kv_norm']], [(BF16, [Q_LORA]), (BF16, [KV_LORA]), (BF16, [r2, r2])], ts=ROW_TILE)
    q = mm3("mla_q", qn, w['w_qb'])
    kv = mm3("mla_kv", kvn, w['w_kvb'])
    qh, kh, vh = mla_heads(q, kv, krt, cos, sin)
    o, got = attn_fwd(qh, kh, vh, ride)
    y = mm3("mla_out", o, w['w_o'])
    return y, dict(h=h, proj=proj, qn=qn, kvn=kvn, qh=qh, kh=kh, vh=vh, o=o), got


def _mla_backward(dy, sv, w, tabs, ride=()):
    cos, sin = tabs
    r2 = MLA_HEADS * (QK_ROPE // 2)
    g = {}
    g['w_o'] = wgrad("mla_out_dw", sv['o'], dy)
    do = mm3("mla_out_dx", dy, w['w_o'], tb=True)
    dqh, dkh, dvh, got = attn_bwd(sv['qh'], sv['kh'], sv['vh'], do, ride)
    dq, dkv, dkrt = mla_heads_bwd(dqh, dkh, dvh, cos, sin)
    g['w_qb'] = wgrad("mla_q_dw", sv['qn'], dq)
    g['w_kvb'] = wgrad("mla_kv_dw", sv['kvn'], dkv)
    dqn = mm3("mla_q_dx", dq, w['w_qb'], tb=True)
    dkvn = mm3("mla_kv_dx", dkv, w['w_kvb'], tb=True)
    proj = sv['proj']
    (dproj,), _, (g['q_norm'], g['kv_norm']) = rowwise_bwd(
        "mla_mid_bwd", f_mla_mid,
        [_view(proj, 0, Q_LORA), _view(proj, Q_LORA, KV_LORA), _view(proj, Q_LORA + KV_LORA, LANES), _full(cos), _full(sin)],
        [], [w['q_norm'], w['kv_norm']], [(dqn, [Q_LORA]), (dkvn, [KV_LORA]), (dkrt, [r2, r2])],
        [(BF16, [0, 1, 2])], ts=ROW_TILE, n_diff=3)
    g['w_in'] = wgrad("mla_proj_dw", sv['h'], dproj)
    dh = mm3("mla_proj_dx", dproj, w['w_in'], tb=True)
    return dh, g, got


def _hgrn_forward(h, w, ride=()):
    proj = mm3("hgrn_proj", h, w['w_in'])
    z, states, got = hgrn_fwd(proj, w['lb'], w['g_norm'], ride)
    y = mm3("hgrn_out", z, w['w_o'])
    return y, dict(h=h, proj=proj, states=states, z=z), got


def _hgrn_backward(dy, sv, w, ride=()):
    g = {}
    g['w_o'] = wgrad("hgrn_out_dw", sv['z'], dy)
    dz = mm3("hgrn_out_dx", dy, w['w_o'], tb=True)
    dparts, g['lb'], g['g_norm'], got = hgrn_bwd(sv['proj'], sv['states'], dz, w['lb'], w['g_norm'], ride)
    g['w_in'] = tuple(wgrad("hgrn_proj_dw", sv['h'], p) for p in dparts)
    dh = mm3("hgrn_proj_dx", dparts, w['w_in'], tb=True)
    return dh, g, got


def _ffn_forward(h, w):
    ug, uu, a = ffn_in_act(h, w['w_in'])
    y = mm3("ffn_out", a, w['w_out'])
    return y, dict(h=h, ug=ug, uu=uu, a=a)


def _ffn_backward(dy, sv, w):
    g = {}
    g['w_out'] = wgrad("ffn_out_dw", sv['a'], dy)
    dug, duu = ffn_out_dx_act(dy, w['w_out'], sv['ug'], sv['uu'])
    g['w_in'] = (wgrad("ffn_in_dw", sv['h'], dug), wgrad("ffn_in_dw", sv['h'], duu))
    dh = mm3("ffn_in_dx", [dug, duu], w['w_in'], tb=True)
    return dh, g


def kernel(x, c, positions, mla_w_in, mla_q_norm, mla_w_qb, mla_kv_norm, mla_w_kvb, mla_w_o, hgrn_lb, hgrn_w_in, hgrn_g_norm, hgrn_w_o, ffn_w_in, ffn_w_out, ada_w, ada_b, ln_g, ln_b, loss_target, m_mla_w_in, m_mla_q_norm, m_mla_w_qb, m_mla_kv_norm, m_mla_w_kvb, m_mla_w_o, m_hgrn_lb, m_hgrn_w_in, m_hgrn_g_norm, m_hgrn_w_o, m_ffn_w_in, m_ffn_w_out, m_ada_w, m_ada_b, m_ln_g, m_ln_b, v_mla_w_in, v_mla_q_norm, v_mla_w_qb, v_mla_kv_norm, v_mla_w_kvb, v_mla_w_o, v_hgrn_lb, v_hgrn_w_in, v_hgrn_g_norm, v_hgrn_w_o, v_ffn_w_in, v_ffn_w_out, v_ada_w, v_ada_b, v_ln_g, v_ln_b):
    W = dict(zip(WEIGHTS, (mla_w_in, mla_q_norm, mla_w_qb, mla_kv_norm, mla_w_kvb, mla_w_o, hgrn_lb, hgrn_w_in, hgrn_g_norm,
                           hgrn_w_o, ffn_w_in, ffn_w_out, ada_w, ada_b, ln_g, ln_b)))
    M1 = dict(zip(WEIGHTS, (m_mla_w_in, m_mla_q_norm, m_mla_w_qb, m_mla_kv_norm, m_mla_w_kvb, m_mla_w_o, m_hgrn_lb, m_hgrn_w_in,
                            m_hgrn_g_norm, m_hgrn_w_o, m_ffn_w_in, m_ffn_w_out, m_ada_w, m_ada_b, m_ln_g, m_ln_b)))
    M2 = dict(zip(WEIGHTS, (v_mla_w_in, v_mla_q_norm, v_mla_w_qb, v_mla_kv_norm, v_mla_w_kvb, v_mla_w_o, v_hgrn_lb, v_hgrn_w_in,
                            v_hgrn_g_norm, v_hgrn_w_o, v_ffn_w_in, v_ffn_w_out, v_ada_w, v_ada_b, v_ln_g, v_ln_b)))
    bsz, seq, d = x.shape
    depth, n_mla, n_hgrn = ffn_w_in.shape[0], mla_w_in.shape[0], hgrn_w_in.shape[0]
    n_sub = 2 * depth

    big = COL_SHARDED + ROW_SHARDED
    wb = {n: cast_bf16("cast_" + n, W[n]) for n in big}

    def mixer_names(layer):
        mixer = ['mla_w_in', 'mla_w_qb', 'mla_w_kvb', 'mla_w_o'] if layer % 2 == 0 else ['hgrn_w_in', 'hgrn_w_o']
        return [(n, layer // 2) for n in mixer]

    def carried(layer):
        return [('ffn_w_in', layer), ('ffn_w_out', layer)] + (mixer_names(layer + 1) if layer + 1 < depth else [])

    def weight_items(names):
        return [(wb[n][j], "gather") for n, j in names]

    G = {}
    internal = {'mla_w_in': _w_in_internal, 'mla_w_qb': _qb_internal, 'mla_w_kvb': _kvb_internal}

    def take_weights(names, got):
        for (n, j), a in zip(names, got):
            G[n, j] = internal.get(n, lambda w: w)(_from_gather(n, a))

    lower_shard = lower_bounds(hgrn_lb)
    got = exchange("gather_first", [(lower_shard, "gather"), (ln_g, "gather"), (ln_b, "gather"), (c, "gather")]
                   + weight_items(mixer_names(0)))
    lower_all = got[0].transpose(1, 0, 2).reshape(n_hgrn, -1)
    ln_g_all = got[1].transpose(1, 2, 0, 3).reshape(depth, 2, d)
    ln_b_all = got[2].transpose(1, 2, 0, 3).reshape(depth, 2, d)
    c_all = got[3].reshape(N_DEV * bsz, d)
    take_weights(mixer_names(0), got[4:])

    cols = ada_w.shape[-1]
    mod_loc = ada_fwd(c_all, ada_w, ada_b)
    (mod_got,) = exchange("scatter_mod", [(mod_loc.reshape(n_sub, N_DEV, bsz, cols).transpose(1, 0, 2, 3), "a2a")])
    mod = mod_got.transpose(1, 2, 0, 3).reshape(n_sub, bsz, 1, 3 * d)
    shift = [mod[k, :, :, 0:d] for k in range(n_sub)]
    scale = [mod[k, :, :, d:2 * d] for k in range(n_sub)]
    gate = [mod[k, :, :, 2 * d:] for k in range(n_sub)]
    lng = [ln_g_all[k // 2, k % 2][None, :] for k in range(n_sub)]
    lnb = [ln_b_all[k // 2, k % 2][None, :] for k in range(n_sub)]

    tabs = rope_tables(positions)

    def sub_weights(k):
        layer, j = k // 2, k // 4
        if k % 2:
            return 'ffn', layer, dict(w_in=G['ffn_w_in', layer], w_out=G['ffn_w_out', layer])
        if layer % 2 == 0:
            return 'mla', j, dict(w_in=G['mla_w_in', j], q_norm=mla_q_norm[j][None, :], w_qb=G['mla_w_qb', j],
                                  kv_norm=mla_kv_norm[j][None, :], w_kvb=G['mla_w_kvb', j], w_o=G['mla_w_o', j])
        return 'hgrn', j, dict(w_in=G['hgrn_w_in', j], lb=lower_all[j][None, :], g_norm=hgrn_g_norm[j][None, :],
                               w_o=G['hgrn_w_o', j])

    (h,) = rowwise("mod_first", lambda rv, ev, gv: (f_mod(rv, ev, gv), []), [_full(x)], [scale[0], shift[0]], [],
                   [(BF16, [d])], ts=ROW_TILE)
    xs, ys, saved = [x], [], []
    loss_acc = None
    for k in range(n_sub):
        kind, _, w = sub_weights(k)
        ride = weight_items(carried(k // 2)) if k % 2 == 0 else []
        if kind == 'ffn':
            y, sv = _ffn_forward(h, w)
        elif kind == 'mla':
            y, sv, got = _mla_forward(h, w, tabs, ride)
        else:
            y, sv, got = _hgrn_forward(h, w, ride)
        if ride:
            take_weights(carried(k // 2), got)
        ys.append(y)
        saved.append(sv)
        if k + 1 < n_sub:
            xn, h = rowwise("ln_mod", lambda rv, ev, gv: (f_ln_mod(rv, ev, gv), []), [_full(xs[k]), _full(y)],
                            [gate[k], scale[k + 1], shift[k + 1]], [lng[k], lnb[k]], [(F32, [d]), (BF16, [d])], ts=ROW_TILE)
            xs.append(xn)
        else:
            def loss_rows(rv, ev, gv):
                (row,) = f_ln_loss(rv, ev, gv)
                return [], [jnp.broadcast_to(jnp.sum(row, keepdims=True), (1, LANES))]
            (loss_acc,) = rowwise("ln_loss", loss_rows, [_full(xs[k]), _full(y), _full(loss_target)], [gate[k]], [lng[k], lnb[k]],
                                  [], ts=ROW_TILE, accs=[LANES])
    loss = lax.psum(loss_acc[0, 0], ("x", "y", "c"))

    d_shift, d_scale, d_gate = [None] * n_sub, [None] * n_sub, [None] * n_sub
    d_lng, d_lnb = [None] * n_sub, [None] * n_sub
    part = {n: [None] * W[n].shape[0] for n in ['mla_q_norm', 'mla_kv_norm', 'hgrn_g_norm']}
    recv = {n: [None] * W[n].shape[0] for n in big}
    d_lower = [None] * n_hgrn
    k = n_sub - 1
    (dx, dy), (d_gate[k],), (d_lng[k], d_lnb[k]) = rowwise_bwd(
        "ln_loss_bwd", f_ln_loss, [_full(xs[k]), _full(ys[k]), _full(loss_target)], [gate[k]], [lng[k], lnb[k]], [],
        [(F32, [0]), (BF16, [1])], ts=ROW_TILE, n_diff=2, unit_ct=1)
    grad_x = None
    mine = {}

    def take_grads(names, got):
        for (n, j), a in zip(names, got):
            recv[n][j] = a

    def grad_items(names):
        return [(_to_slabs(n, mine[n, jj]), "a2a") for n, jj in names]

    for k in range(n_sub - 1, -1, -1):
        kind, j, w = sub_weights(k)
        ride = grad_items(carried(k // 2)) if kind != 'ffn' else []
        if kind == 'ffn':
            dh, g = _ffn_backward(dy, saved[k], w)
            new = {('ffn_w_in', j): g['w_in'], ('ffn_w_out', j): g['w_out']}
        elif kind == 'mla':
            dh, g, got = _mla_backward(dy, saved[k], w, tabs, ride)
            new = {('mla_w_in', j): g['w_in'][:, :mla_w_in.shape[-1] * N_DEV], ('mla_w_qb', j): _qb_internal(g['w_qb'], inverse=True),
                   ('mla_w_kvb', j): _kvb_internal(g['w_kvb'], inverse=True), ('mla_w_o', j): g['w_o']}
            part['mla_q_norm'][j], part['mla_kv_norm'][j] = g['q_norm'][0], g['kv_norm'][0]
        else:
            dh, g, got = _hgrn_backward(dy, saved[k], w, ride)
            new = {('hgrn_w_in', j): g['w_in'], ('hgrn_w_o', j): g['w_o']}
            part['hgrn_g_norm'][j] = g['g_norm'][0]
            d_lower[j] = g['lb'][0]
        if kind != 'ffn':
            take_grads(carried(k // 2), got)
        mine.update(new)
        if k:
            (dx, dy), (d_gate[k - 1], d_scale[k], d_shift[k]), (d_lng[k - 1], d_lnb[k - 1]) = rowwise_bwd(
                "ln_mod_bwd", f_ln_mod, [_full(xs[k - 1]), _full(ys[k - 1])], [gate[k - 1], scale[k], shift[k]],
                [lng[k - 1], lnb[k - 1]], [(dx, [d]), (dh, [d])], [(F32, [0]), (BF16, [1])], ts=ROW_TILE, n_diff=2)
        else:
            (grad_x,), (d_scale[0], d_shift[0]), _ = rowwise_bwd(
                "mod_first_bwd", f_mod_with_x, [_full(x)], [scale[0], shift[0]], [], [(dh, [d]), (dx, [d])],
                [(F32, [0])], ts=ROW_TILE, n_diff=1)

    waiting = mixer_names(0)
    slabs = grad_items(waiting)
    slabs.append((jnp.stack(d_lower).reshape(n_hgrn, N_DEV, -1).transpose(1, 0, 2), "a2a"))
    for parts in (d_lng, d_lnb):
        full = jnp.stack([p[0] for p in parts]).reshape(depth, 2, N_DEV, d // N_DEV)
        slabs.append((full.transpose(2, 0, 1, 3), "a2a"))
    dmod = jnp.concatenate([jnp.stack(d_shift), jnp.stack(d_scale), jnp.stack(d_gate)], axis=-1)
    slabs.append((dmod.reshape(n_sub, bsz, N_DEV, cols).transpose(2, 0, 1, 3), "a2a"))
    small = ['mla_q_norm', 'mla_kv_norm', 'hgrn_g_norm']
    slabs += [(jnp.stack(part[n]), "gather") for n in small]
    got = exchange("scatter_last", slabs)
    take_grads(waiting, got)
    stacks = dict(zip(['hgrn_lb', 'ln_g', 'ln_b', 'dmod'] + small, got[len(waiting):]))

    dmod_all = stacks['dmod'].transpose(1, 0, 2, 3).reshape(n_sub, N_DEV * bsz, cols)
    g_ada_w, g_ada_b = ada_bwd(c_all, dmod_all)
    stacks['ada_w'] = g_ada_w.reshape((1,) + ada_w.shape)
    stacks['ada_b'] = g_ada_b.reshape((1,) + ada_b.shape)

    res = {}
    for n in WEIGHTS:
        if n == 'hgrn_lb':
            res[n] = adam_lb(stacks[n], W[n], M1[n], M2[n])
        elif n in big:
            res[n] = adam_layers("adam_" + n, recv[n], W[n], M1[n], M2[n])
        else:
            res[n] = adam("adam_" + n, stacks[n], W[n], M1[n], M2[n])
    return (loss, grad_x, *[res[n][0] for n in WEIGHTS], *[res[n][1] for n in WEIGHTS], *[res[n][2] for n in WEIGHTS],
            *[res[n][3] for n in WEIGHTS])
```

```python
import functools
import math

import numpy as np
import jax
import jax.numpy as jnp
from jax import lax
from jax.experimental import pallas as pl
from jax.experimental.pallas import tpu as pltpu

F32 = jnp.float32
BF16 = jnp.bfloat16

N_DEV = 8
LANES = 128
VMEM_LIMIT = 52 * 1024 * 1024

D_MODEL = 1024
DEPTH = 4
MLA_HEADS = 16
QK_NOPE = 64
QK_ROPE = 32
V_HEAD = 64
Q_LORA = 768
KV_LORA = 256
ROPE_THETA = 10000.0
HGRN_EXPAND = 128
HGRN_CHUNK = 128
HGRN_HEADS_PER_STEP = 4
D_FF = 2816
ALPHA = (2.0 * DEPTH) ** 0.25
LN_EPS = 1e-5
RMS_EPS = 1e-6
ADAM_LR = 0.001
ADAM_B1 = 0.9
ADAM_B2 = 0.999
ADAM_EPS = 1e-08
ADAM_WD = 0.01
ADAM_STEP = 10

ATTN_TQ = 512
ROW_TILE = 256

WEIGHTS = ['mla_w_in', 'mla_q_norm', 'mla_w_qb', 'mla_kv_norm', 'mla_w_kvb', 'mla_w_o', 'hgrn_lb', 'hgrn_w_in',
           'hgrn_g_norm', 'hgrn_w_o', 'ffn_w_in', 'ffn_w_out', 'ada_w', 'ada_b', 'ln_g', 'ln_b']
COL_SHARDED = ['mla_w_in', 'mla_w_qb', 'mla_w_kvb', 'hgrn_w_in', 'ffn_w_in']
ROW_SHARDED = ['mla_w_o', 'hgrn_w_o', 'ffn_w_out']


def _params(*sem):
    if sem:
        return pltpu.CompilerParams(dimension_semantics=sem, vmem_limit_bytes=VMEM_LIMIT)
    return pltpu.CompilerParams(vmem_limit_bytes=VMEM_LIMIT)


def _pick(n, cap):
    best = None
    for t in range(LANES, min(n, cap) + 1, LANES):
        if n % t == 0:
            best = t
    return best or n


def _pick_rows(n, cap):
    best = None
    for t in range(8, min(n, cap) + 1, 8):
        if n % t == 0:
            best = t
    return best or n


def matmul(name, a, b, *, ta=False, tb=False, out_dtype=F32, tm_cap=1024, tn_cap=1536, tk_cap=2048):
    parts = list(a) if isinstance(a, (list, tuple)) else [a]
    n_parts = len(parts)
    assert n_parts == 1 or not ta
    (kp, m) = parts[0].shape if ta else parts[0].shape[::-1]
    (n, k2) = b.shape if tb else b.shape[::-1]
    assert kp * n_parts == k2, (name, parts[0].shape, b.shape)
    tm, tn, tk = _pick(m, tm_cap), _pick(n, tn_cap), _pick(kp, tk_cap)
    nkp = kp // tk
    nk = nkp * n_parts
    dims = (((0 if ta else 1,), (1 if tb else 0,)), ((), ()))

    def prod(a_ref, b_ref):
        return lax.dot_general(a_ref[...].astype(BF16), b_ref[...].astype(BF16), dims, preferred_element_type=F32)

    if nk == 1:
        def body(a_ref, b_ref, o_ref):
            o_ref[...] = prod(a_ref, b_ref).astype(o_ref.dtype)
        scratch = []
    else:
        def body(*refs):
            a_refs, (b_ref, o_ref, acc_ref) = refs[:n_parts], refs[n_parts:]
            k = pl.program_id(2)

            @pl.when(k == 0)
            def _():
                acc_ref[...] = jnp.zeros_like(acc_ref)

            if n_parts == 1:
                acc_ref[...] += prod(a_refs[0], b_ref)
            else:
                for p in range(n_parts):
                    @pl.when((k >= p * nkp) & (k < (p + 1) * nkp))
                    def _(p=p):
                        acc_ref[...] += prod(a_refs[p], b_ref)

            @pl.when(k == nk - 1)
            def _():
                o_ref[...] = acc_ref[...].astype(o_ref.dtype)
        scratch = [pltpu.VMEM((tm, tn), F32)]

    if ta:
        a_specs = [pl.BlockSpec((tk, tm), lambda i, j, k: (k, i))]
    elif n_parts == 1:
        a_specs = [pl.BlockSpec((tm, tk), lambda i, j, k: (i, k))]
    else:
        a_specs = [pl.BlockSpec((tm, tk), lambda i, j, k, p=p: (i, jnp.clip(k - p * nkp, 0, nkp - 1))) for p in range(n_parts)]
    b_spec = pl.BlockSpec((tn, tk), lambda i, j, k: (j, k)) if tb else pl.BlockSpec((tk, tn), lambda i, j, k: (k, j))
    return pl.pallas_call(
        body, name=name, grid=(m // tm, n // tn, nk),
        in_specs=a_specs + [b_spec], out_specs=pl.BlockSpec((tm, tn), lambda i, j, k: (i, j)),
        out_shape=jax.ShapeDtypeStruct((m, n), out_dtype), scratch_shapes=scratch,
        compiler_params=_params("parallel", "parallel", "arbitrary"),
    )(*parts, b)


def mm3(name, a3, w, **kw):
    parts = list(a3) if isinstance(a3, (list, tuple)) else [a3]
    bsz, s = parts[0].shape[:2]
    flat = [p.reshape(bsz * s, p.shape[-1]) for p in parts]
    out = matmul(name, flat if len(flat) > 1 else flat[0], w, **kw)
    return out.reshape(bsz, s, out.shape[-1])


def ffn_in_act(h3, w_in):
    bsz, s, k = h3.shape
    m, dff = bsz * s, w_in.shape[1] // 2
    tm, tn = _pick(m, 512), _pick(dff, 1536)
    nj = dff // tn

    def body(h_ref, wg_ref, wu_ref, ug_ref, uu_ref, a_ref):
        hv = h_ref[...].astype(BF16)
        ug = _dg(hv, wg_ref[...].astype(BF16), 1, 0)
        uu = _dg(hv, wu_ref[...].astype(BF16), 1, 0)
        ug_ref[...] = ug.astype(ug_ref.dtype)
        uu_ref[...] = uu.astype(uu_ref.dtype)
        a_ref[...] = (jax.nn.silu(ug) * uu).astype(a_ref.dtype)

    out = pl.BlockSpec((tm, tn), lambda j, i: (i, j))
    res = pl.pallas_call(
        body, name="ffn_in_act", grid=(nj, m // tm),
        in_specs=[pl.BlockSpec((tm, k), lambda j, i: (i, 0)), pl.BlockSpec((k, tn), lambda j, i: (0, j)),
                  pl.BlockSpec((k, tn), lambda j, i: (0, nj + j))],
        out_specs=[out, out, out],
        out_shape=[jax.ShapeDtypeStruct((m, dff), BF16)] * 3,
        compiler_params=_params("arbitrary", "arbitrary"),
    )(h3.reshape(m, k), w_in, w_in)
    return [r.reshape(bsz, s, dff) for r in res]


def ffn_out_dx_act(dy3, w_out, ug, uu):
    bsz, s, d = dy3.shape
    m, dff = bsz * s, w_out.shape[0]
    tm, tn = _pick(m, 512), _pick(dff, 1536)

    def body(dy_ref, w_ref, ug_ref, uu_ref, dg_ref, du_ref):
        da = _dg(dy_ref[...].astype(BF16), w_ref[...].astype(BF16), 1, 1)
        _, vjp = jax.vjp(lambda gate, up: jax.nn.silu(gate) * up, ug_ref[...].astype(F32), uu_ref[...].astype(F32))
        dg, du = vjp(da)
        dg_ref[...] = dg.astype(dg_ref.dtype)
        du_ref[...] = du.astype(du_ref.dtype)

    blk = pl.BlockSpec((tm, tn), lambda j, i: (i, j))
    res = pl.pallas_call(
        body, name="ffn_out_dx_act", grid=(dff // tn, m // tm),
        in_specs=[pl.BlockSpec((tm, d), lambda j, i: (i, 0)), pl.BlockSpec((tn, d), lambda j, i: (j, 0)), blk, blk],
        out_specs=[blk, blk], out_shape=[jax.ShapeDtypeStruct((m, dff), BF16)] * 2,
        compiler_params=_params("arbitrary", "arbitrary"),
    )(dy3.reshape(m, d), w_out, ug.reshape(m, dff), uu.reshape(m, dff))
    return [r.reshape(bsz, s, dff) for r in res]


def wgrad(name, a3, g3):
    bsz, s, k = a3.shape
    return matmul(name, a3.reshape(bsz * s, k), g3.reshape(bsz * s, g3.shape[-1]), ta=True, out_dtype=BF16)


def _dg(a, b, ca, cb, **kw):
    return lax.dot_general(a, b, (((ca,), (cb,)), ((), ())), preferred_element_type=F32, **kw)


@functools.partial(jax.custom_vjp, nondiff_argnums=(2, 3))
def bdot(a, b, ca, cb):
    return _dg(a.astype(BF16), b.astype(BF16), ca, cb)


def _bdot_fwd(a, b, ca, cb):
    return bdot(a, b, ca, cb), (a, b)


def _bdot_bwd(ca, cb, res, g):
    a, b = res
    a16, b16, g16 = a.astype(BF16), b.astype(BF16), g.astype(BF16)
    if ca == 1:
        da = _dg(g16, b16, 1, 1 if cb == 0 else 0)
    else:
        da = _dg(b16, g16, 1 if cb == 0 else 0, 1)
    if cb == 0:
        db = _dg(a16, g16, 0 if ca == 1 else 1, 0)
    else:
        db = _dg(g16, a16, 0, 0 if ca == 1 else 1)
    return da, db


bdot.defvjp(_bdot_fwd, _bdot_bwd)


def hdot(a, b, ca=1, cb=0):
    return _dg(a, b, ca, cb, precision=lax.Precision.HIGHEST)


def _row_specs(rows, exs, globs, ts):
    specs = [pl.BlockSpec((1, ts, w), lambda b, s, j=j: (b, s, j)) for (_, j, w) in rows]
    specs += [pl.BlockSpec((1, 1, e.shape[-1]), lambda b, s: (b, 0, 0)) for e in exs]
    specs += [pl.BlockSpec((1, g.shape[-1]), lambda b, s: (0, 0)) for g in globs]
    return specs


def _store_pieces(o_ref, pieces, widths):
    off = 0
    for p, w in zip(pieces, widths):
        o_ref[0, :, off:off + w] = p.astype(o_ref.dtype)
        off += w


def _load_pieces(c_ref, widths):
    out, off = [], 0
    for w in widths:
        out.append(c_ref[0, :, off:off + w].astype(F32))
        off += w
    return out


def rowwise(name, f, rows, exs, globs, outs, *, ts, accs=()):
    bsz, s = rows[0][0].shape[:2]
    ts = min(ts, s)
    n_r, n_e, n_g, n_o = len(rows), len(exs), len(globs), len(outs)

    def body(*refs):
        rv = [r[0].astype(F32) for r in refs[:n_r]]
        ev = [e[0] for e in refs[n_r:n_r + n_e]]
        gv = [g[...] for g in refs[n_r + n_e:n_r + n_e + n_g]]
        o_refs = refs[n_r + n_e + n_g:n_r + n_e + n_g + n_o]
        a_refs = refs[n_r + n_e + n_g + n_o:]
        pieces, sums = f(rv, ev, gv)
        idx = 0
        for o_ref, (_, ws) in zip(o_refs, outs):
            _store_pieces(o_ref, pieces[idx:idx + len(ws)], ws)
            idx += len(ws)
        if accs:
            @pl.when((pl.program_id(0) == 0) & (pl.program_id(1) == 0))
            def _():
                for a_ref in a_refs:
                    a_ref[...] = jnp.zeros_like(a_ref)
            for a_ref, val in zip(a_refs, sums):
                a_ref[...] += val

    out_specs = [pl.BlockSpec((1, ts, sum(ws)), lambda b, s: (b, s, 0)) for (_, ws) in outs]
    out_specs += [pl.BlockSpec((1, w), lambda b, s: (0, 0)) for w in accs]
    out_shape = [jax.ShapeDtypeStruct((bsz, s, sum(ws)), dt) for (dt, ws) in outs]
    out_shape += [jax.ShapeDtypeStruct((1, w), F32) for w in accs]
    return pl.pallas_call(
        body, name=name, grid=(bsz, s // ts),
        in_specs=_row_specs(rows, exs, globs, ts), out_specs=out_specs, out_shape=out_shape,
        compiler_params=_params("arbitrary", "arbitrary"),
    )(*[r[0] for r in rows], *exs, *globs)


def rowwise_bwd(name, f, rows, exs, globs, cts, d_groups, *, ts, n_diff, unit_ct=0):
    bsz, s = rows[0][0].shape[:2]
    ts = min(ts, s)
    n_r, n_e, n_g, n_c = len(rows), len(exs), len(globs), len(cts)
    n_d = len(d_groups)

    def body(*refs):
        rv = [r[0].astype(F32) for r in refs[:n_r]]
        ev = [e[0] for e in refs[n_r:n_r + n_e]]
        gv = [g[...] for g in refs[n_r + n_e:n_r + n_e + n_g]]
        base = n_r + n_e + n_g
        c_refs = refs[base:base + n_c]
        d_refs = refs[base + n_c:base + n_c + n_d]
        de_refs = refs[base + n_c + n_d:base + n_c + n_d + n_e]
        dg_refs = refs[base + n_c + n_d + n_e:]
        fixed = rv[n_diff:]
        out, vjp = jax.vjp(lambda r, e, g: f(r + fixed, e, g), rv[:n_diff], ev, gv)
        ct = []
        for c_ref, (_, ws) in zip(c_refs, cts):
            ct += _load_pieces(c_ref, ws)
        ct += [jnp.ones_like(o) for o in out[len(ct):]]
        assert len(ct) == len(out) and len(out) - unit_ct == sum(len(ws) for _, ws in cts), name
        d_r, d_e, d_g = vjp(ct)
        for d_ref, (_, idxs) in zip(d_refs, d_groups):
            _store_pieces(d_ref, [d_r[i] for i in idxs], [rows[i][2] for i in idxs])
        first_s = pl.program_id(1) == 0
        if n_e:
            @pl.when(first_s)
            def _():
                for r in de_refs:
                    r[...] = jnp.zeros_like(r)
            for r, val in zip(de_refs, d_e):
                r[0] += val
        if n_g:
            @pl.when(first_s & (pl.program_id(0) == 0))
            def _():
                for r in dg_refs:
                    r[...] = jnp.zeros_like(r)
            for r, val in zip(dg_refs, d_g):
                r[...] += val

    in_specs = _row_specs(rows, exs, globs, ts)
    in_specs += [pl.BlockSpec((1, ts, sum(ws)), lambda b, s: (b, s, 0)) for (_, ws) in cts]
    out_specs = [pl.BlockSpec((1, ts, sum(rows[i][2] for i in idxs)), lambda b, s: (b, s, 0)) for (_, idxs) in d_groups]
    out_specs += [pl.BlockSpec((1, 1, e.shape[-1]), lambda b, s: (b, 0, 0)) for e in exs]
    out_specs += [pl.BlockSpec((1, g.shape[-1]), lambda b, s: (0, 0)) for g in globs]
    out_shape = [jax.ShapeDtypeStruct((bsz, s, sum(rows[i][2] for i in idxs)), dt) for (dt, idxs) in d_groups]
    out_shape += [jax.ShapeDtypeStruct(e.shape, F32) for e in exs]
    out_shape += [jax.ShapeDtypeStruct(g.shape, F32) for g in globs]
    res = pl.pallas_call(
        body, name=name, grid=(bsz, s // ts),
        in_specs=in_specs, out_specs=out_specs, out_shape=out_shape,
        compiler_params=_params("arbitrary", "arbitrary"),
    )(*[r[0] for r in rows], *exs, *globs, *[c[0] for c in cts])
    return res[:n_d], res[n_d:n_d + n_e], res[n_d + n_e:]


def _full(a):
    return (a, 0, a.shape[-1])


def _view(a, col, w):
    assert col % w == 0
    return (a, col // w, w)


def _layer_norm(z, g, b):
    mu = jnp.mean(z, -1, keepdims=True)
    var = jnp.mean(jnp.square(z - mu), -1, keepdims=True)
    return (z - mu) * lax.rsqrt(var + LN_EPS) * g + b


def _rms_norm(z, g):
    ms = jnp.mean(jnp.square(z), -1, keepdims=True)
    return z * lax.rsqrt(ms + RMS_EPS) * g


def f_mod(rv, ev, gv):
    (x,), (scale, shift) = rv, ev
    return [x * (1.0 + scale) + shift]


def f_mod_with_x(rv, ev, gv):
    return f_mod(rv, ev, gv) + [rv[0]]


def f_ln_mod(rv, ev, gv):
    (x, y), (gate, scale, shift), (g, b) = rv, ev, gv
    xn = _layer_norm(ALPHA * x + (1.0 + gate) * y, g, b)
    return [xn, xn * (1.0 + scale) + shift]


def f_ln_loss(rv, ev, gv):
    (x, y, target), (gate,), (g, b) = rv, ev, gv
    xn = _layer_norm(ALPHA * x + (1.0 + gate) * y, g, b)
    return [0.5 * jnp.mean(jnp.square(xn - target), -1, keepdims=True)]


def _head_spread(width):
    r2 = QK_ROPE // 2
    j = lax.broadcasted_iota(jnp.int32, (LANES, width), 0)
    col = lax.broadcasted_iota(jnp.int32, (LANES, width), 1) % r2
    return (j == col).astype(F32), (j == col + r2).astype(F32)


def f_mla_mid(rv, ev, gv):
    (q_lat, kv_lat, kr, cos, sin), (q_g, kv_g) = rv, gv
    e1, e2 = _head_spread(cos.shape[-1])
    k1, k2 = hdot(kr, e1), hdot(kr, e2)
    return [_rms_norm(q_lat, q_g), _rms_norm(kv_lat, kv_g), k1 * cos - k2 * sin, k1 * sin + k2 * cos]


def rope_tables(positions):
    bsz, s = positions.shape
    r2 = QK_ROPE // 2
    width = MLA_HEADS * r2
    inv = (ROPE_THETA ** (-np.arange(0, QK_ROPE, 2, dtype=np.float32) / QK_ROPE)).astype(np.float32)
    inv = jnp.asarray(np.tile(inv, MLA_HEADS)[None, :])
    ts = min(ROW_TILE, s)

    def body(p_ref, inv_ref, cos_ref, sin_ref):
        ang = p_ref[0].astype(F32) * inv_ref[...]
        cos_ref[0] = jnp.cos(ang)
        sin_ref[0] = jnp.sin(ang)

    spec = pl.BlockSpec((1, ts, width), lambda b, s: (b, s, 0))
    return pl.pallas_call(
        body, name="rope_tables", grid=(bsz, s // ts),
        in_specs=[pl.BlockSpec((1, ts, 1), lambda b, s: (b, s, 0)), pl.BlockSpec((1, width), lambda b, s: (0, 0))],
        out_specs=[spec, spec], out_shape=[jax.ShapeDtypeStruct((bsz, s, width), F32)] * 2,
        compiler_params=_params("arbitrary", "arbitrary"),
    )(positions[:, :, None], inv)


def _attn_probs(q, k, row0):
    scale = (QK_NOPE + QK_ROPE) ** -0.5
    s = _dg(q, k, 1, 1) * scale
    rows = row0 + lax.broadcasted_iota(jnp.int32, s.shape, 0)
    cols = lax.broadcasted_iota(jnp.int32, s.shape, 1)
    s = jnp.where(cols <= rows, s, jnp.finfo(F32).min)
    e = jnp.exp(s - jnp.max(s, -1, keepdims=True))
    return e / jnp.sum(e, -1, keepdims=True), scale


ATTN_PAIR = 2


def attn_fwd(q, k, v, ride=()):
    bsz, h, s, dq = q.shape
    dv = v.shape[-1]
    tq = min(ATTN_TQ, s)
    grid = (bsz, h // ATTN_PAIR, s // tq)
    rd = _Ride(ride)

    def body(*refs):
        (q_ref, k_ref, v_ref), srcs, (o_ref,), outs, sems = rd.split(refs, 3, 1)
        rd.run(srcs, outs, sems, grid)
        for i in range(grid[2]):
            @pl.when(pl.program_id(2) == i)
            def _(i=i):
                kend = (i + 1) * tq
                for e in range(ATTN_PAIR):
                    p, _ = _attn_probs(q_ref[0, e], k_ref[0, e, :kend, :], i * tq)
                    o_ref[0, :, e * dv:(e + 1) * dv] = _dg(p.astype(BF16), v_ref[0, e, :kend, :], 1, 0).astype(o_ref.dtype)

    res = pl.pallas_call(
        body, name="attn_fwd", grid=grid,
        in_specs=[pl.BlockSpec((1, ATTN_PAIR, tq, dq), lambda b, h, i: (b, h, i, 0)),
                  pl.BlockSpec((1, ATTN_PAIR, s, dq), lambda b, h, i: (b, h, 0, 0)),
                  pl.BlockSpec((1, ATTN_PAIR, s, dv), lambda b, h, i: (b, h, 0, 0))] + rd.in_specs,
        out_specs=[pl.BlockSpec((1, tq, ATTN_PAIR * dv), lambda b, h, i: (b, i, h))] + rd.out_specs,
        out_shape=[jax.ShapeDtypeStruct((bsz, s, h * dv), BF16)] + rd.out_shape, scratch_shapes=rd.scratch,
        compiler_params=_params("arbitrary", "arbitrary", "arbitrary"),
    )(q, k, v, *rd.srcs)
    return res[0], res[1:]


def attn_bwd(q, k, v, do, ride=()):
    bsz, h, s, dq = q.shape
    dv = v.shape[-1]
    tq = min(ATTN_TQ, s)
    grid = (bsz, h // ATTN_PAIR, s // tq)
    rd = _Ride(ride)

    def body(*refs):
        (q_ref, k_ref, v_ref, do_ref), srcs, (dq_ref, dk_ref, dv_ref), outs, sems = rd.split(refs, 4, 3)
        rd.run(srcs, outs, sems, grid)

        @pl.when(pl.program_id(2) == 0)
        def _():
            dk_ref[...] = jnp.zeros_like(dk_ref)
            dv_ref[...] = jnp.zeros_like(dv_ref)

        for i in range(grid[2]):
            @pl.when(pl.program_id(2) == i)
            def _(i=i):
                kend = (i + 1) * tq
                for e in range(ATTN_PAIR):
                    qv, kv, vv = q_ref[0, e], k_ref[0, e, :kend, :], v_ref[0, e, :kend, :]
                    p, scale = _attn_probs(qv, kv, i * tq)
                    do16 = do_ref[0, :, e * dv:(e + 1) * dv].astype(BF16)
                    dv_ref[0, e, :kend, :] += _dg(p.astype(BF16), do16, 0, 0)
                    dp = _dg(do16, vv, 1, 1)
                    ds = (p * (dp - jnp.sum(dp * p, -1, keepdims=True)) * scale).astype(BF16)
                    dq_ref[0, e] = _dg(ds, kv, 1, 0)
                    dk_ref[0, e, :kend, :] += _dg(ds, qv, 0, 0)

    res = pl.pallas_call(
        body, name="attn_bwd", grid=grid,
        in_specs=[pl.BlockSpec((1, ATTN_PAIR, tq, dq), lambda b, h, i: (b, h, i, 0)),
                  pl.BlockSpec((1, ATTN_PAIR, s, dq), lambda b, h, i: (b, h, 0, 0)),
                  pl.BlockSpec((1, ATTN_PAIR, s, dv), lambda b, h, i: (b, h, 0, 0)),
                  pl.BlockSpec((1, tq, ATTN_PAIR * dv), lambda b, h, i: (b, i, h))] + rd.in_specs,
        out_specs=[pl.BlockSpec((1, ATTN_PAIR, tq, dq), lambda b, h, i: (b, h, i, 0)),
                   pl.BlockSpec((1, ATTN_PAIR, s, dq), lambda b, h, i: (b, h, 0, 0)),
                   pl.BlockSpec((1, ATTN_PAIR, s, dv), lambda b, h, i: (b, h, 0, 0))] + rd.out_specs,
        out_shape=[jax.ShapeDtypeStruct((bsz, h, s, dq), F32), jax.ShapeDtypeStruct((bsz, h, s, dq), F32),
                   jax.ShapeDtypeStruct((bsz, h, s, dv), F32)] + rd.out_shape, scratch_shapes=rd.scratch,
        compiler_params=_params("arbitrary", "arbitrary", "arbitrary"),
    )(q, k, v, do, *rd.srcs)
    return res[0], res[1], res[2], res[3:]


def mla_heads(q, kv, krt, cos, sin):
    bsz, s, _ = q.shape
    nh, n, r2, vd = MLA_HEADS, QK_NOPE, QK_ROPE // 2, V_HEAD
    ts = min(ROW_TILE, s)

    def body(q_ref, kv_ref, kr_ref, cos_ref, sin_ref, qh_ref, kh_ref, vh_ref):
        cos, sin = cos_ref[0], sin_ref[0]
        qv, kvv, kr = q_ref[0], kv_ref[0], kr_ref[0].astype(F32)
        q1, q2 = qv[:, nh * n:nh * (n + r2)], qv[:, nh * (n + r2):]
        qr = [q1 * cos - q2 * sin, q1 * sin + q2 * cos]
        for h in range(nh):
            qh_ref[0, h, :, :n] = qv[:, h * n:(h + 1) * n].astype(BF16)
            kh_ref[0, h, :, :n] = kvv[:, h * n:(h + 1) * n].astype(BF16)
            vh_ref[0, h] = kvv[:, nh * n + h * vd:nh * n + (h + 1) * vd].astype(BF16)
            for j in range(2):
                qh_ref[0, h, :, n + j * r2:n + (j + 1) * r2] = qr[j][:, h * r2:(h + 1) * r2].astype(BF16)
                kh_ref[0, h, :, n + j * r2:n + (j + 1) * r2] = kr[:, (j * nh + h) * r2:(j * nh + h + 1) * r2].astype(BF16)

    def row(a):
        return pl.BlockSpec((1, ts, a.shape[-1]), lambda b, i: (b, i, 0))

    def heads(w):
        return pl.BlockSpec((1, nh, ts, w), lambda b, i: (b, 0, i, 0))

    return pl.pallas_call(
        body, name="mla_heads", grid=(bsz, s // ts), in_specs=[row(q), row(kv), row(krt), row(cos), row(sin)],
        out_specs=[heads(n + 2 * r2), heads(n + 2 * r2), heads(vd)],
        out_shape=[jax.ShapeDtypeStruct((bsz, nh, s, n + 2 * r2), BF16)] * 2 + [jax.ShapeDtypeStruct((bsz, nh, s, vd), BF16)],
        compiler_params=_params("arbitrary", "arbitrary"),
    )(q, kv, krt, cos, sin)


def mla_heads_bwd(dqh, dkh, dvh, cos, sin):
    bsz, nh, s, _ = dqh.shape
    n, r2, vd = QK_NOPE, QK_ROPE // 2, V_HEAD
    ts = min(ROW_TILE, s)

    def body(dq_ref, dk_ref, dv_ref, cos_ref, sin_ref, oq_ref, okv_ref, okr_ref, tq_acc, tkv_acc, rot):
        for h in range(nh):
            tq_acc[:, h * n:(h + 1) * n] = dq_ref[0, h, :, :n]
            tkv_acc[:, h * n:(h + 1) * n] = dk_ref[0, h, :, :n]
            tkv_acc[:, nh * n + h * vd:nh * n + (h + 1) * vd] = dv_ref[0, h]
            for j in range(2):
                rot[j, :, h * r2:(h + 1) * r2] = dq_ref[0, h, :, n + j * r2:n + (j + 1) * r2]
                okr_ref[0, :, (j * nh + h) * r2:(j * nh + h + 1) * r2] = dk_ref[0, h, :, n + j * r2:n + (j + 1) * r2]
        cos, sin = cos_ref[0], sin_ref[0]
        d1, d2 = rot[0], rot[1]
        oq_ref[0, :, :nh * n] = tq_acc[...].astype(BF16)
        oq_ref[0, :, nh * n:nh * (n + r2)] = (d1 * cos + d2 * sin).astype(BF16)
        oq_ref[0, :, nh * (n + r2):] = (d2 * cos - d1 * sin).astype(BF16)
        okv_ref[0] = tkv_acc[...].astype(BF16)

    def row(w):
        return pl.BlockSpec((1, ts, w), lambda b, i: (b, i, 0))

    def heads(w):
        return pl.BlockSpec((1, nh, ts, w), lambda b, i: (b, 0, i, 0))

    return pl.pallas_call(
        body, name="mla_heads_bwd", grid=(bsz, s // ts),
        in_specs=[heads(n + 2 * r2), heads(n + 2 * r2), heads(vd), row(nh * r2), row(nh * r2)],
        out_specs=[row(nh * (n + 2 * r2)), row(nh * (n + vd)), row(2 * nh * r2)],
        out_shape=[jax.ShapeDtypeStruct((bsz, s, nh * (n + 2 * r2)), BF16), jax.ShapeDtypeStruct((bsz, s, nh * (n + vd)), BF16),
                   jax.ShapeDtypeStruct((bsz, s, 2 * nh * r2), F32)],
        scratch_shapes=[pltpu.VMEM((ts, nh * n), F32), pltpu.VMEM((ts, nh * (n + vd)), F32), pltpu.VMEM((2, ts, nh * r2), F32)],
        compiler_params=_params("arbitrary", "arbitrary"),
    )(dqh, dkh, dvh, cos, sin)


def _hgrn_tables(c):
    levels = c.bit_length() - 1
    assert 1 << levels == c
    r = np.arange(c)
    sign, mask = [], []
    for l in range(levels):
        lower = ((r >> l) & 1) == 1
        sign.append(np.broadcast_to(np.where(lower, 1.0, -1.0)[:, None], (c, LANES)))
        mask.append((((r[:, None] ^ r[None, :]) >> l) == 1) & lower[:, None])
    return (jnp.asarray(r[:, None] >= r[None, :], BF16), jnp.asarray(np.stack(sign), F32), jnp.asarray(np.stack(mask), F32))


def _const_specs(tables):
    return [pl.BlockSpec(t.shape, lambda b, h, i, nd=t.ndim: (0,) * nd) for t in tables]


def _split3(x):
    hi = x.astype(BF16)
    rest = x - hi.astype(F32)
    mid = rest.astype(BF16)
    return hi, mid, (rest - mid.astype(F32)).astype(BF16)


@functools.partial(jax.custom_vjp, nondiff_argnums=(2,))
def prefix_sums(p, g, n):
    k = g.shape[1]
    r = _dg(p, jnp.concatenate(_split3(g), axis=1), 1, 0)
    r = r[:, :k] + r[:, k:2 * k] + r[:, 2 * k:]
    c = r.shape[0] // n
    return tuple(r[i * c:(i + 1) * c] for i in range(n))


def _prefix_fwd(p, g, n):
    return prefix_sums(p, g, n), p


def _prefix_bwd(n, p, ct):
    ct = jnp.concatenate(ct, axis=0)
    k = ct.shape[1]
    r = _dg(p, jnp.concatenate(_split3(ct), axis=1), 0, 0)
    return jnp.zeros_like(p), r[:, :k] + r[:, k:2 * k] + r[:, 2 * k:]


prefix_sums.defvjp(_prefix_fwd, _prefix_bwd)


@functools.partial(jax.custom_vjp, nondiff_argnums=(1, 2))
def block_row(x, size, row):
    c, k = x.shape
    x3 = x.reshape(c // size, size, k)
    return jnp.broadcast_to(x3[:, row:row + 1, :], x3.shape).reshape(c, k)


def _block_row_fwd(x, size, row):
    return block_row(x, size, row), None


def _block_row_bwd(size, row, _, ct):
    c, k = ct.shape
    ct3 = ct.reshape(c // size, size, k)
    total = jnp.broadcast_to(jnp.sum(ct3, axis=1, keepdims=True), ct3.shape)
    rows = lax.broadcasted_iota(jnp.int32, ct3.shape, 1)
    return (jnp.where(rows == row, total, 0.0).reshape(c, k),)


block_row.defvjp(_block_row_fwd, _block_row_bwd)


def _hgrn_chunk(q, g, k, v, st0, prefix, sign, mask):
    levels = len(mask)
    (b,) = prefix_sums(prefix, g, 1)
    o = bdot(q * jnp.exp(b), st0, 1, 1)
    att = None
    for l in range(levels):
        e = jnp.exp((b - block_row(b, 2 << l, (1 << l) - 1)) * sign[l])
        a = bdot(q * e, k * e, 1, 1) * mask[l]
        att = a if att is None else att + a
    o = o + bdot(att, v, 1, 0) + jnp.sum(q * k, -1, keepdims=True) * v
    total = jnp.sum(g, 0, keepdims=True)
    st1 = st0 * jnp.exp(total) + bdot(v, k * jnp.exp(total - b), 0, 0)
    return o, st1


def _hgrn_step(q_raw, fx, v, g_raw, st0, lb, gn, prefix, sign, mask):
    f = lb + (1.0 - lb) * jax.nn.sigmoid(fx)
    o, st1 = _hgrn_chunk(jax.nn.silu(q_raw), jnp.log(f), 1.0 - f, v, st0, prefix, sign, mask)
    return _rms_norm(o, gn) * jax.nn.silu(g_raw), st1


def _hgrn_layout(proj):
    bsz, s, width = proj.shape
    kd = HGRN_EXPAND
    c = min(HGRN_CHUNK, s)
    nh = width // (4 * kd)
    hp = math.gcd(nh, HGRN_HEADS_PER_STEP)
    return bsz, s, kd, c, nh, s // c, hp


def hgrn_fwd(proj, lb, gn, ride=()):
    bsz, s, kd, c, nh, nc, hp = _hgrn_layout(proj)
    groups = nh // hp
    tables = _hgrn_tables(c)
    levels = tables[2].shape[0]
    grid = (groups, bsz, nc)
    rd = _Ride(ride)

    def body(*refs):
        ((q_ref, f_ref, v_ref, g_ref, lb_ref, gn_ref, p_ref, sg_ref, mk_ref), srcs, (z_ref, st_ref), outs,
         scratch) = rd.split(refs, 9, 2)
        state = scratch[0]
        rd.run(srcs, outs, scratch, grid)

        @pl.when(pl.program_id(2) == 0)
        def _():
            state[...] = jnp.zeros_like(state)

        prefix, sign, mask = p_ref[...], [sg_ref[l] for l in range(levels)], [mk_ref[l] for l in range(levels)]
        for j in range(hp):
            cols = slice(j * kd, (j + 1) * kd)
            st0 = state[j]
            st_ref[0, j, 0] = st0
            z, st1 = _hgrn_step(q_ref[0, :, cols], f_ref[0, :, cols], v_ref[0, :, cols], g_ref[0, :, cols], st0,
                                lb_ref[:, cols], gn_ref[...], prefix, sign, mask)
            z_ref[0, :, cols] = z.astype(z_ref.dtype)
            state[j] = st1

    def part(k):
        return pl.BlockSpec((1, c, hp * kd), lambda h, b, i: (b, i, k * groups + h))

    res = pl.pallas_call(
        body, name="hgrn_fwd", grid=grid,
        in_specs=[part(0), part(1), part(2), part(3), pl.BlockSpec((1, hp * kd), lambda h, b, i: (0, h)),
                  pl.BlockSpec((1, kd), lambda h, b, i: (0, 0))] + _const_specs(tables) + rd.in_specs,
        out_specs=[part(0), pl.BlockSpec((1, hp, 1, kd, kd), lambda h, b, i: (b, h, i, 0, 0))] + rd.out_specs,
        out_shape=[jax.ShapeDtypeStruct((bsz, s, nh * kd), BF16), jax.ShapeDtypeStruct((bsz, nh, nc, kd, kd), F32)] + rd.out_shape,
        scratch_shapes=[pltpu.VMEM((hp, kd, kd), F32)] + rd.scratch,
        compiler_params=_params("arbitrary", "arbitrary", "arbitrary"),
    )(proj, proj, proj, proj, lb, gn, *tables, *rd.srcs)
    return res[0], res[1], res[2:]


def hgrn_bwd(proj, states, dz, lb, gn, ride=()):
    bsz, s, kd, c, nh, nc, hp = _hgrn_layout(proj)
    groups = nh // hp
    tables = _hgrn_tables(c)
    levels = tables[2].shape[0]
    grid = (groups, bsz, nc)
    rd = _Ride(ride)

    def body(*refs):
        ((q_ref, f_ref, v_ref, g_ref, st_ref, dz_ref, lb_ref, gn_ref, p_ref, sg_ref, mk_ref), srcs,
         (dq_ref, df_ref, dv_ref, dg_ref, dlb_ref, dgn_ref), outs, scratch) = rd.split(refs, 11, 6)
        dstate = scratch[0]
        rd.run(srcs, outs, scratch, grid)
        first_of_group = (pl.program_id(1) == 0) & (pl.program_id(2) == 0)

        @pl.when(pl.program_id(2) == 0)
        def _():
            dstate[...] = jnp.zeros_like(dstate)

        @pl.when(first_of_group)
        def _():
            dlb_ref[...] = jnp.zeros_like(dlb_ref)

        @pl.when(first_of_group & (pl.program_id(0) == 0))
        def _():
            dgn_ref[...] = jnp.zeros_like(dgn_ref)

        prefix, sign, mask = p_ref[...], [sg_ref[l] for l in range(levels)], [mk_ref[l] for l in range(levels)]
        for j in range(hp):
            cols = slice(j * kd, (j + 1) * kd)
            _, vjp = jax.vjp(lambda q, f, v, g, st, lb, gn: _hgrn_step(q, f, v, g, st, lb, gn, prefix, sign, mask),
                             q_ref[0, :, cols], f_ref[0, :, cols], v_ref[0, :, cols], g_ref[0, :, cols], st_ref[0, j, 0],
                             lb_ref[:, cols], gn_ref[...])
            dq, df, dv, dg, dst, dlb, dgn = vjp((dz_ref[0, :, cols], dstate[j]))
            dq_ref[0, :, cols] = dq.astype(dq_ref.dtype)
            df_ref[0, :, cols] = df.astype(df_ref.dtype)
            dv_ref[0, :, cols] = dv.astype(dv_ref.dtype)
            dg_ref[0, :, cols] = dg.astype(dg_ref.dtype)
            dlb_ref[:, cols] += dlb
            dgn_ref[...] += dgn
            dstate[j] = dst

    def part(k):
        return pl.BlockSpec((1, c, hp * kd), lambda h, b, i: (b, nc - 1 - i, k * groups + h))

    shape = jax.ShapeDtypeStruct((bsz, s, nh * kd), BF16)
    lb_spec = pl.BlockSpec((1, hp * kd), lambda h, b, i: (0, h))
    gn_spec = pl.BlockSpec((1, kd), lambda h, b, i: (0, 0))
    res = pl.pallas_call(
        body, name="hgrn_bwd", grid=grid,
        in_specs=[part(0), part(1), part(2), part(3), pl.BlockSpec((1, hp, 1, kd, kd), lambda h, b, i: (b, h, nc - 1 - i, 0, 0)),
                  part(0), lb_spec, gn_spec] + _const_specs(tables) + rd.in_specs,
        out_specs=[part(0)] * 4 + [lb_spec, gn_spec] + rd.out_specs,
        out_shape=[shape] * 4 + [jax.ShapeDtypeStruct(lb.shape, F32), jax.ShapeDtypeStruct(gn.shape, F32)] + rd.out_shape,
        scratch_shapes=[pltpu.VMEM((hp, kd, kd), F32)] + rd.scratch,
        compiler_params=_params("arbitrary", "arbitrary", "arbitrary"),
    )(proj, proj, proj, proj, states, dz, lb, gn, *tables, *rd.srcs)
    return list(res[:4]), res[4], res[5], res[6:]


def cast_bf16(name, w):
    blk = pl.BlockSpec((1,) + w.shape[1:], lambda l: (l, 0, 0))

    def body(w_ref, o_ref):
        o_ref[...] = w_ref[...].astype(BF16)

    return pl.pallas_call(body, name=name, grid=(w.shape[0],), in_specs=[blk], out_specs=blk,
                          out_shape=jax.ShapeDtypeStruct(w.shape, BF16), compiler_params=_params("arbitrary"))(w)


def _lower_bounds(rows):
    m = functools.reduce(jnp.maximum, rows)
    e = [jnp.exp(r - m) for r in rows]
    z = functools.reduce(lambda a, b: a + b, e)
    soft = [x / z for x in e]
    out, run = [], jnp.zeros_like(rows[0])
    for sft in soft:
        run = run + sft
        out.append(run - soft[0])
    return out


def lower_bounds(lb):
    n = lb.shape[0]

    def body(lb_ref, o_ref):
        for i, r in enumerate(_lower_bounds([lb_ref[i:i + 1, :] for i in range(n)])):
            o_ref[i:i + 1, :] = r

    return pl.pallas_call(body, name="lower_bounds", out_shape=jax.ShapeDtypeStruct(lb.shape, F32),
                          compiler_params=_params())(lb)


def ada_fwd(c_all, ada_w, ada_b):
    nl, ns, d, cols = ada_w.shape
    n_ex = c_all.shape[0]

    def body(c_ref, w_ref, b_ref, o_ref):
        a = jax.nn.silu(c_ref[...]).astype(BF16)
        o_ref[0] = _dg(a, w_ref[0].astype(BF16), 1, 0) + b_ref[0]

    return pl.pallas_call(
        body, name="ada_fwd", grid=(nl * ns,),
        in_specs=[pl.BlockSpec((n_ex, d), lambda i: (0, 0)), pl.BlockSpec((1, d, cols), lambda i: (i, 0, 0)),
                  pl.BlockSpec((1, 1, cols), lambda i: (i, 0, 0))],
        out_specs=pl.BlockSpec((1, n_ex, cols), lambda i: (i, 0, 0)),
        out_shape=jax.ShapeDtypeStruct((nl * ns, n_ex, cols), F32), compiler_params=_params("arbitrary"),
    )(c_all, ada_w.reshape(nl * ns, d, cols), ada_b.reshape(nl * ns, 1, cols))


def ada_bwd(c_all, dmod):
    n, n_ex, cols = dmod.shape
    d = c_all.shape[1]

    def body(c_ref, g_ref, dw_ref, db_ref):
        a = jax.nn.silu(c_ref[...]).astype(BF16)
        g = g_ref[0]
        dw_ref[0] = _dg(a, g.astype(BF16), 0, 0)
        db_ref[0] = jnp.sum(g, 0, keepdims=True)

    return pl.pallas_call(
        body, name="ada_bwd", grid=(n,),
        in_specs=[pl.BlockSpec((n_ex, d), lambda i: (0, 0)), pl.BlockSpec((1, n_ex, cols), lambda i: (i, 0, 0))],
        out_specs=[pl.BlockSpec((1, d, cols), lambda i: (i, 0, 0)), pl.BlockSpec((1, 1, cols), lambda i: (i, 0, 0))],
        out_shape=[jax.ShapeDtypeStruct((n, d, cols), F32), jax.ShapeDtypeStruct((n, 1, cols), F32)],
        compiler_params=_params("arbitrary"),
    )(c_all, dmod)


def _adam_math(g, w, m, v):
    m = ADAM_B1 * m + (1.0 - ADAM_B1) * g
    v = ADAM_B2 * v + (1.0 - ADAM_B2) * jnp.square(g)
    m_hat = m / (1.0 - ADAM_B1 ** ADAM_STEP)
    v_hat = v / (1.0 - ADAM_B2 ** ADAM_STEP)
    delta = -ADAM_LR * (m_hat / (jnp.sqrt(v_hat) + ADAM_EPS) + ADAM_WD * w)
    return delta, m, v


def adam(name, gstack, w, m, v):
    shape = w.shape
    n, cols = gstack.shape[0], shape[-1]
    rows = math.prod(shape[:-1])
    tr = _pick_rows(rows, max(8, (2 * 1024 * 1024) // (4 * cols * n)))

    def body(g_ref, w_ref, m_ref, v_ref, go_ref, d_ref, mo_ref, vo_ref):
        g = g_ref[0].astype(F32)
        for i in range(1, n):
            g = g + g_ref[i].astype(F32)
        delta, m1, v1 = _adam_math(g, w_ref[...], m_ref[...], v_ref[...])
        go_ref[...] = g
        d_ref[...] = delta
        mo_ref[...] = m1
        vo_ref[...] = v1

    blk = pl.BlockSpec((tr, cols), lambda i: (i, 0))
    out = pl.pallas_call(
        body, name=name, grid=(rows // tr,),
        in_specs=[pl.BlockSpec((n, tr, cols), lambda i: (0, i, 0)), blk, blk, blk],
        out_specs=[blk] * 4, out_shape=[jax.ShapeDtypeStruct((rows, cols), F32)] * 4,
        compiler_params=_params("arbitrary"),
    )(gstack.reshape(n, rows, cols), w.reshape(rows, cols), m.reshape(rows, cols), v.reshape(rows, cols))
    return [o.reshape(shape) for o in out]


def adam_layers(name, gs, w, m, v):
    shape = w.shape
    nl, n, cols = len(gs), gs[0].shape[0], shape[-1]
    rows = math.prod(shape[1:-1])
    tr = _pick_rows(rows, max(16, (2 * 1024 * 1024) // (4 * cols * n)))
    nt = rows // tr

    def body(*refs):
        g_refs, (w_ref, m_ref, v_ref, go_ref, d_ref, mo_ref, vo_ref) = refs[:nl], refs[nl:]
        for j in range(nl):
            @pl.when(pl.program_id(0) == j)
            def _(j=j):
                g = g_refs[j][0].astype(F32)
                for i in range(1, n):
                    g = g + g_refs[j][i].astype(F32)
                delta, m1, v1 = _adam_math(g, w_ref[...], m_ref[...], v_ref[...])
                go_ref[...] = g
                d_ref[...] = delta
                mo_ref[...] = m1
                vo_ref[...] = v1

    def g_spec(j):
        return pl.BlockSpec((n, tr, cols), lambda l, i: (0, jnp.where(l == j, i, jnp.where(l < j, 0, nt - 1)), 0))

    blk = pl.BlockSpec((tr, cols), lambda l, i: (l * nt + i, 0))
    out = pl.pallas_call(
        body, name=name, grid=(nl, nt),
        in_specs=[g_spec(j) for j in range(nl)] + [blk, blk, blk],
        out_specs=[blk] * 4, out_shape=[jax.ShapeDtypeStruct((nl * rows, cols), F32)] * 4,
        compiler_params=_params("arbitrary", "arbitrary"),
    )(*[g.reshape(n, rows, cols) for g in gs], w.reshape(nl * rows, cols), m.reshape(nl * rows, cols), v.reshape(nl * rows, cols))
    return [o.reshape(shape) for o in out]


def adam_lb(gstack, lb, m, v):
    n, nl = gstack.shape[0], lb.shape[0]

    def body(g_ref, w_ref, m_ref, v_ref, go_ref, d_ref, mo_ref, vo_ref):
        rows = [w_ref[i:i + 1, :] for i in range(nl)]
        ct = []
        for i in range(nl):
            g = g_ref[0, i:i + 1, :]
            for j in range(1, n):
                g = g + g_ref[j, i:i + 1, :]
            ct.append(g)
        _, vjp = jax.vjp(_lower_bounds, rows)
        (grads,) = vjp(ct)
        for i in range(nl):
            delta, m1, v1 = _adam_math(grads[i], rows[i], m_ref[i:i + 1, :], v_ref[i:i + 1, :])
            go_ref[i:i + 1, :] = grads[i]
            d_ref[i:i + 1, :] = delta
            mo_ref[i:i + 1, :] = m1
            vo_ref[i:i + 1, :] = v1

    return pl.pallas_call(body, name="adam_hgrn_lb", out_shape=[jax.ShapeDtypeStruct(lb.shape, F32)] * 4,
                          compiler_params=_params())(gstack, lb, m, v)


class _Ride:
    def __init__(self, items):
        self.items = list(items)
        n = len(self.items)
        self.srcs = [src for src, _ in self.items]
        self.in_specs = [pl.BlockSpec(memory_space=pl.ANY)] * n
        self.out_specs = [pl.BlockSpec(memory_space=pl.ANY)] * n
        self.out_shape = [jax.ShapeDtypeStruct(((N_DEV,) + s.shape) if mode == "gather" else s.shape, s.dtype)
                          for s, mode in self.items]
        self.scratch = [pltpu.SemaphoreType.DMA((n, N_DEV - 1)), pltpu.SemaphoreType.DMA((n, N_DEV - 1)),
                        pltpu.SemaphoreType.DMA((n,))] if n else []

    def split(self, refs, n_in, n_out):
        n = len(self.items)
        a, b = n_in + n, n_in + 2 * n + n_out
        return refs[:n_in], refs[n_in:a], refs[a:a + n_out], refs[a + n_out:b], refs[b:]

    def _copies(self, srcs, outs, sems):
        send_sems, recv_sems, local_sems = sems
        x, y, c = lax.axis_index("x"), lax.axis_index("y"), lax.axis_index("c")
        me = 4 * x + 2 * y + c
        copies = []
        for i, (_, mode) in enumerate(self.items):
            mine = srcs[i] if mode == "gather" else srcs[i].at[me]
            copies.append(pltpu.make_async_copy(mine, outs[i].at[me], local_sems.at[i]))
            for p in range(1, N_DEV):
                px = 1 - x if p & 4 else x
                py = 1 - y if p & 2 else y
                pc = 1 - c if p & 1 else c
                part = srcs[i] if mode == "gather" else srcs[i].at[4 * px + 2 * py + pc]
                copies.append(pltpu.make_async_remote_copy(
                    src_ref=part, dst_ref=outs[i].at[me], send_sem=send_sems.at[i, p - 1], recv_sem=recv_sems.at[i, p - 1],
                    device_id=(px, py, pc), device_id_type=pl.DeviceIdType.MESH))
        return copies

    def run(self, srcs, outs, scratch, grid=()):
        if not self.items:
            return
        sems = scratch[len(scratch) - 3:]
        if not grid:
            copies = self._copies(srcs, outs, sems)
            for cp in copies:
                cp.start()
            for cp in copies:
                cp.wait()
            return
        ids = [pl.program_id(a) for a in range(len(grid))]
        first = functools.reduce(lambda a, b: a & b, [i == 0 for i in ids])
        last = functools.reduce(lambda a, b: a & b, [i == g - 1 for i, g in zip(ids, grid)])

        @pl.when(first)
        def _():
            for cp in self._copies(srcs, outs, sems):
                cp.start()

        @pl.when(last)
        def _():
            for cp in self._copies(srcs, outs, sems):
                cp.wait()


def exchange(name, items):
    rd = _Ride(items)

    def body(*refs):
        _, srcs, _, outs, scratch = rd.split(refs, 0, 0)
        rd.run(srcs, outs, scratch)

    return pl.pallas_call(body, name=name, in_specs=rd.in_specs, out_specs=rd.out_specs, out_shape=rd.out_shape,
                          scratch_shapes=rd.scratch)(*rd.srcs)


def _from_gather(name, g):
    if name in COL_SHARDED:
        _, k, n = g.shape
        return g.transpose(1, 0, 2).reshape(k, N_DEV * n)
    return g.reshape(-1, g.shape[-1])


def _to_slabs(name, w):
    if isinstance(w, tuple):
        per = N_DEV // len(w)
        return jnp.concatenate([p.reshape(p.shape[0], per, p.shape[1] // per).transpose(1, 0, 2) for p in w], axis=0)
    k, n = w.shape
    if name in COL_SHARDED:
        return w.reshape(k, N_DEV, n // N_DEV).transpose(1, 0, 2)
    return w.reshape(N_DEV, k // N_DEV, n)


def _w_in_internal(w):
    return jnp.pad(w, ((0, 0), (0, LANES - QK_ROPE)))


def _qb_internal(w, inverse=False):
    h, n, r2 = MLA_HEADS, QK_NOPE, QK_ROPE // 2
    lead = w.shape[:-1]
    if not inverse:
        w = w.reshape(lead + (h, n + 2 * r2))
        parts = [w[..., :n], w[..., n:n + r2], w[..., n + r2:]]
        return jnp.concatenate([p.reshape(lead + (-1,)) for p in parts], axis=-1)
    parts = [w[..., :h * n].reshape(lead + (h, n)), w[..., h * n:h * (n + r2)].reshape(lead + (h, r2)),
             w[..., h * (n + r2):].reshape(lead + (h, r2))]
    return jnp.concatenate(parts, axis=-1).reshape(lead + (-1,))


def _kvb_internal(w, inverse=False):
    h, n, vd = MLA_HEADS, QK_NOPE, V_HEAD
    lead = w.shape[:-1]
    if not inverse:
        w = w.reshape(lead + (h, n + vd))
        return jnp.concatenate([w[..., :n].reshape(lead + (-1,)), w[..., n:].reshape(lead + (-1,))], axis=-1)
    parts = [w[..., :h * n].reshape(lead + (h, n)), w[..., h * n:].reshape(lead + (h, vd))]
    return jnp.concatenate(parts, axis=-1).reshape(lead + (-1,))


def _mla_forward(h, w, tabs, ride=()):
    cos, sin = tabs
    r2 = MLA_HEADS * (QK_ROPE // 2)
    proj = mm3("mla_proj", h, w['w_in'])
    qn, kvn, krt = rowwise(
        "mla_mid", lambda rv, ev, gv: (f_mla_mid(rv, ev, gv), []),
        [_view(proj, 0, Q_LORA), _view(proj, Q_LORA, KV_LORA), _view(proj, Q_LORA + KV_LORA, LANES), _full(cos), _full(sin)],
        [], [w['q_norm'], w['kv_norm']], [(BF16, [Q_LORA]), (BF16, [KV_LORA]), (BF16, [r2, r2])], ts=ROW_TILE)
    q = mm3("mla_q", qn, w['w_qb'])
    kv = mm3("mla_kv", kvn, w['w_kvb'])
    qh, kh, vh = mla_heads(q, kv, krt, cos, sin)
    o, got = attn_fwd(qh, kh, vh, ride)
    y = mm3("mla_out", o, w['w_o'])
    return y, dict(h=h, proj=proj, qn=qn, kvn=kvn, qh=qh, kh=kh, vh=vh, o=o), got


def _mla_backward(dy, sv, w, tabs, ride=()):
    cos, sin = tabs
    r2 = MLA_HEADS * (QK_ROPE // 2)
    g = {}
    g['w_o'] = wgrad("mla_out_dw", sv['o'], dy)
    do = mm3("mla_out_dx", dy, w['w_o'], tb=True)
    dqh, dkh, dvh, got = attn_bwd(sv['qh'], sv['kh'], sv['vh'], do, ride)
    dq, dkv, dkrt = mla_heads_bwd(dqh, dkh, dvh, cos, sin)
    g['w_qb'] = wgrad("mla_q_dw", sv['qn'], dq)
    g['w_kvb'] = wgrad("mla_kv_dw", sv['kvn'], dkv)
    dqn = mm3("mla_q_dx", dq, w['w_qb'], tb=True)
    dkvn = mm3("mla_kv_dx", dkv, w['w_kvb'], tb=True)
    proj = sv['proj']
    (dproj,), _, (g['q_norm'], g['kv_norm']) = rowwise_bwd(
        "mla_mid_bwd", f_mla_mid,
        [_view(proj, 0, Q_LORA), _view(proj, Q_LORA, KV_LORA), _view(proj, Q_LORA + KV_LORA, LANES), _full(cos), _full(sin)],
        [], [w['q_norm'], w['kv_norm']], [(dqn, [Q_LORA]), (dkvn, [KV_LORA]), (dkrt, [r2, r2])],
        [(BF16, [0, 1, 2])], ts=ROW_TILE, n_diff=3)
    g['w_in'] = wgrad("mla_proj_dw", sv['h'], dproj)
    dh = mm3("mla_proj_dx", dproj, w['w_in'], tb=True)
    return dh, g, got


def _hgrn_forward(h, w, ride=()):
    proj = mm3("hgrn_proj", h, w['w_in'])
    z, states, got = hgrn_fwd(proj, w['lb'], w['g_norm'], ride)
    y = mm3("hgrn_out", z, w['w_o'])
    return y, dict(h=h, proj=proj, states=states, z=z), got


def _hgrn_backward(dy, sv, w, ride=()):
    g = {}
    g['w_o'] = wgrad("hgrn_out_dw", sv['z'], dy)
    dz = mm3("hgrn_out_dx", dy, w['w_o'], tb=True)
    dparts, g['lb'], g['g_norm'], got = hgrn_bwd(sv['proj'], sv['states'], dz, w['lb'], w['g_norm'], ride)
    g['w_in'] = tuple(wgrad("hgrn_proj_dw", sv['h'], p) for p in dparts)
    dh = mm3("hgrn_proj_dx", dparts, w['w_in'], tb=True)
    return dh, g, got


def _ffn_forward(h, w):
    ug, uu, a = ffn_in_act(h, w['w_in'])
    y = mm3("ffn_out", a, w['w_out'])
    return y, dict(h=h, ug=ug, uu=uu, a=a)


def _ffn_backward(dy, sv, w):
    g = {}
    g['w_out'] = wgrad("ffn_out_dw", sv['a'], dy)
    dug, duu = ffn_out_dx_act(dy, w['w_out'], sv['ug'], sv['uu'])
    g['w_in'] = (wgrad("ffn_in_dw", sv['h'], dug), wgrad("ffn_in_dw", sv['h'], duu))
    dh = mm3("ffn_in_dx", [dug, duu], w['w_in'], tb=True)
    return dh, g


def kernel(x, c, positions, mla_w_in, mla_q_norm, mla_w_qb, mla_kv_norm, mla_w_kvb, mla_w_o, hgrn_lb, hgrn_w_in, hgrn_g_norm, hgrn_w_o, ffn_w_in, ffn_w_out, ada_w, ada_b, ln_g, ln_b, loss_target, m_mla_w_in, m_mla_q_norm, m_mla_w_qb, m_mla_kv_norm, m_mla_w_kvb, m_mla_w_o, m_hgrn_lb, m_hgrn_w_in, m_hgrn_g_norm, m_hgrn_w_o, m_ffn_w_in, m_ffn_w_out, m_ada_w, m_ada_b, m_ln_g, m_ln_b, v_mla_w_in, v_mla_q_norm, v_mla_w_qb, v_mla_kv_norm, v_mla_w_kvb, v_mla_w_o, v_hgrn_lb, v_hgrn_w_in, v_hgrn_g_norm, v_hgrn_w_o, v_ffn_w_in, v_ffn_w_out, v_ada_w, v_ada_b, v_ln_g, v_ln_b):
    W = dict(zip(WEIGHTS, (mla_w_in, mla_q_norm, mla_w_qb, mla_kv_norm, mla_w_kvb, mla_w_o, hgrn_lb, hgrn_w_in, hgrn_g_norm,
                           hgrn_w_o, ffn_w_in, ffn_w_out, ada_w, ada_b, ln_g, ln_b)))
    M1 = dict(zip(WEIGHTS, (m_mla_w_in, m_mla_q_norm, m_mla_w_qb, m_mla_kv_norm, m_mla_w_kvb, m_mla_w_o, m_hgrn_lb, m_hgrn_w_in,
                            m_hgrn_g_norm, m_hgrn_w_o, m_ffn_w_in, m_ffn_w_out, m_ada_w, m_ada_b, m_ln_g, m_ln_b)))
    M2 = dict(zip(WEIGHTS, (v_mla_w_in, v_mla_q_norm, v_mla_w_qb, v_mla_kv_norm, v_mla_w_kvb, v_mla_w_o, v_hgrn_lb, v_hgrn_w_in,
                            v_hgrn_g_norm, v_hgrn_w_o, v_ffn_w_in, v_ffn_w_out, v_ada_w, v_ada_b, v_ln_g, v_ln_b)))
    bsz, seq, d = x.shape
    depth, n_mla, n_hgrn = ffn_w_in.shape[0], mla_w_in.shape[0], hgrn_w_in.shape[0]
    n_sub = 2 * depth

    big = COL_SHARDED + ROW_SHARDED
    wb = {n: cast_bf16("cast_" + n, W[n]) for n in big}

    def mixer_names(layer):
        mixer = ['mla_w_in', 'mla_w_qb', 'mla_w_kvb', 'mla_w_o'] if layer % 2 == 0 else ['hgrn_w_in', 'hgrn_w_o']
        return [(n, layer // 2) for n in mixer]

    def carried(layer):
        return [('ffn_w_in', layer), ('ffn_w_out', layer)] + (mixer_names(layer + 1) if layer + 1 < depth else [])

    def weight_items(names):
        return [(wb[n][j], "gather") for n, j in names]

    G = {}
    internal = {'mla_w_in': _w_in_internal, 'mla_w_qb': _qb_internal, 'mla_w_kvb': _kvb_internal}

    def take_weights(names, got):
        for (n, j), a in zip(names, got):
            G[n, j] = internal.get(n, lambda w: w)(_from_gather(n, a))

    lower_shard = lower_bounds(hgrn_lb)
    got = exchange("gather_first", [(lower_shard, "gather"), (ln_g, "gather"), (ln_b, "gather"), (c, "gather")]
                   + weight_items(mixer_names(0)))
    lower_all = got[0].transpose(1, 0, 2).reshape(n_hgrn, -1)
    ln_g_all = got[1].transpose(1, 2, 0, 3).reshape(depth, 2, d)
    ln_b_all = got[2].transpose(1, 2, 0, 3).reshape(depth, 2, d)
    c_all = got[3].reshape(N_DEV * bsz, d)
    take_weights(mixer_names(0), got[4:])

    cols = ada_w.shape[-1]
    mod_loc = ada_fwd(c_all, ada_w, ada_b)
    (mod_got,) = exchange("scatter_mod", [(mod_loc.reshape(n_sub, N_DEV, bsz, cols).transpose(1, 0, 2, 3), "a2a")])
    mod = mod_got.transpose(1, 2, 0, 3).reshape(n_sub, bsz, 1, 3 * d)
    shift = [mod[k, :, :, 0:d] for k in range(n_sub)]
    scale = [mod[k, :, :, d:2 * d] for k in range(n_sub)]
    gate = [mod[k, :, :, 2 * d:] for k in range(n_sub)]
    lng = [ln_g_all[k // 2, k % 2][None, :] for k in range(n_sub)]
    lnb = [ln_b_all[k // 2, k % 2][None, :] for k in range(n_sub)]

    tabs = rope_tables(positions)

    def sub_weights(k):
        layer, j = k // 2, k // 4
        if k % 2:
            return 'ffn', layer, dict(w_in=G['ffn_w_in', layer], w_out=G['ffn_w_out', layer])
        if layer % 2 == 0:
            return 'mla', j, dict(w_in=G['mla_w_in', j], q_norm=mla_q_norm[j][None, :], w_qb=G['mla_w_qb', j],
                                  kv_norm=mla_kv_norm[j][None, :], w_kvb=G['mla_w_kvb', j], w_o=G['mla_w_o', j])
        return 'hgrn', j, dict(w_in=G['hgrn_w_in', j], lb=lower_all[j][None, :], g_norm=hgrn_g_norm[j][None, :],
                               w_o=G['hgrn_w_o', j])

    (h,) = rowwise("mod_first", lambda rv, ev, gv: (f_mod(rv, ev, gv), []), [_full(x)], [scale[0], shift[0]], [],
                   [(BF16, [d])], ts=ROW_TILE)
    xs, ys, saved = [x], [], []
    loss_acc = None
    for k in range(n_sub):
        kind, _, w = sub_weights(k)
        ride = weight_items(carried(k // 2)) if k % 2 == 0 else []
        if kind == 'ffn':
            y, sv = _ffn_forward(h, w)
        elif kind == 'mla':
            y, sv, got = _mla_forward(h, w, tabs, ride)
        else:
            y, sv, got = _hgrn_forward(h, w, ride)
        if ride:
            take_weights(carried(k // 2), got)
        ys.append(y)
        saved.append(sv)
        if k + 1 < n_sub:
            xn, h = rowwise("ln_mod", lambda rv, ev, gv: (f_ln_mod(rv, ev, gv), []), [_full(xs[k]), _full(y)],
                            [gate[k], scale[k + 1], shift[k + 1]], [lng[k], lnb[k]], [(F32, [d]), (BF16, [d])], ts=ROW_TILE)
            xs.append(xn)
        else:
            def loss_rows(rv, ev, gv):
                (row,) = f_ln_loss(rv, ev, gv)
                return [], [jnp.broadcast_to(jnp.sum(row, keepdims=True), (1, LANES))]
            (loss_acc,) = rowwise("ln_loss", loss_rows, [_full(xs[k]), _full(y), _full(loss_target)], [gate[k]], [lng[k], lnb[k]],
                                  [], ts=ROW_TILE, accs=[LANES])
    loss = lax.psum(loss_acc[0, 0], ("x", "y", "c"))

    d_shift, d_scale, d_gate = [None] * n_sub, [None] * n_sub, [None] * n_sub
    d_lng, d_lnb = [None] * n_sub, [None] * n_sub
    part = {n: [None] * W[n].shape[0] for n in ['mla_q_norm', 'mla_kv_norm', 'hgrn_g_norm']}
    recv = {n: [None] * W[n].shape[0] for n in big}
    d_lower = [None] * n_hgrn
    k = n_sub - 1
    (dx, dy), (d_gate[k],), (d_lng[k], d_lnb[k]) = rowwise_bwd(
        "ln_loss_bwd", f_ln_loss, [_full(xs[k]), _full(ys[k]), _full(loss_target)], [gate[k]], [lng[k], lnb[k]], [],
        [(F32, [0]), (BF16, [1])], ts=ROW_TILE, n_diff=2, unit_ct=1)
    grad_x = None
    mine = {}

    def take_grads(names, got):
        for (n, j), a in zip(names, got):
            recv[n][j] = a

    def grad_items(names):
        return [(_to_slabs(n, mine[n, jj]), "a2a") for n, jj in names]

    for k in range(n_sub - 1, -1, -1):
        kind, j, w = sub_weights(k)
        ride = grad_items(carried(k // 2)) if kind != 'ffn' else []
        if kind == 'ffn':
            dh, g = _ffn_backward(dy, saved[k], w)
            new = {('ffn_w_in', j): g['w_in'], ('ffn_w_out', j): g['w_out']}
        elif kind == 'mla':
            dh, g, got = _mla_backward(dy, saved[k], w, tabs, ride)
            new = {('mla_w_in', j): g['w_in'][:, :mla_w_in.shape[-1] * N_DEV], ('mla_w_qb', j): _qb_internal(g['w_qb'], inverse=True),
                   ('mla_w_kvb', j): _kvb_internal(g['w_kvb'], inverse=True), ('mla_w_o', j): g['w_o']}
            part['mla_q_norm'][j], part['mla_kv_norm'][j] = g['q_norm'][0], g['kv_norm'][0]
        else:
            dh, g, got = _hgrn_backward(dy, saved[k], w, ride)
            new = {('hgrn_w_in', j): g['w_in'], ('hgrn_w_o', j): g['w_o']}
            part['hgrn_g_norm'][j] = g['g_norm'][0]
            d_lower[j] = g['lb'][0]
        if kind != 'ffn':
            take_grads(carried(k // 2), got)
        mine.update(new)
        if k:
            (dx, dy), (d_gate[k - 1], d_scale[k], d_shift[k]), (d_lng[k - 1], d_lnb[k - 1]) = rowwise_bwd(
                "ln_mod_bwd", f_ln_mod, [_full(xs[k - 1]), _full(ys[k - 1])], [gate[k - 1], scale[k], shift[k]],
                [lng[k - 1], lnb[k - 1]], [(dx, [d]), (dh, [d])], [(F32, [0]), (BF16, [1])], ts=ROW_TILE, n_diff=2)
        else:
            (grad_x,), (d_scale[0], d_shift[0]), _ = rowwise_bwd(
                "mod_first_bwd", f_mod_with_x, [_full(x)], [scale[0], shift[0]], [], [(dh, [d]), (dx, [d])],
                [(F32, [0])], ts=ROW_TILE, n_diff=1)

    waiting = mixer_names(0)
    slabs = grad_items(waiting)
    slabs.append((jnp.stack(d_lower).reshape(n_hgrn, N_DEV, -1).transpose(1, 0, 2), "a2a"))
    for parts in (d_lng, d_lnb):
        full = jnp.stack([p[0] for p in parts]).reshape(depth, 2, N_DEV, d // N_DEV)
        slabs.append((full.transpose(2, 0, 1, 3), "a2a"))
    dmod = jnp.concatenate([jnp.stack(d_shift), jnp.stack(d_scale), jnp.stack(d_gate)], axis=-1)
    slabs.append((dmod.reshape(n_sub, bsz, N_DEV, cols).transpose(2, 0, 1, 3), "a2a"))
    small = ['mla_q_norm', 'mla_kv_norm', 'hgrn_g_norm']
    slabs += [(jnp.stack(part[n]), "gather") for n in small]
    got = exchange("scatter_last", slabs)
    take_grads(waiting, got)
    stacks = dict(zip(['hgrn_lb', 'ln_g', 'ln_b', 'dmod'] + small, got[len(waiting):]))

    dmod_all = stacks['dmod'].transpose(1, 0, 2, 3).reshape(n_sub, N_DEV * bsz, cols)
    g_ada_w, g_ada_b = ada_bwd(c_all, dmod_all)
    stacks['ada_w'] = g_ada_w.reshape((1,) + ada_w.shape)
    stacks['ada_b'] = g_ada_b.reshape((1,) + ada_b.shape)

    res = {}
    for n in WEIGHTS:
        if n == 'hgrn_lb':
            res[n] = adam_lb(stacks[n], W[n], M1[n], M2[n])
        elif n in big:
            res[n] = adam_layers("adam_" + n, recv[n], W[n], M1[n], M2[n])
        else:
            res[n] = adam("adam_" + n, stacks[n], W[n], M1[n], M2[n])
    return (loss, grad_x, *[res[n][0] for n in WEIGHTS], *[res[n][1] for n in WEIGHTS], *[res[n][2] for n in WEIGHTS],
            *[res[n][3] for n in WEIGHTS])
```

```python
import functools
import math

import numpy as np
import jax
import jax.numpy as jnp
from jax import lax
from jax.experimental import pallas as pl
from jax.experimental.pallas import tpu as pltpu

F32 = jnp.float32
BF16 = jnp.bfloat16

N_DEV = 8
LANES = 128
VMEM_LIMIT = 52 * 1024 * 1024

D_MODEL = 1024
DEPTH = 4
MLA_HEADS = 16
QK_NOPE = 64
QK_ROPE = 32
V_HEAD = 64
Q_LORA = 768
KV_LORA = 256
ROPE_THETA = 10000.0
HGRN_EXPAND = 128
HGRN_CHUNK = 128
HGRN_HEADS_PER_STEP = 4
D_FF = 2816
ALPHA = (2.0 * DEPTH) ** 0.25
LN_EPS = 1e-5
RMS_EPS = 1e-6
ADAM_LR = 0.001
ADAM_B1 = 0.9
ADAM_B2 = 0.999
ADAM_EPS = 1e-08
ADAM_WD = 0.01
ADAM_STEP = 10

ATTN_TQ = 512
ROW_TILE = 256

WEIGHTS = ['mla_w_in', 'mla_q_norm', 'mla_w_qb', 'mla_kv_norm', 'mla_w_kvb', 'mla_w_o', 'hgrn_lb', 'hgrn_w_in',
           'hgrn_g_norm', 'hgrn_w_o', 'ffn_w_in', 'ffn_w_out', 'ada_w', 'ada_b', 'ln_g', 'ln_b']
COL_SHARDED = ['mla_w_in', 'mla_w_qb', 'mla_w_kvb', 'hgrn_w_in', 'ffn_w_in']
ROW_SHARDED = ['mla_w_o', 'hgrn_w_o', 'ffn_w_out']


def _params(*sem):
    if sem:
        return pltpu.CompilerParams(dimension_semantics=sem, vmem_limit_bytes=VMEM_LIMIT)
    return pltpu.CompilerParams(vmem_limit_bytes=VMEM_LIMIT)


def _pick(n, cap):
    best = None
    for t in range(LANES, min(n, cap) + 1, LANES):
        if n % t == 0:
            best = t
    return best or n


def _pick_rows(n, cap):
    best = None
    for t in range(8, min(n, cap) + 1, 8):
        if n % t == 0:
            best = t
    return best or n


def matmul(name, a, b, *, ta=False, tb=False, out_dtype=F32, tm_cap=1024, tn_cap=1536, tk_cap=2048):
    parts = list(a) if isinstance(a, (list, tuple)) else [a]
    n_parts = len(parts)
    assert n_parts == 1 or not ta
    (kp, m) = parts[0].shape if ta else parts[0].shape[::-1]
    (n, k2) = b.shape if tb else b.shape[::-1]
    assert kp * n_parts == k2, (name, parts[0].shape, b.shape)
    tm, tn, tk = _pick(m, tm_cap), _pick(n, tn_cap), _pick(kp, tk_cap)
    nkp = kp // tk
    nk = nkp * n_parts
    dims = (((0 if ta else 1,), (1 if tb else 0,)), ((), ()))

    def prod(a_ref, b_ref):
        return lax.dot_general(a_ref[...].astype(BF16), b_ref[...].astype(BF16), dims, preferred_element_type=F32)

    if nk == 1:
        def body(a_ref, b_ref, o_ref):
            o_ref[...] = prod(a_ref, b_ref).astype(o_ref.dtype)
        scratch = []
    else:
        def body(*refs):
            a_refs, (b_ref, o_ref, acc_ref) = refs[:n_parts], refs[n_parts:]
            k = pl.program_id(2)

            @pl.when(k == 0)
            def _():
                acc_ref[...] = jnp.zeros_like(acc_ref)

            if n_parts == 1:
                acc_ref[...] += prod(a_refs[0], b_ref)
            else:
                for p in range(n_parts):
                    @pl.when((k >= p * nkp) & (k < (p + 1) * nkp))
                    def _(p=p):
                        acc_ref[...] += prod(a_refs[p], b_ref)

            @pl.when(k == nk - 1)
            def _():
                o_ref[...] = acc_ref[...].astype(o_ref.dtype)
        scratch = [pltpu.VMEM((tm, tn), F32)]

    if ta:
        a_specs = [pl.BlockSpec((tk, tm), lambda i, j, k: (k, i))]
    elif n_parts == 1:
        a_specs = [pl.BlockSpec((tm, tk), lambda i, j, k: (i, k))]
    else:
        a_specs = [pl.BlockSpec((tm, tk), lambda i, j, k, p=p: (i, jnp.clip(k - p * nkp, 0, nkp - 1))) for p in range(n_parts)]
    b_spec = pl.BlockSpec((tn, tk), lambda i, j, k: (j, k)) if tb else pl.BlockSpec((tk, tn), lambda i, j, k: (k, j))
    return pl.pallas_call(
        body, name=name, grid=(m // tm, n // tn, nk),
        in_specs=a_specs + [b_spec], out_specs=pl.BlockSpec((tm, tn), lambda i, j, k: (i, j)),
        out_shape=jax.ShapeDtypeStruct((m, n), out_dtype), scratch_shapes=scratch,
        compiler_params=_params("parallel", "parallel", "arbitrary"),
    )(*parts, b)


def mm3(name, a3, w, **kw):
    parts = list(a3) if isinstance(a3, (list, tuple)) else [a3]
    bsz, s = parts[0].shape[:2]
    flat = [p.reshape(bsz * s, p.shape[-1]) for p in parts]
    out = matmul(name, flat if len(flat) > 1 else flat[0], w, **kw)
    return out.reshape(bsz, s, out.shape[-1])


def ffn_in_act(h3, w_in, ride=()):
    bsz, s, k = h3.shape
    m, dff = bsz * s, w_in.shape[1] // 2
    tm, tn = _pick(m, 512), _pick(dff, 1536)
    nj = dff // tn
    grid = (nj, m // tm)
    rd = _Ride(ride)

    def body(*refs):
        (h_ref, wg_ref, wu_ref), srcs, (ug_ref, uu_ref, a_ref), outs, scratch = rd.split(refs, 3, 3)
        rd.run(srcs, outs, scratch, grid)
        hv = h_ref[...].astype(BF16)
        ug = _dg(hv, wg_ref[...].astype(BF16), 1, 0)
        uu = _dg(hv, wu_ref[...].astype(BF16), 1, 0)
        ug_ref[...] = ug.astype(ug_ref.dtype)
        uu_ref[...] = uu.astype(uu_ref.dtype)
        a_ref[...] = (jax.nn.silu(ug) * uu).astype(a_ref.dtype)

    out = pl.BlockSpec((tm, tn), lambda j, i: (i, j))
    res = pl.pallas_call(
        body, name="ffn_in_act", grid=grid,
        in_specs=[pl.BlockSpec((tm, k), lambda j, i: (i, 0)), pl.BlockSpec((k, tn), lambda j, i: (0, j)),
                  pl.BlockSpec((k, tn), lambda j, i: (0, nj + j))] + rd.in_specs,
        out_specs=[out, out, out] + rd.out_specs,
        out_shape=[jax.ShapeDtypeStruct((m, dff), BF16)] * 3 + rd.out_shape, scratch_shapes=rd.scratch,
        compiler_params=_params("arbitrary", "arbitrary"),
    )(h3.reshape(m, k), w_in, w_in, *rd.srcs)
    return [r.reshape(bsz, s, dff) for r in res[:3]], res[3:]


def ffn_out_dx_act(dy3, w_out, ug, uu):
    bsz, s, d = dy3.shape
    m, dff = bsz * s, w_out.shape[0]
    tm, tn = _pick(m, 512), _pick(dff, 1536)

    def body(dy_ref, w_ref, ug_ref, uu_ref, dg_ref, du_ref):
        da = _dg(dy_ref[...].astype(BF16), w_ref[...].astype(BF16), 1, 1)
        _, vjp = jax.vjp(lambda gate, up: jax.nn.silu(gate) * up, ug_ref[...].astype(F32), uu_ref[...].astype(F32))
        dg, du = vjp(da)
        dg_ref[...] = dg.astype(dg_ref.dtype)
        du_ref[...] = du.astype(du_ref.dtype)

    blk = pl.BlockSpec((tm, tn), lambda j, i: (i, j))
    res = pl.pallas_call(
        body, name="ffn_out_dx_act", grid=(dff // tn, m // tm),
        in_specs=[pl.BlockSpec((tm, d), lambda j, i: (i, 0)), pl.BlockSpec((tn, d), lambda j, i: (j, 0)), blk, blk],
        out_specs=[blk, blk], out_shape=[jax.ShapeDtypeStruct((m, dff), BF16)] * 2,
        compiler_params=_params("arbitrary", "arbitrary"),
    )(dy3.reshape(m, d), w_out, ug.reshape(m, dff), uu.reshape(m, dff))
    return [r.reshape(bsz, s, dff) for r in res]


def wgrad(name, a3, g3):
    bsz, s, k = a3.shape
    return matmul(name, a3.reshape(bsz * s, k), g3.reshape(bsz * s, g3.shape[-1]), ta=True, out_dtype=BF16)


def _dg(a, b, ca, cb, **kw):
    return lax.dot_general(a, b, (((ca,), (cb,)), ((), ())), preferred_element_type=F32, **kw)


@functools.partial(jax.custom_vjp, nondiff_argnums=(2, 3))
def bdot(a, b, ca, cb):
    return _dg(a.astype(BF16), b.astype(BF16), ca, cb)


def _bdot_fwd(a, b, ca, cb):
    return bdot(a, b, ca, cb), (a, b)


def _bdot_bwd(ca, cb, res, g):
    a, b = res
    a16, b16, g16 = a.astype(BF16), b.astype(BF16), g.astype(BF16)
    if ca == 1:
        da = _dg(g16, b16, 1, 1 if cb == 0 else 0)
    else:
        da = _dg(b16, g16, 1 if cb == 0 else 0, 1)
    if cb == 0:
        db = _dg(a16, g16, 0 if ca == 1 else 1, 0)
    else:
        db = _dg(g16, a16, 0, 0 if ca == 1 else 1)
    return da, db


bdot.defvjp(_bdot_fwd, _bdot_bwd)


def hdot(a, b, ca=1, cb=0):
    return _dg(a, b, ca, cb, precision=lax.Precision.HIGHEST)


def _row_specs(rows, exs, globs, ts):
    specs = [pl.BlockSpec((1, ts, w), lambda b, s, j=j: (b, s, j)) for (_, j, w) in rows]
    specs += [pl.BlockSpec((1, 1, e.shape[-1]), lambda b, s: (b, 0, 0)) for e in exs]
    specs += [pl.BlockSpec((1, g.shape[-1]), lambda b, s: (0, 0)) for g in globs]
    return specs


def _store_pieces(o_ref, pieces, widths):
    off = 0
    for p, w in zip(pieces, widths):
        o_ref[0, :, off:off + w] = p.astype(o_ref.dtype)
        off += w


def _load_pieces(c_ref, widths):
    out, off = [], 0
    for w in widths:
        out.append(c_ref[0, :, off:off + w].astype(F32))
        off += w
    return out


def rowwise(name, f, rows, exs, globs, outs, *, ts, accs=()):
    bsz, s = rows[0][0].shape[:2]
    ts = min(ts, s)
    n_r, n_e, n_g, n_o = len(rows), len(exs), len(globs), len(outs)

    def body(*refs):
        rv = [r[0].astype(F32) for r in refs[:n_r]]
        ev = [e[0] for e in refs[n_r:n_r + n_e]]
        gv = [g[...] for g in refs[n_r + n_e:n_r + n_e + n_g]]
        o_refs = refs[n_r + n_e + n_g:n_r + n_e + n_g + n_o]
        a_refs = refs[n_r + n_e + n_g + n_o:]
        pieces, sums = f(rv, ev, gv)
        idx = 0
        for o_ref, (_, ws) in zip(o_refs, outs):
            _store_pieces(o_ref, pieces[idx:idx + len(ws)], ws)
            idx += len(ws)
        if accs:
            @pl.when((pl.program_id(0) == 0) & (pl.program_id(1) == 0))
            def _():
                for a_ref in a_refs:
                    a_ref[...] = jnp.zeros_like(a_ref)
            for a_ref, val in zip(a_refs, sums):
                a_ref[...] += val

    out_specs = [pl.BlockSpec((1, ts, sum(ws)), lambda b, s: (b, s, 0)) for (_, ws) in outs]
    out_specs += [pl.BlockSpec((1, w), lambda b, s: (0, 0)) for w in accs]
    out_shape = [jax.ShapeDtypeStruct((bsz, s, sum(ws)), dt) for (dt, ws) in outs]
    out_shape += [jax.ShapeDtypeStruct((1, w), F32) for w in accs]
    return pl.pallas_call(
        body, name=name, grid=(bsz, s // ts),
        in_specs=_row_specs(rows, exs, globs, ts), out_specs=out_specs, out_shape=out_shape,
        compiler_params=_params("arbitrary", "arbitrary"),
    )(*[r[0] for r in rows], *exs, *globs)


def rowwise_bwd(name, f, rows, exs, globs, cts, d_groups, *, ts, n_diff, unit_ct=0):
    bsz, s = rows[0][0].shape[:2]
    ts = min(ts, s)
    n_r, n_e, n_g, n_c = len(rows), len(exs), len(globs), len(cts)
    n_d = len(d_groups)

    def body(*refs):
        rv = [r[0].astype(F32) for r in refs[:n_r]]
        ev = [e[0] for e in refs[n_r:n_r + n_e]]
        gv = [g[...] for g in refs[n_r + n_e:n_r + n_e + n_g]]
        base = n_r + n_e + n_g
        c_refs = refs[base:base + n_c]
        d_refs = refs[base + n_c:base + n_c + n_d]
        de_refs = refs[base + n_c + n_d:base + n_c + n_d + n_e]
        dg_refs = refs[base + n_c + n_d + n_e:]
        fixed = rv[n_diff:]
        out, vjp = jax.vjp(lambda r, e, g: f(r + fixed, e, g), rv[:n_diff], ev, gv)
        ct = []
        for c_ref, (_, ws) in zip(c_refs, cts):
            ct += _load_pieces(c_ref, ws)
        ct += [jnp.ones_like(o) for o in out[len(ct):]]
        assert len(ct) == len(out) and len(out) - unit_ct == sum(len(ws) for _, ws in cts), name
        d_r, d_e, d_g = vjp(ct)
        for d_ref, (_, idxs) in zip(d_refs, d_groups):
            _store_pieces(d_ref, [d_r[i] for i in idxs], [rows[i][2] for i in idxs])
        first_s = pl.program_id(1) == 0
        if n_e:
            @pl.when(first_s)
            def _():
                for r in de_refs:
                    r[...] = jnp.zeros_like(r)
            for r, val in zip(de_refs, d_e):
                r[0] += val
        if n_g:
            @pl.when(first_s & (pl.program_id(0) == 0))
            def _():
                for r in dg_refs:
                    r[...] = jnp.zeros_like(r)
            for r, val in zip(dg_refs, d_g):
                r[...] += val

    in_specs = _row_specs(rows, exs, globs, ts)
    in_specs += [pl.BlockSpec((1, ts, sum(ws)), lambda b, s: (b, s, 0)) for (_, ws) in cts]
    out_specs = [pl.BlockSpec((1, ts, sum(rows[i][2] for i in idxs)), lambda b, s: (b, s, 0)) for (_, idxs) in d_groups]
    out_specs += [pl.BlockSpec((1, 1, e.shape[-1]), lambda b, s: (b, 0, 0)) for e in exs]
    out_specs += [pl.BlockSpec((1, g.shape[-1]), lambda b, s: (0, 0)) for g in globs]
    out_shape = [jax.ShapeDtypeStruct((bsz, s, sum(rows[i][2] for i in idxs)), dt) for (dt, idxs) in d_groups]
    out_shape += [jax.ShapeDtypeStruct(e.shape, F32) for e in exs]
    out_shape += [jax.ShapeDtypeStruct(g.shape, F32) for g in globs]
    res = pl.pallas_call(
        body, name=name, grid=(bsz, s // ts),
        in_specs=in_specs, out_specs=out_specs, out_shape=out_shape,
        compiler_params=_params("arbitrary", "arbitrary"),
    )(*[r[0] for r in rows], *exs, *globs, *[c[0] for c in cts])
    return res[:n_d], res[n_d:n_d + n_e], res[n_d + n_e:]


def _full(a):
    return (a, 0, a.shape[-1])


def _view(a, col, w):
    assert col % w == 0
    return (a, col // w, w)


def _layer_norm(z, g, b):
    mu = jnp.mean(z, -1, keepdims=True)
    var = jnp.mean(jnp.square(z - mu), -1, keepdims=True)
    return (z - mu) * lax.rsqrt(var + LN_EPS) * g + b


def _rms_norm(z, g):
    ms = jnp.mean(jnp.square(z), -1, keepdims=True)
    return z * lax.rsqrt(ms + RMS_EPS) * g


def f_mod(rv, ev, gv):
    (x,), (scale, shift) = rv, ev
    return [x * (1.0 + scale) + shift]


def f_mod_with_x(rv, ev, gv):
    return f_mod(rv, ev, gv) + [rv[0]]


def f_ln_mod(rv, ev, gv):
    (x, y), (gate, scale, shift), (g, b) = rv, ev, gv
    xn = _layer_norm(ALPHA * x + (1.0 + gate) * y, g, b)
    return [xn, xn * (1.0 + scale) + shift]


def f_ln_loss(rv, ev, gv):
    (x, y, target), (gate,), (g, b) = rv, ev, gv
    xn = _layer_norm(ALPHA * x + (1.0 + gate) * y, g, b)
    return [0.5 * jnp.mean(jnp.square(xn - target), -1, keepdims=True)]


def _head_spread(width):
    r2 = QK_ROPE // 2
    j = lax.broadcasted_iota(jnp.int32, (LANES, width), 0)
    col = lax.broadcasted_iota(jnp.int32, (LANES, width), 1) % r2
    return (j == col).astype(F32), (j == col + r2).astype(F32)


def f_mla_mid(rv, ev, gv):
    (q_lat, kv_lat, kr, cos, sin), (q_g, kv_g) = rv, gv
    e1, e2 = _head_spread(cos.shape[-1])
    k1, k2 = hdot(kr, e1), hdot(kr, e2)
    return [_rms_norm(q_lat, q_g), _rms_norm(kv_lat, kv_g), k1 * cos - k2 * sin, k1 * sin + k2 * cos]


def rope_tables(positions):
    bsz, s = positions.shape
    r2 = QK_ROPE // 2
    width = MLA_HEADS * r2
    inv = (ROPE_THETA ** (-np.arange(0, QK_ROPE, 2, dtype=np.float32) / QK_ROPE)).astype(np.float32)
    inv = jnp.asarray(np.tile(inv, MLA_HEADS)[None, :])
    ts = min(ROW_TILE, s)

    def body(p_ref, inv_ref, cos_ref, sin_ref):
        ang = p_ref[0].astype(F32) * inv_ref[...]
        cos_ref[0] = jnp.cos(ang)
        sin_ref[0] = jnp.sin(ang)

    spec = pl.BlockSpec((1, ts, width), lambda b, s: (b, s, 0))
    return pl.pallas_call(
        body, name="rope_tables", grid=(bsz, s // ts),
        in_specs=[pl.BlockSpec((1, ts, 1), lambda b, s: (b, s, 0)), pl.BlockSpec((1, width), lambda b, s: (0, 0))],
        out_specs=[spec, spec], out_shape=[jax.ShapeDtypeStruct((bsz, s, width), F32)] * 2,
        compiler_params=_params("arbitrary", "arbitrary"),
    )(positions[:, :, None], inv)


def _attn_probs(q, k, row0):
    scale = (QK_NOPE + QK_ROPE) ** -0.5
    s = _dg(q, k, 1, 1) * scale
    rows = row0 + lax.broadcasted_iota(jnp.int32, s.shape, 0)
    cols = lax.broadcasted_iota(jnp.int32, s.shape, 1)
    s = jnp.where(cols <= rows, s, jnp.finfo(F32).min)
    e = jnp.exp(s - jnp.max(s, -1, keepdims=True))
    return e / jnp.sum(e, -1, keepdims=True), scale


ATTN_PAIR = 2


def attn_fwd(q, k, v, ride=()):
    bsz, h, s, dq = q.shape
    dv = v.shape[-1]
    tq = min(ATTN_TQ, s)
    grid = (bsz, h // ATTN_PAIR, s // tq)
    rd = _Ride(ride)

    def body(*refs):
        (q_ref, k_ref, v_ref), srcs, (o_ref,), outs, sems = rd.split(refs, 3, 1)
        rd.run(srcs, outs, sems, grid)
        for i in range(grid[2]):
            @pl.when(pl.program_id(2) == i)
            def _(i=i):
                kend = (i + 1) * tq
                for e in range(ATTN_PAIR):
                    p, _ = _attn_probs(q_ref[0, e], k_ref[0, e, :kend, :], i * tq)
                    o_ref[0, :, e * dv:(e + 1) * dv] = _dg(p.astype(BF16), v_ref[0, e, :kend, :], 1, 0).astype(o_ref.dtype)

    res = pl.pallas_call(
        body, name="attn_fwd", grid=grid,
        in_specs=[pl.BlockSpec((1, ATTN_PAIR, tq, dq), lambda b, h, i: (b, h, i, 0)),
                  pl.BlockSpec((1, ATTN_PAIR, s, dq), lambda b, h, i: (b, h, 0, 0)),
                  pl.BlockSpec((1, ATTN_PAIR, s, dv), lambda b, h, i: (b, h, 0, 0))] + rd.in_specs,
        out_specs=[pl.BlockSpec((1, tq, ATTN_PAIR * dv), lambda b, h, i: (b, i, h))] + rd.out_specs,
        out_shape=[jax.ShapeDtypeStruct((bsz, s, h * dv), BF16)] + rd.out_shape, scratch_shapes=rd.scratch,
        compiler_params=_params("arbitrary", "arbitrary", "arbitrary"),
    )(q, k, v, *rd.srcs)
    return res[0], res[1:]


def attn_bwd(q, k, v, do, ride=()):
    bsz, h, s, dq = q.shape
    dv = v.shape[-1]
    tq = min(ATTN_TQ, s)
    grid = (bsz, h // ATTN_PAIR, s // tq)
    rd = _Ride(ride)

    def body(*refs):
        (q_ref, k_ref, v_ref, do_ref), srcs, (dq_ref, dk_ref, dv_ref), outs, sems = rd.split(refs, 4, 3)
        rd.run(srcs, outs, sems, grid)

        @pl.when(pl.program_id(2) == 0)
        def _():
            dk_ref[...] = jnp.zeros_like(dk_ref)
            dv_ref[...] = jnp.zeros_like(dv_ref)

        for i in range(grid[2]):
            @pl.when(pl.program_id(2) == i)
            def _(i=i):
                kend = (i + 1) * tq
                for e in range(ATTN_PAIR):
                    qv, kv, vv = q_ref[0, e], k_ref[0, e, :kend, :], v_ref[0, e, :kend, :]
                    p, scale = _attn_probs(qv, kv, i * tq)
                    do16 = do_ref[0, :, e * dv:(e + 1) * dv].astype(BF16)
                    dv_ref[0, e, :kend, :] += _dg(p.astype(BF16), do16, 0, 0)
                    dp = _dg(do16, vv, 1, 1)
                    ds = (p * (dp - jnp.sum(dp * p, -1, keepdims=True)) * scale).astype(BF16)
                    dq_ref[0, e] = _dg(ds, kv, 1, 0)
                    dk_ref[0, e, :kend, :] += _dg(ds, qv, 0, 0)

    res = pl.pallas_call(
        body, name="attn_bwd", grid=grid,
        in_specs=[pl.BlockSpec((1, ATTN_PAIR, tq, dq), lambda b, h, i: (b, h, i, 0)),
                  pl.BlockSpec((1, ATTN_PAIR, s, dq), lambda b, h, i: (b, h, 0, 0)),
                  pl.BlockSpec((1, ATTN_PAIR, s, dv), lambda b, h, i: (b, h, 0, 0)),
                  pl.BlockSpec((1, tq, ATTN_PAIR * dv), lambda b, h, i: (b, i, h))] + rd.in_specs,
        out_specs=[pl.BlockSpec((1, ATTN_PAIR, tq, dq), lambda b, h, i: (b, h, i, 0)),
                   pl.BlockSpec((1, ATTN_PAIR, s, dq), lambda b, h, i: (b, h, 0, 0)),
                   pl.BlockSpec((1, ATTN_PAIR, s, dv), lambda b, h, i: (b, h, 0, 0))] + rd.out_specs,
        out_shape=[jax.ShapeDtypeStruct((bsz, h, s, dq), F32), jax.ShapeDtypeStruct((bsz, h, s, dq), F32),
                   jax.ShapeDtypeStruct((bsz, h, s, dv), F32)] + rd.out_shape, scratch_shapes=rd.scratch,
        compiler_params=_params("arbitrary", "arbitrary", "arbitrary"),
    )(q, k, v, do, *rd.srcs)
    return res[0], res[1], res[2], res[3:]


def mla_heads(q, kv, krt, cos, sin):
    bsz, s, _ = q.shape
    nh, n, r2, vd = MLA_HEADS, QK_NOPE, QK_ROPE // 2, V_HEAD
    ts = min(ROW_TILE, s)

    def body(q_ref, kv_ref, kr_ref, cos_ref, sin_ref, qh_ref, kh_ref, vh_ref):
        cos, sin = cos_ref[0], sin_ref[0]
        qv, kvv, kr = q_ref[0], kv_ref[0], kr_ref[0].astype(F32)
        q1, q2 = qv[:, nh * n:nh * (n + r2)], qv[:, nh * (n + r2):]
        qr = [q1 * cos - q2 * sin, q1 * sin + q2 * cos]
        for h in range(nh):
            qh_ref[0, h, :, :n] = qv[:, h * n:(h + 1) * n].astype(BF16)
            kh_ref[0, h, :, :n] = kvv[:, h * n:(h + 1) * n].astype(BF16)
            vh_ref[0, h] = kvv[:, nh * n + h * vd:nh * n + (h + 1) * vd].astype(BF16)
            for j in range(2):
                qh_ref[0, h, :, n + j * r2:n + (j + 1) * r2] = qr[j][:, h * r2:(h + 1) * r2].astype(BF16)
                kh_ref[0, h, :, n + j * r2:n + (j + 1) * r2] = kr[:, (j * nh + h) * r2:(j * nh + h + 1) * r2].astype(BF16)

    def row(a):
        return pl.BlockSpec((1, ts, a.shape[-1]), lambda b, i: (b, i, 0))

    def heads(w):
        return pl.BlockSpec((1, nh, ts, w), lambda b, i: (b, 0, i, 0))

    return pl.pallas_call(
        body, name="mla_heads", grid=(bsz, s // ts), in_specs=[row(q), row(kv), row(krt), row(cos), row(sin)],
        out_specs=[heads(n + 2 * r2), heads(n + 2 * r2), heads(vd)],
        out_shape=[jax.ShapeDtypeStruct((bsz, nh, s, n + 2 * r2), BF16)] * 2 + [jax.ShapeDtypeStruct((bsz, nh, s, vd), BF16)],
        compiler_params=_params("arbitrary", "arbitrary"),
    )(q, kv, krt, cos, sin)


def mla_heads_bwd(dqh, dkh, dvh, cos, sin):
    bsz, nh, s, _ = dqh.shape
    n, r2, vd = QK_NOPE, QK_ROPE // 2, V_HEAD
    ts = min(ROW_TILE, s)

    def body(dq_ref, dk_ref, dv_ref, cos_ref, sin_ref, oq_ref, okv_ref, okr_ref, tq_acc, tkv_acc, rot):
        for h in range(nh):
            tq_acc[:, h * n:(h + 1) * n] = dq_ref[0, h, :, :n]
            tkv_acc[:, h * n:(h + 1) * n] = dk_ref[0, h, :, :n]
            tkv_acc[:, nh * n + h * vd:nh * n + (h + 1) * vd] = dv_ref[0, h]
            for j in range(2):
                rot[j, :, h * r2:(h + 1) * r2] = dq_ref[0, h, :, n + j * r2:n + (j + 1) * r2]
                okr_ref[0, :, (j * nh + h) * r2:(j * nh + h + 1) * r2] = dk_ref[0, h, :, n + j * r2:n + (j + 1) * r2]
        cos, sin = cos_ref[0], sin_ref[0]
        d1, d2 = rot[0], rot[1]
        oq_ref[0, :, :nh * n] = tq_acc[...].astype(BF16)
        oq_ref[0, :, nh * n:nh * (n + r2)] = (d1 * cos + d2 * sin).astype(BF16)
        oq_ref[0, :, nh * (n + r2):] = (d2 * cos - d1 * sin).astype(BF16)
        okv_ref[0] = tkv_acc[...].astype(BF16)

    def row(w):
        return pl.BlockSpec((1, ts, w), lambda b, i: (b, i, 0))

    def heads(w):
        return pl.BlockSpec((1, nh, ts, w), lambda b, i: (b, 0, i, 0))

    return pl.pallas_call(
        body, name="mla_heads_bwd", grid=(bsz, s // ts),
        in_specs=[heads(n + 2 * r2), heads(n + 2 * r2), heads(vd), row(nh * r2), row(nh * r2)],
        out_specs=[row(nh * (n + 2 * r2)), row(nh * (n + vd)), row(2 * nh * r2)],
        out_shape=[jax.ShapeDtypeStruct((bsz, s, nh * (n + 2 * r2)), BF16), jax.ShapeDtypeStruct((bsz, s, nh * (n + vd)), BF16),
                   jax.ShapeDtypeStruct((bsz, s, 2 * nh * r2), F32)],
        scratch_shapes=[pltpu.VMEM((ts, nh * n), F32), pltpu.VMEM((ts, nh * (n + vd)), F32), pltpu.VMEM((2, ts, nh * r2), F32)],
        compiler_params=_params("arbitrary", "arbitrary"),
    )(dqh, dkh, dvh, cos, sin)


def _hgrn_tables(c):
    levels = c.bit_length() - 1
    assert 1 << levels == c
    r = np.arange(c)
    sign, mask = [], []
    for l in range(levels):
        lower = ((r >> l) & 1) == 1
        sign.append(np.broadcast_to(np.where(lower, 1.0, -1.0)[:, None], (c, LANES)))
        mask.append((((r[:, None] ^ r[None, :]) >> l) == 1) & lower[:, None])
    return (jnp.asarray(r[:, None] >= r[None, :], BF16), jnp.asarray(np.stack(sign), F32), jnp.asarray(np.stack(mask), F32))


def _const_specs(tables):
    return [pl.BlockSpec(t.shape, lambda b, h, i, nd=t.ndim: (0,) * nd) for t in tables]


def _split3(x):
    hi = x.astype(BF16)
    rest = x - hi.astype(F32)
    mid = rest.astype(BF16)
    return hi, mid, (rest - mid.astype(F32)).astype(BF16)


@functools.partial(jax.custom_vjp, nondiff_argnums=(2,))
def prefix_sums(p, g, n):
    k = g.shape[1]
    r = _dg(p, jnp.concatenate(_split3(g), axis=1), 1, 0)
    r = r[:, :k] + r[:, k:2 * k] + r[:, 2 * k:]
    c = r.shape[0] // n
    return tuple(r[i * c:(i + 1) * c] for i in range(n))


def _prefix_fwd(p, g, n):
    return prefix_sums(p, g, n), p


def _prefix_bwd(n, p, ct):
    ct = jnp.concatenate(ct, axis=0)
    k = ct.shape[1]
    r = _dg(p, jnp.concatenate(_split3(ct), axis=1), 0, 0)
    return jnp.zeros_like(p), r[:, :k] + r[:, k:2 * k] + r[:, 2 * k:]


prefix_sums.defvjp(_prefix_fwd, _prefix_bwd)


@functools.partial(jax.custom_vjp, nondiff_argnums=(1, 2))
def block_row(x, size, row):
    c, k = x.shape
    x3 = x.reshape(c // size, size, k)
    return jnp.broadcast_to(x3[:, row:row + 1, :], x3.shape).reshape(c, k)


def _block_row_fwd(x, size, row):
    return block_row(x, size, row), None


def _block_row_bwd(size, row, _, ct):
    c, k = ct.shape
    ct3 = ct.reshape(c // size, size, k)
    total = jnp.broadcast_to(jnp.sum(ct3, axis=1, keepdims=True), ct3.shape)
    rows = lax.broadcasted_iota(jnp.int32, ct3.shape, 1)
    return (jnp.where(rows == row, total, 0.0).reshape(c, k),)


block_row.defvjp(_block_row_fwd, _block_row_bwd)


def _hgrn_chunk(q, g, k, v, st0, prefix, sign, mask):
    levels = len(mask)
    (b,) = prefix_sums(prefix, g, 1)
    o = bdot(q * jnp.exp(b), st0, 1, 1)
    att = None
    for l in range(levels):
        e = jnp.exp((b - block_row(b, 2 << l, (1 << l) - 1)) * sign[l])
        a = bdot(q * e, k * e, 1, 1) * mask[l]
        att = a if att is None else att + a
    o = o + bdot(att, v, 1, 0) + jnp.sum(q * k, -1, keepdims=True) * v
    total = jnp.sum(g, 0, keepdims=True)
    st1 = st0 * jnp.exp(total) + bdot(v, k * jnp.exp(total - b), 0, 0)
    return o, st1


def _hgrn_step(q_raw, fx, v, g_raw, st0, lb, gn, prefix, sign, mask):
    f = lb + (1.0 - lb) * jax.nn.sigmoid(fx)
    o, st1 = _hgrn_chunk(jax.nn.silu(q_raw), jnp.log(f), 1.0 - f, v, st0, prefix, sign, mask)
    return _rms_norm(o, gn) * jax.nn.silu(g_raw), st1


def _hgrn_layout(proj):
    bsz, s, width = proj.shape
    kd = HGRN_EXPAND
    c = min(HGRN_CHUNK, s)
    nh = width // (4 * kd)
    hp = math.gcd(nh, HGRN_HEADS_PER_STEP)
    return bsz, s, kd, c, nh, s // c, hp


def hgrn_fwd(proj, lb, gn, ride=()):
    bsz, s, kd, c, nh, nc, hp = _hgrn_layout(proj)
    groups = nh // hp
    tables = _hgrn_tables(c)
    levels = tables[2].shape[0]
    grid = (groups, bsz, nc)
    rd = _Ride(ride)

    def body(*refs):
        ((q_ref, f_ref, v_ref, g_ref, lb_ref, gn_ref, p_ref, sg_ref, mk_ref), srcs, (z_ref, st_ref), outs,
         scratch) = rd.split(refs, 9, 2)
        state = scratch[0]
        rd.run(srcs, outs, scratch, grid)

        @pl.when(pl.program_id(2) == 0)
        def _():
            state[...] = jnp.zeros_like(state)

        prefix, sign, mask = p_ref[...], [sg_ref[l] for l in range(levels)], [mk_ref[l] for l in range(levels)]
        for j in range(hp):
            cols = slice(j * kd, (j + 1) * kd)
            st0 = state[j]
            st_ref[0, j, 0] = st0
            z, st1 = _hgrn_step(q_ref[0, :, cols], f_ref[0, :, cols], v_ref[0, :, cols], g_ref[0, :, cols], st0,
                                lb_ref[:, cols], gn_ref[...], prefix, sign, mask)
            z_ref[0, :, cols] = z.astype(z_ref.dtype)
            state[j] = st1

    def part(k):
        return pl.BlockSpec((1, c, hp * kd), lambda h, b, i: (b, i, k * groups + h))

    res = pl.pallas_call(
        body, name="hgrn_fwd", grid=grid,
        in_specs=[part(0), part(1), part(2), part(3), pl.BlockSpec((1, hp * kd), lambda h, b, i: (0, h)),
                  pl.BlockSpec((1, kd), lambda h, b, i: (0, 0))] + _const_specs(tables) + rd.in_specs,
        out_specs=[part(0), pl.BlockSpec((1, hp, 1, kd, kd), lambda h, b, i: (b, h, i, 0, 0))] + rd.out_specs,
        out_shape=[jax.ShapeDtypeStruct((bsz, s, nh * kd), BF16), jax.ShapeDtypeStruct((bsz, nh, nc, kd, kd), F32)] + rd.out_shape,
        scratch_shapes=[pltpu.VMEM((hp, kd, kd), F32)] + rd.scratch,
        compiler_params=_params("arbitrary", "arbitrary", "arbitrary"),
    )(proj, proj, proj, proj, lb, gn, *tables, *rd.srcs)
    return res[0], res[1], res[2:]


def hgrn_bwd(proj, states, dz, lb, gn, ride=()):
    bsz, s, kd, c, nh, nc, hp = _hgrn_layout(proj)
    groups = nh // hp
    tables = _hgrn_tables(c)
    levels = tables[2].shape[0]
    grid = (groups, bsz, nc)
    rd = _Ride(ride)

    def body(*refs):
        ((q_ref, f_ref, v_ref, g_ref, st_ref, dz_ref, lb_ref, gn_ref, p_ref, sg_ref, mk_ref), srcs,
         (dq_ref, df_ref, dv_ref, dg_ref, dlb_ref, dgn_ref), outs, scratch) = rd.split(refs, 11, 6)
        dstate = scratch[0]
        rd.run(srcs, outs, scratch, grid)
        first_of_group = (pl.program_id(1) == 0) & (pl.program_id(2) == 0)

        @pl.when(pl.program_id(2) == 0)
        def _():
            dstate[...] = jnp.zeros_like(dstate)

        @pl.when(first_of_group)
        def _():
            dlb_ref[...] = jnp.zeros_like(dlb_ref)

        @pl.when(first_of_group & (pl.program_id(0) == 0))
        def _():
            dgn_ref[...] = jnp.zeros_like(dgn_ref)

        prefix, sign, mask = p_ref[...], [sg_ref[l] for l in range(levels)], [mk_ref[l] for l in range(levels)]
        for j in range(hp):
            cols = slice(j * kd, (j + 1) * kd)
            _, vjp = jax.vjp(lambda q, f, v, g, st, lb, gn: _hgrn_step(q, f, v, g, st, lb, gn, prefix, sign, mask),
                             q_ref[0, :, cols], f_ref[0, :, cols], v_ref[0, :, cols], g_ref[0, :, cols], st_ref[0, j, 0],
                             lb_ref[:, cols], gn_ref[...])
            dq, df, dv, dg, dst, dlb, dgn = vjp((dz_ref[0, :, cols], dstate[j]))
            dq_ref[0, :, cols] = dq.astype(dq_ref.dtype)
            df_ref[0, :, cols] = df.astype(df_ref.dtype)
            dv_ref[0, :, cols] = dv.astype(dv_ref.dtype)
            dg_ref[0, :, cols] = dg.astype(dg_ref.dtype)
            dlb_ref[:, cols] += dlb
            dgn_ref[...] += dgn
            dstate[j] = dst

    def part(k):
        return pl.BlockSpec((1, c, hp * kd), lambda h, b, i: (b, nc - 1 - i, k * groups + h))

    shape = jax.ShapeDtypeStruct((bsz, s, nh * kd), BF16)
    lb_spec = pl.BlockSpec((1, hp * kd), lambda h, b, i: (0, h))
    gn_spec = pl.BlockSpec((1, kd), lambda h, b, i: (0, 0))
    res = pl.pallas_call(
        body, name="hgrn_bwd", grid=grid,
        in_specs=[part(0), part(1), part(2), part(3), pl.BlockSpec((1, hp, 1, kd, kd), lambda h, b, i: (b, h, nc - 1 - i, 0, 0)),
                  part(0), lb_spec, gn_spec] + _const_specs(tables) + rd.in_specs,
        out_specs=[part(0)] * 4 + [lb_spec, gn_spec] + rd.out_specs,
        out_shape=[shape] * 4 + [jax.ShapeDtypeStruct(lb.shape, F32), jax.ShapeDtypeStruct(gn.shape, F32)] + rd.out_shape,
        scratch_shapes=[pltpu.VMEM((hp, kd, kd), F32)] + rd.scratch,
        compiler_params=_params("arbitrary", "arbitrary", "arbitrary"),
    )(proj, proj, proj, proj, states, dz, lb, gn, *tables, *rd.srcs)
    return list(res[:4]), res[4], res[5], res[6:]


def cast_bf16(name, w):
    blk = pl.BlockSpec((1,) + w.shape[1:], lambda l: (l, 0, 0))

    def body(w_ref, o_ref):
        o_ref[...] = w_ref[...].astype(BF16)

    return pl.pallas_call(body, name=name, grid=(w.shape[0],), in_specs=[blk], out_specs=blk,
                          out_shape=jax.ShapeDtypeStruct(w.shape, BF16), compiler_params=_params("arbitrary"))(w)


def _lower_bounds(rows):
    m = functools.reduce(jnp.maximum, rows)
    e = [jnp.exp(r - m) for r in rows]
    z = functools.reduce(lambda a, b: a + b, e)
    soft = [x / z for x in e]
    out, run = [], jnp.zeros_like(rows[0])
    for sft in soft:
        run = run + sft
        out.append(run - soft[0])
    return out


def lower_bounds(lb):
    n = lb.shape[0]

    def body(lb_ref, o_ref):
        for i, r in enumerate(_lower_bounds([lb_ref[i:i + 1, :] for i in range(n)])):
            o_ref[i:i + 1, :] = r

    return pl.pallas_call(body, name="lower_bounds", out_shape=jax.ShapeDtypeStruct(lb.shape, F32),
                          compiler_params=_params())(lb)


def ada_fwd(c_all, ada_w, ada_b):
    nl, ns, d, cols = ada_w.shape
    n_ex = c_all.shape[0]

    def body(c_ref, w_ref, b_ref, o_ref):
        a = jax.nn.silu(c_ref[...]).astype(BF16)
        o_ref[0] = _dg(a, w_ref[0].astype(BF16), 1, 0) + b_ref[0]

    return pl.pallas_call(
        body, name="ada_fwd", grid=(nl * ns,),
        in_specs=[pl.BlockSpec((n_ex, d), lambda i: (0, 0)), pl.BlockSpec((1, d, cols), lambda i: (i, 0, 0)),
                  pl.BlockSpec((1, 1, cols), lambda i: (i, 0, 0))],
        out_specs=pl.BlockSpec((1, n_ex, cols), lambda i: (i, 0, 0)),
        out_shape=jax.ShapeDtypeStruct((nl * ns, n_ex, cols), F32), compiler_params=_params("arbitrary"),
    )(c_all, ada_w.reshape(nl * ns, d, cols), ada_b.reshape(nl * ns, 1, cols))


def ada_bwd(c_all, dmod):
    n, n_ex, cols = dmod.shape
    d = c_all.shape[1]

    def body(c_ref, g_ref, dw_ref, db_ref):
        a = jax.nn.silu(c_ref[...]).astype(BF16)
        g = g_ref[0]
        dw_ref[0] = _dg(a, g.astype(BF16), 0, 0)
        db_ref[0] = jnp.sum(g, 0, keepdims=True)

    return pl.pallas_call(
        body, name="ada_bwd", grid=(n,),
        in_specs=[pl.BlockSpec((n_ex, d), lambda i: (0, 0)), pl.BlockSpec((1, n_ex, cols), lambda i: (i, 0, 0))],
        out_specs=[pl.BlockSpec((1, d, cols), lambda i: (i, 0, 0)), pl.BlockSpec((1, 1, cols), lambda i: (i, 0, 0))],
        out_shape=[jax.ShapeDtypeStruct((n, d, cols), F32), jax.ShapeDtypeStruct((n, 1, cols), F32)],
        compiler_params=_params("arbitrary"),
    )(c_all, dmod)


def _adam_math(g, w, m, v):
    m = ADAM_B1 * m + (1.0 - ADAM_B1) * g
    v = ADAM_B2 * v + (1.0 - ADAM_B2) * jnp.square(g)
    m_hat = m / (1.0 - ADAM_B1 ** ADAM_STEP)
    v_hat = v / (1.0 - ADAM_B2 ** ADAM_STEP)
    delta = -ADAM_LR * (m_hat / (jnp.sqrt(v_hat) + ADAM_EPS) + ADAM_WD * w)
    return delta, m, v


def adam(name, gstack, w, m, v):
    shape = w.shape
    n, cols = gstack.shape[0], shape[-1]
    rows = math.prod(shape[:-1])
    tr = _pick_rows(rows, max(8, (2 * 1024 * 1024) // (4 * cols * n)))

    def body(g_ref, w_ref, m_ref, v_ref, go_ref, d_ref, mo_ref, vo_ref):
        g = g_ref[0].astype(F32)
        for i in range(1, n):
            g = g + g_ref[i].astype(F32)
        delta, m1, v1 = _adam_math(g, w_ref[...], m_ref[...], v_ref[...])
        go_ref[...] = g
        d_ref[...] = delta
        mo_ref[...] = m1
        vo_ref[...] = v1

    blk = pl.BlockSpec((tr, cols), lambda i: (i, 0))
    out = pl.pallas_call(
        body, name=name, grid=(rows // tr,),
        in_specs=[pl.BlockSpec((n, tr, cols), lambda i: (0, i, 0)), blk, blk, blk],
        out_specs=[blk] * 4, out_shape=[jax.ShapeDtypeStruct((rows, cols), F32)] * 4,
        compiler_params=_params("arbitrary"),
    )(gstack.reshape(n, rows, cols), w.reshape(rows, cols), m.reshape(rows, cols), v.reshape(rows, cols))
    return [o.reshape(shape) for o in out]


def adam_layers(name, gs, w, m, v):
    shape = w.shape
    nl, n, cols = len(gs), gs[0].shape[0], shape[-1]
    rows = math.prod(shape[1:-1])
    tr = _pick_rows(rows, max(16, (2 * 1024 * 1024) // (4 * cols * n)))
    nt = rows // tr

    def body(*refs):
        g_refs, (w_ref, m_ref, v_ref, go_ref, d_ref, mo_ref, vo_ref) = refs[:nl], refs[nl:]
        for j in range(nl):
            @pl.when(pl.program_id(0) == j)
            def _(j=j):
                g = g_refs[j][0].astype(F32)
                for i in range(1, n):
                    g = g + g_refs[j][i].astype(F32)
                delta, m1, v1 = _adam_math(g, w_ref[...], m_ref[...], v_ref[...])
                go_ref[...] = g
                d_ref[...] = delta
                mo_ref[...] = m1
                vo_ref[...] = v1

    def g_spec(j):
        return pl.BlockSpec((n, tr, cols), lambda l, i: (0, jnp.where(l == j, i, jnp.where(l < j, 0, nt - 1)), 0))

    blk = pl.BlockSpec((tr, cols), lambda l, i: (l * nt + i, 0))
    out = pl.pallas_call(
        body, name=name, grid=(nl, nt),
        in_specs=[g_spec(j) for j in range(nl)] + [blk, blk, blk],
        out_specs=[blk] * 4, out_shape=[jax.ShapeDtypeStruct((nl * rows, cols), F32)] * 4,
        compiler_params=_params("arbitrary", "arbitrary"),
    )(*[g.reshape(n, rows, cols) for g in gs], w.reshape(nl * rows, cols), m.reshape(nl * rows, cols), v.reshape(nl * rows, cols))
    return [o.reshape(shape) for o in out]


def adam_lb(gstack, lb, m, v):
    n, nl = gstack.shape[0], lb.shape[0]

    def body(g_ref, w_ref, m_ref, v_ref, go_ref, d_ref, mo_ref, vo_ref):
        rows = [w_ref[i:i + 1, :] for i in range(nl)]
        ct = []
        for i in range(nl):
            g = g_ref[0, i:i + 1, :]
            for j in range(1, n):
                g = g + g_ref[j, i:i + 1, :]
            ct.append(g)
        _, vjp = jax.vjp(_lower_bounds, rows)
        (grads,) = vjp(ct)
        for i in range(nl):
            delta, m1, v1 = _adam_math(grads[i], rows[i], m_ref[i:i + 1, :], v_ref[i:i + 1, :])
            go_ref[i:i + 1, :] = grads[i]
            d_ref[i:i + 1, :] = delta
            mo_ref[i:i + 1, :] = m1
            vo_ref[i:i + 1, :] = v1

    return pl.pallas_call(body, name="adam_hgrn_lb", out_shape=[jax.ShapeDtypeStruct(lb.shape, F32)] * 4,
                          compiler_params=_params())(gstack, lb, m, v)


class _Ride:
    def __init__(self, items):
        self.items = list(items)
        n = len(self.items)
        self.srcs = [src for src, _ in self.items]
        self.in_specs = [pl.BlockSpec(memory_space=pl.ANY)] * n
        self.out_specs = [pl.BlockSpec(memory_space=pl.ANY)] * n
        self.out_shape = [jax.ShapeDtypeStruct(((N_DEV,) + s.shape) if mode == "gather" else s.shape, s.dtype)
                          for s, mode in self.items]
        self.scratch = [pltpu.SemaphoreType.DMA((n, N_DEV - 1)), pltpu.SemaphoreType.DMA((n, N_DEV - 1)),
                        pltpu.SemaphoreType.DMA((n,))] if n else []

    def split(self, refs, n_in, n_out):
        n = len(self.items)
        a, b = n_in + n, n_in + 2 * n + n_out
        return refs[:n_in], refs[n_in:a], refs[a:a + n_out], refs[a + n_out:b], refs[b:]

    def _copies(self, srcs, outs, sems):
        send_sems, recv_sems, local_sems = sems
        x, y, c = lax.axis_index("x"), lax.axis_index("y"), lax.axis_index("c")
        me = 4 * x + 2 * y + c
        copies = []
        for i, (_, mode) in enumerate(self.items):
            mine = srcs[i] if mode == "gather" else srcs[i].at[me]
            copies.append(pltpu.make_async_copy(mine, outs[i].at[me], local_sems.at[i]))
            for p in range(1, N_DEV):
                px = 1 - x if p & 4 else x
                py = 1 - y if p & 2 else y
                pc = 1 - c if p & 1 else c
                part = srcs[i] if mode == "gather" else srcs[i].at[4 * px + 2 * py + pc]
                copies.append(pltpu.make_async_remote_copy(
                    src_ref=part, dst_ref=outs[i].at[me], send_sem=send_sems.at[i, p - 1], recv_sem=recv_sems.at[i, p - 1],
                    device_id=(px, py, pc), device_id_type=pl.DeviceIdType.MESH))
        return copies

    def run(self, srcs, outs, scratch, grid=()):
        if not self.items:
            return
        sems = scratch[len(scratch) - 3:]
        if not grid:
            copies = self._copies(srcs, outs, sems)
            for cp in copies:
                cp.start()
            for cp in copies:
                cp.wait()
            return
        ids = [pl.program_id(a) for a in range(len(grid))]
        first = functools.reduce(lambda a, b: a & b, [i == 0 for i in ids])
        last = functools.reduce(lambda a, b: a & b, [i == g - 1 for i, g in zip(ids, grid)])

        @pl.when(first)
        def _():
            for cp in self._copies(srcs, outs, sems):
                cp.start()

        @pl.when(last)
        def _():
            for cp in self._copies(srcs, outs, sems):
                cp.wait()


def exchange(name, items):
    rd = _Ride(items)

    def body(*refs):
        _, srcs, _, outs, scratch = rd.split(refs, 0, 0)
        rd.run(srcs, outs, scratch)

    return pl.pallas_call(body, name=name, in_specs=rd.in_specs, out_specs=rd.out_specs, out_shape=rd.out_shape,
                          scratch_shapes=rd.scratch)(*rd.srcs)


def _from_gather(name, g):
    if name in COL_SHARDED:
        _, k, n = g.shape
        return g.transpose(1, 0, 2).reshape(k, N_DEV * n)
    return g.reshape(-1, g.shape[-1])


def _to_slabs(name, w):
    if isinstance(w, tuple):
        per = N_DEV // len(w)
        return jnp.concatenate([p.reshape(p.shape[0], per, p.shape[1] // per).transpose(1, 0, 2) for p in w], axis=0)
    k, n = w.shape
    if name in COL_SHARDED:
        return w.reshape(k, N_DEV, n // N_DEV).transpose(1, 0, 2)
    return w.reshape(N_DEV, k // N_DEV, n)


def _w_in_internal(w):
    return jnp.pad(w, ((0, 0), (0, LANES - QK_ROPE)))


def _qb_internal(w, inverse=False):
    h, n, r2 = MLA_HEADS, QK_NOPE, QK_ROPE // 2
    lead = w.shape[:-1]
    if not inverse:
        w = w.reshape(lead + (h, n + 2 * r2))
        parts = [w[..., :n], w[..., n:n + r2], w[..., n + r2:]]
        return jnp.concatenate([p.reshape(lead + (-1,)) for p in parts], axis=-1)
    parts = [w[..., :h * n].reshape(lead + (h, n)), w[..., h * n:h * (n + r2)].reshape(lead + (h, r2)),
             w[..., h * (n + r2):].reshape(lead + (h, r2))]
    return jnp.concatenate(parts, axis=-1).reshape(lead + (-1,))


def _kvb_internal(w, inverse=False):
    h, n, vd = MLA_HEADS, QK_NOPE, V_HEAD
    lead = w.shape[:-1]
    if not inverse:
        w = w.reshape(lead + (h, n + vd))
        return jnp.concatenate([w[..., :n].reshape(lead + (-1,)), w[..., n:].reshape(lead + (-1,))], axis=-1)
    parts = [w[..., :h * n].reshape(lead + (h, n)), w[..., h * n:].reshape(lead + (h, vd))]
    return jnp.concatenate(parts, axis=-1).reshape(lead + (-1,))


def _mla_forward(h, w, tabs, ride=()):
    cos, sin = tabs
    r2 = MLA_HEADS * (QK_ROPE // 2)
    proj = mm3("mla_proj", h, w['w_in'])
    qn, kvn, krt = rowwise(
        "mla_mid", lambda rv, ev, gv: (f_mla_mid(rv, ev, gv), []),
        [_view(proj, 0, Q_LORA), _view(proj, Q_LORA, KV_LORA), _view(proj, Q_LORA + KV_LORA, LANES), _full(cos), _full(sin)],
        [], [w['q_norm'], w['kv_norm']], [(BF16, [Q_LORA]), (BF16, [KV_LORA]), (BF16, [r2, r2])], ts=ROW_TILE)
    q = mm3("mla_q", qn, w['w_qb'])
    kv = mm3("mla_kv", kvn, w['w_kvb'])
    qh, kh, vh = mla_heads(q, kv, krt, cos, sin)
    o, got = attn_fwd(qh, kh, vh, ride)
    y = mm3("mla_out", o, w['w_o'])
    return y, dict(h=h, proj=proj, qn=qn, kvn=kvn, qh=qh, kh=kh, vh=vh, o=o), got


def _mla_backward(dy, sv, w, tabs, ride=()):
    cos, sin = tabs
    r2 = MLA_HEADS * (QK_ROPE // 2)
    g = {}
    g['w_o'] = wgrad("mla_out_dw", sv['o'], dy)
    do = mm3("mla_out_dx", dy, w['w_o'], tb=True)
    dqh, dkh, dvh, got = attn_bwd(sv['qh'], sv['kh'], sv['vh'], do, ride)
    dq, dkv, dkrt = mla_heads_bwd(dqh, dkh, dvh, cos, sin)
    g['w_qb'] = wgrad("mla_q_dw", sv['qn'], dq)
    g['w_kvb'] = wgrad("mla_kv_dw", sv['kvn'], dkv)
    dqn = mm3("mla_q_dx", dq, w['w_qb'], tb=True)
    dkvn = mm3("mla_kv_dx", dkv, w['w_kvb'], tb=True)
    proj = sv['proj']
    (dproj,), _, (g['q_norm'], g['kv_norm']) = rowwise_bwd(
        "mla_mid_bwd", f_mla_mid,
        [_view(proj, 0, Q_LORA), _view(proj, Q_LORA, KV_LORA), _view(proj, Q_LORA + KV_LORA, LANES), _full(cos), _full(sin)],
        [], [w['q_norm'], w['kv_norm']], [(dqn, [Q_LORA]), (dkvn, [KV_LORA]), (dkrt, [r2, r2])],
        [(BF16, [0, 1, 2])], ts=ROW_TILE, n_diff=3)
    g['w_in'] = wgrad("mla_proj_dw", sv['h'], dproj)
    dh = mm3("mla_proj_dx", dproj, w['w_in'], tb=True)
    return dh, g, got


def _hgrn_forward(h, w, ride=()):
    proj = mm3("hgrn_proj", h, w['w_in'])
    z, states, got = hgrn_fwd(proj, w['lb'], w['g_norm'], ride)
    y = mm3("hgrn_out", z, w['w_o'])
    return y, dict(h=h, proj=proj, states=states, z=z), got


def _hgrn_backward(dy, sv, w, ride=()):
    g = {}
    g['w_o'] = wgrad("hgrn_out_dw", sv['z'], dy)
    dz = mm3("hgrn_out_dx", dy, w['w_o'], tb=True)
    dparts, g['lb'], g['g_norm'], got = hgrn_bwd(sv['proj'], sv['states'], dz, w['lb'], w['g_norm'], ride)
    g['w_in'] = tuple(wgrad("hgrn_proj_dw", sv['h'], p) for p in dparts)
    dh = mm3("hgrn_proj_dx", dparts, w['w_in'], tb=True)
    return dh, g, got


def _ffn_forward(h, w, ride=()):
    (ug, uu, a), got = ffn_in_act(h, w['w_in'], ride)
    y = mm3("ffn_out", a, w['w_out'])
    return y, dict(h=h, ug=ug, uu=uu, a=a), got


def _ffn_backward(dy, sv, w):
    g = {}
    g['w_out'] = wgrad("ffn_out_dw", sv['a'], dy)
    dug, duu = ffn_out_dx_act(dy, w['w_out'], sv['ug'], sv['uu'])
    g['w_in'] = (wgrad("ffn_in_dw", sv['h'], dug), wgrad("ffn_in_dw", sv['h'], duu))
    dh = mm3("ffn_in_dx", [dug, duu], w['w_in'], tb=True)
    return dh, g


def kernel(x, c, positions, mla_w_in, mla_q_norm, mla_w_qb, mla_kv_norm, mla_w_kvb, mla_w_o, hgrn_lb, hgrn_w_in, hgrn_g_norm, hgrn_w_o, ffn_w_in, ffn_w_out, ada_w, ada_b, ln_g, ln_b, loss_target, m_mla_w_in, m_mla_q_norm, m_mla_w_qb, m_mla_kv_norm, m_mla_w_kvb, m_mla_w_o, m_hgrn_lb, m_hgrn_w_in, m_hgrn_g_norm, m_hgrn_w_o, m_ffn_w_in, m_ffn_w_out, m_ada_w, m_ada_b, m_ln_g, m_ln_b, v_mla_w_in, v_mla_q_norm, v_mla_w_qb, v_mla_kv_norm, v_mla_w_kvb, v_mla_w_o, v_hgrn_lb, v_hgrn_w_in, v_hgrn_g_norm, v_hgrn_w_o, v_ffn_w_in, v_ffn_w_out, v_ada_w, v_ada_b, v_ln_g, v_ln_b):
    W = dict(zip(WEIGHTS, (mla_w_in, mla_q_norm, mla_w_qb, mla_kv_norm, mla_w_kvb, mla_w_o, hgrn_lb, hgrn_w_in, hgrn_g_norm,
                           hgrn_w_o, ffn_w_in, ffn_w_out, ada_w, ada_b, ln_g, ln_b)))
    M1 = dict(zip(WEIGHTS, (m_mla_w_in, m_mla_q_norm, m_mla_w_qb, m_mla_kv_norm, m_mla_w_kvb, m_mla_w_o, m_hgrn_lb, m_hgrn_w_in,
                            m_hgrn_g_norm, m_hgrn_w_o, m_ffn_w_in, m_ffn_w_out, m_ada_w, m_ada_b, m_ln_g, m_ln_b)))
    M2 = dict(zip(WEIGHTS, (v_mla_w_in, v_mla_q_norm, v_mla_w_qb, v_mla_kv_norm, v_mla_w_kvb, v_mla_w_o, v_hgrn_lb, v_hgrn_w_in,
                            v_hgrn_g_norm, v_hgrn_w_o, v_ffn_w_in, v_ffn_w_out, v_ada_w, v_ada_b, v_ln_g, v_ln_b)))
    bsz, seq, d = x.shape
    depth, n_mla, n_hgrn = ffn_w_in.shape[0], mla_w_in.shape[0], hgrn_w_in.shape[0]
    n_sub = 2 * depth

    big = COL_SHARDED + ROW_SHARDED
    wb = {n: cast_bf16("cast_" + n, W[n]) for n in big}

    def mixer_names(layer):
        mixer = ['mla_w_in', 'mla_w_qb', 'mla_w_kvb', 'mla_w_o'] if layer % 2 == 0 else ['hgrn_w_in', 'hgrn_w_o']
        return [(n, layer // 2) for n in mixer]

    def carried(layer):
        return [('ffn_w_in', layer), ('ffn_w_out', layer)] + (mixer_names(layer + 1) if layer + 1 < depth else [])

    def carried_fwd(k):
        layer = k // 2
        if k % 2 == 0:
            return [('ffn_w_in', layer), ('ffn_w_out', layer)]
        return mixer_names(layer + 1) if layer + 1 < depth else []

    def weight_items(names):
        return [(wb[n][j], "gather") for n, j in names]

    G = {}
    internal = {'mla_w_in': _w_in_internal, 'mla_w_qb': _qb_internal, 'mla_w_kvb': _kvb_internal}

    def take_weights(names, got):
        for (n, j), a in zip(names, got):
            G[n, j] = internal.get(n, lambda w: w)(_from_gather(n, a))

    lower_shard = lower_bounds(hgrn_lb)
    got = exchange("gather_first", [(lower_shard, "gather"), (ln_g, "gather"), (ln_b, "gather"), (c, "gather")]
                   + weight_items(mixer_names(0)))
    lower_all = got[0].transpose(1, 0, 2).reshape(n_hgrn, -1)
    ln_g_all = got[1].transpose(1, 2, 0, 3).reshape(depth, 2, d)
    ln_b_all = got[2].transpose(1, 2, 0, 3).reshape(depth, 2, d)
    c_all = got[3].reshape(N_DEV * bsz, d)
    take_weights(mixer_names(0), got[4:])

    cols = ada_w.shape[-1]
    mod_loc = ada_fwd(c_all, ada_w, ada_b)
    (mod_got,) = exchange("scatter_mod", [(mod_loc.reshape(n_sub, N_DEV, bsz, cols).transpose(1, 0, 2, 3), "a2a")])
    mod = mod_got.transpose(1, 2, 0, 3).reshape(n_sub, bsz, 1, 3 * d)
    shift = [mod[k, :, :, 0:d] for k in range(n_sub)]
    scale = [mod[k, :, :, d:2 * d] for k in range(n_sub)]
    gate = [mod[k, :, :, 2 * d:] for k in range(n_sub)]
    lng = [ln_g_all[k // 2, k % 2][None, :] for k in range(n_sub)]
    lnb = [ln_b_all[k // 2, k % 2][None, :] for k in range(n_sub)]

    tabs = rope_tables(positions)

    def sub_weights(k):
        layer, j = k // 2, k // 4
        if k % 2:
            return 'ffn', layer, dict(w_in=G['ffn_w_in', layer], w_out=G['ffn_w_out', layer])
        if layer % 2 == 0:
            return 'mla', j, dict(w_in=G['mla_w_in', j], q_norm=mla_q_norm[j][None, :], w_qb=G['mla_w_qb', j],
                                  kv_norm=mla_kv_norm[j][None, :], w_kvb=G['mla_w_kvb', j], w_o=G['mla_w_o', j])
        return 'hgrn', j, dict(w_in=G['hgrn_w_in', j], lb=lower_all[j][None, :], g_norm=hgrn_g_norm[j][None, :],
                               w_o=G['hgrn_w_o', j])

    (h,) = rowwise("mod_first", lambda rv, ev, gv: (f_mod(rv, ev, gv), []), [_full(x)], [scale[0], shift[0]], [],
                   [(BF16, [d])], ts=ROW_TILE)
    xs, ys, saved = [x], [], []
    loss_acc = None
    for k in range(n_sub):
        kind, _, w = sub_weights(k)
        ride = weight_items(carried_fwd(k))
        if kind == 'ffn':
            y, sv, got = _ffn_forward(h, w, ride)
        elif kind == 'mla':
            y, sv, got = _mla_forward(h, w, tabs, ride)
        else:
            y, sv, got = _hgrn_forward(h, w, ride)
        take_weights(carried_fwd(k), got)
        ys.append(y)
        saved.append(sv)
        if k + 1 < n_sub:
            xn, h = rowwise("ln_mod", lambda rv, ev, gv: (f_ln_mod(rv, ev, gv), []), [_full(xs[k]), _full(y)],
                            [gate[k], scale[k + 1], shift[k + 1]], [lng[k], lnb[k]], [(F32, [d]), (BF16, [d])], ts=ROW_TILE)
            xs.append(xn)
        else:
            def loss_rows(rv, ev, gv):
                (row,) = f_ln_loss(rv, ev, gv)
                return [], [jnp.broadcast_to(jnp.sum(row, keepdims=True), (1, LANES))]
            (loss_acc,) = rowwise("ln_loss", loss_rows, [_full(xs[k]), _full(y), _full(loss_target)], [gate[k]], [lng[k], lnb[k]],
                                  [], ts=ROW_TILE, accs=[LANES])
    loss = lax.psum(loss_acc[0, 0], ("x", "y", "c"))

    d_shift, d_scale, d_gate = [None] * n_sub, [None] * n_sub, [None] * n_sub
    d_lng, d_lnb = [None] * n_sub, [None] * n_sub
    part = {n: [None] * W[n].shape[0] for n in ['mla_q_norm', 'mla_kv_norm', 'hgrn_g_norm']}
    recv = {n: [None] * W[n].shape[0] for n in big}
    d_lower = [None] * n_hgrn
    k = n_sub - 1
    (dx, dy), (d_gate[k],), (d_lng[k], d_lnb[k]) = rowwise_bwd(
        "ln_loss_bwd", f_ln_loss, [_full(xs[k]), _full(ys[k]), _full(loss_target)], [gate[k]], [lng[k], lnb[k]], [],
        [(F32, [0]), (BF16, [1])], ts=ROW_TILE, n_diff=2, unit_ct=1)
    grad_x = None
    mine = {}

    def take_grads(names, got):
        for (n, j), a in zip(names, got):
            recv[n][j] = a

    def grad_items(names):
        return [(_to_slabs(n, mine[n, jj]), "a2a") for n, jj in names]

    for k in range(n_sub - 1, -1, -1):
        kind, j, w = sub_weights(k)
        ride = grad_items(carried(k // 2)) if kind != 'ffn' else []
        if kind == 'ffn':
            dh, g = _ffn_backward(dy, saved[k], w)
            new = {('ffn_w_in', j): g['w_in'], ('ffn_w_out', j): g['w_out']}
        elif kind == 'mla':
            dh, g, got = _mla_backward(dy, saved[k], w, tabs, ride)
            new = {('mla_w_in', j): g['w_in'][:, :mla_w_in.shape[-1] * N_DEV], ('mla_w_qb', j): _qb_internal(g['w_qb'], inverse=True),
                   ('mla_w_kvb', j): _kvb_internal(g['w_kvb'], inverse=True), ('mla_w_o', j): g['w_o']}
            part['mla_q_norm'][j], part['mla_kv_norm'][j] = g['q_norm'][0], g['kv_norm'][0]
        else:
            dh, g, got = _hgrn_backward(dy, saved[k], w, ride)
            new = {('hgrn_w_in', j): g['w_in'], ('hgrn_w_o', j): g['w_o']}
            part['hgrn_g_norm'][j] = g['g_norm'][0]
            d_lower[j] = g['lb'][0]
        if kind != 'ffn':
            take_grads(carried(k // 2), got)
        mine.update(new)
        if k:
            (dx, dy), (d_gate[k - 1], d_scale[k], d_shift[k]), (d_lng[k - 1], d_lnb[k - 1]) = rowwise_bwd(
                "ln_mod_bwd", f_ln_mod, [_full(xs[k - 1]), _full(ys[k - 1])], [gate[k - 1], scale[k], shift[k]],
                [lng[k - 1], lnb[k - 1]], [(dx, [d]), (dh, [d])], [(F32, [0]), (BF16, [1])], ts=ROW_TILE, n_diff=2)
        else:
            (grad_x,), (d_scale[0], d_shift[0]), _ = rowwise_bwd(
                "mod_first_bwd", f_mod_with_x, [_full(x)], [scale[0], shift[0]], [], [(dh, [d]), (dx, [d])],
                [(F32, [0])], ts=ROW_TILE, n_diff=1)

    waiting = mixer_names(0)
    slabs = grad_items(waiting)
    slabs.append((jnp.stack(d_lower).reshape(n_hgrn, N_DEV, -1).transpose(1, 0, 2), "a2a"))
    for parts in (d_lng, d_lnb):
        full = jnp.stack([p[0] for p in parts]).reshape(depth, 2, N_DEV, d // N_DEV)
        slabs.append((full.transpose(2, 0, 1, 3), "a2a"))
    dmod = jnp.concatenate([jnp.stack(d_shift), jnp.stack(d_scale), jnp.stack(d_gate)], axis=-1)
    slabs.append((dmod.reshape(n_sub, bsz, N_DEV, cols).transpose(2, 0, 1, 3), "a2a"))
    small = ['mla_q_norm', 'mla_kv_norm', 'hgrn_g_norm']
    slabs += [(jnp.stack(part[n]), "gather") for n in small]
    got = exchange("scatter_last", slabs)
    take_grads(waiting, got)
    stacks = dict(zip(['hgrn_lb', 'ln_g', 'ln_b', 'dmod'] + small, got[len(waiting):]))

    dmod_all = stacks['dmod'].transpose(1, 0, 2, 3).reshape(n_sub, N_DEV * bsz, cols)
    g_ada_w, g_ada_b = ada_bwd(c_all, dmod_all)
    stacks['ada_w'] = g_ada_w.reshape((1,) + ada_w.shape)
    stacks['ada_b'] = g_ada_b.reshape((1,) + ada_b.shape)

    res = {}
    for n in WEIGHTS:
        if n == 'hgrn_lb':
            res[n] = adam_lb(stacks[n], W[n], M1[n], M2[n])
        elif n in big:
            res[n] = adam_layers("adam_" + n, recv[n], W[n], M1[n], M2[n])
        else:
            res[n] = adam("adam_" + n, stacks[n], W[n], M1[n], M2[n])
    return (loss, grad_x, *[res[n][0] for n in WEIGHTS], *[res[n][1] for n in WEIGHTS], *[res[n][2] for n in WEIGHTS],
            *[res[n][3] for n in WEIGHTS])
```

```python
import functools
import math

import numpy as np
import jax
import jax.numpy as jnp
from jax import lax
from jax.experimental import pallas as pl
from jax.experimental.pallas import tpu as pltpu

F32 = jnp.float32
BF16 = jnp.bfloat16

N_DEV = 8
LANES = 128
VMEM_LIMIT = 52 * 1024 * 1024

D_MODEL = 1024
DEPTH = 4
MLA_HEADS = 16
QK_NOPE = 64
QK_ROPE = 32
V_HEAD = 64
Q_LORA = 768
KV_LORA = 256
ROPE_THETA = 10000.0
HGRN_EXPAND = 128
HGRN_CHUNK = 128
HGRN_HEADS_PER_STEP = 4
D_FF = 2816
ALPHA = (2.0 * DEPTH) ** 0.25
LN_EPS = 1e-5
RMS_EPS = 1e-6
ADAM_LR = 0.001
ADAM_B1 = 0.9
ADAM_B2 = 0.999
ADAM_EPS = 1e-08
ADAM_WD = 0.01
ADAM_STEP = 10

ATTN_TQ = 512
ROW_TILE = 256

WEIGHTS = ['mla_w_in', 'mla_q_norm', 'mla_w_qb', 'mla_kv_norm', 'mla_w_kvb', 'mla_w_o', 'hgrn_lb', 'hgrn_w_in',
           'hgrn_g_norm', 'hgrn_w_o', 'ffn_w_in', 'ffn_w_out', 'ada_w', 'ada_b', 'ln_g', 'ln_b']
COL_SHARDED = ['mla_w_in', 'mla_w_qb', 'mla_w_kvb', 'hgrn_w_in', 'ffn_w_in']
ROW_SHARDED = ['mla_w_o', 'hgrn_w_o', 'ffn_w_out']


def _params(*sem):
    if sem:
        return pltpu.CompilerParams(dimension_semantics=sem, vmem_limit_bytes=VMEM_LIMIT)
    return pltpu.CompilerParams(vmem_limit_bytes=VMEM_LIMIT)


def _pick(n, cap):
    best = None
    for t in range(LANES, min(n, cap) + 1, LANES):
        if n % t == 0:
            best = t
    return best or n


def _pick_rows(n, cap):
    best = None
    for t in range(8, min(n, cap) + 1, 8):
        if n % t == 0:
            best = t
    return best or n


def matmul(name, a, b, *, ta=False, tb=False, out_dtype=F32, tm_cap=1024, tn_cap=1536, tk_cap=2048):
    parts = list(a) if isinstance(a, (list, tuple)) else [a]
    n_parts = len(parts)
    assert n_parts == 1 or not ta
    (kp, m) = parts[0].shape if ta else parts[0].shape[::-1]
    (n, k2) = b.shape if tb else b.shape[::-1]
    assert kp * n_parts == k2, (name, parts[0].shape, b.shape)
    tm, tn, tk = _pick(m, tm_cap), _pick(n, tn_cap), _pick(kp, tk_cap)
    nkp = kp // tk
    nk = nkp * n_parts
    dims = (((0 if ta else 1,), (1 if tb else 0,)), ((), ()))

    def prod(a_ref, b_ref):
        return lax.dot_general(a_ref[...].astype(BF16), b_ref[...].astype(BF16), dims, preferred_element_type=F32)

    if nk == 1:
        def body(a_ref, b_ref, o_ref):
            o_ref[...] = prod(a_ref, b_ref).astype(o_ref.dtype)
        scratch = []
    else:
        def body(*refs):
            a_refs, (b_ref, o_ref, acc_ref) = refs[:n_parts], refs[n_parts:]
            k = pl.program_id(2)

            @pl.when(k == 0)
            def _():
                acc_ref[...] = jnp.zeros_like(acc_ref)

            if n_parts == 1:
                acc_ref[...] += prod(a_refs[0], b_ref)
            else:
                for p in range(n_parts):
                    @pl.when((k >= p * nkp) & (k < (p + 1) * nkp))
                    def _(p=p):
                        acc_ref[...] += prod(a_refs[p], b_ref)

            @pl.when(k == nk - 1)
            def _():
                o_ref[...] = acc_ref[...].astype(o_ref.dtype)
        scratch = [pltpu.VMEM((tm, tn), F32)]

    if ta:
        a_specs = [pl.BlockSpec((tk, tm), lambda i, j, k: (k, i))]
    elif n_parts == 1:
        a_specs = [pl.BlockSpec((tm, tk), lambda i, j, k: (i, k))]
    else:
        a_specs = [pl.BlockSpec((tm, tk), lambda i, j, k, p=p: (i, jnp.clip(k - p * nkp, 0, nkp - 1))) for p in range(n_parts)]
    b_spec = pl.BlockSpec((tn, tk), lambda i, j, k: (j, k)) if tb else pl.BlockSpec((tk, tn), lambda i, j, k: (k, j))
    return pl.pallas_call(
        body, name=name, grid=(m // tm, n // tn, nk),
        in_specs=a_specs + [b_spec], out_specs=pl.BlockSpec((tm, tn), lambda i, j, k: (i, j)),
        out_shape=jax.ShapeDtypeStruct((m, n), out_dtype), scratch_shapes=scratch,
        compiler_params=_params("parallel", "parallel", "arbitrary"),
    )(*parts, b)


def mm3(name, a3, w, **kw):
    parts = list(a3) if isinstance(a3, (list, tuple)) else [a3]
    bsz, s = parts[0].shape[:2]
    flat = [p.reshape(bsz * s, p.shape[-1]) for p in parts]
    out = matmul(name, flat if len(flat) > 1 else flat[0], w, **kw)
    return out.reshape(bsz, s, out.shape[-1])


def ffn_in_act(h3, w_in, ride=()):
    bsz, s, k = h3.shape
    m, dff = bsz * s, w_in.shape[1] // 2
    tm, tn = _pick(m, 512), _pick(dff, 1536)
    nj = dff // tn
    grid = (nj, m // tm)
    rd = _Ride(ride)

    def body(*refs):
        (h_ref, wg_ref, wu_ref), srcs, (ug_ref, uu_ref, a_ref), outs, scratch = rd.split(refs, 3, 3)
        rd.run(srcs, outs, scratch, grid)
        hv = h_ref[...].astype(BF16)
        ug = _dg(hv, wg_ref[...].astype(BF16), 1, 0)
        uu = _dg(hv, wu_ref[...].astype(BF16), 1, 0)
        ug_ref[...] = ug.astype(ug_ref.dtype)
        uu_ref[...] = uu.astype(uu_ref.dtype)
        a_ref[...] = (jax.nn.silu(ug) * uu).astype(a_ref.dtype)

    out = pl.BlockSpec((tm, tn), lambda j, i: (i, j))
    res = pl.pallas_call(
        body, name="ffn_in_act", grid=grid,
        in_specs=[pl.BlockSpec((tm, k), lambda j, i: (i, 0)), pl.BlockSpec((k, tn), lambda j, i: (0, j)),
                  pl.BlockSpec((k, tn), lambda j, i: (0, nj + j))] + rd.in_specs,
        out_specs=[out, out, out] + rd.out_specs,
        out_shape=[jax.ShapeDtypeStruct((m, dff), BF16)] * 3 + rd.out_shape, scratch_shapes=rd.scratch,
        compiler_params=_params("arbitrary", "arbitrary"),
    )(h3.reshape(m, k), w_in, w_in, *rd.srcs)
    return [r.reshape(bsz, s, dff) for r in res[:3]], res[3:]


def ffn_out_dx_act(dy3, w_out, ug, uu):
    bsz, s, d = dy3.shape
    m, dff = bsz * s, w_out.shape[0]
    tm, tn = _pick(m, 512), _pick(dff, 1536)

    def body(dy_ref, w_ref, ug_ref, uu_ref, dg_ref, du_ref):
        da = _dg(dy_ref[...].astype(BF16), w_ref[...].astype(BF16), 1, 1)
        _, vjp = jax.vjp(lambda gate, up: jax.nn.silu(gate) * up, ug_ref[...].astype(F32), uu_ref[...].astype(F32))
        dg, du = vjp(da)
        dg_ref[...] = dg.astype(dg_ref.dtype)
        du_ref[...] = du.astype(du_ref.dtype)

    blk = pl.BlockSpec((tm, tn), lambda j, i: (i, j))
    res = pl.pallas_call(
        body, name="ffn_out_dx_act", grid=(dff // tn, m // tm),
        in_specs=[pl.BlockSpec((tm, d), lambda j, i: (i, 0)), pl.BlockSpec((tn, d), lambda j, i: (j, 0)), blk, blk],
        out_specs=[blk, blk], out_shape=[jax.ShapeDtypeStruct((m, dff), BF16)] * 2,
        compiler_params=_params("arbitrary", "arbitrary"),
    )(dy3.reshape(m, d), w_out, ug.reshape(m, dff), uu.reshape(m, dff))
    return [r.reshape(bsz, s, dff) for r in res]


def wgrad(name, a3, g3):
    bsz, s, k = a3.shape
    return matmul(name, a3.reshape(bsz * s, k), g3.reshape(bsz * s, g3.shape[-1]), ta=True, out_dtype=BF16)


def _dg(a, b, ca, cb, **kw):
    return lax.dot_general(a, b, (((ca,), (cb,)), ((), ())), preferred_element_type=F32, **kw)


@functools.partial(jax.custom_vjp, nondiff_argnums=(2, 3))
def bdot(a, b, ca, cb):
    return _dg(a.astype(BF16), b.astype(BF16), ca, cb)


def _bdot_fwd(a, b, ca, cb):
    return bdot(a, b, ca, cb), (a, b)


def _bdot_bwd(ca, cb, res, g):
    a, b = res
    a16, b16, g16 = a.astype(BF16), b.astype(BF16), g.astype(BF16)
    if ca == 1:
        da = _dg(g16, b16, 1, 1 if cb == 0 else 0)
    else:
        da = _dg(b16, g16, 1 if cb == 0 else 0, 1)
    if cb == 0:
        db = _dg(a16, g16, 0 if ca == 1 else 1, 0)
    else:
        db = _dg(g16, a16, 0, 0 if ca == 1 else 1)
    return da, db


bdot.defvjp(_bdot_fwd, _bdot_bwd)


def hdot(a, b, ca=1, cb=0):
    return _dg(a, b, ca, cb, precision=lax.Precision.HIGHEST)


def _row_specs(rows, exs, globs, ts):
    specs = [pl.BlockSpec((1, ts, w), lambda b, s, j=j: (b, s, j)) for (_, j, w) in rows]
    specs += [pl.BlockSpec((1, 1, e.shape[-1]), lambda b, s: (b, 0, 0)) for e in exs]
    specs += [pl.BlockSpec((1, g.shape[-1]), lambda b, s: (0, 0)) for g in globs]
    return specs


def _store_pieces(o_ref, pieces, widths):
    off = 0
    for p, w in zip(pieces, widths):
        o_ref[0, :, off:off + w] = p.astype(o_ref.dtype)
        off += w


def _load_pieces(c_ref, widths):
    out, off = [], 0
    for w in widths:
        out.append(c_ref[0, :, off:off + w].astype(F32))
        off += w
    return out


def rowwise(name, f, rows, exs, globs, outs, *, ts, accs=()):
    bsz, s = rows[0][0].shape[:2]
    ts = min(ts, s)
    n_r, n_e, n_g, n_o = len(rows), len(exs), len(globs), len(outs)

    def body(*refs):
        rv = [r[0].astype(F32) for r in refs[:n_r]]
        ev = [e[0] for e in refs[n_r:n_r + n_e]]
        gv = [g[...] for g in refs[n_r + n_e:n_r + n_e + n_g]]
        o_refs = refs[n_r + n_e + n_g:n_r + n_e + n_g + n_o]
        a_refs = refs[n_r + n_e + n_g + n_o:]
        pieces, sums = f(rv, ev, gv)
        idx = 0
        for o_ref, (_, ws) in zip(o_refs, outs):
            _store_pieces(o_ref, pieces[idx:idx + len(ws)], ws)
            idx += len(ws)
        if accs:
            @pl.when((pl.program_id(0) == 0) & (pl.program_id(1) == 0))
            def _():
                for a_ref in a_refs:
                    a_ref[...] = jnp.zeros_like(a_ref)
            for a_ref, val in zip(a_refs, sums):
                a_ref[...] += val

    out_specs = [pl.BlockSpec((1, ts, sum(ws)), lambda b, s: (b, s, 0)) for (_, ws) in outs]
    out_specs += [pl.BlockSpec((1, w), lambda b, s: (0, 0)) for w in accs]
    out_shape = [jax.ShapeDtypeStruct((bsz, s, sum(ws)), dt) for (dt, ws) in outs]
    out_shape += [jax.ShapeDtypeStruct((1, w), F32) for w in accs]
    return pl.pallas_call(
        body, name=name, grid=(bsz, s // ts),
        in_specs=_row_specs(rows, exs, globs, ts), out_specs=out_specs, out_shape=out_shape,
        compiler_params=_params("arbitrary", "arbitrary"),
    )(*[r[0] for r in rows], *exs, *globs)


def rowwise_bwd(name, f, rows, exs, globs, cts, d_groups, *, ts, n_diff, unit_ct=0):
    bsz, s = rows[0][0].shape[:2]
    ts = min(ts, s)
    n_r, n_e, n_g, n_c = len(rows), len(exs), len(globs), len(cts)
    n_d = len(d_groups)

    def body(*refs):
        rv = [r[0].astype(F32) for r in refs[:n_r]]
        ev = [e[0] for e in refs[n_r:n_r + n_e]]
        gv = [g[...] for g in refs[n_r + n_e:n_r + n_e + n_g]]
        base = n_r + n_e + n_g
        c_refs = refs[base:base + n_c]
        d_refs = refs[base + n_c:base + n_c + n_d]
        de_refs = refs[base + n_c + n_d:base + n_c + n_d + n_e]
        dg_refs = refs[base + n_c + n_d + n_e:]
        fixed = rv[n_diff:]
        out, vjp = jax.vjp(lambda r, e, g: f(r + fixed, e, g), rv[:n_diff], ev, gv)
        ct = []
        for c_ref, (_, ws) in zip(c_refs, cts):
            ct += _load_pieces(c_ref, ws)
        ct += [jnp.ones_like(o) for o in out[len(ct):]]
        assert len(ct) == len(out) and len(out) - unit_ct == sum(len(ws) for _, ws in cts), name
        d_r, d_e, d_g = vjp(ct)
        for d_ref, (_, idxs) in zip(d_refs, d_groups):
            _store_pieces(d_ref, [d_r[i] for i in idxs], [rows[i][2] for i in idxs])
        first_s = pl.program_id(1) == 0
        if n_e:
            @pl.when(first_s)
            def _():
                for r in de_refs:
                    r[...] = jnp.zeros_like(r)
            for r, val in zip(de_refs, d_e):
                r[0] += val
        if n_g:
            @pl.when(first_s & (pl.program_id(0) == 0))
            def _():
                for r in dg_refs:
                    r[...] = jnp.zeros_like(r)
            for r, val in zip(dg_refs, d_g):
                r[...] += val

    in_specs = _row_specs(rows, exs, globs, ts)
    in_specs += [pl.BlockSpec((1, ts, sum(ws)), lambda b, s: (b, s, 0)) for (_, ws) in cts]
    out_specs = [pl.BlockSpec((1, ts, sum(rows[i][2] for i in idxs)), lambda b, s: (b, s, 0)) for (_, idxs) in d_groups]
    out_specs += [pl.BlockSpec((1, 1, e.shape[-1]), lambda b, s: (b, 0, 0)) for e in exs]
    out_specs += [pl.BlockSpec((1, g.shape[-1]), lambda b, s: (0, 0)) for g in globs]
    out_shape = [jax.ShapeDtypeStruct((bsz, s, sum(rows[i][2] for i in idxs)), dt) for (dt, idxs) in d_groups]
    out_shape += [jax.ShapeDtypeStruct(e.shape, F32) for e in exs]
    out_shape += [jax.ShapeDtypeStruct(g.shape, F32) for g in globs]
    res = pl.pallas_call(
        body, name=name, grid=(bsz, s // ts),
        in_specs=in_specs, out_specs=out_specs, out_shape=out_shape,
        compiler_params=_params("arbitrary", "arbitrary"),
    )(*[r[0] for r in rows], *exs, *globs, *[c[0] for c in cts])
    return res[:n_d], res[n_d:n_d + n_e], res[n_d + n_e:]


def _full(a):
    return (a, 0, a.shape[-1])


def _view(a, col, w):
    assert col % w == 0
    return (a, col // w, w)


def _layer_norm(z, g, b):
    mu = jnp.mean(z, -1, keepdims=True)
    var = jnp.mean(jnp.square(z - mu), -1, keepdims=True)
    return (z - mu) * lax.rsqrt(var + LN_EPS) * g + b


def _rms_norm(z, g):
    ms = jnp.mean(jnp.square(z), -1, keepdims=True)
    return z * lax.rsqrt(ms + RMS_EPS) * g


def f_mod(rv, ev, gv):
    (x,), (scale, shift) = rv, ev
    return [x * (1.0 + scale) + shift]


def f_mod_with_x(rv, ev, gv):
    return f_mod(rv, ev, gv) + [rv[0]]


def f_ln_mod(rv, ev, gv):
    (x, y), (gate, scale, shift), (g, b) = rv, ev, gv
    xn = _layer_norm(ALPHA * x + (1.0 + gate) * y, g, b)
    return [xn, xn * (1.0 + scale) + shift]


def f_ln_loss(rv, ev, gv):
    (x, y, target), (gate,), (g, b) = rv, ev, gv
    xn = _layer_norm(ALPHA * x + (1.0 + gate) * y, g, b)
    return [0.5 * jnp.mean(jnp.square(xn - target), -1, keepdims=True)]


def _head_spread(width):
    r2 = QK_ROPE // 2
    j = lax.broadcasted_iota(jnp.int32, (LANES, width), 0)
    col = lax.broadcasted_iota(jnp.int32, (LANES, width), 1) % r2
    return (j == col).astype(F32), (j == col + r2).astype(F32)


def f_mla_mid(rv, ev, gv):
    (q_lat, kv_lat, kr, cos, sin), (q_g, kv_g) = rv, gv
    e1, e2 = _head_spread(cos.shape[-1])
    k1, k2 = hdot(kr, e1), hdot(kr, e2)
    return [_rms_norm(q_lat, q_g), _rms_norm(kv_lat, kv_g), k1 * cos - k2 * sin, k1 * sin + k2 * cos]


def rope_tables(positions):
    bsz, s = positions.shape
    r2 = QK_ROPE // 2
    width = MLA_HEADS * r2
    inv = (ROPE_THETA ** (-np.arange(0, QK_ROPE, 2, dtype=np.float32) / QK_ROPE)).astype(np.float32)
    inv = jnp.asarray(np.tile(inv, MLA_HEADS)[None, :])
    ts = min(ROW_TILE, s)

    def body(p_ref, inv_ref, cos_ref, sin_ref):
        ang = p_ref[0].astype(F32) * inv_ref[...]
        cos_ref[0] = jnp.cos(ang)
        sin_ref[0] = jnp.sin(ang)

    spec = pl.BlockSpec((1, ts, width), lambda b, s: (b, s, 0))
    return pl.pallas_call(
        body, name="rope_tables", grid=(bsz, s // ts),
        in_specs=[pl.BlockSpec((1, ts, 1), lambda b, s: (b, s, 0)), pl.BlockSpec((1, width), lambda b, s: (0, 0))],
        out_specs=[spec, spec], out_shape=[jax.ShapeDtypeStruct((bsz, s, width), F32)] * 2,
        compiler_params=_params("arbitrary", "arbitrary"),
    )(positions[:, :, None], inv)


def _attn_probs(q, k, row0):
    scale = (QK_NOPE + QK_ROPE) ** -0.5
    s = _dg(q, k, 1, 1) * scale
    rows = row0 + lax.broadcasted_iota(jnp.int32, s.shape, 0)
    cols = lax.broadcasted_iota(jnp.int32, s.shape, 1)
    s = jnp.where(cols <= rows, s, jnp.finfo(F32).min)
    e = jnp.exp(s - jnp.max(s, -1, keepdims=True))
    return e / jnp.sum(e, -1, keepdims=True), scale


ATTN_PAIR = 2


def attn_fwd(q, k, v, ride=()):
    bsz, h, s, dq = q.shape
    dv = v.shape[-1]
    tq = min(ATTN_TQ, s)
    grid = (bsz, h // ATTN_PAIR, s // tq)
    rd = _Ride(ride)

    def body(*refs):
        (q_ref, k_ref, v_ref), srcs, (o_ref,), outs, sems = rd.split(refs, 3, 1)
        rd.run(srcs, outs, sems, grid)
        for i in range(grid[2]):
            @pl.when(pl.program_id(2) == i)
            def _(i=i):
                kend = (i + 1) * tq
                for e in range(ATTN_PAIR):
                    p, _ = _attn_probs(q_ref[0, e], k_ref[0, e, :kend, :], i * tq)
                    o_ref[0, :, e * dv:(e + 1) * dv] = _dg(p.astype(BF16), v_ref[0, e, :kend, :], 1, 0).astype(o_ref.dtype)

    res = pl.pallas_call(
        body, name="attn_fwd", grid=grid,
        in_specs=[pl.BlockSpec((1, ATTN_PAIR, tq, dq), lambda b, h, i: (b, h, i, 0)),
                  pl.BlockSpec((1, ATTN_PAIR, s, dq), lambda b, h, i: (b, h, 0, 0)),
                  pl.BlockSpec((1, ATTN_PAIR, s, dv), lambda b, h, i: (b, h, 0, 0))] + rd.in_specs,
        out_specs=[pl.BlockSpec((1, tq, ATTN_PAIR * dv), lambda b, h, i: (b, i, h))] + rd.out_specs,
        out_shape=[jax.ShapeDtypeStruct((bsz, s, h * dv), BF16)] + rd.out_shape, scratch_shapes=rd.scratch,
        compiler_params=_params("arbitrary", "arbitrary", "arbitrary"),
    )(q, k, v, *rd.srcs)
    return res[0], res[1:]


def attn_bwd(q, k, v, do, ride=()):
    bsz, h, s, dq = q.shape
    dv = v.shape[-1]
    tq = min(ATTN_TQ, s)
    grid = (bsz, h // ATTN_PAIR, s // tq)
    rd = _Ride(ride)

    def body(*refs):
        (q_ref, k_ref, v_ref, do_ref), srcs, (dq_ref, dk_ref, dv_ref), outs, sems = rd.split(refs, 4, 3)
        rd.run(srcs, outs, sems, grid)

        @pl.when(pl.program_id(2) == 0)
        def _():
            dk_ref[...] = jnp.zeros_like(dk_ref)
            dv_ref[...] = jnp.zeros_like(dv_ref)

        for i in range(grid[2]):
            @pl.when(pl.program_id(2) == i)
            def _(i=i):
                kend = (i + 1) * tq
                for e in range(ATTN_PAIR):
                    qv, kv, vv = q_ref[0, e], k_ref[0, e, :kend, :], v_ref[0, e, :kend, :]
                    p, scale = _attn_probs(qv, kv, i * tq)
                    do16 = do_ref[0, :, e * dv:(e + 1) * dv].astype(BF16)
                    dv_ref[0, e, :kend, :] += _dg(p.astype(BF16), do16, 0, 0)
                    dp = _dg(do16, vv, 1, 1)
                    ds = (p * (dp - jnp.sum(dp * p, -1, keepdims=True)) * scale).astype(BF16)
                    dq_ref[0, e] = _dg(ds, kv, 1, 0)
                    dk_ref[0, e, :kend, :] += _dg(ds, qv, 0, 0)

    res = pl.pallas_call(
        body, name="attn_bwd", grid=grid,
        in_specs=[pl.BlockSpec((1, ATTN_PAIR, tq, dq), lambda b, h, i: (b, h, i, 0)),
                  pl.BlockSpec((1, ATTN_PAIR, s, dq), lambda b, h, i: (b, h, 0, 0)),
                  pl.BlockSpec((1, ATTN_PAIR, s, dv), lambda b, h, i: (b, h, 0, 0)),
                  pl.BlockSpec((1, tq, ATTN_PAIR * dv), lambda b, h, i: (b, i, h))] + rd.in_specs,
        out_specs=[pl.BlockSpec((1, ATTN_PAIR, tq, dq), lambda b, h, i: (b, h, i, 0)),
                   pl.BlockSpec((1, ATTN_PAIR, s, dq), lambda b, h, i: (b, h, 0, 0)),
                   pl.BlockSpec((1, ATTN_PAIR, s, dv), lambda b, h, i: (b, h, 0, 0))] + rd.out_specs,
        out_shape=[jax.ShapeDtypeStruct((bsz, h, s, dq), F32), jax.ShapeDtypeStruct((bsz, h, s, dq), F32),
                   jax.ShapeDtypeStruct((bsz, h, s, dv), F32)] + rd.out_shape, scratch_shapes=rd.scratch,
        compiler_params=_params("arbitrary", "arbitrary", "arbitrary"),
    )(q, k, v, do, *rd.srcs)
    return res[0], res[1], res[2], res[3:]


def mla_heads(q, kv, krt, cos, sin):
    bsz, s, _ = q.shape
    nh, n, r2, vd = MLA_HEADS, QK_NOPE, QK_ROPE // 2, V_HEAD
    ts = min(ROW_TILE, s)

    def body(q_ref, kv_ref, kr_ref, cos_ref, sin_ref, qh_ref, kh_ref, vh_ref):
        cos, sin = cos_ref[0], sin_ref[0]
        qv, kvv, kr = q_ref[0], kv_ref[0], kr_ref[0].astype(F32)
        q1, q2 = qv[:, nh * n:nh * (n + r2)], qv[:, nh * (n + r2):]
        qr = [q1 * cos - q2 * sin, q1 * sin + q2 * cos]
        for h in range(nh):
            qh_ref[0, h, :, :n] = qv[:, h * n:(h + 1) * n].astype(BF16)
            kh_ref[0, h, :, :n] = kvv[:, h * n:(h + 1) * n].astype(BF16)
            vh_ref[0, h] = kvv[:, nh * n + h * vd:nh * n + (h + 1) * vd].astype(BF16)
            for j in range(2):
                qh_ref[0, h, :, n + j * r2:n + (j + 1) * r2] = qr[j][:, h * r2:(h + 1) * r2].astype(BF16)
                kh_ref[0, h, :, n + j * r2:n + (j + 1) * r2] = kr[:, (j * nh + h) * r2:(j * nh + h + 1) * r2].astype(BF16)

    def row(a):
        return pl.BlockSpec((1, ts, a.shape[-1]), lambda b, i: (b, i, 0))

    def heads(w):
        return pl.BlockSpec((1, nh, ts, w), lambda b, i: (b, 0, i, 0))

    return pl.pallas_call(
        body, name="mla_heads", grid=(bsz, s // ts), in_specs=[row(q), row(kv), row(krt), row(cos), row(sin)],
        out_specs=[heads(n + 2 * r2), heads(n + 2 * r2), heads(vd)],
        out_shape=[jax.ShapeDtypeStruct((bsz, nh, s, n + 2 * r2), BF16)] * 2 + [jax.ShapeDtypeStruct((bsz, nh, s, vd), BF16)],
        compiler_params=_params("arbitrary", "arbitrary"),
    )(q, kv, krt, cos, sin)


def mla_heads_bwd(dqh, dkh, dvh, cos, sin):
    bsz, nh, s, _ = dqh.shape
    n, r2, vd = QK_NOPE, QK_ROPE // 2, V_HEAD
    ts = min(ROW_TILE, s)

    def body(dq_ref, dk_ref, dv_ref, cos_ref, sin_ref, oq_ref, okv_ref, okr_ref, tq_acc, tkv_acc, rot):
        for h in range(nh):
            tq_acc[:, h * n:(h + 1) * n] = dq_ref[0, h, :, :n]
            tkv_acc[:, h * n:(h + 1) * n] = dk_ref[0, h, :, :n]
            tkv_acc[:, nh * n + h * vd:nh * n + (h + 1) * vd] = dv_ref[0, h]
            for j in range(2):
                rot[j, :, h * r2:(h + 1) * r2] = dq_ref[0, h, :, n + j * r2:n + (j + 1) * r2]
                okr_ref[0, :, (j * nh + h) * r2:(j * nh + h + 1) * r2] = dk_ref[0, h, :, n + j * r2:n + (j + 1) * r2]
        cos, sin = cos_ref[0], sin_ref[0]
        d1, d2 = rot[0], rot[1]
        oq_ref[0, :, :nh * n] = tq_acc[...].astype(BF16)
        oq_ref[0, :, nh * n:nh * (n + r2)] = (d1 * cos + d2 * sin).astype(BF16)
        oq_ref[0, :, nh * (n + r2):] = (d2 * cos - d1 * sin).astype(BF16)
        okv_ref[0] = tkv_acc[...].astype(BF16)

    def row(w):
        return pl.BlockSpec((1, ts, w), lambda b, i: (b, i, 0))

    def heads(w):
        return pl.BlockSpec((1, nh, ts, w), lambda b, i: (b, 0, i, 0))

    return pl.pallas_call(
        body, name="mla_heads_bwd", grid=(bsz, s // ts),
        in_specs=[heads(n + 2 * r2), heads(n + 2 * r2), heads(vd), row(nh * r2), row(nh * r2)],
        out_specs=[row(nh * (n + 2 * r2)), row(nh * (n + vd)), row(2 * nh * r2)],
        out_shape=[jax.ShapeDtypeStruct((bsz, s, nh * (n + 2 * r2)), BF16), jax.ShapeDtypeStruct((bsz, s, nh * (n + vd)), BF16),
                   jax.ShapeDtypeStruct((bsz, s, 2 * nh * r2), F32)],
        scratch_shapes=[pltpu.VMEM((ts, nh * n), F32), pltpu.VMEM((ts, nh * (n + vd)), F32), pltpu.VMEM((2, ts, nh * r2), F32)],
        compiler_params=_params("arbitrary", "arbitrary"),
    )(dqh, dkh, dvh, cos, sin)


def _hgrn_tables(c):
    levels = c.bit_length() - 1
    assert 1 << levels == c
    r = np.arange(c)
    sign, mask = [], []
    for l in range(levels):
        lower = ((r >> l) & 1) == 1
        sign.append(np.broadcast_to(np.where(lower, 1.0, -1.0)[:, None], (c, LANES)))
        mask.append((((r[:, None] ^ r[None, :]) >> l) == 1) & lower[:, None])
    return (jnp.asarray(r[:, None] >= r[None, :], BF16), jnp.asarray(np.stack(sign), F32), jnp.asarray(np.stack(mask), F32))


def _const_specs(tables):
    return [pl.BlockSpec(t.shape, lambda b, h, i, nd=t.ndim: (0,) * nd) for t in tables]


def _split3(x):
    hi = x.astype(BF16)
    rest = x - hi.astype(F32)
    mid = rest.astype(BF16)
    return hi, mid, (rest - mid.astype(F32)).astype(BF16)


@functools.partial(jax.custom_vjp, nondiff_argnums=(2,))
def prefix_sums(p, g, n):
    k = g.shape[1]
    r = _dg(p, jnp.concatenate(_split3(g), axis=1), 1, 0)
    r = r[:, :k] + r[:, k:2 * k] + r[:, 2 * k:]
    c = r.shape[0] // n
    return tuple(r[i * c:(i + 1) * c] for i in range(n))


def _prefix_fwd(p, g, n):
    return prefix_sums(p, g, n), p


def _prefix_bwd(n, p, ct):
    ct = jnp.concatenate(ct, axis=0)
    k = ct.shape[1]
    r = _dg(p, jnp.concatenate(_split3(ct), axis=1), 0, 0)
    return jnp.zeros_like(p), r[:, :k] + r[:, k:2 * k] + r[:, 2 * k:]


prefix_sums.defvjp(_prefix_fwd, _prefix_bwd)


@functools.partial(jax.custom_vjp, nondiff_argnums=(1, 2))
def block_row(x, size, row):
    c, k = x.shape
    x3 = x.reshape(c // size, size, k)
    return jnp.broadcast_to(x3[:, row:row + 1, :], x3.shape).reshape(c, k)


def _block_row_fwd(x, size, row):
    return block_row(x, size, row), None


def _block_row_bwd(size, row, _, ct):
    c, k = ct.shape
    ct3 = ct.reshape(c // size, size, k)
    total = jnp.broadcast_to(jnp.sum(ct3, axis=1, keepdims=True), ct3.shape)
    rows = lax.broadcasted_iota(jnp.int32, ct3.shape, 1)
    return (jnp.where(rows == row, total, 0.0).reshape(c, k),)


block_row.defvjp(_block_row_fwd, _block_row_bwd)


def _hgrn_chunk(q, g, k, v, st0, prefix, sign, mask):
    levels = len(mask)
    (b,) = prefix_sums(prefix, g, 1)
    o = bdot(q * jnp.exp(b), st0, 1, 1)
    att = None
    for l in range(levels):
        e = jnp.exp((b - block_row(b, 2 << l, (1 << l) - 1)) * sign[l])
        a = bdot(q * e, k * e, 1, 1) * mask[l]
        att = a if att is None else att + a
    o = o + bdot(att, v, 1, 0) + jnp.sum(q * k, -1, keepdims=True) * v
    total = jnp.sum(g, 0, keepdims=True)
    st1 = st0 * jnp.exp(total) + bdot(v, k * jnp.exp(total - b), 0, 0)
    return o, st1


def _hgrn_step(q_raw, fx, v, g_raw, st0, lb, gn, prefix, sign, mask):
    f = lb + (1.0 - lb) * jax.nn.sigmoid(fx)
    o, st1 = _hgrn_chunk(jax.nn.silu(q_raw), jnp.log(f), 1.0 - f, v, st0, prefix, sign, mask)
    return _rms_norm(o, gn) * jax.nn.silu(g_raw), st1


def _hgrn_layout(proj):
    bsz, s, width = proj.shape
    kd = HGRN_EXPAND
    c = min(HGRN_CHUNK, s)
    nh = width // (4 * kd)
    hp = math.gcd(nh, HGRN_HEADS_PER_STEP)
    return bsz, s, kd, c, nh, s // c, hp


def hgrn_fwd(proj, lb, gn, ride=()):
    bsz, s, kd, c, nh, nc, hp = _hgrn_layout(proj)
    groups = nh // hp
    tables = _hgrn_tables(c)
    levels = tables[2].shape[0]
    grid = (groups, bsz, nc)
    rd = _Ride(ride)

    def body(*refs):
        ((q_ref, f_ref, v_ref, g_ref, lb_ref, gn_ref, p_ref, sg_ref, mk_ref), srcs, (z_ref, st_ref), outs,
         scratch) = rd.split(refs, 9, 2)
        state = scratch[0]
        rd.run(srcs, outs, scratch, grid)

        @pl.when(pl.program_id(2) == 0)
        def _():
            state[...] = jnp.zeros_like(state)

        prefix, sign, mask = p_ref[...], [sg_ref[l] for l in range(levels)], [mk_ref[l] for l in range(levels)]
        for j in range(hp):
            cols = slice(j * kd, (j + 1) * kd)
            st0 = state[j]
            st_ref[0, j, 0] = st0
            z, st1 = _hgrn_step(q_ref[0, :, cols], f_ref[0, :, cols], v_ref[0, :, cols], g_ref[0, :, cols], st0,
                                lb_ref[:, cols], gn_ref[...], prefix, sign, mask)
            z_ref[0, :, cols] = z.astype(z_ref.dtype)
            state[j] = st1

    def part(k):
        return pl.BlockSpec((1, c, hp * kd), lambda h, b, i: (b, i, k * groups + h))

    res = pl.pallas_call(
        body, name="hgrn_fwd", grid=grid,
        in_specs=[part(0), part(1), part(2), part(3), pl.BlockSpec((1, hp * kd), lambda h, b, i: (0, h)),
                  pl.BlockSpec((1, kd), lambda h, b, i: (0, 0))] + _const_specs(tables) + rd.in_specs,
        out_specs=[part(0), pl.BlockSpec((1, hp, 1, kd, kd), lambda h, b, i: (b, h, i, 0, 0))] + rd.out_specs,
        out_shape=[jax.ShapeDtypeStruct((bsz, s, nh * kd), BF16), jax.ShapeDtypeStruct((bsz, nh, nc, kd, kd), F32)] + rd.out_shape,
        scratch_shapes=[pltpu.VMEM((hp, kd, kd), F32)] + rd.scratch,
        compiler_params=_params("arbitrary", "arbitrary", "arbitrary"),
    )(proj, proj, proj, proj, lb, gn, *tables, *rd.srcs)
    return res[0], res[1], res[2:]


def hgrn_bwd(proj, states, dz, lb, gn, ride=()):
    bsz, s, kd, c, nh, nc, hp = _hgrn_layout(proj)
    groups = nh // hp
    tables = _hgrn_tables(c)
    levels = tables[2].shape[0]
    grid = (groups, bsz, nc)
    rd = _Ride(ride)

    def body(*refs):
        ((q_ref, f_ref, v_ref, g_ref, st_ref, dz_ref, lb_ref, gn_ref, p_ref, sg_ref, mk_ref), srcs,
         (dq_ref, df_ref, dv_ref, dg_ref, dlb_ref, dgn_ref), outs, scratch) = rd.split(refs, 11, 6)
        dstate = scratch[0]
        rd.run(srcs, outs, scratch, grid)
        first_of_group = (pl.program_id(1) == 0) & (pl.program_id(2) == 0)

        @pl.when(pl.program_id(2) == 0)
        def _():
            dstate[...] = jnp.zeros_like(dstate)

        @pl.when(first_of_group)
        def _():
            dlb_ref[...] = jnp.zeros_like(dlb_ref)

        @pl.when(first_of_group & (pl.program_id(0) == 0))
        def _():
            dgn_ref[...] = jnp.zeros_like(dgn_ref)

        prefix, sign, mask = p_ref[...], [sg_ref[l] for l in range(levels)], [mk_ref[l] for l in range(levels)]
        for j in range(hp):
            cols = slice(j * kd, (j + 1) * kd)
            _, vjp = jax.vjp(lambda q, f, v, g, st, lb, gn: _hgrn_step(q, f, v, g, st, lb, gn, prefix, sign, mask),
                             q_ref[0, :, cols], f_ref[0, :, cols], v_ref[0, :, cols], g_ref[0, :, cols], st_ref[0, j, 0],
                             lb_ref[:, cols], gn_ref[...])
            dq, df, dv, dg, dst, dlb, dgn = vjp((dz_ref[0, :, cols], dstate[j]))
            dq_ref[0, :, cols] = dq.astype(dq_ref.dtype)
            df_ref[0, :, cols] = df.astype(df_ref.dtype)
            dv_ref[0, :, cols] = dv.astype(dv_ref.dtype)
            dg_ref[0, :, cols] = dg.astype(dg_ref.dtype)
            dlb_ref[:, cols] += dlb
            dgn_ref[...] += dgn
            dstate[j] = dst

    def part(k):
        return pl.BlockSpec((1, c, hp * kd), lambda h, b, i: (b, nc - 1 - i, k * groups + h))

    shape = jax.ShapeDtypeStruct((bsz, s, nh * kd), BF16)
    lb_spec = pl.BlockSpec((1, hp * kd), lambda h, b, i: (0, h))
    gn_spec = pl.BlockSpec((1, kd), lambda h, b, i: (0, 0))
    res = pl.pallas_call(
        body, name="hgrn_bwd", grid=grid,
        in_specs=[part(0), part(1), part(2), part(3), pl.BlockSpec((1, hp, 1, kd, kd), lambda h, b, i: (b, h, nc - 1 - i, 0, 0)),
                  part(0), lb_spec, gn_spec] + _const_specs(tables) + rd.in_specs,
        out_specs=[part(0)] * 4 + [lb_spec, gn_spec] + rd.out_specs,
        out_shape=[shape] * 4 + [jax.ShapeDtypeStruct(lb.shape, F32), jax.ShapeDtypeStruct(gn.shape, F32)] + rd.out_shape,
        scratch_shapes=[pltpu.VMEM((hp, kd, kd), F32)] + rd.scratch,
        compiler_params=_params("arbitrary", "arbitrary", "arbitrary"),
    )(proj, proj, proj, proj, states, dz, lb, gn, *tables, *rd.srcs)
    return list(res[:4]), res[4], res[5], res[6:]


def cast_bf16(name, w):
    blk = pl.BlockSpec((1,) + w.shape[1:], lambda l: (l, 0, 0))

    def body(w_ref, o_ref):
        o_ref[...] = w_ref[...].astype(BF16)

    return pl.pallas_call(body, name=name, grid=(w.shape[0],), in_specs=[blk], out_specs=blk,
                          out_shape=jax.ShapeDtypeStruct(w.shape, BF16), compiler_params=_params("arbitrary"))(w)


def _lower_bounds(rows):
    m = functools.reduce(jnp.maximum, rows)
    e = [jnp.exp(r - m) for r in rows]
    z = functools.reduce(lambda a, b: a + b, e)
    soft = [x / z for x in e]
    out, run = [], jnp.zeros_like(rows[0])
    for sft in soft:
        run = run + sft
        out.append(run - soft[0])
    return out


def lower_bounds(lb):
    n = lb.shape[0]

    def body(lb_ref, o_ref):
        for i, r in enumerate(_lower_bounds([lb_ref[i:i + 1, :] for i in range(n)])):
            o_ref[i:i + 1, :] = r

    return pl.pallas_call(body, name="lower_bounds", out_shape=jax.ShapeDtypeStruct(lb.shape, F32),
                          compiler_params=_params())(lb)


def ada_fwd(c_all, ada_w, ada_b):
    nl, ns, d, cols = ada_w.shape
    n_ex = c_all.shape[0]

    def body(c_ref, w_ref, b_ref, o_ref):
        a = jax.nn.silu(c_ref[...]).astype(BF16)
        o_ref[0] = _dg(a, w_ref[0].astype(BF16), 1, 0) + b_ref[0]

    return pl.pallas_call(
        body, name="ada_fwd", grid=(nl * ns,),
        in_specs=[pl.BlockSpec((n_ex, d), lambda i: (0, 0)), pl.BlockSpec((1, d, cols), lambda i: (i, 0, 0)),
                  pl.BlockSpec((1, 1, cols), lambda i: (i, 0, 0))],
        out_specs=pl.BlockSpec((1, n_ex, cols), lambda i: (i, 0, 0)),
        out_shape=jax.ShapeDtypeStruct((nl * ns, n_ex, cols), F32), compiler_params=_params("arbitrary"),
    )(c_all, ada_w.reshape(nl * ns, d, cols), ada_b.reshape(nl * ns, 1, cols))


def ada_bwd(c_all, dmod):
    n, n_ex, cols = dmod.shape
    d = c_all.shape[1]

    def body(c_ref, g_ref, dw_ref, db_ref):
        a = jax.nn.silu(c_ref[...]).astype(BF16)
        g = g_ref[0]
        dw_ref[0] = _dg(a, g.astype(BF16), 0, 0)
        db_ref[0] = jnp.sum(g, 0, keepdims=True)

    return pl.pallas_call(
        body, name="ada_bwd", grid=(n,),
        in_specs=[pl.BlockSpec((n_ex, d), lambda i: (0, 0)), pl.BlockSpec((1, n_ex, cols), lambda i: (i, 0, 0))],
        out_specs=[pl.BlockSpec((1, d, cols), lambda i: (i, 0, 0)), pl.BlockSpec((1, 1, cols), lambda i: (i, 0, 0))],
        out_shape=[jax.ShapeDtypeStruct((n, d, cols), F32), jax.ShapeDtypeStruct((n, 1, cols), F32)],
        compiler_params=_params("arbitrary"),
    )(c_all, dmod)


def _adam_math(g, w, m, v):
    m = ADAM_B1 * m + (1.0 - ADAM_B1) * g
    v = ADAM_B2 * v + (1.0 - ADAM_B2) * jnp.square(g)
    m_hat = m / (1.0 - ADAM_B1 ** ADAM_STEP)
    v_hat = v / (1.0 - ADAM_B2 ** ADAM_STEP)
    delta = -ADAM_LR * (m_hat / (jnp.sqrt(v_hat) + ADAM_EPS) + ADAM_WD * w)
    return delta, m, v


def adam(name, gstack, w, m, v):
    shape = w.shape
    n, cols = gstack.shape[0], shape[-1]
    rows = math.prod(shape[:-1])
    tr = _pick_rows(rows, max(8, (2 * 1024 * 1024) // (4 * cols * n)))

    def body(g_ref, w_ref, m_ref, v_ref, go_ref, d_ref, mo_ref, vo_ref):
        g = g_ref[0].astype(F32)
        for i in range(1, n):
            g = g + g_ref[i].astype(F32)
        delta, m1, v1 = _adam_math(g, w_ref[...], m_ref[...], v_ref[...])
        go_ref[...] = g
        d_ref[...] = delta
        mo_ref[...] = m1
        vo_ref[...] = v1

    blk = pl.BlockSpec((tr, cols), lambda i: (i, 0))
    out = pl.pallas_call(
        body, name=name, grid=(rows // tr,),
        in_specs=[pl.BlockSpec((n, tr, cols), lambda i: (0, i, 0)), blk, blk, blk],
        out_specs=[blk] * 4, out_shape=[jax.ShapeDtypeStruct((rows, cols), F32)] * 4,
        compiler_params=_params("arbitrary"),
    )(gstack.reshape(n, rows, cols), w.reshape(rows, cols), m.reshape(rows, cols), v.reshape(rows, cols))
    return [o.reshape(shape) for o in out]


def adam_layers(name, gs, w, m, v):
    shape = w.shape
    nl, n, cols = len(gs), gs[0].shape[0], shape[-1]
    rows = math.prod(shape[1:-1])
    tr = _pick_rows(rows, max(16, (2 * 1024 * 1024) // (4 * cols * n)))
    nt = rows // tr

    def body(*refs):
        g_refs, (w_ref, m_ref, v_ref, go_ref, d_ref, mo_ref, vo_ref) = refs[:nl], refs[nl:]
        for j in range(nl):
            @pl.when(pl.program_id(0) == j)
            def _(j=j):
                g = g_refs[j][0].astype(F32)
                for i in range(1, n):
                    g = g + g_refs[j][i].astype(F32)
                delta, m1, v1 = _adam_math(g, w_ref[...], m_ref[...], v_ref[...])
                go_ref[...] = g
                d_ref[...] = delta
                mo_ref[...] = m1
                vo_ref[...] = v1

    def g_spec(j):
        return pl.BlockSpec((n, tr, cols), lambda l, i: (0, jnp.where(l == j, i, jnp.where(l < j, 0, nt - 1)), 0))

    blk = pl.BlockSpec((tr, cols), lambda l, i: (l * nt + i, 0))
    out = pl.pallas_call(
        body, name=name, grid=(nl, nt),
        in_specs=[g_spec(j) for j in range(nl)] + [blk, blk, blk],
        out_specs=[blk] * 4, out_shape=[jax.ShapeDtypeStruct((nl * rows, cols), F32)] * 4,
        compiler_params=_params("arbitrary", "arbitrary"),
    )(*[g.reshape(n, rows, cols) for g in gs], w.reshape(nl * rows, cols), m.reshape(nl * rows, cols), v.reshape(nl * rows, cols))
    return [o.reshape(shape) for o in out]


def adam_lb(gstack, lb, m, v):
    n, nl = gstack.shape[0], lb.shape[0]

    def body(g_ref, w_ref, m_ref, v_ref, go_ref, d_ref, mo_ref, vo_ref):
        rows = [w_ref[i:i + 1, :] for i in range(nl)]
        ct = []
        for i in range(nl):
            g = g_ref[0, i:i + 1, :]
            for j in range(1, n):
                g = g + g_ref[j, i:i + 1, :]
            ct.append(g)
        _, vjp = jax.vjp(_lower_bounds, rows)
        (grads,) = vjp(ct)
        for i in range(nl):
            delta, m1, v1 = _adam_math(grads[i], rows[i], m_ref[i:i + 1, :], v_ref[i:i + 1, :])
            go_ref[i:i + 1, :] = grads[i]
            d_ref[i:i + 1, :] = delta
            mo_ref[i:i + 1, :] = m1
            vo_ref[i:i + 1, :] = v1

    return pl.pallas_call(body, name="adam_hgrn_lb", out_shape=[jax.ShapeDtypeStruct(lb.shape, F32)] * 4,
                          compiler_params=_params())(gstack, lb, m, v)


class _Ride:
    def __init__(self, items):
        self.items = list(items)
        n = len(self.items)
        self.srcs = [src for src, _ in self.items]
        self.in_specs = [pl.BlockSpec(memory_space=pl.ANY)] * n
        self.out_specs = [pl.BlockSpec(memory_space=pl.ANY)] * n
        self.out_shape = [jax.ShapeDtypeStruct(((N_DEV,) + s.shape) if mode == "gather" else s.shape, s.dtype)
                          for s, mode in self.items]
        self.scratch = [pltpu.SemaphoreType.DMA((n, N_DEV - 1)), pltpu.SemaphoreType.DMA((n, N_DEV - 1)),
                        pltpu.SemaphoreType.DMA((n,))] if n else []

    def split(self, refs, n_in, n_out):
        n = len(self.items)
        a, b = n_in + n, n_in + 2 * n + n_out
        return refs[:n_in], refs[n_in:a], refs[a:a + n_out], refs[a + n_out:b], refs[b:]

    def _plan(self, srcs, outs, sems):
        send_sems, recv_sems, local_sems = sems
        x, y, c = lax.axis_index("x"), lax.axis_index("y"), lax.axis_index("c")
        me = 4 * x + 2 * y + c

        def remote(src, dst, i, k, dev):
            return pltpu.make_async_remote_copy(src_ref=src, dst_ref=dst, send_sem=send_sems.at[i, k], recv_sem=recv_sems.at[i, k],
                                                device_id=dev, device_id_type=pl.DeviceIdType.MESH)

        first, relays, final, final_send = [], [], [], []
        for i, (_, mode) in enumerate(self.items):
            if mode == "a2a":
                copies = [pltpu.make_async_copy(srcs[i].at[me], outs[i].at[me], local_sems.at[i])]
                for p in range(1, N_DEV):
                    px = 1 - x if p & 4 else x
                    py = 1 - y if p & 2 else y
                    pc = 1 - c if p & 1 else c
                    copies.append(remote(srcs[i].at[4 * px + 2 * py + pc], outs[i].at[me], i, p - 1, (px, py, pc)))
                first += copies
                final += copies
                continue
            mine = outs[i].at[me]
            own = [pltpu.make_async_copy(srcs[i], mine, local_sems.at[i]), remote(srcs[i], mine, i, 0, (x, y, 1 - c))]
            first += own
            final += own
            for j, (px, py) in enumerate([(1 - x, y), (x, 1 - y), (1 - x, 1 - y)]):
                theirs = outs[i].at[4 * px + 2 * py + c]
                over_ici = remote(srcs[i], mine, i, 1 + j, (px, py, c))
                relay = remote(theirs, theirs, i, 4 + j, (x, y, 1 - c))
                first.append(over_ici)
                relays.append((over_ici, relay))
                final.append(relay)
                final_send.append(over_ici)
        return first, relays, final, final_send

    def run(self, srcs, outs, scratch, grid=()):
        if not self.items:
            return
        sems = scratch[len(scratch) - 3:]

        def start():
            for cp in self._plan(srcs, outs, sems)[0]:
                cp.start()

        def relay():
            for arrival, onward in self._plan(srcs, outs, sems)[1]:
                arrival.wait_recv()
                onward.start()

        def finish():
            _, _, final, final_send = self._plan(srcs, outs, sems)
            for cp in final_send:
                cp.wait_send()
            for cp in final:
                cp.wait()

        if not grid:
            start()
            relay()
            finish()
            return
        total = math.prod(grid)
        step = pl.program_id(0)
        for a in range(1, len(grid)):
            step = step * grid[a] + pl.program_id(a)
        pl.when(step == 0)(start)
        if any(mode == "gather" for _, mode in self.items):
            pl.when(step == min(total - 1, (3 * total) // 4))(relay)
        pl.when(step == total - 1)(finish)


def exchange(name, items):
    rd = _Ride(items)

    def body(*refs):
        _, srcs, _, outs, scratch = rd.split(refs, 0, 0)
        rd.run(srcs, outs, scratch)

    return pl.pallas_call(body, name=name, in_specs=rd.in_specs, out_specs=rd.out_specs, out_shape=rd.out_shape,
                          scratch_shapes=rd.scratch)(*rd.srcs)


def _from_gather(name, g):
    if name in COL_SHARDED:
        _, k, n = g.shape
        return g.transpose(1, 0, 2).reshape(k, N_DEV * n)
    return g.reshape(-1, g.shape[-1])


def _to_slabs(name, w):
    if isinstance(w, tuple):
        per = N_DEV // len(w)
        return jnp.concatenate([p.reshape(p.shape[0], per, p.shape[1] // per).transpose(1, 0, 2) for p in w], axis=0)
    k, n = w.shape
    if name in COL_SHARDED:
        return w.reshape(k, N_DEV, n // N_DEV).transpose(1, 0, 2)
    return w.reshape(N_DEV, k // N_DEV, n)


def _w_in_internal(w):
    return jnp.pad(w, ((0, 0), (0, LANES - QK_ROPE)))


def _qb_internal(w, inverse=False):
    h, n, r2 = MLA_HEADS, QK_NOPE, QK_ROPE // 2
    lead = w.shape[:-1]
    if not inverse:
        w = w.reshape(lead + (h, n + 2 * r2))
        parts = [w[..., :n], w[..., n:n + r2], w[..., n + r2:]]
        return jnp.concatenate([p.reshape(lead + (-1,)) for p in parts], axis=-1)
    parts = [w[..., :h * n].reshape(lead + (h, n)), w[..., h * n:h * (n + r2)].reshape(lead + (h, r2)),
             w[..., h * (n + r2):].reshape(lead + (h, r2))]
    return jnp.concatenate(parts, axis=-1).reshape(lead + (-1,))


def _kvb_internal(w, inverse=False):
    h, n, vd = MLA_HEADS, QK_NOPE, V_HEAD
    lead = w.shape[:-1]
    if not inverse:
        w = w.reshape(lead + (h, n + vd))
        return jnp.concatenate([w[..., :n].reshape(lead + (-1,)), w[..., n:].reshape(lead + (-1,))], axis=-1)
    parts = [w[..., :h * n].reshape(lead + (h, n)), w[..., h * n:].reshape(lead + (h, vd))]
    return jnp.concatenate(parts, axis=-1).reshape(lead + (-1,))


def _mla_forward(h, w, tabs, ride=()):
    cos, sin = tabs
    r2 = MLA_HEADS * (QK_ROPE // 2)
    proj = mm3("mla_proj", h, w['w_in'])
    qn, kvn, krt = rowwise(
        "mla_mid", lambda rv, ev, gv: (f_mla_mid(rv, ev, gv), []),
        [_view(proj, 0, Q_LORA), _view(proj, Q_LORA, KV_LORA), _view(proj, Q_LORA + KV_LORA, LANES), _full(cos), _full(sin)],
        [], [w['q_norm'], w['kv_norm']], [(BF16, [Q_LORA]), (BF16, [KV_LORA]), (BF16, [r2, r2])], ts=ROW_TILE)
    q = mm3("mla_q", qn, w['w_qb'])
    kv = mm3("mla_kv", kvn, w['w_kvb'])
    qh, kh, vh = mla_heads(q, kv, krt, cos, sin)
    o, got = attn_fwd(qh, kh, vh, ride)
    y = mm3("mla_out", o, w['w_o'])
    return y, dict(h=h, proj=proj, qn=qn, kvn=kvn, qh=qh, kh=kh, vh=vh, o=o), got


def _mla_backward(dy, sv, w, tabs, ride=()):
    cos, sin = tabs
    r2 = MLA_HEADS * (QK_ROPE // 2)
    g = {}
    g['w_o'] = wgrad("mla_out_dw", sv['o'], dy)
    do = mm3("mla_out_dx", dy, w['w_o'], tb=True)
    dqh, dkh, dvh, got = attn_bwd(sv['qh'], sv['kh'], sv['vh'], do, ride)
    dq, dkv, dkrt = mla_heads_bwd(dqh, dkh, dvh, cos, sin)
    g['w_qb'] = wgrad("mla_q_dw", sv['qn'], dq)
    g['w_kvb'] = wgrad("mla_kv_dw", sv['kvn'], dkv)
    dqn = mm3("mla_q_dx", dq, w['w_qb'], tb=True)
    dkvn = mm3("mla_kv_dx", dkv, w['w_kvb'], tb=True)
    proj = sv['proj']
    (dproj,), _, (g['q_norm'], g['kv_norm']) = rowwise_bwd(
        "mla_mid_bwd", f_mla_mid,
        [_view(proj, 0, Q_LORA), _view(proj, Q_LORA, KV_LORA), _view(proj, Q_LORA + KV_LORA, LANES), _full(cos), _full(sin)],
        [], [w['q_norm'], w['kv_norm']], [(dqn, [Q_LORA]), (dkvn, [KV_LORA]), (dkrt, [r2, r2])],
        [(BF16, [0, 1, 2])], ts=ROW_TILE, n_diff=3)
    g['w_in'] = wgrad("mla_proj_dw", sv['h'], dproj)
    dh = mm3("mla_proj_dx", dproj, w['w_in'], tb=True)
    return dh, g, got


def _hgrn_forward(h, w, ride=()):
    proj = mm3("hgrn_proj", h, w['w_in'])
    z, states, got = hgrn_fwd(proj, w['lb'], w['g_norm'], ride)
    y = mm3("hgrn_out", z, w['w_o'])
    return y, dict(h=h, proj=proj, states=states, z=z), got


def _hgrn_backward(dy, sv, w, ride=()):
    g = {}
    g['w_o'] = wgrad("hgrn_out_dw", sv['z'], dy)
    dz = mm3("hgrn_out_dx", dy, w['w_o'], tb=True)
    dparts, g['lb'], g['g_norm'], got = hgrn_bwd(sv['proj'], sv['states'], dz, w['lb'], w['g_norm'], ride)
    g['w_in'] = tuple(wgrad("hgrn_proj_dw", sv['h'], p) for p in dparts)
    dh = mm3("hgrn_proj_dx", dparts, w['w_in'], tb=True)
    return dh, g, got


def _ffn_forward(h, w, ride=()):
    (ug, uu, a), got = ffn_in_act(h, w['w_in'], ride)
    y = mm3("ffn_out", a, w['w_out'])
    return y, dict(h=h, ug=ug, uu=uu, a=a), got


def _ffn_backward(dy, sv, w):
    g = {}
    g['w_out'] = wgrad("ffn_out_dw", sv['a'], dy)
    dug, duu = ffn_out_dx_act(dy, w['w_out'], sv['ug'], sv['uu'])
    g['w_in'] = (wgrad("ffn_in_dw", sv['h'], dug), wgrad("ffn_in_dw", sv['h'], duu))
    dh = mm3("ffn_in_dx", [dug, duu], w['w_in'], tb=True)
    return dh, g


def kernel(x, c, positions, mla_w_in, mla_q_norm, mla_w_qb, mla_kv_norm, mla_w_kvb, mla_w_o, hgrn_lb, hgrn_w_in, hgrn_g_norm, hgrn_w_o, ffn_w_in, ffn_w_out, ada_w, ada_b, ln_g, ln_b, loss_target, m_mla_w_in, m_mla_q_norm, m_mla_w_qb, m_mla_kv_norm, m_mla_w_kvb, m_mla_w_o, m_hgrn_lb, m_hgrn_w_in, m_hgrn_g_norm, m_hgrn_w_o, m_ffn_w_in, m_ffn_w_out, m_ada_w, m_ada_b, m_ln_g, m_ln_b, v_mla_w_in, v_mla_q_norm, v_mla_w_qb, v_mla_kv_norm, v_mla_w_kvb, v_mla_w_o, v_hgrn_lb, v_hgrn_w_in, v_hgrn_g_norm, v_hgrn_w_o, v_ffn_w_in, v_ffn_w_out, v_ada_w, v_ada_b, v_ln_g, v_ln_b):
    W = dict(zip(WEIGHTS, (mla_w_in, mla_q_norm, mla_w_qb, mla_kv_norm, mla_w_kvb, mla_w_o, hgrn_lb, hgrn_w_in, hgrn_g_norm,
                           hgrn_w_o, ffn_w_in, ffn_w_out, ada_w, ada_b, ln_g, ln_b)))
    M1 = dict(zip(WEIGHTS, (m_mla_w_in, m_mla_q_norm, m_mla_w_qb, m_mla_kv_norm, m_mla_w_kvb, m_mla_w_o, m_hgrn_lb, m_hgrn_w_in,
                            m_hgrn_g_norm, m_hgrn_w_o, m_ffn_w_in, m_ffn_w_out, m_ada_w, m_ada_b, m_ln_g, m_ln_b)))
    M2 = dict(zip(WEIGHTS, (v_mla_w_in, v_mla_q_norm, v_mla_w_qb, v_mla_kv_norm, v_mla_w_kvb, v_mla_w_o, v_hgrn_lb, v_hgrn_w_in,
                            v_hgrn_g_norm, v_hgrn_w_o, v_ffn_w_in, v_ffn_w_out, v_ada_w, v_ada_b, v_ln_g, v_ln_b)))
    bsz, seq, d = x.shape
    depth, n_mla, n_hgrn = ffn_w_in.shape[0], mla_w_in.shape[0], hgrn_w_in.shape[0]
    n_sub = 2 * depth

    big = COL_SHARDED + ROW_SHARDED
    wb = {n: cast_bf16("cast_" + n, W[n]) for n in big}

    def mixer_names(layer):
        mixer = ['mla_w_in', 'mla_w_qb', 'mla_w_kvb', 'mla_w_o'] if layer % 2 == 0 else ['hgrn_w_in', 'hgrn_w_o']
        return [(n, layer // 2) for n in mixer]

    def carried(layer):
        return [('ffn_w_in', layer), ('ffn_w_out', layer)] + (mixer_names(layer + 1) if layer + 1 < depth else [])

    def carried_fwd(k):
        layer = k // 2
        if k % 2 == 0:
            return [('ffn_w_in', layer), ('ffn_w_out', layer)]
        return mixer_names(layer + 1) if layer + 1 < depth else []

    def weight_items(names):
        return [(wb[n][j], "gather") for n, j in names]

    G = {}
    internal = {'mla_w_in': _w_in_internal, 'mla_w_qb': _qb_internal, 'mla_w_kvb': _kvb_internal}

    def take_weights(names, got):
        for (n, j), a in zip(names, got):
            G[n, j] = internal.get(n, lambda w: w)(_from_gather(n, a))

    lower_shard = lower_bounds(hgrn_lb)
    got = exchange("gather_first", [(lower_shard, "gather"), (ln_g, "gather"), (ln_b, "gather"), (c, "gather")]
                   + weight_items(mixer_names(0)))
    lower_all = got[0].transpose(1, 0, 2).reshape(n_hgrn, -1)
    ln_g_all = got[1].transpose(1, 2, 0, 3).reshape(depth, 2, d)
    ln_b_all = got[2].transpose(1, 2, 0, 3).reshape(depth, 2, d)
    c_all = got[3].reshape(N_DEV * bsz, d)
    take_weights(mixer_names(0), got[4:])

    cols = ada_w.shape[-1]
    mod_loc = ada_fwd(c_all, ada_w, ada_b)
    (mod_got,) = exchange("scatter_mod", [(mod_loc.reshape(n_sub, N_DEV, bsz, cols).transpose(1, 0, 2, 3), "a2a")])
    mod = mod_got.transpose(1, 2, 0, 3).reshape(n_sub, bsz, 1, 3 * d)
    shift = [mod[k, :, :, 0:d] for k in range(n_sub)]
    scale = [mod[k, :, :, d:2 * d] for k in range(n_sub)]
    gate = [mod[k, :, :, 2 * d:] for k in range(n_sub)]
    lng = [ln_g_all[k // 2, k % 2][None, :] for k in range(n_sub)]
    lnb = [ln_b_all[k // 2, k % 2][None, :] for k in range(n_sub)]

    tabs = rope_tables(positions)

    def sub_weights(k):
        layer, j = k // 2, k // 4
        if k % 2:
            return 'ffn', layer, dict(w_in=G['ffn_w_in', layer], w_out=G['ffn_w_out', layer])
        if layer % 2 == 0:
            return 'mla', j, dict(w_in=G['mla_w_in', j], q_norm=mla_q_norm[j][None, :], w_qb=G['mla_w_qb', j],
                                  kv_norm=mla_kv_norm[j][None, :], w_kvb=G['mla_w_kvb', j], w_o=G['mla_w_o', j])
        return 'hgrn', j, dict(w_in=G['hgrn_w_in', j], lb=lower_all[j][None, :], g_norm=hgrn_g_norm[j][None, :],
                               w_o=G['hgrn_w_o', j])

    (h,) = rowwise("mod_first", lambda rv, ev, gv: (f_mod(rv, ev, gv), []), [_full(x)], [scale[0], shift[0]], [],
                   [(BF16, [d])], ts=ROW_TILE)
    xs, ys, saved = [x], [], []
    loss_acc = None
    for k in range(n_sub):
        kind, _, w = sub_weights(k)
        ride = weight_items(carried_fwd(k))
        if kind == 'ffn':
            y, sv, got = _ffn_forward(h, w, ride)
        elif kind == 'mla':
            y, sv, got = _mla_forward(h, w, tabs, ride)
        else:
            y, sv, got = _hgrn_forward(h, w, ride)
        take_weights(carried_fwd(k), got)
        ys.append(y)
        saved.append(sv)
        if k + 1 < n_sub:
            xn, h = rowwise("ln_mod", lambda rv, ev, gv: (f_ln_mod(rv, ev, gv), []), [_full(xs[k]), _full(y)],
                            [gate[k], scale[k + 1], shift[k + 1]], [lng[k], lnb[k]], [(F32, [d]), (BF16, [d])], ts=ROW_TILE)
            xs.append(xn)
        else:
            def loss_rows(rv, ev, gv):
                (row,) = f_ln_loss(rv, ev, gv)
                return [], [jnp.broadcast_to(jnp.sum(row, keepdims=True), (1, LANES))]
            (loss_acc,) = rowwise("ln_loss", loss_rows, [_full(xs[k]), _full(y), _full(loss_target)], [gate[k]], [lng[k], lnb[k]],
                                  [], ts=ROW_TILE, accs=[LANES])
    loss = lax.psum(loss_acc[0, 0], ("x", "y", "c"))

    d_shift, d_scale, d_gate = [None] * n_sub, [None] * n_sub, [None] * n_sub
    d_lng, d_lnb = [None] * n_sub, [None] * n_sub
    part = {n: [None] * W[n].shape[0] for n in ['mla_q_norm', 'mla_kv_norm', 'hgrn_g_norm']}
    recv = {n: [None] * W[n].shape[0] for n in big}
    d_lower = [None] * n_hgrn
    k = n_sub - 1
    (dx, dy), (d_gate[k],), (d_lng[k], d_lnb[k]) = rowwise_bwd(
        "ln_loss_bwd", f_ln_loss, [_full(xs[k]), _full(ys[k]), _full(loss_target)], [gate[k]], [lng[k], lnb[k]], [],
        [(F32, [0]), (BF16, [1])], ts=ROW_TILE, n_diff=2, unit_ct=1)
    grad_x = None
    mine = {}

    def take_grads(names, got):
        for (n, j), a in zip(names, got):
            recv[n][j] = a

    def grad_items(names):
        return [(_to_slabs(n, mine[n, jj]), "a2a") for n, jj in names]

    for k in range(n_sub - 1, -1, -1):
        kind, j, w = sub_weights(k)
        ride = grad_items(carried(k // 2)) if kind != 'ffn' else []
        if kind == 'ffn':
            dh, g = _ffn_backward(dy, saved[k], w)
            new = {('ffn_w_in', j): g['w_in'], ('ffn_w_out', j): g['w_out']}
        elif kind == 'mla':
            dh, g, got = _mla_backward(dy, saved[k], w, tabs, ride)
            new = {('mla_w_in', j): g['w_in'][:, :mla_w_in.shape[-1] * N_DEV], ('mla_w_qb', j): _qb_internal(g['w_qb'], inverse=True),
                   ('mla_w_kvb', j): _kvb_internal(g['w_kvb'], inverse=True), ('mla_w_o', j): g['w_o']}
            part['mla_q_norm'][j], part['mla_kv_norm'][j] = g['q_norm'][0], g['kv_norm'][0]
        else:
            dh, g, got = _hgrn_backward(dy, saved[k], w, ride)
            new = {('hgrn_w_in', j): g['w_in'], ('hgrn_w_o', j): g['w_o']}
            part['hgrn_g_norm'][j] = g['g_norm'][0]
            d_lower[j] = g['lb'][0]
        if kind != 'ffn':
            take_grads(carried(k // 2), got)
        mine.update(new)
        if k:
            (dx, dy), (d_gate[k - 1], d_scale[k], d_shift[k]), (d_lng[k - 1], d_lnb[k - 1]) = rowwise_bwd(
                "ln_mod_bwd", f_ln_mod, [_full(xs[k - 1]), _full(ys[k - 1])], [gate[k - 1], scale[k], shift[k]],
                [lng[k - 1], lnb[k - 1]], [(dx, [d]), (dh, [d])], [(F32, [0]), (BF16, [1])], ts=ROW_TILE, n_diff=2)
        else:
            (grad_x,), (d_scale[0], d_shift[0]), _ = rowwise_bwd(
                "mod_first_bwd", f_mod_with_x, [_full(x)], [scale[0], shift[0]], [], [(dh, [d]), (dx, [d])],
                [(F32, [0])], ts=ROW_TILE, n_diff=1)

    waiting = mixer_names(0)
    slabs = grad_items(waiting)
    slabs.append((jnp.stack(d_lower).reshape(n_hgrn, N_DEV, -1).transpose(1, 0, 2), "a2a"))
    for parts in (d_lng, d_lnb):
        full = jnp.stack([p[0] for p in parts]).reshape(depth, 2, N_DEV, d // N_DEV)
        slabs.append((full.transpose(2, 0, 1, 3), "a2a"))
    dmod = jnp.concatenate([jnp.stack(d_shift), jnp.stack(d_scale), jnp.stack(d_gate)], axis=-1)
    slabs.append((dmod.reshape(n_sub, bsz, N_DEV, cols).transpose(2, 0, 1, 3), "a2a"))
    small = ['mla_q_norm', 'mla_kv_norm', 'hgrn_g_norm']
    slabs += [(jnp.stack(part[n]), "gather") for n in small]
    got = exchange("scatter_last", slabs)
    take_grads(waiting, got)
    stacks = dict(zip(['hgrn_lb', 'ln_g', 'ln_b', 'dmod'] + small, got[len(waiting):]))

    dmod_all = stacks['dmod'].transpose(1, 0, 2, 3).reshape(n_sub, N_DEV * bsz, cols)
    g_ada_w, g_ada_b = ada_bwd(c_all, dmod_all)
    stacks['ada_w'] = g_ada_w.reshape((1,) + ada_w.shape)
    stacks['ada_b'] = g_ada_b.reshape((1,) + ada_b.shape)

    res = {}
    for n in WEIGHTS:
        if n == 'hgrn_lb':
            res[n] = adam_lb(stacks[n], W[n], M1[n], M2[n])
        elif n in big:
            res[n] = adam_layers("adam_" + n, recv[n], W[n], M1[n], M2[n])
        else:
            res[n] = adam("adam_" + n, stacks[n], W[n], M1[n], M2[n])
    return (loss, grad_x, *[res[n][0] for n in WEIGHTS], *[res[n][1] for n in WEIGHTS], *[res[n][2] for n in WEIGHTS],
            *[res[n][3] for n in WEIGHTS])
```

```python
import functools
import math

import numpy as np
import jax
import jax.numpy as jnp
from jax import lax
from jax.experimental import pallas as pl
from jax.experimental.pallas import tpu as pltpu

F32 = jnp.float32
BF16 = jnp.bfloat16

N_DEV = 8
LANES = 128
VMEM_LIMIT = 52 * 1024 * 1024

D_MODEL = 1024
DEPTH = 4
MLA_HEADS = 16
QK_NOPE = 64
QK_ROPE = 32
V_HEAD = 64
Q_LORA = 768
KV_LORA = 256
ROPE_THETA = 10000.0
HGRN_EXPAND = 128
HGRN_CHUNK = 128
HGRN_HEADS_PER_STEP = 8
D_FF = 2816
ALPHA = (2.0 * DEPTH) ** 0.25
LN_EPS = 1e-5
RMS_EPS = 1e-6
ADAM_LR = 0.001
ADAM_B1 = 0.9
ADAM_B2 = 0.999
ADAM_EPS = 1e-08
ADAM_WD = 0.01
ADAM_STEP = 10

ATTN_TQ = 512
ROW_TILE = 256

WEIGHTS = ['mla_w_in', 'mla_q_norm', 'mla_w_qb', 'mla_kv_norm', 'mla_w_kvb', 'mla_w_o', 'hgrn_lb', 'hgrn_w_in',
           'hgrn_g_norm', 'hgrn_w_o', 'ffn_w_in', 'ffn_w_out', 'ada_w', 'ada_b', 'ln_g', 'ln_b']
COL_SHARDED = ['mla_w_in', 'mla_w_qb', 'mla_w_kvb', 'hgrn_w_in', 'ffn_w_in']
ROW_SHARDED = ['mla_w_o', 'hgrn_w_o', 'ffn_w_out']


def _params(*sem):
    if sem:
        return pltpu.CompilerParams(dimension_semantics=sem, vmem_limit_bytes=VMEM_LIMIT)
    return pltpu.CompilerParams(vmem_limit_bytes=VMEM_LIMIT)


def _pick(n, cap):
    best = None
    for t in range(LANES, min(n, cap) + 1, LANES):
        if n % t == 0:
            best = t
    return best or n


def _pick_rows(n, cap):
    best = None
    for t in range(8, min(n, cap) + 1, 8):
        if n % t == 0:
            best = t
    return best or n


def matmul(name, a, b, *, ta=False, tb=False, out_dtype=F32, tm_cap=1024, tn_cap=1536, tk_cap=2048):
    parts = list(a) if isinstance(a, (list, tuple)) else [a]
    n_parts = len(parts)
    assert n_parts == 1 or not ta
    (kp, m) = parts[0].shape if ta else parts[0].shape[::-1]
    (n, k2) = b.shape if tb else b.shape[::-1]
    assert kp * n_parts == k2, (name, parts[0].shape, b.shape)
    tm, tn, tk = _pick(m, tm_cap), _pick(n, tn_cap), _pick(kp, tk_cap)
    nkp = kp // tk
    nk = nkp * n_parts
    dims = (((0 if ta else 1,), (1 if tb else 0,)), ((), ()))

    def prod(a_ref, b_ref):
        return lax.dot_general(a_ref[...].astype(BF16), b_ref[...].astype(BF16), dims, preferred_element_type=F32)

    if nk == 1:
        def body(a_ref, b_ref, o_ref):
            o_ref[...] = prod(a_ref, b_ref).astype(o_ref.dtype)
        scratch = []
    else:
        def body(*refs):
            a_refs, (b_ref, o_ref, acc_ref) = refs[:n_parts], refs[n_parts:]
            k = pl.program_id(2)

            @pl.when(k == 0)
            def _():
                acc_ref[...] = jnp.zeros_like(acc_ref)

            if n_parts == 1:
                acc_ref[...] += prod(a_refs[0], b_ref)
            else:
                for p in range(n_parts):
                    @pl.when((k >= p * nkp) & (k < (p + 1) * nkp))
                    def _(p=p):
                        acc_ref[...] += prod(a_refs[p], b_ref)

            @pl.when(k == nk - 1)
            def _():
                o_ref[...] = acc_ref[...].astype(o_ref.dtype)
        scratch = [pltpu.VMEM((tm, tn), F32)]

    if ta:
        a_specs = [pl.BlockSpec((tk, tm), lambda i, j, k: (k, i))]
    elif n_parts == 1:
        a_specs = [pl.BlockSpec((tm, tk), lambda i, j, k: (i, k))]
    else:
        a_specs = [pl.BlockSpec((tm, tk), lambda i, j, k, p=p: (i, jnp.clip(k - p * nkp, 0, nkp - 1))) for p in range(n_parts)]
    b_spec = pl.BlockSpec((tn, tk), lambda i, j, k: (j, k)) if tb else pl.BlockSpec((tk, tn), lambda i, j, k: (k, j))
    return pl.pallas_call(
        body, name=name, grid=(m // tm, n // tn, nk),
        in_specs=a_specs + [b_spec], out_specs=pl.BlockSpec((tm, tn), lambda i, j, k: (i, j)),
        out_shape=jax.ShapeDtypeStruct((m, n), out_dtype), scratch_shapes=scratch,
        compiler_params=_params("parallel", "parallel", "arbitrary"),
    )(*parts, b)


def mm3(name, a3, w, **kw):
    parts = list(a3) if isinstance(a3, (list, tuple)) else [a3]
    bsz, s = parts[0].shape[:2]
    flat = [p.reshape(bsz * s, p.shape[-1]) for p in parts]
    out = matmul(name, flat if len(flat) > 1 else flat[0], w, **kw)
    return out.reshape(bsz, s, out.shape[-1])


def ffn_in_act(h3, w_in, ride=()):
    bsz, s, k = h3.shape
    m, dff = bsz * s, w_in.shape[1] // 2
    tm, tn = _pick(m, 512), _pick(dff, 1536)
    nj = dff // tn
    grid = (nj, m // tm)
    rd = _Ride(ride)

    def body(*refs):
        (h_ref, wg_ref, wu_ref), srcs, (ug_ref, uu_ref, a_ref), outs, scratch = rd.split(refs, 3, 3)
        rd.run(srcs, outs, scratch, grid)
        hv = h_ref[...].astype(BF16)
        ug = _dg(hv, wg_ref[...].astype(BF16), 1, 0)
        uu = _dg(hv, wu_ref[...].astype(BF16), 1, 0)
        ug_ref[...] = ug.astype(ug_ref.dtype)
        uu_ref[...] = uu.astype(uu_ref.dtype)
        a_ref[...] = (jax.nn.silu(ug) * uu).astype(a_ref.dtype)

    out = pl.BlockSpec((tm, tn), lambda j, i: (i, j))
    res = pl.pallas_call(
        body, name="ffn_in_act", grid=grid,
        in_specs=[pl.BlockSpec((tm, k), lambda j, i: (i, 0)), pl.BlockSpec((k, tn), lambda j, i: (0, j)),
                  pl.BlockSpec((k, tn), lambda j, i: (0, nj + j))] + rd.in_specs,
        out_specs=[out, out, out] + rd.out_specs,
        out_shape=[jax.ShapeDtypeStruct((m, dff), BF16)] * 3 + rd.out_shape, scratch_shapes=rd.scratch,
        compiler_params=_params("arbitrary", "arbitrary"),
    )(h3.reshape(m, k), w_in, w_in, *rd.srcs)
    return [r.reshape(bsz, s, dff) for r in res[:3]], res[3:]


def ffn_out_dx_act(dy3, w_out, ug, uu):
    bsz, s, d = dy3.shape
    m, dff = bsz * s, w_out.shape[0]
    tm, tn = _pick(m, 512), _pick(dff, 1536)

    def body(dy_ref, w_ref, ug_ref, uu_ref, dg_ref, du_ref):
        da = _dg(dy_ref[...].astype(BF16), w_ref[...].astype(BF16), 1, 1)
        _, vjp = jax.vjp(lambda gate, up: jax.nn.silu(gate) * up, ug_ref[...].astype(F32), uu_ref[...].astype(F32))
        dg, du = vjp(da)
        dg_ref[...] = dg.astype(dg_ref.dtype)
        du_ref[...] = du.astype(du_ref.dtype)

    blk = pl.BlockSpec((tm, tn), lambda j, i: (i, j))
    res = pl.pallas_call(
        body, name="ffn_out_dx_act", grid=(dff // tn, m // tm),
        in_specs=[pl.BlockSpec((tm, d), lambda j, i: (i, 0)), pl.BlockSpec((tn, d), lambda j, i: (j, 0)), blk, blk],
        out_specs=[blk, blk], out_shape=[jax.ShapeDtypeStruct((m, dff), BF16)] * 2,
        compiler_params=_params("arbitrary", "arbitrary"),
    )(dy3.reshape(m, d), w_out, ug.reshape(m, dff), uu.reshape(m, dff))
    return [r.reshape(bsz, s, dff) for r in res]


def wgrad(name, a3, g3):
    bsz, s, k = a3.shape
    return matmul(name, a3.reshape(bsz * s, k), g3.reshape(bsz * s, g3.shape[-1]), ta=True, out_dtype=BF16)


def _dg(a, b, ca, cb, **kw):
    return lax.dot_general(a, b, (((ca,), (cb,)), ((), ())), preferred_element_type=F32, **kw)


@functools.partial(jax.custom_vjp, nondiff_argnums=(2, 3))
def bdot(a, b, ca, cb):
    return _dg(a.astype(BF16), b.astype(BF16), ca, cb)


def _bdot_fwd(a, b, ca, cb):
    return bdot(a, b, ca, cb), (a, b)


def _bdot_bwd(ca, cb, res, g):
    a, b = res
    a16, b16, g16 = a.astype(BF16), b.astype(BF16), g.astype(BF16)
    if ca == 1:
        da = _dg(g16, b16, 1, 1 if cb == 0 else 0)
    else:
        da = _dg(b16, g16, 1 if cb == 0 else 0, 1)
    if cb == 0:
        db = _dg(a16, g16, 0 if ca == 1 else 1, 0)
    else:
        db = _dg(g16, a16, 0, 0 if ca == 1 else 1)
    return da, db


bdot.defvjp(_bdot_fwd, _bdot_bwd)


def hdot(a, b, ca=1, cb=0):
    return _dg(a, b, ca, cb, precision=lax.Precision.HIGHEST)


def _row_specs(rows, exs, globs, ts):
    specs = [pl.BlockSpec((1, ts, w), lambda b, s, j=j: (b, s, j)) for (_, j, w) in rows]
    specs += [pl.BlockSpec((1, 1, e.shape[-1]), lambda b, s: (b, 0, 0)) for e in exs]
    specs += [pl.BlockSpec((1, g.shape[-1]), lambda b, s: (0, 0)) for g in globs]
    return specs


def _store_pieces(o_ref, pieces, widths):
    off = 0
    for p, w in zip(pieces, widths):
        o_ref[0, :, off:off + w] = p.astype(o_ref.dtype)
        off += w


def _load_pieces(c_ref, widths):
    out, off = [], 0
    for w in widths:
        out.append(c_ref[0, :, off:off + w].astype(F32))
        off += w
    return out


def rowwise(name, f, rows, exs, globs, outs, *, ts, accs=()):
    bsz, s = rows[0][0].shape[:2]
    ts = min(ts, s)
    n_r, n_e, n_g, n_o = len(rows), len(exs), len(globs), len(outs)

    def body(*refs):
        rv = [r[0].astype(F32) for r in refs[:n_r]]
        ev = [e[0] for e in refs[n_r:n_r + n_e]]
        gv = [g[...] for g in refs[n_r + n_e:n_r + n_e + n_g]]
        o_refs = refs[n_r + n_e + n_g:n_r + n_e + n_g + n_o]
        a_refs = refs[n_r + n_e + n_g + n_o:]
        pieces, sums = f(rv, ev, gv)
        idx = 0
        for o_ref, (_, ws) in zip(o_refs, outs):
            _store_pieces(o_ref, pieces[idx:idx + len(ws)], ws)
            idx += len(ws)
        if accs:
            @pl.when((pl.program_id(0) == 0) & (pl.program_id(1) == 0))
            def _():
                for a_ref in a_refs:
                    a_ref[...] = jnp.zeros_like(a_ref)
            for a_ref, val in zip(a_refs, sums):
                a_ref[...] += val

    out_specs = [pl.BlockSpec((1, ts, sum(ws)), lambda b, s: (b, s, 0)) for (_, ws) in outs]
    out_specs += [pl.BlockSpec((1, w), lambda b, s: (0, 0)) for w in accs]
    out_shape = [jax.ShapeDtypeStruct((bsz, s, sum(ws)), dt) for (dt, ws) in outs]
    out_shape += [jax.ShapeDtypeStruct((1, w), F32) for w in accs]
    return pl.pallas_call(
        body, name=name, grid=(bsz, s // ts),
        in_specs=_row_specs(rows, exs, globs, ts), out_specs=out_specs, out_shape=out_shape,
        compiler_params=_params("arbitrary", "arbitrary"),
    )(*[r[0] for r in rows], *exs, *globs)


def rowwise_bwd(name, f, rows, exs, globs, cts, d_groups, *, ts, n_diff, unit_ct=0):
    bsz, s = rows[0][0].shape[:2]
    ts = min(ts, s)
    n_r, n_e, n_g, n_c = len(rows), len(exs), len(globs), len(cts)
    n_d = len(d_groups)

    def body(*refs):
        rv = [r[0].astype(F32) for r in refs[:n_r]]
        ev = [e[0] for e in refs[n_r:n_r + n_e]]
        gv = [g[...] for g in refs[n_r + n_e:n_r + n_e + n_g]]
        base = n_r + n_e + n_g
        c_refs = refs[base:base + n_c]
        d_refs = refs[base + n_c:base + n_c + n_d]
        de_refs = refs[base + n_c + n_d:base + n_c + n_d + n_e]
        dg_refs = refs[base + n_c + n_d + n_e:]
        fixed = rv[n_diff:]
        out, vjp = jax.vjp(lambda r, e, g: f(r + fixed, e, g), rv[:n_diff], ev, gv)
        ct = []
        for c_ref, (_, ws) in zip(c_refs, cts):
            ct += _load_pieces(c_ref, ws)
        ct += [jnp.ones_like(o) for o in out[len(ct):]]
        assert len(ct) == len(out) and len(out) - unit_ct == sum(len(ws) for _, ws in cts), name
        d_r, d_e, d_g = vjp(ct)
        for d_ref, (_, idxs) in zip(d_refs, d_groups):
            _store_pieces(d_ref, [d_r[i] for i in idxs], [rows[i][2] for i in idxs])
        first_s = pl.program_id(1) == 0
        if n_e:
            @pl.when(first_s)
            def _():
                for r in de_refs:
                    r[...] = jnp.zeros_like(r)
            for r, val in zip(de_refs, d_e):
                r[0] += val
        if n_g:
            @pl.when(first_s & (pl.program_id(0) == 0))
            def _():
                for r in dg_refs:
                    r[...] = jnp.zeros_like(r)
            for r, val in zip(dg_refs, d_g):
                r[...] += val

    in_specs = _row_specs(rows, exs, globs, ts)
    in_specs += [pl.BlockSpec((1, ts, sum(ws)), lambda b, s: (b, s, 0)) for (_, ws) in cts]
    out_specs = [pl.BlockSpec((1, ts, sum(rows[i][2] for i in idxs)), lambda b, s: (b, s, 0)) for (_, idxs) in d_groups]
    out_specs += [pl.BlockSpec((1, 1, e.shape[-1]), lambda b, s: (b, 0, 0)) for e in exs]
    out_specs += [pl.BlockSpec((1, g.shape[-1]), lambda b, s: (0, 0)) for g in globs]
    out_shape = [jax.ShapeDtypeStruct((bsz, s, sum(rows[i][2] for i in idxs)), dt) for (dt, idxs) in d_groups]
    out_shape += [jax.ShapeDtypeStruct(e.shape, F32) for e in exs]
    out_shape += [jax.ShapeDtypeStruct(g.shape, F32) for g in globs]
    res = pl.pallas_call(
        body, name=name, grid=(bsz, s // ts),
        in_specs=in_specs, out_specs=out_specs, out_shape=out_shape,
        compiler_params=_params("arbitrary", "arbitrary"),
    )(*[r[0] for r in rows], *exs, *globs, *[c[0] for c in cts])
    return res[:n_d], res[n_d:n_d + n_e], res[n_d + n_e:]


def _full(a):
    return (a, 0, a.shape[-1])


def _view(a, col, w):
    assert col % w == 0
    return (a, col // w, w)


def _layer_norm(z, g, b):
    mu = jnp.mean(z, -1, keepdims=True)
    var = jnp.mean(jnp.square(z - mu), -1, keepdims=True)
    return (z - mu) * lax.rsqrt(var + LN_EPS) * g + b


def _rms_norm(z, g):
    ms = jnp.mean(jnp.square(z), -1, keepdims=True)
    return z * lax.rsqrt(ms + RMS_EPS) * g


def f_mod(rv, ev, gv):
    (x,), (scale, shift) = rv, ev
    return [x * (1.0 + scale) + shift]


def f_mod_with_x(rv, ev, gv):
    return f_mod(rv, ev, gv) + [rv[0]]


def f_ln_mod(rv, ev, gv):
    (x, y), (gate, scale, shift), (g, b) = rv, ev, gv
    xn = _layer_norm(ALPHA * x + (1.0 + gate) * y, g, b)
    return [xn, xn * (1.0 + scale) + shift]


def f_ln_loss(rv, ev, gv):
    (x, y, target), (gate,), (g, b) = rv, ev, gv
    xn = _layer_norm(ALPHA * x + (1.0 + gate) * y, g, b)
    return [0.5 * jnp.mean(jnp.square(xn - target), -1, keepdims=True)]


def _head_spread(width):
    r2 = QK_ROPE // 2
    j = lax.broadcasted_iota(jnp.int32, (LANES, width), 0)
    col = lax.broadcasted_iota(jnp.int32, (LANES, width), 1) % r2
    return (j == col).astype(F32), (j == col + r2).astype(F32)


def f_mla_mid(rv, ev, gv):
    (q_lat, kv_lat, kr, cos, sin), (q_g, kv_g) = rv, gv
    e1, e2 = _head_spread(cos.shape[-1])
    k1, k2 = hdot(kr, e1), hdot(kr, e2)
    return [_rms_norm(q_lat, q_g), _rms_norm(kv_lat, kv_g), k1 * cos - k2 * sin, k1 * sin + k2 * cos]


def rope_tables(positions):
    bsz, s = positions.shape
    r2 = QK_ROPE // 2
    width = MLA_HEADS * r2
    inv = (ROPE_THETA ** (-np.arange(0, QK_ROPE, 2, dtype=np.float32) / QK_ROPE)).astype(np.float32)
    inv = jnp.asarray(np.tile(inv, MLA_HEADS)[None, :])
    ts = min(ROW_TILE, s)

    def body(p_ref, inv_ref, cos_ref, sin_ref):
        ang = p_ref[0].astype(F32) * inv_ref[...]
        cos_ref[0] = jnp.cos(ang)
        sin_ref[0] = jnp.sin(ang)

    spec = pl.BlockSpec((1, ts, width), lambda b, s: (b, s, 0))
    return pl.pallas_call(
        body, name="rope_tables", grid=(bsz, s // ts),
        in_specs=[pl.BlockSpec((1, ts, 1), lambda b, s: (b, s, 0)), pl.BlockSpec((1, width), lambda b, s: (0, 0))],
        out_specs=[spec, spec], out_shape=[jax.ShapeDtypeStruct((bsz, s, width), F32)] * 2,
        compiler_params=_params("arbitrary", "arbitrary"),
    )(positions[:, :, None], inv)


def _attn_probs(q, k, row0):
    scale = (QK_NOPE + QK_ROPE) ** -0.5
    s = _dg(q, k, 1, 1) * scale
    rows = row0 + lax.broadcasted_iota(jnp.int32, s.shape, 0)
    cols = lax.broadcasted_iota(jnp.int32, s.shape, 1)
    s = jnp.where(cols <= rows, s, jnp.finfo(F32).min)
    e = jnp.exp(s - jnp.max(s, -1, keepdims=True))
    return e / jnp.sum(e, -1, keepdims=True), scale


ATTN_PAIR = 2


def attn_fwd(q, k, v, ride=()):
    bsz, h, s, dq = q.shape
    dv = v.shape[-1]
    tq = min(ATTN_TQ, s)
    grid = (bsz, h // ATTN_PAIR, s // tq)
    rd = _Ride(ride)

    def body(*refs):
        (q_ref, k_ref, v_ref), srcs, (o_ref,), outs, sems = rd.split(refs, 3, 1)
        rd.run(srcs, outs, sems, grid)
        for i in range(grid[2]):
            @pl.when(pl.program_id(2) == i)
            def _(i=i):
                kend = (i + 1) * tq
                for e in range(ATTN_PAIR):
                    p, _ = _attn_probs(q_ref[0, e], k_ref[0, e, :kend, :], i * tq)
                    o_ref[0, :, e * dv:(e + 1) * dv] = _dg(p.astype(BF16), v_ref[0, e, :kend, :], 1, 0).astype(o_ref.dtype)

    res = pl.pallas_call(
        body, name="attn_fwd", grid=grid,
        in_specs=[pl.BlockSpec((1, ATTN_PAIR, tq, dq), lambda b, h, i: (b, h, i, 0)),
                  pl.BlockSpec((1, ATTN_PAIR, s, dq), lambda b, h, i: (b, h, 0, 0)),
                  pl.BlockSpec((1, ATTN_PAIR, s, dv), lambda b, h, i: (b, h, 0, 0))] + rd.in_specs,
        out_specs=[pl.BlockSpec((1, tq, ATTN_PAIR * dv), lambda b, h, i: (b, i, h))] + rd.out_specs,
        out_shape=[jax.ShapeDtypeStruct((bsz, s, h * dv), BF16)] + rd.out_shape, scratch_shapes=rd.scratch,
        compiler_params=_params("arbitrary", "arbitrary", "arbitrary"),
    )(q, k, v, *rd.srcs)
    return res[0], res[1:]


def attn_bwd(q, k, v, do, ride=()):
    bsz, h, s, dq = q.shape
    dv = v.shape[-1]
    tq = min(ATTN_TQ, s)
    grid = (bsz, h // ATTN_PAIR, s // tq)
    rd = _Ride(ride)

    def body(*refs):
        (q_ref, k_ref, v_ref, do_ref), srcs, (dq_ref, dk_ref, dv_ref), outs, sems = rd.split(refs, 4, 3)
        rd.run(srcs, outs, sems, grid)

        @pl.when(pl.program_id(2) == 0)
        def _():
            dk_ref[...] = jnp.zeros_like(dk_ref)
            dv_ref[...] = jnp.zeros_like(dv_ref)

        for i in range(grid[2]):
            @pl.when(pl.program_id(2) == i)
            def _(i=i):
                kend = (i + 1) * tq
                for e in range(ATTN_PAIR):
                    qv, kv, vv = q_ref[0, e], k_ref[0, e, :kend, :], v_ref[0, e, :kend, :]
                    p, scale = _attn_probs(qv, kv, i * tq)
                    do16 = do_ref[0, :, e * dv:(e + 1) * dv].astype(BF16)
                    dv_ref[0, e, :kend, :] += _dg(p.astype(BF16), do16, 0, 0)
                    dp = _dg(do16, vv, 1, 1)
                    ds = (p * (dp - jnp.sum(dp * p, -1, keepdims=True)) * scale).astype(BF16)
                    dq_ref[0, e] = _dg(ds, kv, 1, 0)
                    dk_ref[0, e, :kend, :] += _dg(ds, qv, 0, 0)

    res = pl.pallas_call(
        body, name="attn_bwd", grid=grid,
        in_specs=[pl.BlockSpec((1, ATTN_PAIR, tq, dq), lambda b, h, i: (b, h, i, 0)),
                  pl.BlockSpec((1, ATTN_PAIR, s, dq), lambda b, h, i: (b, h, 0, 0)),
                  pl.BlockSpec((1, ATTN_PAIR, s, dv), lambda b, h, i: (b, h, 0, 0)),
                  pl.BlockSpec((1, tq, ATTN_PAIR * dv), lambda b, h, i: (b, i, h))] + rd.in_specs,
        out_specs=[pl.BlockSpec((1, ATTN_PAIR, tq, dq), lambda b, h, i: (b, h, i, 0)),
                   pl.BlockSpec((1, ATTN_PAIR, s, dq), lambda b, h, i: (b, h, 0, 0)),
                   pl.BlockSpec((1, ATTN_PAIR, s, dv), lambda b, h, i: (b, h, 0, 0))] + rd.out_specs,
        out_shape=[jax.ShapeDtypeStruct((bsz, h, s, dq), F32), jax.ShapeDtypeStruct((bsz, h, s, dq), F32),
                   jax.ShapeDtypeStruct((bsz, h, s, dv), F32)] + rd.out_shape, scratch_shapes=rd.scratch,
        compiler_params=_params("arbitrary", "arbitrary", "arbitrary"),
    )(q, k, v, do, *rd.srcs)
    return res[0], res[1], res[2], res[3:]


def mla_heads(q, kv, krt, cos, sin):
    bsz, s, _ = q.shape
    nh, n, r2, vd = MLA_HEADS, QK_NOPE, QK_ROPE // 2, V_HEAD
    ts = min(ROW_TILE, s)

    def body(q_ref, kv_ref, kr_ref, cos_ref, sin_ref, qh_ref, kh_ref, vh_ref):
        cos, sin = cos_ref[0], sin_ref[0]
        qv, kvv, kr = q_ref[0], kv_ref[0], kr_ref[0].astype(F32)
        q1, q2 = qv[:, nh * n:nh * (n + r2)], qv[:, nh * (n + r2):]
        qr = [q1 * cos - q2 * sin, q1 * sin + q2 * cos]
        for h in range(nh):
            qh_ref[0, h, :, :n] = qv[:, h * n:(h + 1) * n].astype(BF16)
            kh_ref[0, h, :, :n] = kvv[:, h * n:(h + 1) * n].astype(BF16)
            vh_ref[0, h] = kvv[:, nh * n + h * vd:nh * n + (h + 1) * vd].astype(BF16)
            for j in range(2):
                qh_ref[0, h, :, n + j * r2:n + (j + 1) * r2] = qr[j][:, h * r2:(h + 1) * r2].astype(BF16)
                kh_ref[0, h, :, n + j * r2:n + (j + 1) * r2] = kr[:, (j * nh + h) * r2:(j * nh + h + 1) * r2].astype(BF16)

    def row(a):
        return pl.BlockSpec((1, ts, a.shape[-1]), lambda b, i: (b, i, 0))

    def heads(w):
        return pl.BlockSpec((1, nh, ts, w), lambda b, i: (b, 0, i, 0))

    return pl.pallas_call(
        body, name="mla_heads", grid=(bsz, s // ts), in_specs=[row(q), row(kv), row(krt), row(cos), row(sin)],
        out_specs=[heads(n + 2 * r2), heads(n + 2 * r2), heads(vd)],
        out_shape=[jax.ShapeDtypeStruct((bsz, nh, s, n + 2 * r2), BF16)] * 2 + [jax.ShapeDtypeStruct((bsz, nh, s, vd), BF16)],
        compiler_params=_params("arbitrary", "arbitrary"),
    )(q, kv, krt, cos, sin)


def mla_heads_bwd(dqh, dkh, dvh, cos, sin):
    bsz, nh, s, _ = dqh.shape
    n, r2, vd = QK_NOPE, QK_ROPE // 2, V_HEAD
    ts = min(ROW_TILE, s)

    def body(dq_ref, dk_ref, dv_ref, cos_ref, sin_ref, oq_ref, okv_ref, okr_ref, tq_acc, tkv_acc, rot):
        for h in range(nh):
            tq_acc[:, h * n:(h + 1) * n] = dq_ref[0, h, :, :n]
            tkv_acc[:, h * n:(h + 1) * n] = dk_ref[0, h, :, :n]
            tkv_acc[:, nh * n + h * vd:nh * n + (h + 1) * vd] = dv_ref[0, h]
            for j in range(2):
                rot[j, :, h * r2:(h + 1) * r2] = dq_ref[0, h, :, n + j * r2:n + (j + 1) * r2]
                okr_ref[0, :, (j * nh + h) * r2:(j * nh + h + 1) * r2] = dk_ref[0, h, :, n + j * r2:n + (j + 1) * r2]
        cos, sin = cos_ref[0], sin_ref[0]
        d1, d2 = rot[0], rot[1]
        oq_ref[0, :, :nh * n] = tq_acc[...].astype(BF16)
        oq_ref[0, :, nh * n:nh * (n + r2)] = (d1 * cos + d2 * sin).astype(BF16)
        oq_ref[0, :, nh * (n + r2):] = (d2 * cos - d1 * sin).astype(BF16)
        okv_ref[0] = tkv_acc[...].astype(BF16)

    def row(w):
        return pl.BlockSpec((1, ts, w), lambda b, i: (b, i, 0))

    def heads(w):
        return pl.BlockSpec((1, nh, ts, w), lambda b, i: (b, 0, i, 0))

    return pl.pallas_call(
        body, name="mla_heads_bwd", grid=(bsz, s // ts),
        in_specs=[heads(n + 2 * r2), heads(n + 2 * r2), heads(vd), row(nh * r2), row(nh * r2)],
        out_specs=[row(nh * (n + 2 * r2)), row(nh * (n + vd)), row(2 * nh * r2)],
        out_shape=[jax.ShapeDtypeStruct((bsz, s, nh * (n + 2 * r2)), BF16), jax.ShapeDtypeStruct((bsz, s, nh * (n + vd)), BF16),
                   jax.ShapeDtypeStruct((bsz, s, 2 * nh * r2), F32)],
        scratch_shapes=[pltpu.VMEM((ts, nh * n), F32), pltpu.VMEM((ts, nh * (n + vd)), F32), pltpu.VMEM((2, ts, nh * r2), F32)],
        compiler_params=_params("arbitrary", "arbitrary"),
    )(dqh, dkh, dvh, cos, sin)


def _hgrn_tables(c):
    levels = c.bit_length() - 1
    assert 1 << levels == c
    r = np.arange(c)
    sign, mask = [], []
    for l in range(levels):
        lower = ((r >> l) & 1) == 1
        sign.append(np.broadcast_to(np.where(lower, 1.0, -1.0)[:, None], (c, LANES)))
        mask.append((((r[:, None] ^ r[None, :]) >> l) == 1) & lower[:, None])
    return (jnp.asarray(r[:, None] >= r[None, :], BF16), jnp.asarray(np.stack(sign), F32), jnp.asarray(np.stack(mask), F32))


def _const_specs(tables):
    return [pl.BlockSpec(t.shape, lambda b, h, i, nd=t.ndim: (0,) * nd) for t in tables]


def _split3(x):
    hi = x.astype(BF16)
    rest = x - hi.astype(F32)
    mid = rest.astype(BF16)
    return hi, mid, (rest - mid.astype(F32)).astype(BF16)


@functools.partial(jax.custom_vjp, nondiff_argnums=(2,))
def prefix_sums(p, g, n):
    k = g.shape[1]
    r = _dg(p, jnp.concatenate(_split3(g), axis=1), 1, 0)
    r = r[:, :k] + r[:, k:2 * k] + r[:, 2 * k:]
    c = r.shape[0] // n
    return tuple(r[i * c:(i + 1) * c] for i in range(n))


def _prefix_fwd(p, g, n):
    return prefix_sums(p, g, n), p


def _prefix_bwd(n, p, ct):
    ct = jnp.concatenate(ct, axis=0)
    k = ct.shape[1]
    r = _dg(p, jnp.concatenate(_split3(ct), axis=1), 0, 0)
    return jnp.zeros_like(p), r[:, :k] + r[:, k:2 * k] + r[:, 2 * k:]


prefix_sums.defvjp(_prefix_fwd, _prefix_bwd)


@functools.partial(jax.custom_vjp, nondiff_argnums=(1, 2))
def block_row(x, size, row):
    c, k = x.shape
    x3 = x.reshape(c // size, size, k)
    return jnp.broadcast_to(x3[:, row:row + 1, :], x3.shape).reshape(c, k)


def _block_row_fwd(x, size, row):
    return block_row(x, size, row), None


def _block_row_bwd(size, row, _, ct):
    c, k = ct.shape
    ct3 = ct.reshape(c // size, size, k)
    total = jnp.broadcast_to(jnp.sum(ct3, axis=1, keepdims=True), ct3.shape)
    rows = lax.broadcasted_iota(jnp.int32, ct3.shape, 1)
    return (jnp.where(rows == row, total, 0.0).reshape(c, k),)


block_row.defvjp(_block_row_fwd, _block_row_bwd)


def _hgrn_chunk(q, g, k, v, st0, prefix, sign, mask):
    levels = len(mask)
    (b,) = prefix_sums(prefix, g, 1)
    o = bdot(q * jnp.exp(b), st0, 1, 1)
    att = None
    for l in range(levels):
        e = jnp.exp((b - block_row(b, 2 << l, (1 << l) - 1)) * sign[l])
        a = bdot(q * e, k * e, 1, 1) * mask[l]
        att = a if att is None else att + a
    o = o + bdot(att, v, 1, 0) + jnp.sum(q * k, -1, keepdims=True) * v
    total = jnp.sum(g, 0, keepdims=True)
    st1 = st0 * jnp.exp(total) + bdot(v, k * jnp.exp(total - b), 0, 0)
    return o, st1


def _hgrn_step(q_raw, fx, v, g_raw, st0, lb, gn, prefix, sign, mask):
    f = lb + (1.0 - lb) * jax.nn.sigmoid(fx)
    o, st1 = _hgrn_chunk(jax.nn.silu(q_raw), jnp.log(f), 1.0 - f, v, st0, prefix, sign, mask)
    return _rms_norm(o, gn) * jax.nn.silu(g_raw), st1


def _hgrn_layout(proj):
    bsz, s, width = proj.shape
    kd = HGRN_EXPAND
    c = min(HGRN_CHUNK, s)
    nh = width // (4 * kd)
    hp = math.gcd(nh, HGRN_HEADS_PER_STEP)
    return bsz, s, kd, c, nh, s // c, hp


def hgrn_fwd(proj, lb, gn, ride=()):
    bsz, s, kd, c, nh, nc, hp = _hgrn_layout(proj)
    groups = nh // hp
    tables = _hgrn_tables(c)
    levels = tables[2].shape[0]
    grid = (groups, bsz, nc)
    rd = _Ride(ride)

    def body(*refs):
        ((q_ref, f_ref, v_ref, g_ref, lb_ref, gn_ref, p_ref, sg_ref, mk_ref), srcs, (z_ref, st_ref), outs,
         scratch) = rd.split(refs, 9, 2)
        state = scratch[0]
        rd.run(srcs, outs, scratch, grid)

        @pl.when(pl.program_id(2) == 0)
        def _():
            state[...] = jnp.zeros_like(state)

        prefix, sign, mask = p_ref[...], [sg_ref[l] for l in range(levels)], [mk_ref[l] for l in range(levels)]
        for j in range(hp):
            cols = slice(j * kd, (j + 1) * kd)
            st0 = state[j]
            st_ref[0, j, 0] = st0
            z, st1 = _hgrn_step(q_ref[0, :, cols], f_ref[0, :, cols], v_ref[0, :, cols], g_ref[0, :, cols], st0,
                                lb_ref[:, cols], gn_ref[...], prefix, sign, mask)
            z_ref[0, :, cols] = z.astype(z_ref.dtype)
            state[j] = st1

    def part(k):
        return pl.BlockSpec((1, c, hp * kd), lambda h, b, i: (b, i, k * groups + h))

    res = pl.pallas_call(
        body, name="hgrn_fwd", grid=grid,
        in_specs=[part(0), part(1), part(2), part(3), pl.BlockSpec((1, hp * kd), lambda h, b, i: (0, h)),
                  pl.BlockSpec((1, kd), lambda h, b, i: (0, 0))] + _const_specs(tables) + rd.in_specs,
        out_specs=[part(0), pl.BlockSpec((1, hp, 1, kd, kd), lambda h, b, i: (b, h, i, 0, 0))] + rd.out_specs,
        out_shape=[jax.ShapeDtypeStruct((bsz, s, nh * kd), BF16), jax.ShapeDtypeStruct((bsz, nh, nc, kd, kd), F32)] + rd.out_shape,
        scratch_shapes=[pltpu.VMEM((hp, kd, kd), F32)] + rd.scratch,
        compiler_params=_params("arbitrary", "arbitrary", "arbitrary"),
    )(proj, proj, proj, proj, lb, gn, *tables, *rd.srcs)
    return res[0], res[1], res[2:]


def hgrn_bwd(proj, states, dz, lb, gn, ride=()):
    bsz, s, kd, c, nh, nc, hp = _hgrn_layout(proj)
    groups = nh // hp
    tables = _hgrn_tables(c)
    levels = tables[2].shape[0]
    grid = (groups, bsz, nc)
    rd = _Ride(ride)

    def body(*refs):
        ((q_ref, f_ref, v_ref, g_ref, st_ref, dz_ref, lb_ref, gn_ref, p_ref, sg_ref, mk_ref), srcs,
         (dq_ref, df_ref, dv_ref, dg_ref, dlb_ref, dgn_ref), outs, scratch) = rd.split(refs, 11, 6)
        dstate = scratch[0]
        rd.run(srcs, outs, scratch, grid)
        first_of_group = (pl.program_id(1) == 0) & (pl.program_id(2) == 0)

        @pl.when(pl.program_id(2) == 0)
        def _():
            dstate[...] = jnp.zeros_like(dstate)

        @pl.when(first_of_group)
        def _():
            dlb_ref[...] = jnp.zeros_like(dlb_ref)

        @pl.when(first_of_group & (pl.program_id(0) == 0))
        def _():
            dgn_ref[...] = jnp.zeros_like(dgn_ref)

        prefix, sign, mask = p_ref[...], [sg_ref[l] for l in range(levels)], [mk_ref[l] for l in range(levels)]
        for j in range(hp):
            cols = slice(j * kd, (j + 1) * kd)
            _, vjp = jax.vjp(lambda q, f, v, g, st, lb, gn: _hgrn_step(q, f, v, g, st, lb, gn, prefix, sign, mask),
                             q_ref[0, :, cols], f_ref[0, :, cols], v_ref[0, :, cols], g_ref[0, :, cols], st_ref[0, j, 0],
                             lb_ref[:, cols], gn_ref[...])
            dq, df, dv, dg, dst, dlb, dgn = vjp((dz_ref[0, :, cols], dstate[j]))
            dq_ref[0, :, cols] = dq.astype(dq_ref.dtype)
            df_ref[0, :, cols] = df.astype(df_ref.dtype)
            dv_ref[0, :, cols] = dv.astype(dv_ref.dtype)
            dg_ref[0, :, cols] = dg.astype(dg_ref.dtype)
            dlb_ref[:, cols] += dlb
            dgn_ref[...] += dgn
            dstate[j] = dst

    def part(k):
        return pl.BlockSpec((1, c, hp * kd), lambda h, b, i: (b, nc - 1 - i, k * groups + h))

    shape = jax.ShapeDtypeStruct((bsz, s, nh * kd), BF16)
    lb_spec = pl.BlockSpec((1, hp * kd), lambda h, b, i: (0, h))
    gn_spec = pl.BlockSpec((1, kd), lambda h, b, i: (0, 0))
    res = pl.pallas_call(
        body, name="hgrn_bwd", grid=grid,
        in_specs=[part(0), part(1), part(2), part(3), pl.BlockSpec((1, hp, 1, kd, kd), lambda h, b, i: (b, h, nc - 1 - i, 0, 0)),
                  part(0), lb_spec, gn_spec] + _const_specs(tables) + rd.in_specs,
        out_specs=[part(0)] * 4 + [lb_spec, gn_spec] + rd.out_specs,
        out_shape=[shape] * 4 + [jax.ShapeDtypeStruct(lb.shape, F32), jax.ShapeDtypeStruct(gn.shape, F32)] + rd.out_shape,
        scratch_shapes=[pltpu.VMEM((hp, kd, kd), F32)] + rd.scratch,
        compiler_params=_params("arbitrary", "arbitrary", "arbitrary"),
    )(proj, proj, proj, proj, states, dz, lb, gn, *tables, *rd.srcs)
    return list(res[:4]), res[4], res[5], res[6:]


def cast_bf16(name, w):
    blk = pl.BlockSpec((1,) + w.shape[1:], lambda l: (l, 0, 0))

    def body(w_ref, o_ref):
        o_ref[...] = w_ref[...].astype(BF16)

    return pl.pallas_call(body, name=name, grid=(w.shape[0],), in_specs=[blk], out_specs=blk,
                          out_shape=jax.ShapeDtypeStruct(w.shape, BF16), compiler_params=_params("arbitrary"))(w)


def _lower_bounds(rows):
    m = functools.reduce(jnp.maximum, rows)
    e = [jnp.exp(r - m) for r in rows]
    z = functools.reduce(lambda a, b: a + b, e)
    soft = [x / z for x in e]
    out, run = [], jnp.zeros_like(rows[0])
    for sft in soft:
        run = run + sft
        out.append(run - soft[0])
    return out


def lower_bounds(lb):
    n = lb.shape[0]

    def body(lb_ref, o_ref):
        for i, r in enumerate(_lower_bounds([lb_ref[i:i + 1, :] for i in range(n)])):
            o_ref[i:i + 1, :] = r

    return pl.pallas_call(body, name="lower_bounds", out_shape=jax.ShapeDtypeStruct(lb.shape, F32),
                          compiler_params=_params())(lb)


def ada_fwd(c_all, ada_w, ada_b):
    nl, ns, d, cols = ada_w.shape
    n_ex = c_all.shape[0]

    def body(c_ref, w_ref, b_ref, o_ref):
        a = jax.nn.silu(c_ref[...]).astype(BF16)
        o_ref[0] = _dg(a, w_ref[0].astype(BF16), 1, 0) + b_ref[0]

    return pl.pallas_call(
        body, name="ada_fwd", grid=(nl * ns,),
        in_specs=[pl.BlockSpec((n_ex, d), lambda i: (0, 0)), pl.BlockSpec((1, d, cols), lambda i: (i, 0, 0)),
                  pl.BlockSpec((1, 1, cols), lambda i: (i, 0, 0))],
        out_specs=pl.BlockSpec((1, n_ex, cols), lambda i: (i, 0, 0)),
        out_shape=jax.ShapeDtypeStruct((nl * ns, n_ex, cols), F32), compiler_params=_params("arbitrary"),
    )(c_all, ada_w.reshape(nl * ns, d, cols), ada_b.reshape(nl * ns, 1, cols))


def ada_bwd(c_all, dmod):
    n, n_ex, cols = dmod.shape
    d = c_all.shape[1]

    def body(c_ref, g_ref, dw_ref, db_ref):
        a = jax.nn.silu(c_ref[...]).astype(BF16)
        g = g_ref[0]
        dw_ref[0] = _dg(a, g.astype(BF16), 0, 0)
        db_ref[0] = jnp.sum(g, 0, keepdims=True)

    return pl.pallas_call(
        body, name="ada_bwd", grid=(n,),
        in_specs=[pl.BlockSpec((n_ex, d), lambda i: (0, 0)), pl.BlockSpec((1, n_ex, cols), lambda i: (i, 0, 0))],
        out_specs=[pl.BlockSpec((1, d, cols), lambda i: (i, 0, 0)), pl.BlockSpec((1, 1, cols), lambda i: (i, 0, 0))],
        out_shape=[jax.ShapeDtypeStruct((n, d, cols), F32), jax.ShapeDtypeStruct((n, 1, cols), F32)],
        compiler_params=_params("arbitrary"),
    )(c_all, dmod)


def _adam_math(g, w, m, v):
    m = ADAM_B1 * m + (1.0 - ADAM_B1) * g
    v = ADAM_B2 * v + (1.0 - ADAM_B2) * jnp.square(g)
    m_hat = m / (1.0 - ADAM_B1 ** ADAM_STEP)
    v_hat = v / (1.0 - ADAM_B2 ** ADAM_STEP)
    delta = -ADAM_LR * (m_hat / (jnp.sqrt(v_hat) + ADAM_EPS) + ADAM_WD * w)
    return delta, m, v


def adam(name, gstack, w, m, v):
    shape = w.shape
    n, cols = gstack.shape[0], shape[-1]
    rows = math.prod(shape[:-1])
    tr = _pick_rows(rows, max(8, (2 * 1024 * 1024) // (4 * cols * n)))

    def body(g_ref, w_ref, m_ref, v_ref, go_ref, d_ref, mo_ref, vo_ref):
        g = g_ref[0].astype(F32)
        for i in range(1, n):
            g = g + g_ref[i].astype(F32)
        delta, m1, v1 = _adam_math(g, w_ref[...], m_ref[...], v_ref[...])
        go_ref[...] = g
        d_ref[...] = delta
        mo_ref[...] = m1
        vo_ref[...] = v1

    blk = pl.BlockSpec((tr, cols), lambda i: (i, 0))
    out = pl.pallas_call(
        body, name=name, grid=(rows // tr,),
        in_specs=[pl.BlockSpec((n, tr, cols), lambda i: (0, i, 0)), blk, blk, blk],
        out_specs=[blk] * 4, out_shape=[jax.ShapeDtypeStruct((rows, cols), F32)] * 4,
        compiler_params=_params("arbitrary"),
    )(gstack.reshape(n, rows, cols), w.reshape(rows, cols), m.reshape(rows, cols), v.reshape(rows, cols))
    return [o.reshape(shape) for o in out]


def adam_layers(name, gs, w, m, v):
    shape = w.shape
    nl, n, cols = len(gs), gs[0].shape[0], shape[-1]
    rows = math.prod(shape[1:-1])
    tr = _pick_rows(rows, max(16, (2 * 1024 * 1024) // (4 * cols * n)))
    nt = rows // tr

    def body(*refs):
        g_refs, (w_ref, m_ref, v_ref, go_ref, d_ref, mo_ref, vo_ref) = refs[:nl], refs[nl:]
        for j in range(nl):
            @pl.when(pl.program_id(0) == j)
            def _(j=j):
                g = g_refs[j][0].astype(F32)
                for i in range(1, n):
                    g = g + g_refs[j][i].astype(F32)
                delta, m1, v1 = _adam_math(g, w_ref[...], m_ref[...], v_ref[...])
                go_ref[...] = g
                d_ref[...] = delta
                mo_ref[...] = m1
                vo_ref[...] = v1

    def g_spec(j):
        return pl.BlockSpec((n, tr, cols), lambda l, i: (0, jnp.where(l == j, i, jnp.where(l < j, 0, nt - 1)), 0))

    blk = pl.BlockSpec((tr, cols), lambda l, i: (l * nt + i, 0))
    out = pl.pallas_call(
        body, name=name, grid=(nl, nt),
        in_specs=[g_spec(j) for j in range(nl)] + [blk, blk, blk],
        out_specs=[blk] * 4, out_shape=[jax.ShapeDtypeStruct((nl * rows, cols), F32)] * 4,
        compiler_params=_params("arbitrary", "arbitrary"),
    )(*[g.reshape(n, rows, cols) for g in gs], w.reshape(nl * rows, cols), m.reshape(nl * rows, cols), v.reshape(nl * rows, cols))
    return [o.reshape(shape) for o in out]


def adam_lb(gstack, lb, m, v):
    n, nl = gstack.shape[0], lb.shape[0]

    def body(g_ref, w_ref, m_ref, v_ref, go_ref, d_ref, mo_ref, vo_ref):
        rows = [w_ref[i:i + 1, :] for i in range(nl)]
        ct = []
        for i in range(nl):
            g = g_ref[0, i:i + 1, :]
            for j in range(1, n):
                g = g + g_ref[j, i:i + 1, :]
            ct.append(g)
        _, vjp = jax.vjp(_lower_bounds, rows)
        (grads,) = vjp(ct)
        for i in range(nl):
            delta, m1, v1 = _adam_math(grads[i], rows[i], m_ref[i:i + 1, :], v_ref[i:i + 1, :])
            go_ref[i:i + 1, :] = grads[i]
            d_ref[i:i + 1, :] = delta
            mo_ref[i:i + 1, :] = m1
            vo_ref[i:i + 1, :] = v1

    return pl.pallas_call(body, name="adam_hgrn_lb", out_shape=[jax.ShapeDtypeStruct(lb.shape, F32)] * 4,
                          compiler_params=_params())(gstack, lb, m, v)


class _Ride:
    def __init__(self, items):
        self.items = list(items)
        n = len(self.items)
        self.srcs = [src for src, _ in self.items]
        self.in_specs = [pl.BlockSpec(memory_space=pl.ANY)] * n
        self.out_specs = [pl.BlockSpec(memory_space=pl.ANY)] * n
        self.out_shape = [jax.ShapeDtypeStruct(((N_DEV,) + s.shape) if mode == "gather" else s.shape, s.dtype)
                          for s, mode in self.items]
        self.scratch = [pltpu.SemaphoreType.DMA((n, N_DEV - 1)), pltpu.SemaphoreType.DMA((n, N_DEV - 1)),
                        pltpu.SemaphoreType.DMA((n,))] if n else []

    def split(self, refs, n_in, n_out):
        n = len(self.items)
        a, b = n_in + n, n_in + 2 * n + n_out
        return refs[:n_in], refs[n_in:a], refs[a:a + n_out], refs[a + n_out:b], refs[b:]

    def _plan(self, srcs, outs, sems):
        send_sems, recv_sems, local_sems = sems
        x, y, c = lax.axis_index("x"), lax.axis_index("y"), lax.axis_index("c")
        me = 4 * x + 2 * y + c

        def remote(src, dst, i, k, dev):
            return pltpu.make_async_remote_copy(src_ref=src, dst_ref=dst, send_sem=send_sems.at[i, k], recv_sem=recv_sems.at[i, k],
                                                device_id=dev, device_id_type=pl.DeviceIdType.MESH)

        first, relays, final, final_send = [], [], [], []
        for i, (_, mode) in enumerate(self.items):
            if mode == "a2a":
                copies = [pltpu.make_async_copy(srcs[i].at[me], outs[i].at[me], local_sems.at[i])]
                for p in range(1, N_DEV):
                    px = 1 - x if p & 4 else x
                    py = 1 - y if p & 2 else y
                    pc = 1 - c if p & 1 else c
                    copies.append(remote(srcs[i].at[4 * px + 2 * py + pc], outs[i].at[me], i, p - 1, (px, py, pc)))
                first += copies
                final += copies
                continue
            mine = outs[i].at[me]
            own = [pltpu.make_async_copy(srcs[i], mine, local_sems.at[i]), remote(srcs[i], mine, i, 0, (x, y, 1 - c))]
            first += own
            final += own
            for j, (px, py) in enumerate([(1 - x, y), (x, 1 - y), (1 - x, 1 - y)]):
                theirs = outs[i].at[4 * px + 2 * py + c]
                over_ici = remote(srcs[i], mine, i, 1 + j, (px, py, c))
                relay = remote(theirs, theirs, i, 4 + j, (x, y, 1 - c))
                first.append(over_ici)
                relays.append((over_ici, relay))
                final.append(relay)
                final_send.append(over_ici)
        return first, relays, final, final_send

    def run(self, srcs, outs, scratch, grid=()):
        if not self.items:
            return
        sems = scratch[len(scratch) - 3:]

        def start():
            for cp in self._plan(srcs, outs, sems)[0]:
                cp.start()

        def relay():
            for arrival, onward in self._plan(srcs, outs, sems)[1]:
                arrival.wait_recv()
                onward.start()

        def finish():
            _, _, final, final_send = self._plan(srcs, outs, sems)
            for cp in final_send:
                cp.wait_send()
            for cp in final:
                cp.wait()

        if not grid:
            start()
            relay()
            finish()
            return
        total = math.prod(grid)
        step = pl.program_id(0)
        for a in range(1, len(grid)):
            step = step * grid[a] + pl.program_id(a)
        pl.when(step == 0)(start)
        if any(mode == "gather" for _, mode in self.items):
            pl.when(step == min(total - 1, (3 * total) // 4))(relay)
        pl.when(step == total - 1)(finish)


def exchange(name, items):
    rd = _Ride(items)

    def body(*refs):
        _, srcs, _, outs, scratch = rd.split(refs, 0, 0)
        rd.run(srcs, outs, scratch)

    return pl.pallas_call(body, name=name, in_specs=rd.in_specs, out_specs=rd.out_specs, out_shape=rd.out_shape,
                          scratch_shapes=rd.scratch)(*rd.srcs)


def _from_gather(name, g):
    if name in COL_SHARDED:
        _, k, n = g.shape
        return g.transpose(1, 0, 2).reshape(k, N_DEV * n)
    return g.reshape(-1, g.shape[-1])


def _to_slabs(name, w):
    if isinstance(w, tuple):
        per = N_DEV // len(w)
        return jnp.concatenate([p.reshape(p.shape[0], per, p.shape[1] // per).transpose(1, 0, 2) for p in w], axis=0)
    k, n = w.shape
    if name in COL_SHARDED:
        return w.reshape(k, N_DEV, n // N_DEV).transpose(1, 0, 2)
    return w.reshape(N_DEV, k // N_DEV, n)


def _w_in_internal(w):
    return jnp.pad(w, ((0, 0), (0, LANES - QK_ROPE)))


def _qb_internal(w, inverse=False):
    h, n, r2 = MLA_HEADS, QK_NOPE, QK_ROPE // 2
    lead = w.shape[:-1]
    if not inverse:
        w = w.reshape(lead + (h, n + 2 * r2))
        parts = [w[..., :n], w[..., n:n + r2], w[..., n + r2:]]
        return jnp.concatenate([p.reshape(lead + (-1,)) for p in parts], axis=-1)
    parts = [w[..., :h * n].reshape(lead + (h, n)), w[..., h * n:h * (n + r2)].reshape(lead + (h, r2)),
             w[..., h * (n + r2):].reshape(lead + (h, r2))]
    return jnp.concatenate(parts, axis=-1).reshape(lead + (-1,))


def _kvb_internal(w, inverse=False):
    h, n, vd = MLA_HEADS, QK_NOPE, V_HEAD
    lead = w.shape[:-1]
    if not inverse:
        w = w.reshape(lead + (h, n + vd))
        return jnp.concatenate([w[..., :n].reshape(lead + (-1,)), w[..., n:].reshape(lead + (-1,))], axis=-1)
    parts = [w[..., :h * n].reshape(lead + (h, n)), w[..., h * n:].reshape(lead + (h, vd))]
    return jnp.concatenate(parts, axis=-1).reshape(lead + (-1,))


def _mla_forward(h, w, tabs, ride=()):
    cos, sin = tabs
    r2 = MLA_HEADS * (QK_ROPE // 2)
    proj = mm3("mla_proj", h, w['w_in'])
    qn, kvn, krt = rowwise(
        "mla_mid", lambda rv, ev, gv: (f_mla_mid(rv, ev, gv), []),
        [_view(proj, 0, Q_LORA), _view(proj, Q_LORA, KV_LORA), _view(proj, Q_LORA + KV_LORA, LANES), _full(cos), _full(sin)],
        [], [w['q_norm'], w['kv_norm']], [(BF16, [Q_LORA]), (BF16, [KV_LORA]), (BF16, [r2, r2])], ts=ROW_TILE)
    q = mm3("mla_q", qn, w['w_qb'])
    kv = mm3("mla_kv", kvn, w['w_kvb'])
    qh, kh, vh = mla_heads(q, kv, krt, cos, sin)
    o, got = attn_fwd(qh, kh, vh, ride)
    y = mm3("mla_out", o, w['w_o'])
    return y, dict(h=h, proj=proj, qn=qn, kvn=kvn, qh=qh, kh=kh, vh=vh, o=o), got


def _mla_backward(dy, sv, w, tabs, ride=()):
    cos, sin = tabs
    r2 = MLA_HEADS * (QK_ROPE // 2)
    g = {}
    g['w_o'] = wgrad("mla_out_dw", sv['o'], dy)
    do = mm3("mla_out_dx", dy, w['w_o'], tb=True)
    dqh, dkh, dvh, got = attn_bwd(sv['qh'], sv['kh'], sv['vh'], do, ride)
    dq, dkv, dkrt = mla_heads_bwd(dqh, dkh, dvh, cos, sin)
    g['w_qb'] = wgrad("mla_q_dw", sv['qn'], dq)
    g['w_kvb'] = wgrad("mla_kv_dw", sv['kvn'], dkv)
    dqn = mm3("mla_q_dx", dq, w['w_qb'], tb=True)
    dkvn = mm3("mla_kv_dx", dkv, w['w_kvb'], tb=True)
    proj = sv['proj']
    (dproj,), _, (g['q_norm'], g['kv_norm']) = rowwise_bwd(
        "mla_mid_bwd", f_mla_mid,
        [_view(proj, 0, Q_LORA), _view(proj, Q_LORA, KV_LORA), _view(proj, Q_LORA + KV_LORA, LANES), _full(cos), _full(sin)],
        [], [w['q_norm'], w['kv_norm']], [(dqn, [Q_LORA]), (dkvn, [KV_LORA]), (dkrt, [r2, r2])],
        [(BF16, [0, 1, 2])], ts=ROW_TILE, n_diff=3)
    g['w_in'] = wgrad("mla_proj_dw", sv['h'], dproj)
    dh = mm3("mla_proj_dx", dproj, w['w_in'], tb=True)
    return dh, g, got


def _hgrn_forward(h, w, ride=()):
    proj = mm3("hgrn_proj", h, w['w_in'])
    z, states, got = hgrn_fwd(proj, w['lb'], w['g_norm'], ride)
    y = mm3("hgrn_out", z, w['w_o'])
    return y, dict(h=h, proj=proj, states=states, z=z), got


def _hgrn_backward(dy, sv, w, ride=()):
    g = {}
    g['w_o'] = wgrad("hgrn_out_dw", sv['z'], dy)
    dz = mm3("hgrn_out_dx", dy, w['w_o'], tb=True)
    dparts, g['lb'], g['g_norm'], got = hgrn_bwd(sv['proj'], sv['states'], dz, w['lb'], w['g_norm'], ride)
    g['w_in'] = tuple(wgrad("hgrn_proj_dw", sv['h'], p) for p in dparts)
    dh = mm3("hgrn_proj_dx", dparts, w['w_in'], tb=True)
    return dh, g, got


def _ffn_forward(h, w, ride=()):
    (ug, uu, a), got = ffn_in_act(h, w['w_in'], ride)
    y = mm3("ffn_out", a, w['w_out'])
    return y, dict(h=h, ug=ug, uu=uu, a=a), got


def _ffn_backward(dy, sv, w):
    g = {}
    g['w_out'] = wgrad("ffn_out_dw", sv['a'], dy)
    dug, duu = ffn_out_dx_act(dy, w['w_out'], sv['ug'], sv['uu'])
    g['w_in'] = (wgrad("ffn_in_dw", sv['h'], dug), wgrad("ffn_in_dw", sv['h'], duu))
    dh = mm3("ffn_in_dx", [dug, duu], w['w_in'], tb=True)
    return dh, g


def kernel(x, c, positions, mla_w_in, mla_q_norm, mla_w_qb, mla_kv_norm, mla_w_kvb, mla_w_o, hgrn_lb, hgrn_w_in, hgrn_g_norm, hgrn_w_o, ffn_w_in, ffn_w_out, ada_w, ada_b, ln_g, ln_b, loss_target, m_mla_w_in, m_mla_q_norm, m_mla_w_qb, m_mla_kv_norm, m_mla_w_kvb, m_mla_w_o, m_hgrn_lb, m_hgrn_w_in, m_hgrn_g_norm, m_hgrn_w_o, m_ffn_w_in, m_ffn_w_out, m_ada_w, m_ada_b, m_ln_g, m_ln_b, v_mla_w_in, v_mla_q_norm, v_mla_w_qb, v_mla_kv_norm, v_mla_w_kvb, v_mla_w_o, v_hgrn_lb, v_hgrn_w_in, v_hgrn_g_norm, v_hgrn_w_o, v_ffn_w_in, v_ffn_w_out, v_ada_w, v_ada_b, v_ln_g, v_ln_b):
    W = dict(zip(WEIGHTS, (mla_w_in, mla_q_norm, mla_w_qb, mla_kv_norm, mla_w_kvb, mla_w_o, hgrn_lb, hgrn_w_in, hgrn_g_norm,
                           hgrn_w_o, ffn_w_in, ffn_w_out, ada_w, ada_b, ln_g, ln_b)))
    M1 = dict(zip(WEIGHTS, (m_mla_w_in, m_mla_q_norm, m_mla_w_qb, m_mla_kv_norm, m_mla_w_kvb, m_mla_w_o, m_hgrn_lb, m_hgrn_w_in,
                            m_hgrn_g_norm, m_hgrn_w_o, m_ffn_w_in, m_ffn_w_out, m_ada_w, m_ada_b, m_ln_g, m_ln_b)))
    M2 = dict(zip(WEIGHTS, (v_mla_w_in, v_mla_q_norm, v_mla_w_qb, v_mla_kv_norm, v_mla_w_kvb, v_mla_w_o, v_hgrn_lb, v_hgrn_w_in,
                            v_hgrn_g_norm, v_hgrn_w_o, v_ffn_w_in, v_ffn_w_out, v_ada_w, v_ada_b, v_ln_g, v_ln_b)))
    bsz, seq, d = x.shape
    depth, n_mla, n_hgrn = ffn_w_in.shape[0], mla_w_in.shape[0], hgrn_w_in.shape[0]
    n_sub = 2 * depth

    big = COL_SHARDED + ROW_SHARDED
    wb = {n: cast_bf16("cast_" + n, W[n]) for n in big}

    def mixer_names(layer):
        mixer = ['mla_w_in', 'mla_w_qb', 'mla_w_kvb', 'mla_w_o'] if layer % 2 == 0 else ['hgrn_w_in', 'hgrn_w_o']
        return [(n, layer // 2) for n in mixer]

    def carried(layer):
        return [('ffn_w_in', layer), ('ffn_w_out', layer)] + (mixer_names(layer + 1) if layer + 1 < depth else [])

    def carried_fwd(k):
        layer = k // 2
        if k % 2 == 0:
            return [('ffn_w_in', layer), ('ffn_w_out', layer)]
        return mixer_names(layer + 1) if layer + 1 < depth else []

    def weight_items(names):
        return [(wb[n][j], "gather") for n, j in names]

    G = {}
    internal = {'mla_w_in': _w_in_internal, 'mla_w_qb': _qb_internal, 'mla_w_kvb': _kvb_internal}

    def take_weights(names, got):
        for (n, j), a in zip(names, got):
            G[n, j] = internal.get(n, lambda w: w)(_from_gather(n, a))

    lower_shard = lower_bounds(hgrn_lb)
    got = exchange("gather_first", [(lower_shard, "gather"), (ln_g, "gather"), (ln_b, "gather"), (c, "gather")]
                   + weight_items(mixer_names(0)))
    lower_all = got[0].transpose(1, 0, 2).reshape(n_hgrn, -1)
    ln_g_all = got[1].transpose(1, 2, 0, 3).reshape(depth, 2, d)
    ln_b_all = got[2].transpose(1, 2, 0, 3).reshape(depth, 2, d)
    c_all = got[3].reshape(N_DEV * bsz, d)
    take_weights(mixer_names(0), got[4:])

    cols = ada_w.shape[-1]
    mod_loc = ada_fwd(c_all, ada_w, ada_b)
    (mod_got,) = exchange("scatter_mod", [(mod_loc.reshape(n_sub, N_DEV, bsz, cols).transpose(1, 0, 2, 3), "a2a")])
    mod = mod_got.transpose(1, 2, 0, 3).reshape(n_sub, bsz, 1, 3 * d)
    shift = [mod[k, :, :, 0:d] for k in range(n_sub)]
    scale = [mod[k, :, :, d:2 * d] for k in range(n_sub)]
    gate = [mod[k, :, :, 2 * d:] for k in range(n_sub)]
    lng = [ln_g_all[k // 2, k % 2][None, :] for k in range(n_sub)]
    lnb = [ln_b_all[k // 2, k % 2][None, :] for k in range(n_sub)]

    tabs = rope_tables(positions)

    def sub_weights(k):
        layer, j = k // 2, k // 4
        if k % 2:
            return 'ffn', layer, dict(w_in=G['ffn_w_in', layer], w_out=G['ffn_w_out', layer])
        if layer % 2 == 0:
            return 'mla', j, dict(w_in=G['mla_w_in', j], q_norm=mla_q_norm[j][None, :], w_qb=G['mla_w_qb', j],
                                  kv_norm=mla_kv_norm[j][None, :], w_kvb=G['mla_w_kvb', j], w_o=G['mla_w_o', j])
        return 'hgrn', j, dict(w_in=G['hgrn_w_in', j], lb=lower_all[j][None, :], g_norm=hgrn_g_norm[j][None, :],
                               w_o=G['hgrn_w_o', j])

    (h,) = rowwise("mod_first", lambda rv, ev, gv: (f_mod(rv, ev, gv), []), [_full(x)], [scale[0], shift[0]], [],
                   [(BF16, [d])], ts=ROW_TILE)
    xs, ys, saved = [x], [], []
    loss_acc = None
    for k in range(n_sub):
        kind, _, w = sub_weights(k)
        ride = weight_items(carried_fwd(k))
        if kind == 'ffn':
            y, sv, got = _ffn_forward(h, w, ride)
        elif kind == 'mla':
            y, sv, got = _mla_forward(h, w, tabs, ride)
        else:
            y, sv, got = _hgrn_forward(h, w, ride)
        take_weights(carried_fwd(k), got)
        ys.append(y)
        saved.append(sv)
        if k + 1 < n_sub:
            xn, h = rowwise("ln_mod", lambda rv, ev, gv: (f_ln_mod(rv, ev, gv), []), [_full(xs[k]), _full(y)],
                            [gate[k], scale[k + 1], shift[k + 1]], [lng[k], lnb[k]], [(F32, [d]), (BF16, [d])], ts=ROW_TILE)
            xs.append(xn)
        else:
            def loss_rows(rv, ev, gv):
                (row,) = f_ln_loss(rv, ev, gv)
                return [], [jnp.broadcast_to(jnp.sum(row, keepdims=True), (1, LANES))]
            (loss_acc,) = rowwise("ln_loss", loss_rows, [_full(xs[k]), _full(y), _full(loss_target)], [gate[k]], [lng[k], lnb[k]],
                                  [], ts=ROW_TILE, accs=[LANES])
    loss = lax.psum(loss_acc[0, 0], ("x", "y", "c"))

    d_shift, d_scale, d_gate = [None] * n_sub, [None] * n_sub, [None] * n_sub
    d_lng, d_lnb = [None] * n_sub, [None] * n_sub
    part = {n: [None] * W[n].shape[0] for n in ['mla_q_norm', 'mla_kv_norm', 'hgrn_g_norm']}
    recv = {n: [None] * W[n].shape[0] for n in big}
    d_lower = [None] * n_hgrn
    k = n_sub - 1
    (dx, dy), (d_gate[k],), (d_lng[k], d_lnb[k]) = rowwise_bwd(
        "ln_loss_bwd", f_ln_loss, [_full(xs[k]), _full(ys[k]), _full(loss_target)], [gate[k]], [lng[k], lnb[k]], [],
        [(F32, [0]), (BF16, [1])], ts=ROW_TILE, n_diff=2, unit_ct=1)
    grad_x = None
    mine = {}

    def take_grads(names, got):
        for (n, j), a in zip(names, got):
            recv[n][j] = a

    def grad_items(names):
        return [(_to_slabs(n, mine[n, jj]), "a2a") for n, jj in names]

    for k in range(n_sub - 1, -1, -1):
        kind, j, w = sub_weights(k)
        ride = grad_items(carried(k // 2)) if kind != 'ffn' else []
        if kind == 'ffn':
            dh, g = _ffn_backward(dy, saved[k], w)
            new = {('ffn_w_in', j): g['w_in'], ('ffn_w_out', j): g['w_out']}
        elif kind == 'mla':
            dh, g, got = _mla_backward(dy, saved[k], w, tabs, ride)
            new = {('mla_w_in', j): g['w_in'][:, :mla_w_in.shape[-1] * N_DEV], ('mla_w_qb', j): _qb_internal(g['w_qb'], inverse=True),
                   ('mla_w_kvb', j): _kvb_internal(g['w_kvb'], inverse=True), ('mla_w_o', j): g['w_o']}
            part['mla_q_norm'][j], part['mla_kv_norm'][j] = g['q_norm'][0], g['kv_norm'][0]
        else:
            dh, g, got = _hgrn_backward(dy, saved[k], w, ride)
            new = {('hgrn_w_in', j): g['w_in'], ('hgrn_w_o', j): g['w_o']}
            part['hgrn_g_norm'][j] = g['g_norm'][0]
            d_lower[j] = g['lb'][0]
        if kind != 'ffn':
            take_grads(carried(k // 2), got)
        mine.update(new)
        if k:
            (dx, dy), (d_gate[k - 1], d_scale[k], d_shift[k]), (d_lng[k - 1], d_lnb[k - 1]) = rowwise_bwd(
                "ln_mod_bwd", f_ln_mod, [_full(xs[k - 1]), _full(ys[k - 1])], [gate[k - 1], scale[k], shift[k]],
                [lng[k - 1], lnb[k - 1]], [(dx, [d]), (dh, [d])], [(F32, [0]), (BF16, [1])], ts=ROW_TILE, n_diff=2)
        else:
            (grad_x,), (d_scale[0], d_shift[0]), _ = rowwise_bwd(
                "mod_first_bwd", f_mod_with_x, [_full(x)], [scale[0], shift[0]], [], [(dh, [d]), (dx, [d])],
                [(F32, [0])], ts=ROW_TILE, n_diff=1)

    waiting = mixer_names(0)
    slabs = grad_items(waiting)
    slabs.append((jnp.stack(d_lower).reshape(n_hgrn, N_DEV, -1).transpose(1, 0, 2), "a2a"))
    for parts in (d_lng, d_lnb):
        full = jnp.stack([p[0] for p in parts]).reshape(depth, 2, N_DEV, d // N_DEV)
        slabs.append((full.transpose(2, 0, 1, 3), "a2a"))
    dmod = jnp.concatenate([jnp.stack(d_shift), jnp.stack(d_scale), jnp.stack(d_gate)], axis=-1)
    slabs.append((dmod.reshape(n_sub, bsz, N_DEV, cols).transpose(2, 0, 1, 3), "a2a"))
    small = ['mla_q_norm', 'mla_kv_norm', 'hgrn_g_norm']
    slabs += [(jnp.stack(part[n]), "gather") for n in small]
    got = exchange("scatter_last", slabs)
    take_grads(waiting, got)
    stacks = dict(zip(['hgrn_lb', 'ln_g', 'ln_b', 'dmod'] + small, got[len(waiting):]))

    dmod_all = stacks['dmod'].transpose(1, 0, 2, 3).reshape(n_sub, N_DEV * bsz, cols)
    g_ada_w, g_ada_b = ada_bwd(c_all, dmod_all)
    stacks['ada_w'] = g_ada_w.reshape((1,) + ada_w.shape)
    stacks['ada_b'] = g_ada_b.reshape((1,) + ada_b.shape)

    res = {}
    for n in WEIGHTS:
        if n == 'hgrn_lb':
            res[n] = adam_lb(stacks[n], W[n], M1[n], M2[n])
        elif n in big:
            res[n] = adam_layers("adam_" + n, recv[n], W[n], M1[n], M2[n])
        else:
            res[n] = adam("adam_" + n, stacks[n], W[n], M1[n], M2[n])
    return (loss, grad_x, *[res[n][0] for n in WEIGHTS], *[res[n][1] for n in WEIGHTS], *[res[n][2] for n in WEIGHTS],
            *[res[n][3] for n in WEIGHTS])
```

```python
import functools
import math

import numpy as np
import jax
import jax.numpy as jnp
from jax import lax
from jax.experimental import pallas as pl
from jax.experimental.pallas import tpu as pltpu

F32 = jnp.float32
BF16 = jnp.bfloat16

N_DEV = 8
LANES = 128
VMEM_LIMIT = 52 * 1024 * 1024

D_MODEL = 1024
DEPTH = 4
MLA_HEADS = 16
QK_NOPE = 64
QK_ROPE = 32
V_HEAD = 64
Q_LORA = 768
KV_LORA = 256
ROPE_THETA = 10000.0
HGRN_EXPAND = 128
HGRN_CHUNK = 128
HGRN_HEADS_PER_STEP = 8
D_FF = 2816
ALPHA = (2.0 * DEPTH) ** 0.25
LN_EPS = 1e-5
RMS_EPS = 1e-6
ADAM_LR = 0.001
ADAM_B1 = 0.9
ADAM_B2 = 0.999
ADAM_EPS = 1e-08
ADAM_WD = 0.01
ADAM_STEP = 10

ATTN_TQ = 512
ROW_TILE = 256

WEIGHTS = ['mla_w_in', 'mla_q_norm', 'mla_w_qb', 'mla_kv_norm', 'mla_w_kvb', 'mla_w_o', 'hgrn_lb', 'hgrn_w_in',
           'hgrn_g_norm', 'hgrn_w_o', 'ffn_w_in', 'ffn_w_out', 'ada_w', 'ada_b', 'ln_g', 'ln_b']
COL_SHARDED = ['mla_w_in', 'mla_w_qb', 'mla_w_kvb', 'hgrn_w_in', 'ffn_w_in']
ROW_SHARDED = ['mla_w_o', 'hgrn_w_o', 'ffn_w_out']


def _params(*sem):
    if sem:
        return pltpu.CompilerParams(dimension_semantics=sem, vmem_limit_bytes=VMEM_LIMIT)
    return pltpu.CompilerParams(vmem_limit_bytes=VMEM_LIMIT)


def _pick(n, cap):
    best = None
    for t in range(LANES, min(n, cap) + 1, LANES):
        if n % t == 0:
            best = t
    return best or n


def _pick_rows(n, cap):
    best = None
    for t in range(8, min(n, cap) + 1, 8):
        if n % t == 0:
            best = t
    return best or n


def matmul(name, a, b, *, ta=False, tb=False, out_dtype=F32, tm_cap=1536, tn_cap=1536, tk_cap=2816):
    parts = list(a) if isinstance(a, (list, tuple)) else [a]
    n_parts = len(parts)
    assert n_parts == 1 or not ta
    (kp, m) = parts[0].shape if ta else parts[0].shape[::-1]
    (n, k2) = b.shape if tb else b.shape[::-1]
    assert kp * n_parts == k2, (name, parts[0].shape, b.shape)
    tm, tn, tk = _pick(m, tm_cap), _pick(n, tn_cap), _pick(kp, tk_cap)
    nkp = kp // tk
    nk = nkp * n_parts
    dims = (((0 if ta else 1,), (1 if tb else 0,)), ((), ()))

    def prod(a_ref, b_ref):
        return lax.dot_general(a_ref[...].astype(BF16), b_ref[...].astype(BF16), dims, preferred_element_type=F32)

    if nk == 1:
        def body(a_ref, b_ref, o_ref):
            o_ref[...] = prod(a_ref, b_ref).astype(o_ref.dtype)
        scratch = []
    else:
        def body(*refs):
            a_refs, (b_ref, o_ref, acc_ref) = refs[:n_parts], refs[n_parts:]
            k = pl.program_id(2)

            @pl.when(k == 0)
            def _():
                acc_ref[...] = jnp.zeros_like(acc_ref)

            if n_parts == 1:
                acc_ref[...] += prod(a_refs[0], b_ref)
            else:
                for p in range(n_parts):
                    @pl.when((k >= p * nkp) & (k < (p + 1) * nkp))
                    def _(p=p):
                        acc_ref[...] += prod(a_refs[p], b_ref)

            @pl.when(k == nk - 1)
            def _():
                o_ref[...] = acc_ref[...].astype(o_ref.dtype)
        scratch = [pltpu.VMEM((tm, tn), F32)]

    if ta:
        a_specs = [pl.BlockSpec((tk, tm), lambda i, j, k: (k, i))]
    elif n_parts == 1:
        a_specs = [pl.BlockSpec((tm, tk), lambda i, j, k: (i, k))]
    else:
        a_specs = [pl.BlockSpec((tm, tk), lambda i, j, k, p=p: (i, jnp.clip(k - p * nkp, 0, nkp - 1))) for p in range(n_parts)]
    b_spec = pl.BlockSpec((tn, tk), lambda i, j, k: (j, k)) if tb else pl.BlockSpec((tk, tn), lambda i, j, k: (k, j))
    return pl.pallas_call(
        body, name=name, grid=(m // tm, n // tn, nk),
        in_specs=a_specs + [b_spec], out_specs=pl.BlockSpec((tm, tn), lambda i, j, k: (i, j)),
        out_shape=jax.ShapeDtypeStruct((m, n), out_dtype), scratch_shapes=scratch,
        compiler_params=_params("parallel", "parallel", "arbitrary"),
    )(*parts, b)


def mm3(name, a3, w, **kw):
    parts = list(a3) if isinstance(a3, (list, tuple)) else [a3]
    bsz, s = parts[0].shape[:2]
    flat = [p.reshape(bsz * s, p.shape[-1]) for p in parts]
    out = matmul(name, flat if len(flat) > 1 else flat[0], w, **kw)
    return out.reshape(bsz, s, out.shape[-1])


def ffn_in_act(h3, w_in, ride=()):
    bsz, s, k = h3.shape
    m, dff = bsz * s, w_in.shape[1] // 2
    tm, tn = _pick(m, 512), _pick(dff, 1536)
    nj = dff // tn
    grid = (nj, m // tm)
    rd = _Ride(ride)

    def body(*refs):
        (h_ref, wg_ref, wu_ref), srcs, (ug_ref, uu_ref, a_ref), outs, scratch = rd.split(refs, 3, 3)
        rd.run(srcs, outs, scratch, grid)
        hv = h_ref[...].astype(BF16)
        ug = _dg(hv, wg_ref[...].astype(BF16), 1, 0)
        uu = _dg(hv, wu_ref[...].astype(BF16), 1, 0)
        ug_ref[...] = ug.astype(ug_ref.dtype)
        uu_ref[...] = uu.astype(uu_ref.dtype)
        a_ref[...] = (jax.nn.silu(ug) * uu).astype(a_ref.dtype)

    out = pl.BlockSpec((tm, tn), lambda j, i: (i, j))
    res = pl.pallas_call(
        body, name="ffn_in_act", grid=grid,
        in_specs=[pl.BlockSpec((tm, k), lambda j, i: (i, 0)), pl.BlockSpec((k, tn), lambda j, i: (0, j)),
                  pl.BlockSpec((k, tn), lambda j, i: (0, nj + j))] + rd.in_specs,
        out_specs=[out, out, out] + rd.out_specs,
        out_shape=[jax.ShapeDtypeStruct((m, dff), BF16)] * 3 + rd.out_shape, scratch_shapes=rd.scratch,
        compiler_params=_params("arbitrary", "arbitrary"),
    )(h3.reshape(m, k), w_in, w_in, *rd.srcs)
    return [r.reshape(bsz, s, dff) for r in res[:3]], res[3:]


def ffn_out_dx_act(dy3, w_out, ug, uu):
    bsz, s, d = dy3.shape
    m, dff = bsz * s, w_out.shape[0]
    tm, tn = _pick(m, 512), _pick(dff, 1536)

    def body(dy_ref, w_ref, ug_ref, uu_ref, dg_ref, du_ref):
        da = _dg(dy_ref[...].astype(BF16), w_ref[...].astype(BF16), 1, 1)
        _, vjp = jax.vjp(lambda gate, up: jax.nn.silu(gate) * up, ug_ref[...].astype(F32), uu_ref[...].astype(F32))
        dg, du = vjp(da)
        dg_ref[...] = dg.astype(dg_ref.dtype)
        du_ref[...] = du.astype(du_ref.dtype)

    blk = pl.BlockSpec((tm, tn), lambda j, i: (i, j))
    res = pl.pallas_call(
        body, name="ffn_out_dx_act", grid=(dff // tn, m // tm),
        in_specs=[pl.BlockSpec((tm, d), lambda j, i: (i, 0)), pl.BlockSpec((tn, d), lambda j, i: (j, 0)), blk, blk],
        out_specs=[blk, blk], out_shape=[jax.ShapeDtypeStruct((m, dff), BF16)] * 2,
        compiler_params=_params("arbitrary", "arbitrary"),
    )(dy3.reshape(m, d), w_out, ug.reshape(m, dff), uu.reshape(m, dff))
    return [r.reshape(bsz, s, dff) for r in res]


def wgrad(name, a3, g3):
    bsz, s, k = a3.shape
    return matmul(name, a3.reshape(bsz * s, k), g3.reshape(bsz * s, g3.shape[-1]), ta=True, out_dtype=BF16)


def _dg(a, b, ca, cb, **kw):
    return lax.dot_general(a, b, (((ca,), (cb,)), ((), ())), preferred_element_type=F32, **kw)


@functools.partial(jax.custom_vjp, nondiff_argnums=(2, 3))
def bdot(a, b, ca, cb):
    return _dg(a.astype(BF16), b.astype(BF16), ca, cb)


def _bdot_fwd(a, b, ca, cb):
    return bdot(a, b, ca, cb), (a, b)


def _bdot_bwd(ca, cb, res, g):
    a, b = res
    a16, b16, g16 = a.astype(BF16), b.astype(BF16), g.astype(BF16)
    if ca == 1:
        da = _dg(g16, b16, 1, 1 if cb == 0 else 0)
    else:
        da = _dg(b16, g16, 1 if cb == 0 else 0, 1)
    if cb == 0:
        db = _dg(a16, g16, 0 if ca == 1 else 1, 0)
    else:
        db = _dg(g16, a16, 0, 0 if ca == 1 else 1)
    return da, db


bdot.defvjp(_bdot_fwd, _bdot_bwd)


def hdot(a, b, ca=1, cb=0):
    return _dg(a, b, ca, cb, precision=lax.Precision.HIGHEST)


def _row_specs(rows, exs, globs, ts):
    specs = [pl.BlockSpec((1, ts, w), lambda b, s, j=j: (b, s, j)) for (_, j, w) in rows]
    specs += [pl.BlockSpec((1, 1, e.shape[-1]), lambda b, s: (b, 0, 0)) for e in exs]
    specs += [pl.BlockSpec((1, g.shape[-1]), lambda b, s: (0, 0)) for g in globs]
    return specs


def _store_pieces(o_ref, pieces, widths):
    off = 0
    for p, w in zip(pieces, widths):
        o_ref[0, :, off:off + w] = p.astype(o_ref.dtype)
        off += w


def _load_pieces(c_ref, widths):
    out, off = [], 0
    for w in widths:
        out.append(c_ref[0, :, off:off + w].astype(F32))
        off += w
    return out


def rowwise(name, f, rows, exs, globs, outs, *, ts, accs=()):
    bsz, s = rows[0][0].shape[:2]
    ts = min(ts, s)
    n_r, n_e, n_g, n_o = len(rows), len(exs), len(globs), len(outs)

    def body(*refs):
        rv = [r[0].astype(F32) for r in refs[:n_r]]
        ev = [e[0] for e in refs[n_r:n_r + n_e]]
        gv = [g[...] for g in refs[n_r + n_e:n_r + n_e + n_g]]
        o_refs = refs[n_r + n_e + n_g:n_r + n_e + n_g + n_o]
        a_refs = refs[n_r + n_e + n_g + n_o:]
        pieces, sums = f(rv, ev, gv)
        idx = 0
        for o_ref, (_, ws) in zip(o_refs, outs):
            _store_pieces(o_ref, pieces[idx:idx + len(ws)], ws)
            idx += len(ws)
        if accs:
            @pl.when((pl.program_id(0) == 0) & (pl.program_id(1) == 0))
            def _():
                for a_ref in a_refs:
                    a_ref[...] = jnp.zeros_like(a_ref)
            for a_ref, val in zip(a_refs, sums):
                a_ref[...] += val

    out_specs = [pl.BlockSpec((1, ts, sum(ws)), lambda b, s: (b, s, 0)) for (_, ws) in outs]
    out_specs += [pl.BlockSpec((1, w), lambda b, s: (0, 0)) for w in accs]
    out_shape = [jax.ShapeDtypeStruct((bsz, s, sum(ws)), dt) for (dt, ws) in outs]
    out_shape += [jax.ShapeDtypeStruct((1, w), F32) for w in accs]
    return pl.pallas_call(
        body, name=name, grid=(bsz, s // ts),
        in_specs=_row_specs(rows, exs, globs, ts), out_specs=out_specs, out_shape=out_shape,
        compiler_params=_params("arbitrary", "arbitrary"),
    )(*[r[0] for r in rows], *exs, *globs)


def rowwise_bwd(name, f, rows, exs, globs, cts, d_groups, *, ts, n_diff, unit_ct=0):
    bsz, s = rows[0][0].shape[:2]
    ts = min(ts, s)
    n_r, n_e, n_g, n_c = len(rows), len(exs), len(globs), len(cts)
    n_d = len(d_groups)

    def body(*refs):
        rv = [r[0].astype(F32) for r in refs[:n_r]]
        ev = [e[0] for e in refs[n_r:n_r + n_e]]
        gv = [g[...] for g in refs[n_r + n_e:n_r + n_e + n_g]]
        base = n_r + n_e + n_g
        c_refs = refs[base:base + n_c]
        d_refs = refs[base + n_c:base + n_c + n_d]
        de_refs = refs[base + n_c + n_d:base + n_c + n_d + n_e]
        dg_refs = refs[base + n_c + n_d + n_e:]
        fixed = rv[n_diff:]
        out, vjp = jax.vjp(lambda r, e, g: f(r + fixed, e, g), rv[:n_diff], ev, gv)
        ct = []
        for c_ref, (_, ws) in zip(c_refs, cts):
            ct += _load_pieces(c_ref, ws)
        ct += [jnp.ones_like(o) for o in out[len(ct):]]
        assert len(ct) == len(out) and len(out) - unit_ct == sum(len(ws) for _, ws in cts), name
        d_r, d_e, d_g = vjp(ct)
        for d_ref, (_, idxs) in zip(d_refs, d_groups):
            _store_pieces(d_ref, [d_r[i] for i in idxs], [rows[i][2] for i in idxs])
        first_s = pl.program_id(1) == 0
        if n_e:
            @pl.when(first_s)
            def _():
                for r in de_refs:
                    r[...] = jnp.zeros_like(r)
            for r, val in zip(de_refs, d_e):
                r[0] += val
        if n_g:
            @pl.when(first_s & (pl.program_id(0) == 0))
            def _():
                for r in dg_refs:
                    r[...] = jnp.zeros_like(r)
            for r, val in zip(dg_refs, d_g):
                r[...] += val

    in_specs = _row_specs(rows, exs, globs, ts)
    in_specs += [pl.BlockSpec((1, ts, sum(ws)), lambda b, s: (b, s, 0)) for (_, ws) in cts]
    out_specs = [pl.BlockSpec((1, ts, sum(rows[i][2] for i in idxs)), lambda b, s: (b, s, 0)) for (_, idxs) in d_groups]
    out_specs += [pl.BlockSpec((1, 1, e.shape[-1]), lambda b, s: (b, 0, 0)) for e in exs]
    out_specs += [pl.BlockSpec((1, g.shape[-1]), lambda b, s: (0, 0)) for g in globs]
    out_shape = [jax.ShapeDtypeStruct((bsz, s, sum(rows[i][2] for i in idxs)), dt) for (dt, idxs) in d_groups]
    out_shape += [jax.ShapeDtypeStruct(e.shape, F32) for e in exs]
    out_shape += [jax.ShapeDtypeStruct(g.shape, F32) for g in globs]
    res = pl.pallas_call(
        body, name=name, grid=(bsz, s // ts),
        in_specs=in_specs, out_specs=out_specs, out_shape=out_shape,
        compiler_params=_params("arbitrary", "arbitrary"),
    )(*[r[0] for r in rows], *exs, *globs, *[c[0] for c in cts])
    return res[:n_d], res[n_d:n_d + n_e], res[n_d + n_e:]


def _full(a):
    return (a, 0, a.shape[-1])


def _view(a, col, w):
    assert col % w == 0
    return (a, col // w, w)


def _layer_norm(z, g, b):
    mu = jnp.mean(z, -1, keepdims=True)
    var = jnp.mean(jnp.square(z - mu), -1, keepdims=True)
    return (z - mu) * lax.rsqrt(var + LN_EPS) * g + b


def _rms_norm(z, g):
    ms = jnp.mean(jnp.square(z), -1, keepdims=True)
    return z * lax.rsqrt(ms + RMS_EPS) * g


def f_mod(rv, ev, gv):
    (x,), (scale, shift) = rv, ev
    return [x * (1.0 + scale) + shift]


def f_mod_with_x(rv, ev, gv):
    return f_mod(rv, ev, gv) + [rv[0]]


def f_ln_mod(rv, ev, gv):
    (x, y), (gate, scale, shift), (g, b) = rv, ev, gv
    xn = _layer_norm(ALPHA * x + (1.0 + gate) * y, g, b)
    return [xn, xn * (1.0 + scale) + shift]


def f_ln_loss(rv, ev, gv):
    (x, y, target), (gate,), (g, b) = rv, ev, gv
    xn = _layer_norm(ALPHA * x + (1.0 + gate) * y, g, b)
    return [0.5 * jnp.mean(jnp.square(xn - target), -1, keepdims=True)]


def _head_spread(width):
    r2 = QK_ROPE // 2
    j = lax.broadcasted_iota(jnp.int32, (LANES, width), 0)
    col = lax.broadcasted_iota(jnp.int32, (LANES, width), 1) % r2
    return (j == col).astype(F32), (j == col + r2).astype(F32)


def f_mla_mid(rv, ev, gv):
    (q_lat, kv_lat, kr, cos, sin), (q_g, kv_g) = rv, gv
    e1, e2 = _head_spread(cos.shape[-1])
    k1, k2 = hdot(kr, e1), hdot(kr, e2)
    return [_rms_norm(q_lat, q_g), _rms_norm(kv_lat, kv_g), k1 * cos - k2 * sin, k1 * sin + k2 * cos]


def rope_tables(positions):
    bsz, s = positions.shape
    r2 = QK_ROPE // 2
    width = MLA_HEADS * r2
    inv = (ROPE_THETA ** (-np.arange(0, QK_ROPE, 2, dtype=np.float32) / QK_ROPE)).astype(np.float32)
    inv = jnp.asarray(np.tile(inv, MLA_HEADS)[None, :])
    ts = min(ROW_TILE, s)

    def body(p_ref, inv_ref, cos_ref, sin_ref):
        ang = p_ref[0].astype(F32) * inv_ref[...]
        cos_ref[0] = jnp.cos(ang)
        sin_ref[0] = jnp.sin(ang)

    spec = pl.BlockSpec((1, ts, width), lambda b, s: (b, s, 0))
    return pl.pallas_call(
        body, name="rope_tables", grid=(bsz, s // ts),
        in_specs=[pl.BlockSpec((1, ts, 1), lambda b, s: (b, s, 0)), pl.BlockSpec((1, width), lambda b, s: (0, 0))],
        out_specs=[spec, spec], out_shape=[jax.ShapeDtypeStruct((bsz, s, width), F32)] * 2,
        compiler_params=_params("arbitrary", "arbitrary"),
    )(positions[:, :, None], inv)


def _attn_probs(q, k, row0):
    scale = (QK_NOPE + QK_ROPE) ** -0.5
    s = _dg(q, k, 1, 1) * scale
    rows = row0 + lax.broadcasted_iota(jnp.int32, s.shape, 0)
    cols = lax.broadcasted_iota(jnp.int32, s.shape, 1)
    s = jnp.where(cols <= rows, s, jnp.finfo(F32).min)
    e = jnp.exp(s - jnp.max(s, -1, keepdims=True))
    return e / jnp.sum(e, -1, keepdims=True), scale


ATTN_PAIR = 2


def attn_fwd(q, k, v, ride=()):
    bsz, h, s, dq = q.shape
    dv = v.shape[-1]
    tq = min(ATTN_TQ, s)
    grid = (bsz, h // ATTN_PAIR, s // tq)
    rd = _Ride(ride)

    def body(*refs):
        (q_ref, k_ref, v_ref), srcs, (o_ref,), outs, sems = rd.split(refs, 3, 1)
        rd.run(srcs, outs, sems, grid)
        for i in range(grid[2]):
            @pl.when(pl.program_id(2) == i)
            def _(i=i):
                kend = (i + 1) * tq
                for e in range(ATTN_PAIR):
                    p, _ = _attn_probs(q_ref[0, e], k_ref[0, e, :kend, :], i * tq)
                    o_ref[0, :, e * dv:(e + 1) * dv] = _dg(p.astype(BF16), v_ref[0, e, :kend, :], 1, 0).astype(o_ref.dtype)

    res = pl.pallas_call(
        body, name="attn_fwd", grid=grid,
        in_specs=[pl.BlockSpec((1, ATTN_PAIR, tq, dq), lambda b, h, i: (b, h, i, 0)),
                  pl.BlockSpec((1, ATTN_PAIR, s, dq), lambda b, h, i: (b, h, 0, 0)),
                  pl.BlockSpec((1, ATTN_PAIR, s, dv), lambda b, h, i: (b, h, 0, 0))] + rd.in_specs,
        out_specs=[pl.BlockSpec((1, tq, ATTN_PAIR * dv), lambda b, h, i: (b, i, h))] + rd.out_specs,
        out_shape=[jax.ShapeDtypeStruct((bsz, s, h * dv), BF16)] + rd.out_shape, scratch_shapes=rd.scratch,
        compiler_params=_params("arbitrary", "arbitrary", "arbitrary"),
    )(q, k, v, *rd.srcs)
    return res[0], res[1:]


def attn_bwd(q, k, v, do, ride=()):
    bsz, h, s, dq = q.shape
    dv = v.shape[-1]
    tq = min(ATTN_TQ, s)
    grid = (bsz, h // ATTN_PAIR, s // tq)
    rd = _Ride(ride)

    def body(*refs):
        (q_ref, k_ref, v_ref, do_ref), srcs, (dq_ref, dk_ref, dv_ref), outs, sems = rd.split(refs, 4, 3)
        rd.run(srcs, outs, sems, grid)

        @pl.when(pl.program_id(2) == 0)
        def _():
            dk_ref[...] = jnp.zeros_like(dk_ref)
            dv_ref[...] = jnp.zeros_like(dv_ref)

        for i in range(grid[2]):
            @pl.when(pl.program_id(2) == i)
            def _(i=i):
                kend = (i + 1) * tq
                for e in range(ATTN_PAIR):
                    qv, kv, vv = q_ref[0, e], k_ref[0, e, :kend, :], v_ref[0, e, :kend, :]
                    p, scale = _attn_probs(qv, kv, i * tq)
                    do16 = do_ref[0, :, e * dv:(e + 1) * dv].astype(BF16)
                    dv_ref[0, e, :kend, :] += _dg(p.astype(BF16), do16, 0, 0)
                    dp = _dg(do16, vv, 1, 1)
                    ds = (p * (dp - jnp.sum(dp * p, -1, keepdims=True)) * scale).astype(BF16)
                    dq_ref[0, e] = _dg(ds, kv, 1, 0)
                    dk_ref[0, e, :kend, :] += _dg(ds, qv, 0, 0)

    res = pl.pallas_call(
        body, name="attn_bwd", grid=grid,
        in_specs=[pl.BlockSpec((1, ATTN_PAIR, tq, dq), lambda b, h, i: (b, h, i, 0)),
                  pl.BlockSpec((1, ATTN_PAIR, s, dq), lambda b, h, i: (b, h, 0, 0)),
                  pl.BlockSpec((1, ATTN_PAIR, s, dv), lambda b, h, i: (b, h, 0, 0)),
                  pl.BlockSpec((1, tq, ATTN_PAIR * dv), lambda b, h, i: (b, i, h))] + rd.in_specs,
        out_specs=[pl.BlockSpec((1, ATTN_PAIR, tq, dq), lambda b, h, i: (b, h, i, 0)),
                   pl.BlockSpec((1, ATTN_PAIR, s, dq), lambda b, h, i: (b, h, 0, 0)),
                   pl.BlockSpec((1, ATTN_PAIR, s, dv), lambda b, h, i: (b, h, 0, 0))] + rd.out_specs,
        out_shape=[jax.ShapeDtypeStruct((bsz, h, s, dq), F32), jax.ShapeDtypeStruct((bsz, h, s, dq), F32),
                   jax.ShapeDtypeStruct((bsz, h, s, dv), F32)] + rd.out_shape, scratch_shapes=rd.scratch,
        compiler_params=_params("arbitrary", "arbitrary", "arbitrary"),
    )(q, k, v, do, *rd.srcs)
    return res[0], res[1], res[2], res[3:]


def mla_heads(q, kv, krt, cos, sin):
    bsz, s, _ = q.shape
    nh, n, r2, vd = MLA_HEADS, QK_NOPE, QK_ROPE // 2, V_HEAD
    ts = min(ROW_TILE, s)

    def body(q_ref, kv_ref, kr_ref, cos_ref, sin_ref, qh_ref, kh_ref, vh_ref):
        cos, sin = cos_ref[0], sin_ref[0]
        qv, kvv, kr = q_ref[0], kv_ref[0], kr_ref[0].astype(F32)
        q1, q2 = qv[:, nh * n:nh * (n + r2)], qv[:, nh * (n + r2):]
        qr = [q1 * cos - q2 * sin, q1 * sin + q2 * cos]
        for h in range(nh):
            qh_ref[0, h, :, :n] = qv[:, h * n:(h + 1) * n].astype(BF16)
            kh_ref[0, h, :, :n] = kvv[:, h * n:(h + 1) * n].astype(BF16)
            vh_ref[0, h] = kvv[:, nh * n + h * vd:nh * n + (h + 1) * vd].astype(BF16)
            for j in range(2):
                qh_ref[0, h, :, n + j * r2:n + (j + 1) * r2] = qr[j][:, h * r2:(h + 1) * r2].astype(BF16)
                kh_ref[0, h, :, n + j * r2:n + (j + 1) * r2] = kr[:, (j * nh + h) * r2:(j * nh + h + 1) * r2].astype(BF16)

    def row(a):
        return pl.BlockSpec((1, ts, a.shape[-1]), lambda b, i: (b, i, 0))

    def heads(w):
        return pl.BlockSpec((1, nh, ts, w), lambda b, i: (b, 0, i, 0))

    return pl.pallas_call(
        body, name="mla_heads", grid=(bsz, s // ts), in_specs=[row(q), row(kv), row(krt), row(cos), row(sin)],
        out_specs=[heads(n + 2 * r2), heads(n + 2 * r2), heads(vd)],
        out_shape=[jax.ShapeDtypeStruct((bsz, nh, s, n + 2 * r2), BF16)] * 2 + [jax.ShapeDtypeStruct((bsz, nh, s, vd), BF16)],
        compiler_params=_params("arbitrary", "arbitrary"),
    )(q, kv, krt, cos, sin)


def mla_heads_bwd(dqh, dkh, dvh, cos, sin):
    bsz, nh, s, _ = dqh.shape
    n, r2, vd = QK_NOPE, QK_ROPE // 2, V_HEAD
    ts = min(ROW_TILE, s)

    def body(dq_ref, dk_ref, dv_ref, cos_ref, sin_ref, oq_ref, okv_ref, okr_ref, tq_acc, tkv_acc, rot):
        for h in range(nh):
            tq_acc[:, h * n:(h + 1) * n] = dq_ref[0, h, :, :n]
            tkv_acc[:, h * n:(h + 1) * n] = dk_ref[0, h, :, :n]
            tkv_acc[:, nh * n + h * vd:nh * n + (h + 1) * vd] = dv_ref[0, h]
            for j in range(2):
                rot[j, :, h * r2:(h + 1) * r2] = dq_ref[0, h, :, n + j * r2:n + (j + 1) * r2]
                okr_ref[0, :, (j * nh + h) * r2:(j * nh + h + 1) * r2] = dk_ref[0, h, :, n + j * r2:n + (j + 1) * r2]
        cos, sin = cos_ref[0], sin_ref[0]
        d1, d2 = rot[0], rot[1]
        oq_ref[0, :, :nh * n] = tq_acc[...].astype(BF16)
        oq_ref[0, :, nh * n:nh * (n + r2)] = (d1 * cos + d2 * sin).astype(BF16)
        oq_ref[0, :, nh * (n + r2):] = (d2 * cos - d1 * sin).astype(BF16)
        okv_ref[0] = tkv_acc[...].astype(BF16)

    def row(w):
        return pl.BlockSpec((1, ts, w), lambda b, i: (b, i, 0))

    def heads(w):
        return pl.BlockSpec((1, nh, ts, w), lambda b, i: (b, 0, i, 0))

    return pl.pallas_call(
        body, name="mla_heads_bwd", grid=(bsz, s // ts),
        in_specs=[heads(n + 2 * r2), heads(n + 2 * r2), heads(vd), row(nh * r2), row(nh * r2)],
        out_specs=[row(nh * (n + 2 * r2)), row(nh * (n + vd)), row(2 * nh * r2)],
        out_shape=[jax.ShapeDtypeStruct((bsz, s, nh * (n + 2 * r2)), BF16), jax.ShapeDtypeStruct((bsz, s, nh * (n + vd)), BF16),
                   jax.ShapeDtypeStruct((bsz, s, 2 * nh * r2), F32)],
        scratch_shapes=[pltpu.VMEM((ts, nh * n), F32), pltpu.VMEM((ts, nh * (n + vd)), F32), pltpu.VMEM((2, ts, nh * r2), F32)],
        compiler_params=_params("arbitrary", "arbitrary"),
    )(dqh, dkh, dvh, cos, sin)


def _hgrn_tables(c):
    levels = c.bit_length() - 1
    assert 1 << levels == c
    r = np.arange(c)
    sign, mask = [], []
    for l in range(levels):
        lower = ((r >> l) & 1) == 1
        sign.append(np.broadcast_to(np.where(lower, 1.0, -1.0)[:, None], (c, LANES)))
        mask.append((((r[:, None] ^ r[None, :]) >> l) == 1) & lower[:, None])
    return (jnp.asarray(r[:, None] >= r[None, :], BF16), jnp.asarray(np.stack(sign), F32), jnp.asarray(np.stack(mask), F32))


def _const_specs(tables):
    return [pl.BlockSpec(t.shape, lambda b, h, i, nd=t.ndim: (0,) * nd) for t in tables]


def _split3(x):
    hi = x.astype(BF16)
    rest = x - hi.astype(F32)
    mid = rest.astype(BF16)
    return hi, mid, (rest - mid.astype(F32)).astype(BF16)


@functools.partial(jax.custom_vjp, nondiff_argnums=(2,))
def prefix_sums(p, g, n):
    k = g.shape[1]
    r = _dg(p, jnp.concatenate(_split3(g), axis=1), 1, 0)
    r = r[:, :k] + r[:, k:2 * k] + r[:, 2 * k:]
    c = r.shape[0] // n
    return tuple(r[i * c:(i + 1) * c] for i in range(n))


def _prefix_fwd(p, g, n):
    return prefix_sums(p, g, n), p


def _prefix_bwd(n, p, ct):
    ct = jnp.concatenate(ct, axis=0)
    k = ct.shape[1]
    r = _dg(p, jnp.concatenate(_split3(ct), axis=1), 0, 0)
    return jnp.zeros_like(p), r[:, :k] + r[:, k:2 * k] + r[:, 2 * k:]


prefix_sums.defvjp(_prefix_fwd, _prefix_bwd)


@functools.partial(jax.custom_vjp, nondiff_argnums=(1, 2))
def block_row(x, size, row):
    c, k = x.shape
    x3 = x.reshape(c // size, size, k)
    return jnp.broadcast_to(x3[:, row:row + 1, :], x3.shape).reshape(c, k)


def _block_row_fwd(x, size, row):
    return block_row(x, size, row), None


def _block_row_bwd(size, row, _, ct):
    c, k = ct.shape
    ct3 = ct.reshape(c // size, size, k)
    total = jnp.broadcast_to(jnp.sum(ct3, axis=1, keepdims=True), ct3.shape)
    rows = lax.broadcasted_iota(jnp.int32, ct3.shape, 1)
    return (jnp.where(rows == row, total, 0.0).reshape(c, k),)


block_row.defvjp(_block_row_fwd, _block_row_bwd)


def _hgrn_chunk(q, g, k, v, st0, prefix, sign, mask):
    levels = len(mask)
    (b,) = prefix_sums(prefix, g, 1)
    o = bdot(q * jnp.exp(b), st0, 1, 1)
    att = None
    for l in range(levels):
        e = jnp.exp((b - block_row(b, 2 << l, (1 << l) - 1)) * sign[l])
        a = bdot(q * e, k * e, 1, 1) * mask[l]
        att = a if att is None else att + a
    o = o + bdot(att, v, 1, 0) + jnp.sum(q * k, -1, keepdims=True) * v
    total = jnp.sum(g, 0, keepdims=True)
    st1 = st0 * jnp.exp(total) + bdot(v, k * jnp.exp(total - b), 0, 0)
    return o, st1


def _hgrn_step(q_raw, fx, v, g_raw, st0, lb, gn, prefix, sign, mask):
    f = lb + (1.0 - lb) * jax.nn.sigmoid(fx)
    o, st1 = _hgrn_chunk(jax.nn.silu(q_raw), jnp.log(f), 1.0 - f, v, st0, prefix, sign, mask)
    return _rms_norm(o, gn) * jax.nn.silu(g_raw), st1


def _hgrn_layout(proj):
    bsz, s, width = proj.shape
    kd = HGRN_EXPAND
    c = min(HGRN_CHUNK, s)
    nh = width // (4 * kd)
    hp = math.gcd(nh, HGRN_HEADS_PER_STEP)
    return bsz, s, kd, c, nh, s // c, hp


def hgrn_fwd(proj, lb, gn, ride=()):
    bsz, s, kd, c, nh, nc, hp = _hgrn_layout(proj)
    groups = nh // hp
    tables = _hgrn_tables(c)
    levels = tables[2].shape[0]
    grid = (groups, bsz, nc)
    rd = _Ride(ride)

    def body(*refs):
        ((q_ref, f_ref, v_ref, g_ref, lb_ref, gn_ref, p_ref, sg_ref, mk_ref), srcs, (z_ref, st_ref), outs,
         scratch) = rd.split(refs, 9, 2)
        state = scratch[0]
        rd.run(srcs, outs, scratch, grid)

        @pl.when(pl.program_id(2) == 0)
        def _():
            state[...] = jnp.zeros_like(state)

        prefix, sign, mask = p_ref[...], [sg_ref[l] for l in range(levels)], [mk_ref[l] for l in range(levels)]
        for j in range(hp):
            cols = slice(j * kd, (j + 1) * kd)
            st0 = state[j]
            st_ref[0, j, 0] = st0
            z, st1 = _hgrn_step(q_ref[0, :, cols], f_ref[0, :, cols], v_ref[0, :, cols], g_ref[0, :, cols], st0,
                                lb_ref[:, cols], gn_ref[...], prefix, sign, mask)
            z_ref[0, :, cols] = z.astype(z_ref.dtype)
            state[j] = st1

    def part(k):
        return pl.BlockSpec((1, c, hp * kd), lambda h, b, i: (b, i, k * groups + h))

    res = pl.pallas_call(
        body, name="hgrn_fwd", grid=grid,
        in_specs=[part(0), part(1), part(2), part(3), pl.BlockSpec((1, hp * kd), lambda h, b, i: (0, h)),
                  pl.BlockSpec((1, kd), lambda h, b, i: (0, 0))] + _const_specs(tables) + rd.in_specs,
        out_specs=[part(0), pl.BlockSpec((1, hp, 1, kd, kd), lambda h, b, i: (b, h, i, 0, 0))] + rd.out_specs,
        out_shape=[jax.ShapeDtypeStruct((bsz, s, nh * kd), BF16), jax.ShapeDtypeStruct((bsz, nh, nc, kd, kd), F32)] + rd.out_shape,
        scratch_shapes=[pltpu.VMEM((hp, kd, kd), F32)] + rd.scratch,
        compiler_params=_params("arbitrary", "arbitrary", "arbitrary"),
    )(proj, proj, proj, proj, lb, gn, *tables, *rd.srcs)
    return res[0], res[1], res[2:]


def hgrn_bwd(proj, states, dz, lb, gn, ride=()):
    bsz, s, kd, c, nh, nc, hp = _hgrn_layout(proj)
    groups = nh // hp
    tables = _hgrn_tables(c)
    levels = tables[2].shape[0]
    grid = (groups, bsz, nc)
    rd = _Ride(ride)

    def body(*refs):
        ((q_ref, f_ref, v_ref, g_ref, st_ref, dz_ref, lb_ref, gn_ref, p_ref, sg_ref, mk_ref), srcs,
         (dq_ref, df_ref, dv_ref, dg_ref, dlb_ref, dgn_ref), outs, scratch) = rd.split(refs, 11, 6)
        dstate = scratch[0]
        rd.run(srcs, outs, scratch, grid)
        first_of_group = (pl.program_id(1) == 0) & (pl.program_id(2) == 0)

        @pl.when(pl.program_id(2) == 0)
        def _():
            dstate[...] = jnp.zeros_like(dstate)

        @pl.when(first_of_group)
        def _():
            dlb_ref[...] = jnp.zeros_like(dlb_ref)

        @pl.when(first_of_group & (pl.program_id(0) == 0))
        def _():
            dgn_ref[...] = jnp.zeros_like(dgn_ref)

        prefix, sign, mask = p_ref[...], [sg_ref[l] for l in range(levels)], [mk_ref[l] for l in range(levels)]
        for j in range(hp):
            cols = slice(j * kd, (j + 1) * kd)
            _, vjp = jax.vjp(lambda q, f, v, g, st, lb, gn: _hgrn_step(q, f, v, g, st, lb, gn, prefix, sign, mask),
                             q_ref[0, :, cols], f_ref[0, :, cols], v_ref[0, :, cols], g_ref[0, :, cols], st_ref[0, j, 0],
                             lb_ref[:, cols], gn_ref[...])
            dq, df, dv, dg, dst, dlb, dgn = vjp((dz_ref[0, :, cols], dstate[j]))
            dq_ref[0, :, cols] = dq.astype(dq_ref.dtype)
            df_ref[0, :, cols] = df.astype(df_ref.dtype)
            dv_ref[0, :, cols] = dv.astype(dv_ref.dtype)
            dg_ref[0, :, cols] = dg.astype(dg_ref.dtype)
            dlb_ref[:, cols] += dlb
            dgn_ref[...] += dgn
            dstate[j] = dst

    def part(k):
        return pl.BlockSpec((1, c, hp * kd), lambda h, b, i: (b, nc - 1 - i, k * groups + h))

    shape = jax.ShapeDtypeStruct((bsz, s, nh * kd), BF16)
    lb_spec = pl.BlockSpec((1, hp * kd), lambda h, b, i: (0, h))
    gn_spec = pl.BlockSpec((1, kd), lambda h, b, i: (0, 0))
    res = pl.pallas_call(
        body, name="hgrn_bwd", grid=grid,
        in_specs=[part(0), part(1), part(2), part(3), pl.BlockSpec((1, hp, 1, kd, kd), lambda h, b, i: (b, h, nc - 1 - i, 0, 0)),
                  part(0), lb_spec, gn_spec] + _const_specs(tables) + rd.in_specs,
        out_specs=[part(0)] * 4 + [lb_spec, gn_spec] + rd.out_specs,
        out_shape=[shape] * 4 + [jax.ShapeDtypeStruct(lb.shape, F32), jax.ShapeDtypeStruct(gn.shape, F32)] + rd.out_shape,
        scratch_shapes=[pltpu.VMEM((hp, kd, kd), F32)] + rd.scratch,
        compiler_params=_params("arbitrary", "arbitrary", "arbitrary"),
    )(proj, proj, proj, proj, states, dz, lb, gn, *tables, *rd.srcs)
    return list(res[:4]), res[4], res[5], res[6:]


def cast_bf16(name, w):
    blk = pl.BlockSpec((1,) + w.shape[1:], lambda l: (l, 0, 0))

    def body(w_ref, o_ref):
        o_ref[...] = w_ref[...].astype(BF16)

    return pl.pallas_call(body, name=name, grid=(w.shape[0],), in_specs=[blk], out_specs=blk,
                          out_shape=jax.ShapeDtypeStruct(w.shape, BF16), compiler_params=_params("arbitrary"))(w)


def _lower_bounds(rows):
    m = functools.reduce(jnp.maximum, rows)
    e = [jnp.exp(r - m) for r in rows]
    z = functools.reduce(lambda a, b: a + b, e)
    soft = [x / z for x in e]
    out, run = [], jnp.zeros_like(rows[0])
    for sft in soft:
        run = run + sft
        out.append(run - soft[0])
    return out


def lower_bounds(lb):
    n = lb.shape[0]

    def body(lb_ref, o_ref):
        for i, r in enumerate(_lower_bounds([lb_ref[i:i + 1, :] for i in range(n)])):
            o_ref[i:i + 1, :] = r

    return pl.pallas_call(body, name="lower_bounds", out_shape=jax.ShapeDtypeStruct(lb.shape, F32),
                          compiler_params=_params())(lb)


def ada_fwd(c_all, ada_w, ada_b):
    nl, ns, d, cols = ada_w.shape
    n_ex = c_all.shape[0]

    def body(c_ref, w_ref, b_ref, o_ref):
        a = jax.nn.silu(c_ref[...]).astype(BF16)
        o_ref[0] = _dg(a, w_ref[0].astype(BF16), 1, 0) + b_ref[0]

    return pl.pallas_call(
        body, name="ada_fwd", grid=(nl * ns,),
        in_specs=[pl.BlockSpec((n_ex, d), lambda i: (0, 0)), pl.BlockSpec((1, d, cols), lambda i: (i, 0, 0)),
                  pl.BlockSpec((1, 1, cols), lambda i: (i, 0, 0))],
        out_specs=pl.BlockSpec((1, n_ex, cols), lambda i: (i, 0, 0)),
        out_shape=jax.ShapeDtypeStruct((nl * ns, n_ex, cols), F32), compiler_params=_params("arbitrary"),
    )(c_all, ada_w.reshape(nl * ns, d, cols), ada_b.reshape(nl * ns, 1, cols))


def ada_bwd(c_all, dmod):
    n, n_ex, cols = dmod.shape
    d = c_all.shape[1]

    def body(c_ref, g_ref, dw_ref, db_ref):
        a = jax.nn.silu(c_ref[...]).astype(BF16)
        g = g_ref[0]
        dw_ref[0] = _dg(a, g.astype(BF16), 0, 0)
        db_ref[0] = jnp.sum(g, 0, keepdims=True)

    return pl.pallas_call(
        body, name="ada_bwd", grid=(n,),
        in_specs=[pl.BlockSpec((n_ex, d), lambda i: (0, 0)), pl.BlockSpec((1, n_ex, cols), lambda i: (i, 0, 0))],
        out_specs=[pl.BlockSpec((1, d, cols), lambda i: (i, 0, 0)), pl.BlockSpec((1, 1, cols), lambda i: (i, 0, 0))],
        out_shape=[jax.ShapeDtypeStruct((n, d, cols), F32), jax.ShapeDtypeStruct((n, 1, cols), F32)],
        compiler_params=_params("arbitrary"),
    )(c_all, dmod)


def _adam_math(g, w, m, v):
    m = ADAM_B1 * m + (1.0 - ADAM_B1) * g
    v = ADAM_B2 * v + (1.0 - ADAM_B2) * jnp.square(g)
    m_hat = m / (1.0 - ADAM_B1 ** ADAM_STEP)
    v_hat = v / (1.0 - ADAM_B2 ** ADAM_STEP)
    delta = -ADAM_LR * (m_hat / (jnp.sqrt(v_hat) + ADAM_EPS) + ADAM_WD * w)
    return delta, m, v


def adam(name, gstack, w, m, v):
    shape = w.shape
    n, cols = gstack.shape[0], shape[-1]
    rows = math.prod(shape[:-1])
    tr = _pick_rows(rows, max(8, (2 * 1024 * 1024) // (4 * cols * n)))

    def body(g_ref, w_ref, m_ref, v_ref, go_ref, d_ref, mo_ref, vo_ref):
        g = g_ref[0].astype(F32)
        for i in range(1, n):
            g = g + g_ref[i].astype(F32)
        delta, m1, v1 = _adam_math(g, w_ref[...], m_ref[...], v_ref[...])
        go_ref[...] = g
        d_ref[...] = delta
        mo_ref[...] = m1
        vo_ref[...] = v1

    blk = pl.BlockSpec((tr, cols), lambda i: (i, 0))
    out = pl.pallas_call(
        body, name=name, grid=(rows // tr,),
        in_specs=[pl.BlockSpec((n, tr, cols), lambda i: (0, i, 0)), blk, blk, blk],
        out_specs=[blk] * 4, out_shape=[jax.ShapeDtypeStruct((rows, cols), F32)] * 4,
        compiler_params=_params("arbitrary"),
    )(gstack.reshape(n, rows, cols), w.reshape(rows, cols), m.reshape(rows, cols), v.reshape(rows, cols))
    return [o.reshape(shape) for o in out]


def adam_layers(name, gs, w, m, v):
    shape = w.shape
    nl, n, cols = len(gs), gs[0].shape[0], shape[-1]
    rows = math.prod(shape[1:-1])
    tr = _pick_rows(rows, max(16, (2 * 1024 * 1024) // (4 * cols * n)))
    nt = rows // tr

    def body(*refs):
        g_refs, (w_ref, m_ref, v_ref, go_ref, d_ref, mo_ref, vo_ref) = refs[:nl], refs[nl:]
        for j in range(nl):
            @pl.when(pl.program_id(0) == j)
            def _(j=j):
                g = g_refs[j][0].astype(F32)
                for i in range(1, n):
                    g = g + g_refs[j][i].astype(F32)
                delta, m1, v1 = _adam_math(g, w_ref[...], m_ref[...], v_ref[...])
                go_ref[...] = g
                d_ref[...] = delta
                mo_ref[...] = m1
                vo_ref[...] = v1

    def g_spec(j):
        return pl.BlockSpec((n, tr, cols), lambda l, i: (0, jnp.where(l == j, i, jnp.where(l < j, 0, nt - 1)), 0))

    blk = pl.BlockSpec((tr, cols), lambda l, i: (l * nt + i, 0))
    out = pl.pallas_call(
        body, name=name, grid=(nl, nt),
        in_specs=[g_spec(j) for j in range(nl)] + [blk, blk, blk],
        out_specs=[blk] * 4, out_shape=[jax.ShapeDtypeStruct((nl * rows, cols), F32)] * 4,
        compiler_params=_params("arbitrary", "arbitrary"),
    )(*[g.reshape(n, rows, cols) for g in gs], w.reshape(nl * rows, cols), m.reshape(nl * rows, cols), v.reshape(nl * rows, cols))
    return [o.reshape(shape) for o in out]


def adam_lb(gstack, lb, m, v):
    n, nl = gstack.shape[0], lb.shape[0]

    def body(g_ref, w_ref, m_ref, v_ref, go_ref, d_ref, mo_ref, vo_ref):
        rows = [w_ref[i:i + 1, :] for i in range(nl)]
        ct = []
        for i in range(nl):
            g = g_ref[0, i:i + 1, :]
            for j in range(1, n):
                g = g + g_ref[j, i:i + 1, :]
            ct.append(g)
        _, vjp = jax.vjp(_lower_bounds, rows)
        (grads,) = vjp(ct)
        for i in range(nl):
            delta, m1, v1 = _adam_math(grads[i], rows[i], m_ref[i:i + 1, :], v_ref[i:i + 1, :])
            go_ref[i:i + 1, :] = grads[i]
            d_ref[i:i + 1, :] = delta
            mo_ref[i:i + 1, :] = m1
            vo_ref[i:i + 1, :] = v1

    return pl.pallas_call(body, name="adam_hgrn_lb", out_shape=[jax.ShapeDtypeStruct(lb.shape, F32)] * 4,
                          compiler_params=_params())(gstack, lb, m, v)


class _Ride:
    def __init__(self, items):
        self.items = list(items)
        n = len(self.items)
        self.srcs = [src for src, _ in self.items]
        self.in_specs = [pl.BlockSpec(memory_space=pl.ANY)] * n
        self.out_specs = [pl.BlockSpec(memory_space=pl.ANY)] * n
        self.out_shape = [jax.ShapeDtypeStruct(((N_DEV,) + s.shape) if mode == "gather" else s.shape, s.dtype)
                          for s, mode in self.items]
        self.scratch = [pltpu.SemaphoreType.DMA((n, N_DEV - 1)), pltpu.SemaphoreType.DMA((n, N_DEV - 1)),
                        pltpu.SemaphoreType.DMA((n,))] if n else []

    def split(self, refs, n_in, n_out):
        n = len(self.items)
        a, b = n_in + n, n_in + 2 * n + n_out
        return refs[:n_in], refs[n_in:a], refs[a:a + n_out], refs[a + n_out:b], refs[b:]

    def _plan(self, srcs, outs, sems):
        send_sems, recv_sems, local_sems = sems
        x, y, c = lax.axis_index("x"), lax.axis_index("y"), lax.axis_index("c")
        me = 4 * x + 2 * y + c

        def remote(src, dst, i, k, dev):
            return pltpu.make_async_remote_copy(src_ref=src, dst_ref=dst, send_sem=send_sems.at[i, k], recv_sem=recv_sems.at[i, k],
                                                device_id=dev, device_id_type=pl.DeviceIdType.MESH)

        first, relays, final, final_send = [], [], [], []
        for i, (_, mode) in enumerate(self.items):
            if mode == "a2a":
                copies = [pltpu.make_async_copy(srcs[i].at[me], outs[i].at[me], local_sems.at[i])]
                for p in range(1, N_DEV):
                    px = 1 - x if p & 4 else x
                    py = 1 - y if p & 2 else y
                    pc = 1 - c if p & 1 else c
                    copies.append(remote(srcs[i].at[4 * px + 2 * py + pc], outs[i].at[me], i, p - 1, (px, py, pc)))
                first += copies
                final += copies
                continue
            mine = outs[i].at[me]
            own = [pltpu.make_async_copy(srcs[i], mine, local_sems.at[i]), remote(srcs[i], mine, i, 0, (x, y, 1 - c))]
            first += own
            final += own
            for j, (px, py) in enumerate([(1 - x, y), (x, 1 - y), (1 - x, 1 - y)]):
                theirs = outs[i].at[4 * px + 2 * py + c]
                over_ici = remote(srcs[i], mine, i, 1 + j, (px, py, c))
                relay = remote(theirs, theirs, i, 4 + j, (x, y, 1 - c))
                first.append(over_ici)
                relays.append((over_ici, relay))
                final.append(relay)
                final_send.append(over_ici)
        return first, relays, final, final_send

    def run(self, srcs, outs, scratch, grid=()):
        if not self.items:
            return
        sems = scratch[len(scratch) - 3:]

        def start():
            for cp in self._plan(srcs, outs, sems)[0]:
                cp.start()

        def relay():
            for arrival, onward in self._plan(srcs, outs, sems)[1]:
                arrival.wait_recv()
                onward.start()

        def finish():
            _, _, final, final_send = self._plan(srcs, outs, sems)
            for cp in final_send:
                cp.wait_send()
            for cp in final:
                cp.wait()

        if not grid:
            start()
            relay()
            finish()
            return
        total = math.prod(grid)
        step = pl.program_id(0)
        for a in range(1, len(grid)):
            step = step * grid[a] + pl.program_id(a)
        pl.when(step == 0)(start)
        if any(mode == "gather" for _, mode in self.items):
            pl.when(step == min(total - 1, (3 * total) // 4))(relay)
        pl.when(step == total - 1)(finish)


def exchange(name, items):
    rd = _Ride(items)

    def body(*refs):
        _, srcs, _, outs, scratch = rd.split(refs, 0, 0)
        rd.run(srcs, outs, scratch)

    return pl.pallas_call(body, name=name, in_specs=rd.in_specs, out_specs=rd.out_specs, out_shape=rd.out_shape,
                          scratch_shapes=rd.scratch)(*rd.srcs)


def _from_gather(name, g):
    if name in COL_SHARDED:
        _, k, n = g.shape
        return g.transpose(1, 0, 2).reshape(k, N_DEV * n)
    return g.reshape(-1, g.shape[-1])


def _to_slabs(name, w):
    if isinstance(w, tuple):
        per = N_DEV // len(w)
        return jnp.concatenate([p.reshape(p.shape[0], per, p.shape[1] // per).transpose(1, 0, 2) for p in w], axis=0)
    k, n = w.shape
    if name in COL_SHARDED:
        return w.reshape(k, N_DEV, n // N_DEV).transpose(1, 0, 2)
    return w.reshape(N_DEV, k // N_DEV, n)


def _w_in_internal(w):
    return jnp.pad(w, ((0, 0), (0, LANES - QK_ROPE)))


def _qb_internal(w, inverse=False):
    h, n, r2 = MLA_HEADS, QK_NOPE, QK_ROPE // 2
    lead = w.shape[:-1]
    if not inverse:
        w = w.reshape(lead + (h, n + 2 * r2))
        parts = [w[..., :n], w[..., n:n + r2], w[..., n + r2:]]
        return jnp.concatenate([p.reshape(lead + (-1,)) for p in parts], axis=-1)
    parts = [w[..., :h * n].reshape(lead + (h, n)), w[..., h * n:h * (n + r2)].reshape(lead + (h, r2)),
             w[..., h * (n + r2):].reshape(lead + (h, r2))]
    return jnp.concatenate(parts, axis=-1).reshape(lead + (-1,))


def _kvb_internal(w, inverse=False):
    h, n, vd = MLA_HEADS, QK_NOPE, V_HEAD
    lead = w.shape[:-1]
    if not inverse:
        w = w.reshape(lead + (h, n + vd))
        return jnp.concatenate([w[..., :n].reshape(lead + (-1,)), w[..., n:].reshape(lead + (-1,))], axis=-1)
    parts = [w[..., :h * n].reshape(lead + (h, n)), w[..., h * n:].reshape(lead + (h, vd))]
    return jnp.concatenate(parts, axis=-1).reshape(lead + (-1,))


def _mla_forward(h, w, tabs, ride=()):
    cos, sin = tabs
    r2 = MLA_HEADS * (QK_ROPE // 2)
    proj = mm3("mla_proj", h, w['w_in'])
    qn, kvn, krt = rowwise(
        "mla_mid", lambda rv, ev, gv: (f_mla_mid(rv, ev, gv), []),
        [_view(proj, 0, Q_LORA), _view(proj, Q_LORA, KV_LORA), _view(proj, Q_LORA + KV_LORA, LANES), _full(cos), _full(sin)],
        [], [w['q_norm'], w['kv_norm']], [(BF16, [Q_LORA]), (BF16, [KV_LORA]), (BF16, [r2, r2])], ts=ROW_TILE)
    q = mm3("mla_q", qn, w['w_qb'])
    kv = mm3("mla_kv", kvn, w['w_kvb'])
    qh, kh, vh = mla_heads(q, kv, krt, cos, sin)
    o, got = attn_fwd(qh, kh, vh, ride)
    y = mm3("mla_out", o, w['w_o'])
    return y, dict(h=h, proj=proj, qn=qn, kvn=kvn, qh=qh, kh=kh, vh=vh, o=o), got


def _mla_backward(dy, sv, w, tabs, ride=()):
    cos, sin = tabs
    r2 = MLA_HEADS * (QK_ROPE // 2)
    g = {}
    g['w_o'] = wgrad("mla_out_dw", sv['o'], dy)
    do = mm3("mla_out_dx", dy, w['w_o'], tb=True)
    dqh, dkh, dvh, got = attn_bwd(sv['qh'], sv['kh'], sv['vh'], do, ride)
    dq, dkv, dkrt = mla_heads_bwd(dqh, dkh, dvh, cos, sin)
    g['w_qb'] = wgrad("mla_q_dw", sv['qn'], dq)
    g['w_kvb'] = wgrad("mla_kv_dw", sv['kvn'], dkv)
    dqn = mm3("mla_q_dx", dq, w['w_qb'], tb=True)
    dkvn = mm3("mla_kv_dx", dkv, w['w_kvb'], tb=True)
    proj = sv['proj']
    (dproj,), _, (g['q_norm'], g['kv_norm']) = rowwise_bwd(
        "mla_mid_bwd", f_mla_mid,
        [_view(proj, 0, Q_LORA), _view(proj, Q_LORA, KV_LORA), _view(proj, Q_LORA + KV_LORA, LANES), _full(cos), _full(sin)],
        [], [w['q_norm'], w['kv_norm']], [(dqn, [Q_LORA]), (dkvn, [KV_LORA]), (dkrt, [r2, r2])],
        [(BF16, [0, 1, 2])], ts=ROW_TILE, n_diff=3)
    g['w_in'] = wgrad("mla_proj_dw", sv['h'], dproj)
    dh = mm3("mla_proj_dx", dproj, w['w_in'], tb=True)
    return dh, g, got


def _hgrn_forward(h, w, ride=()):
    proj = mm3("hgrn_proj", h, w['w_in'])
    z, states, got = hgrn_fwd(proj, w['lb'], w['g_norm'], ride)
    y = mm3("hgrn_out", z, w['w_o'])
    return y, dict(h=h, proj=proj, states=states, z=z), got


def _hgrn_backward(dy, sv, w, ride=()):
    g = {}
    g['w_o'] = wgrad("hgrn_out_dw", sv['z'], dy)
    dz = mm3("hgrn_out_dx", dy, w['w_o'], tb=True)
    dparts, g['lb'], g['g_norm'], got = hgrn_bwd(sv['proj'], sv['states'], dz, w['lb'], w['g_norm'], ride)
    g['w_in'] = tuple(wgrad("hgrn_proj_dw", sv['h'], p) for p in dparts)
    dh = mm3("hgrn_proj_dx", dparts, w['w_in'], tb=True)
    return dh, g, got


def _ffn_forward(h, w, ride=()):
    (ug, uu, a), got = ffn_in_act(h, w['w_in'], ride)
    y = mm3("ffn_out", a, w['w_out'])
    return y, dict(h=h, ug=ug, uu=uu, a=a), got


def _ffn_backward(dy, sv, w):
    g = {}
    g['w_out'] = wgrad("ffn_out_dw", sv['a'], dy)
    dug, duu = ffn_out_dx_act(dy, w['w_out'], sv['ug'], sv['uu'])
    g['w_in'] = (wgrad("ffn_in_dw", sv['h'], dug), wgrad("ffn_in_dw", sv['h'], duu))
    dh = mm3("ffn_in_dx", [dug, duu], w['w_in'], tb=True, tk_cap=1408)
    return dh, g


def kernel(x, c, positions, mla_w_in, mla_q_norm, mla_w_qb, mla_kv_norm, mla_w_kvb, mla_w_o, hgrn_lb, hgrn_w_in, hgrn_g_norm, hgrn_w_o, ffn_w_in, ffn_w_out, ada_w, ada_b, ln_g, ln_b, loss_target, m_mla_w_in, m_mla_q_norm, m_mla_w_qb, m_mla_kv_norm, m_mla_w_kvb, m_mla_w_o, m_hgrn_lb, m_hgrn_w_in, m_hgrn_g_norm, m_hgrn_w_o, m_ffn_w_in, m_ffn_w_out, m_ada_w, m_ada_b, m_ln_g, m_ln_b, v_mla_w_in, v_mla_q_norm, v_mla_w_qb, v_mla_kv_norm, v_mla_w_kvb, v_mla_w_o, v_hgrn_lb, v_hgrn_w_in, v_hgrn_g_norm, v_hgrn_w_o, v_ffn_w_in, v_ffn_w_out, v_ada_w, v_ada_b, v_ln_g, v_ln_b):
    W = dict(zip(WEIGHTS, (mla_w_in, mla_q_norm, mla_w_qb, mla_kv_norm, mla_w_kvb, mla_w_o, hgrn_lb, hgrn_w_in, hgrn_g_norm,
                           hgrn_w_o, ffn_w_in, ffn_w_out, ada_w, ada_b, ln_g, ln_b)))
    M1 = dict(zip(WEIGHTS, (m_mla_w_in, m_mla_q_norm, m_mla_w_qb, m_mla_kv_norm, m_mla_w_kvb, m_mla_w_o, m_hgrn_lb, m_hgrn_w_in,
                            m_hgrn_g_norm, m_hgrn_w_o, m_ffn_w_in, m_ffn_w_out, m_ada_w, m_ada_b, m_ln_g, m_ln_b)))
    M2 = dict(zip(WEIGHTS, (v_mla_w_in, v_mla_q_norm, v_mla_w_qb, v_mla_kv_norm, v_mla_w_kvb, v_mla_w_o, v_hgrn_lb, v_hgrn_w_in,
                            v_hgrn_g_norm, v_hgrn_w_o, v_ffn_w_in, v_ffn_w_out, v_ada_w, v_ada_b, v_ln_g, v_ln_b)))
    bsz, seq, d = x.shape
    depth, n_mla, n_hgrn = ffn_w_in.shape[0], mla_w_in.shape[0], hgrn_w_in.shape[0]
    n_sub = 2 * depth

    big = COL_SHARDED + ROW_SHARDED
    wb = {n: cast_bf16("cast_" + n, W[n]) for n in big}

    def mixer_names(layer):
        mixer = ['mla_w_in', 'mla_w_qb', 'mla_w_kvb', 'mla_w_o'] if layer % 2 == 0 else ['hgrn_w_in', 'hgrn_w_o']
        return [(n, layer // 2) for n in mixer]

    def carried(layer):
        return [('ffn_w_in', layer), ('ffn_w_out', layer)] + (mixer_names(layer + 1) if layer + 1 < depth else [])

    def carried_fwd(k):
        layer = k // 2
        if k % 2 == 0:
            return [('ffn_w_in', layer), ('ffn_w_out', layer)]
        return mixer_names(layer + 1) if layer + 1 < depth else []

    def weight_items(names):
        return [(wb[n][j], "gather") for n, j in names]

    G = {}
    internal = {'mla_w_in': _w_in_internal, 'mla_w_qb': _qb_internal, 'mla_w_kvb': _kvb_internal}

    def take_weights(names, got):
        for (n, j), a in zip(names, got):
            G[n, j] = internal.get(n, lambda w: w)(_from_gather(n, a))

    lower_shard = lower_bounds(hgrn_lb)
    got = exchange("gather_first", [(lower_shard, "gather"), (ln_g, "gather"), (ln_b, "gather"), (c, "gather")]
                   + weight_items(mixer_names(0)))
    lower_all = got[0].transpose(1, 0, 2).reshape(n_hgrn, -1)
    ln_g_all = got[1].transpose(1, 2, 0, 3).reshape(depth, 2, d)
    ln_b_all = got[2].transpose(1, 2, 0, 3).reshape(depth, 2, d)
    c_all = got[3].reshape(N_DEV * bsz, d)
    take_weights(mixer_names(0), got[4:])

    cols = ada_w.shape[-1]
    mod_loc = ada_fwd(c_all, ada_w, ada_b)
    (mod_got,) = exchange("scatter_mod", [(mod_loc.reshape(n_sub, N_DEV, bsz, cols).transpose(1, 0, 2, 3), "a2a")])
    mod = mod_got.transpose(1, 2, 0, 3).reshape(n_sub, bsz, 1, 3 * d)
    shift = [mod[k, :, :, 0:d] for k in range(n_sub)]
    scale = [mod[k, :, :, d:2 * d] for k in range(n_sub)]
    gate = [mod[k, :, :, 2 * d:] for k in range(n_sub)]
    lng = [ln_g_all[k // 2, k % 2][None, :] for k in range(n_sub)]
    lnb = [ln_b_all[k // 2, k % 2][None, :] for k in range(n_sub)]

    tabs = rope_tables(positions)

    def sub_weights(k):
        layer, j = k // 2, k // 4
        if k % 2:
            return 'ffn', layer, dict(w_in=G['ffn_w_in', layer], w_out=G['ffn_w_out', layer])
        if layer % 2 == 0:
            return 'mla', j, dict(w_in=G['mla_w_in', j], q_norm=mla_q_norm[j][None, :], w_qb=G['mla_w_qb', j],
                                  kv_norm=mla_kv_norm[j][None, :], w_kvb=G['mla_w_kvb', j], w_o=G['mla_w_o', j])
        return 'hgrn', j, dict(w_in=G['hgrn_w_in', j], lb=lower_all[j][None, :], g_norm=hgrn_g_norm[j][None, :],
                               w_o=G['hgrn_w_o', j])

    (h,) = rowwise("mod_first", lambda rv, ev, gv: (f_mod(rv, ev, gv), []), [_full(x)], [scale[0], shift[0]], [],
                   [(BF16, [d])], ts=ROW_TILE)
    xs, ys, saved = [x], [], []
    loss_acc = None
    for k in range(n_sub):
        kind, _, w = sub_weights(k)
        ride = weight_items(carried_fwd(k))
        if kind == 'ffn':
            y, sv, got = _ffn_forward(h, w, ride)
        elif kind == 'mla':
            y, sv, got = _mla_forward(h, w, tabs, ride)
        else:
            y, sv, got = _hgrn_forward(h, w, ride)
        take_weights(carried_fwd(k), got)
        ys.append(y)
        saved.append(sv)
        if k + 1 < n_sub:
            xn, h = rowwise("ln_mod", lambda rv, ev, gv: (f_ln_mod(rv, ev, gv), []), [_full(xs[k]), _full(y)],
                            [gate[k], scale[k + 1], shift[k + 1]], [lng[k], lnb[k]], [(F32, [d]), (BF16, [d])], ts=ROW_TILE)
            xs.append(xn)
        else:
            def loss_rows(rv, ev, gv):
                (row,) = f_ln_loss(rv, ev, gv)
                return [], [jnp.broadcast_to(jnp.sum(row, keepdims=True), (1, LANES))]
            (loss_acc,) = rowwise("ln_loss", loss_rows, [_full(xs[k]), _full(y), _full(loss_target)], [gate[k]], [lng[k], lnb[k]],
                                  [], ts=ROW_TILE, accs=[LANES])
    loss = lax.psum(loss_acc[0, 0], ("x", "y", "c"))

    d_shift, d_scale, d_gate = [None] * n_sub, [None] * n_sub, [None] * n_sub
    d_lng, d_lnb = [None] * n_sub, [None] * n_sub
    part = {n: [None] * W[n].shape[0] for n in ['mla_q_norm', 'mla_kv_norm', 'hgrn_g_norm']}
    recv = {n: [None] * W[n].shape[0] for n in big}
    d_lower = [None] * n_hgrn
    k = n_sub - 1
    (dx, dy), (d_gate[k],), (d_lng[k], d_lnb[k]) = rowwise_bwd(
        "ln_loss_bwd", f_ln_loss, [_full(xs[k]), _full(ys[k]), _full(loss_target)], [gate[k]], [lng[k], lnb[k]], [],
        [(F32, [0]), (BF16, [1])], ts=ROW_TILE, n_diff=2, unit_ct=1)
    grad_x = None
    mine = {}

    def take_grads(names, got):
        for (n, j), a in zip(names, got):
            recv[n][j] = a

    def grad_items(names):
        return [(_to_slabs(n, mine[n, jj]), "a2a") for n, jj in names]

    for k in range(n_sub - 1, -1, -1):
        kind, j, w = sub_weights(k)
        ride = grad_items(carried(k // 2)) if kind != 'ffn' else []
        if kind == 'ffn':
            dh, g = _ffn_backward(dy, saved[k], w)
            new = {('ffn_w_in', j): g['w_in'], ('ffn_w_out', j): g['w_out']}
        elif kind == 'mla':
            dh, g, got = _mla_backward(dy, saved[k], w, tabs, ride)
            new = {('mla_w_in', j): g['w_in'][:, :mla_w_in.shape[-1] * N_DEV], ('mla_w_qb', j): _qb_internal(g['w_qb'], inverse=True),
                   ('mla_w_kvb', j): _kvb_internal(g['w_kvb'], inverse=True), ('mla_w_o', j): g['w_o']}
            part['mla_q_norm'][j], part['mla_kv_norm'][j] = g['q_norm'][0], g['kv_norm'][0]
        else:
            dh, g, got = _hgrn_backward(dy, saved[k], w, ride)
            new = {('hgrn_w_in', j): g['w_in'], ('hgrn_w_o', j): g['w_o']}
            part['hgrn_g_norm'][j] = g['g_norm'][0]
            d_lower[j] = g['lb'][0]
        if kind != 'ffn':
            take_grads(carried(k // 2), got)
        mine.update(new)
        if k:
            (dx, dy), (d_gate[k - 1], d_scale[k], d_shift[k]), (d_lng[k - 1], d_lnb[k - 1]) = rowwise_bwd(
                "ln_mod_bwd", f_ln_mod, [_full(xs[k - 1]), _full(ys[k - 1])], [gate[k - 1], scale[k], shift[k]],
                [lng[k - 1], lnb[k - 1]], [(dx, [d]), (dh, [d])], [(F32, [0]), (BF16, [1])], ts=ROW_TILE, n_diff=2)
        else:
            (grad_x,), (d_scale[0], d_shift[0]), _ = rowwise_bwd(
                "mod_first_bwd", f_mod_with_x, [_full(x)], [scale[0], shift[0]], [], [(dh, [d]), (dx, [d])],
                [(F32, [0])], ts=ROW_TILE, n_diff=1)

    waiting = mixer_names(0)
    slabs = grad_items(waiting)
    slabs.append((jnp.stack(d_lower).reshape(n_hgrn, N_DEV, -1).transpose(1, 0, 2), "a2a"))
    for parts in (d_lng, d_lnb):
        full = jnp.stack([p[0] for p in parts]).reshape(depth, 2, N_DEV, d // N_DEV)
        slabs.append((full.transpose(2, 0, 1, 3), "a2a"))
    dmod = jnp.concatenate([jnp.stack(d_shift), jnp.stack(d_scale), jnp.stack(d_gate)], axis=-1)
    slabs.append((dmod.reshape(n_sub, bsz, N_DEV, cols).transpose(2, 0, 1, 3), "a2a"))
    small = ['mla_q_norm', 'mla_kv_norm', 'hgrn_g_norm']
    slabs += [(jnp.stack(part[n]), "gather") for n in small]
    got = exchange("scatter_last", slabs)
    take_grads(waiting, got)
    stacks = dict(zip(['hgrn_lb', 'ln_g', 'ln_b', 'dmod'] + small, got[len(waiting):]))

    dmod_all = stacks['dmod'].transpose(1, 0, 2, 3).reshape(n_sub, N_DEV * bsz, cols)
    g_ada_w, g_ada_b = ada_bwd(c_all, dmod_all)
    stacks['ada_w'] = g_ada_w.reshape((1,) + ada_w.shape)
    stacks['ada_b'] = g_ada_b.reshape((1,) + ada_b.shape)

    res = {}
    for n in WEIGHTS:
        if n == 'hgrn_lb':
            res[n] = adam_lb(stacks[n], W[n], M1[n], M2[n])
        elif n in big:
            res[n] = adam_layers("adam_" + n, recv[n], W[n], M1[n], M2[n])
        else:
            res[n] = adam("adam_" + n, stacks[n], W[n], M1[n], M2[n])
    return (loss, grad_x, *[res[n][0] for n in WEIGHTS], *[res[n][1] for n in WEIGHTS], *[res[n][2] for n in WEIGHTS],
            *[res[n][3] for n in WEIGHTS])
```

```python
import functools
import math

import numpy as np
import jax
import jax.numpy as jnp
from jax import lax
from jax.experimental import pallas as pl
from jax.experimental.pallas import tpu as pltpu

F32 = jnp.float32
BF16 = jnp.bfloat16

N_DEV = 8
LANES = 128
VMEM_LIMIT = 52 * 1024 * 1024

D_MODEL = 1024
DEPTH = 4
MLA_HEADS = 16
QK_NOPE = 64
QK_ROPE = 32
V_HEAD = 64
Q_LORA = 768
KV_LORA = 256
ROPE_THETA = 10000.0
HGRN_EXPAND = 128
HGRN_CHUNK = 128
HGRN_HEADS_PER_STEP = 8
D_FF = 2816
ALPHA = (2.0 * DEPTH) ** 0.25
LN_EPS = 1e-5
RMS_EPS = 1e-6
ADAM_LR = 0.001
ADAM_B1 = 0.9
ADAM_B2 = 0.999
ADAM_EPS = 1e-08
ADAM_WD = 0.01
ADAM_STEP = 10

ATTN_TQ = 512
ROW_TILE = 256

WEIGHTS = ['mla_w_in', 'mla_q_norm', 'mla_w_qb', 'mla_kv_norm', 'mla_w_kvb', 'mla_w_o', 'hgrn_lb', 'hgrn_w_in',
           'hgrn_g_norm', 'hgrn_w_o', 'ffn_w_in', 'ffn_w_out', 'ada_w', 'ada_b', 'ln_g', 'ln_b']
COL_SHARDED = ['mla_w_in', 'mla_w_qb', 'mla_w_kvb', 'hgrn_w_in', 'ffn_w_in']
ROW_SHARDED = ['mla_w_o', 'hgrn_w_o', 'ffn_w_out']


def _params(*sem):
    if sem:
        return pltpu.CompilerParams(dimension_semantics=sem, vmem_limit_bytes=VMEM_LIMIT)
    return pltpu.CompilerParams(vmem_limit_bytes=VMEM_LIMIT)


def _pick(n, cap):
    best = None
    for t in range(LANES, min(n, cap) + 1, LANES):
        if n % t == 0:
            best = t
    return best or n


def _pick_rows(n, cap):
    best = None
    for t in range(8, min(n, cap) + 1, 8):
        if n % t == 0:
            best = t
    return best or n


def matmul(name, a, b, *, ta=False, tb=False, out_dtype=F32, tm_cap=1536, tn_cap=1536, tk_cap=2816):
    parts = list(a) if isinstance(a, (list, tuple)) else [a]
    n_parts = len(parts)
    assert n_parts == 1 or not ta
    (kp, m) = parts[0].shape if ta else parts[0].shape[::-1]
    (n, k2) = b.shape if tb else b.shape[::-1]
    assert kp * n_parts == k2, (name, parts[0].shape, b.shape)
    tm, tn, tk = _pick(m, tm_cap), _pick(n, tn_cap), _pick(kp, tk_cap)
    nkp = kp // tk
    nk = nkp * n_parts
    dims = (((0 if ta else 1,), (1 if tb else 0,)), ((), ()))

    def prod(a_ref, b_ref):
        return lax.dot_general(a_ref[...].astype(BF16), b_ref[...].astype(BF16), dims, preferred_element_type=F32)

    if nk == 1:
        def body(a_ref, b_ref, o_ref):
            o_ref[...] = prod(a_ref, b_ref).astype(o_ref.dtype)
        scratch = []
    else:
        def body(*refs):
            a_refs, (b_ref, o_ref, acc_ref) = refs[:n_parts], refs[n_parts:]
            k = pl.program_id(2)

            @pl.when(k == 0)
            def _():
                acc_ref[...] = jnp.zeros_like(acc_ref)

            if n_parts == 1:
                acc_ref[...] += prod(a_refs[0], b_ref)
            else:
                for p in range(n_parts):
                    @pl.when((k >= p * nkp) & (k < (p + 1) * nkp))
                    def _(p=p):
                        acc_ref[...] += prod(a_refs[p], b_ref)

            @pl.when(k == nk - 1)
            def _():
                o_ref[...] = acc_ref[...].astype(o_ref.dtype)
        scratch = [pltpu.VMEM((tm, tn), F32)]

    if ta:
        a_specs = [pl.BlockSpec((tk, tm), lambda i, j, k: (k, i))]
    elif n_parts == 1:
        a_specs = [pl.BlockSpec((tm, tk), lambda i, j, k: (i, k))]
    else:
        a_specs = [pl.BlockSpec((tm, tk), lambda i, j, k, p=p: (i, jnp.clip(k - p * nkp, 0, nkp - 1))) for p in range(n_parts)]
    b_spec = pl.BlockSpec((tn, tk), lambda i, j, k: (j, k)) if tb else pl.BlockSpec((tk, tn), lambda i, j, k: (k, j))
    return pl.pallas_call(
        body, name=name, grid=(m // tm, n // tn, nk),
        in_specs=a_specs + [b_spec], out_specs=pl.BlockSpec((tm, tn), lambda i, j, k: (i, j)),
        out_shape=jax.ShapeDtypeStruct((m, n), out_dtype), scratch_shapes=scratch,
        compiler_params=_params("parallel", "parallel", "arbitrary"),
    )(*parts, b)


def mm3(name, a3, w, **kw):
    parts = list(a3) if isinstance(a3, (list, tuple)) else [a3]
    bsz, s = parts[0].shape[:2]
    flat = [p.reshape(bsz * s, p.shape[-1]) for p in parts]
    out = matmul(name, flat if len(flat) > 1 else flat[0], w, **kw)
    return out.reshape(bsz, s, out.shape[-1])


def ffn_in_act(h3, w_in, ride=()):
    bsz, s, k = h3.shape
    m, dff = bsz * s, w_in.shape[1] // 2
    tm, tn = _pick(m, 512), _pick(dff, 1536)
    nj = dff // tn
    grid = (nj, m // tm)
    rd = _Ride(ride)

    def body(*refs):
        (h_ref, wg_ref, wu_ref), srcs, (ug_ref, uu_ref, a_ref), outs, scratch = rd.split(refs, 3, 3)
        rd.run(srcs, outs, scratch, grid)
        hv = h_ref[...].astype(BF16)
        ug = _dg(hv, wg_ref[...].astype(BF16), 1, 0)
        uu = _dg(hv, wu_ref[...].astype(BF16), 1, 0)
        ug_ref[...] = ug.astype(ug_ref.dtype)
        uu_ref[...] = uu.astype(uu_ref.dtype)
        a_ref[...] = (jax.nn.silu(ug) * uu).astype(a_ref.dtype)

    out = pl.BlockSpec((tm, tn), lambda j, i: (i, j))
    res = pl.pallas_call(
        body, name="ffn_in_act", grid=grid,
        in_specs=[pl.BlockSpec((tm, k), lambda j, i: (i, 0)), pl.BlockSpec((k, tn), lambda j, i: (0, j)),
                  pl.BlockSpec((k, tn), lambda j, i: (0, nj + j))] + rd.in_specs,
        out_specs=[out, out, out] + rd.out_specs,
        out_shape=[jax.ShapeDtypeStruct((m, dff), BF16)] * 3 + rd.out_shape, scratch_shapes=rd.scratch,
        compiler_params=_params("arbitrary", "arbitrary"),
    )(h3.reshape(m, k), w_in, w_in, *rd.srcs)
    return [r.reshape(bsz, s, dff) for r in res[:3]], res[3:]


def ffn_out_dx_act(dy3, w_out, ug, uu):
    bsz, s, d = dy3.shape
    m, dff = bsz * s, w_out.shape[0]
    tm, tn = _pick(m, 512), _pick(dff, 1536)

    def body(dy_ref, w_ref, ug_ref, uu_ref, dg_ref, du_ref):
        da = _dg(dy_ref[...].astype(BF16), w_ref[...].astype(BF16), 1, 1)
        _, vjp = jax.vjp(lambda gate, up: jax.nn.silu(gate) * up, ug_ref[...].astype(F32), uu_ref[...].astype(F32))
        dg, du = vjp(da)
        dg_ref[...] = dg.astype(dg_ref.dtype)
        du_ref[...] = du.astype(du_ref.dtype)

    blk = pl.BlockSpec((tm, tn), lambda j, i: (i, j))
    res = pl.pallas_call(
        body, name="ffn_out_dx_act", grid=(dff // tn, m // tm),
        in_specs=[pl.BlockSpec((tm, d), lambda j, i: (i, 0)), pl.BlockSpec((tn, d), lambda j, i: (j, 0)), blk, blk],
        out_specs=[blk, blk], out_shape=[jax.ShapeDtypeStruct((m, dff), BF16)] * 2,
        compiler_params=_params("arbitrary", "arbitrary"),
    )(dy3.reshape(m, d), w_out, ug.reshape(m, dff), uu.reshape(m, dff))
    return [r.reshape(bsz, s, dff) for r in res]


def wgrad(name, a3, g3):
    bsz, s, k = a3.shape
    return matmul(name, a3.reshape(bsz * s, k), g3.reshape(bsz * s, g3.shape[-1]), ta=True, out_dtype=BF16)


def _dg(a, b, ca, cb, **kw):
    return lax.dot_general(a, b, (((ca,), (cb,)), ((), ())), preferred_element_type=F32, **kw)


@functools.partial(jax.custom_vjp, nondiff_argnums=(2, 3))
def bdot(a, b, ca, cb):
    return _dg(a.astype(BF16), b.astype(BF16), ca, cb)


def _bdot_fwd(a, b, ca, cb):
    return bdot(a, b, ca, cb), (a, b)


def _bdot_bwd(ca, cb, res, g):
    a, b = res
    a16, b16, g16 = a.astype(BF16), b.astype(BF16), g.astype(BF16)
    if ca == 1:
        da = _dg(g16, b16, 1, 1 if cb == 0 else 0)
    else:
        da = _dg(b16, g16, 1 if cb == 0 else 0, 1)
    if cb == 0:
        db = _dg(a16, g16, 0 if ca == 1 else 1, 0)
    else:
        db = _dg(g16, a16, 0, 0 if ca == 1 else 1)
    return da, db


bdot.defvjp(_bdot_fwd, _bdot_bwd)


def hdot(a, b, ca=1, cb=0):
    return _dg(a, b, ca, cb, precision=lax.Precision.HIGHEST)


def _row_specs(rows, exs, globs, ts):
    specs = [pl.BlockSpec((1, ts, w), lambda b, s, j=j: (b, s, j)) for (_, j, w) in rows]
    specs += [pl.BlockSpec((1, 1, e.shape[-1]), lambda b, s: (b, 0, 0)) for e in exs]
    specs += [pl.BlockSpec((1, g.shape[-1]), lambda b, s: (0, 0)) for g in globs]
    return specs


def _store_pieces(o_ref, pieces, widths):
    off = 0
    for p, w in zip(pieces, widths):
        o_ref[0, :, off:off + w] = p.astype(o_ref.dtype)
        off += w


def _load_pieces(c_ref, widths):
    out, off = [], 0
    for w in widths:
        out.append(c_ref[0, :, off:off + w].astype(F32))
        off += w
    return out


def rowwise(name, f, rows, exs, globs, outs, *, ts, accs=()):
    bsz, s = rows[0][0].shape[:2]
    ts = min(ts, s)
    n_r, n_e, n_g, n_o = len(rows), len(exs), len(globs), len(outs)

    def body(*refs):
        rv = [r[0].astype(F32) for r in refs[:n_r]]
        ev = [e[0] for e in refs[n_r:n_r + n_e]]
        gv = [g[...] for g in refs[n_r + n_e:n_r + n_e + n_g]]
        o_refs = refs[n_r + n_e + n_g:n_r + n_e + n_g + n_o]
        a_refs = refs[n_r + n_e + n_g + n_o:]
        pieces, sums = f(rv, ev, gv)
        idx = 0
        for o_ref, (_, ws) in zip(o_refs, outs):
            _store_pieces(o_ref, pieces[idx:idx + len(ws)], ws)
            idx += len(ws)
        if accs:
            @pl.when((pl.program_id(0) == 0) & (pl.program_id(1) == 0))
            def _():
                for a_ref in a_refs:
                    a_ref[...] = jnp.zeros_like(a_ref)
            for a_ref, val in zip(a_refs, sums):
                a_ref[...] += val

    out_specs = [pl.BlockSpec((1, ts, sum(ws)), lambda b, s: (b, s, 0)) for (_, ws) in outs]
    out_specs += [pl.BlockSpec((1, w), lambda b, s: (0, 0)) for w in accs]
    out_shape = [jax.ShapeDtypeStruct((bsz, s, sum(ws)), dt) for (dt, ws) in outs]
    out_shape += [jax.ShapeDtypeStruct((1, w), F32) for w in accs]
    return pl.pallas_call(
        body, name=name, grid=(bsz, s // ts),
        in_specs=_row_specs(rows, exs, globs, ts), out_specs=out_specs, out_shape=out_shape,
        compiler_params=_params("arbitrary", "arbitrary"),
    )(*[r[0] for r in rows], *exs, *globs)


def rowwise_bwd(name, f, rows, exs, globs, cts, d_groups, *, ts, n_diff, unit_ct=0):
    bsz, s = rows[0][0].shape[:2]
    ts = min(ts, s)
    n_r, n_e, n_g, n_c = len(rows), len(exs), len(globs), len(cts)
    n_d = len(d_groups)

    def body(*refs):
        rv = [r[0].astype(F32) for r in refs[:n_r]]
        ev = [e[0] for e in refs[n_r:n_r + n_e]]
        gv = [g[...] for g in refs[n_r + n_e:n_r + n_e + n_g]]
        base = n_r + n_e + n_g
        c_refs = refs[base:base + n_c]
        d_refs = refs[base + n_c:base + n_c + n_d]
        de_refs = refs[base + n_c + n_d:base + n_c + n_d + n_e]
        dg_refs = refs[base + n_c + n_d + n_e:]
        fixed = rv[n_diff:]
        out, vjp = jax.vjp(lambda r, e, g: f(r + fixed, e, g), rv[:n_diff], ev, gv)
        ct = []
        for c_ref, (_, ws) in zip(c_refs, cts):
            ct += _load_pieces(c_ref, ws)
        ct += [jnp.ones_like(o) for o in out[len(ct):]]
        assert len(ct) == len(out) and len(out) - unit_ct == sum(len(ws) for _, ws in cts), name
        d_r, d_e, d_g = vjp(ct)
        for d_ref, (_, idxs) in zip(d_refs, d_groups):
            _store_pieces(d_ref, [d_r[i] for i in idxs], [rows[i][2] for i in idxs])
        first_s = pl.program_id(1) == 0
        if n_e:
            @pl.when(first_s)
            def _():
                for r in de_refs:
                    r[...] = jnp.zeros_like(r)
            for r, val in zip(de_refs, d_e):
                r[0] += val
        if n_g:
            @pl.when(first_s & (pl.program_id(0) == 0))
            def _():
                for r in dg_refs:
                    r[...] = jnp.zeros_like(r)
            for r, val in zip(dg_refs, d_g):
                r[...] += val

    in_specs = _row_specs(rows, exs, globs, ts)
    in_specs += [pl.BlockSpec((1, ts, sum(ws)), lambda b, s: (b, s, 0)) for (_, ws) in cts]
    out_specs = [pl.BlockSpec((1, ts, sum(rows[i][2] for i in idxs)), lambda b, s: (b, s, 0)) for (_, idxs) in d_groups]
    out_specs += [pl.BlockSpec((1, 1, e.shape[-1]), lambda b, s: (b, 0, 0)) for e in exs]
    out_specs += [pl.BlockSpec((1, g.shape[-1]), lambda b, s: (0, 0)) for g in globs]
    out_shape = [jax.ShapeDtypeStruct((bsz, s, sum(rows[i][2] for i in idxs)), dt) for (dt, idxs) in d_groups]
    out_shape += [jax.ShapeDtypeStruct(e.shape, F32) for e in exs]
    out_shape += [jax.ShapeDtypeStruct(g.shape, F32) for g in globs]
    res = pl.pallas_call(
        body, name=name, grid=(bsz, s // ts),
        in_specs=in_specs, out_specs=out_specs, out_shape=out_shape,
        compiler_params=_params("arbitrary", "arbitrary"),
    )(*[r[0] for r in rows], *exs, *globs, *[c[0] for c in cts])
    return res[:n_d], res[n_d:n_d + n_e], res[n_d + n_e:]


def _full(a):
    return (a, 0, a.shape[-1])


def _view(a, col, w):
    assert col % w == 0
    return (a, col // w, w)


def _layer_norm(z, g, b):
    mu = jnp.mean(z, -1, keepdims=True)
    var = jnp.mean(jnp.square(z - mu), -1, keepdims=True)
    return (z - mu) * lax.rsqrt(var + LN_EPS) * g + b


def _rms_norm(z, g):
    ms = jnp.mean(jnp.square(z), -1, keepdims=True)
    return z * lax.rsqrt(ms + RMS_EPS) * g


def f_mod(rv, ev, gv):
    (x,), (scale, shift) = rv, ev
    return [x * (1.0 + scale) + shift]


def f_mod_with_x(rv, ev, gv):
    return f_mod(rv, ev, gv) + [rv[0]]


def f_ln_mod(rv, ev, gv):
    (x, y), (gate, scale, shift), (g, b) = rv, ev, gv
    xn = _layer_norm(ALPHA * x + (1.0 + gate) * y, g, b)
    return [xn, xn * (1.0 + scale) + shift]


def f_ln_loss(rv, ev, gv):
    (x, y, target), (gate,), (g, b) = rv, ev, gv
    xn = _layer_norm(ALPHA * x + (1.0 + gate) * y, g, b)
    return [0.5 * jnp.mean(jnp.square(xn - target), -1, keepdims=True)]


def _head_spread(width):
    r2 = QK_ROPE // 2
    j = lax.broadcasted_iota(jnp.int32, (LANES, width), 0)
    col = lax.broadcasted_iota(jnp.int32, (LANES, width), 1) % r2
    return (j == col).astype(F32), (j == col + r2).astype(F32)


def f_mla_mid(rv, ev, gv):
    (q_lat, kv_lat, kr, cos, sin), (q_g, kv_g) = rv, gv
    e1, e2 = _head_spread(cos.shape[-1])
    k1, k2 = hdot(kr, e1), hdot(kr, e2)
    return [_rms_norm(q_lat, q_g), _rms_norm(kv_lat, kv_g), k1 * cos - k2 * sin, k1 * sin + k2 * cos]


def rope_tables(positions):
    bsz, s = positions.shape
    r2 = QK_ROPE // 2
    width = MLA_HEADS * r2
    inv = (ROPE_THETA ** (-np.arange(0, QK_ROPE, 2, dtype=np.float32) / QK_ROPE)).astype(np.float32)
    inv = jnp.asarray(np.tile(inv, MLA_HEADS)[None, :])
    ts = min(ROW_TILE, s)

    def body(p_ref, inv_ref, cos_ref, sin_ref):
        ang = p_ref[0].astype(F32) * inv_ref[...]
        cos_ref[0] = jnp.cos(ang)
        sin_ref[0] = jnp.sin(ang)

    spec = pl.BlockSpec((1, ts, width), lambda b, s: (b, s, 0))
    return pl.pallas_call(
        body, name="rope_tables", grid=(bsz, s // ts),
        in_specs=[pl.BlockSpec((1, ts, 1), lambda b, s: (b, s, 0)), pl.BlockSpec((1, width), lambda b, s: (0, 0))],
        out_specs=[spec, spec], out_shape=[jax.ShapeDtypeStruct((bsz, s, width), F32)] * 2,
        compiler_params=_params("arbitrary", "arbitrary"),
    )(positions[:, :, None], inv)


def _attn_weights(q, k, row0):
    scale = (QK_NOPE + QK_ROPE) ** -0.5
    s = _dg(q, k, 1, 1) * scale
    rows = row0 + lax.broadcasted_iota(jnp.int32, s.shape, 0)
    cols = lax.broadcasted_iota(jnp.int32, s.shape, 1)
    s = jnp.where(cols <= rows, s, jnp.finfo(F32).min)
    e = jnp.exp(s - jnp.max(s, -1, keepdims=True))
    return e, jnp.sum(e, -1, keepdims=True), scale


def _attn_probs(q, k, row0):
    e, total, scale = _attn_weights(q, k, row0)
    return e / total, scale


ATTN_PAIR = 2


def attn_fwd(q, k, v, ride=()):
    bsz, h, s, dq = q.shape
    dv = v.shape[-1]
    tq = min(ATTN_TQ, s)
    grid = (bsz, h // ATTN_PAIR, s // tq)
    rd = _Ride(ride)

    def body(*refs):
        (q_ref, k_ref, v_ref), srcs, (o_ref,), outs, sems = rd.split(refs, 3, 1)
        rd.run(srcs, outs, sems, grid)
        for i in range(grid[2]):
            @pl.when(pl.program_id(2) == i)
            def _(i=i):
                kend = (i + 1) * tq
                for e in range(ATTN_PAIR):
                    w, total, _ = _attn_weights(q_ref[0, e], k_ref[0, e, :kend, :], i * tq)
                    o = _dg(w.astype(BF16), v_ref[0, e, :kend, :], 1, 0) / total
                    o_ref[0, :, e * dv:(e + 1) * dv] = o.astype(o_ref.dtype)

    res = pl.pallas_call(
        body, name="attn_fwd", grid=grid,
        in_specs=[pl.BlockSpec((1, ATTN_PAIR, tq, dq), lambda b, h, i: (b, h, i, 0)),
                  pl.BlockSpec((1, ATTN_PAIR, s, dq), lambda b, h, i: (b, h, 0, 0)),
                  pl.BlockSpec((1, ATTN_PAIR, s, dv), lambda b, h, i: (b, h, 0, 0))] + rd.in_specs,
        out_specs=[pl.BlockSpec((1, tq, ATTN_PAIR * dv), lambda b, h, i: (b, i, h))] + rd.out_specs,
        out_shape=[jax.ShapeDtypeStruct((bsz, s, h * dv), BF16)] + rd.out_shape, scratch_shapes=rd.scratch,
        compiler_params=_params("arbitrary", "arbitrary", "arbitrary"),
    )(q, k, v, *rd.srcs)
    return res[0], res[1:]


def attn_bwd(q, k, v, do, ride=()):
    bsz, h, s, dq = q.shape
    dv = v.shape[-1]
    tq = min(ATTN_TQ, s)
    grid = (bsz, h // ATTN_PAIR, s // tq)
    rd = _Ride(ride)

    def body(*refs):
        (q_ref, k_ref, v_ref, do_ref), srcs, (dq_ref, dk_ref, dv_ref), outs, sems = rd.split(refs, 4, 3)
        rd.run(srcs, outs, sems, grid)

        @pl.when(pl.program_id(2) == 0)
        def _():
            dk_ref[...] = jnp.zeros_like(dk_ref)
            dv_ref[...] = jnp.zeros_like(dv_ref)

        for i in range(grid[2]):
            @pl.when(pl.program_id(2) == i)
            def _(i=i):
                kend = (i + 1) * tq
                for e in range(ATTN_PAIR):
                    qv, kv, vv = q_ref[0, e], k_ref[0, e, :kend, :], v_ref[0, e, :kend, :]
                    p, scale = _attn_probs(qv, kv, i * tq)
                    do16 = do_ref[0, :, e * dv:(e + 1) * dv].astype(BF16)
                    dv_ref[0, e, :kend, :] += _dg(p.astype(BF16), do16, 0, 0)
                    dp = _dg(do16, vv, 1, 1)
                    ds = (p * (dp - jnp.sum(dp * p, -1, keepdims=True)) * scale).astype(BF16)
                    dq_ref[0, e] = _dg(ds, kv, 1, 0)
                    dk_ref[0, e, :kend, :] += _dg(ds, qv, 0, 0)

    res = pl.pallas_call(
        body, name="attn_bwd", grid=grid,
        in_specs=[pl.BlockSpec((1, ATTN_PAIR, tq, dq), lambda b, h, i: (b, h, i, 0)),
                  pl.BlockSpec((1, ATTN_PAIR, s, dq), lambda b, h, i: (b, h, 0, 0)),
                  pl.BlockSpec((1, ATTN_PAIR, s, dv), lambda b, h, i: (b, h, 0, 0)),
                  pl.BlockSpec((1, tq, ATTN_PAIR * dv), lambda b, h, i: (b, i, h))] + rd.in_specs,
        out_specs=[pl.BlockSpec((1, ATTN_PAIR, tq, dq), lambda b, h, i: (b, h, i, 0)),
                   pl.BlockSpec((1, ATTN_PAIR, s, dq), lambda b, h, i: (b, h, 0, 0)),
                   pl.BlockSpec((1, ATTN_PAIR, s, dv), lambda b, h, i: (b, h, 0, 0))] + rd.out_specs,
        out_shape=[jax.ShapeDtypeStruct((bsz, h, s, dq), F32), jax.ShapeDtypeStruct((bsz, h, s, dq), F32),
                   jax.ShapeDtypeStruct((bsz, h, s, dv), F32)] + rd.out_shape, scratch_shapes=rd.scratch,
        compiler_params=_params("arbitrary", "arbitrary", "arbitrary"),
    )(q, k, v, do, *rd.srcs)
    return res[0], res[1], res[2], res[3:]


def mla_heads(q, kv, krt, cos, sin):
    bsz, s, _ = q.shape
    nh, n, r2, vd = MLA_HEADS, QK_NOPE, QK_ROPE // 2, V_HEAD
    ts = min(ROW_TILE, s)

    def body(q_ref, kv_ref, kr_ref, cos_ref, sin_ref, qh_ref, kh_ref, vh_ref):
        cos, sin = cos_ref[0], sin_ref[0]
        qv, kvv, kr = q_ref[0], kv_ref[0], kr_ref[0].astype(F32)
        q1, q2 = qv[:, nh * n:nh * (n + r2)], qv[:, nh * (n + r2):]
        qr = [q1 * cos - q2 * sin, q1 * sin + q2 * cos]
        for h in range(nh):
            qh_ref[0, h, :, :n] = qv[:, h * n:(h + 1) * n].astype(BF16)
            kh_ref[0, h, :, :n] = kvv[:, h * n:(h + 1) * n].astype(BF16)
            vh_ref[0, h] = kvv[:, nh * n + h * vd:nh * n + (h + 1) * vd].astype(BF16)
            for j in range(2):
                qh_ref[0, h, :, n + j * r2:n + (j + 1) * r2] = qr[j][:, h * r2:(h + 1) * r2].astype(BF16)
                kh_ref[0, h, :, n + j * r2:n + (j + 1) * r2] = kr[:, (j * nh + h) * r2:(j * nh + h + 1) * r2].astype(BF16)

    def row(a):
        return pl.BlockSpec((1, ts, a.shape[-1]), lambda b, i: (b, i, 0))

    def heads(w):
        return pl.BlockSpec((1, nh, ts, w), lambda b, i: (b, 0, i, 0))

    return pl.pallas_call(
        body, name="mla_heads", grid=(bsz, s // ts), in_specs=[row(q), row(kv), row(krt), row(cos), row(sin)],
        out_specs=[heads(n + 2 * r2), heads(n + 2 * r2), heads(vd)],
        out_shape=[jax.ShapeDtypeStruct((bsz, nh, s, n + 2 * r2), BF16)] * 2 + [jax.ShapeDtypeStruct((bsz, nh, s, vd), BF16)],
        compiler_params=_params("arbitrary", "arbitrary"),
    )(q, kv, krt, cos, sin)


def mla_heads_bwd(dqh, dkh, dvh, cos, sin):
    bsz, nh, s, _ = dqh.shape
    n, r2, vd = QK_NOPE, QK_ROPE // 2, V_HEAD
    ts = min(ROW_TILE, s)

    def body(dq_ref, dk_ref, dv_ref, cos_ref, sin_ref, oq_ref, okv_ref, okr_ref, tq_acc, tkv_acc, rot):
        for h in range(nh):
            tq_acc[:, h * n:(h + 1) * n] = dq_ref[0, h, :, :n]
            tkv_acc[:, h * n:(h + 1) * n] = dk_ref[0, h, :, :n]
            tkv_acc[:, nh * n + h * vd:nh * n + (h + 1) * vd] = dv_ref[0, h]
            for j in range(2):
                rot[j, :, h * r2:(h + 1) * r2] = dq_ref[0, h, :, n + j * r2:n + (j + 1) * r2]
                okr_ref[0, :, (j * nh + h) * r2:(j * nh + h + 1) * r2] = dk_ref[0, h, :, n + j * r2:n + (j + 1) * r2]
        cos, sin = cos_ref[0], sin_ref[0]
        d1, d2 = rot[0], rot[1]
        oq_ref[0, :, :nh * n] = tq_acc[...].astype(BF16)
        oq_ref[0, :, nh * n:nh * (n + r2)] = (d1 * cos + d2 * sin).astype(BF16)
        oq_ref[0, :, nh * (n + r2):] = (d2 * cos - d1 * sin).astype(BF16)
        okv_ref[0] = tkv_acc[...].astype(BF16)

    def row(w):
        return pl.BlockSpec((1, ts, w), lambda b, i: (b, i, 0))

    def heads(w):
        return pl.BlockSpec((1, nh, ts, w), lambda b, i: (b, 0, i, 0))

    return pl.pallas_call(
        body, name="mla_heads_bwd", grid=(bsz, s // ts),
        in_specs=[heads(n + 2 * r2), heads(n + 2 * r2), heads(vd), row(nh * r2), row(nh * r2)],
        out_specs=[row(nh * (n + 2 * r2)), row(nh * (n + vd)), row(2 * nh * r2)],
        out_shape=[jax.ShapeDtypeStruct((bsz, s, nh * (n + 2 * r2)), BF16), jax.ShapeDtypeStruct((bsz, s, nh * (n + vd)), BF16),
                   jax.ShapeDtypeStruct((bsz, s, 2 * nh * r2), F32)],
        scratch_shapes=[pltpu.VMEM((ts, nh * n), F32), pltpu.VMEM((ts, nh * (n + vd)), F32), pltpu.VMEM((2, ts, nh * r2), F32)],
        compiler_params=_params("arbitrary", "arbitrary"),
    )(dqh, dkh, dvh, cos, sin)


def _hgrn_tables(c):
    levels = c.bit_length() - 1
    assert 1 << levels == c
    r = np.arange(c)
    sign, mask = [], []
    for l in range(levels):
        lower = ((r >> l) & 1) == 1
        sign.append(np.broadcast_to(np.where(lower, 1.0, -1.0)[:, None], (c, LANES)))
        mask.append((((r[:, None] ^ r[None, :]) >> l) == 1) & lower[:, None])
    return (jnp.asarray(r[:, None] >= r[None, :], BF16), jnp.asarray(np.stack(sign), F32), jnp.asarray(np.stack(mask), F32))


def _const_specs(tables):
    return [pl.BlockSpec(t.shape, lambda b, h, i, nd=t.ndim: (0,) * nd) for t in tables]


def _split3(x):
    hi = x.astype(BF16)
    rest = x - hi.astype(F32)
    mid = rest.astype(BF16)
    return hi, mid, (rest - mid.astype(F32)).astype(BF16)


@functools.partial(jax.custom_vjp, nondiff_argnums=(2,))
def prefix_sums(p, g, n):
    k = g.shape[1]
    r = _dg(p, jnp.concatenate(_split3(g), axis=1), 1, 0)
    r = r[:, :k] + r[:, k:2 * k] + r[:, 2 * k:]
    c = r.shape[0] // n
    return tuple(r[i * c:(i + 1) * c] for i in range(n))


def _prefix_fwd(p, g, n):
    return prefix_sums(p, g, n), p


def _prefix_bwd(n, p, ct):
    ct = jnp.concatenate(ct, axis=0)
    k = ct.shape[1]
    r = _dg(p, jnp.concatenate(_split3(ct), axis=1), 0, 0)
    return jnp.zeros_like(p), r[:, :k] + r[:, k:2 * k] + r[:, 2 * k:]


prefix_sums.defvjp(_prefix_fwd, _prefix_bwd)


@functools.partial(jax.custom_vjp, nondiff_argnums=(1, 2))
def block_row(x, size, row):
    c, k = x.shape
    x3 = x.reshape(c // size, size, k)
    return jnp.broadcast_to(x3[:, row:row + 1, :], x3.shape).reshape(c, k)


def _block_row_fwd(x, size, row):
    return block_row(x, size, row), None


def _block_row_bwd(size, row, _, ct):
    c, k = ct.shape
    ct3 = ct.reshape(c // size, size, k)
    total = jnp.broadcast_to(jnp.sum(ct3, axis=1, keepdims=True), ct3.shape)
    rows = lax.broadcasted_iota(jnp.int32, ct3.shape, 1)
    return (jnp.where(rows == row, total, 0.0).reshape(c, k),)


block_row.defvjp(_block_row_fwd, _block_row_bwd)


def _hgrn_chunk(q, g, k, v, st0, prefix, sign, mask):
    levels = len(mask)
    (b,) = prefix_sums(prefix, g, 1)
    o = bdot(q * jnp.exp(b), st0, 1, 1)
    att = None
    for l in range(levels):
        e = jnp.exp((b - block_row(b, 2 << l, (1 << l) - 1)) * sign[l])
        a = bdot(q * e, k * e, 1, 1) * mask[l]
        att = a if att is None else att + a
    o = o + bdot(att, v, 1, 0) + jnp.sum(q * k, -1, keepdims=True) * v
    total = jnp.sum(g, 0, keepdims=True)
    st1 = st0 * jnp.exp(total) + bdot(v, k * jnp.exp(total - b), 0, 0)
    return o, st1


def _hgrn_step(q_raw, fx, v, g_raw, st0, lb, gn, prefix, sign, mask):
    f = lb + (1.0 - lb) * jax.nn.sigmoid(fx)
    o, st1 = _hgrn_chunk(jax.nn.silu(q_raw), jnp.log(f), 1.0 - f, v, st0, prefix, sign, mask)
    return _rms_norm(o, gn) * jax.nn.silu(g_raw), st1


def _hgrn_layout(proj):
    bsz, s, width = proj.shape
    kd = HGRN_EXPAND
    c = min(HGRN_CHUNK, s)
    nh = width // (4 * kd)
    hp = math.gcd(nh, HGRN_HEADS_PER_STEP)
    return bsz, s, kd, c, nh, s // c, hp


def hgrn_fwd(proj, lb, gn, ride=()):
    bsz, s, kd, c, nh, nc, hp = _hgrn_layout(proj)
    groups = nh // hp
    tables = _hgrn_tables(c)
    levels = tables[2].shape[0]
    grid = (groups, bsz, nc)
    rd = _Ride(ride)

    def body(*refs):
        ((q_ref, f_ref, v_ref, g_ref, lb_ref, gn_ref, p_ref, sg_ref, mk_ref), srcs, (z_ref, st_ref), outs,
         scratch) = rd.split(refs, 9, 2)
        state = scratch[0]
        rd.run(srcs, outs, scratch, grid)

        @pl.when(pl.program_id(2) == 0)
        def _():
            state[...] = jnp.zeros_like(state)

        prefix, sign, mask = p_ref[...], [sg_ref[l] for l in range(levels)], [mk_ref[l] for l in range(levels)]
        for j in range(hp):
            cols = slice(j * kd, (j + 1) * kd)
            st0 = state[j]
            st_ref[0, j, 0] = st0
            z, st1 = _hgrn_step(q_ref[0, :, cols], f_ref[0, :, cols], v_ref[0, :, cols], g_ref[0, :, cols], st0,
                                lb_ref[:, cols], gn_ref[...], prefix, sign, mask)
            z_ref[0, :, cols] = z.astype(z_ref.dtype)
            state[j] = st1

    def part(k):
        return pl.BlockSpec((1, c, hp * kd), lambda h, b, i: (b, i, k * groups + h))

    res = pl.pallas_call(
        body, name="hgrn_fwd", grid=grid,
        in_specs=[part(0), part(1), part(2), part(3), pl.BlockSpec((1, hp * kd), lambda h, b, i: (0, h)),
                  pl.BlockSpec((1, kd), lambda h, b, i: (0, 0))] + _const_specs(tables) + rd.in_specs,
        out_specs=[part(0), pl.BlockSpec((1, hp, 1, kd, kd), lambda h, b, i: (b, h, i, 0, 0))] + rd.out_specs,
        out_shape=[jax.ShapeDtypeStruct((bsz, s, nh * kd), BF16), jax.ShapeDtypeStruct((bsz, nh, nc, kd, kd), F32)] + rd.out_shape,
        scratch_shapes=[pltpu.VMEM((hp, kd, kd), F32)] + rd.scratch,
        compiler_params=_params("arbitrary", "arbitrary", "arbitrary"),
    )(proj, proj, proj, proj, lb, gn, *tables, *rd.srcs)
    return res[0], res[1], res[2:]


def hgrn_bwd(proj, states, dz, lb, gn, ride=()):
    bsz, s, kd, c, nh, nc, hp = _hgrn_layout(proj)
    groups = nh // hp
    tables = _hgrn_tables(c)
    levels = tables[2].shape[0]
    grid = (groups, bsz, nc)
    rd = _Ride(ride)

    def body(*refs):
        ((q_ref, f_ref, v_ref, g_ref, st_ref, dz_ref, lb_ref, gn_ref, p_ref, sg_ref, mk_ref), srcs,
         (dq_ref, df_ref, dv_ref, dg_ref, dlb_ref, dgn_ref), outs, scratch) = rd.split(refs, 11, 6)
        dstate = scratch[0]
        rd.run(srcs, outs, scratch, grid)
        first_of_group = (pl.program_id(1) == 0) & (pl.program_id(2) == 0)

        @pl.when(pl.program_id(2) == 0)
        def _():
            dstate[...] = jnp.zeros_like(dstate)

        @pl.when(first_of_group)
        def _():
            dlb_ref[...] = jnp.zeros_like(dlb_ref)

        @pl.when(first_of_group & (pl.program_id(0) == 0))
        def _():
            dgn_ref[...] = jnp.zeros_like(dgn_ref)

        prefix, sign, mask = p_ref[...], [sg_ref[l] for l in range(levels)], [mk_ref[l] for l in range(levels)]
        for j in range(hp):
            cols = slice(j * kd, (j + 1) * kd)
            _, vjp = jax.vjp(lambda q, f, v, g, st, lb, gn: _hgrn_step(q, f, v, g, st, lb, gn, prefix, sign, mask),
                             q_ref[0, :, cols], f_ref[0, :, cols], v_ref[0, :, cols], g_ref[0, :, cols], st_ref[0, j, 0],
                             lb_ref[:, cols], gn_ref[...])
            dq, df, dv, dg, dst, dlb, dgn = vjp((dz_ref[0, :, cols], dstate[j]))
            dq_ref[0, :, cols] = dq.astype(dq_ref.dtype)
            df_ref[0, :, cols] = df.astype(df_ref.dtype)
            dv_ref[0, :, cols] = dv.astype(dv_ref.dtype)
            dg_ref[0, :, cols] = dg.astype(dg_ref.dtype)
            dlb_ref[:, cols] += dlb
            dgn_ref[...] += dgn
            dstate[j] = dst

    def part(k):
        return pl.BlockSpec((1, c, hp * kd), lambda h, b, i: (b, nc - 1 - i, k * groups + h))

    shape = jax.ShapeDtypeStruct((bsz, s, nh * kd), BF16)
    lb_spec = pl.BlockSpec((1, hp * kd), lambda h, b, i: (0, h))
    gn_spec = pl.BlockSpec((1, kd), lambda h, b, i: (0, 0))
    res = pl.pallas_call(
        body, name="hgrn_bwd", grid=grid,
        in_specs=[part(0), part(1), part(2), part(3), pl.BlockSpec((1, hp, 1, kd, kd), lambda h, b, i: (b, h, nc - 1 - i, 0, 0)),
                  part(0), lb_spec, gn_spec] + _const_specs(tables) + rd.in_specs,
        out_specs=[part(0)] * 4 + [lb_spec, gn_spec] + rd.out_specs,
        out_shape=[shape] * 4 + [jax.ShapeDtypeStruct(lb.shape, F32), jax.ShapeDtypeStruct(gn.shape, F32)] + rd.out_shape,
        scratch_shapes=[pltpu.VMEM((hp, kd, kd), F32)] + rd.scratch,
        compiler_params=_params("arbitrary", "arbitrary", "arbitrary"),
    )(proj, proj, proj, proj, states, dz, lb, gn, *tables, *rd.srcs)
    return list(res[:4]), res[4], res[5], res[6:]


def cast_bf16(name, w):
    blk = pl.BlockSpec((1,) + w.shape[1:], lambda l: (l, 0, 0))

    def body(w_ref, o_ref):
        o_ref[...] = w_ref[...].astype(BF16)

    return pl.pallas_call(body, name=name, grid=(w.shape[0],), in_specs=[blk], out_specs=blk,
                          out_shape=jax.ShapeDtypeStruct(w.shape, BF16), compiler_params=_params("arbitrary"))(w)


def _lower_bounds(rows):
    m = functools.reduce(jnp.maximum, rows)
    e = [jnp.exp(r - m) for r in rows]
    z = functools.reduce(lambda a, b: a + b, e)
    soft = [x / z for x in e]
    out, run = [], jnp.zeros_like(rows[0])
    for sft in soft:
        run = run + sft
        out.append(run - soft[0])
    return out


def lower_bounds(lb):
    n = lb.shape[0]

    def body(lb_ref, o_ref):
        for i, r in enumerate(_lower_bounds([lb_ref[i:i + 1, :] for i in range(n)])):
            o_ref[i:i + 1, :] = r

    return pl.pallas_call(body, name="lower_bounds", out_shape=jax.ShapeDtypeStruct(lb.shape, F32),
                          compiler_params=_params())(lb)


def ada_fwd(c_all, ada_w, ada_b):
    nl, ns, d, cols = ada_w.shape
    n_ex = c_all.shape[0]

    def body(c_ref, w_ref, b_ref, o_ref):
        a = jax.nn.silu(c_ref[...]).astype(BF16)
        o_ref[0] = _dg(a, w_ref[0].astype(BF16), 1, 0) + b_ref[0]

    return pl.pallas_call(
        body, name="ada_fwd", grid=(nl * ns,),
        in_specs=[pl.BlockSpec((n_ex, d), lambda i: (0, 0)), pl.BlockSpec((1, d, cols), lambda i: (i, 0, 0)),
                  pl.BlockSpec((1, 1, cols), lambda i: (i, 0, 0))],
        out_specs=pl.BlockSpec((1, n_ex, cols), lambda i: (i, 0, 0)),
        out_shape=jax.ShapeDtypeStruct((nl * ns, n_ex, cols), F32), compiler_params=_params("arbitrary"),
    )(c_all, ada_w.reshape(nl * ns, d, cols), ada_b.reshape(nl * ns, 1, cols))


def ada_bwd(c_all, dmod):
    n, n_ex, cols = dmod.shape
    d = c_all.shape[1]

    def body(c_ref, g_ref, dw_ref, db_ref):
        a = jax.nn.silu(c_ref[...]).astype(BF16)
        g = g_ref[0]
        dw_ref[0] = _dg(a, g.astype(BF16), 0, 0)
        db_ref[0] = jnp.sum(g, 0, keepdims=True)

    return pl.pallas_call(
        body, name="ada_bwd", grid=(n,),
        in_specs=[pl.BlockSpec((n_ex, d), lambda i: (0, 0)), pl.BlockSpec((1, n_ex, cols), lambda i: (i, 0, 0))],
        out_specs=[pl.BlockSpec((1, d, cols), lambda i: (i, 0, 0)), pl.BlockSpec((1, 1, cols), lambda i: (i, 0, 0))],
        out_shape=[jax.ShapeDtypeStruct((n, d, cols), F32), jax.ShapeDtypeStruct((n, 1, cols), F32)],
        compiler_params=_params("arbitrary"),
    )(c_all, dmod)


def _adam_math(g, w, m, v):
    m = ADAM_B1 * m + (1.0 - ADAM_B1) * g
    v = ADAM_B2 * v + (1.0 - ADAM_B2) * jnp.square(g)
    m_hat = m / (1.0 - ADAM_B1 ** ADAM_STEP)
    v_hat = v / (1.0 - ADAM_B2 ** ADAM_STEP)
    delta = -ADAM_LR * (m_hat / (jnp.sqrt(v_hat) + ADAM_EPS) + ADAM_WD * w)
    return delta, m, v


def adam(name, gstack, w, m, v):
    shape = w.shape
    n, cols = gstack.shape[0], shape[-1]
    rows = math.prod(shape[:-1])
    tr = _pick_rows(rows, max(8, (2 * 1024 * 1024) // (4 * cols * n)))

    def body(g_ref, w_ref, m_ref, v_ref, go_ref, d_ref, mo_ref, vo_ref):
        g = g_ref[0].astype(F32)
        for i in range(1, n):
            g = g + g_ref[i].astype(F32)
        delta, m1, v1 = _adam_math(g, w_ref[...], m_ref[...], v_ref[...])
        go_ref[...] = g
        d_ref[...] = delta
        mo_ref[...] = m1
        vo_ref[...] = v1

    blk = pl.BlockSpec((tr, cols), lambda i: (i, 0))
    out = pl.pallas_call(
        body, name=name, grid=(rows // tr,),
        in_specs=[pl.BlockSpec((n, tr, cols), lambda i: (0, i, 0)), blk, blk, blk],
        out_specs=[blk] * 4, out_shape=[jax.ShapeDtypeStruct((rows, cols), F32)] * 4,
        compiler_params=_params("arbitrary"),
    )(gstack.reshape(n, rows, cols), w.reshape(rows, cols), m.reshape(rows, cols), v.reshape(rows, cols))
    return [o.reshape(shape) for o in out]


def adam_layers(name, gs, w, m, v):
    shape = w.shape
    nl, n, cols = len(gs), gs[0].shape[0], shape[-1]
    rows = math.prod(shape[1:-1])
    tr = _pick_rows(rows, max(16, (2 * 1024 * 1024) // (4 * cols * n)))
    nt = rows // tr

    def body(*refs):
        g_refs, (w_ref, m_ref, v_ref, go_ref, d_ref, mo_ref, vo_ref) = refs[:nl], refs[nl:]
        for j in range(nl):
            @pl.when(pl.program_id(0) == j)
            def _(j=j):
                g = g_refs[j][0].astype(F32)
                for i in range(1, n):
                    g = g + g_refs[j][i].astype(F32)
                delta, m1, v1 = _adam_math(g, w_ref[...], m_ref[...], v_ref[...])
                go_ref[...] = g
                d_ref[...] = delta
                mo_ref[...] = m1
                vo_ref[...] = v1

    def g_spec(j):
        return pl.BlockSpec((n, tr, cols), lambda l, i: (0, jnp.where(l == j, i, jnp.where(l < j, 0, nt - 1)), 0))

    blk = pl.BlockSpec((tr, cols), lambda l, i: (l * nt + i, 0))
    out = pl.pallas_call(
        body, name=name, grid=(nl, nt),
        in_specs=[g_spec(j) for j in range(nl)] + [blk, blk, blk],
        out_specs=[blk] * 4, out_shape=[jax.ShapeDtypeStruct((nl * rows, cols), F32)] * 4,
        compiler_params=_params("arbitrary", "arbitrary"),
    )(*[g.reshape(n, rows, cols) for g in gs], w.reshape(nl * rows, cols), m.reshape(nl * rows, cols), v.reshape(nl * rows, cols))
    return [o.reshape(shape) for o in out]


def adam_lb(gstack, lb, m, v):
    n, nl = gstack.shape[0], lb.shape[0]

    def body(g_ref, w_ref, m_ref, v_ref, go_ref, d_ref, mo_ref, vo_ref):
        rows = [w_ref[i:i + 1, :] for i in range(nl)]
        ct = []
        for i in range(nl):
            g = g_ref[0, i:i + 1, :]
            for j in range(1, n):
                g = g + g_ref[j, i:i + 1, :]
            ct.append(g)
        _, vjp = jax.vjp(_lower_bounds, rows)
        (grads,) = vjp(ct)
        for i in range(nl):
            delta, m1, v1 = _adam_math(grads[i], rows[i], m_ref[i:i + 1, :], v_ref[i:i + 1, :])
            go_ref[i:i + 1, :] = grads[i]
            d_ref[i:i + 1, :] = delta
            mo_ref[i:i + 1, :] = m1
            vo_ref[i:i + 1, :] = v1

    return pl.pallas_call(body, name="adam_hgrn_lb", out_shape=[jax.ShapeDtypeStruct(lb.shape, F32)] * 4,
                          compiler_params=_params())(gstack, lb, m, v)


class _Ride:
    def __init__(self, items):
        self.items = list(items)
        n = len(self.items)
        self.srcs = [src for src, _ in self.items]
        self.in_specs = [pl.BlockSpec(memory_space=pl.ANY)] * n
        self.out_specs = [pl.BlockSpec(memory_space=pl.ANY)] * n
        self.out_shape = [jax.ShapeDtypeStruct(((N_DEV,) + s.shape) if mode == "gather" else s.shape, s.dtype)
                          for s, mode in self.items]
        self.scratch = [pltpu.SemaphoreType.DMA((n, N_DEV - 1)), pltpu.SemaphoreType.DMA((n, N_DEV - 1)),
                        pltpu.SemaphoreType.DMA((n,))] if n else []

    def split(self, refs, n_in, n_out):
        n = len(self.items)
        a, b = n_in + n, n_in + 2 * n + n_out
        return refs[:n_in], refs[n_in:a], refs[a:a + n_out], refs[a + n_out:b], refs[b:]

    def _plan(self, srcs, outs, sems):
        send_sems, recv_sems, local_sems = sems
        x, y, c = lax.axis_index("x"), lax.axis_index("y"), lax.axis_index("c")
        me = 4 * x + 2 * y + c

        def remote(src, dst, i, k, dev):
            return pltpu.make_async_remote_copy(src_ref=src, dst_ref=dst, send_sem=send_sems.at[i, k], recv_sem=recv_sems.at[i, k],
                                                device_id=dev, device_id_type=pl.DeviceIdType.MESH)

        first, relays, final, final_send = [], [], [], []
        for i, (_, mode) in enumerate(self.items):
            if mode == "a2a":
                copies = [pltpu.make_async_copy(srcs[i].at[me], outs[i].at[me], local_sems.at[i])]
                for p in range(1, N_DEV):
                    px = 1 - x if p & 4 else x
                    py = 1 - y if p & 2 else y
                    pc = 1 - c if p & 1 else c
                    copies.append(remote(srcs[i].at[4 * px + 2 * py + pc], outs[i].at[me], i, p - 1, (px, py, pc)))
                first += copies
                final += copies
                continue
            mine = outs[i].at[me]
            own = [pltpu.make_async_copy(srcs[i], mine, local_sems.at[i]), remote(srcs[i], mine, i, 0, (x, y, 1 - c))]
            first += own
            final += own
            for j, (px, py) in enumerate([(1 - x, y), (x, 1 - y), (1 - x, 1 - y)]):
                theirs = outs[i].at[4 * px + 2 * py + c]
                over_ici = remote(srcs[i], mine, i, 1 + j, (px, py, c))
                relay = remote(theirs, theirs, i, 4 + j, (x, y, 1 - c))
                first.append(over_ici)
                relays.append((over_ici, relay))
                final.append(relay)
                final_send.append(over_ici)
        return first, relays, final, final_send

    def run(self, srcs, outs, scratch, grid=()):
        if not self.items:
            return
        sems = scratch[len(scratch) - 3:]

        def start():
            for cp in self._plan(srcs, outs, sems)[0]:
                cp.start()

        def relay():
            for arrival, onward in self._plan(srcs, outs, sems)[1]:
                arrival.wait_recv()
                onward.start()

        def finish():
            _, _, final, final_send = self._plan(srcs, outs, sems)
            for cp in final_send:
                cp.wait_send()
            for cp in final:
                cp.wait()

        if not grid:
            start()
            relay()
            finish()
            return
        total = math.prod(grid)
        step = pl.program_id(0)
        for a in range(1, len(grid)):
            step = step * grid[a] + pl.program_id(a)
        pl.when(step == 0)(start)
        if any(mode == "gather" for _, mode in self.items):
            pl.when(step == min(total - 1, (3 * total) // 4))(relay)
        pl.when(step == total - 1)(finish)


def exchange(name, items):
    rd = _Ride(items)

    def body(*refs):
        _, srcs, _, outs, scratch = rd.split(refs, 0, 0)
        rd.run(srcs, outs, scratch)

    return pl.pallas_call(body, name=name, in_specs=rd.in_specs, out_specs=rd.out_specs, out_shape=rd.out_shape,
                          scratch_shapes=rd.scratch)(*rd.srcs)


def _from_gather(name, g):
    if name in COL_SHARDED:
        _, k, n = g.shape
        return g.transpose(1, 0, 2).reshape(k, N_DEV * n)
    return g.reshape(-1, g.shape[-1])


def _to_slabs(name, w):
    if isinstance(w, tuple):
        per = N_DEV // len(w)
        return jnp.concatenate([p.reshape(p.shape[0], per, p.shape[1] // per).transpose(1, 0, 2) for p in w], axis=0)
    k, n = w.shape
    if name in COL_SHARDED:
        return w.reshape(k, N_DEV, n // N_DEV).transpose(1, 0, 2)
    return w.reshape(N_DEV, k // N_DEV, n)


def _w_in_internal(w):
    return jnp.pad(w, ((0, 0), (0, LANES - QK_ROPE)))


def _qb_internal(w, inverse=False):
    h, n, r2 = MLA_HEADS, QK_NOPE, QK_ROPE // 2
    lead = w.shape[:-1]
    if not inverse:
        w = w.reshape(lead + (h, n + 2 * r2))
        parts = [w[..., :n], w[..., n:n + r2], w[..., n + r2:]]
        return jnp.concatenate([p.reshape(lead + (-1,)) for p in parts], axis=-1)
    parts = [w[..., :h * n].reshape(lead + (h, n)), w[..., h * n:h * (n + r2)].reshape(lead + (h, r2)),
             w[..., h * (n + r2):].reshape(lead + (h, r2))]
    return jnp.concatenate(parts, axis=-1).reshape(lead + (-1,))


def _kvb_internal(w, inverse=False):
    h, n, vd = MLA_HEADS, QK_NOPE, V_HEAD
    lead = w.shape[:-1]
    if not inverse:
        w = w.reshape(lead + (h, n + vd))
        return jnp.concatenate([w[..., :n].reshape(lead + (-1,)), w[..., n:].reshape(lead + (-1,))], axis=-1)
    parts = [w[..., :h * n].reshape(lead + (h, n)), w[..., h * n:].reshape(lead + (h, vd))]
    return jnp.concatenate(parts, axis=-1).reshape(lead + (-1,))


def _mla_forward(h, w, tabs, ride=()):
    cos, sin = tabs
    r2 = MLA_HEADS * (QK_ROPE // 2)
    proj = mm3("mla_proj", h, w['w_in'])
    qn, kvn, krt = rowwise(
        "mla_mid", lambda rv, ev, gv: (f_mla_mid(rv, ev, gv), []),
        [_view(proj, 0, Q_LORA), _view(proj, Q_LORA, KV_LORA), _view(proj, Q_LORA + KV_LORA, LANES), _full(cos), _full(sin)],
        [], [w['q_norm'], w['kv_norm']], [(BF16, [Q_LORA]), (BF16, [KV_LORA]), (BF16, [r2, r2])], ts=ROW_TILE)
    q = mm3("mla_q", qn, w['w_qb'])
    kv = mm3("mla_kv", kvn, w['w_kvb'])
    qh, kh, vh = mla_heads(q, kv, krt, cos, sin)
    o, got = attn_fwd(qh, kh, vh, ride)
    y = mm3("mla_out", o, w['w_o'])
    return y, dict(h=h, proj=proj, qn=qn, kvn=kvn, qh=qh, kh=kh, vh=vh, o=o), got


def _mla_backward(dy, sv, w, tabs, ride=()):
    cos, sin = tabs
    r2 = MLA_HEADS * (QK_ROPE // 2)
    g = {}
    g['w_o'] = wgrad("mla_out_dw", sv['o'], dy)
    do = mm3("mla_out_dx", dy, w['w_o'], tb=True)
    dqh, dkh, dvh, got = attn_bwd(sv['qh'], sv['kh'], sv['vh'], do, ride)
    dq, dkv, dkrt = mla_heads_bwd(dqh, dkh, dvh, cos, sin)
    g['w_qb'] = wgrad("mla_q_dw", sv['qn'], dq)
    g['w_kvb'] = wgrad("mla_kv_dw", sv['kvn'], dkv)
    dqn = mm3("mla_q_dx", dq, w['w_qb'], tb=True)
    dkvn = mm3("mla_kv_dx", dkv, w['w_kvb'], tb=True)
    proj = sv['proj']
    (dproj,), _, (g['q_norm'], g['kv_norm']) = rowwise_bwd(
        "mla_mid_bwd", f_mla_mid,
        [_view(proj, 0, Q_LORA), _view(proj, Q_LORA, KV_LORA), _view(proj, Q_LORA + KV_LORA, LANES), _full(cos), _full(sin)],
        [], [w['q_norm'], w['kv_norm']], [(dqn, [Q_LORA]), (dkvn, [KV_LORA]), (dkrt, [r2, r2])],
        [(BF16, [0, 1, 2])], ts=ROW_TILE, n_diff=3)
    g['w_in'] = wgrad("mla_proj_dw", sv['h'], dproj)
    dh = mm3("mla_proj_dx", dproj, w['w_in'], tb=True)
    return dh, g, got


def _hgrn_forward(h, w, ride=()):
    proj = mm3("hgrn_proj", h, w['w_in'])
    z, states, got = hgrn_fwd(proj, w['lb'], w['g_norm'], ride)
    y = mm3("hgrn_out", z, w['w_o'])
    return y, dict(h=h, proj=proj, states=states, z=z), got


def _hgrn_backward(dy, sv, w, ride=()):
    g = {}
    g['w_o'] = wgrad("hgrn_out_dw", sv['z'], dy)
    dz = mm3("hgrn_out_dx", dy, w['w_o'], tb=True)
    dparts, g['lb'], g['g_norm'], got = hgrn_bwd(sv['proj'], sv['states'], dz, w['lb'], w['g_norm'], ride)
    g['w_in'] = tuple(wgrad("hgrn_proj_dw", sv['h'], p) for p in dparts)
    dh = mm3("hgrn_proj_dx", dparts, w['w_in'], tb=True)
    return dh, g, got


def _ffn_forward(h, w, ride=()):
    (ug, uu, a), got = ffn_in_act(h, w['w_in'], ride)
    y = mm3("ffn_out", a, w['w_out'])
    return y, dict(h=h, ug=ug, uu=uu, a=a), got


def _ffn_backward(dy, sv, w):
    g = {}
    g['w_out'] = wgrad("ffn_out_dw", sv['a'], dy)
    dug, duu = ffn_out_dx_act(dy, w['w_out'], sv['ug'], sv['uu'])
    g['w_in'] = (wgrad("ffn_in_dw", sv['h'], dug), wgrad("ffn_in_dw", sv['h'], duu))
    dh = mm3("ffn_in_dx", [dug, duu], w['w_in'], tb=True, tk_cap=1408)
    return dh, g


def kernel(x, c, positions, mla_w_in, mla_q_norm, mla_w_qb, mla_kv_norm, mla_w_kvb, mla_w_o, hgrn_lb, hgrn_w_in, hgrn_g_norm, hgrn_w_o, ffn_w_in, ffn_w_out, ada_w, ada_b, ln_g, ln_b, loss_target, m_mla_w_in, m_mla_q_norm, m_mla_w_qb, m_mla_kv_norm, m_mla_w_kvb, m_mla_w_o, m_hgrn_lb, m_hgrn_w_in, m_hgrn_g_norm, m_hgrn_w_o, m_ffn_w_in, m_ffn_w_out, m_ada_w, m_ada_b, m_ln_g, m_ln_b, v_mla_w_in, v_mla_q_norm, v_mla_w_qb, v_mla_kv_norm, v_mla_w_kvb, v_mla_w_o, v_hgrn_lb, v_hgrn_w_in, v_hgrn_g_norm, v_hgrn_w_o, v_ffn_w_in, v_ffn_w_out, v_ada_w, v_ada_b, v_ln_g, v_ln_b):
    W = dict(zip(WEIGHTS, (mla_w_in, mla_q_norm, mla_w_qb, mla_kv_norm, mla_w_kvb, mla_w_o, hgrn_lb, hgrn_w_in, hgrn_g_norm,
                           hgrn_w_o, ffn_w_in, ffn_w_out, ada_w, ada_b, ln_g, ln_b)))
    M1 = dict(zip(WEIGHTS, (m_mla_w_in, m_mla_q_norm, m_mla_w_qb, m_mla_kv_norm, m_mla_w_kvb, m_mla_w_o, m_hgrn_lb, m_hgrn_w_in,
                            m_hgrn_g_norm, m_hgrn_w_o, m_ffn_w_in, m_ffn_w_out, m_ada_w, m_ada_b, m_ln_g, m_ln_b)))
    M2 = dict(zip(WEIGHTS, (v_mla_w_in, v_mla_q_norm, v_mla_w_qb, v_mla_kv_norm, v_mla_w_kvb, v_mla_w_o, v_hgrn_lb, v_hgrn_w_in,
                            v_hgrn_g_norm, v_hgrn_w_o, v_ffn_w_in, v_ffn_w_out, v_ada_w, v_ada_b, v_ln_g, v_ln_b)))
    bsz, seq, d = x.shape
    depth, n_mla, n_hgrn = ffn_w_in.shape[0], mla_w_in.shape[0], hgrn_w_in.shape[0]
    n_sub = 2 * depth

    big = COL_SHARDED + ROW_SHARDED
    wb = {n: cast_bf16("cast_" + n, W[n]) for n in big}

    def mixer_names(layer):
        mixer = ['mla_w_in', 'mla_w_qb', 'mla_w_kvb', 'mla_w_o'] if layer % 2 == 0 else ['hgrn_w_in', 'hgrn_w_o']
        return [(n, layer // 2) for n in mixer]

    def carried(layer):
        return [('ffn_w_in', layer), ('ffn_w_out', layer)] + (mixer_names(layer + 1) if layer + 1 < depth else [])

    def carried_fwd(k):
        layer = k // 2
        if k % 2 == 0:
            return [('ffn_w_in', layer), ('ffn_w_out', layer)]
        return mixer_names(layer + 1) if layer + 1 < depth else []

    def weight_items(names):
        return [(wb[n][j], "gather") for n, j in names]

    G = {}
    internal = {'mla_w_in': _w_in_internal, 'mla_w_qb': _qb_internal, 'mla_w_kvb': _kvb_internal}

    def take_weights(names, got):
        for (n, j), a in zip(names, got):
            G[n, j] = internal.get(n, lambda w: w)(_from_gather(n, a))

    lower_shard = lower_bounds(hgrn_lb)
    got = exchange("gather_first", [(lower_shard, "gather"), (ln_g, "gather"), (ln_b, "gather"), (c, "gather")]
                   + weight_items(mixer_names(0)))
    lower_all = got[0].transpose(1, 0, 2).reshape(n_hgrn, -1)
    ln_g_all = got[1].transpose(1, 2, 0, 3).reshape(depth, 2, d)
    ln_b_all = got[2].transpose(1, 2, 0, 3).reshape(depth, 2, d)
    c_all = got[3].reshape(N_DEV * bsz, d)
    take_weights(mixer_names(0), got[4:])

    cols = ada_w.shape[-1]
    mod_loc = ada_fwd(c_all, ada_w, ada_b)
    (mod_got,) = exchange("scatter_mod", [(mod_loc.reshape(n_sub, N_DEV, bsz, cols).transpose(1, 0, 2, 3), "a2a")])
    mod = mod_got.transpose(1, 2, 0, 3).reshape(n_sub, bsz, 1, 3 * d)
    shift = [mod[k, :, :, 0:d] for k in range(n_sub)]
    scale = [mod[k, :, :, d:2 * d] for k in range(n_sub)]
    gate = [mod[k, :, :, 2 * d:] for k in range(n_sub)]
    lng = [ln_g_all[k // 2, k % 2][None, :] for k in range(n_sub)]
    lnb = [ln_b_all[k // 2, k % 2][None, :] for k in range(n_sub)]

    tabs = rope_tables(positions)

    def sub_weights(k):
        layer, j = k // 2, k // 4
        if k % 2:
            return 'ffn', layer, dict(w_in=G['ffn_w_in', layer], w_out=G['ffn_w_out', layer])
        if layer % 2 == 0:
            return 'mla', j, dict(w_in=G['mla_w_in', j], q_norm=mla_q_norm[j][None, :], w_qb=G['mla_w_qb', j],
                                  kv_norm=mla_kv_norm[j][None, :], w_kvb=G['mla_w_kvb', j], w_o=G['mla_w_o', j])
        return 'hgrn', j, dict(w_in=G['hgrn_w_in', j], lb=lower_all[j][None, :], g_norm=hgrn_g_norm[j][None, :],
                               w_o=G['hgrn_w_o', j])

    (h,) = rowwise("mod_first", lambda rv, ev, gv: (f_mod(rv, ev, gv), []), [_full(x)], [scale[0], shift[0]], [],
                   [(BF16, [d])], ts=ROW_TILE)
    xs, ys, saved = [x], [], []
    loss_acc = None
    for k in range(n_sub):
        kind, _, w = sub_weights(k)
        ride = weight_items(carried_fwd(k))
        if kind == 'ffn':
            y, sv, got = _ffn_forward(h, w, ride)
        elif kind == 'mla':
            y, sv, got = _mla_forward(h, w, tabs, ride)
        else:
            y, sv, got = _hgrn_forward(h, w, ride)
        take_weights(carried_fwd(k), got)
        ys.append(y)
        saved.append(sv)
        if k + 1 < n_sub:
            xn, h = rowwise("ln_mod", lambda rv, ev, gv: (f_ln_mod(rv, ev, gv), []), [_full(xs[k]), _full(y)],
                            [gate[k], scale[k + 1], shift[k + 1]], [lng[k], lnb[k]], [(F32, [d]), (BF16, [d])], ts=ROW_TILE)
            xs.append(xn)
        else:
            def loss_rows(rv, ev, gv):
                (row,) = f_ln_loss(rv, ev, gv)
                return [], [jnp.broadcast_to(jnp.sum(row, keepdims=True), (1, LANES))]
            (loss_acc,) = rowwise("ln_loss", loss_rows, [_full(xs[k]), _full(y), _full(loss_target)], [gate[k]], [lng[k], lnb[k]],
                                  [], ts=ROW_TILE, accs=[LANES])
    loss = lax.psum(loss_acc[0, 0], ("x", "y", "c"))

    d_shift, d_scale, d_gate = [None] * n_sub, [None] * n_sub, [None] * n_sub
    d_lng, d_lnb = [None] * n_sub, [None] * n_sub
    part = {n: [None] * W[n].shape[0] for n in ['mla_q_norm', 'mla_kv_norm', 'hgrn_g_norm']}
    recv = {n: [None] * W[n].shape[0] for n in big}
    d_lower = [None] * n_hgrn
    k = n_sub - 1
    (dx, dy), (d_gate[k],), (d_lng[k], d_lnb[k]) = rowwise_bwd(
        "ln_loss_bwd", f_ln_loss, [_full(xs[k]), _full(ys[k]), _full(loss_target)], [gate[k]], [lng[k], lnb[k]], [],
        [(F32, [0]), (BF16, [1])], ts=ROW_TILE, n_diff=2, unit_ct=1)
    grad_x = None
    mine = {}

    def take_grads(names, got):
        for (n, j), a in zip(names, got):
            recv[n][j] = a

    def grad_items(names):
        return [(_to_slabs(n, mine[n, jj]), "a2a") for n, jj in names]

    for k in range(n_sub - 1, -1, -1):
        kind, j, w = sub_weights(k)
        ride = grad_items(carried(k // 2)) if kind != 'ffn' else []
        if kind == 'ffn':
            dh, g = _ffn_backward(dy, saved[k], w)
            new = {('ffn_w_in', j): g['w_in'], ('ffn_w_out', j): g['w_out']}
        elif kind == 'mla':
            dh, g, got = _mla_backward(dy, saved[k], w, tabs, ride)
            new = {('mla_w_in', j): g['w_in'][:, :mla_w_in.shape[-1] * N_DEV], ('mla_w_qb', j): _qb_internal(g['w_qb'], inverse=True),
                   ('mla_w_kvb', j): _kvb_internal(g['w_kvb'], inverse=True), ('mla_w_o', j): g['w_o']}
            part['mla_q_norm'][j], part['mla_kv_norm'][j] = g['q_norm'][0], g['kv_norm'][0]
        else:
            dh, g, got = _hgrn_backward(dy, saved[k], w, ride)
            new = {('hgrn_w_in', j): g['w_in'], ('hgrn_w_o', j): g['w_o']}
            part['hgrn_g_norm'][j] = g['g_norm'][0]
            d_lower[j] = g['lb'][0]
        if kind != 'ffn':
            take_grads(carried(k // 2), got)
        mine.update(new)
        if k:
            (dx, dy), (d_gate[k - 1], d_scale[k], d_shift[k]), (d_lng[k - 1], d_lnb[k - 1]) = rowwise_bwd(
                "ln_mod_bwd", f_ln_mod, [_full(xs[k - 1]), _full(ys[k - 1])], [gate[k - 1], scale[k], shift[k]],
                [lng[k - 1], lnb[k - 1]], [(dx, [d]), (dh, [d])], [(F32, [0]), (BF16, [1])], ts=ROW_TILE, n_diff=2)
        else:
            (grad_x,), (d_scale[0], d_shift[0]), _ = rowwise_bwd(
                "mod_first_bwd", f_mod_with_x, [_full(x)], [scale[0], shift[0]], [], [(dh, [d]), (dx, [d])],
                [(F32, [0])], ts=ROW_TILE, n_diff=1)

    waiting = mixer_names(0)
    slabs = grad_items(waiting)
    slabs.append((jnp.stack(d_lower).reshape(n_hgrn, N_DEV, -1).transpose(1, 0, 2), "a2a"))
    for parts in (d_lng, d_lnb):
        full = jnp.stack([p[0] for p in parts]).reshape(depth, 2, N_DEV, d // N_DEV)
        slabs.append((full.transpose(2, 0, 1, 3), "a2a"))
    dmod = jnp.concatenate([jnp.stack(d_shift), jnp.stack(d_scale), jnp.stack(d_gate)], axis=-1)
    slabs.append((dmod.reshape(n_sub, bsz, N_DEV, cols).transpose(2, 0, 1, 3), "a2a"))
    small = ['mla_q_norm', 'mla_kv_norm', 'hgrn_g_norm']
    slabs += [(jnp.stack(part[n]), "gather") for n in small]
    got = exchange("scatter_last", slabs)
    take_grads(waiting, got)
    stacks = dict(zip(['hgrn_lb', 'ln_g', 'ln_b', 'dmod'] + small, got[len(waiting):]))

    dmod_all = stacks['dmod'].transpose(1, 0, 2, 3).reshape(n_sub, N_DEV * bsz, cols)
    g_ada_w, g_ada_b = ada_bwd(c_all, dmod_all)
    stacks['ada_w'] = g_ada_w.reshape((1,) + ada_w.shape)
    stacks['ada_b'] = g_ada_b.reshape((1,) + ada_b.shape)

    res = {}
    for n in WEIGHTS:
        if n == 'hgrn_lb':
            res[n] = adam_lb(stacks[n], W[n], M1[n], M2[n])
        elif n in big:
            res[n] = adam_layers("adam_" + n, recv[n], W[n], M1[n], M2[n])
        else:
            res[n] = adam("adam_" + n, stacks[n], W[n], M1[n], M2[n])
    return (loss, grad_x, *[res[n][0] for n in WEIGHTS], *[res[n][1] for n in WEIGHTS], *[res[n][2] for n in WEIGHTS],
            *[res[n][3] for n in WEIGHTS])
```

```python
import functools
import math

import numpy as np
import jax
import jax.numpy as jnp
from jax import lax
from jax.experimental import pallas as pl
from jax.experimental.pallas import tpu as pltpu

F32 = jnp.float32
BF16 = jnp.bfloat16

N_DEV = 8
LANES = 128
VMEM_LIMIT = 52 * 1024 * 1024

D_MODEL = 1024
DEPTH = 4
MLA_HEADS = 16
QK_NOPE = 64
QK_ROPE = 32
V_HEAD = 64
Q_LORA = 768
KV_LORA = 256
ROPE_THETA = 10000.0
HGRN_EXPAND = 128
HGRN_CHUNK = 128
HGRN_HEADS_PER_STEP = 8
D_FF = 2816
ALPHA = (2.0 * DEPTH) ** 0.25
LN_EPS = 1e-5
RMS_EPS = 1e-6
ADAM_LR = 0.001
ADAM_B1 = 0.9
ADAM_B2 = 0.999
ADAM_EPS = 1e-08
ADAM_WD = 0.01
ADAM_STEP = 10

ATTN_TQ = 512
ROW_TILE = 256

WEIGHTS = ['mla_w_in', 'mla_q_norm', 'mla_w_qb', 'mla_kv_norm', 'mla_w_kvb', 'mla_w_o', 'hgrn_lb', 'hgrn_w_in',
           'hgrn_g_norm', 'hgrn_w_o', 'ffn_w_in', 'ffn_w_out', 'ada_w', 'ada_b', 'ln_g', 'ln_b']
COL_SHARDED = ['mla_w_in', 'mla_w_qb', 'mla_w_kvb', 'hgrn_w_in', 'ffn_w_in']
ROW_SHARDED = ['mla_w_o', 'hgrn_w_o', 'ffn_w_out']


def _params(*sem):
    if sem:
        return pltpu.CompilerParams(dimension_semantics=sem, vmem_limit_bytes=VMEM_LIMIT)
    return pltpu.CompilerParams(vmem_limit_bytes=VMEM_LIMIT)


def _pick(n, cap):
    best = None
    for t in range(LANES, min(n, cap) + 1, LANES):
        if n % t == 0:
            best = t
    return best or n


def _pick_rows(n, cap):
    best = None
    for t in range(8, min(n, cap) + 1, 8):
        if n % t == 0:
            best = t
    return best or n


def matmul(name, a, b, *, ta=False, tb=False, out_dtype=F32, tm_cap=1536, tn_cap=1536, tk_cap=2816):
    parts = list(a) if isinstance(a, (list, tuple)) else [a]
    n_parts = len(parts)
    assert n_parts == 1 or not ta
    (kp, m) = parts[0].shape if ta else parts[0].shape[::-1]
    (n, k2) = b.shape if tb else b.shape[::-1]
    assert kp * n_parts == k2, (name, parts[0].shape, b.shape)
    tm, tn, tk = _pick(m, tm_cap), _pick(n, tn_cap), _pick(kp, tk_cap)
    nkp = kp // tk
    nk = nkp * n_parts
    dims = (((0 if ta else 1,), (1 if tb else 0,)), ((), ()))

    def prod(a_ref, b_ref):
        return lax.dot_general(a_ref[...].astype(BF16), b_ref[...].astype(BF16), dims, preferred_element_type=F32)

    if nk == 1:
        def body(a_ref, b_ref, o_ref):
            o_ref[...] = prod(a_ref, b_ref).astype(o_ref.dtype)
        scratch = []
    else:
        def body(*refs):
            a_refs, (b_ref, o_ref, acc_ref) = refs[:n_parts], refs[n_parts:]
            k = pl.program_id(2)

            @pl.when(k == 0)
            def _():
                acc_ref[...] = jnp.zeros_like(acc_ref)

            if n_parts == 1:
                acc_ref[...] += prod(a_refs[0], b_ref)
            else:
                for p in range(n_parts):
                    @pl.when((k >= p * nkp) & (k < (p + 1) * nkp))
                    def _(p=p):
                        acc_ref[...] += prod(a_refs[p], b_ref)

            @pl.when(k == nk - 1)
            def _():
                o_ref[...] = acc_ref[...].astype(o_ref.dtype)
        scratch = [pltpu.VMEM((tm, tn), F32)]

    if ta:
        a_specs = [pl.BlockSpec((tk, tm), lambda i, j, k: (k, i))]
    elif n_parts == 1:
        a_specs = [pl.BlockSpec((tm, tk), lambda i, j, k: (i, k))]
    else:
        a_specs = [pl.BlockSpec((tm, tk), lambda i, j, k, p=p: (i, jnp.clip(k - p * nkp, 0, nkp - 1))) for p in range(n_parts)]
    b_spec = pl.BlockSpec((tn, tk), lambda i, j, k: (j, k)) if tb else pl.BlockSpec((tk, tn), lambda i, j, k: (k, j))
    return pl.pallas_call(
        body, name=name, grid=(m // tm, n // tn, nk),
        in_specs=a_specs + [b_spec], out_specs=pl.BlockSpec((tm, tn), lambda i, j, k: (i, j)),
        out_shape=jax.ShapeDtypeStruct((m, n), out_dtype), scratch_shapes=scratch,
        compiler_params=_params("parallel", "parallel", "arbitrary"),
    )(*parts, b)


def mm3(name, a3, w, **kw):
    parts = list(a3) if isinstance(a3, (list, tuple)) else [a3]
    bsz, s = parts[0].shape[:2]
    flat = [p.reshape(bsz * s, p.shape[-1]) for p in parts]
    out = matmul(name, flat if len(flat) > 1 else flat[0], w, **kw)
    return out.reshape(bsz, s, out.shape[-1])


def ffn_in_act(h3, w_in, ride=()):
    bsz, s, k = h3.shape
    m, dff = bsz * s, w_in.shape[1] // 2
    tm, tn = _pick(m, 512), _pick(dff, 1536)
    nj = dff // tn
    grid = (nj, m // tm)
    rd = _Ride(ride)

    def body(*refs):
        (h_ref, wg_ref, wu_ref), srcs, (ug_ref, uu_ref, a_ref), outs, scratch = rd.split(refs, 3, 3)
        rd.run(srcs, outs, scratch, grid)
        hv = h_ref[...].astype(BF16)
        ug = _dg(hv, wg_ref[...].astype(BF16), 1, 0)
        uu = _dg(hv, wu_ref[...].astype(BF16), 1, 0)
        ug_ref[...] = ug.astype(ug_ref.dtype)
        uu_ref[...] = uu.astype(uu_ref.dtype)
        a_ref[...] = (jax.nn.silu(ug) * uu).astype(a_ref.dtype)

    out = pl.BlockSpec((tm, tn), lambda j, i: (i, j))
    res = pl.pallas_call(
        body, name="ffn_in_act", grid=grid,
        in_specs=[pl.BlockSpec((tm, k), lambda j, i: (i, 0)), pl.BlockSpec((k, tn), lambda j, i: (0, j)),
                  pl.BlockSpec((k, tn), lambda j, i: (0, nj + j))] + rd.in_specs,
        out_specs=[out, out, out] + rd.out_specs,
        out_shape=[jax.ShapeDtypeStruct((m, dff), BF16)] * 3 + rd.out_shape, scratch_shapes=rd.scratch,
        compiler_params=_params("arbitrary", "arbitrary"),
    )(h3.reshape(m, k), w_in, w_in, *rd.srcs)
    return [r.reshape(bsz, s, dff) for r in res[:3]], res[3:]


def ffn_out_dx_act(dy3, w_out, ug, uu):
    bsz, s, d = dy3.shape
    m, dff = bsz * s, w_out.shape[0]
    tm, tn = _pick(m, 512), _pick(dff, 1536)

    def body(dy_ref, w_ref, ug_ref, uu_ref, dg_ref, du_ref):
        da = _dg(dy_ref[...].astype(BF16), w_ref[...].astype(BF16), 1, 1)
        _, vjp = jax.vjp(lambda gate, up: jax.nn.silu(gate) * up, ug_ref[...].astype(F32), uu_ref[...].astype(F32))
        dg, du = vjp(da)
        dg_ref[...] = dg.astype(dg_ref.dtype)
        du_ref[...] = du.astype(du_ref.dtype)

    blk = pl.BlockSpec((tm, tn), lambda j, i: (i, j))
    res = pl.pallas_call(
        body, name="ffn_out_dx_act", grid=(dff // tn, m // tm),
        in_specs=[pl.BlockSpec((tm, d), lambda j, i: (i, 0)), pl.BlockSpec((tn, d), lambda j, i: (j, 0)), blk, blk],
        out_specs=[blk, blk], out_shape=[jax.ShapeDtypeStruct((m, dff), BF16)] * 2,
        compiler_params=_params("arbitrary", "arbitrary"),
    )(dy3.reshape(m, d), w_out, ug.reshape(m, dff), uu.reshape(m, dff))
    return [r.reshape(bsz, s, dff) for r in res]


def wgrad(name, a3, g3):
    bsz, s, k = a3.shape
    return matmul(name, a3.reshape(bsz * s, k), g3.reshape(bsz * s, g3.shape[-1]), ta=True, out_dtype=BF16)


def _dg(a, b, ca, cb, **kw):
    return lax.dot_general(a, b, (((ca,), (cb,)), ((), ())), preferred_element_type=F32, **kw)


@functools.partial(jax.custom_vjp, nondiff_argnums=(2, 3))
def bdot(a, b, ca, cb):
    return _dg(a.astype(BF16), b.astype(BF16), ca, cb)


def _bdot_fwd(a, b, ca, cb):
    return bdot(a, b, ca, cb), (a, b)


def _bdot_bwd(ca, cb, res, g):
    a, b = res
    a16, b16, g16 = a.astype(BF16), b.astype(BF16), g.astype(BF16)
    if ca == 1:
        da = _dg(g16, b16, 1, 1 if cb == 0 else 0)
    else:
        da = _dg(b16, g16, 1 if cb == 0 else 0, 1)
    if cb == 0:
        db = _dg(a16, g16, 0 if ca == 1 else 1, 0)
    else:
        db = _dg(g16, a16, 0, 0 if ca == 1 else 1)
    return da, db


bdot.defvjp(_bdot_fwd, _bdot_bwd)


def hdot(a, b, ca=1, cb=0):
    return _dg(a, b, ca, cb, precision=lax.Precision.HIGHEST)


def _row_specs(rows, exs, globs, ts):
    specs = [pl.BlockSpec((1, ts, w), lambda b, s, j=j: (b, s, j)) for (_, j, w) in rows]
    specs += [pl.BlockSpec((1, 1, e.shape[-1]), lambda b, s: (b, 0, 0)) for e in exs]
    specs += [pl.BlockSpec((1, g.shape[-1]), lambda b, s: (0, 0)) for g in globs]
    return specs


def _store_pieces(o_ref, pieces, widths):
    off = 0
    for p, w in zip(pieces, widths):
        o_ref[0, :, off:off + w] = p.astype(o_ref.dtype)
        off += w


def _load_pieces(c_ref, widths):
    out, off = [], 0
    for w in widths:
        out.append(c_ref[0, :, off:off + w].astype(F32))
        off += w
    return out


def rowwise(name, f, rows, exs, globs, outs, *, ts, accs=()):
    bsz, s = rows[0][0].shape[:2]
    ts = min(ts, s)
    n_r, n_e, n_g, n_o = len(rows), len(exs), len(globs), len(outs)

    def body(*refs):
        rv = [r[0].astype(F32) for r in refs[:n_r]]
        ev = [e[0] for e in refs[n_r:n_r + n_e]]
        gv = [g[...] for g in refs[n_r + n_e:n_r + n_e + n_g]]
        o_refs = refs[n_r + n_e + n_g:n_r + n_e + n_g + n_o]
        a_refs = refs[n_r + n_e + n_g + n_o:]
        pieces, sums = f(rv, ev, gv)
        idx = 0
        for o_ref, (_, ws) in zip(o_refs, outs):
            _store_pieces(o_ref, pieces[idx:idx + len(ws)], ws)
            idx += len(ws)
        if accs:
            @pl.when((pl.program_id(0) == 0) & (pl.program_id(1) == 0))
            def _():
                for a_ref in a_refs:
                    a_ref[...] = jnp.zeros_like(a_ref)
            for a_ref, val in zip(a_refs, sums):
                a_ref[...] += val

    out_specs = [pl.BlockSpec((1, ts, sum(ws)), lambda b, s: (b, s, 0)) for (_, ws) in outs]
    out_specs += [pl.BlockSpec((1, w), lambda b, s: (0, 0)) for w in accs]
    out_shape = [jax.ShapeDtypeStruct((bsz, s, sum(ws)), dt) for (dt, ws) in outs]
    out_shape += [jax.ShapeDtypeStruct((1, w), F32) for w in accs]
    return pl.pallas_call(
        body, name=name, grid=(bsz, s // ts),
        in_specs=_row_specs(rows, exs, globs, ts), out_specs=out_specs, out_shape=out_shape,
        compiler_params=_params("arbitrary", "arbitrary"),
    )(*[r[0] for r in rows], *exs, *globs)


def rowwise_bwd(name, f, rows, exs, globs, cts, d_groups, *, ts, n_diff, unit_ct=0):
    bsz, s = rows[0][0].shape[:2]
    ts = min(ts, s)
    n_r, n_e, n_g, n_c = len(rows), len(exs), len(globs), len(cts)
    n_d = len(d_groups)

    def body(*refs):
        rv = [r[0].astype(F32) for r in refs[:n_r]]
        ev = [e[0] for e in refs[n_r:n_r + n_e]]
        gv = [g[...] for g in refs[n_r + n_e:n_r + n_e + n_g]]
        base = n_r + n_e + n_g
        c_refs = refs[base:base + n_c]
        d_refs = refs[base + n_c:base + n_c + n_d]
        de_refs = refs[base + n_c + n_d:base + n_c + n_d + n_e]
        dg_refs = refs[base + n_c + n_d + n_e:]
        fixed = rv[n_diff:]
        out, vjp = jax.vjp(lambda r, e, g: f(r + fixed, e, g), rv[:n_diff], ev, gv)
        ct = []
        for c_ref, (_, ws) in zip(c_refs, cts):
            ct += _load_pieces(c_ref, ws)
        ct += [jnp.ones_like(o) for o in out[len(ct):]]
        assert len(ct) == len(out) and len(out) - unit_ct == sum(len(ws) for _, ws in cts), name
        d_r, d_e, d_g = vjp(ct)
        for d_ref, (_, idxs) in zip(d_refs, d_groups):
            _store_pieces(d_ref, [d_r[i] for i in idxs], [rows[i][2] for i in idxs])
        first_s = pl.program_id(1) == 0
        if n_e:
            @pl.when(first_s)
            def _():
                for r in de_refs:
                    r[...] = jnp.zeros_like(r)
            for r, val in zip(de_refs, d_e):
                r[0] += val
        if n_g:
            @pl.when(first_s & (pl.program_id(0) == 0))
            def _():
                for r in dg_refs:
                    r[...] = jnp.zeros_like(r)
            for r, val in zip(dg_refs, d_g):
                r[...] += val

    in_specs = _row_specs(rows, exs, globs, ts)
    in_specs += [pl.BlockSpec((1, ts, sum(ws)), lambda b, s: (b, s, 0)) for (_, ws) in cts]
    out_specs = [pl.BlockSpec((1, ts, sum(rows[i][2] for i in idxs)), lambda b, s: (b, s, 0)) for (_, idxs) in d_groups]
    out_specs += [pl.BlockSpec((1, 1, e.shape[-1]), lambda b, s: (b, 0, 0)) for e in exs]
    out_specs += [pl.BlockSpec((1, g.shape[-1]), lambda b, s: (0, 0)) for g in globs]
    out_shape = [jax.ShapeDtypeStruct((bsz, s, sum(rows[i][2] for i in idxs)), dt) for (dt, idxs) in d_groups]
    out_shape += [jax.ShapeDtypeStruct(e.shape, F32) for e in exs]
    out_shape += [jax.ShapeDtypeStruct(g.shape, F32) for g in globs]
    res = pl.pallas_call(
        body, name=name, grid=(bsz, s // ts),
        in_specs=in_specs, out_specs=out_specs, out_shape=out_shape,
        compiler_params=_params("arbitrary", "arbitrary"),
    )(*[r[0] for r in rows], *exs, *globs, *[c[0] for c in cts])
    return res[:n_d], res[n_d:n_d + n_e], res[n_d + n_e:]


def _full(a):
    return (a, 0, a.shape[-1])


def _view(a, col, w):
    assert col % w == 0
    return (a, col // w, w)


def _layer_norm(z, g, b):
    mu = jnp.mean(z, -1, keepdims=True)
    var = jnp.mean(jnp.square(z - mu), -1, keepdims=True)
    return (z - mu) * lax.rsqrt(var + LN_EPS) * g + b


def _rms_norm(z, g):
    ms = jnp.mean(jnp.square(z), -1, keepdims=True)
    return z * lax.rsqrt(ms + RMS_EPS) * g


def f_mod(rv, ev, gv):
    (x,), (scale, shift) = rv, ev
    return [x * (1.0 + scale) + shift]


def f_mod_with_x(rv, ev, gv):
    return f_mod(rv, ev, gv) + [rv[0]]


def f_ln_mod(rv, ev, gv):
    (x, y), (gate, scale, shift), (g, b) = rv, ev, gv
    xn = _layer_norm(ALPHA * x + (1.0 + gate) * y, g, b)
    return [xn, xn * (1.0 + scale) + shift]


def f_ln_loss(rv, ev, gv):
    (x, y, target), (gate,), (g, b) = rv, ev, gv
    xn = _layer_norm(ALPHA * x + (1.0 + gate) * y, g, b)
    return [0.5 * jnp.mean(jnp.square(xn - target), -1, keepdims=True)]


def _head_spread(width):
    r2 = QK_ROPE // 2
    j = lax.broadcasted_iota(jnp.int32, (LANES, width), 0)
    col = lax.broadcasted_iota(jnp.int32, (LANES, width), 1) % r2
    return (j == col).astype(F32), (j == col + r2).astype(F32)


def f_mla_mid(rv, ev, gv):
    (q_lat, kv_lat, kr, cos, sin), (q_g, kv_g) = rv, gv
    e1, e2 = _head_spread(cos.shape[-1])
    k1, k2 = hdot(kr, e1), hdot(kr, e2)
    return [_rms_norm(q_lat, q_g), _rms_norm(kv_lat, kv_g), k1 * cos - k2 * sin, k1 * sin + k2 * cos]


def rope_tables(positions):
    bsz, s = positions.shape
    r2 = QK_ROPE // 2
    width = MLA_HEADS * r2
    inv = (ROPE_THETA ** (-np.arange(0, QK_ROPE, 2, dtype=np.float32) / QK_ROPE)).astype(np.float32)
    inv = jnp.asarray(np.tile(inv, MLA_HEADS)[None, :])
    ts = min(ROW_TILE, s)

    def body(p_ref, inv_ref, cos_ref, sin_ref):
        ang = p_ref[0].astype(F32) * inv_ref[...]
        cos_ref[0] = jnp.cos(ang)
        sin_ref[0] = jnp.sin(ang)

    spec = pl.BlockSpec((1, ts, width), lambda b, s: (b, s, 0))
    return pl.pallas_call(
        body, name="rope_tables", grid=(bsz, s // ts),
        in_specs=[pl.BlockSpec((1, ts, 1), lambda b, s: (b, s, 0)), pl.BlockSpec((1, width), lambda b, s: (0, 0))],
        out_specs=[spec, spec], out_shape=[jax.ShapeDtypeStruct((bsz, s, width), F32)] * 2,
        compiler_params=_params("arbitrary", "arbitrary"),
    )(positions[:, :, None], inv)


def _attn_weights(q, k, row0):
    scale = (QK_NOPE + QK_ROPE) ** -0.5
    s = _dg(q, k, 1, 1) * scale
    rows = row0 + lax.broadcasted_iota(jnp.int32, s.shape, 0)
    cols = lax.broadcasted_iota(jnp.int32, s.shape, 1)
    s = jnp.where(cols <= rows, s, jnp.finfo(F32).min)
    e = jnp.exp(s - jnp.max(s, -1, keepdims=True))
    return e, jnp.sum(e, -1, keepdims=True), scale


ATTN_PAIR = 2


def attn_fwd(q, k, v, ride=()):
    bsz, h, s, dq = q.shape
    dv = v.shape[-1]
    tq = min(ATTN_TQ, s)
    grid = (bsz, h // ATTN_PAIR, s // tq)
    rd = _Ride(ride)

    def body(*refs):
        (q_ref, k_ref, v_ref), srcs, (o_ref,), outs, sems = rd.split(refs, 3, 1)
        rd.run(srcs, outs, sems, grid)
        for i in range(grid[2]):
            @pl.when(pl.program_id(2) == i)
            def _(i=i):
                kend = (i + 1) * tq
                for e in range(ATTN_PAIR):
                    w, total, _ = _attn_weights(q_ref[0, e], k_ref[0, e, :kend, :], i * tq)
                    o = _dg(w.astype(BF16), v_ref[0, e, :kend, :], 1, 0) / total
                    o_ref[0, :, e * dv:(e + 1) * dv] = o.astype(o_ref.dtype)

    res = pl.pallas_call(
        body, name="attn_fwd", grid=grid,
        in_specs=[pl.BlockSpec((1, ATTN_PAIR, tq, dq), lambda b, h, i: (b, h, i, 0)),
                  pl.BlockSpec((1, ATTN_PAIR, s, dq), lambda b, h, i: (b, h, 0, 0)),
                  pl.BlockSpec((1, ATTN_PAIR, s, dv), lambda b, h, i: (b, h, 0, 0))] + rd.in_specs,
        out_specs=[pl.BlockSpec((1, tq, ATTN_PAIR * dv), lambda b, h, i: (b, i, h))] + rd.out_specs,
        out_shape=[jax.ShapeDtypeStruct((bsz, s, h * dv), BF16)] + rd.out_shape, scratch_shapes=rd.scratch,
        compiler_params=_params("arbitrary", "arbitrary", "arbitrary"),
    )(q, k, v, *rd.srcs)
    return res[0], res[1:]


def attn_bwd(q, k, v, do, ride=()):
    bsz, h, s, dq = q.shape
    dv = v.shape[-1]
    tq = min(ATTN_TQ, s)
    grid = (bsz, h // ATTN_PAIR, s // tq)
    rd = _Ride(ride)

    def body(*refs):
        (q_ref, k_ref, v_ref, do_ref), srcs, (dq_ref, dk_ref, dv_ref), outs, sems = rd.split(refs, 4, 3)
        rd.run(srcs, outs, sems, grid)

        @pl.when(pl.program_id(2) == 0)
        def _():
            dk_ref[...] = jnp.zeros_like(dk_ref)
            dv_ref[...] = jnp.zeros_like(dv_ref)

        for i in range(grid[2]):
            @pl.when(pl.program_id(2) == i)
            def _(i=i):
                kend = (i + 1) * tq
                for e in range(ATTN_PAIR):
                    qv, kv, vv = q_ref[0, e], k_ref[0, e, :kend, :], v_ref[0, e, :kend, :]
                    w, total, scale = _attn_weights(qv, kv, i * tq)
                    inv = 1.0 / total
                    do_e = do_ref[0, :, e * dv:(e + 1) * dv]
                    dv_ref[0, e, :kend, :] += _dg(w.astype(BF16), (do_e * inv).astype(BF16), 0, 0)
                    dp = _dg(do_e.astype(BF16), vv, 1, 1)
                    delta = jnp.sum(dp * w, -1, keepdims=True) * inv
                    ds = (w * (dp - delta) * (scale * inv)).astype(BF16)
                    dq_ref[0, e] = _dg(ds, kv, 1, 0)
                    dk_ref[0, e, :kend, :] += _dg(ds, qv, 0, 0)

    res = pl.pallas_call(
        body, name="attn_bwd", grid=grid,
        in_specs=[pl.BlockSpec((1, ATTN_PAIR, tq, dq), lambda b, h, i: (b, h, i, 0)),
                  pl.BlockSpec((1, ATTN_PAIR, s, dq), lambda b, h, i: (b, h, 0, 0)),
                  pl.BlockSpec((1, ATTN_PAIR, s, dv), lambda b, h, i: (b, h, 0, 0)),
                  pl.BlockSpec((1, tq, ATTN_PAIR * dv), lambda b, h, i: (b, i, h))] + rd.in_specs,
        out_specs=[pl.BlockSpec((1, ATTN_PAIR, tq, dq), lambda b, h, i: (b, h, i, 0)),
                   pl.BlockSpec((1, ATTN_PAIR, s, dq), lambda b, h, i: (b, h, 0, 0)),
                   pl.BlockSpec((1, ATTN_PAIR, s, dv), lambda b, h, i: (b, h, 0, 0))] + rd.out_specs,
        out_shape=[jax.ShapeDtypeStruct((bsz, h, s, dq), F32), jax.ShapeDtypeStruct((bsz, h, s, dq), F32),
                   jax.ShapeDtypeStruct((bsz, h, s, dv), F32)] + rd.out_shape, scratch_shapes=rd.scratch,
        compiler_params=_params("arbitrary", "arbitrary", "arbitrary"),
    )(q, k, v, do, *rd.srcs)
    return res[0], res[1], res[2], res[3:]


def mla_heads(q, kv, krt, cos, sin):
    bsz, s, _ = q.shape
    nh, n, r2, vd = MLA_HEADS, QK_NOPE, QK_ROPE // 2, V_HEAD
    ts = min(ROW_TILE, s)

    def body(q_ref, kv_ref, kr_ref, cos_ref, sin_ref, qh_ref, kh_ref, vh_ref):
        cos, sin = cos_ref[0], sin_ref[0]
        qv, kvv, kr = q_ref[0], kv_ref[0], kr_ref[0].astype(F32)
        q1, q2 = qv[:, nh * n:nh * (n + r2)], qv[:, nh * (n + r2):]
        qr = [q1 * cos - q2 * sin, q1 * sin + q2 * cos]
        for h in range(nh):
            qh_ref[0, h, :, :n] = qv[:, h * n:(h + 1) * n].astype(BF16)
            kh_ref[0, h, :, :n] = kvv[:, h * n:(h + 1) * n].astype(BF16)
            vh_ref[0, h] = kvv[:, nh * n + h * vd:nh * n + (h + 1) * vd].astype(BF16)
            for j in range(2):
                qh_ref[0, h, :, n + j * r2:n + (j + 1) * r2] = qr[j][:, h * r2:(h + 1) * r2].astype(BF16)
                kh_ref[0, h, :, n + j * r2:n + (j + 1) * r2] = kr[:, (j * nh + h) * r2:(j * nh + h + 1) * r2].astype(BF16)

    def row(a):
        return pl.BlockSpec((1, ts, a.shape[-1]), lambda b, i: (b, i, 0))

    def heads(w):
        return pl.BlockSpec((1, nh, ts, w), lambda b, i: (b, 0, i, 0))

    return pl.pallas_call(
        body, name="mla_heads", grid=(bsz, s // ts), in_specs=[row(q), row(kv), row(krt), row(cos), row(sin)],
        out_specs=[heads(n + 2 * r2), heads(n + 2 * r2), heads(vd)],
        out_shape=[jax.ShapeDtypeStruct((bsz, nh, s, n + 2 * r2), BF16)] * 2 + [jax.ShapeDtypeStruct((bsz, nh, s, vd), BF16)],
        compiler_params=_params("arbitrary", "arbitrary"),
    )(q, kv, krt, cos, sin)


def mla_heads_bwd(dqh, dkh, dvh, cos, sin):
    bsz, nh, s, _ = dqh.shape
    n, r2, vd = QK_NOPE, QK_ROPE // 2, V_HEAD
    ts = min(ROW_TILE, s)

    def body(dq_ref, dk_ref, dv_ref, cos_ref, sin_ref, oq_ref, okv_ref, okr_ref, tq_acc, tkv_acc, rot):
        for h in range(nh):
            tq_acc[:, h * n:(h + 1) * n] = dq_ref[0, h, :, :n]
            tkv_acc[:, h * n:(h + 1) * n] = dk_ref[0, h, :, :n]
            tkv_acc[:, nh * n + h * vd:nh * n + (h + 1) * vd] = dv_ref[0, h]
            for j in range(2):
                rot[j, :, h * r2:(h + 1) * r2] = dq_ref[0, h, :, n + j * r2:n + (j + 1) * r2]
                okr_ref[0, :, (j * nh + h) * r2:(j * nh + h + 1) * r2] = dk_ref[0, h, :, n + j * r2:n + (j + 1) * r2]
        cos, sin = cos_ref[0], sin_ref[0]
        d1, d2 = rot[0], rot[1]
        oq_ref[0, :, :nh * n] = tq_acc[...].astype(BF16)
        oq_ref[0, :, nh * n:nh * (n + r2)] = (d1 * cos + d2 * sin).astype(BF16)
        oq_ref[0, :, nh * (n + r2):] = (d2 * cos - d1 * sin).astype(BF16)
        okv_ref[0] = tkv_acc[...].astype(BF16)

    def row(w):
        return pl.BlockSpec((1, ts, w), lambda b, i: (b, i, 0))

    def heads(w):
        return pl.BlockSpec((1, nh, ts, w), lambda b, i: (b, 0, i, 0))

    return pl.pallas_call(
        body, name="mla_heads_bwd", grid=(bsz, s // ts),
        in_specs=[heads(n + 2 * r2), heads(n + 2 * r2), heads(vd), row(nh * r2), row(nh * r2)],
        out_specs=[row(nh * (n + 2 * r2)), row(nh * (n + vd)), row(2 * nh * r2)],
        out_shape=[jax.ShapeDtypeStruct((bsz, s, nh * (n + 2 * r2)), BF16), jax.ShapeDtypeStruct((bsz, s, nh * (n + vd)), BF16),
                   jax.ShapeDtypeStruct((bsz, s, 2 * nh * r2), F32)],
        scratch_shapes=[pltpu.VMEM((ts, nh * n), F32), pltpu.VMEM((ts, nh * (n + vd)), F32), pltpu.VMEM((2, ts, nh * r2), F32)],
        compiler_params=_params("arbitrary", "arbitrary"),
    )(dqh, dkh, dvh, cos, sin)


def _hgrn_tables(c):
    levels = c.bit_length() - 1
    assert 1 << levels == c
    r = np.arange(c)
    sign, mask = [], []
    for l in range(levels):
        lower = ((r >> l) & 1) == 1
        sign.append(np.broadcast_to(np.where(lower, 1.0, -1.0)[:, None], (c, LANES)))
        mask.append((((r[:, None] ^ r[None, :]) >> l) == 1) & lower[:, None])
    return (jnp.asarray(r[:, None] >= r[None, :], BF16), jnp.asarray(np.stack(sign), F32), jnp.asarray(np.stack(mask), F32))


def _const_specs(tables):
    return [pl.BlockSpec(t.shape, lambda b, h, i, nd=t.ndim: (0,) * nd) for t in tables]


def _split3(x):
    hi = x.astype(BF16)
    rest = x - hi.astype(F32)
    mid = rest.astype(BF16)
    return hi, mid, (rest - mid.astype(F32)).astype(BF16)


@functools.partial(jax.custom_vjp, nondiff_argnums=(2,))
def prefix_sums(p, g, n):
    k = g.shape[1]
    r = _dg(p, jnp.concatenate(_split3(g), axis=1), 1, 0)
    r = r[:, :k] + r[:, k:2 * k] + r[:, 2 * k:]
    c = r.shape[0] // n
    return tuple(r[i * c:(i + 1) * c] for i in range(n))


def _prefix_fwd(p, g, n):
    return prefix_sums(p, g, n), p


def _prefix_bwd(n, p, ct):
    ct = jnp.concatenate(ct, axis=0)
    k = ct.shape[1]
    r = _dg(p, jnp.concatenate(_split3(ct), axis=1), 0, 0)
    return jnp.zeros_like(p), r[:, :k] + r[:, k:2 * k] + r[:, 2 * k:]


prefix_sums.defvjp(_prefix_fwd, _prefix_bwd)


@functools.partial(jax.custom_vjp, nondiff_argnums=(1, 2))
def block_row(x, size, row):
    c, k = x.shape
    x3 = x.reshape(c // size, size, k)
    return jnp.broadcast_to(x3[:, row:row + 1, :], x3.shape).reshape(c, k)


def _block_row_fwd(x, size, row):
    return block_row(x, size, row), None


def _block_row_bwd(size, row, _, ct):
    c, k = ct.shape
    ct3 = ct.reshape(c // size, size, k)
    total = jnp.broadcast_to(jnp.sum(ct3, axis=1, keepdims=True), ct3.shape)
    rows = lax.broadcasted_iota(jnp.int32, ct3.shape, 1)
    return (jnp.where(rows == row, total, 0.0).reshape(c, k),)


block_row.defvjp(_block_row_fwd, _block_row_bwd)


def _hgrn_chunk(q, g, k, v, st0, prefix, sign, mask):
    levels = len(mask)
    (b,) = prefix_sums(prefix, g, 1)
    o = bdot(q * jnp.exp(b), st0, 1, 1)
    att = None
    for l in range(levels):
        e = jnp.exp((b - block_row(b, 2 << l, (1 << l) - 1)) * sign[l])
        a = bdot(q * e, k * e, 1, 1) * mask[l]
        att = a if att is None else att + a
    o = o + bdot(att, v, 1, 0) + jnp.sum(q * k, -1, keepdims=True) * v
    total = jnp.sum(g, 0, keepdims=True)
    st1 = st0 * jnp.exp(total) + bdot(v, k * jnp.exp(total - b), 0, 0)
    return o, st1


def _hgrn_step(q_raw, fx, v, g_raw, st0, lb, gn, prefix, sign, mask):
    f = lb + (1.0 - lb) * jax.nn.sigmoid(fx)
    o, st1 = _hgrn_chunk(jax.nn.silu(q_raw), jnp.log(f), 1.0 - f, v, st0, prefix, sign, mask)
    return _rms_norm(o, gn) * jax.nn.silu(g_raw), st1


def _hgrn_layout(proj):
    bsz, s, width = proj.shape
    kd = HGRN_EXPAND
    c = min(HGRN_CHUNK, s)
    nh = width // (4 * kd)
    hp = math.gcd(nh, HGRN_HEADS_PER_STEP)
    return bsz, s, kd, c, nh, s // c, hp


def hgrn_fwd(proj, lb, gn, ride=()):
    bsz, s, kd, c, nh, nc, hp = _hgrn_layout(proj)
    groups = nh // hp
    tables = _hgrn_tables(c)
    levels = tables[2].shape[0]
    grid = (groups, bsz, nc)
    rd = _Ride(ride)

    def body(*refs):
        ((q_ref, f_ref, v_ref, g_ref, lb_ref, gn_ref, p_ref, sg_ref, mk_ref), srcs, (z_ref, st_ref), outs,
         scratch) = rd.split(refs, 9, 2)
        state = scratch[0]
        rd.run(srcs, outs, scratch, grid)

        @pl.when(pl.program_id(2) == 0)
        def _():
            state[...] = jnp.zeros_like(state)

        prefix, sign, mask = p_ref[...], [sg_ref[l] for l in range(levels)], [mk_ref[l] for l in range(levels)]
        for j in range(hp):
            cols = slice(j * kd, (j + 1) * kd)
            st0 = state[j]
            st_ref[0, j, 0] = st0
            z, st1 = _hgrn_step(q_ref[0, :, cols], f_ref[0, :, cols], v_ref[0, :, cols], g_ref[0, :, cols], st0,
                                lb_ref[:, cols], gn_ref[...], prefix, sign, mask)
            z_ref[0, :, cols] = z.astype(z_ref.dtype)
            state[j] = st1

    def part(k):
        return pl.BlockSpec((1, c, hp * kd), lambda h, b, i: (b, i, k * groups + h))

    res = pl.pallas_call(
        body, name="hgrn_fwd", grid=grid,
        in_specs=[part(0), part(1), part(2), part(3), pl.BlockSpec((1, hp * kd), lambda h, b, i: (0, h)),
                  pl.BlockSpec((1, kd), lambda h, b, i: (0, 0))] + _const_specs(tables) + rd.in_specs,
        out_specs=[part(0), pl.BlockSpec((1, hp, 1, kd, kd), lambda h, b, i: (b, h, i, 0, 0))] + rd.out_specs,
        out_shape=[jax.ShapeDtypeStruct((bsz, s, nh * kd), BF16), jax.ShapeDtypeStruct((bsz, nh, nc, kd, kd), F32)] + rd.out_shape,
        scratch_shapes=[pltpu.VMEM((hp, kd, kd), F32)] + rd.scratch,
        compiler_params=_params("arbitrary", "arbitrary", "arbitrary"),
    )(proj, proj, proj, proj, lb, gn, *tables, *rd.srcs)
    return res[0], res[1], res[2:]


def hgrn_bwd(proj, states, dz, lb, gn, ride=()):
    bsz, s, kd, c, nh, nc, hp = _hgrn_layout(proj)
    groups = nh // hp
    tables = _hgrn_tables(c)
    levels = tables[2].shape[0]
    grid = (groups, bsz, nc)
    rd = _Ride(ride)

    def body(*refs):
        ((q_ref, f_ref, v_ref, g_ref, st_ref, dz_ref, lb_ref, gn_ref, p_ref, sg_ref, mk_ref), srcs,
         (dq_ref, df_ref, dv_ref, dg_ref, dlb_ref, dgn_ref), outs, scratch) = rd.split(refs, 11, 6)
        dstate = scratch[0]
        rd.run(srcs, outs, scratch, grid)
        first_of_group = (pl.program_id(1) == 0) & (pl.program_id(2) == 0)

        @pl.when(pl.program_id(2) == 0)
        def _():
            dstate[...] = jnp.zeros_like(dstate)

        @pl.when(first_of_group)
        def _():
            dlb_ref[...] = jnp.zeros_like(dlb_ref)

        @pl.when(first_of_group & (pl.program_id(0) == 0))
        def _():
            dgn_ref[...] = jnp.zeros_like(dgn_ref)

        prefix, sign, mask = p_ref[...], [sg_ref[l] for l in range(levels)], [mk_ref[l] for l in range(levels)]
        for j in range(hp):
            cols = slice(j * kd, (j + 1) * kd)
            _, vjp = jax.vjp(lambda q, f, v, g, st, lb, gn: _hgrn_step(q, f, v, g, st, lb, gn, prefix, sign, mask),
                             q_ref[0, :, cols], f_ref[0, :, cols], v_ref[0, :, cols], g_ref[0, :, cols], st_ref[0, j, 0],
                             lb_ref[:, cols], gn_ref[...])
            dq, df, dv, dg, dst, dlb, dgn = vjp((dz_ref[0, :, cols], dstate[j]))
            dq_ref[0, :, cols] = dq.astype(dq_ref.dtype)
            df_ref[0, :, cols] = df.astype(df_ref.dtype)
            dv_ref[0, :, cols] = dv.astype(dv_ref.dtype)
            dg_ref[0, :, cols] = dg.astype(dg_ref.dtype)
            dlb_ref[:, cols] += dlb
            dgn_ref[...] += dgn
            dstate[j] = dst

    def part(k):
        return pl.BlockSpec((1, c, hp * kd), lambda h, b, i: (b, nc - 1 - i, k * groups + h))

    shape = jax.ShapeDtypeStruct((bsz, s, nh * kd), BF16)
    lb_spec = pl.BlockSpec((1, hp * kd), lambda h, b, i: (0, h))
    gn_spec = pl.BlockSpec((1, kd), lambda h, b, i: (0, 0))
    res = pl.pallas_call(
        body, name="hgrn_bwd", grid=grid,
        in_specs=[part(0), part(1), part(2), part(3), pl.BlockSpec((1, hp, 1, kd, kd), lambda h, b, i: (b, h, nc - 1 - i, 0, 0)),
                  part(0), lb_spec, gn_spec] + _const_specs(tables) + rd.in_specs,
        out_specs=[part(0)] * 4 + [lb_spec, gn_spec] + rd.out_specs,
        out_shape=[shape] * 4 + [jax.ShapeDtypeStruct(lb.shape, F32), jax.ShapeDtypeStruct(gn.shape, F32)] + rd.out_shape,
        scratch_shapes=[pltpu.VMEM((hp, kd, kd), F32)] + rd.scratch,
        compiler_params=_params("arbitrary", "arbitrary", "arbitrary"),
    )(proj, proj, proj, proj, states, dz, lb, gn, *tables, *rd.srcs)
    return list(res[:4]), res[4], res[5], res[6:]


def cast_bf16(name, w):
    blk = pl.BlockSpec((1,) + w.shape[1:], lambda l: (l, 0, 0))

    def body(w_ref, o_ref):
        o_ref[...] = w_ref[...].astype(BF16)

    return pl.pallas_call(body, name=name, grid=(w.shape[0],), in_specs=[blk], out_specs=blk,
                          out_shape=jax.ShapeDtypeStruct(w.shape, BF16), compiler_params=_params("arbitrary"))(w)


def _lower_bounds(rows):
    m = functools.reduce(jnp.maximum, rows)
    e = [jnp.exp(r - m) for r in rows]
    z = functools.reduce(lambda a, b: a + b, e)
    soft = [x / z for x in e]
    out, run = [], jnp.zeros_like(rows[0])
    for sft in soft:
        run = run + sft
        out.append(run - soft[0])
    return out


def lower_bounds(lb):
    n = lb.shape[0]

    def body(lb_ref, o_ref):
        for i, r in enumerate(_lower_bounds([lb_ref[i:i + 1, :] for i in range(n)])):
            o_ref[i:i + 1, :] = r

    return pl.pallas_call(body, name="lower_bounds", out_shape=jax.ShapeDtypeStruct(lb.shape, F32),
                          compiler_params=_params())(lb)


def ada_fwd(c_all, ada_w, ada_b):
    nl, ns, d, cols = ada_w.shape
    n_ex = c_all.shape[0]

    def body(c_ref, w_ref, b_ref, o_ref):
        a = jax.nn.silu(c_ref[...]).astype(BF16)
        o_ref[0] = _dg(a, w_ref[0].astype(BF16), 1, 0) + b_ref[0]

    return pl.pallas_call(
        body, name="ada_fwd", grid=(nl * ns,),
        in_specs=[pl.BlockSpec((n_ex, d), lambda i: (0, 0)), pl.BlockSpec((1, d, cols), lambda i: (i, 0, 0)),
                  pl.BlockSpec((1, 1, cols), lambda i: (i, 0, 0))],
        out_specs=pl.BlockSpec((1, n_ex, cols), lambda i: (i, 0, 0)),
        out_shape=jax.ShapeDtypeStruct((nl * ns, n_ex, cols), F32), compiler_params=_params("arbitrary"),
    )(c_all, ada_w.reshape(nl * ns, d, cols), ada_b.reshape(nl * ns, 1, cols))


def ada_bwd(c_all, dmod):
    n, n_ex, cols = dmod.shape
    d = c_all.shape[1]

    def body(c_ref, g_ref, dw_ref, db_ref):
        a = jax.nn.silu(c_ref[...]).astype(BF16)
        g = g_ref[0]
        dw_ref[0] = _dg(a, g.astype(BF16), 0, 0)
        db_ref[0] = jnp.sum(g, 0, keepdims=True)

    return pl.pallas_call(
        body, name="ada_bwd", grid=(n,),
        in_specs=[pl.BlockSpec((n_ex, d), lambda i: (0, 0)), pl.BlockSpec((1, n_ex, cols), lambda i: (i, 0, 0))],
        out_specs=[pl.BlockSpec((1, d, cols), lambda i: (i, 0, 0)), pl.BlockSpec((1, 1, cols), lambda i: (i, 0, 0))],
        out_shape=[jax.ShapeDtypeStruct((n, d, cols), F32), jax.ShapeDtypeStruct((n, 1, cols), F32)],
        compiler_params=_params("arbitrary"),
    )(c_all, dmod)


def _adam_math(g, w, m, v):
    m = ADAM_B1 * m + (1.0 - ADAM_B1) * g
    v = ADAM_B2 * v + (1.0 - ADAM_B2) * jnp.square(g)
    m_hat = m / (1.0 - ADAM_B1 ** ADAM_STEP)
    v_hat = v / (1.0 - ADAM_B2 ** ADAM_STEP)
    delta = -ADAM_LR * (m_hat / (jnp.sqrt(v_hat) + ADAM_EPS) + ADAM_WD * w)
    return delta, m, v


def adam(name, gstack, w, m, v):
    shape = w.shape
    n, cols = gstack.shape[0], shape[-1]
    rows = math.prod(shape[:-1])
    tr = _pick_rows(rows, max(8, (2 * 1024 * 1024) // (4 * cols * n)))

    def body(g_ref, w_ref, m_ref, v_ref, go_ref, d_ref, mo_ref, vo_ref):
        g = g_ref[0].astype(F32)
        for i in range(1, n):
            g = g + g_ref[i].astype(F32)
        delta, m1, v1 = _adam_math(g, w_ref[...], m_ref[...], v_ref[...])
        go_ref[...] = g
        d_ref[...] = delta
        mo_ref[...] = m1
        vo_ref[...] = v1

    blk = pl.BlockSpec((tr, cols), lambda i: (i, 0))
    out = pl.pallas_call(
        body, name=name, grid=(rows // tr,),
        in_specs=[pl.BlockSpec((n, tr, cols), lambda i: (0, i, 0)), blk, blk, blk],
        out_specs=[blk] * 4, out_shape=[jax.ShapeDtypeStruct((rows, cols), F32)] * 4,
        compiler_params=_params("arbitrary"),
    )(gstack.reshape(n, rows, cols), w.reshape(rows, cols), m.reshape(rows, cols), v.reshape(rows, cols))
    return [o.reshape(shape) for o in out]


def adam_layers(name, gs, w, m, v):
    shape = w.shape
    nl, n, cols = len(gs), gs[0].shape[0], shape[-1]
    rows = math.prod(shape[1:-1])
    tr = _pick_rows(rows, max(16, (2 * 1024 * 1024) // (4 * cols * n)))
    nt = rows // tr

    def body(*refs):
        g_refs, (w_ref, m_ref, v_ref, go_ref, d_ref, mo_ref, vo_ref) = refs[:nl], refs[nl:]
        for j in range(nl):
            @pl.when(pl.program_id(0) == j)
            def _(j=j):
                g = g_refs[j][0].astype(F32)
                for i in range(1, n):
                    g = g + g_refs[j][i].astype(F32)
                delta, m1, v1 = _adam_math(g, w_ref[...], m_ref[...], v_ref[...])
                go_ref[...] = g
                d_ref[...] = delta
                mo_ref[...] = m1
                vo_ref[...] = v1

    def g_spec(j):
        return pl.BlockSpec((n, tr, cols), lambda l, i: (0, jnp.where(l == j, i, jnp.where(l < j, 0, nt - 1)), 0))

    blk = pl.BlockSpec((tr, cols), lambda l, i: (l * nt + i, 0))
    out = pl.pallas_call(
        body, name=name, grid=(nl, nt),
        in_specs=[g_spec(j) for j in range(nl)] + [blk, blk, blk],
        out_specs=[blk] * 4, out_shape=[jax.ShapeDtypeStruct((nl * rows, cols), F32)] * 4,
        compiler_params=_params("arbitrary", "arbitrary"),
    )(*[g.reshape(n, rows, cols) for g in gs], w.reshape(nl * rows, cols), m.reshape(nl * rows, cols), v.reshape(nl * rows, cols))
    return [o.reshape(shape) for o in out]


def adam_lb(gstack, lb, m, v):
    n, nl = gstack.shape[0], lb.shape[0]

    def body(g_ref, w_ref, m_ref, v_ref, go_ref, d_ref, mo_ref, vo_ref):
        rows = [w_ref[i:i + 1, :] for i in range(nl)]
        ct = []
        for i in range(nl):
            g = g_ref[0, i:i + 1, :]
            for j in range(1, n):
                g = g + g_ref[j, i:i + 1, :]
            ct.append(g)
        _, vjp = jax.vjp(_lower_bounds, rows)
        (grads,) = vjp(ct)
        for i in range(nl):
            delta, m1, v1 = _adam_math(grads[i], rows[i], m_ref[i:i + 1, :], v_ref[i:i + 1, :])
            go_ref[i:i + 1, :] = grads[i]
            d_ref[i:i + 1, :] = delta
            mo_ref[i:i + 1, :] = m1
            vo_ref[i:i + 1, :] = v1

    return pl.pallas_call(body, name="adam_hgrn_lb", out_shape=[jax.ShapeDtypeStruct(lb.shape, F32)] * 4,
                          compiler_params=_params())(gstack, lb, m, v)


class _Ride:
    def __init__(self, items):
        self.items = list(items)
        n = len(self.items)
        self.srcs = [src for src, _ in self.items]
        self.in_specs = [pl.BlockSpec(memory_space=pl.ANY)] * n
        self.out_specs = [pl.BlockSpec(memory_space=pl.ANY)] * n
        self.out_shape = [jax.ShapeDtypeStruct(((N_DEV,) + s.shape) if mode == "gather" else s.shape, s.dtype)
                          for s, mode in self.items]
        self.scratch = [pltpu.SemaphoreType.DMA((n, N_DEV - 1)), pltpu.SemaphoreType.DMA((n, N_DEV - 1)),
                        pltpu.SemaphoreType.DMA((n,))] if n else []

    def split(self, refs, n_in, n_out):
        n = len(self.items)
        a, b = n_in + n, n_in + 2 * n + n_out
        return refs[:n_in], refs[n_in:a], refs[a:a + n_out], refs[a + n_out:b], refs[b:]

    def _plan(self, srcs, outs, sems):
        send_sems, recv_sems, local_sems = sems
        x, y, c = lax.axis_index("x"), lax.axis_index("y"), lax.axis_index("c")
        me = 4 * x + 2 * y + c

        def remote(src, dst, i, k, dev):
            return pltpu.make_async_remote_copy(src_ref=src, dst_ref=dst, send_sem=send_sems.at[i, k], recv_sem=recv_sems.at[i, k],
                                                device_id=dev, device_id_type=pl.DeviceIdType.MESH)

        first, relays, final, final_send = [], [], [], []
        for i, (_, mode) in enumerate(self.items):
            if mode == "a2a":
                copies = [pltpu.make_async_copy(srcs[i].at[me], outs[i].at[me], local_sems.at[i])]
                for p in range(1, N_DEV):
                    px = 1 - x if p & 4 else x
                    py = 1 - y if p & 2 else y
                    pc = 1 - c if p & 1 else c
                    copies.append(remote(srcs[i].at[4 * px + 2 * py + pc], outs[i].at[me], i, p - 1, (px, py, pc)))
                first += copies
                final += copies
                continue
            mine = outs[i].at[me]
            own = [pltpu.make_async_copy(srcs[i], mine, local_sems.at[i]), remote(srcs[i], mine, i, 0, (x, y, 1 - c))]
            first += own
            final += own
            for j, (px, py) in enumerate([(1 - x, y), (x, 1 - y), (1 - x, 1 - y)]):
                theirs = outs[i].at[4 * px + 2 * py + c]
                over_ici = remote(srcs[i], mine, i, 1 + j, (px, py, c))
                relay = remote(theirs, theirs, i, 4 + j, (x, y, 1 - c))
                first.append(over_ici)
                relays.append((over_ici, relay))
                final.append(relay)
                final_send.append(over_ici)
        return first, relays, final, final_send

    def run(self, srcs, outs, scratch, grid=()):
        if not self.items:
            return
        sems = scratch[len(scratch) - 3:]

        def start():
            for cp in self._plan(srcs, outs, sems)[0]:
                cp.start()

        def relay():
            for arrival, onward in self._plan(srcs, outs, sems)[1]:
                arrival.wait_recv()
                onward.start()

        def finish():
            _, _, final, final_send = self._plan(srcs, outs, sems)
            for cp in final_send:
                cp.wait_send()
            for cp in final:
                cp.wait()

        if not grid:
            start()
            relay()
            finish()
            return
        total = math.prod(grid)
        step = pl.program_id(0)
        for a in range(1, len(grid)):
            step = step * grid[a] + pl.program_id(a)
        pl.when(step == 0)(start)
        if any(mode == "gather" for _, mode in self.items):
            pl.when(step == min(total - 1, (3 * total) // 4))(relay)
        pl.when(step == total - 1)(finish)


def exchange(name, items):
    rd = _Ride(items)

    def body(*refs):
        _, srcs, _, outs, scratch = rd.split(refs, 0, 0)
        rd.run(srcs, outs, scratch)

    return pl.pallas_call(body, name=name, in_specs=rd.in_specs, out_specs=rd.out_specs, out_shape=rd.out_shape,
                          scratch_shapes=rd.scratch)(*rd.srcs)


def _from_gather(name, g):
    if name in COL_SHARDED:
        _, k, n = g.shape
        return g.transpose(1, 0, 2).reshape(k, N_DEV * n)
    return g.reshape(-1, g.shape[-1])


def _to_slabs(name, w):
    if isinstance(w, tuple):
        per = N_DEV // len(w)
        return jnp.concatenate([p.reshape(p.shape[0], per, p.shape[1] // per).transpose(1, 0, 2) for p in w], axis=0)
    k, n = w.shape
    if name in COL_SHARDED:
        return w.reshape(k, N_DEV, n // N_DEV).transpose(1, 0, 2)
    return w.reshape(N_DEV, k // N_DEV, n)


def _w_in_internal(w):
    return jnp.pad(w, ((0, 0), (0, LANES - QK_ROPE)))


def _qb_internal(w, inverse=False):
    h, n, r2 = MLA_HEADS, QK_NOPE, QK_ROPE // 2
    lead = w.shape[:-1]
    if not inverse:
        w = w.reshape(lead + (h, n + 2 * r2))
        parts = [w[..., :n], w[..., n:n + r2], w[..., n + r2:]]
        return jnp.concatenate([p.reshape(lead + (-1,)) for p in parts], axis=-1)
    parts = [w[..., :h * n].reshape(lead + (h, n)), w[..., h * n:h * (n + r2)].reshape(lead + (h, r2)),
             w[..., h * (n + r2):].reshape(lead + (h, r2))]
    return jnp.concatenate(parts, axis=-1).reshape(lead + (-1,))


def _kvb_internal(w, inverse=False):
    h, n, vd = MLA_HEADS, QK_NOPE, V_HEAD
    lead = w.shape[:-1]
    if not inverse:
        w = w.reshape(lead + (h, n + vd))
        return jnp.concatenate([w[..., :n].reshape(lead + (-1,)), w[..., n:].reshape(lead + (-1,))], axis=-1)
    parts = [w[..., :h * n].reshape(lead + (h, n)), w[..., h * n:].reshape(lead + (h, vd))]
    return jnp.concatenate(parts, axis=-1).reshape(lead + (-1,))


def _mla_forward(h, w, tabs, ride=()):
    cos, sin = tabs
    r2 = MLA_HEADS * (QK_ROPE // 2)
    proj = mm3("mla_proj", h, w['w_in'])
    qn, kvn, krt = rowwise(
        "mla_mid", lambda rv, ev, gv: (f_mla_mid(rv, ev, gv), []),
        [_view(proj, 0, Q_LORA), _view(proj, Q_LORA, KV_LORA), _view(proj, Q_LORA + KV_LORA, LANES), _full(cos), _full(sin)],
        [], [w['q_norm'], w['kv_norm']], [(BF16, [Q_LORA]), (BF16, [KV_LORA]), (BF16, [r2, r2])], ts=ROW_TILE)
    q = mm3("mla_q", qn, w['w_qb'])
    kv = mm3("mla_kv", kvn, w['w_kvb'])
    qh, kh, vh = mla_heads(q, kv, krt, cos, sin)
    o, got = attn_fwd(qh, kh, vh, ride)
    y = mm3("mla_out", o, w['w_o'])
    return y, dict(h=h, proj=proj, qn=qn, kvn=kvn, qh=qh, kh=kh, vh=vh, o=o), got


def _mla_backward(dy, sv, w, tabs, ride=()):
    cos, sin = tabs
    r2 = MLA_HEADS * (QK_ROPE // 2)
    g = {}
    g['w_o'] = wgrad("mla_out_dw", sv['o'], dy)
    do = mm3("mla_out_dx", dy, w['w_o'], tb=True)
    dqh, dkh, dvh, got = attn_bwd(sv['qh'], sv['kh'], sv['vh'], do, ride)
    dq, dkv, dkrt = mla_heads_bwd(dqh, dkh, dvh, cos, sin)
    g['w_qb'] = wgrad("mla_q_dw", sv['qn'], dq)
    g['w_kvb'] = wgrad("mla_kv_dw", sv['kvn'], dkv)
    dqn = mm3("mla_q_dx", dq, w['w_qb'], tb=True)
    dkvn = mm3("mla_kv_dx", dkv, w['w_kvb'], tb=True)
    proj = sv['proj']
    (dproj,), _, (g['q_norm'], g['kv_norm']) = rowwise_bwd(
        "mla_mid_bwd", f_mla_mid,
        [_view(proj, 0, Q_LORA), _view(proj, Q_LORA, KV_LORA), _view(proj, Q_LORA + KV_LORA, LANES), _full(cos), _full(sin)],
        [], [w['q_norm'], w['kv_norm']], [(dqn, [Q_LORA]), (dkvn, [KV_LORA]), (dkrt, [r2, r2])],
        [(BF16, [0, 1, 2])], ts=ROW_TILE, n_diff=3)
    g['w_in'] = wgrad("mla_proj_dw", sv['h'], dproj)
    dh = mm3("mla_proj_dx", dproj, w['w_in'], tb=True)
    return dh, g, got


def _hgrn_forward(h, w, ride=()):
    proj = mm3("hgrn_proj", h, w['w_in'])
    z, states, got = hgrn_fwd(proj, w['lb'], w['g_norm'], ride)
    y = mm3("hgrn_out", z, w['w_o'])
    return y, dict(h=h, proj=proj, states=states, z=z), got


def _hgrn_backward(dy, sv, w, ride=()):
    g = {}
    g['w_o'] = wgrad("hgrn_out_dw", sv['z'], dy)
    dz = mm3("hgrn_out_dx", dy, w['w_o'], tb=True)
    dparts, g['lb'], g['g_norm'], got = hgrn_bwd(sv['proj'], sv['states'], dz, w['lb'], w['g_norm'], ride)
    g['w_in'] = tuple(wgrad("hgrn_proj_dw", sv['h'], p) for p in dparts)
    dh = mm3("hgrn_proj_dx", dparts, w['w_in'], tb=True)
    return dh, g, got


def _ffn_forward(h, w, ride=()):
    (ug, uu, a), got = ffn_in_act(h, w['w_in'], ride)
    y = mm3("ffn_out", a, w['w_out'])
    return y, dict(h=h, ug=ug, uu=uu, a=a), got


def _ffn_backward(dy, sv, w):
    g = {}
    g['w_out'] = wgrad("ffn_out_dw", sv['a'], dy)
    dug, duu = ffn_out_dx_act(dy, w['w_out'], sv['ug'], sv['uu'])
    g['w_in'] = (wgrad("ffn_in_dw", sv['h'], dug), wgrad("ffn_in_dw", sv['h'], duu))
    dh = mm3("ffn_in_dx", [dug, duu], w['w_in'], tb=True, tk_cap=1408)
    return dh, g


def kernel(x, c, positions, mla_w_in, mla_q_norm, mla_w_qb, mla_kv_norm, mla_w_kvb, mla_w_o, hgrn_lb, hgrn_w_in, hgrn_g_norm, hgrn_w_o, ffn_w_in, ffn_w_out, ada_w, ada_b, ln_g, ln_b, loss_target, m_mla_w_in, m_mla_q_norm, m_mla_w_qb, m_mla_kv_norm, m_mla_w_kvb, m_mla_w_o, m_hgrn_lb, m_hgrn_w_in, m_hgrn_g_norm, m_hgrn_w_o, m_ffn_w_in, m_ffn_w_out, m_ada_w, m_ada_b, m_ln_g, m_ln_b, v_mla_w_in, v_mla_q_norm, v_mla_w_qb, v_mla_kv_norm, v_mla_w_kvb, v_mla_w_o, v_hgrn_lb, v_hgrn_w_in, v_hgrn_g_norm, v_hgrn_w_o, v_ffn_w_in, v_ffn_w_out, v_ada_w, v_ada_b, v_ln_g, v_ln_b):
    W = dict(zip(WEIGHTS, (mla_w_in, mla_q_norm, mla_w_qb, mla_kv_norm, mla_w_kvb, mla_w_o, hgrn_lb, hgrn_w_in, hgrn_g_norm,
                           hgrn_w_o, ffn_w_in, ffn_w_out, ada_w, ada_b, ln_g, ln_b)))
    M1 = dict(zip(WEIGHTS, (m_mla_w_in, m_mla_q_norm, m_mla_w_qb, m_mla_kv_norm, m_mla_w_kvb, m_mla_w_o, m_hgrn_lb, m_hgrn_w_in,
                            m_hgrn_g_norm, m_hgrn_w_o, m_ffn_w_in, m_ffn_w_out, m_ada_w, m_ada_b, m_ln_g, m_ln_b)))
    M2 = dict(zip(WEIGHTS, (v_mla_w_in, v_mla_q_norm, v_mla_w_qb, v_mla_kv_norm, v_mla_w_kvb, v_mla_w_o, v_hgrn_lb, v_hgrn_w_in,
                            v_hgrn_g_norm, v_hgrn_w_o, v_ffn_w_in, v_ffn_w_out, v_ada_w, v_ada_b, v_ln_g, v_ln_b)))
    bsz, seq, d = x.shape
    depth, n_mla, n_hgrn = ffn_w_in.shape[0], mla_w_in.shape[0], hgrn_w_in.shape[0]
    n_sub = 2 * depth

    big = COL_SHARDED + ROW_SHARDED
    wb = {n: cast_bf16("cast_" + n, W[n]) for n in big}

    def mixer_names(layer):
        mixer = ['mla_w_in', 'mla_w_qb', 'mla_w_kvb', 'mla_w_o'] if layer % 2 == 0 else ['hgrn_w_in', 'hgrn_w_o']
        return [(n, layer // 2) for n in mixer]

    def carried(layer):
        return [('ffn_w_in', layer), ('ffn_w_out', layer)] + (mixer_names(layer + 1) if layer + 1 < depth else [])

    def carried_fwd(k):
        layer = k // 2
        if k % 2 == 0:
            return [('ffn_w_in', layer), ('ffn_w_out', layer)]
        return mixer_names(layer + 1) if layer + 1 < depth else []

    def weight_items(names):
        return [(wb[n][j], "gather") for n, j in names]

    G = {}
    internal = {'mla_w_in': _w_in_internal, 'mla_w_qb': _qb_internal, 'mla_w_kvb': _kvb_internal}

    def take_weights(names, got):
        for (n, j), a in zip(names, got):
            G[n, j] = internal.get(n, lambda w: w)(_from_gather(n, a))

    lower_shard = lower_bounds(hgrn_lb)
    got = exchange("gather_first", [(lower_shard, "gather"), (ln_g, "gather"), (ln_b, "gather"), (c, "gather")]
                   + weight_items(mixer_names(0)))
    lower_all = got[0].transpose(1, 0, 2).reshape(n_hgrn, -1)
    ln_g_all = got[1].transpose(1, 2, 0, 3).reshape(depth, 2, d)
    ln_b_all = got[2].transpose(1, 2, 0, 3).reshape(depth, 2, d)
    c_all = got[3].reshape(N_DEV * bsz, d)
    take_weights(mixer_names(0), got[4:])

    cols = ada_w.shape[-1]
    mod_loc = ada_fwd(c_all, ada_w, ada_b)
    (mod_got,) = exchange("scatter_mod", [(mod_loc.reshape(n_sub, N_DEV, bsz, cols).transpose(1, 0, 2, 3), "a2a")])
    mod = mod_got.transpose(1, 2, 0, 3).reshape(n_sub, bsz, 1, 3 * d)
    shift = [mod[k, :, :, 0:d] for k in range(n_sub)]
    scale = [mod[k, :, :, d:2 * d] for k in range(n_sub)]
    gate = [mod[k, :, :, 2 * d:] for k in range(n_sub)]
    lng = [ln_g_all[k // 2, k % 2][None, :] for k in range(n_sub)]
    lnb = [ln_b_all[k // 2, k % 2][None, :] for k in range(n_sub)]

    tabs = rope_tables(positions)

    def sub_weights(k):
        layer, j = k // 2, k // 4
        if k % 2:
            return 'ffn', layer, dict(w_in=G['ffn_w_in', layer], w_out=G['ffn_w_out', layer])
        if layer % 2 == 0:
            return 'mla', j, dict(w_in=G['mla_w_in', j], q_norm=mla_q_norm[j][None, :], w_qb=G['mla_w_qb', j],
                                  kv_norm=mla_kv_norm[j][None, :], w_kvb=G['mla_w_kvb', j], w_o=G['mla_w_o', j])
        return 'hgrn', j, dict(w_in=G['hgrn_w_in', j], lb=lower_all[j][None, :], g_norm=hgrn_g_norm[j][None, :],
                               w_o=G['hgrn_w_o', j])

    (h,) = rowwise("mod_first", lambda rv, ev, gv: (f_mod(rv, ev, gv), []), [_full(x)], [scale[0], shift[0]], [],
                   [(BF16, [d])], ts=ROW_TILE)
    xs, ys, saved = [x], [], []
    loss_acc = None
    for k in range(n_sub):
        kind, _, w = sub_weights(k)
        ride = weight_items(carried_fwd(k))
        if kind == 'ffn':
            y, sv, got = _ffn_forward(h, w, ride)
        elif kind == 'mla':
            y, sv, got = _mla_forward(h, w, tabs, ride)
        else:
            y, sv, got = _hgrn_forward(h, w, ride)
        take_weights(carried_fwd(k), got)
        ys.append(y)
        saved.append(sv)
        if k + 1 < n_sub:
            xn, h = rowwise("ln_mod", lambda rv, ev, gv: (f_ln_mod(rv, ev, gv), []), [_full(xs[k]), _full(y)],
                            [gate[k], scale[k + 1], shift[k + 1]], [lng[k], lnb[k]], [(F32, [d]), (BF16, [d])], ts=ROW_TILE)
            xs.append(xn)
        else:
            def loss_rows(rv, ev, gv):
                (row,) = f_ln_loss(rv, ev, gv)
                return [], [jnp.broadcast_to(jnp.sum(row, keepdims=True), (1, LANES))]
            (loss_acc,) = rowwise("ln_loss", loss_rows, [_full(xs[k]), _full(y), _full(loss_target)], [gate[k]], [lng[k], lnb[k]],
                                  [], ts=ROW_TILE, accs=[LANES])
    loss = lax.psum(loss_acc[0, 0], ("x", "y", "c"))

    d_shift, d_scale, d_gate = [None] * n_sub, [None] * n_sub, [None] * n_sub
    d_lng, d_lnb = [None] * n_sub, [None] * n_sub
    part = {n: [None] * W[n].shape[0] for n in ['mla_q_norm', 'mla_kv_norm', 'hgrn_g_norm']}
    recv = {n: [None] * W[n].shape[0] for n in big}
    d_lower = [None] * n_hgrn
    k = n_sub - 1
    (dx, dy), (d_gate[k],), (d_lng[k], d_lnb[k]) = rowwise_bwd(
        "ln_loss_bwd", f_ln_loss, [_full(xs[k]), _full(ys[k]), _full(loss_target)], [gate[k]], [lng[k], lnb[k]], [],
        [(F32, [0]), (BF16, [1])], ts=ROW_TILE, n_diff=2, unit_ct=1)
    grad_x = None
    mine = {}

    def take_grads(names, got):
        for (n, j), a in zip(names, got):
            recv[n][j] = a

    def grad_items(names):
        return [(_to_slabs(n, mine[n, jj]), "a2a") for n, jj in names]

    for k in range(n_sub - 1, -1, -1):
        kind, j, w = sub_weights(k)
        ride = grad_items(carried(k // 2)) if kind != 'ffn' else []
        if kind == 'ffn':
            dh, g = _ffn_backward(dy, saved[k], w)
            new = {('ffn_w_in', j): g['w_in'], ('ffn_w_out', j): g['w_out']}
        elif kind == 'mla':
            dh, g, got = _mla_backward(dy, saved[k], w, tabs, ride)
            new = {('mla_w_in', j): g['w_in'][:, :mla_w_in.shape[-1] * N_DEV], ('mla_w_qb', j): _qb_internal(g['w_qb'], inverse=True),
                   ('mla_w_kvb', j): _kvb_internal(g['w_kvb'], inverse=True), ('mla_w_o', j): g['w_o']}
            part['mla_q_norm'][j], part['mla_kv_norm'][j] = g['q_norm'][0], g['kv_norm'][0]
        else:
            dh, g, got = _hgrn_backward(dy, saved[k], w, ride)
            new = {('hgrn_w_in', j): g['w_in'], ('hgrn_w_o', j): g['w_o']}
            part['hgrn_g_norm'][j] = g['g_norm'][0]
            d_lower[j] = g['lb'][0]
        if kind != 'ffn':
            take_grads(carried(k // 2), got)
        mine.update(new)
        if k:
            (dx, dy), (d_gate[k - 1], d_scale[k], d_shift[k]), (d_lng[k - 1], d_lnb[k - 1]) = rowwise_bwd(
                "ln_mod_bwd", f_ln_mod, [_full(xs[k - 1]), _full(ys[k - 1])], [gate[k - 1], scale[k], shift[k]],
                [lng[k - 1], lnb[k - 1]], [(dx, [d]), (dh, [d])], [(F32, [0]), (BF16, [1])], ts=ROW_TILE, n_diff=2)
        else:
            (grad_x,), (d_scale[0], d_shift[0]), _ = rowwise_bwd(
                "mod_first_bwd", f_mod_with_x, [_full(x)], [scale[0], shift[0]], [], [(dh, [d]), (dx, [d])],
                [(F32, [0])], ts=ROW_TILE, n_diff=1)

    waiting = mixer_names(0)
    slabs = grad_items(waiting)
    slabs.append((jnp.stack(d_lower).reshape(n_hgrn, N_DEV, -1).transpose(1, 0, 2), "a2a"))
    for parts in (d_lng, d_lnb):
        full = jnp.stack([p[0] for p in parts]).reshape(depth, 2, N_DEV, d // N_DEV)
        slabs.append((full.transpose(2, 0, 1, 3), "a2a"))
    dmod = jnp.concatenate([jnp.stack(d_shift), jnp.stack(d_scale), jnp.stack(d_gate)], axis=-1)
    slabs.append((dmod.reshape(n_sub, bsz, N_DEV, cols).transpose(2, 0, 1, 3), "a2a"))
    small = ['mla_q_norm', 'mla_kv_norm', 'hgrn_g_norm']
    slabs += [(jnp.stack(part[n]), "gather") for n in small]
    got = exchange("scatter_last", slabs)
    take_grads(waiting, got)
    stacks = dict(zip(['hgrn_lb', 'ln_g', 'ln_b', 'dmod'] + small, got[len(waiting):]))

    dmod_all = stacks['dmod'].transpose(1, 0, 2, 3).reshape(n_sub, N_DEV * bsz, cols)
    g_ada_w, g_ada_b = ada_bwd(c_all, dmod_all)
    stacks['ada_w'] = g_ada_w.reshape((1,) + ada_w.shape)
    stacks['ada_b'] = g_ada_b.reshape((1,) + ada_b.shape)

    res = {}
    for n in WEIGHTS:
        if n == 'hgrn_lb':
            res[n] = adam_lb(stacks[n], W[n], M1[n], M2[n])
        elif n in big:
            res[n] = adam_layers("adam_" + n, recv[n], W[n], M1[n], M2[n])
        else:
            res[n] = adam("adam_" + n, stacks[n], W[n], M1[n], M2[n])
    return (loss, grad_x, *[res[n][0] for n in WEIGHTS], *[res[n][1] for n in WEIGHTS], *[res[n][2] for n in WEIGHTS],
            *[res[n][3] for n in WEIGHTS])
```
